```python
import jax
import jax.numpy as jnp
from jax import lax
import numpy as np

D_MODEL = 1024
BATCH = 32
SEQ = 256
DEPTH = 2
DEC_BATCH = 4
DEC_SEQ = 4096
PAST_LEN = 512

GRID_W = 64
D_MIX = D_MODEL
CONV_CH = D_MIX // 4
CONV_K = 31
NA_HEADS = 8
NA_DIM = 64
NA_WIDTH = NA_HEADS * NA_DIM
NA_KH = 8
NA_KW = 16
RET_HEADS = 4
RET_DIM = 64
RET_WIDTH = RET_HEADS * RET_DIM
RET_CHUNK = 128
ROPE_BASE = 10000.0
N_GROUPS = 4
EXPERTS_PER_GROUP = 8
N_EXPERTS = N_GROUPS * EXPERTS_PER_GROUP
TOP_K = 2
D_EXPERT = 512
MOE_BLOCK = 128
Q_BLOCK = 128
IN_COLS = 2 * CONV_CH + 3 * NA_WIDTH + 4 * RET_WIDTH
EPS = 1e-6
NEG_INF = -1e30

kernel_name = 'hybrid_diffusion_prefix_step'


def rms_norm(x, g):
    xf = x.astype(jnp.float32)
    y = xf * lax.rsqrt(jnp.mean(xf * xf, axis=-1, keepdims=True) + EPS)
    return (y * g.astype(jnp.float32)).astype(x.dtype)


def to_heads(t, n_heads, head_dim):
    b, l, _ = t.shape
    return t.reshape(b, l, n_heads, head_dim).transpose(0, 2, 1, 3)


def from_heads(t):
    b, h, l, d = t.shape
    return t.transpose(0, 2, 1, 3).reshape(b, l, h * d)


def split_proj(z):
    sizes = (CONV_CH, CONV_CH, NA_WIDTH, NA_WIDTH, NA_WIDTH, RET_WIDTH, RET_WIDTH, RET_WIDTH, RET_WIDTH)
    offs = np.cumsum(sizes)[:-1].tolist()
    return jnp.split(z, offs, axis=-1)


def conv_module(a, b, w_dw, b_dw, ln_g, ln_b):
    u = a * jax.nn.sigmoid(b)
    u = lax.conv_general_dilated(u, w_dw[:, None, :].astype(u.dtype), window_strides=(1,),
                                 padding=[(CONV_K // 2, CONV_K // 2)],
                                 dimension_numbers=('NWC', 'WIO', 'NWC'),
                                 feature_group_count=CONV_CH)
    uf = u.astype(jnp.float32) + b_dw.astype(jnp.float32)
    mu = jnp.mean(uf, axis=-1, keepdims=True)
    var = jnp.mean(jnp.square(uf - mu), axis=-1, keepdims=True)
    n = (uf - mu) * lax.rsqrt(var + EPS) * ln_g.astype(jnp.float32) + ln_b.astype(jnp.float32)
    return jax.nn.silu(n).astype(a.dtype)


def rope_half(x, ang):
    half = x.shape[-1] // 2
    cos = jnp.cos(ang).astype(x.dtype)
    sin = jnp.sin(ang).astype(x.dtype)
    x1, x2 = x[..., :half], x[..., half:]
    return jnp.concatenate([x1 * cos - x2 * sin, x1 * sin + x2 * cos], axis=-1)


def axial_rope(x):
    n, d = x.shape[-2], x.shape[-1]
    n_freq = d // 4
    t = jnp.arange(n)
    inv = ROPE_BASE ** (-jnp.arange(n_freq, dtype=jnp.float32) / n_freq)
    ang_r = (t // GRID_W).astype(jnp.float32)[:, None] * inv[None, :]
    ang_c = (t % GRID_W).astype(jnp.float32)[:, None] * inv[None, :]
    return jnp.concatenate([rope_half(x[..., :d // 2], ang_r), rope_half(x[..., d // 2:], ang_c)], axis=-1)


def softmax_attention(q, k, v):
    b, h, l, d = q.shape
    nq = l // Q_BLOCK
    qb = q.reshape(b, h, nq, Q_BLOCK, d).transpose(2, 0, 1, 3, 4)
    scale = d ** -0.5

    def attend(qi):
        s = jnp.einsum('bhqd,bhkd->bhqk', qi, k).astype(jnp.float32) * scale
        p = jax.nn.softmax(s, axis=-1).astype(v.dtype)
        return jnp.einsum('bhqk,bhkd->bhqd', p, v)

    o = lax.map(attend, qb)
    return o.transpose(1, 2, 0, 3, 4).reshape(b, h, l, d)


def neighbourhood_attention(q, k, v, k_ctx, v_ctx, rpb):
    b, h, n, d = q.shape
    rows_n = n // GRID_W
    kh = min(NA_KH, rows_n)
    scale = d ** -0.5
    qg = q.reshape(b, h, rows_n, GRID_W, d)
    kg = k.reshape(b, h, rows_n, GRID_W, d)
    vg = v.reshape(b, h, rows_n, GRID_W, d)
    rows = jnp.arange(rows_n)
    cols = jnp.arange(GRID_W)
    rs = jnp.clip(rows - kh // 2, 0, rows_n - kh)
    row_idx = rs[:, None] + jnp.arange(kh)[None, :]
    kb = jnp.take(kg, row_idx, axis=2)
    vb = jnp.take(vg, row_idx, axis=2)
    cs = jnp.clip(cols - NA_KW // 2, 0, GRID_W - NA_KW)
    in_win = (cols[None, :] >= cs[:, None]) & (cols[None, :] < cs[:, None] + NA_KW)
    dr = row_idx - rows[:, None] + NA_KH - 1
    dc = jnp.clip(cols[None, :] - cols[:, None], -(NA_KW - 1), NA_KW - 1) + NA_KW - 1
    bias = rpb[:, dr[:, None, :, None], dc[None, :, None, :]].astype(jnp.float32)
    s_loc = jnp.einsum('bhrcd,bhrjwd->bhrcjw', qg, kb).astype(jnp.float32) * scale + bias
    s_loc = jnp.where(in_win[:, None, :], s_loc, NEG_INF)
    s_ctx = jnp.einsum('bhrcd,bhmd->bhrcm', qg, k_ctx).astype(jnp.float32) * scale
    m = jnp.maximum(jnp.max(s_loc, axis=(-2, -1)), jnp.max(s_ctx, axis=-1))
    p_loc = jnp.exp(s_loc - m[..., None, None])
    p_ctx = jnp.exp(s_ctx - m[..., None])
    den = jnp.sum(p_loc, axis=(-2, -1)) + jnp.sum(p_ctx, axis=-1)
    o = (jnp.einsum('bhrcjw,bhrjwd->bhrcd', p_loc.astype(v.dtype), vb)
         + jnp.einsum('bhrcm,bhmd->bhrcd', p_ctx.astype(v.dtype), v_ctx))
    o = (o.astype(jnp.float32) / den[..., None]).astype(v.dtype)
    return o.reshape(b, h, n, d)


def retention_scan(q, k, v, log_g, s0):
    b, h, l, d = q.shape
    nc = l // RET_CHUNK
    lg = log_g.astype(jnp.float32)
    pos = jnp.arange(RET_CHUNK, dtype=jnp.float32)
    qc = q.reshape(b, h, nc, RET_CHUNK, d).astype(jnp.float32)
    kc = k.reshape(b, h, nc, RET_CHUNK, d).astype(jnp.float32)
    vc = v.reshape(b, h, nc, RET_CHUNK, d).astype(jnp.float32)
    diff = pos[:, None] - pos[None, :]
    decay = jnp.where(diff >= 0, jnp.exp(jnp.maximum(diff, 0.0)[None] * lg[:, None, None]), 0.0)
    s = jnp.einsum('bhnid,bhnjd->bhnij', qc, kc) * decay[None, :, None]
    intra = jnp.einsum('bhnij,bhnjd->bhnid', s, vc)
    k_dec = jnp.exp((RET_CHUNK - 1 - pos)[None, :] * lg[:, None])
    chunk_kv = jnp.einsum('bhnjd,bhnje->nbhde', kc * k_dec[None, :, None, :, None], vc)
    chunk_dec = jnp.exp(RET_CHUNK * lg)[None, :, None, None]

    def step(state, kv):
        return chunk_dec * state + kv, state

    s_fin, s_before = lax.scan(step, s0.astype(jnp.float32), chunk_kv)
    q_dec = jnp.exp((pos + 1.0)[None, :] * lg[:, None])
    cross = jnp.einsum('bhnid,nbhde->bhnie', qc * q_dec[None, :, None, :, None], s_before)
    return (intra + cross).reshape(b, h, l, d), s_fin


def bidir_retention(q, k, v, lg_f, lg_b, s0_f, s0_b):
    o_f, s_f = retention_scan(q, k, v, lg_f, s0_f)
    rev = lambda t: jnp.flip(t, axis=2)
    o_b, s_b = retention_scan(rev(q), rev(k), rev(v), lg_b, s0_b)
    return o_f + rev(o_b), s_f, s_b


def retention_out(o, gate, gn_g):
    mu = jnp.mean(o, axis=-1, keepdims=True)
    var = jnp.mean(jnp.square(o - mu), axis=-1, keepdims=True)
    n = from_heads((o - mu) * lax.rsqrt(var + EPS))
    y = n * gn_g.astype(jnp.float32) * jax.nn.silu(gate.astype(jnp.float32))
    return y.astype(gate.dtype)


def hier_moe(h, w_rg, b_rg, w_re, b_re, w1, w3, w2):
    t = h.shape[0]
    ar = jnp.arange(t)
    lg = (h @ w_rg).astype(jnp.float32) + b_rg.astype(jnp.float32)
    grp = jnp.argmax(lg, axis=-1)
    p_grp = jax.nn.softmax(lg, axis=-1)[ar, grp][:, None]
    le = ((h @ w_re).astype(jnp.float32) + b_re.astype(jnp.float32)).reshape(t, N_GROUPS, EXPERTS_PER_GROUP)
    le = le[ar, grp]
    top_l, top_i = lax.top_k(le, TOP_K)
    gate = (p_grp * jax.nn.softmax(top_l, axis=-1)).reshape(-1)
    eid = (grp[:, None] * EXPERTS_PER_GROUP + top_i).reshape(-1).astype(jnp.int32)
    tok = jnp.repeat(ar, TOP_K).astype(jnp.int32)
    a = t * TOP_K
    order = jnp.argsort(eid)
    se, st, sw = eid[order], tok[order], gate[order]
    counts = jnp.bincount(eid, length=N_EXPERTS)
    starts = jnp.cumsum(counts) - counts
    padded = (counts + MOE_BLOCK - 1) // MOE_BLOCK * MOE_BLOCK
    pend = jnp.cumsum(padded)
    dest = (pend - padded)[se] + jnp.arange(a) - starts[se]
    n_rows = -(-a // MOE_BLOCK) * MOE_BLOCK + N_EXPERTS * MOE_BLOCK
    n_blk = n_rows // MOE_BLOCK
    row_tok = jnp.zeros((n_rows,), jnp.int32).at[dest].set(st)
    row_w = jnp.zeros((n_rows,), h.dtype).at[dest].set(sw.astype(h.dtype))
    blk_e = jnp.minimum(jnp.searchsorted(pend, jnp.arange(n_blk) * MOE_BLOCK, side='right'), N_EXPERTS - 1)

    def expert_block(args):
        tk, e = args
        xb = h[tk]
        return (jax.nn.silu(xb @ w1[e]) * (xb @ w3[e])) @ w2[e]

    out = lax.map(expert_block, (row_tok.reshape(n_blk, MOE_BLOCK), blk_e))
    return jnp.zeros_like(h).at[row_tok].add(out.reshape(n_rows, -1) * row_w[:, None])


def mixer_context(h, w_in, w_out, conv_w, conv_b, conv_ln_g, conv_ln_b, ret_lg_f, ret_lg_b, ret_gn_g):
    b = h.shape[0]
    ca, cb, qn, kn, vn, qr, kr, vr, gr = split_proj(h @ w_in)
    y_conv = conv_module(ca, cb, conv_w, conv_b, conv_ln_g, conv_ln_b)
    kn = to_heads(kn, NA_HEADS, NA_DIM)
    vn = to_heads(vn, NA_HEADS, NA_DIM)
    y_na = from_heads(softmax_attention(to_heads(qn, NA_HEADS, NA_DIM), kn, vn))
    s0 = jnp.zeros((b, RET_HEADS, RET_DIM, RET_DIM), jnp.float32)
    o, s_f, s_b = bidir_retention(to_heads(qr, RET_HEADS, RET_DIM),
                                  to_heads(kr, RET_HEADS, RET_DIM) * RET_DIM ** -0.5,
                                  to_heads(vr, RET_HEADS, RET_DIM), ret_lg_f, ret_lg_b, s0, s0)
    y_ret = retention_out(o, gr, ret_gn_g)
    y = jnp.concatenate([y_conv, y_na, y_ret], axis=-1) @ w_out
    return y, kn, vn, s_f, s_b


def mixer_latent(h, k_ctx, v_ctx, s0_f, s0_b, w_in, w_out, conv_w, conv_b, conv_ln_g, conv_ln_b,
                 na_rpb, ret_lg_f, ret_lg_b, ret_gn_g):
    ca, cb, qn, kn, vn, qr, kr, vr, gr = split_proj(h @ w_in)
    y_conv = conv_module(ca, cb, conv_w, conv_b, conv_ln_g, conv_ln_b)
    y_na = from_heads(neighbourhood_attention(to_heads(qn, NA_HEADS, NA_DIM), to_heads(kn, NA_HEADS, NA_DIM),
                                              to_heads(vn, NA_HEADS, NA_DIM), k_ctx, v_ctx, na_rpb))
    q_r = axial_rope(to_heads(qr, RET_HEADS, RET_DIM))
    k_r = axial_rope(to_heads(kr, RET_HEADS, RET_DIM) * RET_DIM ** -0.5)
    o, _, _ = bidir_retention(q_r, k_r, to_heads(vr, RET_HEADS, RET_DIM), ret_lg_f, ret_lg_b, s0_f, s0_b)
    y_ret = retention_out(o, gr, ret_gn_g)
    return jnp.concatenate([y_conv, y_na, y_ret], axis=-1) @ w_out


def setup_inputs(seed: int = 0) -> dict:
    key = jax.random.key(seed)
    ks = jax.random.split(key, 32)
    f32 = jnp.float32

    def nrm(k, shape, s):
        return jax.random.normal(k, shape, f32) * s

    base_lg = jnp.log(1.0 - 2.0 ** (-5.0 - jnp.arange(RET_HEADS, dtype=f32)))
    return {
        'x_prompt': nrm(ks[0], (BATCH, SEQ, D_MODEL), 1.0),
        'x_sample': nrm(ks[1], (DEC_BATCH, DEC_SEQ, D_MODEL), 1.0),
        'c': nrm(ks[2], (DEC_BATCH, D_MODEL), 1.0),
        'cache_k': nrm(ks[3], (DEC_BATCH, DEPTH, NA_HEADS, PAST_LEN, NA_DIM), 1.0),
        'cache_v': nrm(ks[4], (DEC_BATCH, DEPTH, NA_HEADS, PAST_LEN, NA_DIM), 1.0),
        'state_ret_f': nrm(ks[5], (DEC_BATCH, DEPTH, RET_HEADS, RET_DIM, RET_DIM), 0.5),
        'state_ret_b': nrm(ks[6], (DEC_BATCH, DEPTH, RET_HEADS, RET_DIM, RET_DIM), 0.5),
        'c_ctx': nrm(ks[7], (D_MODEL,), 1.0),
        'w_ada': nrm(ks[8], (DEPTH, D_MODEL, 6 * D_MODEL), 0.5 * D_MODEL ** -0.5),
        'b_ada': nrm(ks[9], (DEPTH, 6 * D_MODEL), 0.02),
        'norm1_g': 1.0 + nrm(ks[10], (DEPTH, D_MODEL), 0.05),
        'norm2_g': 1.0 + nrm(ks[11], (DEPTH, D_MODEL), 0.05),
        'w_in': nrm(ks[12], (DEPTH, D_MODEL, IN_COLS), D_MODEL ** -0.5),
        'w_out': nrm(ks[13], (DEPTH, D_MIX, D_MODEL), D_MIX ** -0.5),
        'conv_w': nrm(ks[14], (DEPTH, CONV_K, CONV_CH), CONV_K ** -0.5),
        'conv_b': nrm(ks[15], (DEPTH, CONV_CH), 0.02),
        'conv_ln_g': 1.0 + nrm(ks[16], (DEPTH, CONV_CH), 0.05),
        'conv_ln_b': nrm(ks[17], (DEPTH, CONV_CH), 0.02),
        'na_rpb': nrm(ks[18], (DEPTH, NA_HEADS, 2 * NA_KH - 1, 2 * NA_KW - 1), 0.2),
        'ret_lg_f': base_lg[None, :] * jnp.exp(nrm(ks[19], (DEPTH, RET_HEADS), 0.1)),
        'ret_lg_b': base_lg[None, :] * jnp.exp(nrm(ks[20], (DEPTH, RET_HEADS), 0.1)),
        'ret_gn_g': 1.0 + nrm(ks[21], (DEPTH, RET_WIDTH), 0.05),
        'w_route_g': nrm(ks[22], (DEPTH, D_MODEL, N_GROUPS), D_MODEL ** -0.5),
        'b_route_g': nrm(ks[23], (DEPTH, N_GROUPS), 0.01),
        'w_route_e': nrm(ks[24], (DEPTH, D_MODEL, N_EXPERTS), D_MODEL ** -0.5),
        'b_route_e': nrm(ks[25], (DEPTH, N_EXPERTS), 0.01),
        'w1': nrm(ks[26], (DEPTH, N_EXPERTS, D_MODEL, D_EXPERT), D_MODEL ** -0.5),
        'w3': nrm(ks[27], (DEPTH, N_EXPERTS, D_MODEL, D_EXPERT), D_MODEL ** -0.5),
        'w2': nrm(ks[28], (DEPTH, N_EXPERTS, D_EXPERT, D_MODEL), D_EXPERT ** -0.5),
        'final_g': 1.0 + nrm(ks[29], (D_MODEL,), 0.05),
    }


def reference(x_prompt, x_sample, c, cache_k, cache_v, state_ret_f, state_ret_b, c_ctx, w_ada, b_ada,
              norm1_g, norm2_g, w_in, w_out, conv_w, conv_b, conv_ln_g, conv_ln_b, na_rpb, ret_lg_f,
              ret_lg_b, ret_gn_g, w_route_g, b_route_g, w_route_e, b_route_e, w1, w3, w2, final_g):
    xp, xs = x_prompt, x_sample
    k_list, v_list, sf_list, sb_list = [], [], [], []
    for l in range(DEPTH):
        conv_args = (conv_w[l], conv_b[l], conv_ln_g[l], conv_ln_b[l])
        moe_args = (w_route_g[l], b_route_g[l], w_route_e[l], b_route_e[l], w1[l], w3[l], w2[l])
        sh1, sc1, g1, sh2, sc2, g2 = jnp.split(jax.nn.silu(c_ctx) @ w_ada[l] + b_ada[l], 6, axis=-1)
        h = rms_norm(xp, norm1_g[l]) * (1 + sc1) + sh1
        y, k_l, v_l, sf_l, sb_l = mixer_context(h, w_in[l], w_out[l], *conv_args,
                                                ret_lg_f[l], ret_lg_b[l], ret_gn_g[l])
        xp = xp + g1 * y
        h = rms_norm(xp, norm2_g[l]) * (1 + sc2) + sh2
        xp = xp + g2 * hier_moe(h.reshape(-1, D_MODEL), *moe_args).reshape(xp.shape)
        k_list.append(k_l)
        v_list.append(v_l)
        sf_list.append(sf_l.astype(xp.dtype))
        sb_list.append(sb_l.astype(xp.dtype))
        sh1, sc1, g1, sh2, sc2, g2 = jnp.split((jax.nn.silu(c) @ w_ada[l] + b_ada[l])[:, None, :], 6, axis=-1)
        h = rms_norm(xs, norm1_g[l]) * (1 + sc1) + sh1
        y = mixer_latent(h, cache_k[:, l], cache_v[:, l], state_ret_f[:, l], state_ret_b[:, l],
                         w_in[l], w_out[l], *conv_args, na_rpb[l], ret_lg_f[l], ret_lg_b[l], ret_gn_g[l])
        xs = xs + g1 * y
        h = rms_norm(xs, norm2_g[l]) * (1 + sc2) + sh2
        xs = xs + g2 * hier_moe(h.reshape(-1, D_MODEL), *moe_args).reshape(xs.shape)
    y_prompt = rms_norm(xp, final_g)
    y_sample = rms_norm(xs, final_g)
    new_cache_k = jnp.stack(k_list, axis=1)
    new_cache_v = jnp.stack(v_list, axis=1)
    new_state_ret_f = jnp.stack(sf_list, axis=1)
    new_state_ret_b = jnp.stack(sb_list, axis=1)
    return (y_prompt, y_sample, new_cache_k, new_cache_v, new_state_ret_f, new_state_ret_b)
```

```python
import functools

import numpy as np
import jax
import jax.numpy as jnp
from jax import lax
from jax.experimental import pallas as pl
from jax.experimental.pallas import tpu as pltpu

D_MODEL = 1024
BATCH = 32
SEQ = 256
DEPTH = 2
DEC_BATCH = 4
DEC_SEQ = 4096
PAST_LEN = 512
GRID_W = 64
GRID_H = DEC_SEQ // GRID_W
CONV_CH = 256
CONV_K = 31
NA_HEADS = 8
NA_DIM = 64
NA_WIDTH = NA_HEADS * NA_DIM
NA_KH = 8
NA_KW = 16
RET_HEADS = 4
RET_DIM = 64
RET_WIDTH = RET_HEADS * RET_DIM
RET_CHUNK = 128
ROPE_BASE = 10000.0
N_GROUPS = 4
EXPERTS_PER_GROUP = 8
N_EXPERTS = N_GROUPS * EXPERTS_PER_GROUP
D_EXPERT = 512
IN_COLS = 2 * CONV_CH + 3 * NA_WIDTH + 4 * RET_WIDTH
EPS = 1e-6
NEG_INF = -1e30

F32 = jnp.float32
BF16 = jnp.bfloat16
HIGHEST = lax.Precision.HIGHEST

T_CTX = BATCH * SEQ
T_LAT = DEC_BATCH * DEC_SEQ
T_ALL = T_CTX + T_LAT
N_COND = 1 + DEC_BATCH
COND_ROWS = 8

TM = 512
NB_CTX = T_CTX // TM
NB_LAT = T_LAT // TM
NB_ALL = NB_CTX + NB_LAT
LAT_BLOCKS_PER_REQ = DEC_SEQ // TM

LANES = 128
SUBLANES = 8
ROUTE_COLS = LANES

COL_CONV = 0
COL_NA_Q = 2 * CONV_CH
COL_NA_K = COL_NA_Q + NA_WIDTH
COL_NA_V = COL_NA_K + NA_WIDTH
COL_RET = COL_NA_V + NA_WIDTH

NA_ROWS = 8
NA_Q = NA_ROWS * GRID_W
NA_KROWS = 2 * NA_ROWS
NA_KEYS = NA_KROWS * GRID_W
NA_RB = GRID_H // NA_ROWS

MOE_BLK = 256
MOE_NBLK = (2 * T_ALL) // MOE_BLK + N_EXPERTS
MOE_ROWS = MOE_NBLK * MOE_BLK

VMEM_LIMIT = 56 * 1024 * 1024


def _cparams(sem):
    return pltpu.CompilerParams(dimension_semantics=sem, vmem_limit_bytes=VMEM_LIMIT)


def _sigmoid(x):
    return 1.0 / (1.0 + jnp.exp(-x))


def _cond_row(i):
    return jnp.where(i < NB_CTX, 0, 1 + (i - NB_CTX) // LAT_BLOCKS_PER_REQ)


ADA_TN = 1536


def _ada_kernel(cv_ref, w_ref, b_ref, o_ref):
    cv = cv_ref[...]
    s = cv * _sigmoid(cv)
    o_ref[...] = jnp.dot(s, w_ref[...], precision=HIGHEST, preferred_element_type=F32) + b_ref[...]


def _ada(cv, w_ada, b_ada):
    n = 6 * D_MODEL
    return pl.pallas_call(
        _ada_kernel,
        grid=(DEPTH, n // ADA_TN),
        in_specs=[
            pl.BlockSpec((COND_ROWS, D_MODEL), lambda l, j: (0, 0)),
            pl.BlockSpec((None, D_MODEL, ADA_TN), lambda l, j: (l, 0, j)),
            pl.BlockSpec((None, 1, ADA_TN), lambda l, j: (l, 0, j)),
        ],
        out_specs=pl.BlockSpec((None, COND_ROWS, ADA_TN), lambda l, j: (l, 0, j)),
        out_shape=jax.ShapeDtypeStruct((DEPTH, COND_ROWS, n), F32),
        compiler_params=_cparams(("arbitrary", "arbitrary")),
        name="ada_mod",
    )(cv, w_ada, b_ada.reshape(DEPTH, 1, n))


IN_TN = 768


def _norm_mod(x, g, shift, scale):
    ms = jnp.mean(x * x, axis=-1, keepdims=True)
    return (x * lax.rsqrt(ms + EPS) * g) * (1.0 + scale) + shift


def _inproj_body(x, mod_ref, g_ref, w_ref, z_ref):
    h = _norm_mod(x, g_ref[...], mod_ref[0:1, :], mod_ref[1:2, :]).astype(BF16)
    for c in range(IN_COLS // IN_TN):
        cols = slice(c * IN_TN, (c + 1) * IN_TN)
        z_ref[:, cols] = jnp.dot(h, w_ref[:, cols], preferred_element_type=F32).astype(BF16)


def _inproj_first_kernel(xc_ref, xl_ref, mod_ref, g_ref, w_ref, z_ref, xo_ref):
    i = pl.program_id(0)
    x = jnp.where(i < NB_CTX, xc_ref[...], xl_ref[...])
    xo_ref[...] = x
    _inproj_body(x, mod_ref, g_ref, w_ref, z_ref)


def _inproj_next_kernel(x_ref, y0_ref, y1_ref, modp_ref, mod_ref, g_ref, w_ref, z_ref, xo_ref):
    x = x_ref[...] + modp_ref[5:6, :] * (y0_ref[...] + y1_ref[...])
    xo_ref[...] = x
    _inproj_body(x, mod_ref, g_ref, w_ref, z_ref)


def _tok_spec(cols):
    return pl.BlockSpec((TM, cols), lambda i: (i, 0))


def _mod_spec():
    return pl.BlockSpec((None, 6, D_MODEL), lambda i: (_cond_row(i), 0, 0))


def _full_spec(shape):
    return pl.BlockSpec(shape, lambda i: (0,) * len(shape))


def _ctx_lat_specs(cols):
    return [pl.BlockSpec((TM, cols), lambda i: (jnp.minimum(i, NB_CTX - 1), 0)),
            pl.BlockSpec((TM, cols), lambda i: (jnp.maximum(i - NB_CTX, 0), 0))]


def _inproj_first(x_ctx, x_lat, mod, g, w_bf16):
    return pl.pallas_call(
        _inproj_first_kernel,
        grid=(NB_ALL,),
        in_specs=_ctx_lat_specs(D_MODEL) + [_mod_spec(), _full_spec((1, D_MODEL)),
                                            _full_spec((D_MODEL, IN_COLS))],
        out_specs=[_tok_spec(IN_COLS), _tok_spec(D_MODEL)],
        out_shape=[jax.ShapeDtypeStruct((T_ALL, IN_COLS), BF16),
                   jax.ShapeDtypeStruct((T_ALL, D_MODEL), F32)],
        compiler_params=_cparams(("arbitrary",)),
        name="inproj_first",
    )(x_ctx, x_lat, mod, g, w_bf16)


def _inproj_next(x, y, mod_prev, mod, g, w_bf16):
    return pl.pallas_call(
        _inproj_next_kernel,
        grid=(NB_ALL,),
        in_specs=[_tok_spec(D_MODEL),
                  pl.BlockSpec((None, TM, D_MODEL), lambda i: (0, i, 0)),
                  pl.BlockSpec((None, TM, D_MODEL), lambda i: (1, i, 0)),
                  _mod_spec(), _mod_spec(), _full_spec((1, D_MODEL)),
                  _full_spec((D_MODEL, IN_COLS))],
        out_specs=[_tok_spec(IN_COLS), _tok_spec(D_MODEL)],
        out_shape=[jax.ShapeDtypeStruct((T_ALL, IN_COLS), BF16),
                   jax.ShapeDtypeStruct((T_ALL, D_MODEL), F32)],
        compiler_params=_cparams(("arbitrary",)),
        name="inproj_next",
    )(x, y, y, mod_prev, mod, g, w_bf16)


CONV_PAD = 16
CONV_CHUNK = 64


def _conv_kernel(seq, z_ref, w_ref, b_ref, g_ref, be_ref, o_ref, upad_ref):
    zeros = jnp.zeros((CONV_PAD, CONV_CH), F32)
    upad_ref[0:CONV_PAD, :] = zeros
    upad_ref[seq + CONV_PAD:seq + 2 * CONV_PAD, :] = zeros

    def glu(ci, carry):
        base = pl.multiple_of(ci * 256, 256)
        zc = z_ref[pl.ds(base, 256), :].astype(F32)
        upad_ref[pl.ds(base + CONV_PAD, 256), :] = zc[:, :CONV_CH] * _sigmoid(zc[:, CONV_CH:])
        return carry

    lax.fori_loop(0, seq // 256, glu, 0)

    shift = CONV_PAD - CONV_K // 2

    def chunk(ci, carry):
        base = pl.multiple_of(ci * CONV_CHUNK, CONV_CHUNK)
        win = upad_ref[pl.ds(base, CONV_CHUNK + 2 * CONV_PAD), :]
        acc = jnp.zeros((CONV_CHUNK, CONV_CH), F32)
        for k in range(CONV_K):
            acc = acc + w_ref[k:k + 1, :] * win[k + shift:k + shift + CONV_CHUNK, :]
        acc = acc + b_ref[...]
        mu = jnp.mean(acc, axis=-1, keepdims=True)
        d = acc - mu
        var = jnp.mean(d * d, axis=-1, keepdims=True)
        n = d * lax.rsqrt(var + EPS) * g_ref[...] + be_ref[...]
        o_ref[pl.ds(base, CONV_CHUNK), :] = (n * _sigmoid(n)).astype(BF16)
        return carry

    lax.fori_loop(0, seq // CONV_CHUNK, chunk, 0)


def _conv(z, row_block0, nseq, seq, w, b, g, be):
    return pl.pallas_call(
        functools.partial(_conv_kernel, seq),
        grid=(nseq,),
        in_specs=[pl.BlockSpec((seq, 2 * CONV_CH), lambda s: (row_block0 + s, 0)),
                  _full_spec((CONV_K, CONV_CH)), _full_spec((1, CONV_CH)),
                  _full_spec((1, CONV_CH)), _full_spec((1, CONV_CH))],
        out_specs=pl.BlockSpec((seq, CONV_CH), lambda s: (s, 0)),
        out_shape=jax.ShapeDtypeStruct((nseq * seq, CONV_CH), BF16),
        scratch_shapes=[pltpu.VMEM((seq + 2 * CONV_PAD, CONV_CH), F32)],
        compiler_params=_cparams(("arbitrary",)),
        name="conv_seq%d" % seq,
    )(z, w, b, g, be)


def _dot_nt(a, b):
    return lax.dot_general(a, b, (((1,), (1,)), ((), ())), preferred_element_type=F32)


def _ctx_attn_kernel(q_ref, k_ref, v_ref, o_ref, ko_ref, vo_ref):
    scale = NA_DIM ** -0.5
    outs = []
    for h in range(NA_HEADS):
        cols = slice(h * NA_DIM, (h + 1) * NA_DIM)
        qh, kh, vh = q_ref[:, cols], k_ref[:, cols], v_ref[:, cols]
        ko_ref[h] = kh.astype(F32)
        vo_ref[h] = vh.astype(F32)
        s = _dot_nt(qh, kh) * scale
        m = jnp.max(s, axis=-1, keepdims=True)
        p = jnp.exp(s - m)
        den = jnp.sum(p, axis=-1, keepdims=True)
        o = jnp.dot(p.astype(BF16), vh, preferred_element_type=F32)
        outs.append(o / den)
    o_ref[...] = jnp.concatenate(outs, axis=-1).astype(BF16)


def _ctx_attn(z):
    qb, kb, vb = COL_NA_Q // NA_WIDTH, COL_NA_K // NA_WIDTH, COL_NA_V // NA_WIDTH
    head_shape = jax.ShapeDtypeStruct((BATCH, NA_HEADS, SEQ, NA_DIM), F32)
    head_spec = pl.BlockSpec((None, NA_HEADS, SEQ, NA_DIM), lambda b: (b, 0, 0, 0))
    return pl.pallas_call(
        _ctx_attn_kernel,
        grid=(BATCH,),
        in_specs=[pl.BlockSpec((SEQ, NA_WIDTH), lambda b: (b, qb)),
                  pl.BlockSpec((SEQ, NA_WIDTH), lambda b: (b, kb)),
                  pl.BlockSpec((SEQ, NA_WIDTH), lambda b: (b, vb))],
        out_specs=[pl.BlockSpec((SEQ, NA_WIDTH), lambda b: (b, 0)), head_spec, head_spec],
        out_shape=[jax.ShapeDtypeStruct((T_CTX, NA_WIDTH), BF16), head_shape, head_shape],
        compiler_params=_cparams(("arbitrary",)),
        name="ctx_attn",
    )(z, z, z)


def _na_bias_tables(rpb):
    tabs = []
    for r0 in (0, NA_ROWS, GRID_H - NA_ROWS):
        ks = int(np.clip(r0 - NA_KH // 2, 0, GRID_H - NA_KROWS))
        qi = np.arange(NA_Q)
        ki = np.arange(NA_KEYS)
        r = r0 + qi // GRID_W
        qc = qi % GRID_W
        kr = ks + ki // GRID_W
        kc = ki % GRID_W
        rs = np.clip(r - NA_KH // 2, 0, GRID_H - NA_KH)
        cs = np.clip(qc - NA_KW // 2, 0, GRID_W - NA_KW)
        ok = ((kr[None, :] >= rs[:, None]) & (kr[None, :] < rs[:, None] + NA_KH)
              & (kc[None, :] >= cs[:, None]) & (kc[None, :] < cs[:, None] + NA_KW))
        dr = np.clip(kr[None, :] - r[:, None] + NA_KH - 1, 0, 2 * NA_KH - 2)
        dc = np.clip(kc[None, :] - qc[:, None], -(NA_KW - 1), NA_KW - 1) + NA_KW - 1
        vals = rpb[:, :, dr, dc].astype(F32)
        tabs.append(jnp.where(ok[None, None], vals, NEG_INF))
    return jnp.stack(tabs, axis=2)


def _na_kernel(q_ref, k_ref, v_ref, kc_ref, vc_ref, bias_ref, o_ref):
    rb = pl.program_id(2)
    scale = NA_DIM ** -0.5
    ks = jnp.clip(rb * NA_ROWS - NA_KH // 2, 0, GRID_H - NA_KROWS)
    start = pl.multiple_of(ks * GRID_W, GRID_W)
    q = q_ref[...]
    kl = k_ref[pl.ds(start, NA_KEYS), :]
    vl = v_ref[pl.ds(start, NA_KEYS), :]
    outs = []
    for hh in range(2):
        cols = slice(hh * NA_DIM, (hh + 1) * NA_DIM)
        qh = q[:, cols]
        s_loc = _dot_nt(qh, kl[:, cols]) * scale + bias_ref[hh]
        s_ctx = _dot_nt(qh, kc_ref[hh].astype(BF16)) * scale
        m = jnp.maximum(jnp.max(s_loc, axis=-1, keepdims=True), jnp.max(s_ctx, axis=-1, keepdims=True))
        p_loc = jnp.exp(s_loc - m)
        p_ctx = jnp.exp(s_ctx - m)
        den = jnp.sum(p_loc, axis=-1, keepdims=True) + jnp.sum(p_ctx, axis=-1, keepdims=True)
        o = (jnp.dot(p_loc.astype(BF16), vl[:, cols], preferred_element_type=F32)
             + jnp.dot(p_ctx.astype(BF16), vc_ref[hh].astype(BF16), preferred_element_type=F32))
        outs.append(o / den)
    o_ref[...] = jnp.concatenate(outs, axis=-1).astype(BF16)


def _na_attn(z, cache_k, cache_v, bias, layer):
    lat_q0 = T_CTX // NA_Q
    lat_s0 = T_CTX // DEC_SEQ
    qc, kc, vc = COL_NA_Q // LANES, COL_NA_K // LANES, COL_NA_V // LANES

    def kind(rb):
        return jnp.where(rb == 0, 0, jnp.where(rb == NA_RB - 1, 2, 1))

    ctx_spec = pl.BlockSpec((None, None, 2, PAST_LEN, NA_DIM), lambda b, hp, rb: (b, layer, hp, 0, 0))
    return pl.pallas_call(
        _na_kernel,
        grid=(DEC_BATCH, NA_HEADS // 2, NA_RB),
        in_specs=[pl.BlockSpec((NA_Q, LANES), lambda b, hp, rb: (lat_q0 + b * NA_RB + rb, qc + hp)),
                  pl.BlockSpec((DEC_SEQ, LANES), lambda b, hp, rb: (lat_s0 + b, kc + hp)),
                  pl.BlockSpec((DEC_SEQ, LANES), lambda b, hp, rb: (lat_s0 + b, vc + hp)),
                  ctx_spec, ctx_spec,
                  pl.BlockSpec((None, 2, None, NA_Q, NA_KEYS), lambda b, hp, rb: (layer, hp, kind(rb), 0, 0))],
        out_specs=pl.BlockSpec((NA_Q, LANES), lambda b, hp, rb: (b * NA_RB + rb, hp)),
        out_shape=jax.ShapeDtypeStruct((T_LAT, NA_WIDTH), BF16),
        compiler_params=_cparams(("arbitrary", "arbitrary", "arbitrary")),
        name="nbr_attn",
    )(z, z, z, cache_k, cache_v, bias)


def _rope_tables():
    n_freq = RET_DIM // 4
    t = np.arange(DEC_SEQ)
    inv = jnp.asarray(ROPE_BASE, F32) ** (-jnp.arange(n_freq, dtype=F32) / n_freq)
    ang_r = jnp.asarray(t // GRID_W, F32)[:, None] * inv[None, :]
    ang_c = jnp.asarray(t % GRID_W, F32)[:, None] * inv[None, :]
    cos = jnp.concatenate([jnp.cos(ang_r)] * 2 + [jnp.cos(ang_c)] * 2, axis=-1)
    sin = jnp.concatenate([-jnp.sin(ang_r), jnp.sin(ang_r), -jnp.sin(ang_c), jnp.sin(ang_c)], axis=-1)
    return jnp.tile(cos, (1, RET_HEADS)), jnp.tile(sin, (1, RET_HEADS))


def _ret_kernel(seq, latent, *refs):
    if latent:
        (lg_ref, z_ref, gn_ref, cos_ref, sin_ref, s0f_ref, s0b_ref, y_ref,
         q_s, k_s, kv_s, st_s) = refs
    else:
        lg_ref, z_ref, gn_ref, y_ref, sf_ref, sb_ref, q_s, k_s, kv_s, st_s = refs
    nc = seq // RET_CHUNK
    ch, hd = RET_CHUNK, RET_DIM
    half = RET_DIM // 4

    row = lax.broadcasted_iota(jnp.int32, (ch, ch), 0).astype(F32)
    col = lax.broadcasted_iota(jnp.int32, (ch, ch), 1).astype(F32)
    pos = lax.broadcasted_iota(jnp.int32, (ch, hd), 0).astype(F32)
    decay, q_dec, k_dec, c_dec_f, c_dec_b = [], [], [], [], []
    for h in range(RET_HEADS):
        lf, lb = lg_ref[0, h], lg_ref[1, h]
        d_f = jnp.where(row >= col, jnp.exp(jnp.maximum(row - col, 0.0) * lf), 0.0)
        d_b = jnp.where(col >= row, jnp.exp(jnp.maximum(col - row, 0.0) * lb), 0.0)
        decay.append(d_f + d_b)
        q_dec.append(jnp.concatenate([jnp.exp((pos + 1.0) * lf), jnp.exp((ch - pos) * lb)], axis=-1))
        k_dec.append(jnp.concatenate([jnp.exp((ch - 1.0 - pos) * lf), jnp.exp(pos * lb)], axis=-1))
        c_dec_f.append(jnp.exp(jnp.zeros((hd, hd), F32) + ch * lf))
        c_dec_b.append(jnp.exp(jnp.zeros((hd, hd), F32) + ch * lb))

    if latent:
        lane = lax.broadcasted_iota(jnp.int32, (ch, RET_WIDTH), 1)
        first_half = (lane % (2 * half)) < half

    def rope(x, base):
        if not latent:
            return x
        swapped = jnp.where(first_half, pltpu.roll(x, RET_WIDTH - half, 1), pltpu.roll(x, half, 1))
        return x * cos_ref[pl.ds(base, ch), :] + swapped * sin_ref[pl.ds(base, ch), :]

    def pass1(n, carry):
        base = pl.multiple_of(n * ch, ch)
        zc = z_ref[pl.ds(base, ch), :]
        q = rope(zc[:, 0:RET_WIDTH].astype(F32), base)
        k = rope(zc[:, RET_WIDTH:2 * RET_WIDTH].astype(F32) * (RET_DIM ** -0.5), base)
        q_s[pl.ds(base, ch), :] = q.astype(BF16)
        k_s[pl.ds(base, ch), :] = k.astype(BF16)
        v = zc[:, 2 * RET_WIDTH:3 * RET_WIDTH]
        for h in range(RET_HEADS):
            cols = slice(h * hd, (h + 1) * hd)
            kh = k[:, cols]
            k2 = (jnp.concatenate([kh, kh], axis=-1) * k_dec[h]).astype(BF16)
            kv_s[n, h] = lax.dot_general(k2, v[:, cols], (((0,), (0,)), ((), ())),
                                         preferred_element_type=F32)
        return carry

    lax.fori_loop(0, nc, pass1, 0)

    for h in range(RET_HEADS):
        if latent:
            s_f, s_b = s0f_ref[h], s0b_ref[h]
        else:
            s_f = s_b = jnp.zeros((hd, hd), F32)

        def fwd(n, s, h=h):
            st_s[n, h, 0:hd, :] = s
            return c_dec_f[h] * s + kv_s[n, h, 0:hd, :]

        def bwd(i, s, h=h):
            n = nc - 1 - i
            st_s[n, h, hd:2 * hd, :] = s
            return c_dec_b[h] * s + kv_s[n, h, hd:2 * hd, :]

        s_f = lax.fori_loop(0, nc, fwd, s_f)
        s_b = lax.fori_loop(0, nc, bwd, s_b)
        if not latent:
            sf_ref[h] = s_f
            sb_ref[h] = s_b

    def pass3(n, carry):
        base = pl.multiple_of(n * ch, ch)
        zc = z_ref[pl.ds(base, ch), :]
        q = q_s[pl.ds(base, ch), :]
        k = k_s[pl.ds(base, ch), :]
        v = zc[:, 2 * RET_WIDTH:3 * RET_WIDTH]
        gate = zc[:, 3 * RET_WIDTH:4 * RET_WIDTH].astype(F32)
        outs = []
        for h in range(RET_HEADS):
            cols = slice(h * hd, (h + 1) * hd)
            qh = q[:, cols]
            s = _dot_nt(qh, k[:, cols]) * decay[h]
            o = jnp.dot(s.astype(BF16), v[:, cols], preferred_element_type=F32)
            qf = qh.astype(F32)
            q2 = (jnp.concatenate([qf, qf], axis=-1) * q_dec[h]).astype(BF16)
            o = o + jnp.dot(q2, st_s[n, h].astype(BF16), preferred_element_type=F32)
            mu = jnp.mean(o, axis=-1, keepdims=True)
            d = o - mu
            var = jnp.mean(d * d, axis=-1, keepdims=True)
            outs.append(d * lax.rsqrt(var + EPS))
        nrm = jnp.concatenate(outs, axis=-1)
        y_ref[pl.ds(base, ch), :] = (nrm * gn_ref[...] * (gate * _sigmoid(gate))).astype(BF16)
        return carry

    lax.fori_loop(0, nc, pass3, 0)


def _retention(z, lg, gn_g, latent, layer=None, rope=None, s0_f=None, s0_b=None):
    seq = DEC_SEQ if latent else SEQ
    nseq = DEC_BATCH if latent else BATCH
    nc = seq // RET_CHUNK
    row0 = (T_CTX // DEC_SEQ) if latent else 0
    cb = COL_RET // (4 * RET_WIDTH)
    in_specs = [pl.BlockSpec(memory_space=pltpu.SMEM),
                pl.BlockSpec((seq, 4 * RET_WIDTH), lambda s: (row0 + s, cb)),
                _full_spec((1, RET_WIDTH))]
    args = [lg, z, gn_g]
    state_shape = jax.ShapeDtypeStruct((nseq, RET_HEADS, RET_DIM, RET_DIM), F32)
    y_spec = pl.BlockSpec((seq, RET_WIDTH), lambda s: (s, 0))
    y_shape = jax.ShapeDtypeStruct((nseq * seq, RET_WIDTH), BF16)
    if latent:
        st_spec = pl.BlockSpec((None, None, RET_HEADS, RET_DIM, RET_DIM), lambda s: (s, layer, 0, 0, 0))
        in_specs += [_full_spec((seq, RET_WIDTH)), _full_spec((seq, RET_WIDTH)), st_spec, st_spec]
        args += [rope[0], rope[1], s0_f, s0_b]
        out_specs, out_shape = y_spec, y_shape
    else:
        so_spec = pl.BlockSpec((None, RET_HEADS, RET_DIM, RET_DIM), lambda s: (s, 0, 0, 0))
        out_specs, out_shape = [y_spec, so_spec, so_spec], [y_shape, state_shape, state_shape]
    return pl.pallas_call(
        functools.partial(_ret_kernel, seq, latent),
        grid=(nseq,),
        in_specs=in_specs,
        out_specs=out_specs,
        out_shape=out_shape,
        scratch_shapes=[pltpu.VMEM((seq, RET_WIDTH), BF16), pltpu.VMEM((seq, RET_WIDTH), BF16),
                        pltpu.VMEM((nc, RET_HEADS, 2 * RET_DIM, RET_DIM), F32),
                        pltpu.VMEM((nc, RET_HEADS, 2 * RET_DIM, RET_DIM), F32)],
        compiler_params=_cparams(("arbitrary",)),
        name="retention_lat" if latent else "retention_ctx",
    )(*args)


def _route(logits):
    lane = lax.broadcasted_iota(jnp.int32, logits.shape, 1)
    lane_f = lane.astype(F32)
    big = float(ROUTE_COLS)
    neg = -jnp.inf
    is_grp = lane < N_GROUPS
    gl = jnp.where(is_grp, logits, neg)
    gmax = jnp.max(gl, axis=-1, keepdims=True)
    grp = jnp.min(jnp.where(gl == gmax, lane_f, big), axis=-1, keepdims=True)
    p_grp = 1.0 / jnp.sum(jnp.exp(gl - gmax), axis=-1, keepdims=True)
    e_f = lane_f - N_GROUPS
    lo = grp * EXPERTS_PER_GROUP
    in_grp = (e_f >= lo) & (e_f < lo + EXPERTS_PER_GROUP)
    el = jnp.where(in_grp, logits, neg)
    m1 = jnp.max(el, axis=-1, keepdims=True)
    i1 = jnp.min(jnp.where(el == m1, lane_f, big), axis=-1, keepdims=True)
    el2 = jnp.where(lane_f == i1, neg, el)
    m2 = jnp.max(el2, axis=-1, keepdims=True)
    i2 = jnp.min(jnp.where(el2 == m2, lane_f, big), axis=-1, keepdims=True)
    t = jnp.exp(m2 - m1)
    g1 = p_grp / (1.0 + t)
    g2 = p_grp * t / (1.0 + t)
    return jnp.where(lane == 0, i1 - N_GROUPS,
                     jnp.where(lane == 1, i2 - N_GROUPS,
                               jnp.where(lane == 2, g1, jnp.where(lane == 3, g2, 0.0))))


def _outproj_kernel(ycc, ycl, ync, ynl, yrc, yrl, x_ref, mod_ref, g_ref, w_ref, wr_ref, br_ref,
                    xo_ref, h_ref, r_ref):
    is_ctx = pl.program_id(0) < NB_CTX
    yc = jnp.where(is_ctx, ycc[...], ycl[...])
    yn = jnp.where(is_ctx, ync[...], ynl[...])
    yr = jnp.where(is_ctx, yrc[...], yrl[...])
    y = (jnp.dot(yc, w_ref[0:CONV_CH, :], preferred_element_type=F32)
         + jnp.dot(yn, w_ref[CONV_CH:CONV_CH + NA_WIDTH, :], preferred_element_type=F32)
         + jnp.dot(yr, w_ref[CONV_CH + NA_WIDTH:, :], preferred_element_type=F32))
    x = x_ref[...] + mod_ref[2:3, :] * y
    xo_ref[...] = x
    h = _norm_mod(x, g_ref[...], mod_ref[3:4, :], mod_ref[4:5, :])
    h_ref[...] = h
    logits = jnp.dot(h, wr_ref[...], precision=HIGHEST, preferred_element_type=F32) + br_ref[...]
    r_ref[...] = _route(logits)


def _outproj(y_conv, y_na, y_ret, x, mod, g, w_bf16, w_route, b_route):
    return pl.pallas_call(
        _outproj_kernel,
        grid=(NB_ALL,),
        in_specs=(_ctx_lat_specs(CONV_CH) + _ctx_lat_specs(NA_WIDTH) + _ctx_lat_specs(RET_WIDTH)
                  + [_tok_spec(D_MODEL), _mod_spec(), _full_spec((1, D_MODEL)),
                     _full_spec((D_MODEL, D_MODEL)), _full_spec((D_MODEL, ROUTE_COLS)),
                     _full_spec((1, ROUTE_COLS))]),
        out_specs=[_tok_spec(D_MODEL), _tok_spec(D_MODEL), _tok_spec(ROUTE_COLS)],
        out_shape=[jax.ShapeDtypeStruct((T_ALL, D_MODEL), F32),
                   jax.ShapeDtypeStruct((T_ALL, D_MODEL), F32),
                   jax.ShapeDtypeStruct((T_ALL, ROUTE_COLS), F32)],
        compiler_params=_cparams(("arbitrary",)),
        name="outproj_route",
    )(y_conv[0], y_conv[1], y_na[0], y_na[1], y_ret[0], y_ret[1], x, mod, g, w_bf16, w_route, b_route)


def _dispatch(route):
    eid = route[:, 0:2].astype(jnp.int32).reshape(-1)
    gate = route[:, 2:4].reshape(-1)
    onehot = (eid[:, None] == jnp.arange(N_EXPERTS, dtype=jnp.int32)[None, :]).astype(jnp.int32)
    csum = jnp.cumsum(onehot, axis=0)
    rank = jnp.take_along_axis(csum, eid[:, None], axis=1)[:, 0] - 1
    counts = csum[-1]
    nblk = (counts + MOE_BLK - 1) // MOE_BLK
    blk_end = jnp.cumsum(nblk)
    blk_start = blk_end - nblk
    dest = blk_start[eid] * MOE_BLK + rank
    rows = jnp.arange(2 * T_ALL, dtype=jnp.int32)
    row_code = jnp.zeros((MOE_ROWS,), jnp.int32).at[dest].set(rows)
    row_w = jnp.zeros((MOE_ROWS,), F32).at[dest].set(gate)
    blk = jnp.arange(MOE_NBLK, dtype=jnp.int32)
    n_active = blk_end[-1]
    blk_c = jnp.minimum(blk, n_active - 1)
    blk_e = jnp.minimum(jnp.searchsorted(blk_end, blk_c, side='right'), N_EXPERTS - 1).astype(jnp.int32)
    left = counts[blk_e] - (blk - blk_start[blk_e]) * MOE_BLK
    blk_nv = jnp.where(blk < n_active, jnp.clip(left, 0, MOE_BLK), 0).astype(jnp.int32)
    return blk_e, blk_nv, row_code, row_w.reshape(MOE_ROWS, 1)


def _moe_kernel(blk_e, blk_nv, row_code, h_hbm, w1_ref, w3_ref, w2_ref, rw_ref, y_hbm,
                xbuf, obuf, w1b, w3b, w2b, gsem, ssem):
    i = pl.program_id(0)
    nv = blk_nv[i]
    base = i * MOE_BLK

    @pl.when(i == 0)
    def _():
        xbuf[...] = jnp.zeros_like(xbuf)

    @pl.when(nv > 0)
    def _():
        nv_up = pl.multiple_of((nv + SUBLANES - 1) // SUBLANES * SUBLANES, SUBLANES)
        nv_down = pl.multiple_of(nv // SUBLANES * SUBLANES, SUBLANES)

        def gather(r, carry):
            tok = row_code[base + r] >> 1
            pltpu.make_async_copy(h_hbm.at[pl.ds(tok, 1)], xbuf.at[pl.ds(r, 1)], gsem).start()
            return carry

        lax.fori_loop(0, nv_up, gather, 0)

        @pl.when((i == 0) | (blk_e[i] != blk_e[jnp.maximum(i - 1, 0)]))
        def _():
            w1b[...] = w1_ref[...].astype(BF16)
            w3b[...] = w3_ref[...].astype(BF16)
            w2b[...] = w2_ref[...].astype(BF16)

        pltpu.make_async_copy(h_hbm.at[pl.ds(0, nv_up)], xbuf.at[pl.ds(0, nv_up)], gsem).wait()
        xb = xbuf[...].astype(BF16)
        a = jnp.dot(xb, w1b[...], preferred_element_type=F32)
        b = jnp.dot(xb, w3b[...], preferred_element_type=F32)
        mid = (a * _sigmoid(a) * b).astype(BF16)
        obuf[...] = jnp.dot(mid, w2b[...], preferred_element_type=F32) * rw_ref[...]

        def scatter(r, carry):
            code = row_code[base + r]
            pltpu.make_async_copy(obuf.at[pl.ds(r, 1)], y_hbm.at[code & 1, pl.ds(code >> 1, 1)], ssem).start()
            return carry

        lax.fori_loop(0, nv, scatter, 0)

        @pl.when(nv_down > 0)
        def _():
            pltpu.make_async_copy(obuf.at[pl.ds(0, nv_down)], y_hbm.at[0, pl.ds(0, nv_down)], ssem).wait()

        for r in range(SUBLANES - 1):
            @pl.when(nv_down + r < nv)
            def _():
                pltpu.make_async_copy(obuf.at[pl.ds(0, 1)], y_hbm.at[0, pl.ds(0, 1)], ssem).wait()


def _moe(h, w1, w3, w2, layer, blk_e, blk_nv, row_code, row_w):
    def w_spec(rows, cols):
        return pl.BlockSpec((None, None, rows, cols), lambda i, be, bn, rc: (layer, be[i], 0, 0))

    grid_spec = pltpu.PrefetchScalarGridSpec(
        num_scalar_prefetch=3,
        grid=(MOE_NBLK,),
        in_specs=[pl.BlockSpec(memory_space=pl.ANY),
                  w_spec(D_MODEL, D_EXPERT), w_spec(D_MODEL, D_EXPERT), w_spec(D_EXPERT, D_MODEL),
                  pl.BlockSpec((MOE_BLK, 1), lambda i, be, bn, rc: (i, 0))],
        out_specs=pl.BlockSpec(memory_space=pl.ANY),
        scratch_shapes=[pltpu.VMEM((MOE_BLK, D_MODEL), F32), pltpu.VMEM((MOE_BLK, D_MODEL), F32),
                        pltpu.VMEM((D_MODEL, D_EXPERT), BF16), pltpu.VMEM((D_MODEL, D_EXPERT), BF16),
                        pltpu.VMEM((D_EXPERT, D_MODEL), BF16),
                        pltpu.SemaphoreType.DMA, pltpu.SemaphoreType.DMA])
    return pl.pallas_call(
        _moe_kernel,
        grid_spec=grid_spec,
        out_shape=jax.ShapeDtypeStruct((2, T_ALL, D_MODEL), F32),
        compiler_params=_cparams(("arbitrary",)),
        name="moe_experts",
    )(blk_e, blk_nv, row_code, h, w1, w3, w2, row_w)


def _final_kernel(x_ref, y0_ref, y1_ref, mod_ref, g_ref, o_ref):
    x = x_ref[...] + mod_ref[5:6, :] * (y0_ref[...] + y1_ref[...])
    ms = jnp.mean(x * x, axis=-1, keepdims=True)
    o_ref[...] = x * lax.rsqrt(ms + EPS) * g_ref[...]


def _final(x, y, mod, g, block0, nblocks):
    return pl.pallas_call(
        _final_kernel,
        grid=(nblocks,),
        in_specs=[pl.BlockSpec((TM, D_MODEL), lambda i: (block0 + i, 0)),
                  pl.BlockSpec((None, TM, D_MODEL), lambda i: (0, block0 + i, 0)),
                  pl.BlockSpec((None, TM, D_MODEL), lambda i: (1, block0 + i, 0)),
                  pl.BlockSpec((None, 6, D_MODEL), lambda i: (_cond_row(block0 + i), 0, 0)),
                  _full_spec((1, D_MODEL))],
        out_specs=_tok_spec(D_MODEL),
        out_shape=jax.ShapeDtypeStruct((nblocks * TM, D_MODEL), F32),
        compiler_params=_cparams(("arbitrary",)),
        name="final_norm",
    )(x, y, y, mod, g)


def kernel(x_prompt, x_sample, c, cache_k, cache_v, state_ret_f, state_ret_b, c_ctx, w_ada, b_ada, norm1_g, norm2_g, w_in, w_out, conv_w, conv_b, conv_ln_g, conv_ln_b, na_rpb, ret_lg_f, ret_lg_b, ret_gn_g, w_route_g, b_route_g, w_route_e, b_route_e, w1, w3, w2, final_g):
    cv = jnp.zeros((COND_ROWS, D_MODEL), F32).at[0].set(c_ctx).at[1:N_COND].set(c)
    mods = _ada(cv, w_ada, b_ada).reshape(DEPTH, COND_ROWS, 6, D_MODEL)
    w_in_b = w_in.astype(BF16)
    w_out_b = w_out.astype(BF16)
    pad = ROUTE_COLS - N_GROUPS - N_EXPERTS
    w_route = jnp.pad(jnp.concatenate([w_route_g, w_route_e], axis=-1), ((0, 0), (0, 0), (0, pad)))
    b_route = jnp.pad(jnp.concatenate([b_route_g, b_route_e], axis=-1), ((0, 0), (0, pad)))
    na_bias = _na_bias_tables(na_rpb)
    rope = _rope_tables()
    lg = jnp.stack([ret_lg_f, ret_lg_b], axis=1)

    x_ctx = x_prompt.reshape(T_CTX, D_MODEL)
    x_lat = x_sample.reshape(T_LAT, D_MODEL)
    x = y = None
    k_list, v_list, sf_list, sb_list = [], [], [], []
    for l in range(DEPTH):
        g1 = norm1_g[l].reshape(1, D_MODEL)
        if l == 0:
            z, x = _inproj_first(x_ctx, x_lat, mods[l], g1, w_in_b[l])
        else:
            z, x = _inproj_next(x, y, mods[l - 1], mods[l], g1, w_in_b[l])
        conv_args = (conv_w[l], conv_b[l].reshape(1, -1), conv_ln_g[l].reshape(1, -1),
                     conv_ln_b[l].reshape(1, -1))
        yc_c = _conv(z, 0, BATCH, SEQ, *conv_args)
        yc_l = _conv(z, T_CTX // DEC_SEQ, DEC_BATCH, DEC_SEQ, *conv_args)
        yn_c, k_l, v_l = _ctx_attn(z)
        yn_l = _na_attn(z, cache_k, cache_v, na_bias, l)
        gn = ret_gn_g[l].reshape(1, RET_WIDTH)
        yr_c, sf_l, sb_l = _retention(z, lg[l], gn, latent=False)
        yr_l = _retention(z, lg[l], gn, latent=True, layer=l, rope=rope,
                          s0_f=state_ret_f, s0_b=state_ret_b)
        x, h, route = _outproj((yc_c, yc_l), (yn_c, yn_l), (yr_c, yr_l), x, mods[l],
                               norm2_g[l].reshape(1, D_MODEL), w_out_b[l], w_route[l],
                               b_route[l].reshape(1, ROUTE_COLS))
        y = _moe(h, w1, w3, w2, l, *_dispatch(route))
        k_list.append(k_l)
        v_list.append(v_l)
        sf_list.append(sf_l)
        sb_list.append(sb_l)
    fg = final_g.reshape(1, D_MODEL)
    y_prompt = _final(x, y, mods[DEPTH - 1], fg, 0, NB_CTX).reshape(BATCH, SEQ, D_MODEL)
    y_sample = _final(x, y, mods[DEPTH - 1], fg, NB_CTX, NB_LAT).reshape(DEC_BATCH, DEC_SEQ, D_MODEL)
    return (y_prompt, y_sample, jnp.stack(k_list, axis=1), jnp.stack(v_list, axis=1),
            jnp.stack(sf_list, axis=1), jnp.stack(sb_list, axis=1))
```

```python
import functools

import numpy as np
import jax
import jax.numpy as jnp
from jax import lax
from jax.experimental import pallas as pl
from jax.experimental.pallas import tpu as pltpu

D_MODEL = 1024
BATCH = 32
SEQ = 256
DEPTH = 2
DEC_BATCH = 4
DEC_SEQ = 4096
PAST_LEN = 512
GRID_W = 64
GRID_H = DEC_SEQ // GRID_W
CONV_CH = 256
CONV_K = 31
NA_HEADS = 8
NA_DIM = 64
NA_WIDTH = NA_HEADS * NA_DIM
NA_KH = 8
NA_KW = 16
RET_HEADS = 4
RET_DIM = 64
RET_WIDTH = RET_HEADS * RET_DIM
RET_CHUNK = 128
ROPE_BASE = 10000.0
N_GROUPS = 4
EXPERTS_PER_GROUP = 8
N_EXPERTS = N_GROUPS * EXPERTS_PER_GROUP
D_EXPERT = 512
IN_COLS = 2 * CONV_CH + 3 * NA_WIDTH + 4 * RET_WIDTH
EPS = 1e-6
NEG_INF = -1e30

F32 = jnp.float32
BF16 = jnp.bfloat16
HIGHEST = lax.Precision.HIGHEST

T_CTX = BATCH * SEQ
T_LAT = DEC_BATCH * DEC_SEQ
T_ALL = T_CTX + T_LAT
N_COND = 1 + DEC_BATCH
COND_ROWS = 8

TM = 512
NB_CTX = T_CTX // TM
NB_LAT = T_LAT // TM
NB_ALL = NB_CTX + NB_LAT
LAT_BLOCKS_PER_REQ = DEC_SEQ // TM

LANES = 128
SUBLANES = 8
ROUTE_COLS = LANES

COL_CONV = 0
COL_NA_Q = 2 * CONV_CH
COL_NA_K = COL_NA_Q + NA_WIDTH
COL_NA_V = COL_NA_K + NA_WIDTH
COL_RET = COL_NA_V + NA_WIDTH

NA_ROWS = 8
NA_Q = NA_ROWS * GRID_W
NA_KROWS = 2 * NA_ROWS
NA_KEYS = NA_KROWS * GRID_W
NA_RB = GRID_H // NA_ROWS

MOE_BLK = 256
MOE_NBLK = (2 * T_ALL) // MOE_BLK + N_EXPERTS
MOE_ROWS = MOE_NBLK * MOE_BLK

VMEM_LIMIT = 56 * 1024 * 1024


def _cparams(sem):
    return pltpu.CompilerParams(dimension_semantics=sem, vmem_limit_bytes=VMEM_LIMIT)


def _sigmoid(x):
    return 1.0 / (1.0 + jnp.exp(-x))


def _cond_row(i):
    return jnp.where(i < NB_CTX, 0, 1 + (i - NB_CTX) // LAT_BLOCKS_PER_REQ)


ADA_TN = 1536


def _ada_kernel(cv_ref, w_ref, b_ref, o_ref):
    cv = cv_ref[...]
    s = cv * _sigmoid(cv)
    o_ref[...] = jnp.dot(s, w_ref[...], precision=HIGHEST, preferred_element_type=F32) + b_ref[...]


def _ada(cv, w_ada, b_ada):
    n = 6 * D_MODEL
    return pl.pallas_call(
        _ada_kernel,
        grid=(DEPTH, n // ADA_TN),
        in_specs=[
            pl.BlockSpec((COND_ROWS, D_MODEL), lambda l, j: (0, 0)),
            pl.BlockSpec((None, D_MODEL, ADA_TN), lambda l, j: (l, 0, j)),
            pl.BlockSpec((None, 1, ADA_TN), lambda l, j: (l, 0, j)),
        ],
        out_specs=pl.BlockSpec((None, COND_ROWS, ADA_TN), lambda l, j: (l, 0, j)),
        out_shape=jax.ShapeDtypeStruct((DEPTH, COND_ROWS, n), F32),
        compiler_params=_cparams(("arbitrary", "arbitrary")),
        name="ada_mod",
    )(cv, w_ada, b_ada.reshape(DEPTH, 1, n))


IN_TN = 768


def _norm_mod(x, g, shift, scale):
    ms = jnp.mean(x * x, axis=-1, keepdims=True)
    return (x * lax.rsqrt(ms + EPS) * g) * (1.0 + scale) + shift


def _inproj_body(x, mod_ref, g_ref, w_ref, z_ref):
    h = _norm_mod(x, g_ref[...], mod_ref[0:1, :], mod_ref[1:2, :]).astype(BF16)
    for c in range(IN_COLS // IN_TN):
        cols = slice(c * IN_TN, (c + 1) * IN_TN)
        z_ref[:, cols] = jnp.dot(h, w_ref[:, cols], preferred_element_type=F32).astype(BF16)


def _inproj_first_kernel(xc_ref, xl_ref, mod_ref, g_ref, w_ref, z_ref, xo_ref):
    i = pl.program_id(0)
    x = jnp.where(i < NB_CTX, xc_ref[...], xl_ref[...])
    xo_ref[...] = x
    _inproj_body(x, mod_ref, g_ref, w_ref, z_ref)


def _moe_residual(x_ref, y0_ref, y1_ref, r_ref, mod_ref):
    r = r_ref[...]
    return x_ref[...] + mod_ref[5:6, :] * (r[:, 2:3] * y0_ref[...] + r[:, 3:4] * y1_ref[...])


def _inproj_next_kernel(x_ref, y0_ref, y1_ref, r_ref, modp_ref, mod_ref, g_ref, w_ref, z_ref, xo_ref):
    x = _moe_residual(x_ref, y0_ref, y1_ref, r_ref, modp_ref)
    xo_ref[...] = x
    _inproj_body(x, mod_ref, g_ref, w_ref, z_ref)


def _tok_spec(cols):
    return pl.BlockSpec((TM, cols), lambda i: (i, 0))


def _mod_spec():
    return pl.BlockSpec((None, 6, D_MODEL), lambda i: (_cond_row(i), 0, 0))


def _full_spec(shape):
    return pl.BlockSpec(shape, lambda i: (0,) * len(shape))


def _ctx_lat_specs(cols):
    return [pl.BlockSpec((TM, cols), lambda i: (jnp.minimum(i, NB_CTX - 1), 0)),
            pl.BlockSpec((TM, cols), lambda i: (jnp.maximum(i - NB_CTX, 0), 0))]


def _inproj_first(x_ctx, x_lat, mod, g, w_bf16):
    return pl.pallas_call(
        _inproj_first_kernel,
        grid=(NB_ALL,),
        in_specs=_ctx_lat_specs(D_MODEL) + [_mod_spec(), _full_spec((1, D_MODEL)),
                                            _full_spec((D_MODEL, IN_COLS))],
        out_specs=[_tok_spec(IN_COLS), _tok_spec(D_MODEL)],
        out_shape=[jax.ShapeDtypeStruct((T_ALL, IN_COLS), BF16),
                   jax.ShapeDtypeStruct((T_ALL, D_MODEL), F32)],
        compiler_params=_cparams(("arbitrary",)),
        name="inproj_first",
    )(x_ctx, x_lat, mod, g, w_bf16)


def _inproj_next(x, y, route, mod_prev, mod, g, w_bf16):
    return pl.pallas_call(
        _inproj_next_kernel,
        grid=(NB_ALL,),
        in_specs=[_tok_spec(D_MODEL),
                  pl.BlockSpec((TM, D_MODEL), lambda i: (i, 0)),
                  pl.BlockSpec((TM, D_MODEL), lambda i: (i, 1)),
                  _tok_spec(ROUTE_COLS),
                  _mod_spec(), _mod_spec(), _full_spec((1, D_MODEL)),
                  _full_spec((D_MODEL, IN_COLS))],
        out_specs=[_tok_spec(IN_COLS), _tok_spec(D_MODEL)],
        out_shape=[jax.ShapeDtypeStruct((T_ALL, IN_COLS), BF16),
                   jax.ShapeDtypeStruct((T_ALL, D_MODEL), F32)],
        compiler_params=_cparams(("arbitrary",)),
        name="inproj_next",
    )(x, y, y, route, mod_prev, mod, g, w_bf16)


CONV_PAD = 16
CONV_CHUNK = 64


def _conv_kernel(seq, z_ref, w_ref, b_ref, g_ref, be_ref, o_ref, upad_ref):
    zeros = jnp.zeros((CONV_PAD, CONV_CH), F32)
    upad_ref[0:CONV_PAD, :] = zeros
    upad_ref[seq + CONV_PAD:seq + 2 * CONV_PAD, :] = zeros

    def glu(ci, carry):
        base = pl.multiple_of(ci * 256, 256)
        zc = z_ref[pl.ds(base, 256), :].astype(F32)
        upad_ref[pl.ds(base + CONV_PAD, 256), :] = zc[:, :CONV_CH] * _sigmoid(zc[:, CONV_CH:])
        return carry

    lax.fori_loop(0, seq // 256, glu, 0)

    shift = CONV_PAD - CONV_K // 2

    def chunk(ci, carry):
        base = pl.multiple_of(ci * CONV_CHUNK, CONV_CHUNK)
        win = upad_ref[pl.ds(base, CONV_CHUNK + 2 * CONV_PAD), :]
        acc = jnp.zeros((CONV_CHUNK, CONV_CH), F32)
        for k in range(CONV_K):
            acc = acc + w_ref[k:k + 1, :] * win[k + shift:k + shift + CONV_CHUNK, :]
        acc = acc + b_ref[...]
        mu = jnp.mean(acc, axis=-1, keepdims=True)
        d = acc - mu
        var = jnp.mean(d * d, axis=-1, keepdims=True)
        n = d * lax.rsqrt(var + EPS) * g_ref[...] + be_ref[...]
        o_ref[pl.ds(base, CONV_CHUNK), :] = (n * _sigmoid(n)).astype(BF16)
        return carry

    lax.fori_loop(0, seq // CONV_CHUNK, chunk, 0)


def _conv(z, row_block0, nseq, seq, w, b, g, be):
    return pl.pallas_call(
        functools.partial(_conv_kernel, seq),
        grid=(nseq,),
        in_specs=[pl.BlockSpec((seq, 2 * CONV_CH), lambda s: (row_block0 + s, 0)),
                  _full_spec((CONV_K, CONV_CH)), _full_spec((1, CONV_CH)),
                  _full_spec((1, CONV_CH)), _full_spec((1, CONV_CH))],
        out_specs=pl.BlockSpec((seq, CONV_CH), lambda s: (s, 0)),
        out_shape=jax.ShapeDtypeStruct((nseq * seq, CONV_CH), BF16),
        scratch_shapes=[pltpu.VMEM((seq + 2 * CONV_PAD, CONV_CH), F32)],
        compiler_params=_cparams(("arbitrary",)),
        name="conv_seq%d" % seq,
    )(z, w, b, g, be)


def _dot_nt(a, b):
    return lax.dot_general(a, b, (((1,), (1,)), ((), ())), preferred_element_type=F32)


def _ctx_attn_kernel(q_ref, k_ref, v_ref, o_ref, ko_ref, vo_ref):
    scale = NA_DIM ** -0.5
    outs = []
    for h in range(NA_HEADS):
        cols = slice(h * NA_DIM, (h + 1) * NA_DIM)
        qh, kh, vh = q_ref[:, cols], k_ref[:, cols], v_ref[:, cols]
        ko_ref[h] = kh.astype(F32)
        vo_ref[h] = vh.astype(F32)
        s = _dot_nt(qh, kh) * scale
        m = jnp.max(s, axis=-1, keepdims=True)
        p = jnp.exp(s - m)
        den = jnp.sum(p, axis=-1, keepdims=True)
        o = jnp.dot(p.astype(BF16), vh, preferred_element_type=F32)
        outs.append(o / den)
    o_ref[...] = jnp.concatenate(outs, axis=-1).astype(BF16)


def _ctx_attn(z):
    qb, kb, vb = COL_NA_Q // NA_WIDTH, COL_NA_K // NA_WIDTH, COL_NA_V // NA_WIDTH
    head_shape = jax.ShapeDtypeStruct((BATCH, NA_HEADS, SEQ, NA_DIM), F32)
    head_spec = pl.BlockSpec((None, NA_HEADS, SEQ, NA_DIM), lambda b: (b, 0, 0, 0))
    return pl.pallas_call(
        _ctx_attn_kernel,
        grid=(BATCH,),
        in_specs=[pl.BlockSpec((SEQ, NA_WIDTH), lambda b: (b, qb)),
                  pl.BlockSpec((SEQ, NA_WIDTH), lambda b: (b, kb)),
                  pl.BlockSpec((SEQ, NA_WIDTH), lambda b: (b, vb))],
        out_specs=[pl.BlockSpec((SEQ, NA_WIDTH), lambda b: (b, 0)), head_spec, head_spec],
        out_shape=[jax.ShapeDtypeStruct((T_CTX, NA_WIDTH), BF16), head_shape, head_shape],
        compiler_params=_cparams(("arbitrary",)),
        name="ctx_attn",
    )(z, z, z)


NA_KINDS = (0, NA_ROWS, GRID_H - NA_ROWS)
N_DR = 2 * NA_KH - 1
N_DC = 2 * NA_KW - 1


def _na_row_offset(r0, i, j):
    ks = min(max(r0 - NA_KH // 2, 0), GRID_H - NA_KROWS)
    r, kr = r0 + i, ks + j
    rs = min(max(r - NA_KH // 2, 0), GRID_H - NA_KH)
    return kr - r + NA_KH - 1 if rs <= kr < rs + NA_KH else None


def _na_bias_kernel(rpb_ref, o_ref):
    lh = pl.program_id(0)
    shape = (GRID_W, 2 * GRID_W)
    qc = lax.broadcasted_iota(jnp.int32, shape, 0)
    lane = lax.broadcasted_iota(jnp.int32, shape, 1)
    kc = lane % GRID_W
    dc = jnp.clip(kc - qc, -(NA_KW - 1), NA_KW - 1) + NA_KW - 1
    cs = jnp.clip(qc - NA_KW // 2, 0, GRID_W - NA_KW)
    col_ok = (kc >= cs) & (kc < cs + NA_KW)
    neg = jnp.full(shape, NEG_INF, F32)
    tiles = []
    for dr in range(N_DR):
        base = (lh * N_DR + dr) * N_DC
        val = jnp.zeros(shape, F32)
        for d in range(N_DC):
            val = jnp.where(dc == d, rpb_ref[base + d], val)
        tiles.append(jnp.where(col_ok, val, neg))
    left = lane < GRID_W
    for kind, r0 in enumerate(NA_KINDS):
        for i in range(NA_ROWS):
            for jp in range(NA_KROWS // 2):
                dl, dr_ = _na_row_offset(r0, i, 2 * jp), _na_row_offset(r0, i, 2 * jp + 1)
                tl = neg if dl is None else tiles[dl]
                tr = neg if dr_ is None else tiles[dr_]
                o_ref[kind, i * GRID_W:(i + 1) * GRID_W, jp * 2 * GRID_W:(jp + 1) * 2 * GRID_W] = (
                    jnp.where(left, tl, tr))


def _na_bias_tables(rpb):
    return pl.pallas_call(
        _na_bias_kernel,
        grid=(DEPTH * NA_HEADS,),
        in_specs=[pl.BlockSpec(memory_space=pltpu.SMEM)],
        out_specs=pl.BlockSpec((None, len(NA_KINDS), NA_Q, NA_KEYS), lambda i: (i, 0, 0, 0)),
        out_shape=jax.ShapeDtypeStruct((DEPTH * NA_HEADS, len(NA_KINDS), NA_Q, NA_KEYS), F32),
        compiler_params=_cparams(("arbitrary",)),
        name="nbr_bias",
    )(rpb.reshape(-1))


def _na_kernel(q_ref, k_ref, v_ref, kc_ref, vc_ref, bias_ref, o_ref):
    rb = pl.program_id(2)
    scale = NA_DIM ** -0.5
    ks = jnp.clip(rb * NA_ROWS - NA_KH // 2, 0, GRID_H - NA_KROWS)
    start = pl.multiple_of(ks * GRID_W, GRID_W)
    q = q_ref[...]
    kl = k_ref[pl.ds(start, NA_KEYS), :]
    vl = v_ref[pl.ds(start, NA_KEYS), :]
    outs = []
    for hh in range(2):
        cols = slice(hh * NA_DIM, (hh + 1) * NA_DIM)
        qh = q[:, cols]
        s_loc = _dot_nt(qh, kl[:, cols]) * scale + bias_ref[hh]
        s_ctx = _dot_nt(qh, kc_ref[hh].astype(BF16)) * scale
        m = jnp.maximum(jnp.max(s_loc, axis=-1, keepdims=True), jnp.max(s_ctx, axis=-1, keepdims=True))
        p_loc = jnp.exp(s_loc - m)
        p_ctx = jnp.exp(s_ctx - m)
        den = jnp.sum(p_loc, axis=-1, keepdims=True) + jnp.sum(p_ctx, axis=-1, keepdims=True)
        o = (jnp.dot(p_loc.astype(BF16), vl[:, cols], preferred_element_type=F32)
             + jnp.dot(p_ctx.astype(BF16), vc_ref[hh].astype(BF16), preferred_element_type=F32))
        outs.append(o / den)
    o_ref[...] = jnp.concatenate(outs, axis=-1).astype(BF16)


def _na_attn(z, cache_k, cache_v, bias, layer):
    lat_q0 = T_CTX // NA_Q
    lat_s0 = T_CTX // DEC_SEQ
    qc, kc, vc = COL_NA_Q // LANES, COL_NA_K // LANES, COL_NA_V // LANES

    def kind(rb):
        return jnp.where(rb == 0, 0, jnp.where(rb == NA_RB - 1, 2, 1))

    ctx_spec = pl.BlockSpec((None, None, 2, PAST_LEN, NA_DIM), lambda b, hp, rb: (b, layer, hp, 0, 0))
    return pl.pallas_call(
        _na_kernel,
        grid=(DEC_BATCH, NA_HEADS // 2, NA_RB),
        in_specs=[pl.BlockSpec((NA_Q, LANES), lambda b, hp, rb: (lat_q0 + b * NA_RB + rb, qc + hp)),
                  pl.BlockSpec((DEC_SEQ, LANES), lambda b, hp, rb: (lat_s0 + b, kc + hp)),
                  pl.BlockSpec((DEC_SEQ, LANES), lambda b, hp, rb: (lat_s0 + b, vc + hp)),
                  ctx_spec, ctx_spec,
                  pl.BlockSpec((2, None, NA_Q, NA_KEYS),
                               lambda b, hp, rb: (layer * (NA_HEADS // 2) + hp, kind(rb), 0, 0))],
        out_specs=pl.BlockSpec((NA_Q, LANES), lambda b, hp, rb: (b * NA_RB + rb, hp)),
        out_shape=jax.ShapeDtypeStruct((T_LAT, NA_WIDTH), BF16),
        compiler_params=_cparams(("arbitrary", "arbitrary", "arbitrary")),
        name="nbr_attn",
    )(z, z, z, cache_k, cache_v, bias)


def _rope_tables():
    n_freq = RET_DIM // 4
    t = np.arange(DEC_SEQ)
    inv = jnp.asarray(ROPE_BASE, F32) ** (-jnp.arange(n_freq, dtype=F32) / n_freq)
    ang_r = jnp.asarray(t // GRID_W, F32)[:, None] * inv[None, :]
    ang_c = jnp.asarray(t % GRID_W, F32)[:, None] * inv[None, :]
    cos = jnp.concatenate([jnp.cos(ang_r)] * 2 + [jnp.cos(ang_c)] * 2, axis=-1)
    sin = jnp.concatenate([-jnp.sin(ang_r), jnp.sin(ang_r), -jnp.sin(ang_c), jnp.sin(ang_c)], axis=-1)
    return jnp.tile(cos, (1, RET_HEADS)), jnp.tile(sin, (1, RET_HEADS))


def _ret_kernel(seq, latent, *refs):
    if latent:
        (lg_ref, z_ref, gn_ref, cos_ref, sin_ref, s0f_ref, s0b_ref, y_ref,
         q_s, k_s, kv_s, st_s) = refs
    else:
        lg_ref, z_ref, gn_ref, y_ref, sf_ref, sb_ref, q_s, k_s, kv_s, st_s = refs
    nc = seq // RET_CHUNK
    ch, hd = RET_CHUNK, RET_DIM
    half = RET_DIM // 4

    row = lax.broadcasted_iota(jnp.int32, (ch, ch), 0).astype(F32)
    col = lax.broadcasted_iota(jnp.int32, (ch, ch), 1).astype(F32)
    pos = lax.broadcasted_iota(jnp.int32, (ch, hd), 0).astype(F32)
    decay, q_dec, k_dec, c_dec_f, c_dec_b = [], [], [], [], []
    for h in range(RET_HEADS):
        lf, lb = lg_ref[0, h], lg_ref[1, h]
        d_f = jnp.where(row >= col, jnp.exp(jnp.maximum(row - col, 0.0) * lf), 0.0)
        d_b = jnp.where(col >= row, jnp.exp(jnp.maximum(col - row, 0.0) * lb), 0.0)
        decay.append(d_f + d_b)
        q_dec.append(jnp.concatenate([jnp.exp((pos + 1.0) * lf), jnp.exp((ch - pos) * lb)], axis=-1))
        k_dec.append(jnp.concatenate([jnp.exp((ch - 1.0 - pos) * lf), jnp.exp(pos * lb)], axis=-1))
        c_dec_f.append(jnp.exp(jnp.zeros((hd, hd), F32) + ch * lf))
        c_dec_b.append(jnp.exp(jnp.zeros((hd, hd), F32) + ch * lb))

    if latent:
        lane = lax.broadcasted_iota(jnp.int32, (ch, RET_WIDTH), 1)
        first_half = (lane % (2 * half)) < half

    def rope(x, base):
        if not latent:
            return x
        swapped = jnp.where(first_half, pltpu.roll(x, RET_WIDTH - half, 1), pltpu.roll(x, half, 1))
        return x * cos_ref[pl.ds(base, ch), :] + swapped * sin_ref[pl.ds(base, ch), :]

    def pass1(n, carry):
        base = pl.multiple_of(n * ch, ch)
        zc = z_ref[pl.ds(base, ch), :]
        q = rope(zc[:, 0:RET_WIDTH].astype(F32), base)
        k = rope(zc[:, RET_WIDTH:2 * RET_WIDTH].astype(F32) * (RET_DIM ** -0.5), base)
        q_s[pl.ds(base, ch), :] = q.astype(BF16)
        k_s[pl.ds(base, ch), :] = k.astype(BF16)
        v = zc[:, 2 * RET_WIDTH:3 * RET_WIDTH]
        for h in range(RET_HEADS):
            cols = slice(h * hd, (h + 1) * hd)
            kh = k[:, cols]
            k2 = (jnp.concatenate([kh, kh], axis=-1) * k_dec[h]).astype(BF16)
            kv_s[n, h] = lax.dot_general(k2, v[:, cols], (((0,), (0,)), ((), ())),
                                         preferred_element_type=F32)
        return carry

    lax.fori_loop(0, nc, pass1, 0)

    for h in range(RET_HEADS):
        if latent:
            s_f, s_b = s0f_ref[h], s0b_ref[h]
        else:
            s_f = s_b = jnp.zeros((hd, hd), F32)

        def fwd(n, s, h=h):
            st_s[n, h, 0:hd, :] = s
            return c_dec_f[h] * s + kv_s[n, h, 0:hd, :]

        def bwd(i, s, h=h):
            n = nc - 1 - i
            st_s[n, h, hd:2 * hd, :] = s
            return c_dec_b[h] * s + kv_s[n, h, hd:2 * hd, :]

        s_f = lax.fori_loop(0, nc, fwd, s_f)
        s_b = lax.fori_loop(0, nc, bwd, s_b)
        if not latent:
            sf_ref[h] = s_f
            sb_ref[h] = s_b

    def pass3(n, carry):
        base = pl.multiple_of(n * ch, ch)
        zc = z_ref[pl.ds(base, ch), :]
        q = q_s[pl.ds(base, ch), :]
        k = k_s[pl.ds(base, ch), :]
        v = zc[:, 2 * RET_WIDTH:3 * RET_WIDTH]
        gate = zc[:, 3 * RET_WIDTH:4 * RET_WIDTH].astype(F32)
        outs = []
        for h in range(RET_HEADS):
            cols = slice(h * hd, (h + 1) * hd)
            qh = q[:, cols]
            s = _dot_nt(qh, k[:, cols]) * decay[h]
            o = jnp.dot(s.astype(BF16), v[:, cols], preferred_element_type=F32)
            qf = qh.astype(F32)
            q2 = (jnp.concatenate([qf, qf], axis=-1) * q_dec[h]).astype(BF16)
            o = o + jnp.dot(q2, st_s[n, h].astype(BF16), preferred_element_type=F32)
            mu = jnp.mean(o, axis=-1, keepdims=True)
            d = o - mu
            var = jnp.mean(d * d, axis=-1, keepdims=True)
            outs.append(d * lax.rsqrt(var + EPS))
        nrm = jnp.concatenate(outs, axis=-1)
        y_ref[pl.ds(base, ch), :] = (nrm * gn_ref[...] * (gate * _sigmoid(gate))).astype(BF16)
        return carry

    lax.fori_loop(0, nc, pass3, 0)


def _retention(z, lg, gn_g, latent, layer=None, rope=None, s0_f=None, s0_b=None):
    seq = DEC_SEQ if latent else SEQ
    nseq = DEC_BATCH if latent else BATCH
    nc = seq // RET_CHUNK
    row0 = (T_CTX // DEC_SEQ) if latent else 0
    cb = COL_RET // (4 * RET_WIDTH)
    in_specs = [pl.BlockSpec(memory_space=pltpu.SMEM),
                pl.BlockSpec((seq, 4 * RET_WIDTH), lambda s: (row0 + s, cb)),
                _full_spec((1, RET_WIDTH))]
    args = [lg, z, gn_g]
    state_shape = jax.ShapeDtypeStruct((nseq, RET_HEADS, RET_DIM, RET_DIM), F32)
    y_spec = pl.BlockSpec((seq, RET_WIDTH), lambda s: (s, 0))
    y_shape = jax.ShapeDtypeStruct((nseq * seq, RET_WIDTH), BF16)
    if latent:
        st_spec = pl.BlockSpec((None, None, RET_HEADS, RET_DIM, RET_DIM), lambda s: (s, layer, 0, 0, 0))
        in_specs += [_full_spec((seq, RET_WIDTH)), _full_spec((seq, RET_WIDTH)), st_spec, st_spec]
        args += [rope[0], rope[1], s0_f, s0_b]
        out_specs, out_shape = y_spec, y_shape
    else:
        so_spec = pl.BlockSpec((None, RET_HEADS, RET_DIM, RET_DIM), lambda s: (s, 0, 0, 0))
        out_specs, out_shape = [y_spec, so_spec, so_spec], [y_shape, state_shape, state_shape]
    return pl.pallas_call(
        functools.partial(_ret_kernel, seq, latent),
        grid=(nseq,),
        in_specs=in_specs,
        out_specs=out_specs,
        out_shape=out_shape,
        scratch_shapes=[pltpu.VMEM((seq, RET_WIDTH), BF16), pltpu.VMEM((seq, RET_WIDTH), BF16),
                        pltpu.VMEM((nc, RET_HEADS, 2 * RET_DIM, RET_DIM), F32),
                        pltpu.VMEM((nc, RET_HEADS, 2 * RET_DIM, RET_DIM), F32)],
        compiler_params=_cparams(("arbitrary",)),
        name="retention_lat" if latent else "retention_ctx",
    )(*args)


def _route(logits, count_ref):
    lane = lax.broadcasted_iota(jnp.int32, logits.shape, 1)
    lane_f = lane.astype(F32)
    big = float(ROUTE_COLS)
    neg = -jnp.inf
    is_grp = lane < N_GROUPS
    gl = jnp.where(is_grp, logits, neg)
    gmax = jnp.max(gl, axis=-1, keepdims=True)
    grp = jnp.min(jnp.where(gl == gmax, lane_f, big), axis=-1, keepdims=True)
    p_grp = 1.0 / jnp.sum(jnp.exp(gl - gmax), axis=-1, keepdims=True)
    e_f = lane_f - N_GROUPS
    lo = grp * EXPERTS_PER_GROUP
    in_grp = (e_f >= lo) & (e_f < lo + EXPERTS_PER_GROUP)
    el = jnp.where(in_grp, logits, neg)
    m1 = jnp.max(el, axis=-1, keepdims=True)
    i1 = jnp.min(jnp.where(el == m1, lane_f, big), axis=-1, keepdims=True)
    el2 = jnp.where(lane_f == i1, neg, el)
    m2 = jnp.max(el2, axis=-1, keepdims=True)
    i2 = jnp.min(jnp.where(el2 == m2, lane_f, big), axis=-1, keepdims=True)
    t = jnp.exp(m2 - m1)
    g1 = p_grp / (1.0 + t)
    g2 = p_grp * t / (1.0 + t)
    rows = logits.shape[0]
    oh1, oh2 = lane_f == i1, lane_f == i2
    oh = jnp.where(oh1 | oh2, 1.0, 0.0)
    tri = (lax.broadcasted_iota(jnp.int32, (rows, rows), 0)
           > lax.broadcasted_iota(jnp.int32, (rows, rows), 1))
    before = count_ref[...] + jnp.dot(jnp.where(tri, 1.0, 0.0).astype(BF16), oh.astype(BF16),
                                      preferred_element_type=F32)
    r1 = jnp.sum(jnp.where(oh1, before, 0.0), axis=-1, keepdims=True)
    r2 = jnp.sum(jnp.where(oh2, before, 0.0), axis=-1, keepdims=True)
    count_ref[...] = count_ref[...] + jnp.sum(oh, axis=0, keepdims=True)
    out = jnp.zeros(logits.shape, F32)
    for k, val in enumerate((i1 - N_GROUPS, i2 - N_GROUPS, g1, g2, r1, r2)):
        out = jnp.where(lane == k, val, out)
    return out


def _outproj_kernel(ycc, ycl, ync, ynl, yrc, yrl, x_ref, mod_ref, g_ref, w_ref, wr_ref, br_ref,
                    xo_ref, h_ref, r_ref, cnt_ref):
    is_ctx = pl.program_id(0) < NB_CTX

    @pl.when(pl.program_id(0) == 0)
    def _():
        cnt_ref[...] = jnp.zeros_like(cnt_ref)

    yc = jnp.where(is_ctx, ycc[...], ycl[...])
    yn = jnp.where(is_ctx, ync[...], ynl[...])
    yr = jnp.where(is_ctx, yrc[...], yrl[...])
    y = (jnp.dot(yc, w_ref[0:CONV_CH, :], preferred_element_type=F32)
         + jnp.dot(yn, w_ref[CONV_CH:CONV_CH + NA_WIDTH, :], preferred_element_type=F32)
         + jnp.dot(yr, w_ref[CONV_CH + NA_WIDTH:, :], preferred_element_type=F32))
    x = x_ref[...] + mod_ref[2:3, :] * y
    xo_ref[...] = x
    h = _norm_mod(x, g_ref[...], mod_ref[3:4, :], mod_ref[4:5, :])
    h_ref[...] = h
    logits = jnp.dot(h, wr_ref[...], precision=HIGHEST, preferred_element_type=F32) + br_ref[...]
    r_ref[...] = _route(logits, cnt_ref)


def _outproj(y_conv, y_na, y_ret, x, mod, g, w_bf16, w_route, b_route):
    return pl.pallas_call(
        _outproj_kernel,
        grid=(NB_ALL,),
        in_specs=(_ctx_lat_specs(CONV_CH) + _ctx_lat_specs(NA_WIDTH) + _ctx_lat_specs(RET_WIDTH)
                  + [_tok_spec(D_MODEL), _mod_spec(), _full_spec((1, D_MODEL)),
                     _full_spec((D_MODEL, D_MODEL)), _full_spec((D_MODEL, ROUTE_COLS)),
                     _full_spec((1, ROUTE_COLS))]),
        out_specs=[_tok_spec(D_MODEL), _tok_spec(D_MODEL), _tok_spec(ROUTE_COLS),
                   _full_spec((1, ROUTE_COLS))],
        out_shape=[jax.ShapeDtypeStruct((T_ALL, D_MODEL), F32),
                   jax.ShapeDtypeStruct((T_ALL, D_MODEL), F32),
                   jax.ShapeDtypeStruct((T_ALL, ROUTE_COLS), F32),
                   jax.ShapeDtypeStruct((1, ROUTE_COLS), F32)],
        compiler_params=_cparams(("arbitrary",)),
        name="outproj_route",
    )(y_conv[0], y_conv[1], y_na[0], y_na[1], y_ret[0], y_ret[1], x, mod, g, w_bf16, w_route, b_route)


def _dispatch_tables(route, counts):
    counts = counts[0, N_GROUPS:N_GROUPS + N_EXPERTS].astype(jnp.int32)
    experts = jnp.arange(N_EXPERTS, dtype=jnp.int32)
    nblk = (counts + MOE_BLK - 1) // MOE_BLK
    blk_end = jnp.cumsum(nblk)
    blk_start = blk_end - nblk
    row_start = blk_start * MOE_BLK
    eid = route[:, 0:2].astype(jnp.int32)
    rank = route[:, 4:6].astype(jnp.int32)
    start_of = jnp.sum(jnp.where(eid[:, :, None] == experts[None, None, :], row_start[None, None, :], 0), axis=-1)
    dest = (start_of + rank).reshape(-1)
    blk = jnp.arange(MOE_NBLK, dtype=jnp.int32)
    n_active = blk_end[-1]
    blk_src = jnp.minimum(blk, n_active - 1)
    blk_e = jnp.minimum(jnp.sum((blk_end[None, :] <= blk_src[:, None]).astype(jnp.int32), axis=-1),
                        N_EXPERTS - 1)
    pad_start = row_start + counts
    pad_len = nblk * MOE_BLK - counts
    return dest, blk_src, blk_e, pad_start, pad_len


PERM_CHUNK = 1024


def _wait_rows(src_hbm, dst_hbm, sem, rows):
    pltpu.make_async_copy(src_hbm.at[pl.ds(0, rows)], dst_hbm.at[pl.ds(0, rows)], sem).wait()


def _permute_steps(issue_row, src_hbm, dst_hbm, sem):
    i = pl.program_id(0)
    base = i * PERM_CHUNK

    def issue(r, carry):
        issue_row(base + r)
        return carry

    lax.fori_loop(0, PERM_CHUNK, issue, 0)

    @pl.when(i > 0)
    def _():
        _wait_rows(src_hbm, dst_hbm, sem, PERM_CHUNK)

    @pl.when(i == pl.num_programs(0) - 1)
    def _():
        _wait_rows(src_hbm, dst_hbm, sem, PERM_CHUNK)


PAD_BITS = MOE_BLK.bit_length() - 1


def _scatter_rows_kernel(dest, pad_start, pad_len, h_hbm, xs_hbm, zeros, sem, zsem):
    @pl.when(pl.program_id(0) == 0)
    def _():
        zeros[...] = jnp.zeros_like(zeros)

        def pad_copies(e):
            n_pad, start = pad_len[e], pad_start[e]
            head = n_pad & (SUBLANES - 1)
            for r in range(SUBLANES - 1):
                yield r < head, pltpu.make_async_copy(
                    zeros.at[pl.ds(0, 1)], xs_hbm.at[pl.ds(start + r, 1)], zsem)
            for bit in range(SUBLANES.bit_length() - 1, PAD_BITS):
                n = 1 << bit
                off = pl.multiple_of(start + head + ((n_pad - head) & (n - 1)), SUBLANES)
                yield (n_pad >> bit) & 1 == 1, pltpu.make_async_copy(
                    zeros.at[pl.ds(0, n)], xs_hbm.at[pl.ds(off, n)], zsem)

        def fill(e, carry):
            for cond, copy in pad_copies(e):
                pl.when(cond)(copy.start)
            return carry

        def drain(e, carry):
            for cond, copy in pad_copies(e):
                pl.when(cond)(copy.wait)
            return carry

        lax.fori_loop(0, N_EXPERTS, fill, 0)
        lax.fori_loop(0, N_EXPERTS, drain, 0)

        n_z = zeros.shape[0]
        used = pad_start[N_EXPERTS - 1] + pad_len[N_EXPERTS - 1]

        def tail_copy(k):
            off = pl.multiple_of(used + k * n_z, n_z)
            return pltpu.make_async_copy(zeros, xs_hbm.at[pl.ds(off, n_z)], zsem)

        n_tail = (MOE_ROWS - used) // n_z
        lax.fori_loop(0, n_tail, lambda k, c: (tail_copy(k).start(), c)[1], 0)
        lax.fori_loop(0, n_tail, lambda k, c: (tail_copy(k).wait(), c)[1], 0)

    def issue_row(a):
        pltpu.make_async_copy(h_hbm.at[pl.ds(a >> 1, 1)], xs_hbm.at[pl.ds(dest[a], 1)], sem).start()

    _permute_steps(issue_row, h_hbm, xs_hbm, sem)


def _scatter_rows(h, dest, pad_start, pad_len):
    grid_spec = pltpu.PrefetchScalarGridSpec(
        num_scalar_prefetch=3,
        grid=(2 * T_ALL // PERM_CHUNK,),
        in_specs=[pl.BlockSpec(memory_space=pl.ANY)],
        out_specs=pl.BlockSpec(memory_space=pl.ANY),
        scratch_shapes=[pltpu.VMEM((MOE_BLK // 2, D_MODEL), F32),
                        pltpu.SemaphoreType.DMA, pltpu.SemaphoreType.DMA])
    return pl.pallas_call(
        _scatter_rows_kernel,
        grid_spec=grid_spec,
        out_shape=jax.ShapeDtypeStruct((MOE_ROWS, D_MODEL), F32),
        compiler_params=_cparams(("arbitrary",)),
        name="moe_dispatch",
    )(dest, pad_start, pad_len, h)


def _gather_rows_kernel(dest, ys_hbm, y_hbm, sem):
    def issue_row(a):
        pltpu.make_async_copy(ys_hbm.at[pl.ds(dest[a], 1)], y_hbm.at[pl.ds(a, 1)], sem).start()

    _permute_steps(issue_row, ys_hbm, y_hbm, sem)


def _gather_rows(ys, dest):
    grid_spec = pltpu.PrefetchScalarGridSpec(
        num_scalar_prefetch=1,
        grid=(2 * T_ALL // PERM_CHUNK,),
        in_specs=[pl.BlockSpec(memory_space=pl.ANY)],
        out_specs=pl.BlockSpec(memory_space=pl.ANY),
        scratch_shapes=[pltpu.SemaphoreType.DMA])
    return pl.pallas_call(
        _gather_rows_kernel,
        grid_spec=grid_spec,
        out_shape=jax.ShapeDtypeStruct((2 * T_ALL, D_MODEL), F32),
        compiler_params=_cparams(("arbitrary",)),
        name="moe_combine",
    )(dest, ys)


def _moe_kernel(blk_src, blk_e, xs_ref, w1_ref, w3_ref, w2_ref, ys_ref, w1b, w3b, w2b):
    i = pl.program_id(0)

    @pl.when(blk_src[i] == i)
    def _():
        @pl.when((i == 0) | (blk_e[i] != blk_e[jnp.maximum(i - 1, 0)]))
        def _():
            w1b[...] = w1_ref[...].astype(BF16)
            w3b[...] = w3_ref[...].astype(BF16)
            w2b[...] = w2_ref[...].astype(BF16)

        xb = xs_ref[...].astype(BF16)
        a = jnp.dot(xb, w1b[...], preferred_element_type=F32)
        b = jnp.dot(xb, w3b[...], preferred_element_type=F32)
        mid = (a * _sigmoid(a) * b).astype(BF16)
        ys_ref[...] = jnp.dot(mid, w2b[...], preferred_element_type=F32)

    @pl.when(blk_src[i] != i)
    def _():
        ys_ref[...] = jnp.zeros_like(ys_ref)


def _moe(xs, w1, w3, w2, layer, blk_src, blk_e):
    def w_spec(rows, cols):
        return pl.BlockSpec((None, None, rows, cols), lambda i, bs, be: (layer, be[i], 0, 0))

    row_spec = pl.BlockSpec((MOE_BLK, D_MODEL), lambda i, bs, be: (bs[i], 0))
    grid_spec = pltpu.PrefetchScalarGridSpec(
        num_scalar_prefetch=2,
        grid=(MOE_NBLK,),
        in_specs=[row_spec, w_spec(D_MODEL, D_EXPERT), w_spec(D_MODEL, D_EXPERT),
                  w_spec(D_EXPERT, D_MODEL)],
        out_specs=pl.BlockSpec((MOE_BLK, D_MODEL), lambda i, bs, be: (i, 0)),
        scratch_shapes=[pltpu.VMEM((D_MODEL, D_EXPERT), BF16), pltpu.VMEM((D_MODEL, D_EXPERT), BF16),
                        pltpu.VMEM((D_EXPERT, D_MODEL), BF16)])
    return pl.pallas_call(
        _moe_kernel,
        grid_spec=grid_spec,
        out_shape=jax.ShapeDtypeStruct((MOE_ROWS, D_MODEL), F32),
        compiler_params=_cparams(("arbitrary",)),
        name="moe_experts",
    )(blk_src, blk_e, xs, w1, w3, w2)


def _final_kernel(x_ref, y0_ref, y1_ref, r_ref, mod_ref, g_ref, o_ref):
    x = _moe_residual(x_ref, y0_ref, y1_ref, r_ref, mod_ref)
    ms = jnp.mean(x * x, axis=-1, keepdims=True)
    o_ref[...] = x * lax.rsqrt(ms + EPS) * g_ref[...]


def _final(x, y, route, mod, g, block0, nblocks):
    return pl.pallas_call(
        _final_kernel,
        grid=(nblocks,),
        in_specs=[pl.BlockSpec((TM, D_MODEL), lambda i: (block0 + i, 0)),
                  pl.BlockSpec((TM, D_MODEL), lambda i: (block0 + i, 0)),
                  pl.BlockSpec((TM, D_MODEL), lambda i: (block0 + i, 1)),
                  pl.BlockSpec((TM, ROUTE_COLS), lambda i: (block0 + i, 0)),
                  pl.BlockSpec((None, 6, D_MODEL), lambda i: (_cond_row(block0 + i), 0, 0)),
                  _full_spec((1, D_MODEL))],
        out_specs=_tok_spec(D_MODEL),
        out_shape=jax.ShapeDtypeStruct((nblocks * TM, D_MODEL), F32),
        compiler_params=_cparams(("arbitrary",)),
        name="final_norm",
    )(x, y, y, route, mod, g)


def kernel(x_prompt, x_sample, c, cache_k, cache_v, state_ret_f, state_ret_b, c_ctx, w_ada, b_ada, norm1_g, norm2_g, w_in, w_out, conv_w, conv_b, conv_ln_g, conv_ln_b, na_rpb, ret_lg_f, ret_lg_b, ret_gn_g, w_route_g, b_route_g, w_route_e, b_route_e, w1, w3, w2, final_g):
    cv = jnp.zeros((COND_ROWS, D_MODEL), F32).at[0].set(c_ctx).at[1:N_COND].set(c)
    mods = _ada(cv, w_ada, b_ada).reshape(DEPTH, COND_ROWS, 6, D_MODEL)
    w_in_b = w_in.astype(BF16)
    w_out_b = w_out.astype(BF16)
    pad = ROUTE_COLS - N_GROUPS - N_EXPERTS
    w_route = jnp.pad(jnp.concatenate([w_route_g, w_route_e], axis=-1), ((0, 0), (0, 0), (0, pad)))
    b_route = jnp.pad(jnp.concatenate([b_route_g, b_route_e], axis=-1), ((0, 0), (0, pad)))
    na_bias = _na_bias_tables(na_rpb)
    rope = _rope_tables()
    lg = jnp.stack([ret_lg_f, ret_lg_b], axis=1)

    x_ctx = x_prompt.reshape(T_CTX, D_MODEL)
    x_lat = x_sample.reshape(T_LAT, D_MODEL)
    x = y = route = None
    k_list, v_list, sf_list, sb_list = [], [], [], []
    for l in range(DEPTH):
        g1 = norm1_g[l].reshape(1, D_MODEL)
        if l == 0:
            z, x = _inproj_first(x_ctx, x_lat, mods[l], g1, w_in_b[l])
        else:
            z, x = _inproj_next(x, y, route, mods[l - 1], mods[l], g1, w_in_b[l])
        conv_args = (conv_w[l], conv_b[l].reshape(1, -1), conv_ln_g[l].reshape(1, -1),
                     conv_ln_b[l].reshape(1, -1))
        yc_c = _conv(z, 0, BATCH, SEQ, *conv_args)
        yc_l = _conv(z, T_CTX // DEC_SEQ, DEC_BATCH, DEC_SEQ, *conv_args)
        yn_c, k_l, v_l = _ctx_attn(z)
        yn_l = _na_attn(z, cache_k, cache_v, na_bias, l)
        gn = ret_gn_g[l].reshape(1, RET_WIDTH)
        yr_c, sf_l, sb_l = _retention(z, lg[l], gn, latent=False)
        yr_l = _retention(z, lg[l], gn, latent=True, layer=l, rope=rope,
                          s0_f=state_ret_f, s0_b=state_ret_b)
        x, h, route, counts = _outproj((yc_c, yc_l), (yn_c, yn_l), (yr_c, yr_l), x, mods[l],
                                       norm2_g[l].reshape(1, D_MODEL), w_out_b[l], w_route[l],
                                       b_route[l].reshape(1, ROUTE_COLS))
        dest, blk_src, blk_e, pad_start, pad_len = _dispatch_tables(route, counts)
        xs = _scatter_rows(h, dest, pad_start, pad_len)
        ys = _moe(xs, w1, w3, w2, l, blk_src, blk_e)
        y = _gather_rows(ys, dest).reshape(T_ALL, 2 * D_MODEL)
        k_list.append(k_l)
        v_list.append(v_l)
        sf_list.append(sf_l)
        sb_list.append(sb_l)
    fg = final_g.reshape(1, D_MODEL)
    y_prompt = _final(x, y, route, mods[DEPTH - 1], fg, 0, NB_CTX).reshape(BATCH, SEQ, D_MODEL)
    y_sample = _final(x, y, route, mods[DEPTH - 1], fg, NB_CTX, NB_LAT).reshape(DEC_BATCH, DEC_SEQ, D_MODEL)
    return (y_prompt, y_sample, jnp.stack(k_list, axis=1), jnp.stack(v_list, axis=1),
            jnp.stack(sf_list, axis=1), jnp.stack(sb_list, axis=1))
```

```python
import functools

import numpy as np
import jax
import jax.numpy as jnp
from jax import lax
from jax.experimental import pallas as pl
from jax.experimental.pallas import tpu as pltpu

D_MODEL = 1024
BATCH = 32
SEQ = 256
DEPTH = 2
DEC_BATCH = 4
DEC_SEQ = 4096
PAST_LEN = 512
GRID_W = 64
GRID_H = DEC_SEQ // GRID_W
CONV_CH = 256
CONV_K = 31
NA_HEADS = 8
NA_DIM = 64
NA_WIDTH = NA_HEADS * NA_DIM
NA_KH = 8
NA_KW = 16
RET_HEADS = 4
RET_DIM = 64
RET_WIDTH = RET_HEADS * RET_DIM
RET_CHUNK = 128
ROPE_BASE = 10000.0
N_GROUPS = 4
EXPERTS_PER_GROUP = 8
N_EXPERTS = N_GROUPS * EXPERTS_PER_GROUP
D_EXPERT = 512
IN_COLS = 2 * CONV_CH + 3 * NA_WIDTH + 4 * RET_WIDTH
EPS = 1e-6
NEG_INF = -1e30

F32 = jnp.float32
BF16 = jnp.bfloat16
HIGHEST = lax.Precision.HIGHEST

T_CTX = BATCH * SEQ
T_LAT = DEC_BATCH * DEC_SEQ
T_ALL = T_CTX + T_LAT
N_COND = 1 + DEC_BATCH
COND_ROWS = 8

TM = 512
NB_CTX = T_CTX // TM
NB_LAT = T_LAT // TM
NB_ALL = NB_CTX + NB_LAT
LAT_BLOCKS_PER_REQ = DEC_SEQ // TM

LANES = 128
SUBLANES = 8
ROUTE_COLS = LANES

COL_CONV = 0
COL_NA_Q = 2 * CONV_CH
COL_NA_K = COL_NA_Q + NA_WIDTH
COL_NA_V = COL_NA_K + NA_WIDTH
COL_RET = COL_NA_V + NA_WIDTH

NA_ROWS = 8
NA_Q = NA_ROWS * GRID_W
NA_KROWS = 2 * NA_ROWS
NA_KEYS = NA_KROWS * GRID_W
NA_RB = GRID_H // NA_ROWS

MOE_BLK = 256
MOE_NBLK = (2 * T_ALL) // MOE_BLK + N_EXPERTS
MOE_ROWS = MOE_NBLK * MOE_BLK

VMEM_LIMIT = 56 * 1024 * 1024


def _cparams(sem):
    return pltpu.CompilerParams(dimension_semantics=sem, vmem_limit_bytes=VMEM_LIMIT)


def _sigmoid(x):
    return 1.0 / (1.0 + jnp.exp(-x))


def _cond_row(i):
    return jnp.where(i < NB_CTX, 0, 1 + (i - NB_CTX) // LAT_BLOCKS_PER_REQ)


ADA_TN = 1536


def _ada_kernel(cv_ref, w_ref, b_ref, o_ref):
    cv = cv_ref[...]
    s = cv * _sigmoid(cv)
    o_ref[...] = jnp.dot(s, w_ref[...], precision=HIGHEST, preferred_element_type=F32) + b_ref[...]


def _ada(cv, w_ada, b_ada):
    n = 6 * D_MODEL
    return pl.pallas_call(
        _ada_kernel,
        grid=(DEPTH, n // ADA_TN),
        in_specs=[
            pl.BlockSpec((COND_ROWS, D_MODEL), lambda l, j: (0, 0)),
            pl.BlockSpec((None, D_MODEL, ADA_TN), lambda l, j: (l, 0, j)),
            pl.BlockSpec((None, 1, ADA_TN), lambda l, j: (l, 0, j)),
        ],
        out_specs=pl.BlockSpec((None, COND_ROWS, ADA_TN), lambda l, j: (l, 0, j)),
        out_shape=jax.ShapeDtypeStruct((DEPTH, COND_ROWS, n), F32),
        compiler_params=_cparams(("arbitrary", "arbitrary")),
        name="ada_mod",
    )(cv, w_ada, b_ada.reshape(DEPTH, 1, n))


IN_TN = 768


def _norm_mod(x, g, shift, scale):
    ms = jnp.mean(x * x, axis=-1, keepdims=True)
    return (x * lax.rsqrt(ms + EPS) * g) * (1.0 + scale) + shift


def _inproj_body(x, mod_ref, g_ref, w_ref, z_ref):
    h = _norm_mod(x, g_ref[...], mod_ref[0:1, :], mod_ref[1:2, :]).astype(BF16)
    for c in range(IN_COLS // IN_TN):
        cols = slice(c * IN_TN, (c + 1) * IN_TN)
        z_ref[:, cols] = jnp.dot(h, w_ref[:, cols], preferred_element_type=F32).astype(BF16)


def _inproj_first_kernel(xc_ref, xl_ref, mod_ref, g_ref, w_ref, z_ref, xo_ref):
    i = pl.program_id(0)
    x = jnp.where(i < NB_CTX, xc_ref[...], xl_ref[...])
    xo_ref[...] = x
    _inproj_body(x, mod_ref, g_ref, w_ref, z_ref)


def _moe_residual(x_ref, y0_ref, y1_ref, r_ref, mod_ref):
    r = r_ref[...]
    return x_ref[...] + mod_ref[5:6, :] * (r[:, 2:3] * y0_ref[...] + r[:, 3:4] * y1_ref[...])


def _inproj_next_kernel(x_ref, y0_ref, y1_ref, r_ref, modp_ref, mod_ref, g_ref, w_ref, z_ref, xo_ref):
    x = _moe_residual(x_ref, y0_ref, y1_ref, r_ref, modp_ref)
    xo_ref[...] = x
    _inproj_body(x, mod_ref, g_ref, w_ref, z_ref)


def _tok_spec(cols):
    return pl.BlockSpec((TM, cols), lambda i: (i, 0))


def _mod_spec():
    return pl.BlockSpec((None, 6, D_MODEL), lambda i: (_cond_row(i), 0, 0))


def _full_spec(shape):
    return pl.BlockSpec(shape, lambda i: (0,) * len(shape))


def _ctx_lat_specs(cols):
    return [pl.BlockSpec((TM, cols), lambda i: (jnp.minimum(i, NB_CTX - 1), 0)),
            pl.BlockSpec((TM, cols), lambda i: (jnp.maximum(i - NB_CTX, 0), 0))]


def _inproj_first(x_ctx, x_lat, mod, g, w_bf16):
    return pl.pallas_call(
        _inproj_first_kernel,
        grid=(NB_ALL,),
        in_specs=_ctx_lat_specs(D_MODEL) + [_mod_spec(), _full_spec((1, D_MODEL)),
                                            _full_spec((D_MODEL, IN_COLS))],
        out_specs=[_tok_spec(IN_COLS), _tok_spec(D_MODEL)],
        out_shape=[jax.ShapeDtypeStruct((T_ALL, IN_COLS), BF16),
                   jax.ShapeDtypeStruct((T_ALL, D_MODEL), F32)],
        compiler_params=_cparams(("arbitrary",)),
        name="inproj_first",
    )(x_ctx, x_lat, mod, g, w_bf16)


def _inproj_next(x, y, route, mod_prev, mod, g, w_bf16):
    return pl.pallas_call(
        _inproj_next_kernel,
        grid=(NB_ALL,),
        in_specs=[_tok_spec(D_MODEL),
                  pl.BlockSpec((TM, D_MODEL), lambda i: (i, 0)),
                  pl.BlockSpec((TM, D_MODEL), lambda i: (i, 1)),
                  _tok_spec(ROUTE_COLS),
                  _mod_spec(), _mod_spec(), _full_spec((1, D_MODEL)),
                  _full_spec((D_MODEL, IN_COLS))],
        out_specs=[_tok_spec(IN_COLS), _tok_spec(D_MODEL)],
        out_shape=[jax.ShapeDtypeStruct((T_ALL, IN_COLS), BF16),
                   jax.ShapeDtypeStruct((T_ALL, D_MODEL), F32)],
        compiler_params=_cparams(("arbitrary",)),
        name="inproj_next",
    )(x, y, y, route, mod_prev, mod, g, w_bf16)


CONV_PAD = 16
CONV_CHUNK = 64


def _conv_kernel(seq, z_ref, w_ref, b_ref, g_ref, be_ref, o_ref, upad_ref):
    zeros = jnp.zeros((CONV_PAD, CONV_CH), F32)
    upad_ref[0:CONV_PAD, :] = zeros
    upad_ref[seq + CONV_PAD:seq + 2 * CONV_PAD, :] = zeros

    def glu(ci, carry):
        base = pl.multiple_of(ci * 256, 256)
        zc = z_ref[pl.ds(base, 256), :].astype(F32)
        upad_ref[pl.ds(base + CONV_PAD, 256), :] = zc[:, :CONV_CH] * _sigmoid(zc[:, CONV_CH:])
        return carry

    lax.fori_loop(0, seq // 256, glu, 0)

    shift = CONV_PAD - CONV_K // 2

    def chunk(ci, carry):
        base = pl.multiple_of(ci * CONV_CHUNK, CONV_CHUNK)
        win = upad_ref[pl.ds(base, CONV_CHUNK + 2 * CONV_PAD), :]
        acc = jnp.zeros((CONV_CHUNK, CONV_CH), F32)
        for k in range(CONV_K):
            acc = acc + w_ref[k:k + 1, :] * win[k + shift:k + shift + CONV_CHUNK, :]
        acc = acc + b_ref[...]
        mu = jnp.mean(acc, axis=-1, keepdims=True)
        d = acc - mu
        var = jnp.mean(d * d, axis=-1, keepdims=True)
        n = d * lax.rsqrt(var + EPS) * g_ref[...] + be_ref[...]
        o_ref[pl.ds(base, CONV_CHUNK), :] = (n * _sigmoid(n)).astype(BF16)
        return carry

    lax.fori_loop(0, seq // CONV_CHUNK, chunk, 0)


def _conv(z, row_block0, nseq, seq, w, b, g, be):
    return pl.pallas_call(
        functools.partial(_conv_kernel, seq),
        grid=(nseq,),
        in_specs=[pl.BlockSpec((seq, 2 * CONV_CH), lambda s: (row_block0 + s, 0)),
                  _full_spec((CONV_K, CONV_CH)), _full_spec((1, CONV_CH)),
                  _full_spec((1, CONV_CH)), _full_spec((1, CONV_CH))],
        out_specs=pl.BlockSpec((seq, CONV_CH), lambda s: (s, 0)),
        out_shape=jax.ShapeDtypeStruct((nseq * seq, CONV_CH), BF16),
        scratch_shapes=[pltpu.VMEM((seq + 2 * CONV_PAD, CONV_CH), F32)],
        compiler_params=_cparams(("arbitrary",)),
        name="conv_seq%d" % seq,
    )(z, w, b, g, be)


def _dot_nt(a, b):
    return lax.dot_general(a, b, (((1,), (1,)), ((), ())), preferred_element_type=F32)


def _ctx_attn_kernel(q_ref, k_ref, v_ref, o_ref, ko_ref, vo_ref):
    scale = NA_DIM ** -0.5
    outs = []
    for h in range(NA_HEADS):
        cols = slice(h * NA_DIM, (h + 1) * NA_DIM)
        qh, kh, vh = q_ref[:, cols], k_ref[:, cols], v_ref[:, cols]
        ko_ref[h] = kh.astype(F32)
        vo_ref[h] = vh.astype(F32)
        s = _dot_nt(qh, kh) * scale
        m = jnp.max(s, axis=-1, keepdims=True)
        p = jnp.exp(s - m)
        den = jnp.sum(p, axis=-1, keepdims=True)
        o = jnp.dot(p.astype(BF16), vh, preferred_element_type=F32)
        outs.append(o / den)
    o_ref[...] = jnp.concatenate(outs, axis=-1).astype(BF16)


def _ctx_attn(z):
    qb, kb, vb = COL_NA_Q // NA_WIDTH, COL_NA_K // NA_WIDTH, COL_NA_V // NA_WIDTH
    head_shape = jax.ShapeDtypeStruct((BATCH, NA_HEADS, SEQ, NA_DIM), F32)
    head_spec = pl.BlockSpec((None, NA_HEADS, SEQ, NA_DIM), lambda b: (b, 0, 0, 0))
    return pl.pallas_call(
        _ctx_attn_kernel,
        grid=(BATCH,),
        in_specs=[pl.BlockSpec((SEQ, NA_WIDTH), lambda b: (b, qb)),
                  pl.BlockSpec((SEQ, NA_WIDTH), lambda b: (b, kb)),
                  pl.BlockSpec((SEQ, NA_WIDTH), lambda b: (b, vb))],
        out_specs=[pl.BlockSpec((SEQ, NA_WIDTH), lambda b: (b, 0)), head_spec, head_spec],
        out_shape=[jax.ShapeDtypeStruct((T_CTX, NA_WIDTH), BF16), head_shape, head_shape],
        compiler_params=_cparams(("arbitrary",)),
        name="ctx_attn",
    )(z, z, z)


NA_KINDS = (0, NA_ROWS, GRID_H - NA_ROWS)
N_DR = 2 * NA_KH - 1
N_DC = 2 * NA_KW - 1


def _na_row_offset(r0, i, j):
    ks = min(max(r0 - NA_KH // 2, 0), GRID_H - NA_KROWS)
    r, kr = r0 + i, ks + j
    rs = min(max(r - NA_KH // 2, 0), GRID_H - NA_KH)
    return kr - r + NA_KH - 1 if rs <= kr < rs + NA_KH else None


def _na_bias_kernel(rpb_ref, o_ref):
    lh = pl.program_id(0)
    shape = (GRID_W, 2 * GRID_W)
    qc = lax.broadcasted_iota(jnp.int32, shape, 0)
    lane = lax.broadcasted_iota(jnp.int32, shape, 1)
    kc = lane % GRID_W
    dc = jnp.clip(kc - qc, -(NA_KW - 1), NA_KW - 1) + NA_KW - 1
    cs = jnp.clip(qc - NA_KW // 2, 0, GRID_W - NA_KW)
    col_ok = (kc >= cs) & (kc < cs + NA_KW)
    neg = jnp.full(shape, NEG_INF, F32)
    tiles = []
    for dr in range(N_DR):
        base = (lh * N_DR + dr) * N_DC
        val = jnp.zeros(shape, F32)
        for d in range(N_DC):
            val = jnp.where(dc == d, rpb_ref[base + d], val)
        tiles.append(jnp.where(col_ok, val, neg))
    left = lane < GRID_W
    for kind, r0 in enumerate(NA_KINDS):
        for i in range(NA_ROWS):
            for jp in range(NA_KROWS // 2):
                dl, dr_ = _na_row_offset(r0, i, 2 * jp), _na_row_offset(r0, i, 2 * jp + 1)
                tl = neg if dl is None else tiles[dl]
                tr = neg if dr_ is None else tiles[dr_]
                o_ref[kind, i * GRID_W:(i + 1) * GRID_W, jp * 2 * GRID_W:(jp + 1) * 2 * GRID_W] = (
                    jnp.where(left, tl, tr))


def _na_bias_tables(rpb):
    return pl.pallas_call(
        _na_bias_kernel,
        grid=(DEPTH * NA_HEADS,),
        in_specs=[pl.BlockSpec(memory_space=pltpu.SMEM)],
        out_specs=pl.BlockSpec((None, len(NA_KINDS), NA_Q, NA_KEYS), lambda i: (i, 0, 0, 0)),
        out_shape=jax.ShapeDtypeStruct((DEPTH * NA_HEADS, len(NA_KINDS), NA_Q, NA_KEYS), F32),
        compiler_params=_cparams(("arbitrary",)),
        name="nbr_bias",
    )(rpb.reshape(-1))


def _na_kernel(q_ref, k_ref, v_ref, kc_ref, vc_ref, bias_ref, o_ref):
    rb = pl.program_id(2)
    scale = NA_DIM ** -0.5
    ks = jnp.clip(rb * NA_ROWS - NA_KH // 2, 0, GRID_H - NA_KROWS)
    start = pl.multiple_of(ks * GRID_W, GRID_W)
    q = q_ref[...]
    kl = k_ref[pl.ds(start, NA_KEYS), :]
    vl = v_ref[pl.ds(start, NA_KEYS), :]
    outs = []
    for hh in range(2):
        cols = slice(hh * NA_DIM, (hh + 1) * NA_DIM)
        qh = q[:, cols]
        s_loc = _dot_nt(qh, kl[:, cols]) * scale + bias_ref[hh]
        s_ctx = _dot_nt(qh, kc_ref[hh].astype(BF16)) * scale
        m = jnp.maximum(jnp.max(s_loc, axis=-1, keepdims=True), jnp.max(s_ctx, axis=-1, keepdims=True))
        p_loc = jnp.exp(s_loc - m)
        p_ctx = jnp.exp(s_ctx - m)
        den = jnp.sum(p_loc, axis=-1, keepdims=True) + jnp.sum(p_ctx, axis=-1, keepdims=True)
        o = (jnp.dot(p_loc.astype(BF16), vl[:, cols], preferred_element_type=F32)
             + jnp.dot(p_ctx.astype(BF16), vc_ref[hh].astype(BF16), preferred_element_type=F32))
        outs.append(o / den)
    o_ref[...] = jnp.concatenate(outs, axis=-1).astype(BF16)


def _na_attn(z, cache_k, cache_v, bias, layer):
    lat_q0 = T_CTX // NA_Q
    lat_s0 = T_CTX // DEC_SEQ
    qc, kc, vc = COL_NA_Q // LANES, COL_NA_K // LANES, COL_NA_V // LANES

    def kind(rb):
        return jnp.where(rb == 0, 0, jnp.where(rb == NA_RB - 1, 2, 1))

    ctx_spec = pl.BlockSpec((None, None, 2, PAST_LEN, NA_DIM), lambda b, hp, rb: (b, layer, hp, 0, 0))
    return pl.pallas_call(
        _na_kernel,
        grid=(DEC_BATCH, NA_HEADS // 2, NA_RB),
        in_specs=[pl.BlockSpec((NA_Q, LANES), lambda b, hp, rb: (lat_q0 + b * NA_RB + rb, qc + hp)),
                  pl.BlockSpec((DEC_SEQ, LANES), lambda b, hp, rb: (lat_s0 + b, kc + hp)),
                  pl.BlockSpec((DEC_SEQ, LANES), lambda b, hp, rb: (lat_s0 + b, vc + hp)),
                  ctx_spec, ctx_spec,
                  pl.BlockSpec((2, None, NA_Q, NA_KEYS),
                               lambda b, hp, rb: (layer * (NA_HEADS // 2) + hp, kind(rb), 0, 0))],
        out_specs=pl.BlockSpec((NA_Q, LANES), lambda b, hp, rb: (b * NA_RB + rb, hp)),
        out_shape=jax.ShapeDtypeStruct((T_LAT, NA_WIDTH), BF16),
        compiler_params=_cparams(("arbitrary", "arbitrary", "arbitrary")),
        name="nbr_attn",
    )(z, z, z, cache_k, cache_v, bias)


def _rope_tables():
    n_freq = RET_DIM // 4
    t = np.arange(DEC_SEQ)
    inv = jnp.asarray(ROPE_BASE, F32) ** (-jnp.arange(n_freq, dtype=F32) / n_freq)
    ang_r = jnp.asarray(t // GRID_W, F32)[:, None] * inv[None, :]
    ang_c = jnp.asarray(t % GRID_W, F32)[:, None] * inv[None, :]
    cos = jnp.concatenate([jnp.cos(ang_r)] * 2 + [jnp.cos(ang_c)] * 2, axis=-1)
    sin = jnp.concatenate([-jnp.sin(ang_r), jnp.sin(ang_r), -jnp.sin(ang_c), jnp.sin(ang_c)], axis=-1)
    return jnp.tile(cos, (1, RET_HEADS)), jnp.tile(sin, (1, RET_HEADS))


def _ret_kernel(seq, latent, *refs):
    if latent:
        (lg_ref, z_ref, gn_ref, cos_ref, sin_ref, s0f_ref, s0b_ref, y_ref,
         q_s, k_s, kv_s, st_s) = refs
    else:
        lg_ref, z_ref, gn_ref, y_ref, sf_ref, sb_ref, q_s, k_s, kv_s, st_s = refs
    nc = seq // RET_CHUNK
    ch, hd = RET_CHUNK, RET_DIM
    half = RET_DIM // 4

    row = lax.broadcasted_iota(jnp.int32, (ch, ch), 0).astype(F32)
    col = lax.broadcasted_iota(jnp.int32, (ch, ch), 1).astype(F32)
    pos = lax.broadcasted_iota(jnp.int32, (ch, hd), 0).astype(F32)
    decay, q_dec, k_dec, c_dec_f, c_dec_b = [], [], [], [], []
    for h in range(RET_HEADS):
        lf, lb = lg_ref[0, h], lg_ref[1, h]
        d_f = jnp.where(row >= col, jnp.exp(jnp.maximum(row - col, 0.0) * lf), 0.0)
        d_b = jnp.where(col >= row, jnp.exp(jnp.maximum(col - row, 0.0) * lb), 0.0)
        decay.append(d_f + d_b)
        q_dec.append(jnp.concatenate([jnp.exp((pos + 1.0) * lf), jnp.exp((ch - pos) * lb)], axis=-1))
        k_dec.append(jnp.concatenate([jnp.exp((ch - 1.0 - pos) * lf), jnp.exp(pos * lb)], axis=-1))
        c_dec_f.append(jnp.exp(jnp.zeros((hd, hd), F32) + ch * lf))
        c_dec_b.append(jnp.exp(jnp.zeros((hd, hd), F32) + ch * lb))

    if latent:
        lane = lax.broadcasted_iota(jnp.int32, (ch, RET_WIDTH), 1)
        first_half = (lane % (2 * half)) < half

    def rope(x, base):
        if not latent:
            return x
        swapped = jnp.where(first_half, pltpu.roll(x, RET_WIDTH - half, 1), pltpu.roll(x, half, 1))
        return x * cos_ref[pl.ds(base, ch), :] + swapped * sin_ref[pl.ds(base, ch), :]

    def pass1(n, carry):
        base = pl.multiple_of(n * ch, ch)
        zc = z_ref[pl.ds(base, ch), :]
        q = rope(zc[:, 0:RET_WIDTH].astype(F32), base)
        k = rope(zc[:, RET_WIDTH:2 * RET_WIDTH].astype(F32) * (RET_DIM ** -0.5), base)
        q_s[pl.ds(base, ch), :] = q.astype(BF16)
        k_s[pl.ds(base, ch), :] = k.astype(BF16)
        v = zc[:, 2 * RET_WIDTH:3 * RET_WIDTH]
        for h in range(RET_HEADS):
            cols = slice(h * hd, (h + 1) * hd)
            kh = k[:, cols]
            k2 = (jnp.concatenate([kh, kh], axis=-1) * k_dec[h]).astype(BF16)
            kv_s[n, h] = lax.dot_general(k2, v[:, cols], (((0,), (0,)), ((), ())),
                                         preferred_element_type=F32)
        return carry

    lax.fori_loop(0, nc, pass1, 0)

    for h in range(RET_HEADS):
        if latent:
            s_f, s_b = s0f_ref[h], s0b_ref[h]
        else:
            s_f = s_b = jnp.zeros((hd, hd), F32)

        def fwd(n, s, h=h):
            st_s[n, h, 0:hd, :] = s
            return c_dec_f[h] * s + kv_s[n, h, 0:hd, :]

        def bwd(i, s, h=h):
            n = nc - 1 - i
            st_s[n, h, hd:2 * hd, :] = s
            return c_dec_b[h] * s + kv_s[n, h, hd:2 * hd, :]

        s_f = lax.fori_loop(0, nc, fwd, s_f)
        s_b = lax.fori_loop(0, nc, bwd, s_b)
        if not latent:
            sf_ref[h] = s_f
            sb_ref[h] = s_b

    def pass3(n, carry):
        base = pl.multiple_of(n * ch, ch)
        zc = z_ref[pl.ds(base, ch), :]
        q = q_s[pl.ds(base, ch), :]
        k = k_s[pl.ds(base, ch), :]
        v = zc[:, 2 * RET_WIDTH:3 * RET_WIDTH]
        gate = zc[:, 3 * RET_WIDTH:4 * RET_WIDTH].astype(F32)
        outs = []
        for h in range(RET_HEADS):
            cols = slice(h * hd, (h + 1) * hd)
            qh = q[:, cols]
            s = _dot_nt(qh, k[:, cols]) * decay[h]
            o = jnp.dot(s.astype(BF16), v[:, cols], preferred_element_type=F32)
            qf = qh.astype(F32)
            q2 = (jnp.concatenate([qf, qf], axis=-1) * q_dec[h]).astype(BF16)
            o = o + jnp.dot(q2, st_s[n, h].astype(BF16), preferred_element_type=F32)
            mu = jnp.mean(o, axis=-1, keepdims=True)
            d = o - mu
            var = jnp.mean(d * d, axis=-1, keepdims=True)
            outs.append(d * lax.rsqrt(var + EPS))
        nrm = jnp.concatenate(outs, axis=-1)
        y_ref[pl.ds(base, ch), :] = (nrm * gn_ref[...] * (gate * _sigmoid(gate))).astype(BF16)
        return carry

    lax.fori_loop(0, nc, pass3, 0)


def _retention(z, lg, gn_g, latent, layer=None, rope=None, s0_f=None, s0_b=None):
    seq = DEC_SEQ if latent else SEQ
    nseq = DEC_BATCH if latent else BATCH
    nc = seq // RET_CHUNK
    row0 = (T_CTX // DEC_SEQ) if latent else 0
    cb = COL_RET // (4 * RET_WIDTH)
    in_specs = [pl.BlockSpec(memory_space=pltpu.SMEM),
                pl.BlockSpec((seq, 4 * RET_WIDTH), lambda s: (row0 + s, cb)),
                _full_spec((1, RET_WIDTH))]
    args = [lg, z, gn_g]
    state_shape = jax.ShapeDtypeStruct((nseq, RET_HEADS, RET_DIM, RET_DIM), F32)
    y_spec = pl.BlockSpec((seq, RET_WIDTH), lambda s: (s, 0))
    y_shape = jax.ShapeDtypeStruct((nseq * seq, RET_WIDTH), BF16)
    if latent:
        st_spec = pl.BlockSpec((None, None, RET_HEADS, RET_DIM, RET_DIM), lambda s: (s, layer, 0, 0, 0))
        in_specs += [_full_spec((seq, RET_WIDTH)), _full_spec((seq, RET_WIDTH)), st_spec, st_spec]
        args += [rope[0], rope[1], s0_f, s0_b]
        out_specs, out_shape = y_spec, y_shape
    else:
        so_spec = pl.BlockSpec((None, RET_HEADS, RET_DIM, RET_DIM), lambda s: (s, 0, 0, 0))
        out_specs, out_shape = [y_spec, so_spec, so_spec], [y_shape, state_shape, state_shape]
    return pl.pallas_call(
        functools.partial(_ret_kernel, seq, latent),
        grid=(nseq,),
        in_specs=in_specs,
        out_specs=out_specs,
        out_shape=out_shape,
        scratch_shapes=[pltpu.VMEM((seq, RET_WIDTH), BF16), pltpu.VMEM((seq, RET_WIDTH), BF16),
                        pltpu.VMEM((nc, RET_HEADS, 2 * RET_DIM, RET_DIM), F32),
                        pltpu.VMEM((nc, RET_HEADS, 2 * RET_DIM, RET_DIM), F32)],
        compiler_params=_cparams(("arbitrary",)),
        name="retention_lat" if latent else "retention_ctx",
    )(*args)


def _route(logits, count_ref):
    lane = lax.broadcasted_iota(jnp.int32, logits.shape, 1)
    lane_f = lane.astype(F32)
    big = float(ROUTE_COLS)
    neg = -jnp.inf
    is_grp = lane < N_GROUPS
    gl = jnp.where(is_grp, logits, neg)
    gmax = jnp.max(gl, axis=-1, keepdims=True)
    grp = jnp.min(jnp.where(gl == gmax, lane_f, big), axis=-1, keepdims=True)
    p_grp = 1.0 / jnp.sum(jnp.exp(gl - gmax), axis=-1, keepdims=True)
    e_f = lane_f - N_GROUPS
    lo = grp * EXPERTS_PER_GROUP
    in_grp = (e_f >= lo) & (e_f < lo + EXPERTS_PER_GROUP)
    el = jnp.where(in_grp, logits, neg)
    m1 = jnp.max(el, axis=-1, keepdims=True)
    i1 = jnp.min(jnp.where(el == m1, lane_f, big), axis=-1, keepdims=True)
    el2 = jnp.where(lane_f == i1, neg, el)
    m2 = jnp.max(el2, axis=-1, keepdims=True)
    i2 = jnp.min(jnp.where(el2 == m2, lane_f, big), axis=-1, keepdims=True)
    t = jnp.exp(m2 - m1)
    g1 = p_grp / (1.0 + t)
    g2 = p_grp * t / (1.0 + t)
    rows = logits.shape[0]
    oh1, oh2 = lane_f == i1, lane_f == i2
    oh = jnp.where(oh1 | oh2, 1.0, 0.0)
    tri = (lax.broadcasted_iota(jnp.int32, (rows, rows), 0)
           > lax.broadcasted_iota(jnp.int32, (rows, rows), 1))
    before = count_ref[...] + jnp.dot(jnp.where(tri, 1.0, 0.0).astype(BF16), oh.astype(BF16),
                                      preferred_element_type=F32)
    r1 = jnp.sum(jnp.where(oh1, before, 0.0), axis=-1, keepdims=True)
    r2 = jnp.sum(jnp.where(oh2, before, 0.0), axis=-1, keepdims=True)
    count_ref[...] = count_ref[...] + jnp.sum(oh, axis=0, keepdims=True)
    out = jnp.zeros(logits.shape, F32)
    for k, val in enumerate((i1 - N_GROUPS, i2 - N_GROUPS, g1, g2, r1, r2)):
        out = jnp.where(lane == k, val, out)
    return out


def _outproj_kernel(ycc, ycl, ync, ynl, yrc, yrl, x_ref, mod_ref, g_ref, w_ref, wr_ref, br_ref,
                    xo_ref, h_ref, r_ref, cnt_ref):
    is_ctx = pl.program_id(0) < NB_CTX

    @pl.when(pl.program_id(0) == 0)
    def _():
        cnt_ref[...] = jnp.zeros_like(cnt_ref)

    yc = jnp.where(is_ctx, ycc[...], ycl[...])
    yn = jnp.where(is_ctx, ync[...], ynl[...])
    yr = jnp.where(is_ctx, yrc[...], yrl[...])
    y = (jnp.dot(yc, w_ref[0:CONV_CH, :], preferred_element_type=F32)
         + jnp.dot(yn, w_ref[CONV_CH:CONV_CH + NA_WIDTH, :], preferred_element_type=F32)
         + jnp.dot(yr, w_ref[CONV_CH + NA_WIDTH:, :], preferred_element_type=F32))
    x = x_ref[...] + mod_ref[2:3, :] * y
    xo_ref[...] = x
    h = _norm_mod(x, g_ref[...], mod_ref[3:4, :], mod_ref[4:5, :])
    h_ref[...] = h
    logits = jnp.dot(h, wr_ref[...], precision=HIGHEST, preferred_element_type=F32) + br_ref[...]
    r_ref[...] = _route(logits, cnt_ref)


def _outproj(y_conv, y_na, y_ret, x, mod, g, w_bf16, w_route, b_route):
    return pl.pallas_call(
        _outproj_kernel,
        grid=(NB_ALL,),
        in_specs=(_ctx_lat_specs(CONV_CH) + _ctx_lat_specs(NA_WIDTH) + _ctx_lat_specs(RET_WIDTH)
                  + [_tok_spec(D_MODEL), _mod_spec(), _full_spec((1, D_MODEL)),
                     _full_spec((D_MODEL, D_MODEL)), _full_spec((D_MODEL, ROUTE_COLS)),
                     _full_spec((1, ROUTE_COLS))]),
        out_specs=[_tok_spec(D_MODEL), _tok_spec(D_MODEL), _tok_spec(ROUTE_COLS),
                   _full_spec((1, ROUTE_COLS))],
        out_shape=[jax.ShapeDtypeStruct((T_ALL, D_MODEL), F32),
                   jax.ShapeDtypeStruct((T_ALL, D_MODEL), F32),
                   jax.ShapeDtypeStruct((T_ALL, ROUTE_COLS), F32),
                   jax.ShapeDtypeStruct((1, ROUTE_COLS), F32)],
        compiler_params=_cparams(("arbitrary",)),
        name="outproj_route",
    )(y_conv[0], y_conv[1], y_na[0], y_na[1], y_ret[0], y_ret[1], x, mod, g, w_bf16, w_route, b_route)


def _dispatch_tables(route, counts):
    counts = counts[0, N_GROUPS:N_GROUPS + N_EXPERTS].astype(jnp.int32)
    experts = jnp.arange(N_EXPERTS, dtype=jnp.int32)
    nblk = (counts + MOE_BLK - 1) // MOE_BLK
    blk_end = jnp.cumsum(nblk)
    blk_start = blk_end - nblk
    row_start = blk_start * MOE_BLK
    eid = route[:, 0:2].astype(jnp.int32)
    rank = route[:, 4:6].astype(jnp.int32)
    start_of = jnp.sum(jnp.where(eid[:, :, None] == experts[None, None, :], row_start[None, None, :], 0), axis=-1)
    dest = (start_of + rank).reshape(-1)
    blk = jnp.arange(MOE_NBLK, dtype=jnp.int32)
    n_active = blk_end[-1]
    blk_src = jnp.minimum(blk, n_active - 1)
    blk_e = jnp.minimum(jnp.sum((blk_end[None, :] <= blk_src[:, None]).astype(jnp.int32), axis=-1),
                        N_EXPERTS - 1)
    pad_start = row_start + counts
    pad_len = nblk * MOE_BLK - counts
    return dest, blk_src, blk_e, pad_start, pad_len


PAD_BITS = MOE_BLK.bit_length() - 1


def _scatter_rows_kernel(dest, pad_start, pad_len, h_ref, xs_hbm, zeros, sem, zsem):
    @pl.when(pl.program_id(0) == 0)
    def _():
        zeros[...] = jnp.zeros_like(zeros)

        def pad_copies(e):
            n_pad, start = pad_len[e], pad_start[e]
            head = n_pad & (SUBLANES - 1)
            for r in range(SUBLANES - 1):
                yield r < head, pltpu.make_async_copy(
                    zeros.at[pl.ds(0, 1)], xs_hbm.at[pl.ds(start + r, 1)], zsem)
            for bit in range(SUBLANES.bit_length() - 1, PAD_BITS):
                n = 1 << bit
                off = pl.multiple_of(start + head + ((n_pad - head) & (n - 1)), SUBLANES)
                yield (n_pad >> bit) & 1 == 1, pltpu.make_async_copy(
                    zeros.at[pl.ds(0, n)], xs_hbm.at[pl.ds(off, n)], zsem)

        def fill(e, carry):
            for cond, copy in pad_copies(e):
                pl.when(cond)(copy.start)
            return carry

        def drain(e, carry):
            for cond, copy in pad_copies(e):
                pl.when(cond)(copy.wait)
            return carry

        lax.fori_loop(0, N_EXPERTS, fill, 0)
        lax.fori_loop(0, N_EXPERTS, drain, 0)

        n_z = zeros.shape[0]
        used = pad_start[N_EXPERTS - 1] + pad_len[N_EXPERTS - 1]

        def tail_copy(k):
            off = pl.multiple_of(used + k * n_z, n_z)
            return pltpu.make_async_copy(zeros, xs_hbm.at[pl.ds(off, n_z)], zsem)

        n_tail = (MOE_ROWS - used) // n_z
        lax.fori_loop(0, n_tail, lambda k, c: (tail_copy(k).start(), c)[1], 0)
        lax.fori_loop(0, n_tail, lambda k, c: (tail_copy(k).wait(), c)[1], 0)

    base = 2 * TM * pl.program_id(0)

    def issue(t, carry):
        for k in range(2):
            pltpu.make_async_copy(h_ref.at[pl.ds(t, 1)], xs_hbm.at[pl.ds(dest[base + 2 * t + k], 1)],
                                  sem).start()
        return carry

    lax.fori_loop(0, TM, issue, 0)
    for k in range(2):
        pltpu.make_async_copy(h_ref, xs_hbm.at[pl.ds(0, TM)], sem).wait()


def _scatter_rows(h, dest, pad_start, pad_len):
    grid_spec = pltpu.PrefetchScalarGridSpec(
        num_scalar_prefetch=3,
        grid=(NB_ALL,),
        in_specs=[pl.BlockSpec((TM, D_MODEL), lambda i, d, ps, pn: (i, 0))],
        out_specs=pl.BlockSpec(memory_space=pl.ANY),
        scratch_shapes=[pltpu.VMEM((MOE_BLK // 2, D_MODEL), F32),
                        pltpu.SemaphoreType.DMA, pltpu.SemaphoreType.DMA])
    return pl.pallas_call(
        _scatter_rows_kernel,
        grid_spec=grid_spec,
        out_shape=jax.ShapeDtypeStruct((MOE_ROWS, D_MODEL), F32),
        compiler_params=_cparams(("arbitrary",)),
        name="moe_dispatch",
    )(dest, pad_start, pad_len, h)


def _gather_rows_kernel(dest, ys_hbm, y_ref, sem):
    base = 2 * TM * pl.program_id(0)

    def issue(t, carry):
        for k in range(2):
            pltpu.make_async_copy(ys_hbm.at[pl.ds(dest[base + 2 * t + k], 1)],
                                  y_ref.at[pl.ds(t, 1), pl.ds(k * D_MODEL, D_MODEL)], sem).start()
        return carry

    lax.fori_loop(0, TM, issue, 0)
    for k in range(2):
        pltpu.make_async_copy(ys_hbm.at[pl.ds(0, TM)], y_ref.at[:, pl.ds(k * D_MODEL, D_MODEL)], sem).wait()


def _gather_rows(ys, dest):
    grid_spec = pltpu.PrefetchScalarGridSpec(
        num_scalar_prefetch=1,
        grid=(NB_ALL,),
        in_specs=[pl.BlockSpec(memory_space=pl.ANY)],
        out_specs=pl.BlockSpec((TM, 2 * D_MODEL), lambda i, d: (i, 0)),
        scratch_shapes=[pltpu.SemaphoreType.DMA])
    return pl.pallas_call(
        _gather_rows_kernel,
        grid_spec=grid_spec,
        out_shape=jax.ShapeDtypeStruct((T_ALL, 2 * D_MODEL), F32),
        compiler_params=_cparams(("arbitrary",)),
        name="moe_combine",
    )(dest, ys)


def _moe_kernel(blk_src, blk_e, xs_ref, w1_ref, w3_ref, w2_ref, ys_ref, w1b, w3b, w2b):
    i = pl.program_id(0)

    @pl.when(blk_src[i] == i)
    def _():
        @pl.when((i == 0) | (blk_e[i] != blk_e[jnp.maximum(i - 1, 0)]))
        def _():
            w1b[...] = w1_ref[...].astype(BF16)
            w3b[...] = w3_ref[...].astype(BF16)
            w2b[...] = w2_ref[...].astype(BF16)

        xb = xs_ref[...].astype(BF16)
        a = jnp.dot(xb, w1b[...], preferred_element_type=F32)
        b = jnp.dot(xb, w3b[...], preferred_element_type=F32)
        mid = (a * _sigmoid(a) * b).astype(BF16)
        ys_ref[...] = jnp.dot(mid, w2b[...], preferred_element_type=F32)

    @pl.when(blk_src[i] != i)
    def _():
        ys_ref[...] = jnp.zeros_like(ys_ref)


def _moe(xs, w1, w3, w2, layer, blk_src, blk_e):
    def w_spec(rows, cols):
        return pl.BlockSpec((None, None, rows, cols), lambda i, bs, be: (layer, be[i], 0, 0))

    row_spec = pl.BlockSpec((MOE_BLK, D_MODEL), lambda i, bs, be: (bs[i], 0))
    grid_spec = pltpu.PrefetchScalarGridSpec(
        num_scalar_prefetch=2,
        grid=(MOE_NBLK,),
        in_specs=[row_spec, w_spec(D_MODEL, D_EXPERT), w_spec(D_MODEL, D_EXPERT),
                  w_spec(D_EXPERT, D_MODEL)],
        out_specs=pl.BlockSpec((MOE_BLK, D_MODEL), lambda i, bs, be: (i, 0)),
        scratch_shapes=[pltpu.VMEM((D_MODEL, D_EXPERT), BF16), pltpu.VMEM((D_MODEL, D_EXPERT), BF16),
                        pltpu.VMEM((D_EXPERT, D_MODEL), BF16)])
    return pl.pallas_call(
        _moe_kernel,
        grid_spec=grid_spec,
        out_shape=jax.ShapeDtypeStruct((MOE_ROWS, D_MODEL), F32),
        compiler_params=_cparams(("arbitrary",)),
        name="moe_experts",
    )(blk_src, blk_e, xs, w1, w3, w2)


def _final_kernel(x_ref, y0_ref, y1_ref, r_ref, mod_ref, g_ref, o_ref):
    x = _moe_residual(x_ref, y0_ref, y1_ref, r_ref, mod_ref)
    ms = jnp.mean(x * x, axis=-1, keepdims=True)
    o_ref[...] = x * lax.rsqrt(ms + EPS) * g_ref[...]


def _final(x, y, route, mod, g, block0, nblocks):
    return pl.pallas_call(
        _final_kernel,
        grid=(nblocks,),
        in_specs=[pl.BlockSpec((TM, D_MODEL), lambda i: (block0 + i, 0)),
                  pl.BlockSpec((TM, D_MODEL), lambda i: (block0 + i, 0)),
                  pl.BlockSpec((TM, D_MODEL), lambda i: (block0 + i, 1)),
                  pl.BlockSpec((TM, ROUTE_COLS), lambda i: (block0 + i, 0)),
                  pl.BlockSpec((None, 6, D_MODEL), lambda i: (_cond_row(block0 + i), 0, 0)),
                  _full_spec((1, D_MODEL))],
        out_specs=_tok_spec(D_MODEL),
        out_shape=jax.ShapeDtypeStruct((nblocks * TM, D_MODEL), F32),
        compiler_params=_cparams(("arbitrary",)),
        name="final_norm",
    )(x, y, y, route, mod, g)


def kernel(x_prompt, x_sample, c, cache_k, cache_v, state_ret_f, state_ret_b, c_ctx, w_ada, b_ada, norm1_g, norm2_g, w_in, w_out, conv_w, conv_b, conv_ln_g, conv_ln_b, na_rpb, ret_lg_f, ret_lg_b, ret_gn_g, w_route_g, b_route_g, w_route_e, b_route_e, w1, w3, w2, final_g):
    cv = jnp.zeros((COND_ROWS, D_MODEL), F32).at[0].set(c_ctx).at[1:N_COND].set(c)
    mods = _ada(cv, w_ada, b_ada).reshape(DEPTH, COND_ROWS, 6, D_MODEL)
    w_in_b = w_in.astype(BF16)
    w_out_b = w_out.astype(BF16)
    pad = ROUTE_COLS - N_GROUPS - N_EXPERTS
    w_route = jnp.pad(jnp.concatenate([w_route_g, w_route_e], axis=-1), ((0, 0), (0, 0), (0, pad)))
    b_route = jnp.pad(jnp.concatenate([b_route_g, b_route_e], axis=-1), ((0, 0), (0, pad)))
    na_bias = _na_bias_tables(na_rpb)
    rope = _rope_tables()
    lg = jnp.stack([ret_lg_f, ret_lg_b], axis=1)

    x_ctx = x_prompt.reshape(T_CTX, D_MODEL)
    x_lat = x_sample.reshape(T_LAT, D_MODEL)
    x = y = route = None
    k_list, v_list, sf_list, sb_list = [], [], [], []
    for l in range(DEPTH):
        g1 = norm1_g[l].reshape(1, D_MODEL)
        if l == 0:
            z, x = _inproj_first(x_ctx, x_lat, mods[l], g1, w_in_b[l])
        else:
            z, x = _inproj_next(x, y, route, mods[l - 1], mods[l], g1, w_in_b[l])
        conv_args = (conv_w[l], conv_b[l].reshape(1, -1), conv_ln_g[l].reshape(1, -1),
                     conv_ln_b[l].reshape(1, -1))
        yc_c = _conv(z, 0, BATCH, SEQ, *conv_args)
        yc_l = _conv(z, T_CTX // DEC_SEQ, DEC_BATCH, DEC_SEQ, *conv_args)
        yn_c, k_l, v_l = _ctx_attn(z)
        yn_l = _na_attn(z, cache_k, cache_v, na_bias, l)
        gn = ret_gn_g[l].reshape(1, RET_WIDTH)
        yr_c, sf_l, sb_l = _retention(z, lg[l], gn, latent=False)
        yr_l = _retention(z, lg[l], gn, latent=True, layer=l, rope=rope,
                          s0_f=state_ret_f, s0_b=state_ret_b)
        x, h, route, counts = _outproj((yc_c, yc_l), (yn_c, yn_l), (yr_c, yr_l), x, mods[l],
                                       norm2_g[l].reshape(1, D_MODEL), w_out_b[l], w_route[l],
                                       b_route[l].reshape(1, ROUTE_COLS))
        dest, blk_src, blk_e, pad_start, pad_len = _dispatch_tables(route, counts)
        xs = _scatter_rows(h, dest, pad_start, pad_len)
        ys = _moe(xs, w1, w3, w2, l, blk_src, blk_e)
        y = _gather_rows(ys, dest)
        k_list.append(k_l)
        v_list.append(v_l)
        sf_list.append(sf_l)
        sb_list.append(sb_l)
    fg = final_g.reshape(1, D_MODEL)
    y_prompt = _final(x, y, route, mods[DEPTH - 1], fg, 0, NB_CTX).reshape(BATCH, SEQ, D_MODEL)
    y_sample = _final(x, y, route, mods[DEPTH - 1], fg, NB_CTX, NB_LAT).reshape(DEC_BATCH, DEC_SEQ, D_MODEL)
    return (y_prompt, y_sample, jnp.stack(k_list, axis=1), jnp.stack(v_list, axis=1),
            jnp.stack(sf_list, axis=1), jnp.stack(sb_list, axis=1))
```

```python
import functools

import numpy as np
import jax
import jax.numpy as jnp
from jax import lax
from jax.experimental import pallas as pl
from jax.experimental.pallas import tpu as pltpu

D_MODEL = 1024
BATCH = 32
SEQ = 256
DEPTH = 2
DEC_BATCH = 4
DEC_SEQ = 4096
PAST_LEN = 512
GRID_W = 64
GRID_H = DEC_SEQ // GRID_W
CONV_CH = 256
CONV_K = 31
NA_HEADS = 8
NA_DIM = 64
NA_WIDTH = NA_HEADS * NA_DIM
NA_KH = 8
NA_KW = 16
RET_HEADS = 4
RET_DIM = 64
RET_WIDTH = RET_HEADS * RET_DIM
RET_CHUNK = 128
ROPE_BASE = 10000.0
N_GROUPS = 4
EXPERTS_PER_GROUP = 8
N_EXPERTS = N_GROUPS * EXPERTS_PER_GROUP
D_EXPERT = 512
IN_COLS = 2 * CONV_CH + 3 * NA_WIDTH + 4 * RET_WIDTH
EPS = 1e-6
NEG_INF = -1e30

F32 = jnp.float32
BF16 = jnp.bfloat16
HIGHEST = lax.Precision.HIGHEST

T_CTX = BATCH * SEQ
T_LAT = DEC_BATCH * DEC_SEQ
T_ALL = T_CTX + T_LAT
N_COND = 1 + DEC_BATCH
COND_ROWS = 8

TM = 512
NB_CTX = T_CTX // TM
NB_LAT = T_LAT // TM
NB_ALL = NB_CTX + NB_LAT
LAT_BLOCKS_PER_REQ = DEC_SEQ // TM

LANES = 128
SUBLANES = 8
ROUTE_COLS = LANES

COL_CONV = 0
COL_NA_Q = 2 * CONV_CH
COL_NA_K = COL_NA_Q + NA_WIDTH
COL_NA_V = COL_NA_K + NA_WIDTH
COL_RET = COL_NA_V + NA_WIDTH

NA_ROWS = 8
NA_Q = NA_ROWS * GRID_W
NA_KROWS = 2 * NA_ROWS
NA_KEYS = NA_KROWS * GRID_W
NA_RB = GRID_H // NA_ROWS

MOE_BLK = 256
MOE_NBLK = (2 * T_ALL) // MOE_BLK + N_EXPERTS
MOE_ROWS = MOE_NBLK * MOE_BLK

VMEM_LIMIT = 56 * 1024 * 1024


def _cparams(sem):
    return pltpu.CompilerParams(dimension_semantics=sem, vmem_limit_bytes=VMEM_LIMIT)


def _sigmoid(x):
    return 1.0 / (1.0 + jnp.exp(-x))


def _cond_row(i):
    return jnp.where(i < NB_CTX, 0, 1 + (i - NB_CTX) // LAT_BLOCKS_PER_REQ)


ADA_TN = 1536


def _ada_kernel(cv_ref, w_ref, b_ref, o_ref):
    cv = cv_ref[...]
    s = cv * _sigmoid(cv)
    o_ref[...] = jnp.dot(s, w_ref[...], precision=HIGHEST, preferred_element_type=F32) + b_ref[...]


def _ada(cv, w_ada, b_ada):
    n = 6 * D_MODEL
    return pl.pallas_call(
        _ada_kernel,
        grid=(DEPTH, n // ADA_TN),
        in_specs=[
            pl.BlockSpec((COND_ROWS, D_MODEL), lambda l, j: (0, 0)),
            pl.BlockSpec((None, D_MODEL, ADA_TN), lambda l, j: (l, 0, j)),
            pl.BlockSpec((None, 1, ADA_TN), lambda l, j: (l, 0, j)),
        ],
        out_specs=pl.BlockSpec((None, COND_ROWS, ADA_TN), lambda l, j: (l, 0, j)),
        out_shape=jax.ShapeDtypeStruct((DEPTH, COND_ROWS, n), F32),
        compiler_params=_cparams(("arbitrary", "arbitrary")),
        name="ada_mod",
    )(cv, w_ada, b_ada.reshape(DEPTH, 1, n))


IN_TN = 768


def _norm_mod(x, g, shift, scale):
    ms = jnp.mean(x * x, axis=-1, keepdims=True)
    return (x * lax.rsqrt(ms + EPS) * g) * (1.0 + scale) + shift


def _inproj_body(x, mod_ref, g_ref, w_ref, z_ref):
    h = _norm_mod(x, g_ref[...], mod_ref[0:1, :], mod_ref[1:2, :]).astype(BF16)
    for c in range(IN_COLS // IN_TN):
        cols = slice(c * IN_TN, (c + 1) * IN_TN)
        z_ref[:, cols] = jnp.dot(h, w_ref[:, cols], preferred_element_type=F32).astype(BF16)


def _inproj_first_kernel(xc_ref, xl_ref, mod_ref, g_ref, w_ref, z_ref, xo_ref):
    i = pl.program_id(0)
    x = jnp.where(i < NB_CTX, xc_ref[...], xl_ref[...])
    xo_ref[...] = x
    _inproj_body(x, mod_ref, g_ref, w_ref, z_ref)


def _moe_residual(x_ref, y0_ref, y1_ref, r_ref, mod_ref):
    r = r_ref[...]
    return x_ref[...] + mod_ref[5:6, :] * (r[:, 2:3] * y0_ref[...] + r[:, 3:4] * y1_ref[...])


def _inproj_next_kernel(x_ref, y0_ref, y1_ref, r_ref, modp_ref, mod_ref, g_ref, w_ref, z_ref, xo_ref):
    x = _moe_residual(x_ref, y0_ref, y1_ref, r_ref, modp_ref)
    xo_ref[...] = x
    _inproj_body(x, mod_ref, g_ref, w_ref, z_ref)


def _tok_spec(cols):
    return pl.BlockSpec((TM, cols), lambda i: (i, 0))


def _mod_spec():
    return pl.BlockSpec((None, 6, D_MODEL), lambda i: (_cond_row(i), 0, 0))


def _full_spec(shape):
    return pl.BlockSpec(shape, lambda i: (0,) * len(shape))


def _ctx_lat_specs(cols):
    return [pl.BlockSpec((TM, cols), lambda i: (jnp.minimum(i, NB_CTX - 1), 0)),
            pl.BlockSpec((TM, cols), lambda i: (jnp.maximum(i - NB_CTX, 0), 0))]


def _inproj_first(x_ctx, x_lat, mod, g, w_bf16):
    return pl.pallas_call(
        _inproj_first_kernel,
        grid=(NB_ALL,),
        in_specs=_ctx_lat_specs(D_MODEL) + [_mod_spec(), _full_spec((1, D_MODEL)),
                                            _full_spec((D_MODEL, IN_COLS))],
        out_specs=[_tok_spec(IN_COLS), _tok_spec(D_MODEL)],
        out_shape=[jax.ShapeDtypeStruct((T_ALL, IN_COLS), BF16),
                   jax.ShapeDtypeStruct((T_ALL, D_MODEL), F32)],
        compiler_params=_cparams(("arbitrary",)),
        name="inproj_first",
    )(x_ctx, x_lat, mod, g, w_bf16)


def _inproj_next(x, y, route, mod_prev, mod, g, w_bf16):
    return pl.pallas_call(
        _inproj_next_kernel,
        grid=(NB_ALL,),
        in_specs=[_tok_spec(D_MODEL),
                  pl.BlockSpec((TM, D_MODEL), lambda i: (i, 0)),
                  pl.BlockSpec((TM, D_MODEL), lambda i: (i, 1)),
                  _tok_spec(ROUTE_COLS),
                  _mod_spec(), _mod_spec(), _full_spec((1, D_MODEL)),
                  _full_spec((D_MODEL, IN_COLS))],
        out_specs=[_tok_spec(IN_COLS), _tok_spec(D_MODEL)],
        out_shape=[jax.ShapeDtypeStruct((T_ALL, IN_COLS), BF16),
                   jax.ShapeDtypeStruct((T_ALL, D_MODEL), F32)],
        compiler_params=_cparams(("arbitrary",)),
        name="inproj_next",
    )(x, y, y, route, mod_prev, mod, g, w_bf16)


CONV_PAD = 16
CONV_CHUNK = 64


def _conv_kernel(seq, z_ref, w_ref, b_ref, g_ref, be_ref, o_ref, upad_ref):
    zeros = jnp.zeros((CONV_PAD, CONV_CH), F32)
    upad_ref[0:CONV_PAD, :] = zeros
    upad_ref[seq + CONV_PAD:seq + 2 * CONV_PAD, :] = zeros

    def glu(ci, carry):
        base = pl.multiple_of(ci * 256, 256)
        zc = z_ref[pl.ds(base, 256), :].astype(F32)
        upad_ref[pl.ds(base + CONV_PAD, 256), :] = zc[:, :CONV_CH] * _sigmoid(zc[:, CONV_CH:])
        return carry

    lax.fori_loop(0, seq // 256, glu, 0)

    shift = CONV_PAD - CONV_K // 2

    def chunk(ci, carry):
        base = pl.multiple_of(ci * CONV_CHUNK, CONV_CHUNK)
        win = upad_ref[pl.ds(base, CONV_CHUNK + 2 * CONV_PAD), :]
        acc = jnp.zeros((CONV_CHUNK, CONV_CH), F32)
        for k in range(CONV_K):
            acc = acc + w_ref[k:k + 1, :] * win[k + shift:k + shift + CONV_CHUNK, :]
        acc = acc + b_ref[...]
        mu = jnp.mean(acc, axis=-1, keepdims=True)
        d = acc - mu
        var = jnp.mean(d * d, axis=-1, keepdims=True)
        n = d * lax.rsqrt(var + EPS) * g_ref[...] + be_ref[...]
        o_ref[pl.ds(base, CONV_CHUNK), :] = (n * _sigmoid(n)).astype(BF16)
        return carry

    lax.fori_loop(0, seq // CONV_CHUNK, chunk, 0)


def _conv(z, row_block0, nseq, seq, w, b, g, be):
    return pl.pallas_call(
        functools.partial(_conv_kernel, seq),
        grid=(nseq,),
        in_specs=[pl.BlockSpec((seq, 2 * CONV_CH), lambda s: (row_block0 + s, 0)),
                  _full_spec((CONV_K, CONV_CH)), _full_spec((1, CONV_CH)),
                  _full_spec((1, CONV_CH)), _full_spec((1, CONV_CH))],
        out_specs=pl.BlockSpec((seq, CONV_CH), lambda s: (s, 0)),
        out_shape=jax.ShapeDtypeStruct((nseq * seq, CONV_CH), BF16),
        scratch_shapes=[pltpu.VMEM((seq + 2 * CONV_PAD, CONV_CH), F32)],
        compiler_params=_cparams(("arbitrary",)),
        name="conv_seq%d" % seq,
    )(z, w, b, g, be)


def _dot_nt(a, b):
    return lax.dot_general(a, b, (((1,), (1,)), ((), ())), preferred_element_type=F32)


def _ctx_attn_kernel(q_ref, k_ref, v_ref, o_ref, ko_ref, vo_ref):
    scale = NA_DIM ** -0.5
    outs = []
    for h in range(NA_HEADS):
        cols = slice(h * NA_DIM, (h + 1) * NA_DIM)
        qh, kh, vh = q_ref[:, cols], k_ref[:, cols], v_ref[:, cols]
        ko_ref[h] = kh.astype(F32)
        vo_ref[h] = vh.astype(F32)
        s = _dot_nt(qh, kh) * scale
        m = jnp.max(s, axis=-1, keepdims=True)
        p = jnp.exp(s - m)
        den = jnp.sum(p, axis=-1, keepdims=True)
        o = jnp.dot(p.astype(BF16), vh, preferred_element_type=F32)
        outs.append(o / den)
    o_ref[...] = jnp.concatenate(outs, axis=-1).astype(BF16)


def _ctx_attn(z):
    qb, kb, vb = COL_NA_Q // NA_WIDTH, COL_NA_K // NA_WIDTH, COL_NA_V // NA_WIDTH
    head_shape = jax.ShapeDtypeStruct((BATCH, NA_HEADS, SEQ, NA_DIM), F32)
    head_spec = pl.BlockSpec((None, NA_HEADS, SEQ, NA_DIM), lambda b: (b, 0, 0, 0))
    return pl.pallas_call(
        _ctx_attn_kernel,
        grid=(BATCH,),
        in_specs=[pl.BlockSpec((SEQ, NA_WIDTH), lambda b: (b, qb)),
                  pl.BlockSpec((SEQ, NA_WIDTH), lambda b: (b, kb)),
                  pl.BlockSpec((SEQ, NA_WIDTH), lambda b: (b, vb))],
        out_specs=[pl.BlockSpec((SEQ, NA_WIDTH), lambda b: (b, 0)), head_spec, head_spec],
        out_shape=[jax.ShapeDtypeStruct((T_CTX, NA_WIDTH), BF16), head_shape, head_shape],
        compiler_params=_cparams(("arbitrary",)),
        name="ctx_attn",
    )(z, z, z)


NA_KINDS = (0, NA_ROWS, GRID_H - NA_ROWS)
N_DR = 2 * NA_KH - 1
N_DC = 2 * NA_KW - 1


def _na_row_offset(r0, i, j):
    ks = min(max(r0 - NA_KH // 2, 0), GRID_H - NA_KROWS)
    r, kr = r0 + i, ks + j
    rs = min(max(r - NA_KH // 2, 0), GRID_H - NA_KH)
    return kr - r + NA_KH - 1 if rs <= kr < rs + NA_KH else None


def _na_bias_kernel(rpb_ref, o_ref):
    lh = pl.program_id(0)
    shape = (GRID_W, 2 * GRID_W)
    qc = lax.broadcasted_iota(jnp.int32, shape, 0)
    lane = lax.broadcasted_iota(jnp.int32, shape, 1)
    kc = lane % GRID_W
    dc = jnp.clip(kc - qc, -(NA_KW - 1), NA_KW - 1) + NA_KW - 1
    cs = jnp.clip(qc - NA_KW // 2, 0, GRID_W - NA_KW)
    col_ok = (kc >= cs) & (kc < cs + NA_KW)
    neg = jnp.full(shape, NEG_INF, F32)
    tiles = []
    for dr in range(N_DR):
        base = (lh * N_DR + dr) * N_DC
        val = jnp.zeros(shape, F32)
        for d in range(N_DC):
            val = jnp.where(dc == d, rpb_ref[base + d], val)
        tiles.append(jnp.where(col_ok, val, neg))
    left = lane < GRID_W
    for kind, r0 in enumerate(NA_KINDS):
        for i in range(NA_ROWS):
            for jp in range(NA_KROWS // 2):
                dl, dr_ = _na_row_offset(r0, i, 2 * jp), _na_row_offset(r0, i, 2 * jp + 1)
                tl = neg if dl is None else tiles[dl]
                tr = neg if dr_ is None else tiles[dr_]
                o_ref[kind, i * GRID_W:(i + 1) * GRID_W, jp * 2 * GRID_W:(jp + 1) * 2 * GRID_W] = (
                    jnp.where(left, tl, tr))


def _na_bias_tables(rpb):
    return pl.pallas_call(
        _na_bias_kernel,
        grid=(DEPTH * NA_HEADS,),
        in_specs=[pl.BlockSpec(memory_space=pltpu.SMEM)],
        out_specs=pl.BlockSpec((None, len(NA_KINDS), NA_Q, NA_KEYS), lambda i: (i, 0, 0, 0)),
        out_shape=jax.ShapeDtypeStruct((DEPTH * NA_HEADS, len(NA_KINDS), NA_Q, NA_KEYS), F32),
        compiler_params=_cparams(("arbitrary",)),
        name="nbr_bias",
    )(rpb.reshape(-1))


def _na_kernel(q_ref, k_ref, v_ref, kc_ref, vc_ref, bias_ref, o_ref):
    rb = pl.program_id(2)
    scale = NA_DIM ** -0.5
    ks = jnp.clip(rb * NA_ROWS - NA_KH // 2, 0, GRID_H - NA_KROWS)
    start = pl.multiple_of(ks * GRID_W, GRID_W)
    q = q_ref[...]
    kl = k_ref[pl.ds(start, NA_KEYS), :]
    vl = v_ref[pl.ds(start, NA_KEYS), :]
    outs = []
    for hh in range(2):
        cols = slice(hh * NA_DIM, (hh + 1) * NA_DIM)
        qh = q[:, cols]
        s_loc = _dot_nt(qh, kl[:, cols]) * scale + bias_ref[hh]
        s_ctx = _dot_nt(qh, kc_ref[hh].astype(BF16)) * scale
        m = jnp.maximum(jnp.max(s_loc, axis=-1, keepdims=True), jnp.max(s_ctx, axis=-1, keepdims=True))
        p_loc = jnp.exp(s_loc - m)
        p_ctx = jnp.exp(s_ctx - m)
        den = jnp.sum(p_loc, axis=-1, keepdims=True) + jnp.sum(p_ctx, axis=-1, keepdims=True)
        o = (jnp.dot(p_loc.astype(BF16), vl[:, cols], preferred_element_type=F32)
             + jnp.dot(p_ctx.astype(BF16), vc_ref[hh].astype(BF16), preferred_element_type=F32))
        outs.append(o / den)
    o_ref[...] = jnp.concatenate(outs, axis=-1).astype(BF16)


def _na_attn(z, cache_k, cache_v, bias, layer):
    lat_q0 = T_CTX // NA_Q
    lat_s0 = T_CTX // DEC_SEQ
    qc, kc, vc = COL_NA_Q // LANES, COL_NA_K // LANES, COL_NA_V // LANES

    def kind(rb):
        return jnp.where(rb == 0, 0, jnp.where(rb == NA_RB - 1, 2, 1))

    ctx_spec = pl.BlockSpec((None, None, 2, PAST_LEN, NA_DIM), lambda b, hp, rb: (b, layer, hp, 0, 0))
    return pl.pallas_call(
        _na_kernel,
        grid=(DEC_BATCH, NA_HEADS // 2, NA_RB),
        in_specs=[pl.BlockSpec((NA_Q, LANES), lambda b, hp, rb: (lat_q0 + b * NA_RB + rb, qc + hp)),
                  pl.BlockSpec((DEC_SEQ, LANES), lambda b, hp, rb: (lat_s0 + b, kc + hp)),
                  pl.BlockSpec((DEC_SEQ, LANES), lambda b, hp, rb: (lat_s0 + b, vc + hp)),
                  ctx_spec, ctx_spec,
                  pl.BlockSpec((2, None, NA_Q, NA_KEYS),
                               lambda b, hp, rb: (layer * (NA_HEADS // 2) + hp, kind(rb), 0, 0))],
        out_specs=pl.BlockSpec((NA_Q, LANES), lambda b, hp, rb: (b * NA_RB + rb, hp)),
        out_shape=jax.ShapeDtypeStruct((T_LAT, NA_WIDTH), BF16),
        compiler_params=_cparams(("arbitrary", "arbitrary", "arbitrary")),
        name="nbr_attn",
    )(z, z, z, cache_k, cache_v, bias)


def _rope_tables():
    n_freq = RET_DIM // 4
    t = np.arange(DEC_SEQ)
    inv = jnp.asarray(ROPE_BASE, F32) ** (-jnp.arange(n_freq, dtype=F32) / n_freq)
    ang_r = jnp.asarray(t // GRID_W, F32)[:, None] * inv[None, :]
    ang_c = jnp.asarray(t % GRID_W, F32)[:, None] * inv[None, :]
    cos = jnp.concatenate([jnp.cos(ang_r)] * 2 + [jnp.cos(ang_c)] * 2, axis=-1)
    sin = jnp.concatenate([-jnp.sin(ang_r), jnp.sin(ang_r), -jnp.sin(ang_c), jnp.sin(ang_c)], axis=-1)
    return jnp.tile(cos, (1, RET_HEADS)), jnp.tile(sin, (1, RET_HEADS))


def _ret_kernel(seq, latent, *refs):
    if latent:
        (lg_ref, z_ref, gn_ref, cos_ref, sin_ref, s0f_ref, s0b_ref, y_ref,
         q_s, k_s, kv_s, st_s) = refs
    else:
        lg_ref, z_ref, gn_ref, y_ref, sf_ref, sb_ref, q_s, k_s, kv_s, st_s = refs
    nc = seq // RET_CHUNK
    ch, hd = RET_CHUNK, RET_DIM
    half = RET_DIM // 4

    row = lax.broadcasted_iota(jnp.int32, (ch, ch), 0).astype(F32)
    col = lax.broadcasted_iota(jnp.int32, (ch, ch), 1).astype(F32)
    pos = lax.broadcasted_iota(jnp.int32, (ch, hd), 0).astype(F32)
    decay, q_dec, k_dec, c_dec_f, c_dec_b = [], [], [], [], []
    for h in range(RET_HEADS):
        lf, lb = lg_ref[0, h], lg_ref[1, h]
        d_f = jnp.where(row >= col, jnp.exp(jnp.maximum(row - col, 0.0) * lf), 0.0)
        d_b = jnp.where(col >= row, jnp.exp(jnp.maximum(col - row, 0.0) * lb), 0.0)
        decay.append(d_f + d_b)
        q_dec.append(jnp.concatenate([jnp.exp((pos + 1.0) * lf), jnp.exp((ch - pos) * lb)], axis=-1))
        k_dec.append(jnp.concatenate([jnp.exp((ch - 1.0 - pos) * lf), jnp.exp(pos * lb)], axis=-1))
        c_dec_f.append(jnp.exp(jnp.zeros((hd, hd), F32) + ch * lf))
        c_dec_b.append(jnp.exp(jnp.zeros((hd, hd), F32) + ch * lb))

    if latent:
        lane = lax.broadcasted_iota(jnp.int32, (ch, RET_WIDTH), 1)
        first_half = (lane % (2 * half)) < half

    def rope(x, base):
        if not latent:
            return x
        swapped = jnp.where(first_half, pltpu.roll(x, RET_WIDTH - half, 1), pltpu.roll(x, half, 1))
        return x * cos_ref[pl.ds(base, ch), :] + swapped * sin_ref[pl.ds(base, ch), :]

    def pass1(n, carry):
        base = pl.multiple_of(n * ch, ch)
        zc = z_ref[pl.ds(base, ch), :]
        q = rope(zc[:, 0:RET_WIDTH].astype(F32), base)
        k = rope(zc[:, RET_WIDTH:2 * RET_WIDTH].astype(F32) * (RET_DIM ** -0.5), base)
        q_s[pl.ds(base, ch), :] = q.astype(BF16)
        k_s[pl.ds(base, ch), :] = k.astype(BF16)
        v = zc[:, 2 * RET_WIDTH:3 * RET_WIDTH]
        for h in range(RET_HEADS):
            cols = slice(h * hd, (h + 1) * hd)
            kh = k[:, cols]
            k2 = (jnp.concatenate([kh, kh], axis=-1) * k_dec[h]).astype(BF16)
            kv_s[n, h] = lax.dot_general(k2, v[:, cols], (((0,), (0,)), ((), ())),
                                         preferred_element_type=F32)
        return carry

    lax.fori_loop(0, nc, pass1, 0)

    for h in range(RET_HEADS):
        if latent:
            s_f, s_b = s0f_ref[h], s0b_ref[h]
        else:
            s_f = s_b = jnp.zeros((hd, hd), F32)

        def fwd(n, s, h=h):
            st_s[n, h, 0:hd, :] = s
            return c_dec_f[h] * s + kv_s[n, h, 0:hd, :]

        def bwd(i, s, h=h):
            n = nc - 1 - i
            st_s[n, h, hd:2 * hd, :] = s
            return c_dec_b[h] * s + kv_s[n, h, hd:2 * hd, :]

        s_f = lax.fori_loop(0, nc, fwd, s_f)
        s_b = lax.fori_loop(0, nc, bwd, s_b)
        if not latent:
            sf_ref[h] = s_f
            sb_ref[h] = s_b

    def pass3(n, carry):
        base = pl.multiple_of(n * ch, ch)
        zc = z_ref[pl.ds(base, ch), :]
        q = q_s[pl.ds(base, ch), :]
        k = k_s[pl.ds(base, ch), :]
        v = zc[:, 2 * RET_WIDTH:3 * RET_WIDTH]
        gate = zc[:, 3 * RET_WIDTH:4 * RET_WIDTH].astype(F32)
        outs = []
        for h in range(RET_HEADS):
            cols = slice(h * hd, (h + 1) * hd)
            qh = q[:, cols]
            s = _dot_nt(qh, k[:, cols]) * decay[h]
            o = jnp.dot(s.astype(BF16), v[:, cols], preferred_element_type=F32)
            qf = qh.astype(F32)
            q2 = (jnp.concatenate([qf, qf], axis=-1) * q_dec[h]).astype(BF16)
            o = o + jnp.dot(q2, st_s[n, h].astype(BF16), preferred_element_type=F32)
            mu = jnp.mean(o, axis=-1, keepdims=True)
            d = o - mu
            var = jnp.mean(d * d, axis=-1, keepdims=True)
            outs.append(d * lax.rsqrt(var + EPS))
        nrm = jnp.concatenate(outs, axis=-1)
        y_ref[pl.ds(base, ch), :] = (nrm * gn_ref[...] * (gate * _sigmoid(gate))).astype(BF16)
        return carry

    lax.fori_loop(0, nc, pass3, 0)


def _retention(z, lg, gn_g, latent, layer=None, rope=None, s0_f=None, s0_b=None):
    seq = DEC_SEQ if latent else SEQ
    nseq = DEC_BATCH if latent else BATCH
    nc = seq // RET_CHUNK
    row0 = (T_CTX // DEC_SEQ) if latent else 0
    cb = COL_RET // (4 * RET_WIDTH)
    in_specs = [pl.BlockSpec(memory_space=pltpu.SMEM),
                pl.BlockSpec((seq, 4 * RET_WIDTH), lambda s: (row0 + s, cb)),
                _full_spec((1, RET_WIDTH))]
    args = [lg, z, gn_g]
    state_shape = jax.ShapeDtypeStruct((nseq, RET_HEADS, RET_DIM, RET_DIM), F32)
    y_spec = pl.BlockSpec((seq, RET_WIDTH), lambda s: (s, 0))
    y_shape = jax.ShapeDtypeStruct((nseq * seq, RET_WIDTH), BF16)
    if latent:
        st_spec = pl.BlockSpec((None, None, RET_HEADS, RET_DIM, RET_DIM), lambda s: (s, layer, 0, 0, 0))
        in_specs += [_full_spec((seq, RET_WIDTH)), _full_spec((seq, RET_WIDTH)), st_spec, st_spec]
        args += [rope[0], rope[1], s0_f, s0_b]
        out_specs, out_shape = y_spec, y_shape
    else:
        so_spec = pl.BlockSpec((None, RET_HEADS, RET_DIM, RET_DIM), lambda s: (s, 0, 0, 0))
        out_specs, out_shape = [y_spec, so_spec, so_spec], [y_shape, state_shape, state_shape]
    return pl.pallas_call(
        functools.partial(_ret_kernel, seq, latent),
        grid=(nseq,),
        in_specs=in_specs,
        out_specs=out_specs,
        out_shape=out_shape,
        scratch_shapes=[pltpu.VMEM((seq, RET_WIDTH), BF16), pltpu.VMEM((seq, RET_WIDTH), BF16),
                        pltpu.VMEM((nc, RET_HEADS, 2 * RET_DIM, RET_DIM), F32),
                        pltpu.VMEM((nc, RET_HEADS, 2 * RET_DIM, RET_DIM), F32)],
        compiler_params=_cparams(("arbitrary",)),
        name="retention_lat" if latent else "retention_ctx",
    )(*args)


def _route(logits, count_ref):
    lane = lax.broadcasted_iota(jnp.int32, logits.shape, 1)
    lane_f = lane.astype(F32)
    big = float(ROUTE_COLS)
    neg = -jnp.inf
    is_grp = lane < N_GROUPS
    gl = jnp.where(is_grp, logits, neg)
    gmax = jnp.max(gl, axis=-1, keepdims=True)
    grp = jnp.min(jnp.where(gl == gmax, lane_f, big), axis=-1, keepdims=True)
    p_grp = 1.0 / jnp.sum(jnp.exp(gl - gmax), axis=-1, keepdims=True)
    e_f = lane_f - N_GROUPS
    lo = grp * EXPERTS_PER_GROUP
    in_grp = (e_f >= lo) & (e_f < lo + EXPERTS_PER_GROUP)
    el = jnp.where(in_grp, logits, neg)
    m1 = jnp.max(el, axis=-1, keepdims=True)
    i1 = jnp.min(jnp.where(el == m1, lane_f, big), axis=-1, keepdims=True)
    el2 = jnp.where(lane_f == i1, neg, el)
    m2 = jnp.max(el2, axis=-1, keepdims=True)
    i2 = jnp.min(jnp.where(el2 == m2, lane_f, big), axis=-1, keepdims=True)
    t = jnp.exp(m2 - m1)
    g1 = p_grp / (1.0 + t)
    g2 = p_grp * t / (1.0 + t)
    rows = logits.shape[0]
    oh1, oh2 = lane_f == i1, lane_f == i2
    oh = jnp.where(oh1 | oh2, 1.0, 0.0)
    tri = (lax.broadcasted_iota(jnp.int32, (rows, rows), 0)
           > lax.broadcasted_iota(jnp.int32, (rows, rows), 1))
    before = count_ref[...] + jnp.dot(jnp.where(tri, 1.0, 0.0).astype(BF16), oh.astype(BF16),
                                      preferred_element_type=F32)
    r1 = jnp.sum(jnp.where(oh1, before, 0.0), axis=-1, keepdims=True)
    r2 = jnp.sum(jnp.where(oh2, before, 0.0), axis=-1, keepdims=True)
    count_ref[...] = count_ref[...] + jnp.sum(oh, axis=0, keepdims=True)
    out = jnp.zeros(logits.shape, F32)
    for k, val in enumerate((i1 - N_GROUPS, i2 - N_GROUPS, g1, g2, r1, r2)):
        out = jnp.where(lane == k, val, out)
    return out


def _outproj_kernel(ycc, ycl, ync, ynl, yrc, yrl, x_ref, mod_ref, g_ref, w_ref, wr_ref, br_ref,
                    xo_ref, h_ref, r_ref, cnt_ref):
    is_ctx = pl.program_id(0) < NB_CTX

    @pl.when(pl.program_id(0) == 0)
    def _():
        cnt_ref[...] = jnp.zeros_like(cnt_ref)

    yc = jnp.where(is_ctx, ycc[...], ycl[...])
    yn = jnp.where(is_ctx, ync[...], ynl[...])
    yr = jnp.where(is_ctx, yrc[...], yrl[...])
    y = (jnp.dot(yc, w_ref[0:CONV_CH, :], preferred_element_type=F32)
         + jnp.dot(yn, w_ref[CONV_CH:CONV_CH + NA_WIDTH, :], preferred_element_type=F32)
         + jnp.dot(yr, w_ref[CONV_CH + NA_WIDTH:, :], preferred_element_type=F32))
    x = x_ref[...] + mod_ref[2:3, :] * y
    xo_ref[...] = x
    h = _norm_mod(x, g_ref[...], mod_ref[3:4, :], mod_ref[4:5, :])
    h_ref[...] = h
    h_hi = h.astype(BF16)
    h_lo = (h - h_hi.astype(F32)).astype(BF16)
    hw = jnp.dot(h_hi, wr_ref[...], preferred_element_type=F32)
    logits = (hw[:, :ROUTE_COLS] + hw[:, ROUTE_COLS:]
              + jnp.dot(h_lo, wr_ref[:, :ROUTE_COLS], preferred_element_type=F32) + br_ref[...])
    r_ref[...] = _route(logits, cnt_ref)


def _outproj(y_conv, y_na, y_ret, x, mod, g, w_bf16, w_route, b_route):
    return pl.pallas_call(
        _outproj_kernel,
        grid=(NB_ALL,),
        in_specs=(_ctx_lat_specs(CONV_CH) + _ctx_lat_specs(NA_WIDTH) + _ctx_lat_specs(RET_WIDTH)
                  + [_tok_spec(D_MODEL), _mod_spec(), _full_spec((1, D_MODEL)),
                     _full_spec((D_MODEL, D_MODEL)), _full_spec((D_MODEL, 2 * ROUTE_COLS)),
                     _full_spec((1, ROUTE_COLS))]),
        out_specs=[_tok_spec(D_MODEL), _tok_spec(D_MODEL), _tok_spec(ROUTE_COLS),
                   _full_spec((1, ROUTE_COLS))],
        out_shape=[jax.ShapeDtypeStruct((T_ALL, D_MODEL), F32),
                   jax.ShapeDtypeStruct((T_ALL, D_MODEL), F32),
                   jax.ShapeDtypeStruct((T_ALL, ROUTE_COLS), F32),
                   jax.ShapeDtypeStruct((1, ROUTE_COLS), F32)],
        compiler_params=_cparams(("arbitrary",)),
        name="outproj_route",
    )(y_conv[0], y_conv[1], y_na[0], y_na[1], y_ret[0], y_ret[1], x, mod, g, w_bf16, w_route, b_route)


def _dispatch_tables(route, counts):
    counts = counts[0, N_GROUPS:N_GROUPS + N_EXPERTS].astype(jnp.int32)
    experts = jnp.arange(N_EXPERTS, dtype=jnp.int32)
    nblk = (counts + MOE_BLK - 1) // MOE_BLK
    blk_end = jnp.cumsum(nblk)
    blk_start = blk_end - nblk
    row_start = blk_start * MOE_BLK
    eid = route[:, 0:2].astype(jnp.int32)
    rank = route[:, 4:6].astype(jnp.int32)
    start_of = jnp.sum(jnp.where(eid[:, :, None] == experts[None, None, :], row_start[None, None, :], 0), axis=-1)
    dest = (start_of + rank).reshape(-1)
    blk = jnp.arange(MOE_NBLK, dtype=jnp.int32)
    n_active = blk_end[-1]
    blk_src = jnp.minimum(blk, n_active - 1)
    blk_e = jnp.minimum(jnp.sum((blk_end[None, :] <= blk_src[:, None]).astype(jnp.int32), axis=-1),
                        N_EXPERTS - 1)
    counts_of = jnp.sum(jnp.where(blk_e[:, None] == experts[None, :], counts[None, :], 0), axis=-1)
    start_of_blk = jnp.sum(jnp.where(blk_e[:, None] == experts[None, :], blk_start[None, :], 0), axis=-1)
    left = counts_of - (blk - start_of_blk) * MOE_BLK
    blk_nv = jnp.where(blk < n_active, jnp.clip(left, 0, MOE_BLK), 0).astype(jnp.int32)
    return dest, blk_e, blk_nv


ISSUE_UNROLL = SUBLANES


def _for_rows(n, fn):
    groups = n // ISSUE_UNROLL

    def group(j, carry):
        for u in range(ISSUE_UNROLL):
            fn(j * ISSUE_UNROLL + u)
        return carry

    def single(r, carry):
        fn(r)
        return carry

    lax.fori_loop(0, groups, group, 0)
    lax.fori_loop(groups * ISSUE_UNROLL, n, single, 0)


def _wait_row_copies(n, copy_of_rows):
    n_down = pl.multiple_of(n // SUBLANES * SUBLANES, SUBLANES)

    @pl.when(n_down > 0)
    def _():
        copy_of_rows(n_down).wait()

    for r in range(SUBLANES - 1):
        @pl.when(n_down + r < n)
        def _():
            copy_of_rows(1).wait()


def _moe_kernel(dest, blk_e, blk_nv, h_hbm, w1_ref, w3_ref, w2_ref, y_hbm,
                code, xbuf, obuf, w1b, w3b, w2b, gsem, ssem):
    i = pl.program_id(0)
    last = pl.num_programs(0) - 1
    slot = i % 2

    def gather_copy(blk, s, r):
        tok = code[blk * MOE_BLK + r] >> 1
        return pltpu.make_async_copy(h_hbm.at[pl.ds(tok, 1)], xbuf.at[s, pl.ds(r, 1)], gsem.at[s])

    def scatter_copy(blk, s, r):
        c = code[blk * MOE_BLK + r]
        col = pl.multiple_of((c & 1) * D_MODEL, D_MODEL)
        return pltpu.make_async_copy(obuf.at[s, pl.ds(r, 1)],
                                     y_hbm.at[pl.ds(c >> 1, 1), pl.ds(col, D_MODEL)], ssem.at[s])

    def start_gathers(blk, s):
        _for_rows(blk_nv[blk], lambda r: gather_copy(blk, s, r).start())

    def wait_gathers(blk, s):
        _wait_row_copies(blk_nv[blk], lambda k: pltpu.make_async_copy(
            h_hbm.at[pl.ds(0, k)], xbuf.at[s, pl.ds(0, k)], gsem.at[s]))

    def wait_scatters(blk, s):
        _wait_row_copies(blk_nv[blk], lambda k: pltpu.make_async_copy(
            obuf.at[s, pl.ds(0, k)], y_hbm.at[pl.ds(0, k), pl.ds(0, D_MODEL)], ssem.at[s]))

    @pl.when(i == 0)
    def _():
        xbuf[...] = jnp.zeros_like(xbuf)

        def invert(a):
            code[dest[a]] = a

        _for_rows(2 * T_ALL, invert)
        start_gathers(0, 0)

    @pl.when(i < last)
    def _():
        start_gathers(i + 1, 1 - slot)

    @pl.when(i >= 2)
    def _():
        wait_scatters(i - 2, slot)

    @pl.when(blk_nv[i] > 0)
    def _():
        @pl.when((i == 0) | (blk_e[i] != blk_e[jnp.maximum(i - 1, 0)]))
        def _():
            w1b[...] = w1_ref[...].astype(BF16)
            w3b[...] = w3_ref[...].astype(BF16)
            w2b[...] = w2_ref[...].astype(BF16)

        wait_gathers(i, slot)
        xb = xbuf[slot].astype(BF16)
        a = jnp.dot(xb, w1b[...], preferred_element_type=F32)
        b = jnp.dot(xb, w3b[...], preferred_element_type=F32)
        mid = (a * _sigmoid(a) * b).astype(BF16)
        obuf[slot] = jnp.dot(mid, w2b[...], preferred_element_type=F32)
        _for_rows(blk_nv[i], lambda r: scatter_copy(i, slot, r).start())

    @pl.when(i == last)
    def _():
        wait_scatters(i - 1, 1 - slot)
        wait_scatters(i, slot)


def _moe(h, w1, w3, w2, layer, dest, blk_e, blk_nv):
    def w_spec(rows, cols):
        return pl.BlockSpec((None, None, rows, cols), lambda i, d, be, bn: (layer, be[i], 0, 0))

    grid_spec = pltpu.PrefetchScalarGridSpec(
        num_scalar_prefetch=3,
        grid=(MOE_NBLK,),
        in_specs=[pl.BlockSpec(memory_space=pl.ANY), w_spec(D_MODEL, D_EXPERT),
                  w_spec(D_MODEL, D_EXPERT), w_spec(D_EXPERT, D_MODEL)],
        out_specs=pl.BlockSpec(memory_space=pl.ANY),
        scratch_shapes=[pltpu.SMEM((MOE_ROWS,), jnp.int32),
                        pltpu.VMEM((2, MOE_BLK, D_MODEL), F32), pltpu.VMEM((2, MOE_BLK, D_MODEL), F32),
                        pltpu.VMEM((D_MODEL, D_EXPERT), BF16), pltpu.VMEM((D_MODEL, D_EXPERT), BF16),
                        pltpu.VMEM((D_EXPERT, D_MODEL), BF16),
                        pltpu.SemaphoreType.DMA((2,)), pltpu.SemaphoreType.DMA((2,))])
    return pl.pallas_call(
        _moe_kernel,
        grid_spec=grid_spec,
        out_shape=jax.ShapeDtypeStruct((T_ALL, 2 * D_MODEL), F32),
        compiler_params=_cparams(("arbitrary",)),
        name="moe_experts",
    )(dest, blk_e, blk_nv, h, w1, w3, w2)


def _final_kernel(x_ref, y0_ref, y1_ref, r_ref, mod_ref, g_ref, o_ref):
    x = _moe_residual(x_ref, y0_ref, y1_ref, r_ref, mod_ref)
    ms = jnp.mean(x * x, axis=-1, keepdims=True)
    o_ref[...] = x * lax.rsqrt(ms + EPS) * g_ref[...]


def _final(x, y, route, mod, g, block0, nblocks):
    return pl.pallas_call(
        _final_kernel,
        grid=(nblocks,),
        in_specs=[pl.BlockSpec((TM, D_MODEL), lambda i: (block0 + i, 0)),
                  pl.BlockSpec((TM, D_MODEL), lambda i: (block0 + i, 0)),
                  pl.BlockSpec((TM, D_MODEL), lambda i: (block0 + i, 1)),
                  pl.BlockSpec((TM, ROUTE_COLS), lambda i: (block0 + i, 0)),
                  pl.BlockSpec((None, 6, D_MODEL), lambda i: (_cond_row(block0 + i), 0, 0)),
                  _full_spec((1, D_MODEL))],
        out_specs=_tok_spec(D_MODEL),
        out_shape=jax.ShapeDtypeStruct((nblocks * TM, D_MODEL), F32),
        compiler_params=_cparams(("arbitrary",)),
        name="final_norm",
    )(x, y, y, route, mod, g)


def kernel(x_prompt, x_sample, c, cache_k, cache_v, state_ret_f, state_ret_b, c_ctx, w_ada, b_ada, norm1_g, norm2_g, w_in, w_out, conv_w, conv_b, conv_ln_g, conv_ln_b, na_rpb, ret_lg_f, ret_lg_b, ret_gn_g, w_route_g, b_route_g, w_route_e, b_route_e, w1, w3, w2, final_g):
    cv = jnp.zeros((COND_ROWS, D_MODEL), F32).at[0].set(c_ctx).at[1:N_COND].set(c)
    mods = _ada(cv, w_ada, b_ada).reshape(DEPTH, COND_ROWS, 6, D_MODEL)
    w_in_b = w_in.astype(BF16)
    w_out_b = w_out.astype(BF16)
    pad = ROUTE_COLS - N_GROUPS - N_EXPERTS
    w_route = jnp.pad(jnp.concatenate([w_route_g, w_route_e], axis=-1), ((0, 0), (0, 0), (0, pad)))
    b_route = jnp.pad(jnp.concatenate([b_route_g, b_route_e], axis=-1), ((0, 0), (0, pad)))
    w_route_hi = w_route.astype(BF16)
    w_route_lo = (w_route - w_route_hi.astype(F32)).astype(BF16)
    w_route = jnp.concatenate([w_route_hi, w_route_lo], axis=-1)
    na_bias = _na_bias_tables(na_rpb)
    rope = _rope_tables()
    lg = jnp.stack([ret_lg_f, ret_lg_b], axis=1)

    x_ctx = x_prompt.reshape(T_CTX, D_MODEL)
    x_lat = x_sample.reshape(T_LAT, D_MODEL)
    x = y = route = None
    k_list, v_list, sf_list, sb_list = [], [], [], []
    for l in range(DEPTH):
        g1 = norm1_g[l].reshape(1, D_MODEL)
        if l == 0:
            z, x = _inproj_first(x_ctx, x_lat, mods[l], g1, w_in_b[l])
        else:
            z, x = _inproj_next(x, y, route, mods[l - 1], mods[l], g1, w_in_b[l])
        conv_args = (conv_w[l], conv_b[l].reshape(1, -1), conv_ln_g[l].reshape(1, -1),
                     conv_ln_b[l].reshape(1, -1))
        yc_c = _conv(z, 0, BATCH, SEQ, *conv_args)
        yc_l = _conv(z, T_CTX // DEC_SEQ, DEC_BATCH, DEC_SEQ, *conv_args)
        yn_c, k_l, v_l = _ctx_attn(z)
        yn_l = _na_attn(z, cache_k, cache_v, na_bias, l)
        gn = ret_gn_g[l].reshape(1, RET_WIDTH)
        yr_c, sf_l, sb_l = _retention(z, lg[l], gn, latent=False)
        yr_l = _retention(z, lg[l], gn, latent=True, layer=l, rope=rope,
                          s0_f=state_ret_f, s0_b=state_ret_b)
        x, h, route, counts = _outproj((yc_c, yc_l), (yn_c, yn_l), (yr_c, yr_l), x, mods[l],
                                       norm2_g[l].reshape(1, D_MODEL), w_out_b[l], w_route[l],
                                       b_route[l].reshape(1, ROUTE_COLS))
        y = _moe(h, w1, w3, w2, l, *_dispatch_tables(route, counts))
        k_list.append(k_l)
        v_list.append(v_l)
        sf_list.append(sf_l)
        sb_list.append(sb_l)
    fg = final_g.reshape(1, D_MODEL)
    y_prompt = _final(x, y, route, mods[DEPTH - 1], fg, 0, NB_CTX).reshape(BATCH, SEQ, D_MODEL)
    y_sample = _final(x, y, route, mods[DEPTH - 1], fg, NB_CTX, NB_LAT).reshape(DEC_BATCH, DEC_SEQ, D_MODEL)
    return (y_prompt, y_sample, jnp.stack(k_list, axis=1), jnp.stack(v_list, axis=1),
            jnp.stack(sf_list, axis=1), jnp.stack(sb_list, axis=1))
```

```python
import functools

import numpy as np
import jax
import jax.numpy as jnp
from jax import lax
from jax.experimental import pallas as pl
from jax.experimental.pallas import tpu as pltpu

D_MODEL = 1024
BATCH = 32
SEQ = 256
DEPTH = 2
DEC_BATCH = 4
DEC_SEQ = 4096
PAST_LEN = 512
GRID_W = 64
GRID_H = DEC_SEQ // GRID_W
CONV_CH = 256
CONV_K = 31
NA_HEADS = 8
NA_DIM = 64
NA_WIDTH = NA_HEADS * NA_DIM
NA_KH = 8
NA_KW = 16
RET_HEADS = 4
RET_DIM = 64
RET_WIDTH = RET_HEADS * RET_DIM
RET_CHUNK = 128
ROPE_BASE = 10000.0
N_GROUPS = 4
EXPERTS_PER_GROUP = 8
N_EXPERTS = N_GROUPS * EXPERTS_PER_GROUP
D_EXPERT = 512
IN_COLS = 2 * CONV_CH + 3 * NA_WIDTH + 4 * RET_WIDTH
EPS = 1e-6
NEG_INF = -1e30

F32 = jnp.float32
BF16 = jnp.bfloat16
HIGHEST = lax.Precision.HIGHEST

T_CTX = BATCH * SEQ
T_LAT = DEC_BATCH * DEC_SEQ
T_ALL = T_CTX + T_LAT
N_COND = 1 + DEC_BATCH
COND_ROWS = 8

TM = 512
NB_CTX = T_CTX // TM
NB_LAT = T_LAT // TM
NB_ALL = NB_CTX + NB_LAT
LAT_BLOCKS_PER_REQ = DEC_SEQ // TM

LANES = 128
SUBLANES = 8
ROUTE_COLS = LANES

COL_CONV = 0
COL_NA_Q = 2 * CONV_CH
COL_NA_K = COL_NA_Q + NA_WIDTH
COL_NA_V = COL_NA_K + NA_WIDTH
COL_RET = COL_NA_V + NA_WIDTH

NA_ROWS = 8
NA_Q = NA_ROWS * GRID_W
NA_KROWS = 2 * NA_ROWS
NA_KEYS = NA_KROWS * GRID_W
NA_RB = GRID_H // NA_ROWS

MOE_BLK = 256
MOE_LC = -(-(2 * TM + N_EXPERTS * (SUBLANES - 1)) // LANES) * LANES
MOE_NBLK = -(-(NB_ALL * MOE_LC) // MOE_BLK) + N_EXPERTS
N_SEG = NB_ALL * N_EXPERTS

VMEM_LIMIT = 56 * 1024 * 1024


def _cparams(sem):
    return pltpu.CompilerParams(dimension_semantics=sem, vmem_limit_bytes=VMEM_LIMIT)


def _sigmoid(x):
    return 1.0 / (1.0 + jnp.exp(-x))


def _cond_row(i):
    return jnp.where(i < NB_CTX, 0, 1 + (i - NB_CTX) // LAT_BLOCKS_PER_REQ)


ADA_TN = 1536


def _ada_kernel(cv_ref, w_ref, b_ref, o_ref):
    cv = cv_ref[...]
    s = cv * _sigmoid(cv)
    o_ref[...] = jnp.dot(s, w_ref[...], precision=HIGHEST, preferred_element_type=F32) + b_ref[...]


def _ada(cv, w_ada, b_ada):
    n = 6 * D_MODEL
    return pl.pallas_call(
        _ada_kernel,
        grid=(DEPTH, n // ADA_TN),
        in_specs=[
            pl.BlockSpec((COND_ROWS, D_MODEL), lambda l, j: (0, 0)),
            pl.BlockSpec((None, D_MODEL, ADA_TN), lambda l, j: (l, 0, j)),
            pl.BlockSpec((None, 1, ADA_TN), lambda l, j: (l, 0, j)),
        ],
        out_specs=pl.BlockSpec((None, COND_ROWS, ADA_TN), lambda l, j: (l, 0, j)),
        out_shape=jax.ShapeDtypeStruct((DEPTH, COND_ROWS, n), F32),
        compiler_params=_cparams(("arbitrary", "arbitrary")),
        name="ada_mod",
    )(cv, w_ada, b_ada.reshape(DEPTH, 1, n))


IN_TN = 768


def _norm_mod(x, g, shift, scale):
    ms = jnp.mean(x * x, axis=-1, keepdims=True)
    return (x * lax.rsqrt(ms + EPS) * g) * (1.0 + scale) + shift


def _inproj_body(x, mod_ref, g_ref, w_ref, z_ref):
    h = _norm_mod(x, g_ref[...], mod_ref[0:1, :], mod_ref[1:2, :]).astype(BF16)
    for c in range(IN_COLS // IN_TN):
        cols = slice(c * IN_TN, (c + 1) * IN_TN)
        z_ref[:, cols] = jnp.dot(h, w_ref[:, cols], preferred_element_type=F32).astype(BF16)


def _inproj_first_kernel(xc_ref, xl_ref, mod_ref, g_ref, w_ref, z_ref, xo_ref):
    i = pl.program_id(0)
    x = jnp.where(i < NB_CTX, xc_ref[...], xl_ref[...])
    xo_ref[...] = x
    _inproj_body(x, mod_ref, g_ref, w_ref, z_ref)


def _slot_onehot(route, slot):
    pos = route[:, 4 + slot:5 + slot].astype(jnp.int32)
    return lax.broadcasted_iota(jnp.int32, (route.shape[0], MOE_LC), 1) == pos


def _moe_residual(x_ref, ys_ref, r_ref, mod_ref):
    r = r_ref[...]
    ys = ys_ref[...].astype(BF16)
    y = [jnp.dot(jnp.where(_slot_onehot(r, k), 1.0, 0.0).astype(BF16), ys, preferred_element_type=F32)
         for k in range(2)]
    return x_ref[...] + mod_ref[5:6, :] * (r[:, 2:3] * y[0] + r[:, 3:4] * y[1])


def _inproj_next_kernel(x_ref, ys_ref, r_ref, modp_ref, mod_ref, g_ref, w_ref, z_ref, xo_ref):
    x = _moe_residual(x_ref, ys_ref, r_ref, modp_ref)
    xo_ref[...] = x
    _inproj_body(x, mod_ref, g_ref, w_ref, z_ref)


def _tok_spec(cols):
    return pl.BlockSpec((TM, cols), lambda i: (i, 0))


def _mod_spec():
    return pl.BlockSpec((None, 6, D_MODEL), lambda i: (_cond_row(i), 0, 0))


def _full_spec(shape):
    return pl.BlockSpec(shape, lambda i: (0,) * len(shape))


def _ctx_lat_specs(cols):
    return [pl.BlockSpec((TM, cols), lambda i: (jnp.minimum(i, NB_CTX - 1), 0)),
            pl.BlockSpec((TM, cols), lambda i: (jnp.maximum(i - NB_CTX, 0), 0))]


def _inproj_first(x_ctx, x_lat, mod, g, w_bf16):
    return pl.pallas_call(
        _inproj_first_kernel,
        grid=(NB_ALL,),
        in_specs=_ctx_lat_specs(D_MODEL) + [_mod_spec(), _full_spec((1, D_MODEL)),
                                            _full_spec((D_MODEL, IN_COLS))],
        out_specs=[_tok_spec(IN_COLS), _tok_spec(D_MODEL)],
        out_shape=[jax.ShapeDtypeStruct((T_ALL, IN_COLS), BF16),
                   jax.ShapeDtypeStruct((T_ALL, D_MODEL), F32)],
        compiler_params=_cparams(("arbitrary",)),
        name="inproj_first",
    )(x_ctx, x_lat, mod, g, w_bf16)


def _inproj_next(x, ys, route, mod_prev, mod, g, w_bf16):
    return pl.pallas_call(
        _inproj_next_kernel,
        grid=(NB_ALL,),
        in_specs=[_tok_spec(D_MODEL),
                  pl.BlockSpec((MOE_LC, D_MODEL), lambda i: (i, 0)),
                  _tok_spec(ROUTE_COLS),
                  _mod_spec(), _mod_spec(), _full_spec((1, D_MODEL)),
                  _full_spec((D_MODEL, IN_COLS))],
        out_specs=[_tok_spec(IN_COLS), _tok_spec(D_MODEL)],
        out_shape=[jax.ShapeDtypeStruct((T_ALL, IN_COLS), BF16),
                   jax.ShapeDtypeStruct((T_ALL, D_MODEL), F32)],
        compiler_params=_cparams(("arbitrary",)),
        name="inproj_next",
    )(x, ys, route, mod_prev, mod, g, w_bf16)


CONV_PAD = 16
CONV_CHUNK = 64


def _conv_kernel(seq, z_ref, w_ref, b_ref, g_ref, be_ref, o_ref, upad_ref):
    zeros = jnp.zeros((CONV_PAD, CONV_CH), F32)
    upad_ref[0:CONV_PAD, :] = zeros
    upad_ref[seq + CONV_PAD:seq + 2 * CONV_PAD, :] = zeros

    def glu(ci, carry):
        base = pl.multiple_of(ci * 256, 256)
        zc = z_ref[pl.ds(base, 256), :].astype(F32)
        upad_ref[pl.ds(base + CONV_PAD, 256), :] = zc[:, :CONV_CH] * _sigmoid(zc[:, CONV_CH:])
        return carry

    lax.fori_loop(0, seq // 256, glu, 0)

    shift = CONV_PAD - CONV_K // 2

    def chunk(ci, carry):
        base = pl.multiple_of(ci * CONV_CHUNK, CONV_CHUNK)
        win = upad_ref[pl.ds(base, CONV_CHUNK + 2 * CONV_PAD), :]
        acc = jnp.zeros((CONV_CHUNK, CONV_CH), F32)
        for k in range(CONV_K):
            acc = acc + w_ref[k:k + 1, :] * win[k + shift:k + shift + CONV_CHUNK, :]
        acc = acc + b_ref[...]
        mu = jnp.mean(acc, axis=-1, keepdims=True)
        d = acc - mu
        var = jnp.mean(d * d, axis=-1, keepdims=True)
        n = d * lax.rsqrt(var + EPS) * g_ref[...] + be_ref[...]
        o_ref[pl.ds(base, CONV_CHUNK), :] = (n * _sigmoid(n)).astype(BF16)
        return carry

    lax.fori_loop(0, seq // CONV_CHUNK, chunk, 0)


def _conv(z, row_block0, nseq, seq, w, b, g, be):
    return pl.pallas_call(
        functools.partial(_conv_kernel, seq),
        grid=(nseq,),
        in_specs=[pl.BlockSpec((seq, 2 * CONV_CH), lambda s: (row_block0 + s, 0)),
                  _full_spec((CONV_K, CONV_CH)), _full_spec((1, CONV_CH)),
                  _full_spec((1, CONV_CH)), _full_spec((1, CONV_CH))],
        out_specs=pl.BlockSpec((seq, CONV_CH), lambda s: (s, 0)),
        out_shape=jax.ShapeDtypeStruct((nseq * seq, CONV_CH), BF16),
        scratch_shapes=[pltpu.VMEM((seq + 2 * CONV_PAD, CONV_CH), F32)],
        compiler_params=_cparams(("arbitrary",)),
        name="conv_seq%d" % seq,
    )(z, w, b, g, be)


def _dot_nt(a, b):
    return lax.dot_general(a, b, (((1,), (1,)), ((), ())), preferred_element_type=F32)


def _ctx_attn_kernel(q_ref, k_ref, v_ref, o_ref, ko_ref, vo_ref):
    scale = NA_DIM ** -0.5
    outs = []
    for h in range(NA_HEADS):
        cols = slice(h * NA_DIM, (h + 1) * NA_DIM)
        qh, kh, vh = q_ref[:, cols], k_ref[:, cols], v_ref[:, cols]
        ko_ref[h] = kh.astype(F32)
        vo_ref[h] = vh.astype(F32)
        s = _dot_nt(qh, kh) * scale
        m = jnp.max(s, axis=-1, keepdims=True)
        p = jnp.exp(s - m)
        den = jnp.sum(p, axis=-1, keepdims=True)
        o = jnp.dot(p.astype(BF16), vh, preferred_element_type=F32)
        outs.append(o / den)
    o_ref[...] = jnp.concatenate(outs, axis=-1).astype(BF16)


def _ctx_attn(z):
    qb, kb, vb = COL_NA_Q // NA_WIDTH, COL_NA_K // NA_WIDTH, COL_NA_V // NA_WIDTH
    head_shape = jax.ShapeDtypeStruct((BATCH, NA_HEADS, SEQ, NA_DIM), F32)
    head_spec = pl.BlockSpec((None, NA_HEADS, SEQ, NA_DIM), lambda b: (b, 0, 0, 0))
    return pl.pallas_call(
        _ctx_attn_kernel,
        grid=(BATCH,),
        in_specs=[pl.BlockSpec((SEQ, NA_WIDTH), lambda b: (b, qb)),
                  pl.BlockSpec((SEQ, NA_WIDTH), lambda b: (b, kb)),
                  pl.BlockSpec((SEQ, NA_WIDTH), lambda b: (b, vb))],
        out_specs=[pl.BlockSpec((SEQ, NA_WIDTH), lambda b: (b, 0)), head_spec, head_spec],
        out_shape=[jax.ShapeDtypeStruct((T_CTX, NA_WIDTH), BF16), head_shape, head_shape],
        compiler_params=_cparams(("arbitrary",)),
        name="ctx_attn",
    )(z, z, z)


NA_KINDS = (0, NA_ROWS, GRID_H - NA_ROWS)
N_DR = 2 * NA_KH - 1
N_DC = 2 * NA_KW - 1


def _na_row_offset(r0, i, j):
    ks = min(max(r0 - NA_KH // 2, 0), GRID_H - NA_KROWS)
    r, kr = r0 + i, ks + j
    rs = min(max(r - NA_KH // 2, 0), GRID_H - NA_KH)
    return kr - r + NA_KH - 1 if rs <= kr < rs + NA_KH else None


def _na_bias_kernel(rpb_ref, o_ref):
    lh = pl.program_id(0)
    shape = (GRID_W, 2 * GRID_W)
    qc = lax.broadcasted_iota(jnp.int32, shape, 0)
    lane = lax.broadcasted_iota(jnp.int32, shape, 1)
    kc = lane % GRID_W
    dc = jnp.clip(kc - qc, -(NA_KW - 1), NA_KW - 1) + NA_KW - 1
    cs = jnp.clip(qc - NA_KW // 2, 0, GRID_W - NA_KW)
    col_ok = (kc >= cs) & (kc < cs + NA_KW)
    neg = jnp.full(shape, NEG_INF, F32)
    tiles = []
    for dr in range(N_DR):
        base = (lh * N_DR + dr) * N_DC
        val = jnp.zeros(shape, F32)
        for d in range(N_DC):
            val = jnp.where(dc == d, rpb_ref[base + d], val)
        tiles.append(jnp.where(col_ok, val, neg))
    left = lane < GRID_W
    for kind, r0 in enumerate(NA_KINDS):
        for i in range(NA_ROWS):
            for jp in range(NA_KROWS // 2):
                dl, dr_ = _na_row_offset(r0, i, 2 * jp), _na_row_offset(r0, i, 2 * jp + 1)
                tl = neg if dl is None else tiles[dl]
                tr = neg if dr_ is None else tiles[dr_]
                o_ref[kind, i * GRID_W:(i + 1) * GRID_W, jp * 2 * GRID_W:(jp + 1) * 2 * GRID_W] = (
                    jnp.where(left, tl, tr))


def _na_bias_tables(rpb):
    return pl.pallas_call(
        _na_bias_kernel,
        grid=(DEPTH * NA_HEADS,),
        in_specs=[pl.BlockSpec(memory_space=pltpu.SMEM)],
        out_specs=pl.BlockSpec((None, len(NA_KINDS), NA_Q, NA_KEYS), lambda i: (i, 0, 0, 0)),
        out_shape=jax.ShapeDtypeStruct((DEPTH * NA_HEADS, len(NA_KINDS), NA_Q, NA_KEYS), F32),
        compiler_params=_cparams(("arbitrary",)),
        name="nbr_bias",
    )(rpb.reshape(-1))


def _na_kernel(q_ref, k_ref, v_ref, kc_ref, vc_ref, bias_ref, o_ref):
    rb = pl.program_id(2)
    scale = NA_DIM ** -0.5
    ks = jnp.clip(rb * NA_ROWS - NA_KH // 2, 0, GRID_H - NA_KROWS)
    start = pl.multiple_of(ks * GRID_W, GRID_W)
    q = q_ref[...]
    kl = k_ref[pl.ds(start, NA_KEYS), :]
    vl = v_ref[pl.ds(start, NA_KEYS), :]
    outs = []
    for hh in range(2):
        cols = slice(hh * NA_DIM, (hh + 1) * NA_DIM)
        qh = q[:, cols]
        s_loc = _dot_nt(qh, kl[:, cols]) * scale + bias_ref[hh]
        s_ctx = _dot_nt(qh, kc_ref[hh].astype(BF16)) * scale
        m = jnp.maximum(jnp.max(s_loc, axis=-1, keepdims=True), jnp.max(s_ctx, axis=-1, keepdims=True))
        p_loc = jnp.exp(s_loc - m)
        p_ctx = jnp.exp(s_ctx - m)
        den = jnp.sum(p_loc, axis=-1, keepdims=True) + jnp.sum(p_ctx, axis=-1, keepdims=True)
        o = (jnp.dot(p_loc.astype(BF16), vl[:, cols], preferred_element_type=F32)
             + jnp.dot(p_ctx.astype(BF16), vc_ref[hh].astype(BF16), preferred_element_type=F32))
        outs.append(o / den)
    o_ref[...] = jnp.concatenate(outs, axis=-1).astype(BF16)


def _na_attn(z, cache_k, cache_v, bias, layer):
    lat_q0 = T_CTX // NA_Q
    lat_s0 = T_CTX // DEC_SEQ
    qc, kc, vc = COL_NA_Q // LANES, COL_NA_K // LANES, COL_NA_V // LANES

    def kind(rb):
        return jnp.where(rb == 0, 0, jnp.where(rb == NA_RB - 1, 2, 1))

    ctx_spec = pl.BlockSpec((None, None, 2, PAST_LEN, NA_DIM), lambda b, hp, rb: (b, layer, hp, 0, 0))
    return pl.pallas_call(
        _na_kernel,
        grid=(DEC_BATCH, NA_HEADS // 2, NA_RB),
        in_specs=[pl.BlockSpec((NA_Q, LANES), lambda b, hp, rb: (lat_q0 + b * NA_RB + rb, qc + hp)),
                  pl.BlockSpec((DEC_SEQ, LANES), lambda b, hp, rb: (lat_s0 + b, kc + hp)),
                  pl.BlockSpec((DEC_SEQ, LANES), lambda b, hp, rb: (lat_s0 + b, vc + hp)),
                  ctx_spec, ctx_spec,
                  pl.BlockSpec((2, None, NA_Q, NA_KEYS),
                               lambda b, hp, rb: (layer * (NA_HEADS // 2) + hp, kind(rb), 0, 0))],
        out_specs=pl.BlockSpec((NA_Q, LANES), lambda b, hp, rb: (b * NA_RB + rb, hp)),
        out_shape=jax.ShapeDtypeStruct((T_LAT, NA_WIDTH), BF16),
        compiler_params=_cparams(("arbitrary", "arbitrary", "arbitrary")),
        name="nbr_attn",
    )(z, z, z, cache_k, cache_v, bias)


def _rope_tables():
    n_freq = RET_DIM // 4
    t = np.arange(DEC_SEQ)
    inv = jnp.asarray(ROPE_BASE, F32) ** (-jnp.arange(n_freq, dtype=F32) / n_freq)
    ang_r = jnp.asarray(t // GRID_W, F32)[:, None] * inv[None, :]
    ang_c = jnp.asarray(t % GRID_W, F32)[:, None] * inv[None, :]
    cos = jnp.concatenate([jnp.cos(ang_r)] * 2 + [jnp.cos(ang_c)] * 2, axis=-1)
    sin = jnp.concatenate([-jnp.sin(ang_r), jnp.sin(ang_r), -jnp.sin(ang_c), jnp.sin(ang_c)], axis=-1)
    return jnp.tile(cos, (1, RET_HEADS)), jnp.tile(sin, (1, RET_HEADS))


def _ret_kernel(seq, latent, *refs):
    if latent:
        (lg_ref, z_ref, gn_ref, cos_ref, sin_ref, s0f_ref, s0b_ref, y_ref,
         q_s, k_s, kv_s, st_s) = refs
    else:
        lg_ref, z_ref, gn_ref, y_ref, sf_ref, sb_ref, q_s, k_s, kv_s, st_s = refs
    nc = seq // RET_CHUNK
    ch, hd = RET_CHUNK, RET_DIM
    half = RET_DIM // 4

    row = lax.broadcasted_iota(jnp.int32, (ch, ch), 0).astype(F32)
    col = lax.broadcasted_iota(jnp.int32, (ch, ch), 1).astype(F32)
    pos = lax.broadcasted_iota(jnp.int32, (ch, hd), 0).astype(F32)
    decay, q_dec, k_dec, c_dec_f, c_dec_b = [], [], [], [], []
    for h in range(RET_HEADS):
        lf, lb = lg_ref[0, h], lg_ref[1, h]
        d_f = jnp.where(row >= col, jnp.exp(jnp.maximum(row - col, 0.0) * lf), 0.0)
        d_b = jnp.where(col >= row, jnp.exp(jnp.maximum(col - row, 0.0) * lb), 0.0)
        decay.append(d_f + d_b)
        q_dec.append(jnp.concatenate([jnp.exp((pos + 1.0) * lf), jnp.exp((ch - pos) * lb)], axis=-1))
        k_dec.append(jnp.concatenate([jnp.exp((ch - 1.0 - pos) * lf), jnp.exp(pos * lb)], axis=-1))
        c_dec_f.append(jnp.exp(jnp.zeros((hd, hd), F32) + ch * lf))
        c_dec_b.append(jnp.exp(jnp.zeros((hd, hd), F32) + ch * lb))

    if latent:
        lane = lax.broadcasted_iota(jnp.int32, (ch, RET_WIDTH), 1)
        first_half = (lane % (2 * half)) < half

    def rope(x, base):
        if not latent:
            return x
        swapped = jnp.where(first_half, pltpu.roll(x, RET_WIDTH - half, 1), pltpu.roll(x, half, 1))
        return x * cos_ref[pl.ds(base, ch), :] + swapped * sin_ref[pl.ds(base, ch), :]

    def pass1(n, carry):
        base = pl.multiple_of(n * ch, ch)
        zc = z_ref[pl.ds(base, ch), :]
        q = rope(zc[:, 0:RET_WIDTH].astype(F32), base)
        k = rope(zc[:, RET_WIDTH:2 * RET_WIDTH].astype(F32) * (RET_DIM ** -0.5), base)
        q_s[pl.ds(base, ch), :] = q.astype(BF16)
        k_s[pl.ds(base, ch), :] = k.astype(BF16)
        v = zc[:, 2 * RET_WIDTH:3 * RET_WIDTH]
        for h in range(RET_HEADS):
            cols = slice(h * hd, (h + 1) * hd)
            kh = k[:, cols]
            k2 = (jnp.concatenate([kh, kh], axis=-1) * k_dec[h]).astype(BF16)
            kv_s[n, h] = lax.dot_general(k2, v[:, cols], (((0,), (0,)), ((), ())),
                                         preferred_element_type=F32)
        return carry

    lax.fori_loop(0, nc, pass1, 0)

    for h in range(RET_HEADS):
        if latent:
            s_f, s_b = s0f_ref[h], s0b_ref[h]
        else:
            s_f = s_b = jnp.zeros((hd, hd), F32)

        def fwd(n, s, h=h):
            st_s[n, h, 0:hd, :] = s
            return c_dec_f[h] * s + kv_s[n, h, 0:hd, :]

        def bwd(i, s, h=h):
            n = nc - 1 - i
            st_s[n, h, hd:2 * hd, :] = s
            return c_dec_b[h] * s + kv_s[n, h, hd:2 * hd, :]

        s_f = lax.fori_loop(0, nc, fwd, s_f)
        s_b = lax.fori_loop(0, nc, bwd, s_b)
        if not latent:
            sf_ref[h] = s_f
            sb_ref[h] = s_b

    def pass3(n, carry):
        base = pl.multiple_of(n * ch, ch)
        zc = z_ref[pl.ds(base, ch), :]
        q = q_s[pl.ds(base, ch), :]
        k = k_s[pl.ds(base, ch), :]
        v = zc[:, 2 * RET_WIDTH:3 * RET_WIDTH]
        gate = zc[:, 3 * RET_WIDTH:4 * RET_WIDTH].astype(F32)
        outs = []
        for h in range(RET_HEADS):
            cols = slice(h * hd, (h + 1) * hd)
            qh = q[:, cols]
            s = _dot_nt(qh, k[:, cols]) * decay[h]
            o = jnp.dot(s.astype(BF16), v[:, cols], preferred_element_type=F32)
            qf = qh.astype(F32)
            q2 = (jnp.concatenate([qf, qf], axis=-1) * q_dec[h]).astype(BF16)
            o = o + jnp.dot(q2, st_s[n, h].astype(BF16), preferred_element_type=F32)
            mu = jnp.mean(o, axis=-1, keepdims=True)
            d = o - mu
            var = jnp.mean(d * d, axis=-1, keepdims=True)
            outs.append(d * lax.rsqrt(var + EPS))
        nrm = jnp.concatenate(outs, axis=-1)
        y_ref[pl.ds(base, ch), :] = (nrm * gn_ref[...] * (gate * _sigmoid(gate))).astype(BF16)
        return carry

    lax.fori_loop(0, nc, pass3, 0)


def _retention(z, lg, gn_g, latent, layer=None, rope=None, s0_f=None, s0_b=None):
    seq = DEC_SEQ if latent else SEQ
    nseq = DEC_BATCH if latent else BATCH
    nc = seq // RET_CHUNK
    row0 = (T_CTX // DEC_SEQ) if latent else 0
    cb = COL_RET // (4 * RET_WIDTH)
    in_specs = [pl.BlockSpec(memory_space=pltpu.SMEM),
                pl.BlockSpec((seq, 4 * RET_WIDTH), lambda s: (row0 + s, cb)),
                _full_spec((1, RET_WIDTH))]
    args = [lg, z, gn_g]
    state_shape = jax.ShapeDtypeStruct((nseq, RET_HEADS, RET_DIM, RET_DIM), F32)
    y_spec = pl.BlockSpec((seq, RET_WIDTH), lambda s: (s, 0))
    y_shape = jax.ShapeDtypeStruct((nseq * seq, RET_WIDTH), BF16)
    if latent:
        st_spec = pl.BlockSpec((None, None, RET_HEADS, RET_DIM, RET_DIM), lambda s: (s, layer, 0, 0, 0))
        in_specs += [_full_spec((seq, RET_WIDTH)), _full_spec((seq, RET_WIDTH)), st_spec, st_spec]
        args += [rope[0], rope[1], s0_f, s0_b]
        out_specs, out_shape = y_spec, y_shape
    else:
        so_spec = pl.BlockSpec((None, RET_HEADS, RET_DIM, RET_DIM), lambda s: (s, 0, 0, 0))
        out_specs, out_shape = [y_spec, so_spec, so_spec], [y_shape, state_shape, state_shape]
    return pl.pallas_call(
        functools.partial(_ret_kernel, seq, latent),
        grid=(nseq,),
        in_specs=in_specs,
        out_specs=out_specs,
        out_shape=out_shape,
        scratch_shapes=[pltpu.VMEM((seq, RET_WIDTH), BF16), pltpu.VMEM((seq, RET_WIDTH), BF16),
                        pltpu.VMEM((nc, RET_HEADS, 2 * RET_DIM, RET_DIM), F32),
                        pltpu.VMEM((nc, RET_HEADS, 2 * RET_DIM, RET_DIM), F32)],
        compiler_params=_cparams(("arbitrary",)),
        name="retention_lat" if latent else "retention_ctx",
    )(*args)


def _route(logits):
    lane = lax.broadcasted_iota(jnp.int32, logits.shape, 1)
    lane_f = lane.astype(F32)
    big = float(ROUTE_COLS)
    neg = -jnp.inf
    is_grp = lane < N_GROUPS
    gl = jnp.where(is_grp, logits, neg)
    gmax = jnp.max(gl, axis=-1, keepdims=True)
    grp = jnp.min(jnp.where(gl == gmax, lane_f, big), axis=-1, keepdims=True)
    p_grp = 1.0 / jnp.sum(jnp.exp(gl - gmax), axis=-1, keepdims=True)
    e_f = lane_f - N_GROUPS
    lo = grp * EXPERTS_PER_GROUP
    in_grp = (e_f >= lo) & (e_f < lo + EXPERTS_PER_GROUP)
    el = jnp.where(in_grp, logits, neg)
    m1 = jnp.max(el, axis=-1, keepdims=True)
    i1 = jnp.min(jnp.where(el == m1, lane_f, big), axis=-1, keepdims=True)
    el2 = jnp.where(lane_f == i1, neg, el)
    m2 = jnp.max(el2, axis=-1, keepdims=True)
    i2 = jnp.min(jnp.where(el2 == m2, lane_f, big), axis=-1, keepdims=True)
    t = jnp.exp(m2 - m1)
    g1 = p_grp / (1.0 + t)
    g2 = p_grp * t / (1.0 + t)
    rows = logits.shape[0]
    oh1, oh2 = lane_f == i1, lane_f == i2
    oh = jnp.where(oh1 | oh2, 1.0, 0.0)
    tri = (lax.broadcasted_iota(jnp.int32, (rows, rows), 0)
           > lax.broadcasted_iota(jnp.int32, (rows, rows), 1))
    rank = jnp.dot(jnp.where(tri, 1.0, 0.0).astype(BF16), oh.astype(BF16), preferred_element_type=F32)
    tiles = jnp.floor((jnp.sum(oh, axis=0, keepdims=True) + (SUBLANES - 1)) * (1.0 / SUBLANES))
    upper = (lax.broadcasted_iota(jnp.int32, (ROUTE_COLS, ROUTE_COLS), 0)
             < lax.broadcasted_iota(jnp.int32, (ROUTE_COLS, ROUTE_COLS), 1))
    start = SUBLANES * jnp.dot(jnp.broadcast_to(tiles, (SUBLANES, ROUTE_COLS)).astype(BF16),
                               jnp.where(upper, 1.0, 0.0).astype(BF16),
                               preferred_element_type=F32)[0:1, :]
    pos = start + rank
    p1 = jnp.sum(jnp.where(oh1, pos, 0.0), axis=-1, keepdims=True)
    p2 = jnp.sum(jnp.where(oh2, pos, 0.0), axis=-1, keepdims=True)
    out = jnp.zeros(logits.shape, F32)
    for k, val in enumerate((i1 - N_GROUPS, i2 - N_GROUPS, g1, g2, p1, p2)):
        out = jnp.where(lane == k, val, out)
    return out, SUBLANES * tiles


def _outproj_kernel(ycc, ycl, ync, ynl, yrc, yrl, x_ref, mod_ref, g_ref, w_ref, wr_ref, br_ref,
                    xo_ref, xs_ref, r_ref, seg_ref):
    is_ctx = pl.program_id(0) < NB_CTX
    yc = jnp.where(is_ctx, ycc[...], ycl[...])
    yn = jnp.where(is_ctx, ync[...], ynl[...])
    yr = jnp.where(is_ctx, yrc[...], yrl[...])
    y = (jnp.dot(yc, w_ref[0:CONV_CH, :], preferred_element_type=F32)
         + jnp.dot(yn, w_ref[CONV_CH:CONV_CH + NA_WIDTH, :], preferred_element_type=F32)
         + jnp.dot(yr, w_ref[CONV_CH + NA_WIDTH:, :], preferred_element_type=F32))
    x = x_ref[...] + mod_ref[2:3, :] * y
    xo_ref[...] = x
    h = _norm_mod(x, g_ref[...], mod_ref[3:4, :], mod_ref[4:5, :])
    h_hi = h.astype(BF16)
    h_lo = (h - h_hi.astype(F32)).astype(BF16)
    hw = jnp.dot(h_hi, wr_ref[...], preferred_element_type=F32)
    logits = (hw[:, :ROUTE_COLS] + hw[:, ROUTE_COLS:]
              + jnp.dot(h_lo, wr_ref[:, :ROUTE_COLS], preferred_element_type=F32) + br_ref[...])
    route, seg = _route(logits)
    r_ref[...] = route
    seg_ref[...] = jnp.broadcast_to(seg, seg_ref.shape)
    sel = _slot_onehot(route, 0) | _slot_onehot(route, 1)
    xs_ref[...] = lax.dot_general(jnp.where(sel, 1.0, 0.0).astype(BF16), h_hi, (((0,), (0,)), ((), ())),
                                  preferred_element_type=F32)


def _outproj(y_conv, y_na, y_ret, x, mod, g, w_bf16, w_route, b_route):
    return pl.pallas_call(
        _outproj_kernel,
        grid=(NB_ALL,),
        in_specs=(_ctx_lat_specs(CONV_CH) + _ctx_lat_specs(NA_WIDTH) + _ctx_lat_specs(RET_WIDTH)
                  + [_tok_spec(D_MODEL), _mod_spec(), _full_spec((1, D_MODEL)),
                     _full_spec((D_MODEL, D_MODEL)), _full_spec((D_MODEL, 2 * ROUTE_COLS)),
                     _full_spec((1, ROUTE_COLS))]),
        out_specs=[_tok_spec(D_MODEL), pl.BlockSpec((MOE_LC, D_MODEL), lambda i: (i, 0)),
                   _tok_spec(ROUTE_COLS), pl.BlockSpec((None, SUBLANES, ROUTE_COLS), lambda i: (i, 0, 0))],
        out_shape=[jax.ShapeDtypeStruct((T_ALL, D_MODEL), F32),
                   jax.ShapeDtypeStruct((NB_ALL * MOE_LC, D_MODEL), F32),
                   jax.ShapeDtypeStruct((T_ALL, ROUTE_COLS), F32),
                   jax.ShapeDtypeStruct((NB_ALL, SUBLANES, ROUTE_COLS), F32)],
        compiler_params=_cparams(("arbitrary",)),
        name="outproj_route",
    )(y_conv[0], y_conv[1], y_na[0], y_na[1], y_ret[0], y_ret[1], x, mod, g, w_bf16, w_route, b_route)


def _dispatch_tables(seg):
    seg_len = seg[:, 0, N_GROUPS:N_GROUPS + N_EXPERTS].astype(jnp.int32)
    experts = jnp.arange(N_EXPERTS, dtype=jnp.int32)
    in_chunk = jnp.cumsum(seg_len, axis=1) - seg_len
    seg_row = in_chunk + MOE_LC * jnp.arange(NB_ALL, dtype=jnp.int32)[:, None]
    seg_off = jnp.cumsum(seg_len, axis=0) - seg_len
    rows_e = jnp.sum(seg_len, axis=0)
    chunk_rows = jnp.sum(seg_len, axis=1)
    nblk = (rows_e + MOE_BLK - 1) // MOE_BLK
    blk_end = jnp.cumsum(nblk)
    blk_start = blk_end - nblk
    blk = jnp.arange(MOE_NBLK, dtype=jnp.int32)
    n_active = blk_end[-1]
    blk_e = jnp.minimum(jnp.sum((blk_end[None, :] <= jnp.minimum(blk, n_active - 1)[:, None]).astype(jnp.int32),
                                axis=-1), N_EXPERTS - 1)
    mine = blk_e[:, None] == experts[None, :]
    blk_lo = (blk - jnp.sum(jnp.where(mine, blk_start[None, :], 0), axis=-1)) * MOE_BLK
    left = jnp.sum(jnp.where(mine, rows_e[None, :], 0), axis=-1) - blk_lo
    blk_nv = jnp.where(blk < n_active, jnp.clip(left, 0, MOE_BLK), 0).astype(jnp.int32)
    return (blk_e, blk_lo.astype(jnp.int32), blk_nv, seg_off.reshape(-1), seg_len.reshape(-1),
            seg_row.reshape(-1), chunk_rows)


def _moe_kernel(blk_e, blk_lo, blk_nv, seg_off, seg_len, seg_row, chunk_rows,
                xs_hbm, w1_ref, w3_ref, w2_ref, ys_hbm, xbuf, obuf, zeros, w1b, w3b, w2b, gsem, ssem, zsem):
    i = pl.program_id(0)
    last = pl.num_programs(0) - 1
    slot = i % 2

    def tiles(v):
        return pl.multiple_of(v, SUBLANES)

    def for_segments(blk, fn):
        lo = blk_lo[blk]
        hi = lo + blk_nv[blk]

        def body(c, carry):
            k = c * N_EXPERTS + blk_e[blk]
            s_lo = seg_off[k]
            a = jnp.maximum(s_lo, lo)
            n = jnp.minimum(s_lo + seg_len[k], hi) - a

            @pl.when(n > 0)
            def _():
                fn(tiles(seg_row[k] + a - s_lo), tiles(a - lo), tiles(n))

            return carry

        @pl.when(blk_nv[blk] > 0)
        def _():
            lax.fori_loop(0, NB_ALL, body, 0)

    def start_gathers(blk, s):
        for_segments(blk, lambda src, dst, n: pltpu.make_async_copy(
            xs_hbm.at[pl.ds(src, n)], xbuf.at[s, pl.ds(dst, n)], gsem.at[s]).start())

    def start_scatters(blk, s):
        for_segments(blk, lambda dst, src, n: pltpu.make_async_copy(
            obuf.at[s, pl.ds(src, n)], ys_hbm.at[pl.ds(dst, n)], ssem.at[s]).start())

    def wait_rows(blk, s, sem):
        n = tiles(blk_nv[blk])

        @pl.when(n > 0)
        def _():
            pltpu.make_async_copy(xs_hbm.at[pl.ds(0, n)], xbuf.at[s, pl.ds(0, n)], sem.at[s]).wait()

    @pl.when(i == 0)
    def _():
        xbuf[...] = jnp.zeros_like(xbuf)
        zeros[...] = jnp.zeros_like(zeros)

        def tail(c):
            n = tiles(MOE_LC - chunk_rows[c])
            return n, pltpu.make_async_copy(zeros.at[pl.ds(0, n)],
                                            ys_hbm.at[pl.ds(tiles(c * MOE_LC + chunk_rows[c]), n)], zsem)

        def fill(c, carry):
            n, copy = tail(c)
            pl.when(n > 0)(copy.start)
            return carry

        def drain(c, carry):
            n, copy = tail(c)
            pl.when(n > 0)(copy.wait)
            return carry

        lax.fori_loop(0, NB_ALL, fill, 0)
        lax.fori_loop(0, NB_ALL, drain, 0)
        start_gathers(0, 0)

    @pl.when(i < last)
    def _():
        start_gathers(i + 1, 1 - slot)

    @pl.when(i >= 2)
    def _():
        wait_rows(i - 2, slot, ssem)

    @pl.when(blk_nv[i] > 0)
    def _():
        @pl.when((i == 0) | (blk_e[i] != blk_e[jnp.maximum(i - 1, 0)]))
        def _():
            w1b[...] = w1_ref[...].astype(BF16)
            w3b[...] = w3_ref[...].astype(BF16)
            w2b[...] = w2_ref[...].astype(BF16)

        wait_rows(i, slot, gsem)
        xb = xbuf[slot].astype(BF16)
        a = jnp.dot(xb, w1b[...], preferred_element_type=F32)
        b = jnp.dot(xb, w3b[...], preferred_element_type=F32)
        mid = (a * _sigmoid(a) * b).astype(BF16)
        obuf[slot] = jnp.dot(mid, w2b[...], preferred_element_type=F32)
        start_scatters(i, slot)

    @pl.when(i == last)
    def _():
        wait_rows(i - 1, 1 - slot, ssem)
        wait_rows(i, slot, ssem)


def _moe(xs, w1, w3, w2, layer, blk_e, blk_lo, blk_nv, seg_off, seg_len, seg_row, chunk_rows):
    def w_spec(rows, cols):
        return pl.BlockSpec((None, None, rows, cols), lambda i, be, *_: (layer, be[i], 0, 0))

    grid_spec = pltpu.PrefetchScalarGridSpec(
        num_scalar_prefetch=7,
        grid=(MOE_NBLK,),
        in_specs=[pl.BlockSpec(memory_space=pl.ANY), w_spec(D_MODEL, D_EXPERT),
                  w_spec(D_MODEL, D_EXPERT), w_spec(D_EXPERT, D_MODEL)],
        out_specs=pl.BlockSpec(memory_space=pl.ANY),
        scratch_shapes=[pltpu.VMEM((2, MOE_BLK, D_MODEL), F32), pltpu.VMEM((2, MOE_BLK, D_MODEL), F32),
                        pltpu.VMEM((MOE_LC - 2 * TM, D_MODEL), F32),
                        pltpu.VMEM((D_MODEL, D_EXPERT), BF16), pltpu.VMEM((D_MODEL, D_EXPERT), BF16),
                        pltpu.VMEM((D_EXPERT, D_MODEL), BF16),
                        pltpu.SemaphoreType.DMA((2,)), pltpu.SemaphoreType.DMA((2,)),
                        pltpu.SemaphoreType.DMA])
    return pl.pallas_call(
        _moe_kernel,
        grid_spec=grid_spec,
        out_shape=jax.ShapeDtypeStruct((NB_ALL * MOE_LC, D_MODEL), F32),
        compiler_params=_cparams(("arbitrary",)),
        name="moe_experts",
    )(blk_e, blk_lo, blk_nv, seg_off, seg_len, seg_row, chunk_rows, xs, w1, w3, w2)


def _final_kernel(x_ref, ys_ref, r_ref, mod_ref, g_ref, o_ref):
    x = _moe_residual(x_ref, ys_ref, r_ref, mod_ref)
    ms = jnp.mean(x * x, axis=-1, keepdims=True)
    o_ref[...] = x * lax.rsqrt(ms + EPS) * g_ref[...]


def _final(x, ys, route, mod, g, block0, nblocks):
    return pl.pallas_call(
        _final_kernel,
        grid=(nblocks,),
        in_specs=[pl.BlockSpec((TM, D_MODEL), lambda i: (block0 + i, 0)),
                  pl.BlockSpec((MOE_LC, D_MODEL), lambda i: (block0 + i, 0)),
                  pl.BlockSpec((TM, ROUTE_COLS), lambda i: (block0 + i, 0)),
                  pl.BlockSpec((None, 6, D_MODEL), lambda i: (_cond_row(block0 + i), 0, 0)),
                  _full_spec((1, D_MODEL))],
        out_specs=_tok_spec(D_MODEL),
        out_shape=jax.ShapeDtypeStruct((nblocks * TM, D_MODEL), F32),
        compiler_params=_cparams(("arbitrary",)),
        name="final_norm",
    )(x, ys, route, mod, g)


def kernel(x_prompt, x_sample, c, cache_k, cache_v, state_ret_f, state_ret_b, c_ctx, w_ada, b_ada, norm1_g, norm2_g, w_in, w_out, conv_w, conv_b, conv_ln_g, conv_ln_b, na_rpb, ret_lg_f, ret_lg_b, ret_gn_g, w_route_g, b_route_g, w_route_e, b_route_e, w1, w3, w2, final_g):
    cv = jnp.zeros((COND_ROWS, D_MODEL), F32).at[0].set(c_ctx).at[1:N_COND].set(c)
    mods = _ada(cv, w_ada, b_ada).reshape(DEPTH, COND_ROWS, 6, D_MODEL)
    w_in_b = w_in.astype(BF16)
    w_out_b = w_out.astype(BF16)
    pad = ROUTE_COLS - N_GROUPS - N_EXPERTS
    w_route = jnp.pad(jnp.concatenate([w_route_g, w_route_e], axis=-1), ((0, 0), (0, 0), (0, pad)))
    b_route = jnp.pad(jnp.concatenate([b_route_g, b_route_e], axis=-1), ((0, 0), (0, pad)))
    w_route_hi = w_route.astype(BF16)
    w_route_lo = (w_route - w_route_hi.astype(F32)).astype(BF16)
    w_route = jnp.concatenate([w_route_hi, w_route_lo], axis=-1)
    na_bias = _na_bias_tables(na_rpb)
    rope = _rope_tables()
    lg = jnp.stack([ret_lg_f, ret_lg_b], axis=1)

    x_ctx = x_prompt.reshape(T_CTX, D_MODEL)
    x_lat = x_sample.reshape(T_LAT, D_MODEL)
    x = y = route = None
    k_list, v_list, sf_list, sb_list = [], [], [], []
    for l in range(DEPTH):
        g1 = norm1_g[l].reshape(1, D_MODEL)
        if l == 0:
            z, x = _inproj_first(x_ctx, x_lat, mods[l], g1, w_in_b[l])
        else:
            z, x = _inproj_next(x, y, route, mods[l - 1], mods[l], g1, w_in_b[l])
        conv_args = (conv_w[l], conv_b[l].reshape(1, -1), conv_ln_g[l].reshape(1, -1),
                     conv_ln_b[l].reshape(1, -1))
        yc_c = _conv(z, 0, BATCH, SEQ, *conv_args)
        yc_l = _conv(z, T_CTX // DEC_SEQ, DEC_BATCH, DEC_SEQ, *conv_args)
        yn_c, k_l, v_l = _ctx_attn(z)
        yn_l = _na_attn(z, cache_k, cache_v, na_bias, l)
        gn = ret_gn_g[l].reshape(1, RET_WIDTH)
        yr_c, sf_l, sb_l = _retention(z, lg[l], gn, latent=False)
        yr_l = _retention(z, lg[l], gn, latent=True, layer=l, rope=rope,
                          s0_f=state_ret_f, s0_b=state_ret_b)
        x, xs, route, seg = _outproj((yc_c, yc_l), (yn_c, yn_l), (yr_c, yr_l), x, mods[l],
                                     norm2_g[l].reshape(1, D_MODEL), w_out_b[l], w_route[l],
                                     b_route[l].reshape(1, ROUTE_COLS))
        y = _moe(xs, w1, w3, w2, l, *_dispatch_tables(seg))
        k_list.append(k_l)
        v_list.append(v_l)
        sf_list.append(sf_l)
        sb_list.append(sb_l)
    fg = final_g.reshape(1, D_MODEL)
    y_prompt = _final(x, y, route, mods[DEPTH - 1], fg, 0, NB_CTX).reshape(BATCH, SEQ, D_MODEL)
    y_sample = _final(x, y, route, mods[DEPTH - 1], fg, NB_CTX, NB_LAT).reshape(DEC_BATCH, DEC_SEQ, D_MODEL)
    return (y_prompt, y_sample, jnp.stack(k_list, axis=1), jnp.stack(v_list, axis=1),
            jnp.stack(sf_list, axis=1), jnp.stack(sb_list, axis=1))
```

```python
import functools

import numpy as np
import jax
import jax.numpy as jnp
from jax import lax
from jax.experimental import pallas as pl
from jax.experimental.pallas import tpu as pltpu

D_MODEL = 1024
BATCH = 32
SEQ = 256
DEPTH = 2
DEC_BATCH = 4
DEC_SEQ = 4096
PAST_LEN = 512
GRID_W = 64
GRID_H = DEC_SEQ // GRID_W
CONV_CH = 256
CONV_K = 31
NA_HEADS = 8
NA_DIM = 64
NA_WIDTH = NA_HEADS * NA_DIM
NA_KH = 8
NA_KW = 16
RET_HEADS = 4
RET_DIM = 64
RET_WIDTH = RET_HEADS * RET_DIM
RET_CHUNK = 128
ROPE_BASE = 10000.0
N_GROUPS = 4
EXPERTS_PER_GROUP = 8
N_EXPERTS = N_GROUPS * EXPERTS_PER_GROUP
D_EXPERT = 512
IN_COLS = 2 * CONV_CH + 3 * NA_WIDTH + 4 * RET_WIDTH
EPS = 1e-6
NEG_INF = -1e30

F32 = jnp.float32
BF16 = jnp.bfloat16
HIGHEST = lax.Precision.HIGHEST

T_CTX = BATCH * SEQ
T_LAT = DEC_BATCH * DEC_SEQ
T_ALL = T_CTX + T_LAT
N_COND = 1 + DEC_BATCH
COND_ROWS = 8

TM = 512
NB_CTX = T_CTX // TM
NB_LAT = T_LAT // TM
NB_ALL = NB_CTX + NB_LAT
LAT_BLOCKS_PER_REQ = DEC_SEQ // TM

LANES = 128
SUBLANES = 8
ROUTE_COLS = LANES

COL_CONV = 0
COL_NA_Q = 2 * CONV_CH
COL_NA_K = COL_NA_Q + NA_WIDTH
COL_NA_V = COL_NA_K + NA_WIDTH
COL_RET = COL_NA_V + NA_WIDTH

NA_ROWS = 8
NA_Q = NA_ROWS * GRID_W
NA_KROWS = 2 * NA_ROWS
NA_KEYS = NA_KROWS * GRID_W
NA_RB = GRID_H // NA_ROWS

MOE_BLK = 512
MOE_LC = -(-(2 * TM + N_EXPERTS * (SUBLANES - 1)) // LANES) * LANES
MOE_NBLK = -(-(NB_ALL * MOE_LC) // MOE_BLK) + N_EXPERTS
N_SEG = NB_ALL * N_EXPERTS

VMEM_LIMIT = 56 * 1024 * 1024


def _cparams(sem):
    return pltpu.CompilerParams(dimension_semantics=sem, vmem_limit_bytes=VMEM_LIMIT)


def _sigmoid(x):
    return 1.0 / (1.0 + jnp.exp(-x))


def _cond_row(i):
    return jnp.where(i < NB_CTX, 0, 1 + (i - NB_CTX) // LAT_BLOCKS_PER_REQ)


ADA_TN = 1536


def _ada_kernel(cv_ref, w_ref, b_ref, o_ref):
    cv = cv_ref[...]
    s = cv * _sigmoid(cv)
    o_ref[...] = jnp.dot(s, w_ref[...], precision=HIGHEST, preferred_element_type=F32) + b_ref[...]


def _ada(cv, w_ada, b_ada):
    n = 6 * D_MODEL
    return pl.pallas_call(
        _ada_kernel,
        grid=(DEPTH, n // ADA_TN),
        in_specs=[
            pl.BlockSpec((COND_ROWS, D_MODEL), lambda l, j: (0, 0)),
            pl.BlockSpec((None, D_MODEL, ADA_TN), lambda l, j: (l, 0, j)),
            pl.BlockSpec((None, 1, ADA_TN), lambda l, j: (l, 0, j)),
        ],
        out_specs=pl.BlockSpec((None, COND_ROWS, ADA_TN), lambda l, j: (l, 0, j)),
        out_shape=jax.ShapeDtypeStruct((DEPTH, COND_ROWS, n), F32),
        compiler_params=_cparams(("arbitrary", "arbitrary")),
        name="ada_mod",
    )(cv, w_ada, b_ada.reshape(DEPTH, 1, n))


IN_TN = 768


def _norm_mod(x, g, shift, scale):
    ms = jnp.mean(x * x, axis=-1, keepdims=True)
    return (x * lax.rsqrt(ms + EPS) * g) * (1.0 + scale) + shift


def _inproj_body(x, mod_ref, g_ref, w_ref, z_ref):
    h = _norm_mod(x, g_ref[...], mod_ref[0:1, :], mod_ref[1:2, :]).astype(BF16)
    for c in range(IN_COLS // IN_TN):
        cols = slice(c * IN_TN, (c + 1) * IN_TN)
        z_ref[:, cols] = jnp.dot(h, w_ref[:, cols], preferred_element_type=F32).astype(BF16)


def _inproj_first_kernel(xc_ref, xl_ref, mod_ref, g_ref, w_ref, z_ref, xo_ref):
    i = pl.program_id(0)
    x = jnp.where(i < NB_CTX, xc_ref[...], xl_ref[...])
    xo_ref[...] = x
    _inproj_body(x, mod_ref, g_ref, w_ref, z_ref)


def _slot_onehot(route, slot):
    pos = route[:, 4 + slot:5 + slot].astype(jnp.int32)
    return lax.broadcasted_iota(jnp.int32, (route.shape[0], MOE_LC), 1) == pos


def _moe_residual(x_ref, ys_ref, r_ref, mod_ref):
    r = r_ref[...]
    ys = ys_ref[...].astype(BF16)
    y = [jnp.dot(jnp.where(_slot_onehot(r, k), 1.0, 0.0).astype(BF16), ys, preferred_element_type=F32)
         for k in range(2)]
    return x_ref[...] + mod_ref[5:6, :] * (r[:, 2:3] * y[0] + r[:, 3:4] * y[1])


def _inproj_next_kernel(x_ref, ys_ref, r_ref, modp_ref, mod_ref, g_ref, w_ref, z_ref, xo_ref):
    x = _moe_residual(x_ref, ys_ref, r_ref, modp_ref)
    xo_ref[...] = x
    _inproj_body(x, mod_ref, g_ref, w_ref, z_ref)


def _tok_spec(cols):
    return pl.BlockSpec((TM, cols), lambda i: (i, 0))


def _mod_spec():
    return pl.BlockSpec((None, 6, D_MODEL), lambda i: (_cond_row(i), 0, 0))


def _full_spec(shape):
    return pl.BlockSpec(shape, lambda i: (0,) * len(shape))


def _ctx_lat_specs(cols):
    return [pl.BlockSpec((TM, cols), lambda i: (jnp.minimum(i, NB_CTX - 1), 0)),
            pl.BlockSpec((TM, cols), lambda i: (jnp.maximum(i - NB_CTX, 0), 0))]


def _inproj_first(x_ctx, x_lat, mod, g, w_bf16):
    return pl.pallas_call(
        _inproj_first_kernel,
        grid=(NB_ALL,),
        in_specs=_ctx_lat_specs(D_MODEL) + [_mod_spec(), _full_spec((1, D_MODEL)),
                                            _full_spec((D_MODEL, IN_COLS))],
        out_specs=[_tok_spec(IN_COLS), _tok_spec(D_MODEL)],
        out_shape=[jax.ShapeDtypeStruct((T_ALL, IN_COLS), BF16),
                   jax.ShapeDtypeStruct((T_ALL, D_MODEL), F32)],
        compiler_params=_cparams(("arbitrary",)),
        name="inproj_first",
    )(x_ctx, x_lat, mod, g, w_bf16)


def _inproj_next(x, ys, route, mod_prev, mod, g, w_bf16):
    return pl.pallas_call(
        _inproj_next_kernel,
        grid=(NB_ALL,),
        in_specs=[_tok_spec(D_MODEL),
                  pl.BlockSpec((MOE_LC, D_MODEL), lambda i: (i, 0)),
                  _tok_spec(ROUTE_COLS),
                  _mod_spec(), _mod_spec(), _full_spec((1, D_MODEL)),
                  _full_spec((D_MODEL, IN_COLS))],
        out_specs=[_tok_spec(IN_COLS), _tok_spec(D_MODEL)],
        out_shape=[jax.ShapeDtypeStruct((T_ALL, IN_COLS), BF16),
                   jax.ShapeDtypeStruct((T_ALL, D_MODEL), F32)],
        compiler_params=_cparams(("arbitrary",)),
        name="inproj_next",
    )(x, ys, route, mod_prev, mod, g, w_bf16)


CONV_PAD = 16
CONV_CHUNK = 64


def _conv_kernel(seq, z_ref, w_ref, b_ref, g_ref, be_ref, o_ref, upad_ref):
    zeros = jnp.zeros((CONV_PAD, CONV_CH), F32)
    upad_ref[0:CONV_PAD, :] = zeros
    upad_ref[seq + CONV_PAD:seq + 2 * CONV_PAD, :] = zeros

    def glu(ci, carry):
        base = pl.multiple_of(ci * 256, 256)
        zc = z_ref[pl.ds(base, 256), :].astype(F32)
        upad_ref[pl.ds(base + CONV_PAD, 256), :] = zc[:, :CONV_CH] * _sigmoid(zc[:, CONV_CH:])
        return carry

    lax.fori_loop(0, seq // 256, glu, 0)

    shift = CONV_PAD - CONV_K // 2

    def chunk(ci, carry):
        base = pl.multiple_of(ci * CONV_CHUNK, CONV_CHUNK)
        win = upad_ref[pl.ds(base, CONV_CHUNK + 2 * CONV_PAD), :]
        acc = jnp.zeros((CONV_CHUNK, CONV_CH), F32)
        for k in range(CONV_K):
            acc = acc + w_ref[k:k + 1, :] * win[k + shift:k + shift + CONV_CHUNK, :]
        acc = acc + b_ref[...]
        mu = jnp.mean(acc, axis=-1, keepdims=True)
        d = acc - mu
        var = jnp.mean(d * d, axis=-1, keepdims=True)
        n = d * lax.rsqrt(var + EPS) * g_ref[...] + be_ref[...]
        o_ref[pl.ds(base, CONV_CHUNK), :] = (n * _sigmoid(n)).astype(BF16)
        return carry

    lax.fori_loop(0, seq // CONV_CHUNK, chunk, 0)


def _conv(z, row_block0, nseq, seq, w, b, g, be):
    return pl.pallas_call(
        functools.partial(_conv_kernel, seq),
        grid=(nseq,),
        in_specs=[pl.BlockSpec((seq, 2 * CONV_CH), lambda s: (row_block0 + s, 0)),
                  _full_spec((CONV_K, CONV_CH)), _full_spec((1, CONV_CH)),
                  _full_spec((1, CONV_CH)), _full_spec((1, CONV_CH))],
        out_specs=pl.BlockSpec((seq, CONV_CH), lambda s: (s, 0)),
        out_shape=jax.ShapeDtypeStruct((nseq * seq, CONV_CH), BF16),
        scratch_shapes=[pltpu.VMEM((seq + 2 * CONV_PAD, CONV_CH), F32)],
        compiler_params=_cparams(("arbitrary",)),
        name="conv_seq%d" % seq,
    )(z, w, b, g, be)


def _dot_nt(a, b):
    return lax.dot_general(a, b, (((1,), (1,)), ((), ())), preferred_element_type=F32)


NA_SCALE = NA_DIM ** -0.5
assert NA_SCALE == 2.0 ** round(np.log2(NA_SCALE)), "query pre-scaling assumes a power-of-two scale"


def _ctx_attn_kernel(q_ref, k_ref, v_ref, o_ref, ko_ref, vo_ref):
    scale = NA_SCALE
    outs = []
    for h in range(NA_HEADS):
        cols = slice(h * NA_DIM, (h + 1) * NA_DIM)
        qh, kh, vh = q_ref[:, cols], k_ref[:, cols], v_ref[:, cols]
        ko_ref[h] = kh.astype(F32)
        vo_ref[h] = vh.astype(F32)
        s = _dot_nt(qh, kh) * scale
        m = jnp.max(s, axis=-1, keepdims=True)
        p = jnp.exp(s - m)
        den = jnp.sum(p, axis=-1, keepdims=True)
        o = jnp.dot(p.astype(BF16), vh, preferred_element_type=F32)
        outs.append(o / den)
    o_ref[...] = jnp.concatenate(outs, axis=-1).astype(BF16)


def _ctx_attn(z):
    qb, kb, vb = COL_NA_Q // NA_WIDTH, COL_NA_K // NA_WIDTH, COL_NA_V // NA_WIDTH
    head_shape = jax.ShapeDtypeStruct((BATCH, NA_HEADS, SEQ, NA_DIM), F32)
    head_spec = pl.BlockSpec((None, NA_HEADS, SEQ, NA_DIM), lambda b: (b, 0, 0, 0))
    return pl.pallas_call(
        _ctx_attn_kernel,
        grid=(BATCH,),
        in_specs=[pl.BlockSpec((SEQ, NA_WIDTH), lambda b: (b, qb)),
                  pl.BlockSpec((SEQ, NA_WIDTH), lambda b: (b, kb)),
                  pl.BlockSpec((SEQ, NA_WIDTH), lambda b: (b, vb))],
        out_specs=[pl.BlockSpec((SEQ, NA_WIDTH), lambda b: (b, 0)), head_spec, head_spec],
        out_shape=[jax.ShapeDtypeStruct((T_CTX, NA_WIDTH), BF16), head_shape, head_shape],
        compiler_params=_cparams(("arbitrary",)),
        name="ctx_attn",
    )(z, z, z)


NA_KINDS = (0, NA_ROWS, GRID_H - NA_ROWS)
N_DR = 2 * NA_KH - 1
N_DC = 2 * NA_KW - 1


def _na_row_offset(r0, i, j):
    ks = min(max(r0 - NA_KH // 2, 0), GRID_H - NA_KROWS)
    r, kr = r0 + i, ks + j
    rs = min(max(r - NA_KH // 2, 0), GRID_H - NA_KH)
    return kr - r + NA_KH - 1 if rs <= kr < rs + NA_KH else None


def _na_bias_kernel(rpb_ref, o_ref):
    lh = pl.program_id(0)
    shape = (GRID_W, 2 * GRID_W)
    qc = lax.broadcasted_iota(jnp.int32, shape, 0)
    lane = lax.broadcasted_iota(jnp.int32, shape, 1)
    kc = lane % GRID_W
    dc = jnp.clip(kc - qc, -(NA_KW - 1), NA_KW - 1) + NA_KW - 1
    cs = jnp.clip(qc - NA_KW // 2, 0, GRID_W - NA_KW)
    col_ok = (kc >= cs) & (kc < cs + NA_KW)
    neg = jnp.full(shape, NEG_INF, F32)
    tiles = []
    for dr in range(N_DR):
        base = (lh * N_DR + dr) * N_DC
        val = jnp.zeros(shape, F32)
        for d in range(N_DC):
            val = jnp.where(dc == d, rpb_ref[base + d], val)
        tiles.append(jnp.where(col_ok, val, neg))
    left = lane < GRID_W
    for kind, r0 in enumerate(NA_KINDS):
        for i in range(NA_ROWS):
            for jp in range(NA_KROWS // 2):
                dl, dr_ = _na_row_offset(r0, i, 2 * jp), _na_row_offset(r0, i, 2 * jp + 1)
                tl = neg if dl is None else tiles[dl]
                tr = neg if dr_ is None else tiles[dr_]
                o_ref[kind, i * GRID_W:(i + 1) * GRID_W, jp * 2 * GRID_W:(jp + 1) * 2 * GRID_W] = (
                    jnp.where(left, tl, tr))


def _na_bias_tables(rpb):
    return pl.pallas_call(
        _na_bias_kernel,
        grid=(DEPTH * NA_HEADS,),
        in_specs=[pl.BlockSpec(memory_space=pltpu.SMEM)],
        out_specs=pl.BlockSpec((None, len(NA_KINDS), NA_Q, NA_KEYS), lambda i: (i, 0, 0, 0)),
        out_shape=jax.ShapeDtypeStruct((DEPTH * NA_HEADS, len(NA_KINDS), NA_Q, NA_KEYS), F32),
        compiler_params=_cparams(("arbitrary",)),
        name="nbr_bias",
    )(rpb.reshape(-1))


def _na_kernel(q_ref, k_ref, v_ref, kc_ref, vc_ref, bias_ref, o_ref):
    rb = pl.program_id(2)
    ks = jnp.clip(rb * NA_ROWS - NA_KH // 2, 0, GRID_H - NA_KROWS)
    start = pl.multiple_of(ks * GRID_W, GRID_W)
    q = q_ref[...] * NA_SCALE
    kl = k_ref[pl.ds(start, NA_KEYS), :]
    vl = v_ref[pl.ds(start, NA_KEYS), :]
    outs = []
    for hh in range(2):
        cols = slice(hh * NA_DIM, (hh + 1) * NA_DIM)
        qh = q[:, cols]
        s_loc = _dot_nt(qh, kl[:, cols]) + bias_ref[hh]
        s_ctx = _dot_nt(qh, kc_ref[hh].astype(BF16))
        m = jnp.maximum(jnp.max(s_loc, axis=-1, keepdims=True), jnp.max(s_ctx, axis=-1, keepdims=True))
        p_loc = jnp.exp(s_loc - m)
        p_ctx = jnp.exp(s_ctx - m)
        den = jnp.sum(p_loc, axis=-1, keepdims=True) + jnp.sum(p_ctx, axis=-1, keepdims=True)
        o = (jnp.dot(p_loc.astype(BF16), vl[:, cols], preferred_element_type=F32)
             + jnp.dot(p_ctx.astype(BF16), vc_ref[hh].astype(BF16), preferred_element_type=F32))
        outs.append(o / den)
    o_ref[...] = jnp.concatenate(outs, axis=-1).astype(BF16)


def _na_attn(z, cache_k, cache_v, bias, layer):
    lat_q0 = T_CTX // NA_Q
    lat_s0 = T_CTX // DEC_SEQ
    qc, kc, vc = COL_NA_Q // LANES, COL_NA_K // LANES, COL_NA_V // LANES

    def kind(rb):
        return jnp.where(rb == 0, 0, jnp.where(rb == NA_RB - 1, 2, 1))

    ctx_spec = pl.BlockSpec((None, None, 2, PAST_LEN, NA_DIM), lambda b, hp, rb: (b, layer, hp, 0, 0))
    return pl.pallas_call(
        _na_kernel,
        grid=(DEC_BATCH, NA_HEADS // 2, NA_RB),
        in_specs=[pl.BlockSpec((NA_Q, LANES), lambda b, hp, rb: (lat_q0 + b * NA_RB + rb, qc + hp)),
                  pl.BlockSpec((DEC_SEQ, LANES), lambda b, hp, rb: (lat_s0 + b, kc + hp)),
                  pl.BlockSpec((DEC_SEQ, LANES), lambda b, hp, rb: (lat_s0 + b, vc + hp)),
                  ctx_spec, ctx_spec,
                  pl.BlockSpec((2, None, NA_Q, NA_KEYS),
                               lambda b, hp, rb: (layer * (NA_HEADS // 2) + hp, kind(rb), 0, 0))],
        out_specs=pl.BlockSpec((NA_Q, LANES), lambda b, hp, rb: (b * NA_RB + rb, hp)),
        out_shape=jax.ShapeDtypeStruct((T_LAT, NA_WIDTH), BF16),
        compiler_params=_cparams(("arbitrary", "arbitrary", "arbitrary")),
        name="nbr_attn",
    )(z, z, z, cache_k, cache_v, bias)


def _rope_tables():
    n_freq = RET_DIM // 4
    t = np.arange(DEC_SEQ)
    inv = jnp.asarray(ROPE_BASE, F32) ** (-jnp.arange(n_freq, dtype=F32) / n_freq)
    ang_r = jnp.asarray(t // GRID_W, F32)[:, None] * inv[None, :]
    ang_c = jnp.asarray(t % GRID_W, F32)[:, None] * inv[None, :]
    cos = jnp.concatenate([jnp.cos(ang_r)] * 2 + [jnp.cos(ang_c)] * 2, axis=-1)
    sin = jnp.concatenate([-jnp.sin(ang_r), jnp.sin(ang_r), -jnp.sin(ang_c), jnp.sin(ang_c)], axis=-1)
    return jnp.tile(cos, (1, RET_HEADS)), jnp.tile(sin, (1, RET_HEADS))


def _ret_kernel(seq, latent, *refs):
    if latent:
        (lg_ref, z_ref, gn_ref, cos_ref, sin_ref, s0f_ref, s0b_ref, y_ref,
         q_s, k_s, kv_s, st_s) = refs
    else:
        lg_ref, z_ref, gn_ref, y_ref, sf_ref, sb_ref, q_s, k_s, kv_s, st_s = refs
    nc = seq // RET_CHUNK
    ch, hd = RET_CHUNK, RET_DIM
    half = RET_DIM // 4

    row = lax.broadcasted_iota(jnp.int32, (ch, ch), 0).astype(F32)
    col = lax.broadcasted_iota(jnp.int32, (ch, ch), 1).astype(F32)
    pos = lax.broadcasted_iota(jnp.int32, (ch, hd), 0).astype(F32)
    decay, q_dec, k_dec, c_dec_f, c_dec_b = [], [], [], [], []
    for h in range(RET_HEADS):
        lf, lb = lg_ref[0, h], lg_ref[1, h]
        d_f = jnp.where(row >= col, jnp.exp(jnp.maximum(row - col, 0.0) * lf), 0.0)
        d_b = jnp.where(col >= row, jnp.exp(jnp.maximum(col - row, 0.0) * lb), 0.0)
        decay.append(d_f + d_b)
        q_dec.append(jnp.concatenate([jnp.exp((pos + 1.0) * lf), jnp.exp((ch - pos) * lb)], axis=-1))
        k_dec.append(jnp.concatenate([jnp.exp((ch - 1.0 - pos) * lf), jnp.exp(pos * lb)], axis=-1))
        c_dec_f.append(jnp.exp(jnp.zeros((hd, hd), F32) + ch * lf))
        c_dec_b.append(jnp.exp(jnp.zeros((hd, hd), F32) + ch * lb))

    if latent:
        lane = lax.broadcasted_iota(jnp.int32, (ch, RET_WIDTH), 1)
        first_half = (lane % (2 * half)) < half

    def rope(x, base):
        if not latent:
            return x
        swapped = jnp.where(first_half, pltpu.roll(x, RET_WIDTH - half, 1), pltpu.roll(x, half, 1))
        return x * cos_ref[pl.ds(base, ch), :] + swapped * sin_ref[pl.ds(base, ch), :]

    def pass1(n, carry):
        base = pl.multiple_of(n * ch, ch)
        zc = z_ref[pl.ds(base, ch), :]
        q = rope(zc[:, 0:RET_WIDTH].astype(F32), base)
        k = rope(zc[:, RET_WIDTH:2 * RET_WIDTH].astype(F32) * (RET_DIM ** -0.5), base)
        q_s[pl.ds(base, ch), :] = q.astype(BF16)
        k_s[pl.ds(base, ch), :] = k.astype(BF16)
        v = zc[:, 2 * RET_WIDTH:3 * RET_WIDTH]
        for h in range(RET_HEADS):
            cols = slice(h * hd, (h + 1) * hd)
            kh = k[:, cols]
            k2 = (jnp.concatenate([kh, kh], axis=-1) * k_dec[h]).astype(BF16)
            kv_s[n, h] = lax.dot_general(k2, v[:, cols], (((0,), (0,)), ((), ())),
                                         preferred_element_type=F32)
        return carry

    lax.fori_loop(0, nc, pass1, 0)

    for h in range(RET_HEADS):
        if latent:
            s_f, s_b = s0f_ref[h], s0b_ref[h]
        else:
            s_f = s_b = jnp.zeros((hd, hd), F32)

        def fwd(n, s, h=h):
            st_s[n, h, 0:hd, :] = s
            return c_dec_f[h] * s + kv_s[n, h, 0:hd, :]

        def bwd(i, s, h=h):
            n = nc - 1 - i
            st_s[n, h, hd:2 * hd, :] = s
            return c_dec_b[h] * s + kv_s[n, h, hd:2 * hd, :]

        s_f = lax.fori_loop(0, nc, fwd, s_f)
        s_b = lax.fori_loop(0, nc, bwd, s_b)
        if not latent:
            sf_ref[h] = s_f
            sb_ref[h] = s_b

    def pass3(n, carry):
        base = pl.multiple_of(n * ch, ch)
        zc = z_ref[pl.ds(base, ch), :]
        q = q_s[pl.ds(base, ch), :]
        k = k_s[pl.ds(base, ch), :]
        v = zc[:, 2 * RET_WIDTH:3 * RET_WIDTH]
        gate = zc[:, 3 * RET_WIDTH:4 * RET_WIDTH].astype(F32)
        outs = []
        for h in range(RET_HEADS):
            cols = slice(h * hd, (h + 1) * hd)
            qh = q[:, cols]
            s = _dot_nt(qh, k[:, cols]) * decay[h]
            o = jnp.dot(s.astype(BF16), v[:, cols], preferred_element_type=F32)
            qf = qh.astype(F32)
            q2 = (jnp.concatenate([qf, qf], axis=-1) * q_dec[h]).astype(BF16)
            o = o + jnp.dot(q2, st_s[n, h].astype(BF16), preferred_element_type=F32)
            mu = jnp.mean(o, axis=-1, keepdims=True)
            d = o - mu
            var = jnp.mean(d * d, axis=-1, keepdims=True)
            outs.append(d * lax.rsqrt(var + EPS))
        nrm = jnp.concatenate(outs, axis=-1)
        y_ref[pl.ds(base, ch), :] = (nrm * gn_ref[...] * (gate * _sigmoid(gate))).astype(BF16)
        return carry

    lax.fori_loop(0, nc, pass3, 0)


def _retention(z, lg, gn_g, latent, layer=None, rope=None, s0_f=None, s0_b=None):
    seq = DEC_SEQ if latent else SEQ
    nseq = DEC_BATCH if latent else BATCH
    nc = seq // RET_CHUNK
    row0 = (T_CTX // DEC_SEQ) if latent else 0
    cb = COL_RET // (4 * RET_WIDTH)
    in_specs = [pl.BlockSpec(memory_space=pltpu.SMEM),
                pl.BlockSpec((seq, 4 * RET_WIDTH), lambda s: (row0 + s, cb)),
                _full_spec((1, RET_WIDTH))]
    args = [lg, z, gn_g]
    state_shape = jax.ShapeDtypeStruct((nseq, RET_HEADS, RET_DIM, RET_DIM), F32)
    y_spec = pl.BlockSpec((seq, RET_WIDTH), lambda s: (s, 0))
    y_shape = jax.ShapeDtypeStruct((nseq * seq, RET_WIDTH), BF16)
    if latent:
        st_spec = pl.BlockSpec((None, None, RET_HEADS, RET_DIM, RET_DIM), lambda s: (s, layer, 0, 0, 0))
        in_specs += [_full_spec((seq, RET_WIDTH)), _full_spec((seq, RET_WIDTH)), st_spec, st_spec]
        args += [rope[0], rope[1], s0_f, s0_b]
        out_specs, out_shape = y_spec, y_shape
    else:
        so_spec = pl.BlockSpec((None, RET_HEADS, RET_DIM, RET_DIM), lambda s: (s, 0, 0, 0))
        out_specs, out_shape = [y_spec, so_spec, so_spec], [y_shape, state_shape, state_shape]
    return pl.pallas_call(
        functools.partial(_ret_kernel, seq, latent),
        grid=(nseq,),
        in_specs=in_specs,
        out_specs=out_specs,
        out_shape=out_shape,
        scratch_shapes=[pltpu.VMEM((seq, RET_WIDTH), BF16), pltpu.VMEM((seq, RET_WIDTH), BF16),
                        pltpu.VMEM((nc, RET_HEADS, 2 * RET_DIM, RET_DIM), F32),
                        pltpu.VMEM((nc, RET_HEADS, 2 * RET_DIM, RET_DIM), F32)],
        compiler_params=_cparams(("arbitrary",)),
        name="retention_lat" if latent else "retention_ctx",
    )(*args)


def _route(logits):
    lane = lax.broadcasted_iota(jnp.int32, logits.shape, 1)
    lane_f = lane.astype(F32)
    big = float(ROUTE_COLS)
    neg = -jnp.inf
    is_grp = lane < N_GROUPS
    gl = jnp.where(is_grp, logits, neg)
    gmax = jnp.max(gl, axis=-1, keepdims=True)
    grp = jnp.min(jnp.where(gl == gmax, lane_f, big), axis=-1, keepdims=True)
    p_grp = 1.0 / jnp.sum(jnp.exp(gl - gmax), axis=-1, keepdims=True)
    e_f = lane_f - N_GROUPS
    lo = grp * EXPERTS_PER_GROUP
    in_grp = (e_f >= lo) & (e_f < lo + EXPERTS_PER_GROUP)
    el = jnp.where(in_grp, logits, neg)
    m1 = jnp.max(el, axis=-1, keepdims=True)
    i1 = jnp.min(jnp.where(el == m1, lane_f, big), axis=-1, keepdims=True)
    el2 = jnp.where(lane_f == i1, neg, el)
    m2 = jnp.max(el2, axis=-1, keepdims=True)
    i2 = jnp.min(jnp.where(el2 == m2, lane_f, big), axis=-1, keepdims=True)
    t = jnp.exp(m2 - m1)
    g1 = p_grp / (1.0 + t)
    g2 = p_grp * t / (1.0 + t)
    rows = logits.shape[0]
    oh1, oh2 = lane_f == i1, lane_f == i2
    oh = jnp.where(oh1 | oh2, 1.0, 0.0)
    tri = (lax.broadcasted_iota(jnp.int32, (rows, rows), 0)
           > lax.broadcasted_iota(jnp.int32, (rows, rows), 1))
    rank = jnp.dot(jnp.where(tri, 1.0, 0.0).astype(BF16), oh.astype(BF16), preferred_element_type=F32)
    tiles = jnp.floor((jnp.sum(oh, axis=0, keepdims=True) + (SUBLANES - 1)) * (1.0 / SUBLANES))
    upper = (lax.broadcasted_iota(jnp.int32, (ROUTE_COLS, ROUTE_COLS), 0)
             < lax.broadcasted_iota(jnp.int32, (ROUTE_COLS, ROUTE_COLS), 1))
    start = SUBLANES * jnp.dot(jnp.broadcast_to(tiles, (SUBLANES, ROUTE_COLS)).astype(BF16),
                               jnp.where(upper, 1.0, 0.0).astype(BF16),
                               preferred_element_type=F32)[0:1, :]
    pos = start + rank
    p1 = jnp.sum(jnp.where(oh1, pos, 0.0), axis=-1, keepdims=True)
    p2 = jnp.sum(jnp.where(oh2, pos, 0.0), axis=-1, keepdims=True)
    out = jnp.zeros(logits.shape, F32)
    for k, val in enumerate((i1 - N_GROUPS, i2 - N_GROUPS, g1, g2, p1, p2)):
        out = jnp.where(lane == k, val, out)
    return out, SUBLANES * tiles


def _outproj_kernel(ycc, ycl, ync, ynl, yrc, yrl, x_ref, mod_ref, g_ref, w_ref, wr_ref, br_ref,
                    xo_ref, xs_ref, r_ref, seg_ref):
    is_ctx = pl.program_id(0) < NB_CTX
    yc = jnp.where(is_ctx, ycc[...], ycl[...])
    yn = jnp.where(is_ctx, ync[...], ynl[...])
    yr = jnp.where(is_ctx, yrc[...], yrl[...])
    y = (jnp.dot(yc, w_ref[0:CONV_CH, :], preferred_element_type=F32)
         + jnp.dot(yn, w_ref[CONV_CH:CONV_CH + NA_WIDTH, :], preferred_element_type=F32)
         + jnp.dot(yr, w_ref[CONV_CH + NA_WIDTH:, :], preferred_element_type=F32))
    x = x_ref[...] + mod_ref[2:3, :] * y
    xo_ref[...] = x
    h = _norm_mod(x, g_ref[...], mod_ref[3:4, :], mod_ref[4:5, :])
    h_hi = h.astype(BF16)
    h_lo = (h - h_hi.astype(F32)).astype(BF16)
    hw = jnp.dot(h_hi, wr_ref[...], preferred_element_type=F32)
    logits = (hw[:, :ROUTE_COLS] + hw[:, ROUTE_COLS:]
              + jnp.dot(h_lo, wr_ref[:, :ROUTE_COLS], preferred_element_type=F32) + br_ref[...])
    route, seg = _route(logits)
    r_ref[...] = route
    seg_ref[...] = jnp.broadcast_to(seg, seg_ref.shape)
    sel = _slot_onehot(route, 0) | _slot_onehot(route, 1)
    xs_ref[...] = lax.dot_general(jnp.where(sel, 1.0, 0.0).astype(BF16), h_hi, (((0,), (0,)), ((), ())),
                                  preferred_element_type=F32)


def _outproj(y_conv, y_na, y_ret, x, mod, g, w_bf16, w_route, b_route):
    return pl.pallas_call(
        _outproj_kernel,
        grid=(NB_ALL,),
        in_specs=(_ctx_lat_specs(CONV_CH) + _ctx_lat_specs(NA_WIDTH) + _ctx_lat_specs(RET_WIDTH)
                  + [_tok_spec(D_MODEL), _mod_spec(), _full_spec((1, D_MODEL)),
                     _full_spec((D_MODEL, D_MODEL)), _full_spec((D_MODEL, 2 * ROUTE_COLS)),
                     _full_spec((1, ROUTE_COLS))]),
        out_specs=[_tok_spec(D_MODEL), pl.BlockSpec((MOE_LC, D_MODEL), lambda i: (i, 0)),
                   _tok_spec(ROUTE_COLS), pl.BlockSpec((None, SUBLANES, ROUTE_COLS), lambda i: (i, 0, 0))],
        out_shape=[jax.ShapeDtypeStruct((T_ALL, D_MODEL), F32),
                   jax.ShapeDtypeStruct((NB_ALL * MOE_LC, D_MODEL), F32),
                   jax.ShapeDtypeStruct((T_ALL, ROUTE_COLS), F32),
                   jax.ShapeDtypeStruct((NB_ALL, SUBLANES, ROUTE_COLS), F32)],
        compiler_params=_cparams(("arbitrary",)),
        name="outproj_route",
    )(y_conv[0], y_conv[1], y_na[0], y_na[1], y_ret[0], y_ret[1], x, mod, g, w_bf16, w_route, b_route)


def _dispatch_tables(seg):
    seg_len = seg[:, 0, N_GROUPS:N_GROUPS + N_EXPERTS].astype(jnp.int32)
    experts = jnp.arange(N_EXPERTS, dtype=jnp.int32)
    in_chunk = jnp.cumsum(seg_len, axis=1) - seg_len
    seg_row = in_chunk + MOE_LC * jnp.arange(NB_ALL, dtype=jnp.int32)[:, None]
    seg_off = jnp.cumsum(seg_len, axis=0) - seg_len
    rows_e = jnp.sum(seg_len, axis=0)
    chunk_rows = jnp.sum(seg_len, axis=1)
    nblk = (rows_e + MOE_BLK - 1) // MOE_BLK
    blk_end = jnp.cumsum(nblk)
    blk_start = blk_end - nblk
    blk = jnp.arange(MOE_NBLK, dtype=jnp.int32)
    n_active = blk_end[-1]
    blk_e = jnp.minimum(jnp.sum((blk_end[None, :] <= jnp.minimum(blk, n_active - 1)[:, None]).astype(jnp.int32),
                                axis=-1), N_EXPERTS - 1)
    mine = blk_e[:, None] == experts[None, :]
    blk_lo = (blk - jnp.sum(jnp.where(mine, blk_start[None, :], 0), axis=-1)) * MOE_BLK
    left = jnp.sum(jnp.where(mine, rows_e[None, :], 0), axis=-1) - blk_lo
    blk_nv = jnp.where(blk < n_active, jnp.clip(left, 0, MOE_BLK), 0).astype(jnp.int32)
    off_b = jnp.sum(jnp.where(mine[:, None, :], seg_off[None, :, :], 0), axis=-1)
    end_b = off_b + jnp.sum(jnp.where(mine[:, None, :], seg_len[None, :, :], 0), axis=-1)
    blk_c0 = jnp.sum((end_b <= blk_lo[:, None]).astype(jnp.int32), axis=-1)
    blk_c1 = jnp.sum((off_b < (blk_lo + blk_nv)[:, None]).astype(jnp.int32), axis=-1)
    return (blk_e, blk_lo.astype(jnp.int32), blk_nv, blk_c0, blk_c1, seg_off.reshape(-1),
            seg_len.reshape(-1), seg_row.reshape(-1), chunk_rows)


def _moe_kernel(blk_e, blk_lo, blk_nv, blk_c0, blk_c1, seg_off, seg_len, seg_row, chunk_rows,
                xs_hbm, w1_ref, w3_ref, w2_ref, ys_hbm, xbuf, obuf, zeros, w1b, w3b, w2b, gsem, ssem, zsem):
    i = pl.program_id(0)
    last = pl.num_programs(0) - 1
    slot = i % 2

    def tiles(v):
        return pl.multiple_of(v, SUBLANES)

    def for_segments(blk, fn):
        lo = blk_lo[blk]
        hi = lo + blk_nv[blk]

        def body(c, carry):
            k = c * N_EXPERTS + blk_e[blk]
            s_lo = seg_off[k]
            a = jnp.maximum(s_lo, lo)
            n = jnp.minimum(s_lo + seg_len[k], hi) - a

            @pl.when(n > 0)
            def _():
                fn(tiles(seg_row[k] + a - s_lo), tiles(a - lo), tiles(n))

            return carry

        lax.fori_loop(blk_c0[blk], blk_c1[blk], body, 0)

    def start_gathers(blk, s):
        for_segments(blk, lambda src, dst, n: pltpu.make_async_copy(
            xs_hbm.at[pl.ds(src, n)], xbuf.at[s, pl.ds(dst, n)], gsem.at[s]).start())

    def start_scatters(blk, s):
        for_segments(blk, lambda dst, src, n: pltpu.make_async_copy(
            obuf.at[s, pl.ds(src, n)], ys_hbm.at[pl.ds(dst, n)], ssem.at[s]).start())

    def wait_rows(blk, s, sem):
        n = tiles(blk_nv[blk])

        @pl.when(n > 0)
        def _():
            pltpu.make_async_copy(xs_hbm.at[pl.ds(0, n)], xbuf.at[s, pl.ds(0, n)], sem.at[s]).wait()

    @pl.when(i == 0)
    def _():
        xbuf[...] = jnp.zeros_like(xbuf)
        zeros[...] = jnp.zeros_like(zeros)

        def tail(c):
            n = tiles(MOE_LC - chunk_rows[c])
            return n, pltpu.make_async_copy(zeros.at[pl.ds(0, n)],
                                            ys_hbm.at[pl.ds(tiles(c * MOE_LC + chunk_rows[c]), n)], zsem)

        def fill(c, carry):
            n, copy = tail(c)
            pl.when(n > 0)(copy.start)
            return carry

        def drain(c, carry):
            n, copy = tail(c)
            pl.when(n > 0)(copy.wait)
            return carry

        lax.fori_loop(0, NB_ALL, fill, 0)
        lax.fori_loop(0, NB_ALL, drain, 0)
        start_gathers(0, 0)

    @pl.when(i < last)
    def _():
        start_gathers(i + 1, 1 - slot)

    @pl.when(i >= 2)
    def _():
        wait_rows(i - 2, slot, ssem)

    @pl.when(blk_nv[i] > 0)
    def _():
        @pl.when((i == 0) | (blk_e[i] != blk_e[jnp.maximum(i - 1, 0)]))
        def _():
            w1b[...] = w1_ref[...].astype(BF16)
            w3b[...] = w3_ref[...].astype(BF16)
            w2b[...] = w2_ref[...].astype(BF16)

        wait_rows(i, slot, gsem)
        xb = xbuf[slot].astype(BF16)
        a = jnp.dot(xb, w1b[...], preferred_element_type=F32)
        b = jnp.dot(xb, w3b[...], preferred_element_type=F32)
        mid = (a * _sigmoid(a) * b).astype(BF16)
        obuf[slot] = jnp.dot(mid, w2b[...], preferred_element_type=F32)
        start_scatters(i, slot)

    @pl.when(i == last)
    def _():
        wait_rows(i - 1, 1 - slot, ssem)
        wait_rows(i, slot, ssem)


def _moe(xs, w1, w3, w2, layer, blk_e, blk_lo, blk_nv, blk_c0, blk_c1, seg_off, seg_len, seg_row,
         chunk_rows):
    def w_spec(rows, cols):
        return pl.BlockSpec((None, None, rows, cols), lambda i, be, *_: (layer, be[i], 0, 0))

    grid_spec = pltpu.PrefetchScalarGridSpec(
        num_scalar_prefetch=9,
        grid=(MOE_NBLK,),
        in_specs=[pl.BlockSpec(memory_space=pl.ANY), w_spec(D_MODEL, D_EXPERT),
                  w_spec(D_MODEL, D_EXPERT), w_spec(D_EXPERT, D_MODEL)],
        out_specs=pl.BlockSpec(memory_space=pl.ANY),
        scratch_shapes=[pltpu.VMEM((2, MOE_BLK, D_MODEL), F32), pltpu.VMEM((2, MOE_BLK, D_MODEL), F32),
                        pltpu.VMEM((MOE_LC - 2 * TM, D_MODEL), F32),
                        pltpu.VMEM((D_MODEL, D_EXPERT), BF16), pltpu.VMEM((D_MODEL, D_EXPERT), BF16),
                        pltpu.VMEM((D_EXPERT, D_MODEL), BF16),
                        pltpu.SemaphoreType.DMA((2,)), pltpu.SemaphoreType.DMA((2,)),
                        pltpu.SemaphoreType.DMA])
    return pl.pallas_call(
        _moe_kernel,
        grid_spec=grid_spec,
        out_shape=jax.ShapeDtypeStruct((NB_ALL * MOE_LC, D_MODEL), F32),
        compiler_params=_cparams(("arbitrary",)),
        name="moe_experts",
    )(blk_e, blk_lo, blk_nv, blk_c0, blk_c1, seg_off, seg_len, seg_row, chunk_rows, xs, w1, w3, w2)


def _final_kernel(x_ref, ys_ref, r_ref, mod_ref, g_ref, o_ref):
    x = _moe_residual(x_ref, ys_ref, r_ref, mod_ref)
    ms = jnp.mean(x * x, axis=-1, keepdims=True)
    o_ref[...] = x * lax.rsqrt(ms + EPS) * g_ref[...]


def _final(x, ys, route, mod, g, block0, nblocks):
    return pl.pallas_call(
        _final_kernel,
        grid=(nblocks,),
        in_specs=[pl.BlockSpec((TM, D_MODEL), lambda i: (block0 + i, 0)),
                  pl.BlockSpec((MOE_LC, D_MODEL), lambda i: (block0 + i, 0)),
                  pl.BlockSpec((TM, ROUTE_COLS), lambda i: (block0 + i, 0)),
                  pl.BlockSpec((None, 6, D_MODEL), lambda i: (_cond_row(block0 + i), 0, 0)),
                  _full_spec((1, D_MODEL))],
        out_specs=_tok_spec(D_MODEL),
        out_shape=jax.ShapeDtypeStruct((nblocks * TM, D_MODEL), F32),
        compiler_params=_cparams(("arbitrary",)),
        name="final_norm",
    )(x, ys, route, mod, g)


def kernel(x_prompt, x_sample, c, cache_k, cache_v, state_ret_f, state_ret_b, c_ctx, w_ada, b_ada, norm1_g, norm2_g, w_in, w_out, conv_w, conv_b, conv_ln_g, conv_ln_b, na_rpb, ret_lg_f, ret_lg_b, ret_gn_g, w_route_g, b_route_g, w_route_e, b_route_e, w1, w3, w2, final_g):
    cv = jnp.zeros((COND_ROWS, D_MODEL), F32).at[0].set(c_ctx).at[1:N_COND].set(c)
    mods = _ada(cv, w_ada, b_ada).reshape(DEPTH, COND_ROWS, 6, D_MODEL)
    w_in_b = w_in.astype(BF16)
    w_out_b = w_out.astype(BF16)
    pad = ROUTE_COLS - N_GROUPS - N_EXPERTS
    w_route = jnp.pad(jnp.concatenate([w_route_g, w_route_e], axis=-1), ((0, 0), (0, 0), (0, pad)))
    b_route = jnp.pad(jnp.concatenate([b_route_g, b_route_e], axis=-1), ((0, 0), (0, pad)))
    w_route_hi = w_route.astype(BF16)
    w_route_lo = (w_route - w_route_hi.astype(F32)).astype(BF16)
    w_route = jnp.concatenate([w_route_hi, w_route_lo], axis=-1)
    na_bias = _na_bias_tables(na_rpb)
    rope = _rope_tables()
    lg = jnp.stack([ret_lg_f, ret_lg_b], axis=1)

    x_ctx = x_prompt.reshape(T_CTX, D_MODEL)
    x_lat = x_sample.reshape(T_LAT, D_MODEL)
    x = y = route = None
    k_list, v_list, sf_list, sb_list = [], [], [], []
    for l in range(DEPTH):
        g1 = norm1_g[l].reshape(1, D_MODEL)
        if l == 0:
            z, x = _inproj_first(x_ctx, x_lat, mods[l], g1, w_in_b[l])
        else:
            z, x = _inproj_next(x, y, route, mods[l - 1], mods[l], g1, w_in_b[l])
        conv_args = (conv_w[l], conv_b[l].reshape(1, -1), conv_ln_g[l].reshape(1, -1),
                     conv_ln_b[l].reshape(1, -1))
        yc_c = _conv(z, 0, BATCH, SEQ, *conv_args)
        yc_l = _conv(z, T_CTX // DEC_SEQ, DEC_BATCH, DEC_SEQ, *conv_args)
        yn_c, k_l, v_l = _ctx_attn(z)
        yn_l = _na_attn(z, cache_k, cache_v, na_bias, l)
        gn = ret_gn_g[l].reshape(1, RET_WIDTH)
        yr_c, sf_l, sb_l = _retention(z, lg[l], gn, latent=False)
        yr_l = _retention(z, lg[l], gn, latent=True, layer=l, rope=rope,
                          s0_f=state_ret_f, s0_b=state_ret_b)
        x, xs, route, seg = _outproj((yc_c, yc_l), (yn_c, yn_l), (yr_c, yr_l), x, mods[l],
                                     norm2_g[l].reshape(1, D_MODEL), w_out_b[l], w_route[l],
                                     b_route[l].reshape(1, ROUTE_COLS))
        y = _moe(xs, w1, w3, w2, l, *_dispatch_tables(seg))
        k_list.append(k_l)
        v_list.append(v_l)
        sf_list.append(sf_l)
        sb_list.append(sb_l)
    fg = final_g.reshape(1, D_MODEL)
    y_prompt = _final(x, y, route, mods[DEPTH - 1], fg, 0, NB_CTX).reshape(BATCH, SEQ, D_MODEL)
    y_sample = _final(x, y, route, mods[DEPTH - 1], fg, NB_CTX, NB_LAT).reshape(DEC_BATCH, DEC_SEQ, D_MODEL)
    return (y_prompt, y_sample, jnp.stack(k_list, axis=1), jnp.stack(v_list, axis=1),
            jnp.stack(sf_list, axis=1), jnp.stack(sb_list, axis=1))
```

```python
import functools

import numpy as np
import jax
import jax.numpy as jnp
from jax import lax
from jax.experimental import pallas as pl
from jax.experimental.pallas import tpu as pltpu

D_MODEL = 1024
BATCH = 32
SEQ = 256
DEPTH = 2
DEC_BATCH = 4
DEC_SEQ = 4096
PAST_LEN = 512
GRID_W = 64
GRID_H = DEC_SEQ // GRID_W
CONV_CH = 256
CONV_K = 31
NA_HEADS = 8
NA_DIM = 64
NA_WIDTH = NA_HEADS * NA_DIM
NA_KH = 8
NA_KW = 16
RET_HEADS = 4
RET_DIM = 64
RET_WIDTH = RET_HEADS * RET_DIM
RET_CHUNK = 128
ROPE_BASE = 10000.0
N_GROUPS = 4
EXPERTS_PER_GROUP = 8
N_EXPERTS = N_GROUPS * EXPERTS_PER_GROUP
D_EXPERT = 512
IN_COLS = 2 * CONV_CH + 3 * NA_WIDTH + 4 * RET_WIDTH
EPS = 1e-6
NEG_INF = -1e30

F32 = jnp.float32
BF16 = jnp.bfloat16
HIGHEST = lax.Precision.HIGHEST

T_CTX = BATCH * SEQ
T_LAT = DEC_BATCH * DEC_SEQ
T_ALL = T_CTX + T_LAT
N_COND = 1 + DEC_BATCH
COND_ROWS = 8

TM = 512
NB_CTX = T_CTX // TM
NB_LAT = T_LAT // TM
NB_ALL = NB_CTX + NB_LAT
LAT_BLOCKS_PER_REQ = DEC_SEQ // TM

LANES = 128
SUBLANES = 8
ROUTE_COLS = LANES

COL_CONV = 0
COL_NA_Q = 2 * CONV_CH
COL_NA_K = COL_NA_Q + NA_WIDTH
COL_NA_V = COL_NA_K + NA_WIDTH
COL_RET = COL_NA_V + NA_WIDTH

NA_ROWS = 8
NA_Q = NA_ROWS * GRID_W
NA_KROWS = 2 * NA_ROWS
NA_KEYS = NA_KROWS * GRID_W
NA_RB = GRID_H // NA_ROWS

MOE_BLK = 512
MOE_LC = -(-(2 * TM + N_EXPERTS * (SUBLANES - 1)) // LANES) * LANES
MOE_NBLK = -(-(NB_ALL * MOE_LC) // MOE_BLK) + N_EXPERTS
N_SEG = NB_ALL * N_EXPERTS

VMEM_LIMIT = 56 * 1024 * 1024


def _cparams(sem):
    return pltpu.CompilerParams(dimension_semantics=sem, vmem_limit_bytes=VMEM_LIMIT)


def _sigmoid(x):
    return 1.0 / (1.0 + jnp.exp(-x))


def _cond_row(i):
    return jnp.where(i < NB_CTX, 0, 1 + (i - NB_CTX) // LAT_BLOCKS_PER_REQ)


ADA_TN = 1536


def _ada_kernel(cv_ref, w_ref, b_ref, o_ref):
    cv = cv_ref[...]
    s = cv * _sigmoid(cv)
    o_ref[...] = jnp.dot(s, w_ref[...], precision=HIGHEST, preferred_element_type=F32) + b_ref[...]


def _ada(cv, w_ada, b_ada):
    n = 6 * D_MODEL
    return pl.pallas_call(
        _ada_kernel,
        grid=(DEPTH, n // ADA_TN),
        in_specs=[
            pl.BlockSpec((COND_ROWS, D_MODEL), lambda l, j: (0, 0)),
            pl.BlockSpec((None, D_MODEL, ADA_TN), lambda l, j: (l, 0, j)),
            pl.BlockSpec((None, 1, ADA_TN), lambda l, j: (l, 0, j)),
        ],
        out_specs=pl.BlockSpec((None, COND_ROWS, ADA_TN), lambda l, j: (l, 0, j)),
        out_shape=jax.ShapeDtypeStruct((DEPTH, COND_ROWS, n), F32),
        compiler_params=_cparams(("arbitrary", "arbitrary")),
        name="ada_mod",
    )(cv, w_ada, b_ada.reshape(DEPTH, 1, n))


IN_TN = 768


def _norm_mod(x, g, shift, scale):
    ms = jnp.mean(x * x, axis=-1, keepdims=True)
    return (x * lax.rsqrt(ms + EPS) * g) * (1.0 + scale) + shift


def _inproj_body(x, mod_ref, g_ref, w_ref, z_ref):
    h = _norm_mod(x, g_ref[...], mod_ref[0:1, :], mod_ref[1:2, :]).astype(BF16)
    for c in range(IN_COLS // IN_TN):
        cols = slice(c * IN_TN, (c + 1) * IN_TN)
        z_ref[:, cols] = jnp.dot(h, w_ref[:, cols], preferred_element_type=F32).astype(BF16)


def _inproj_first_kernel(xc_ref, xl_ref, mod_ref, g_ref, w_ref, z_ref, xo_ref):
    i = pl.program_id(0)
    x = jnp.where(i < NB_CTX, xc_ref[...], xl_ref[...])
    xo_ref[...] = x
    _inproj_body(x, mod_ref, g_ref, w_ref, z_ref)


def _slot_onehot(route, slot):
    pos = route[:, 4 + slot:5 + slot].astype(jnp.int32)
    return lax.broadcasted_iota(jnp.int32, (route.shape[0], MOE_LC), 1) == pos


def _moe_residual(x_ref, ys_ref, r_ref, mod_ref):
    r = r_ref[...]
    sel = jnp.where(_slot_onehot(r, 0), r[:, 2:3], jnp.where(_slot_onehot(r, 1), r[:, 3:4], 0.0))
    y = jnp.dot(sel.astype(BF16), ys_ref[...].astype(BF16), preferred_element_type=F32)
    return x_ref[...] + mod_ref[5:6, :] * y


def _inproj_next_kernel(x_ref, ys_ref, r_ref, modp_ref, mod_ref, g_ref, w_ref, z_ref, xo_ref):
    x = _moe_residual(x_ref, ys_ref, r_ref, modp_ref)
    xo_ref[...] = x
    _inproj_body(x, mod_ref, g_ref, w_ref, z_ref)


def _tok_spec(cols):
    return pl.BlockSpec((TM, cols), lambda i: (i, 0))


def _mod_spec():
    return pl.BlockSpec((None, 6, D_MODEL), lambda i: (_cond_row(i), 0, 0))


def _full_spec(shape):
    return pl.BlockSpec(shape, lambda i: (0,) * len(shape))


def _ctx_lat_specs(cols):
    return [pl.BlockSpec((TM, cols), lambda i: (jnp.minimum(i, NB_CTX - 1), 0)),
            pl.BlockSpec((TM, cols), lambda i: (jnp.maximum(i - NB_CTX, 0), 0))]


def _inproj_first(x_ctx, x_lat, mod, g, w_bf16):
    return pl.pallas_call(
        _inproj_first_kernel,
        grid=(NB_ALL,),
        in_specs=_ctx_lat_specs(D_MODEL) + [_mod_spec(), _full_spec((1, D_MODEL)),
                                            _full_spec((D_MODEL, IN_COLS))],
        out_specs=[_tok_spec(IN_COLS), _tok_spec(D_MODEL)],
        out_shape=[jax.ShapeDtypeStruct((T_ALL, IN_COLS), BF16),
                   jax.ShapeDtypeStruct((T_ALL, D_MODEL), F32)],
        compiler_params=_cparams(("arbitrary",)),
        name="inproj_first",
    )(x_ctx, x_lat, mod, g, w_bf16)


def _inproj_next(x, ys, route, mod_prev, mod, g, w_bf16):
    return pl.pallas_call(
        _inproj_next_kernel,
        grid=(NB_ALL,),
        in_specs=[_tok_spec(D_MODEL),
                  pl.BlockSpec((MOE_LC, D_MODEL), lambda i: (i, 0)),
                  _tok_spec(ROUTE_COLS),
                  _mod_spec(), _mod_spec(), _full_spec((1, D_MODEL)),
                  _full_spec((D_MODEL, IN_COLS))],
        out_specs=[_tok_spec(IN_COLS), _tok_spec(D_MODEL)],
        out_shape=[jax.ShapeDtypeStruct((T_ALL, IN_COLS), BF16),
                   jax.ShapeDtypeStruct((T_ALL, D_MODEL), F32)],
        compiler_params=_cparams(("arbitrary",)),
        name="inproj_next",
    )(x, ys, route, mod_prev, mod, g, w_bf16)


CONV_PAD = 16
CONV_CHUNK = 64


def _conv_kernel(seq, z_ref, w_ref, b_ref, g_ref, be_ref, o_ref, upad_ref):
    zeros = jnp.zeros((CONV_PAD, CONV_CH), F32)
    upad_ref[0:CONV_PAD, :] = zeros
    upad_ref[seq + CONV_PAD:seq + 2 * CONV_PAD, :] = zeros

    def glu(ci, carry):
        base = pl.multiple_of(ci * 256, 256)
        zc = z_ref[pl.ds(base, 256), :].astype(F32)
        upad_ref[pl.ds(base + CONV_PAD, 256), :] = zc[:, :CONV_CH] * _sigmoid(zc[:, CONV_CH:])
        return carry

    lax.fori_loop(0, seq // 256, glu, 0)

    shift = CONV_PAD - CONV_K // 2

    def chunk(ci, carry):
        base = pl.multiple_of(ci * CONV_CHUNK, CONV_CHUNK)
        win = upad_ref[pl.ds(base, CONV_CHUNK + 2 * CONV_PAD), :]
        acc = jnp.zeros((CONV_CHUNK, CONV_CH), F32)
        span = CONV_CHUNK + 2 * CONV_PAD - SUBLANES
        for sub in range(SUBLANES):
            shifted = win[sub:sub + span, :]
            for k in range(CONV_K):
                if (k + shift) % SUBLANES == sub:
                    lo = k + shift - sub
                    acc = acc + w_ref[k:k + 1, :] * shifted[lo:lo + CONV_CHUNK, :]
        acc = acc + b_ref[...]
        mu = jnp.mean(acc, axis=-1, keepdims=True)
        d = acc - mu
        var = jnp.mean(d * d, axis=-1, keepdims=True)
        n = d * lax.rsqrt(var + EPS) * g_ref[...] + be_ref[...]
        o_ref[pl.ds(base, CONV_CHUNK), :] = (n * _sigmoid(n)).astype(BF16)
        return carry

    lax.fori_loop(0, seq // CONV_CHUNK, chunk, 0)


def _conv(z, row_block0, nseq, seq, w, b, g, be):
    return pl.pallas_call(
        functools.partial(_conv_kernel, seq),
        grid=(nseq,),
        in_specs=[pl.BlockSpec((seq, 2 * CONV_CH), lambda s: (row_block0 + s, 0)),
                  _full_spec((CONV_K, CONV_CH)), _full_spec((1, CONV_CH)),
                  _full_spec((1, CONV_CH)), _full_spec((1, CONV_CH))],
        out_specs=pl.BlockSpec((seq, CONV_CH), lambda s: (s, 0)),
        out_shape=jax.ShapeDtypeStruct((nseq * seq, CONV_CH), BF16),
        scratch_shapes=[pltpu.VMEM((seq + 2 * CONV_PAD, CONV_CH), F32)],
        compiler_params=_cparams(("arbitrary",)),
        name="conv_seq%d" % seq,
    )(z, w, b, g, be)


def _dot_nt(a, b):
    return lax.dot_general(a, b, (((1,), (1,)), ((), ())), preferred_element_type=F32)


NA_SCALE = NA_DIM ** -0.5
assert NA_SCALE == 2.0 ** round(np.log2(NA_SCALE)), "query pre-scaling assumes a power-of-two scale"


def _ctx_attn_kernel(layer, q_ref, k_ref, v_ref, *refs):
    if layer:
        kprev_ref, vprev_ref, o_ref, ko_ref, vo_ref = refs
    else:
        o_ref, ko_ref, vo_ref = refs
    for j in range(DEPTH):
        if j < layer:
            ko_ref[j] = kprev_ref[j]
            vo_ref[j] = vprev_ref[j]
        elif j > layer:
            ko_ref[j] = jnp.zeros(ko_ref.shape[1:], F32)
            vo_ref[j] = jnp.zeros(vo_ref.shape[1:], F32)
    scale = NA_SCALE
    outs = []
    for h in range(NA_HEADS):
        cols = slice(h * NA_DIM, (h + 1) * NA_DIM)
        qh, kh, vh = q_ref[:, cols], k_ref[:, cols], v_ref[:, cols]
        ko_ref[layer, h] = kh.astype(F32)
        vo_ref[layer, h] = vh.astype(F32)
        s = _dot_nt(qh, kh) * scale
        m = jnp.max(s, axis=-1, keepdims=True)
        p = jnp.exp(s - m)
        den = jnp.sum(p, axis=-1, keepdims=True)
        o = jnp.dot(p.astype(BF16), vh, preferred_element_type=F32)
        outs.append(o / den)
    o_ref[...] = jnp.concatenate(outs, axis=-1).astype(BF16)


def _ctx_attn(z, layer, k_prev=None, v_prev=None):
    qb, kb, vb = COL_NA_Q // NA_WIDTH, COL_NA_K // NA_WIDTH, COL_NA_V // NA_WIDTH
    head_shape = jax.ShapeDtypeStruct((BATCH, DEPTH, NA_HEADS, SEQ, NA_DIM), F32)
    head_spec = pl.BlockSpec((None, DEPTH, NA_HEADS, SEQ, NA_DIM), lambda b: (b, 0, 0, 0, 0))
    in_specs = [pl.BlockSpec((SEQ, NA_WIDTH), lambda b: (b, qb)),
                pl.BlockSpec((SEQ, NA_WIDTH), lambda b: (b, kb)),
                pl.BlockSpec((SEQ, NA_WIDTH), lambda b: (b, vb))]
    args = [z, z, z]
    aliases = {}
    if layer:
        in_specs += [head_spec, head_spec]
        args += [k_prev, v_prev]
        aliases = {3: 1, 4: 2}
    return pl.pallas_call(
        functools.partial(_ctx_attn_kernel, layer),
        grid=(BATCH,),
        in_specs=in_specs,
        out_specs=[pl.BlockSpec((SEQ, NA_WIDTH), lambda b: (b, 0)), head_spec, head_spec],
        out_shape=[jax.ShapeDtypeStruct((T_CTX, NA_WIDTH), BF16), head_shape, head_shape],
        input_output_aliases=aliases,
        compiler_params=_cparams(("arbitrary",)),
        name="ctx_attn",
    )(*args)


NA_KINDS = (0, NA_ROWS, GRID_H - NA_ROWS)
N_DR = 2 * NA_KH - 1
N_DC = 2 * NA_KW - 1


def _na_row_offset(r0, i, j):
    ks = min(max(r0 - NA_KH // 2, 0), GRID_H - NA_KROWS)
    r, kr = r0 + i, ks + j
    rs = min(max(r - NA_KH // 2, 0), GRID_H - NA_KH)
    return kr - r + NA_KH - 1 if rs <= kr < rs + NA_KH else None


def _na_bias_kernel(rpb_ref, o_ref):
    lh = pl.program_id(0)
    shape = (GRID_W, 2 * GRID_W)
    qc = lax.broadcasted_iota(jnp.int32, shape, 0)
    lane = lax.broadcasted_iota(jnp.int32, shape, 1)
    kc = lane % GRID_W
    dc = jnp.clip(kc - qc, -(NA_KW - 1), NA_KW - 1) + NA_KW - 1
    cs = jnp.clip(qc - NA_KW // 2, 0, GRID_W - NA_KW)
    col_ok = (kc >= cs) & (kc < cs + NA_KW)
    neg = jnp.full(shape, NEG_INF, F32)
    tiles = []
    for dr in range(N_DR):
        base = (lh * N_DR + dr) * N_DC
        val = jnp.zeros(shape, F32)
        for d in range(N_DC):
            val = jnp.where(dc == d, rpb_ref[base + d], val)
        tiles.append(jnp.where(col_ok, val, neg))
    left = lane < GRID_W
    for kind, r0 in enumerate(NA_KINDS):
        for i in range(NA_ROWS):
            for jp in range(NA_KROWS // 2):
                dl, dr_ = _na_row_offset(r0, i, 2 * jp), _na_row_offset(r0, i, 2 * jp + 1)
                tl = neg if dl is None else tiles[dl]
                tr = neg if dr_ is None else tiles[dr_]
                o_ref[kind, i * GRID_W:(i + 1) * GRID_W, jp * 2 * GRID_W:(jp + 1) * 2 * GRID_W] = (
                    jnp.where(left, tl, tr))


def _na_bias_tables(rpb):
    return pl.pallas_call(
        _na_bias_kernel,
        grid=(DEPTH * NA_HEADS,),
        in_specs=[pl.BlockSpec(memory_space=pltpu.SMEM)],
        out_specs=pl.BlockSpec((None, len(NA_KINDS), NA_Q, NA_KEYS), lambda i: (i, 0, 0, 0)),
        out_shape=jax.ShapeDtypeStruct((DEPTH * NA_HEADS, len(NA_KINDS), NA_Q, NA_KEYS), F32),
        compiler_params=_cparams(("arbitrary",)),
        name="nbr_bias",
    )(rpb.reshape(-1))


def _na_kernel(q_ref, k_ref, v_ref, kc_ref, vc_ref, bias_ref, o_ref):
    rb = pl.program_id(2)
    ks = jnp.clip(rb * NA_ROWS - NA_KH // 2, 0, GRID_H - NA_KROWS)
    start = pl.multiple_of(ks * GRID_W, GRID_W)
    q = q_ref[...] * NA_SCALE
    kl = k_ref[pl.ds(start, NA_KEYS), :]
    vl = v_ref[pl.ds(start, NA_KEYS), :]
    outs = []
    for hh in range(2):
        cols = slice(hh * NA_DIM, (hh + 1) * NA_DIM)
        qh = q[:, cols]
        s_loc = _dot_nt(qh, kl[:, cols]) + bias_ref[hh]
        s_ctx = _dot_nt(qh, kc_ref[hh].astype(BF16))
        m = jnp.maximum(jnp.max(s_loc, axis=-1, keepdims=True), jnp.max(s_ctx, axis=-1, keepdims=True))
        p_loc = jnp.exp(s_loc - m)
        p_ctx = jnp.exp(s_ctx - m)
        den = jnp.sum(p_loc, axis=-1, keepdims=True) + jnp.sum(p_ctx, axis=-1, keepdims=True)
        o = (jnp.dot(p_loc.astype(BF16), vl[:, cols], preferred_element_type=F32)
             + jnp.dot(p_ctx.astype(BF16), vc_ref[hh].astype(BF16), preferred_element_type=F32))
        outs.append(o / den)
    o_ref[...] = jnp.concatenate(outs, axis=-1).astype(BF16)


def _na_attn(z, cache_k, cache_v, bias, layer):
    lat_q0 = T_CTX // NA_Q
    lat_s0 = T_CTX // DEC_SEQ
    qc, kc, vc = COL_NA_Q // LANES, COL_NA_K // LANES, COL_NA_V // LANES

    def kind(rb):
        return jnp.where(rb == 0, 0, jnp.where(rb == NA_RB - 1, 2, 1))

    ctx_spec = pl.BlockSpec((None, None, 2, PAST_LEN, NA_DIM), lambda b, hp, rb: (b, layer, hp, 0, 0))
    return pl.pallas_call(
        _na_kernel,
        grid=(DEC_BATCH, NA_HEADS // 2, NA_RB),
        in_specs=[pl.BlockSpec((NA_Q, LANES), lambda b, hp, rb: (lat_q0 + b * NA_RB + rb, qc + hp)),
                  pl.BlockSpec((DEC_SEQ, LANES), lambda b, hp, rb: (lat_s0 + b, kc + hp)),
                  pl.BlockSpec((DEC_SEQ, LANES), lambda b, hp, rb: (lat_s0 + b, vc + hp)),
                  ctx_spec, ctx_spec,
                  pl.BlockSpec((2, None, NA_Q, NA_KEYS),
                               lambda b, hp, rb: (layer * (NA_HEADS // 2) + hp, kind(rb), 0, 0))],
        out_specs=pl.BlockSpec((NA_Q, LANES), lambda b, hp, rb: (b * NA_RB + rb, hp)),
        out_shape=jax.ShapeDtypeStruct((T_LAT, NA_WIDTH), BF16),
        compiler_params=_cparams(("arbitrary", "arbitrary", "arbitrary")),
        name="nbr_attn",
    )(z, z, z, cache_k, cache_v, bias)


def _rope_tables():
    n_freq = RET_DIM // 4
    t = np.arange(DEC_SEQ)
    inv = jnp.asarray(ROPE_BASE, F32) ** (-jnp.arange(n_freq, dtype=F32) / n_freq)
    ang_r = jnp.asarray(t // GRID_W, F32)[:, None] * inv[None, :]
    ang_c = jnp.asarray(t % GRID_W, F32)[:, None] * inv[None, :]
    cos = jnp.concatenate([jnp.cos(ang_r)] * 2 + [jnp.cos(ang_c)] * 2, axis=-1)
    sin = jnp.concatenate([-jnp.sin(ang_r), jnp.sin(ang_r), -jnp.sin(ang_c), jnp.sin(ang_c)], axis=-1)
    return jnp.tile(cos, (1, RET_HEADS)), jnp.tile(sin, (1, RET_HEADS))


def _ret_kernel(seq, latent, *refs):
    if latent:
        (lg_ref, z_ref, gn_ref, cos_ref, sin_ref, s0f_ref, s0b_ref, y_ref,
         q_s, k_s, kv_s, st_s) = refs
    else:
        lg_ref, z_ref, gn_ref, y_ref, sf_ref, sb_ref, q_s, k_s, kv_s, st_s = refs
    nc = seq // RET_CHUNK
    ch, hd = RET_CHUNK, RET_DIM
    half = RET_DIM // 4

    row = lax.broadcasted_iota(jnp.int32, (ch, ch), 0).astype(F32)
    col = lax.broadcasted_iota(jnp.int32, (ch, ch), 1).astype(F32)
    pos = lax.broadcasted_iota(jnp.int32, (ch, hd), 0).astype(F32)
    decay, q_dec, k_dec, c_dec_f, c_dec_b = [], [], [], [], []
    for h in range(RET_HEADS):
        lf, lb = lg_ref[0, h], lg_ref[1, h]
        d_f = jnp.where(row >= col, jnp.exp(jnp.maximum(row - col, 0.0) * lf), 0.0)
        d_b = jnp.where(col >= row, jnp.exp(jnp.maximum(col - row, 0.0) * lb), 0.0)
        decay.append(d_f + d_b)
        q_dec.append(jnp.concatenate([jnp.exp((pos + 1.0) * lf), jnp.exp((ch - pos) * lb)], axis=-1))
        k_dec.append(jnp.concatenate([jnp.exp((ch - 1.0 - pos) * lf), jnp.exp(pos * lb)], axis=-1))
        c_dec_f.append(jnp.exp(jnp.zeros((hd, hd), F32) + ch * lf))
        c_dec_b.append(jnp.exp(jnp.zeros((hd, hd), F32) + ch * lb))

    if latent:
        lane = lax.broadcasted_iota(jnp.int32, (ch, RET_WIDTH), 1)
        first_half = (lane % (2 * half)) < half

    def rope(x, base):
        if not latent:
            return x
        swapped = jnp.where(first_half, pltpu.roll(x, RET_WIDTH - half, 1), pltpu.roll(x, half, 1))
        return x * cos_ref[pl.ds(base, ch), :] + swapped * sin_ref[pl.ds(base, ch), :]

    def pass1(n, carry):
        base = pl.multiple_of(n * ch, ch)
        zc = z_ref[pl.ds(base, ch), :]
        q = rope(zc[:, 0:RET_WIDTH].astype(F32), base)
        k = rope(zc[:, RET_WIDTH:2 * RET_WIDTH].astype(F32) * (RET_DIM ** -0.5), base)
        q_s[pl.ds(base, ch), :] = q.astype(BF16)
        k_s[pl.ds(base, ch), :] = k.astype(BF16)
        v = zc[:, 2 * RET_WIDTH:3 * RET_WIDTH]
        for h in range(RET_HEADS):
            cols = slice(h * hd, (h + 1) * hd)
            kh = k[:, cols]
            k2 = (jnp.concatenate([kh, kh], axis=-1) * k_dec[h]).astype(BF16)
            kv_s[n, h] = lax.dot_general(k2, v[:, cols], (((0,), (0,)), ((), ())),
                                         preferred_element_type=F32)
        return carry

    lax.fori_loop(0, nc, pass1, 0)

    for h in range(RET_HEADS):
        if latent:
            s_f, s_b = s0f_ref[h], s0b_ref[h]
        else:
            s_f = s_b = jnp.zeros((hd, hd), F32)

        def fwd(n, s, h=h):
            st_s[n, h, 0:hd, :] = s
            return c_dec_f[h] * s + kv_s[n, h, 0:hd, :]

        def bwd(i, s, h=h):
            n = nc - 1 - i
            st_s[n, h, hd:2 * hd, :] = s
            return c_dec_b[h] * s + kv_s[n, h, hd:2 * hd, :]

        s_f = lax.fori_loop(0, nc, fwd, s_f)
        s_b = lax.fori_loop(0, nc, bwd, s_b)
        if not latent:
            sf_ref[h] = s_f
            sb_ref[h] = s_b

    def pass3(n, carry):
        base = pl.multiple_of(n * ch, ch)
        zc = z_ref[pl.ds(base, ch), :]
        q = q_s[pl.ds(base, ch), :]
        k = k_s[pl.ds(base, ch), :]
        v = zc[:, 2 * RET_WIDTH:3 * RET_WIDTH]
        gate = zc[:, 3 * RET_WIDTH:4 * RET_WIDTH].astype(F32)
        outs = []
        for h in range(RET_HEADS):
            cols = slice(h * hd, (h + 1) * hd)
            qh = q[:, cols]
            s = _dot_nt(qh, k[:, cols]) * decay[h]
            o = jnp.dot(s.astype(BF16), v[:, cols], preferred_element_type=F32)
            qf = qh.astype(F32)
            q2 = (jnp.concatenate([qf, qf], axis=-1) * q_dec[h]).astype(BF16)
            o = o + jnp.dot(q2, st_s[n, h].astype(BF16), preferred_element_type=F32)
            mu = jnp.mean(o, axis=-1, keepdims=True)
            d = o - mu
            var = jnp.mean(d * d, axis=-1, keepdims=True)
            outs.append(d * lax.rsqrt(var + EPS))
        nrm = jnp.concatenate(outs, axis=-1)
        y_ref[pl.ds(base, ch), :] = (nrm * gn_ref[...] * (gate * _sigmoid(gate))).astype(BF16)
        return carry

    lax.fori_loop(0, nc, pass3, 0)


def _retention(z, lg, gn_g, latent, layer=None, rope=None, s0_f=None, s0_b=None):
    seq = DEC_SEQ if latent else SEQ
    nseq = DEC_BATCH if latent else BATCH
    nc = seq // RET_CHUNK
    row0 = (T_CTX // DEC_SEQ) if latent else 0
    cb = COL_RET // (4 * RET_WIDTH)
    in_specs = [pl.BlockSpec(memory_space=pltpu.SMEM),
                pl.BlockSpec((seq, 4 * RET_WIDTH), lambda s: (row0 + s, cb)),
                _full_spec((1, RET_WIDTH))]
    args = [lg, z, gn_g]
    state_shape = jax.ShapeDtypeStruct((nseq, RET_HEADS, RET_DIM, RET_DIM), F32)
    y_spec = pl.BlockSpec((seq, RET_WIDTH), lambda s: (s, 0))
    y_shape = jax.ShapeDtypeStruct((nseq * seq, RET_WIDTH), BF16)
    if latent:
        st_spec = pl.BlockSpec((None, None, RET_HEADS, RET_DIM, RET_DIM), lambda s: (s, layer, 0, 0, 0))
        in_specs += [_full_spec((seq, RET_WIDTH)), _full_spec((seq, RET_WIDTH)), st_spec, st_spec]
        args += [rope[0], rope[1], s0_f, s0_b]
        out_specs, out_shape = y_spec, y_shape
    else:
        so_spec = pl.BlockSpec((None, RET_HEADS, RET_DIM, RET_DIM), lambda s: (s, 0, 0, 0))
        out_specs, out_shape = [y_spec, so_spec, so_spec], [y_shape, state_shape, state_shape]
    return pl.pallas_call(
        functools.partial(_ret_kernel, seq, latent),
        grid=(nseq,),
        in_specs=in_specs,
        out_specs=out_specs,
        out_shape=out_shape,
        scratch_shapes=[pltpu.VMEM((seq, RET_WIDTH), BF16), pltpu.VMEM((seq, RET_WIDTH), BF16),
                        pltpu.VMEM((nc, RET_HEADS, 2 * RET_DIM, RET_DIM), F32),
                        pltpu.VMEM((nc, RET_HEADS, 2 * RET_DIM, RET_DIM), F32)],
        compiler_params=_cparams(("arbitrary",)),
        name="retention_lat" if latent else "retention_ctx",
    )(*args)


def _route(logits):
    lane = lax.broadcasted_iota(jnp.int32, logits.shape, 1)
    lane_f = lane.astype(F32)
    big = float(ROUTE_COLS)
    neg = -jnp.inf
    is_grp = lane < N_GROUPS
    gl = jnp.where(is_grp, logits, neg)
    gmax = jnp.max(gl, axis=-1, keepdims=True)
    grp = jnp.min(jnp.where(gl == gmax, lane_f, big), axis=-1, keepdims=True)
    p_grp = 1.0 / jnp.sum(jnp.exp(gl - gmax), axis=-1, keepdims=True)
    e_f = lane_f - N_GROUPS
    lo = grp * EXPERTS_PER_GROUP
    in_grp = (e_f >= lo) & (e_f < lo + EXPERTS_PER_GROUP)
    el = jnp.where(in_grp, logits, neg)
    m1 = jnp.max(el, axis=-1, keepdims=True)
    i1 = jnp.min(jnp.where(el == m1, lane_f, big), axis=-1, keepdims=True)
    el2 = jnp.where(lane_f == i1, neg, el)
    m2 = jnp.max(el2, axis=-1, keepdims=True)
    i2 = jnp.min(jnp.where(el2 == m2, lane_f, big), axis=-1, keepdims=True)
    t = jnp.exp(m2 - m1)
    g1 = p_grp / (1.0 + t)
    g2 = p_grp * t / (1.0 + t)
    rows = logits.shape[0]
    oh1, oh2 = lane_f == i1, lane_f == i2
    oh = jnp.where(oh1 | oh2, 1.0, 0.0)
    tri = (lax.broadcasted_iota(jnp.int32, (rows, rows), 0)
           > lax.broadcasted_iota(jnp.int32, (rows, rows), 1))
    rank = jnp.dot(jnp.where(tri, 1.0, 0.0).astype(BF16), oh.astype(BF16), preferred_element_type=F32)
    tiles = jnp.floor((jnp.sum(oh, axis=0, keepdims=True) + (SUBLANES - 1)) * (1.0 / SUBLANES))
    upper = (lax.broadcasted_iota(jnp.int32, (ROUTE_COLS, ROUTE_COLS), 0)
             < lax.broadcasted_iota(jnp.int32, (ROUTE_COLS, ROUTE_COLS), 1))
    start = SUBLANES * jnp.dot(jnp.broadcast_to(tiles, (SUBLANES, ROUTE_COLS)).astype(BF16),
                               jnp.where(upper, 1.0, 0.0).astype(BF16),
                               preferred_element_type=F32)[0:1, :]
    pos = start + rank
    p1 = jnp.sum(jnp.where(oh1, pos, 0.0), axis=-1, keepdims=True)
    p2 = jnp.sum(jnp.where(oh2, pos, 0.0), axis=-1, keepdims=True)
    out = jnp.zeros(logits.shape, F32)
    for k, val in enumerate((i1 - N_GROUPS, i2 - N_GROUPS, g1, g2, p1, p2)):
        out = jnp.where(lane == k, val, out)
    return out, SUBLANES * tiles


def _outproj_kernel(ycc, ycl, ync, ynl, yrc, yrl, x_ref, mod_ref, g_ref, w_ref, wr_ref, br_ref,
                    xo_ref, xs_ref, r_ref, seg_ref):
    is_ctx = pl.program_id(0) < NB_CTX
    yc = jnp.where(is_ctx, ycc[...], ycl[...])
    yn = jnp.where(is_ctx, ync[...], ynl[...])
    yr = jnp.where(is_ctx, yrc[...], yrl[...])
    y = (jnp.dot(yc, w_ref[0:CONV_CH, :], preferred_element_type=F32)
         + jnp.dot(yn, w_ref[CONV_CH:CONV_CH + NA_WIDTH, :], preferred_element_type=F32)
         + jnp.dot(yr, w_ref[CONV_CH + NA_WIDTH:, :], preferred_element_type=F32))
    x = x_ref[...] + mod_ref[2:3, :] * y
    xo_ref[...] = x
    h = _norm_mod(x, g_ref[...], mod_ref[3:4, :], mod_ref[4:5, :])
    h_hi = h.astype(BF16)
    h_lo = (h - h_hi.astype(F32)).astype(BF16)
    hw = jnp.dot(h_hi, wr_ref[...], preferred_element_type=F32)
    logits = (hw[:, :ROUTE_COLS] + hw[:, ROUTE_COLS:]
              + jnp.dot(h_lo, wr_ref[:, :ROUTE_COLS], preferred_element_type=F32) + br_ref[...])
    route, seg = _route(logits)
    r_ref[...] = route
    seg_ref[...] = jnp.broadcast_to(seg, seg_ref.shape)
    sel = _slot_onehot(route, 0) | _slot_onehot(route, 1)
    xs_ref[...] = lax.dot_general(jnp.where(sel, 1.0, 0.0).astype(BF16), h_hi, (((0,), (0,)), ((), ())),
                                  preferred_element_type=F32)


def _outproj(y_conv, y_na, y_ret, x, mod, g, w_bf16, w_route, b_route):
    return pl.pallas_call(
        _outproj_kernel,
        grid=(NB_ALL,),
        in_specs=(_ctx_lat_specs(CONV_CH) + _ctx_lat_specs(NA_WIDTH) + _ctx_lat_specs(RET_WIDTH)
                  + [_tok_spec(D_MODEL), _mod_spec(), _full_spec((1, D_MODEL)),
                     _full_spec((D_MODEL, D_MODEL)), _full_spec((D_MODEL, 2 * ROUTE_COLS)),
                     _full_spec((1, ROUTE_COLS))]),
        out_specs=[_tok_spec(D_MODEL), pl.BlockSpec((MOE_LC, D_MODEL), lambda i: (i, 0)),
                   _tok_spec(ROUTE_COLS), pl.BlockSpec((None, SUBLANES, ROUTE_COLS), lambda i: (i, 0, 0))],
        out_shape=[jax.ShapeDtypeStruct((T_ALL, D_MODEL), F32),
                   jax.ShapeDtypeStruct((NB_ALL * MOE_LC, D_MODEL), F32),
                   jax.ShapeDtypeStruct((T_ALL, ROUTE_COLS), F32),
                   jax.ShapeDtypeStruct((NB_ALL, SUBLANES, ROUTE_COLS), F32)],
        compiler_params=_cparams(("arbitrary",)),
        name="outproj_route",
    )(y_conv[0], y_conv[1], y_na[0], y_na[1], y_ret[0], y_ret[1], x, mod, g, w_bf16, w_route, b_route)


def _dispatch_tables(seg):
    seg_len = seg[:, 0, N_GROUPS:N_GROUPS + N_EXPERTS].astype(jnp.int32)
    experts = jnp.arange(N_EXPERTS, dtype=jnp.int32)
    in_chunk = jnp.cumsum(seg_len, axis=1) - seg_len
    seg_row = in_chunk + MOE_LC * jnp.arange(NB_ALL, dtype=jnp.int32)[:, None]
    seg_off = jnp.cumsum(seg_len, axis=0) - seg_len
    rows_e = jnp.sum(seg_len, axis=0)
    chunk_rows = jnp.sum(seg_len, axis=1)
    nblk = (rows_e + MOE_BLK - 1) // MOE_BLK
    blk_end = jnp.cumsum(nblk)
    blk_start = blk_end - nblk
    blk = jnp.arange(MOE_NBLK, dtype=jnp.int32)
    n_active = blk_end[-1]
    blk_e = jnp.minimum(jnp.sum((blk_end[None, :] <= jnp.minimum(blk, n_active - 1)[:, None]).astype(jnp.int32),
                                axis=-1), N_EXPERTS - 1)
    mine = blk_e[:, None] == experts[None, :]
    blk_lo = (blk - jnp.sum(jnp.where(mine, blk_start[None, :], 0), axis=-1)) * MOE_BLK
    left = jnp.sum(jnp.where(mine, rows_e[None, :], 0), axis=-1) - blk_lo
    blk_nv = jnp.where(blk < n_active, jnp.clip(left, 0, MOE_BLK), 0).astype(jnp.int32)
    off_b = jnp.sum(jnp.where(mine[:, None, :], seg_off[None, :, :], 0), axis=-1)
    end_b = off_b + jnp.sum(jnp.where(mine[:, None, :], seg_len[None, :, :], 0), axis=-1)
    blk_c0 = jnp.sum((end_b <= blk_lo[:, None]).astype(jnp.int32), axis=-1)
    blk_c1 = jnp.sum((off_b < (blk_lo + blk_nv)[:, None]).astype(jnp.int32), axis=-1)
    return (blk_e, blk_lo.astype(jnp.int32), blk_nv, blk_c0, blk_c1, seg_off.reshape(-1),
            seg_len.reshape(-1), seg_row.reshape(-1), chunk_rows)


def _moe_kernel(blk_e, blk_lo, blk_nv, blk_c0, blk_c1, seg_off, seg_len, seg_row, chunk_rows,
                xs_hbm, w1_ref, w3_ref, w2_ref, ys_hbm, xbuf, obuf, zeros, w1b, w3b, w2b, gsem, ssem, zsem):
    i = pl.program_id(0)
    last = pl.num_programs(0) - 1
    slot = i % 2

    def tiles(v):
        return pl.multiple_of(v, SUBLANES)

    def for_segments(blk, fn):
        lo = blk_lo[blk]
        hi = lo + blk_nv[blk]

        def body(c, carry):
            k = c * N_EXPERTS + blk_e[blk]
            s_lo = seg_off[k]
            a = jnp.maximum(s_lo, lo)
            n = jnp.minimum(s_lo + seg_len[k], hi) - a

            @pl.when(n > 0)
            def _():
                fn(tiles(seg_row[k] + a - s_lo), tiles(a - lo), tiles(n))

            return carry

        lax.fori_loop(blk_c0[blk], blk_c1[blk], body, 0)

    def start_gathers(blk, s):
        for_segments(blk, lambda src, dst, n: pltpu.make_async_copy(
            xs_hbm.at[pl.ds(src, n)], xbuf.at[s, pl.ds(dst, n)], gsem.at[s]).start())

    def start_scatters(blk, s):
        for_segments(blk, lambda dst, src, n: pltpu.make_async_copy(
            obuf.at[s, pl.ds(src, n)], ys_hbm.at[pl.ds(dst, n)], ssem.at[s]).start())

    def wait_rows(blk, s, sem):
        n = tiles(blk_nv[blk])

        @pl.when(n > 0)
        def _():
            pltpu.make_async_copy(xs_hbm.at[pl.ds(0, n)], xbuf.at[s, pl.ds(0, n)], sem.at[s]).wait()

    @pl.when(i == 0)
    def _():
        xbuf[...] = jnp.zeros_like(xbuf)
        zeros[...] = jnp.zeros_like(zeros)

        def tail(c):
            n = tiles(MOE_LC - chunk_rows[c])
            return n, pltpu.make_async_copy(zeros.at[pl.ds(0, n)],
                                            ys_hbm.at[pl.ds(tiles(c * MOE_LC + chunk_rows[c]), n)], zsem)

        def fill(c, carry):
            n, copy = tail(c)
            pl.when(n > 0)(copy.start)
            return carry

        def drain(c, carry):
            n, copy = tail(c)
            pl.when(n > 0)(copy.wait)
            return carry

        lax.fori_loop(0, NB_ALL, fill, 0)
        lax.fori_loop(0, NB_ALL, drain, 0)
        start_gathers(0, 0)

    @pl.when(i < last)
    def _():
        start_gathers(i + 1, 1 - slot)

    @pl.when(i >= 2)
    def _():
        wait_rows(i - 2, slot, ssem)

    @pl.when(blk_nv[i] > 0)
    def _():
        @pl.when((i == 0) | (blk_e[i] != blk_e[jnp.maximum(i - 1, 0)]))
        def _():
            w1b[...] = w1_ref[...].astype(BF16)
            w3b[...] = w3_ref[...].astype(BF16)
            w2b[...] = w2_ref[...].astype(BF16)

        wait_rows(i, slot, gsem)
        xb = xbuf[slot].astype(BF16)
        a = jnp.dot(xb, w1b[...], preferred_element_type=F32)
        b = jnp.dot(xb, w3b[...], preferred_element_type=F32)
        mid = (a * _sigmoid(a) * b).astype(BF16)
        obuf[slot] = jnp.dot(mid, w2b[...], preferred_element_type=F32)
        start_scatters(i, slot)

    @pl.when(i == last)
    def _():
        wait_rows(i - 1, 1 - slot, ssem)
        wait_rows(i, slot, ssem)


def _moe(xs, w1, w3, w2, layer, blk_e, blk_lo, blk_nv, blk_c0, blk_c1, seg_off, seg_len, seg_row,
         chunk_rows):
    def w_spec(rows, cols):
        return pl.BlockSpec((None, None, rows, cols), lambda i, be, *_: (layer, be[i], 0, 0))

    grid_spec = pltpu.PrefetchScalarGridSpec(
        num_scalar_prefetch=9,
        grid=(MOE_NBLK,),
        in_specs=[pl.BlockSpec(memory_space=pl.ANY), w_spec(D_MODEL, D_EXPERT),
                  w_spec(D_MODEL, D_EXPERT), w_spec(D_EXPERT, D_MODEL)],
        out_specs=pl.BlockSpec(memory_space=pl.ANY),
        scratch_shapes=[pltpu.VMEM((2, MOE_BLK, D_MODEL), F32), pltpu.VMEM((2, MOE_BLK, D_MODEL), F32),
                        pltpu.VMEM((MOE_LC - 2 * TM, D_MODEL), F32),
                        pltpu.VMEM((D_MODEL, D_EXPERT), BF16), pltpu.VMEM((D_MODEL, D_EXPERT), BF16),
                        pltpu.VMEM((D_EXPERT, D_MODEL), BF16),
                        pltpu.SemaphoreType.DMA((2,)), pltpu.SemaphoreType.DMA((2,)),
                        pltpu.SemaphoreType.DMA])
    return pl.pallas_call(
        _moe_kernel,
        grid_spec=grid_spec,
        out_shape=jax.ShapeDtypeStruct((NB_ALL * MOE_LC, D_MODEL), F32),
        compiler_params=_cparams(("arbitrary",)),
        name="moe_experts",
    )(blk_e, blk_lo, blk_nv, blk_c0, blk_c1, seg_off, seg_len, seg_row, chunk_rows, xs, w1, w3, w2)


def _final_kernel(x_ref, ys_ref, r_ref, mod_ref, g_ref, o_ref):
    x = _moe_residual(x_ref, ys_ref, r_ref, mod_ref)
    ms = jnp.mean(x * x, axis=-1, keepdims=True)
    o_ref[...] = x * lax.rsqrt(ms + EPS) * g_ref[...]


def _final(x, ys, route, mod, g, block0, nblocks):
    return pl.pallas_call(
        _final_kernel,
        grid=(nblocks,),
        in_specs=[pl.BlockSpec((TM, D_MODEL), lambda i: (block0 + i, 0)),
                  pl.BlockSpec((MOE_LC, D_MODEL), lambda i: (block0 + i, 0)),
                  pl.BlockSpec((TM, ROUTE_COLS), lambda i: (block0 + i, 0)),
                  pl.BlockSpec((None, 6, D_MODEL), lambda i: (_cond_row(block0 + i), 0, 0)),
                  _full_spec((1, D_MODEL))],
        out_specs=_tok_spec(D_MODEL),
        out_shape=jax.ShapeDtypeStruct((nblocks * TM, D_MODEL), F32),
        compiler_params=_cparams(("arbitrary",)),
        name="final_norm",
    )(x, ys, route, mod, g)


def kernel(x_prompt, x_sample, c, cache_k, cache_v, state_ret_f, state_ret_b, c_ctx, w_ada, b_ada, norm1_g, norm2_g, w_in, w_out, conv_w, conv_b, conv_ln_g, conv_ln_b, na_rpb, ret_lg_f, ret_lg_b, ret_gn_g, w_route_g, b_route_g, w_route_e, b_route_e, w1, w3, w2, final_g):
    cv = jnp.zeros((COND_ROWS, D_MODEL), F32).at[0].set(c_ctx).at[1:N_COND].set(c)
    mods = _ada(cv, w_ada, b_ada).reshape(DEPTH, COND_ROWS, 6, D_MODEL)
    w_in_b = w_in.astype(BF16)
    w_out_b = w_out.astype(BF16)
    pad = ROUTE_COLS - N_GROUPS - N_EXPERTS
    w_route = jnp.pad(jnp.concatenate([w_route_g, w_route_e], axis=-1), ((0, 0), (0, 0), (0, pad)))
    b_route = jnp.pad(jnp.concatenate([b_route_g, b_route_e], axis=-1), ((0, 0), (0, pad)))
    w_route_hi = w_route.astype(BF16)
    w_route_lo = (w_route - w_route_hi.astype(F32)).astype(BF16)
    w_route = jnp.concatenate([w_route_hi, w_route_lo], axis=-1)
    na_bias = _na_bias_tables(na_rpb)
    rope = _rope_tables()
    lg = jnp.stack([ret_lg_f, ret_lg_b], axis=1)

    x_ctx = x_prompt.reshape(T_CTX, D_MODEL)
    x_lat = x_sample.reshape(T_LAT, D_MODEL)
    x = y = route = new_k = new_v = None
    sf_list, sb_list = [], []
    for l in range(DEPTH):
        g1 = norm1_g[l].reshape(1, D_MODEL)
        if l == 0:
            z, x = _inproj_first(x_ctx, x_lat, mods[l], g1, w_in_b[l])
        else:
            z, x = _inproj_next(x, y, route, mods[l - 1], mods[l], g1, w_in_b[l])
        conv_args = (conv_w[l], conv_b[l].reshape(1, -1), conv_ln_g[l].reshape(1, -1),
                     conv_ln_b[l].reshape(1, -1))
        yc_c = _conv(z, 0, BATCH, SEQ, *conv_args)
        yc_l = _conv(z, T_CTX // DEC_SEQ, DEC_BATCH, DEC_SEQ, *conv_args)
        yn_c, new_k, new_v = _ctx_attn(z, l, new_k, new_v)
        yn_l = _na_attn(z, cache_k, cache_v, na_bias, l)
        gn = ret_gn_g[l].reshape(1, RET_WIDTH)
        yr_c, sf_l, sb_l = _retention(z, lg[l], gn, latent=False)
        yr_l = _retention(z, lg[l], gn, latent=True, layer=l, rope=rope,
                          s0_f=state_ret_f, s0_b=state_ret_b)
        x, xs, route, seg = _outproj((yc_c, yc_l), (yn_c, yn_l), (yr_c, yr_l), x, mods[l],
                                     norm2_g[l].reshape(1, D_MODEL), w_out_b[l], w_route[l],
                                     b_route[l].reshape(1, ROUTE_COLS))
        y = _moe(xs, w1, w3, w2, l, *_dispatch_tables(seg))
        sf_list.append(sf_l)
        sb_list.append(sb_l)
    fg = final_g.reshape(1, D_MODEL)
    y_prompt = _final(x, y, route, mods[DEPTH - 1], fg, 0, NB_CTX).reshape(BATCH, SEQ, D_MODEL)
    y_sample = _final(x, y, route, mods[DEPTH - 1], fg, NB_CTX, NB_LAT).reshape(DEC_BATCH, DEC_SEQ, D_MODEL)
    return (y_prompt, y_sample, new_k, new_v, jnp.stack(sf_list, axis=1), jnp.stack(sb_list, axis=1))
```

```python
import functools

import numpy as np
import jax
import jax.numpy as jnp
from jax import lax
from jax.experimental import pallas as pl
from jax.experimental.pallas import tpu as pltpu

D_MODEL = 1024
BATCH = 32
SEQ = 256
DEPTH = 2
DEC_BATCH = 4
DEC_SEQ = 4096
PAST_LEN = 512
GRID_W = 64
GRID_H = DEC_SEQ // GRID_W
CONV_CH = 256
CONV_K = 31
NA_HEADS = 8
NA_DIM = 64
NA_WIDTH = NA_HEADS * NA_DIM
NA_KH = 8
NA_KW = 16
RET_HEADS = 4
RET_DIM = 64
RET_WIDTH = RET_HEADS * RET_DIM
RET_CHUNK = 128
ROPE_BASE = 10000.0
N_GROUPS = 4
EXPERTS_PER_GROUP = 8
N_EXPERTS = N_GROUPS * EXPERTS_PER_GROUP
D_EXPERT = 512
IN_COLS = 2 * CONV_CH + 3 * NA_WIDTH + 4 * RET_WIDTH
EPS = 1e-6
NEG_INF = -1e30

F32 = jnp.float32
BF16 = jnp.bfloat16
HIGHEST = lax.Precision.HIGHEST

T_CTX = BATCH * SEQ
T_LAT = DEC_BATCH * DEC_SEQ
T_ALL = T_CTX + T_LAT
N_COND = 1 + DEC_BATCH
COND_ROWS = 8

TM = 512
NB_CTX = T_CTX // TM
NB_LAT = T_LAT // TM
NB_ALL = NB_CTX + NB_LAT
LAT_BLOCKS_PER_REQ = DEC_SEQ // TM

LANES = 128
SUBLANES = 8
ROUTE_COLS = LANES

COL_CONV = 0
COL_NA_Q = 2 * CONV_CH
COL_NA_K = COL_NA_Q + NA_WIDTH
COL_NA_V = COL_NA_K + NA_WIDTH
COL_RET = COL_NA_V + NA_WIDTH

NA_ROWS = 8
NA_Q = NA_ROWS * GRID_W
NA_KROWS = 2 * NA_ROWS
NA_KEYS = NA_KROWS * GRID_W
NA_RB = GRID_H // NA_ROWS

MOE_BLK = 512
MOE_LC = -(-(2 * TM + N_EXPERTS * (SUBLANES - 1)) // LANES) * LANES
MOE_NBLK = -(-(NB_ALL * MOE_LC) // MOE_BLK) + N_EXPERTS
N_SEG = NB_ALL * N_EXPERTS

VMEM_LIMIT = 56 * 1024 * 1024


def _cparams(sem):
    return pltpu.CompilerParams(dimension_semantics=sem, vmem_limit_bytes=VMEM_LIMIT)


def _sigmoid(x):
    return 1.0 / (1.0 + jnp.exp(-x))


def _cond_row(i):
    return jnp.where(i < NB_CTX, 0, 1 + (i - NB_CTX) // LAT_BLOCKS_PER_REQ)


ADA_TN = 1536


def _ada_kernel(cv_ref, w_ref, b_ref, o_ref):
    cv = cv_ref[...]
    s = cv * _sigmoid(cv)
    o_ref[...] = jnp.dot(s, w_ref[...], precision=HIGHEST, preferred_element_type=F32) + b_ref[...]


def _ada(cv, w_ada, b_ada):
    n = 6 * D_MODEL
    return pl.pallas_call(
        _ada_kernel,
        grid=(DEPTH, n // ADA_TN),
        in_specs=[
            pl.BlockSpec((COND_ROWS, D_MODEL), lambda l, j: (0, 0)),
            pl.BlockSpec((None, D_MODEL, ADA_TN), lambda l, j: (l, 0, j)),
            pl.BlockSpec((None, 1, ADA_TN), lambda l, j: (l, 0, j)),
        ],
        out_specs=pl.BlockSpec((None, COND_ROWS, ADA_TN), lambda l, j: (l, 0, j)),
        out_shape=jax.ShapeDtypeStruct((DEPTH, COND_ROWS, n), F32),
        compiler_params=_cparams(("arbitrary", "arbitrary")),
        name="ada_mod",
    )(cv, w_ada, b_ada.reshape(DEPTH, 1, n))


IN_TN = 768


def _norm_mod(x, g, shift, scale):
    ms = jnp.mean(x * x, axis=-1, keepdims=True)
    return (x * lax.rsqrt(ms + EPS) * g) * (1.0 + scale) + shift


def _inproj_body(x, mod_ref, g_ref, w_ref, z_ref):
    h = _norm_mod(x, g_ref[...], mod_ref[0:1, :], mod_ref[1:2, :]).astype(BF16)
    for c in range(IN_COLS // IN_TN):
        cols = slice(c * IN_TN, (c + 1) * IN_TN)
        z_ref[:, cols] = jnp.dot(h, w_ref[:, cols], preferred_element_type=F32).astype(BF16)


def _inproj_first_kernel(xc_ref, xl_ref, mod_ref, g_ref, w_ref, z_ref, xo_ref):
    i = pl.program_id(0)
    x = jnp.where(i < NB_CTX, xc_ref[...], xl_ref[...])
    xo_ref[...] = x
    _inproj_body(x, mod_ref, g_ref, w_ref, z_ref)


def _slot_onehot(route, slot):
    pos = route[:, 4 + slot:5 + slot].astype(jnp.int32)
    return lax.broadcasted_iota(jnp.int32, (route.shape[0], MOE_LC), 1) == pos


def _moe_residual(x_ref, ys_ref, r_ref, mod_ref):
    r = r_ref[...]
    sel = jnp.where(_slot_onehot(r, 0), r[:, 2:3], jnp.where(_slot_onehot(r, 1), r[:, 3:4], 0.0))
    y = jnp.dot(sel.astype(BF16), ys_ref[...].astype(BF16), preferred_element_type=F32)
    return x_ref[...] + mod_ref[5:6, :] * y


def _inproj_next_kernel(x_ref, ys_ref, r_ref, modp_ref, mod_ref, g_ref, w_ref, z_ref, xo_ref):
    x = _moe_residual(x_ref, ys_ref, r_ref, modp_ref)
    xo_ref[...] = x
    _inproj_body(x, mod_ref, g_ref, w_ref, z_ref)


def _tok_spec(cols):
    return pl.BlockSpec((TM, cols), lambda i: (i, 0))


def _mod_spec():
    return pl.BlockSpec((None, 6, D_MODEL), lambda i: (_cond_row(i), 0, 0))


def _full_spec(shape):
    return pl.BlockSpec(shape, lambda i: (0,) * len(shape))


def _ctx_lat_specs(cols):
    return [pl.BlockSpec((TM, cols), lambda i: (jnp.minimum(i, NB_CTX - 1), 0)),
            pl.BlockSpec((TM, cols), lambda i: (jnp.maximum(i - NB_CTX, 0), 0))]


def _inproj_first(x_ctx, x_lat, mod, g, w_bf16):
    return pl.pallas_call(
        _inproj_first_kernel,
        grid=(NB_ALL,),
        in_specs=_ctx_lat_specs(D_MODEL) + [_mod_spec(), _full_spec((1, D_MODEL)),
                                            _full_spec((D_MODEL, IN_COLS))],
        out_specs=[_tok_spec(IN_COLS), _tok_spec(D_MODEL)],
        out_shape=[jax.ShapeDtypeStruct((T_ALL, IN_COLS), BF16),
                   jax.ShapeDtypeStruct((T_ALL, D_MODEL), F32)],
        compiler_params=_cparams(("arbitrary",)),
        name="inproj_first",
    )(x_ctx, x_lat, mod, g, w_bf16)


def _inproj_next(x, ys, route, mod_prev, mod, g, w_bf16):
    return pl.pallas_call(
        _inproj_next_kernel,
        grid=(NB_ALL,),
        in_specs=[_tok_spec(D_MODEL),
                  pl.BlockSpec((MOE_LC, D_MODEL), lambda i: (i, 0)),
                  _tok_spec(ROUTE_COLS),
                  _mod_spec(), _mod_spec(), _full_spec((1, D_MODEL)),
                  _full_spec((D_MODEL, IN_COLS))],
        out_specs=[_tok_spec(IN_COLS), _tok_spec(D_MODEL)],
        out_shape=[jax.ShapeDtypeStruct((T_ALL, IN_COLS), BF16),
                   jax.ShapeDtypeStruct((T_ALL, D_MODEL), F32)],
        compiler_params=_cparams(("arbitrary",)),
        name="inproj_next",
    )(x, ys, route, mod_prev, mod, g, w_bf16)


CONV_PAD = 16
CONV_CHUNK = 64


CONV_SPAN = CONV_CHUNK + 2 * CONV_PAD - SUBLANES


def _conv_kernel(seq, z_ref, w_ref, b_ref, g_ref, be_ref, o_ref, upad_ref, shift_ref):
    zeros = jnp.zeros((CONV_PAD, CONV_CH), F32)
    upad_ref[0:CONV_PAD, :] = zeros
    upad_ref[seq + CONV_PAD:seq + 2 * CONV_PAD, :] = zeros

    def glu(ci, carry):
        base = pl.multiple_of(ci * 256, 256)
        zc = z_ref[pl.ds(base, 256), :].astype(F32)
        upad_ref[pl.ds(base + CONV_PAD, 256), :] = zc[:, :CONV_CH] * _sigmoid(zc[:, CONV_CH:])
        return carry

    lax.fori_loop(0, seq // 256, glu, 0)

    shift = CONV_PAD - CONV_K // 2

    def chunk(ci, carry):
        base = pl.multiple_of(ci * CONV_CHUNK, CONV_CHUNK)
        win = upad_ref[pl.ds(base, CONV_CHUNK + 2 * CONV_PAD), :]
        acc = jnp.zeros((CONV_CHUNK, CONV_CH), F32)
        for sub in range(SUBLANES):
            shift_ref[sub] = win[sub:sub + CONV_SPAN, :]
            for k in range(CONV_K):
                if (k + shift) % SUBLANES == sub:
                    lo = k + shift - sub
                    acc = acc + w_ref[k:k + 1, :] * shift_ref[sub, lo:lo + CONV_CHUNK, :]
        acc = acc + b_ref[...]
        mu = jnp.mean(acc, axis=-1, keepdims=True)
        d = acc - mu
        var = jnp.mean(d * d, axis=-1, keepdims=True)
        n = d * lax.rsqrt(var + EPS) * g_ref[...] + be_ref[...]
        o_ref[pl.ds(base, CONV_CHUNK), :] = (n * _sigmoid(n)).astype(BF16)
        return carry

    lax.fori_loop(0, seq // CONV_CHUNK, chunk, 0)


def _conv(z, row_block0, nseq, seq, w, b, g, be):
    return pl.pallas_call(
        functools.partial(_conv_kernel, seq),
        grid=(nseq,),
        in_specs=[pl.BlockSpec((seq, 2 * CONV_CH), lambda s: (row_block0 + s, 0)),
                  _full_spec((CONV_K, CONV_CH)), _full_spec((1, CONV_CH)),
                  _full_spec((1, CONV_CH)), _full_spec((1, CONV_CH))],
        out_specs=pl.BlockSpec((seq, CONV_CH), lambda s: (s, 0)),
        out_shape=jax.ShapeDtypeStruct((nseq * seq, CONV_CH), BF16),
        scratch_shapes=[pltpu.VMEM((seq + 2 * CONV_PAD, CONV_CH), F32),
                        pltpu.VMEM((SUBLANES, CONV_SPAN, CONV_CH), F32)],
        compiler_params=_cparams(("arbitrary",)),
        name="conv_seq%d" % seq,
    )(z, w, b, g, be)


def _dot_nt(a, b):
    return lax.dot_general(a, b, (((1,), (1,)), ((), ())), preferred_element_type=F32)


NA_SCALE = NA_DIM ** -0.5
assert NA_SCALE == 2.0 ** round(np.log2(NA_SCALE)), "query pre-scaling assumes a power-of-two scale"


def _ctx_attn_kernel(layer, q_ref, k_ref, v_ref, *refs):
    if layer:
        kprev_ref, vprev_ref, o_ref, ko_ref, vo_ref = refs
    else:
        o_ref, ko_ref, vo_ref = refs
    for j in range(DEPTH):
        if j < layer:
            ko_ref[j] = kprev_ref[j]
            vo_ref[j] = vprev_ref[j]
        elif j > layer:
            ko_ref[j] = jnp.zeros(ko_ref.shape[1:], F32)
            vo_ref[j] = jnp.zeros(vo_ref.shape[1:], F32)
    scale = NA_SCALE
    outs = []
    for h in range(NA_HEADS):
        cols = slice(h * NA_DIM, (h + 1) * NA_DIM)
        qh, kh, vh = q_ref[:, cols], k_ref[:, cols], v_ref[:, cols]
        ko_ref[layer, h] = kh.astype(F32)
        vo_ref[layer, h] = vh.astype(F32)
        s = _dot_nt(qh, kh) * scale
        m = jnp.max(s, axis=-1, keepdims=True)
        p = jnp.exp(s - m)
        den = jnp.sum(p, axis=-1, keepdims=True)
        o = jnp.dot(p.astype(BF16), vh, preferred_element_type=F32)
        outs.append(o / den)
    o_ref[...] = jnp.concatenate(outs, axis=-1).astype(BF16)


def _ctx_attn(z, layer, k_prev=None, v_prev=None):
    qb, kb, vb = COL_NA_Q // NA_WIDTH, COL_NA_K // NA_WIDTH, COL_NA_V // NA_WIDTH
    head_shape = jax.ShapeDtypeStruct((BATCH, DEPTH, NA_HEADS, SEQ, NA_DIM), F32)
    head_spec = pl.BlockSpec((None, DEPTH, NA_HEADS, SEQ, NA_DIM), lambda b: (b, 0, 0, 0, 0))
    in_specs = [pl.BlockSpec((SEQ, NA_WIDTH), lambda b: (b, qb)),
                pl.BlockSpec((SEQ, NA_WIDTH), lambda b: (b, kb)),
                pl.BlockSpec((SEQ, NA_WIDTH), lambda b: (b, vb))]
    args = [z, z, z]
    aliases = {}
    if layer:
        in_specs += [head_spec, head_spec]
        args += [k_prev, v_prev]
        aliases = {3: 1, 4: 2}
    return pl.pallas_call(
        functools.partial(_ctx_attn_kernel, layer),
        grid=(BATCH,),
        in_specs=in_specs,
        out_specs=[pl.BlockSpec((SEQ, NA_WIDTH), lambda b: (b, 0)), head_spec, head_spec],
        out_shape=[jax.ShapeDtypeStruct((T_CTX, NA_WIDTH), BF16), head_shape, head_shape],
        input_output_aliases=aliases,
        compiler_params=_cparams(("arbitrary",)),
        name="ctx_attn",
    )(*args)


NA_KINDS = (0, NA_ROWS, GRID_H - NA_ROWS)
N_DR = 2 * NA_KH - 1
N_DC = 2 * NA_KW - 1


def _na_row_offset(r0, i, j):
    ks = min(max(r0 - NA_KH // 2, 0), GRID_H - NA_KROWS)
    r, kr = r0 + i, ks + j
    rs = min(max(r - NA_KH // 2, 0), GRID_H - NA_KH)
    return kr - r + NA_KH - 1 if rs <= kr < rs + NA_KH else None


def _na_bias_kernel(rpb_ref, o_ref):
    lh = pl.program_id(0)
    shape = (GRID_W, 2 * GRID_W)
    qc = lax.broadcasted_iota(jnp.int32, shape, 0)
    lane = lax.broadcasted_iota(jnp.int32, shape, 1)
    kc = lane % GRID_W
    dc = jnp.clip(kc - qc, -(NA_KW - 1), NA_KW - 1) + NA_KW - 1
    cs = jnp.clip(qc - NA_KW // 2, 0, GRID_W - NA_KW)
    col_ok = (kc >= cs) & (kc < cs + NA_KW)
    neg = jnp.full(shape, NEG_INF, F32)
    tiles = []
    for dr in range(N_DR):
        base = (lh * N_DR + dr) * N_DC
        val = jnp.zeros(shape, F32)
        for d in range(N_DC):
            val = jnp.where(dc == d, rpb_ref[base + d], val)
        tiles.append(jnp.where(col_ok, val, neg))
    left = lane < GRID_W
    for kind, r0 in enumerate(NA_KINDS):
        for i in range(NA_ROWS):
            for jp in range(NA_KROWS // 2):
                dl, dr_ = _na_row_offset(r0, i, 2 * jp), _na_row_offset(r0, i, 2 * jp + 1)
                tl = neg if dl is None else tiles[dl]
                tr = neg if dr_ is None else tiles[dr_]
                o_ref[kind, i * GRID_W:(i + 1) * GRID_W, jp * 2 * GRID_W:(jp + 1) * 2 * GRID_W] = (
                    jnp.where(left, tl, tr))


def _na_bias_tables(rpb):
    return pl.pallas_call(
        _na_bias_kernel,
        grid=(DEPTH * NA_HEADS,),
        in_specs=[pl.BlockSpec(memory_space=pltpu.SMEM)],
        out_specs=pl.BlockSpec((None, len(NA_KINDS), NA_Q, NA_KEYS), lambda i: (i, 0, 0, 0)),
        out_shape=jax.ShapeDtypeStruct((DEPTH * NA_HEADS, len(NA_KINDS), NA_Q, NA_KEYS), F32),
        compiler_params=_cparams(("arbitrary",)),
        name="nbr_bias",
    )(rpb.reshape(-1))


NA_G = 4


def _na_kernel(q_ref, k_ref, v_ref, kc_ref, vc_ref, bias_ref, o_ref):
    rb = pl.program_id(2)
    ks = jnp.clip(rb * NA_ROWS - NA_KH // 2, 0, GRID_H - NA_KROWS)
    start = pl.multiple_of(ks * GRID_W, GRID_W)
    q = q_ref[...] * NA_SCALE
    kl = k_ref[pl.ds(start, NA_KEYS), :]
    vl = v_ref[pl.ds(start, NA_KEYS), :]
    ones_loc = jnp.ones((NA_KEYS, NA_DIM), BF16)
    ones_ctx = jnp.ones((PAST_LEN, NA_DIM), BF16)

    def scores(hh):
        cols = slice(hh * NA_DIM, (hh + 1) * NA_DIM)
        qh = q[:, cols]
        return _dot_nt(qh, kl[:, cols]) + bias_ref[hh], _dot_nt(qh, kc_ref[hh].astype(BF16))

    outs = []
    nxt = scores(0)
    for hh in range(NA_G):
        s_loc, s_ctx = nxt
        if hh + 1 < NA_G:
            nxt = scores(hh + 1)
        cols = slice(hh * NA_DIM, (hh + 1) * NA_DIM)
        m = jnp.maximum(jnp.max(s_loc, axis=-1, keepdims=True), jnp.max(s_ctx, axis=-1, keepdims=True))
        p_loc = jnp.exp(s_loc - m).astype(BF16)
        p_ctx = jnp.exp(s_ctx - m).astype(BF16)
        v_ext = jnp.concatenate([vl[:, cols], ones_loc], axis=-1)
        vc_ext = jnp.concatenate([vc_ref[hh].astype(BF16), ones_ctx], axis=-1)
        o = (jnp.dot(p_loc, v_ext, preferred_element_type=F32)
             + jnp.dot(p_ctx, vc_ext, preferred_element_type=F32))
        outs.append(o[:, :NA_DIM] / o[:, NA_DIM:])
    o_ref[...] = jnp.concatenate(outs, axis=-1).astype(BF16)


def _na_attn(z, cache_k, cache_v, bias, layer):
    lat_q0 = T_CTX // NA_Q
    lat_s0 = T_CTX // DEC_SEQ
    width = NA_G * NA_DIM
    qc, kc, vc = COL_NA_Q // width, COL_NA_K // width, COL_NA_V // width
    groups = NA_HEADS // NA_G

    def kind(rb):
        return jnp.where(rb == 0, 0, jnp.where(rb == NA_RB - 1, 2, 1))

    ctx_spec = pl.BlockSpec((None, None, NA_G, PAST_LEN, NA_DIM), lambda b, hg, rb: (b, layer, hg, 0, 0))
    return pl.pallas_call(
        _na_kernel,
        grid=(DEC_BATCH, groups, NA_RB),
        in_specs=[pl.BlockSpec((NA_Q, width), lambda b, hg, rb: (lat_q0 + b * NA_RB + rb, qc + hg)),
                  pl.BlockSpec((DEC_SEQ, width), lambda b, hg, rb: (lat_s0 + b, kc + hg)),
                  pl.BlockSpec((DEC_SEQ, width), lambda b, hg, rb: (lat_s0 + b, vc + hg)),
                  ctx_spec, ctx_spec,
                  pl.BlockSpec((NA_G, None, NA_Q, NA_KEYS),
                               lambda b, hg, rb: (layer * groups + hg, kind(rb), 0, 0))],
        out_specs=pl.BlockSpec((NA_Q, width), lambda b, hg, rb: (b * NA_RB + rb, hg)),
        out_shape=jax.ShapeDtypeStruct((T_LAT, NA_WIDTH), BF16),
        compiler_params=_cparams(("arbitrary", "arbitrary", "arbitrary")),
        name="nbr_attn",
    )(z, z, z, cache_k, cache_v, bias)


def _rope_tables():
    n_freq = RET_DIM // 4
    t = np.arange(DEC_SEQ)
    inv = jnp.asarray(ROPE_BASE, F32) ** (-jnp.arange(n_freq, dtype=F32) / n_freq)
    ang_r = jnp.asarray(t // GRID_W, F32)[:, None] * inv[None, :]
    ang_c = jnp.asarray(t % GRID_W, F32)[:, None] * inv[None, :]
    cos = jnp.concatenate([jnp.cos(ang_r)] * 2 + [jnp.cos(ang_c)] * 2, axis=-1)
    sin = jnp.concatenate([-jnp.sin(ang_r), jnp.sin(ang_r), -jnp.sin(ang_c), jnp.sin(ang_c)], axis=-1)
    return jnp.tile(cos, (1, RET_HEADS)), jnp.tile(sin, (1, RET_HEADS))


def _ret_kernel(seq, latent, *refs):
    if latent:
        (lg_ref, z_ref, gn_ref, cos_ref, sin_ref, s0f_ref, s0b_ref, y_ref,
         q_s, k_s, kv_s, st_s) = refs
    else:
        lg_ref, z_ref, gn_ref, y_ref, sf_ref, sb_ref, q_s, k_s, kv_s, st_s = refs
    nc = seq // RET_CHUNK
    ch, hd = RET_CHUNK, RET_DIM
    half = RET_DIM // 4

    row = lax.broadcasted_iota(jnp.int32, (ch, ch), 0).astype(F32)
    col = lax.broadcasted_iota(jnp.int32, (ch, ch), 1).astype(F32)
    pos = lax.broadcasted_iota(jnp.int32, (ch, hd), 0).astype(F32)
    decay, q_dec, k_dec, c_dec_f, c_dec_b = [], [], [], [], []
    for h in range(RET_HEADS):
        lf, lb = lg_ref[0, h], lg_ref[1, h]
        d_f = jnp.where(row >= col, jnp.exp(jnp.maximum(row - col, 0.0) * lf), 0.0)
        d_b = jnp.where(col >= row, jnp.exp(jnp.maximum(col - row, 0.0) * lb), 0.0)
        decay.append(d_f + d_b)
        q_dec.append(jnp.concatenate([jnp.exp((pos + 1.0) * lf), jnp.exp((ch - pos) * lb)], axis=-1))
        k_dec.append(jnp.concatenate([jnp.exp((ch - 1.0 - pos) * lf), jnp.exp(pos * lb)], axis=-1))
        c_dec_f.append(jnp.exp(jnp.zeros((hd, hd), F32) + ch * lf))
        c_dec_b.append(jnp.exp(jnp.zeros((hd, hd), F32) + ch * lb))

    if latent:
        lane = lax.broadcasted_iota(jnp.int32, (ch, RET_WIDTH), 1)
        first_half = (lane % (2 * half)) < half

    def rope(x, base):
        if not latent:
            return x
        swapped = jnp.where(first_half, pltpu.roll(x, RET_WIDTH - half, 1), pltpu.roll(x, half, 1))
        return x * cos_ref[pl.ds(base, ch), :] + swapped * sin_ref[pl.ds(base, ch), :]

    def pass1(n, carry):
        base = pl.multiple_of(n * ch, ch)
        zc = z_ref[pl.ds(base, ch), :]
        q = rope(zc[:, 0:RET_WIDTH].astype(F32), base)
        k = rope(zc[:, RET_WIDTH:2 * RET_WIDTH].astype(F32) * (RET_DIM ** -0.5), base)
        q_s[pl.ds(base, ch), :] = q.astype(BF16)
        k_s[pl.ds(base, ch), :] = k.astype(BF16)
        v = zc[:, 2 * RET_WIDTH:3 * RET_WIDTH]
        for h in range(RET_HEADS):
            cols = slice(h * hd, (h + 1) * hd)
            kh = k[:, cols]
            k2 = (jnp.concatenate([kh, kh], axis=-1) * k_dec[h]).astype(BF16)
            kv_s[n, h] = lax.dot_general(k2, v[:, cols], (((0,), (0,)), ((), ())),
                                         preferred_element_type=F32)
        return carry

    lax.fori_loop(0, nc, pass1, 0, unroll=2)

    for h in range(RET_HEADS):
        if latent:
            s_f, s_b = s0f_ref[h], s0b_ref[h]
        else:
            s_f = s_b = jnp.zeros((hd, hd), F32)

        def fwd(n, s, h=h):
            st_s[n, h, 0:hd, :] = s
            return c_dec_f[h] * s + kv_s[n, h, 0:hd, :]

        def bwd(i, s, h=h):
            n = nc - 1 - i
            st_s[n, h, hd:2 * hd, :] = s
            return c_dec_b[h] * s + kv_s[n, h, hd:2 * hd, :]

        s_f = lax.fori_loop(0, nc, fwd, s_f)
        s_b = lax.fori_loop(0, nc, bwd, s_b)
        if not latent:
            sf_ref[h] = s_f
            sb_ref[h] = s_b

    def pass3(n, carry):
        base = pl.multiple_of(n * ch, ch)
        zc = z_ref[pl.ds(base, ch), :]
        q = q_s[pl.ds(base, ch), :]
        k = k_s[pl.ds(base, ch), :]
        v = zc[:, 2 * RET_WIDTH:3 * RET_WIDTH]
        gate = zc[:, 3 * RET_WIDTH:4 * RET_WIDTH].astype(F32)
        outs = []
        for h in range(RET_HEADS):
            cols = slice(h * hd, (h + 1) * hd)
            qh = q[:, cols]
            s = _dot_nt(qh, k[:, cols]) * decay[h]
            o = jnp.dot(s.astype(BF16), v[:, cols], preferred_element_type=F32)
            qf = qh.astype(F32)
            q2 = (jnp.concatenate([qf, qf], axis=-1) * q_dec[h]).astype(BF16)
            o = o + jnp.dot(q2, st_s[n, h].astype(BF16), preferred_element_type=F32)
            mu = jnp.mean(o, axis=-1, keepdims=True)
            d = o - mu
            var = jnp.mean(d * d, axis=-1, keepdims=True)
            outs.append(d * lax.rsqrt(var + EPS))
        nrm = jnp.concatenate(outs, axis=-1)
        y_ref[pl.ds(base, ch), :] = (nrm * gn_ref[...] * (gate * _sigmoid(gate))).astype(BF16)
        return carry

    lax.fori_loop(0, nc, pass3, 0, unroll=2)


def _retention(z, lg, gn_g, latent, layer=None, rope=None, s0_f=None, s0_b=None):
    seq = DEC_SEQ if latent else SEQ
    nseq = DEC_BATCH if latent else BATCH
    nc = seq // RET_CHUNK
    row0 = (T_CTX // DEC_SEQ) if latent else 0
    cb = COL_RET // (4 * RET_WIDTH)
    in_specs = [pl.BlockSpec(memory_space=pltpu.SMEM),
                pl.BlockSpec((seq, 4 * RET_WIDTH), lambda s: (row0 + s, cb)),
                _full_spec((1, RET_WIDTH))]
    args = [lg, z, gn_g]
    state_shape = jax.ShapeDtypeStruct((nseq, RET_HEADS, RET_DIM, RET_DIM), F32)
    y_spec = pl.BlockSpec((seq, RET_WIDTH), lambda s: (s, 0))
    y_shape = jax.ShapeDtypeStruct((nseq * seq, RET_WIDTH), BF16)
    if latent:
        st_spec = pl.BlockSpec((None, None, RET_HEADS, RET_DIM, RET_DIM), lambda s: (s, layer, 0, 0, 0))
        in_specs += [_full_spec((seq, RET_WIDTH)), _full_spec((seq, RET_WIDTH)), st_spec, st_spec]
        args += [rope[0], rope[1], s0_f, s0_b]
        out_specs, out_shape = y_spec, y_shape
    else:
        so_spec = pl.BlockSpec((None, RET_HEADS, RET_DIM, RET_DIM), lambda s: (s, 0, 0, 0))
        out_specs, out_shape = [y_spec, so_spec, so_spec], [y_shape, state_shape, state_shape]
    return pl.pallas_call(
        functools.partial(_ret_kernel, seq, latent),
        grid=(nseq,),
        in_specs=in_specs,
        out_specs=out_specs,
        out_shape=out_shape,
        scratch_shapes=[pltpu.VMEM((seq, RET_WIDTH), BF16), pltpu.VMEM((seq, RET_WIDTH), BF16),
                        pltpu.VMEM((nc, RET_HEADS, 2 * RET_DIM, RET_DIM), F32),
                        pltpu.VMEM((nc, RET_HEADS, 2 * RET_DIM, RET_DIM), F32)],
        compiler_params=_cparams(("arbitrary",)),
        name="retention_lat" if latent else "retention_ctx",
    )(*args)


def _route(logits):
    lane = lax.broadcasted_iota(jnp.int32, logits.shape, 1)
    lane_f = lane.astype(F32)
    big = float(ROUTE_COLS)
    neg = -jnp.inf
    is_grp = lane < N_GROUPS
    gl = jnp.where(is_grp, logits, neg)
    gmax = jnp.max(gl, axis=-1, keepdims=True)
    grp = jnp.min(jnp.where(gl == gmax, lane_f, big), axis=-1, keepdims=True)
    p_grp = 1.0 / jnp.sum(jnp.exp(gl - gmax), axis=-1, keepdims=True)
    e_f = lane_f - N_GROUPS
    lo = grp * EXPERTS_PER_GROUP
    in_grp = (e_f >= lo) & (e_f < lo + EXPERTS_PER_GROUP)
    el = jnp.where(in_grp, logits, neg)
    m1 = jnp.max(el, axis=-1, keepdims=True)
    i1 = jnp.min(jnp.where(el == m1, lane_f, big), axis=-1, keepdims=True)
    el2 = jnp.where(lane_f == i1, neg, el)
    m2 = jnp.max(el2, axis=-1, keepdims=True)
    i2 = jnp.min(jnp.where(el2 == m2, lane_f, big), axis=-1, keepdims=True)
    t = jnp.exp(m2 - m1)
    g1 = p_grp / (1.0 + t)
    g2 = p_grp * t / (1.0 + t)
    rows = logits.shape[0]
    oh1, oh2 = lane_f == i1, lane_f == i2
    oh = jnp.where(oh1 | oh2, 1.0, 0.0)
    tri = (lax.broadcasted_iota(jnp.int32, (rows, rows), 0)
           > lax.broadcasted_iota(jnp.int32, (rows, rows), 1))
    rank = jnp.dot(jnp.where(tri, 1.0, 0.0).astype(BF16), oh.astype(BF16), preferred_element_type=F32)
    tiles = jnp.floor((jnp.sum(oh, axis=0, keepdims=True) + (SUBLANES - 1)) * (1.0 / SUBLANES))
    upper = (lax.broadcasted_iota(jnp.int32, (ROUTE_COLS, ROUTE_COLS), 0)
             < lax.broadcasted_iota(jnp.int32, (ROUTE_COLS, ROUTE_COLS), 1))
    start = SUBLANES * jnp.dot(jnp.broadcast_to(tiles, (SUBLANES, ROUTE_COLS)).astype(BF16),
                               jnp.where(upper, 1.0, 0.0).astype(BF16),
                               preferred_element_type=F32)[0:1, :]
    pos = start + rank
    p1 = jnp.sum(jnp.where(oh1, pos, 0.0), axis=-1, keepdims=True)
    p2 = jnp.sum(jnp.where(oh2, pos, 0.0), axis=-1, keepdims=True)
    out = jnp.zeros(logits.shape, F32)
    for k, val in enumerate((i1 - N_GROUPS, i2 - N_GROUPS, g1, g2, p1, p2)):
        out = jnp.where(lane == k, val, out)
    return out, SUBLANES * tiles


def _outproj_kernel(ycc, ycl, ync, ynl, yrc, yrl, x_ref, mod_ref, g_ref, w_ref, wr_ref, br_ref,
                    xo_ref, xs_ref, r_ref, seg_ref):
    is_ctx = pl.program_id(0) < NB_CTX
    yc = jnp.where(is_ctx, ycc[...], ycl[...])
    yn = jnp.where(is_ctx, ync[...], ynl[...])
    yr = jnp.where(is_ctx, yrc[...], yrl[...])
    y = (jnp.dot(yc, w_ref[0:CONV_CH, :], preferred_element_type=F32)
         + jnp.dot(yn, w_ref[CONV_CH:CONV_CH + NA_WIDTH, :], preferred_element_type=F32)
         + jnp.dot(yr, w_ref[CONV_CH + NA_WIDTH:, :], preferred_element_type=F32))
    x = x_ref[...] + mod_ref[2:3, :] * y
    xo_ref[...] = x
    h = _norm_mod(x, g_ref[...], mod_ref[3:4, :], mod_ref[4:5, :])
    h_hi = h.astype(BF16)
    h_lo = (h - h_hi.astype(F32)).astype(BF16)
    hw = jnp.dot(h_hi, wr_ref[...], preferred_element_type=F32)
    logits = (hw[:, :ROUTE_COLS] + hw[:, ROUTE_COLS:]
              + jnp.dot(h_lo, wr_ref[:, :ROUTE_COLS], preferred_element_type=F32) + br_ref[...])
    route, seg = _route(logits)
    r_ref[...] = route
    seg_ref[...] = jnp.broadcast_to(seg, seg_ref.shape)
    sel = _slot_onehot(route, 0) | _slot_onehot(route, 1)
    xs_ref[...] = lax.dot_general(jnp.where(sel, 1.0, 0.0).astype(BF16), h_hi, (((0,), (0,)), ((), ())),
                                  preferred_element_type=F32)


def _outproj(y_conv, y_na, y_ret, x, mod, g, w_bf16, w_route, b_route):
    return pl.pallas_call(
        _outproj_kernel,
        grid=(NB_ALL,),
        in_specs=(_ctx_lat_specs(CONV_CH) + _ctx_lat_specs(NA_WIDTH) + _ctx_lat_specs(RET_WIDTH)
                  + [_tok_spec(D_MODEL), _mod_spec(), _full_spec((1, D_MODEL)),
                     _full_spec((D_MODEL, D_MODEL)), _full_spec((D_MODEL, 2 * ROUTE_COLS)),
                     _full_spec((1, ROUTE_COLS))]),
        out_specs=[_tok_spec(D_MODEL), pl.BlockSpec((MOE_LC, D_MODEL), lambda i: (i, 0)),
                   _tok_spec(ROUTE_COLS), pl.BlockSpec((None, SUBLANES, ROUTE_COLS), lambda i: (i, 0, 0))],
        out_shape=[jax.ShapeDtypeStruct((T_ALL, D_MODEL), F32),
                   jax.ShapeDtypeStruct((NB_ALL * MOE_LC, D_MODEL), F32),
                   jax.ShapeDtypeStruct((T_ALL, ROUTE_COLS), F32),
                   jax.ShapeDtypeStruct((NB_ALL, SUBLANES, ROUTE_COLS), F32)],
        compiler_params=_cparams(("arbitrary",)),
        name="outproj_route",
    )(y_conv[0], y_conv[1], y_na[0], y_na[1], y_ret[0], y_ret[1], x, mod, g, w_bf16, w_route, b_route)


def _dispatch_tables(seg):
    seg_len = seg[:, 0, N_GROUPS:N_GROUPS + N_EXPERTS].astype(jnp.int32)
    experts = jnp.arange(N_EXPERTS, dtype=jnp.int32)
    in_chunk = jnp.cumsum(seg_len, axis=1) - seg_len
    seg_row = in_chunk + MOE_LC * jnp.arange(NB_ALL, dtype=jnp.int32)[:, None]
    seg_off = jnp.cumsum(seg_len, axis=0) - seg_len
    rows_e = jnp.sum(seg_len, axis=0)
    chunk_rows = jnp.sum(seg_len, axis=1)
    nblk = (rows_e + MOE_BLK - 1) // MOE_BLK
    blk_end = jnp.cumsum(nblk)
    blk_start = blk_end - nblk
    blk = jnp.arange(MOE_NBLK, dtype=jnp.int32)
    n_active = blk_end[-1]
    blk_e = jnp.minimum(jnp.sum((blk_end[None, :] <= jnp.minimum(blk, n_active - 1)[:, None]).astype(jnp.int32),
                                axis=-1), N_EXPERTS - 1)
    mine = blk_e[:, None] == experts[None, :]
    blk_lo = (blk - jnp.sum(jnp.where(mine, blk_start[None, :], 0), axis=-1)) * MOE_BLK
    left = jnp.sum(jnp.where(mine, rows_e[None, :], 0), axis=-1) - blk_lo
    blk_nv = jnp.where(blk < n_active, jnp.clip(left, 0, MOE_BLK), 0).astype(jnp.int32)
    off_b = jnp.sum(jnp.where(mine[:, None, :], seg_off[None, :, :], 0), axis=-1)
    end_b = off_b + jnp.sum(jnp.where(mine[:, None, :], seg_len[None, :, :], 0), axis=-1)
    blk_c0 = jnp.sum((end_b <= blk_lo[:, None]).astype(jnp.int32), axis=-1)
    blk_c1 = jnp.sum((off_b < (blk_lo + blk_nv)[:, None]).astype(jnp.int32), axis=-1)
    return (blk_e, blk_lo.astype(jnp.int32), blk_nv, blk_c0, blk_c1, seg_off.reshape(-1),
            seg_len.reshape(-1), seg_row.reshape(-1), chunk_rows)


def _moe_kernel(blk_e, blk_lo, blk_nv, blk_c0, blk_c1, seg_off, seg_len, seg_row, chunk_rows,
                xs_hbm, w1_ref, w3_ref, w2_ref, ys_hbm, xbuf, obuf, zeros, w1b, w3b, w2b, gsem, ssem, zsem):
    i = pl.program_id(0)
    last = pl.num_programs(0) - 1
    slot = i % 2

    def tiles(v):
        return pl.multiple_of(v, SUBLANES)

    def for_segments(blk, fn):
        lo = blk_lo[blk]
        hi = lo + blk_nv[blk]

        def body(c, carry):
            k = c * N_EXPERTS + blk_e[blk]
            s_lo = seg_off[k]
            a = jnp.maximum(s_lo, lo)
            n = jnp.minimum(s_lo + seg_len[k], hi) - a

            @pl.when(n > 0)
            def _():
                fn(tiles(seg_row[k] + a - s_lo), tiles(a - lo), tiles(n))

            return carry

        lax.fori_loop(blk_c0[blk], blk_c1[blk], body, 0)

    def start_gathers(blk, s):
        for_segments(blk, lambda src, dst, n: pltpu.make_async_copy(
            xs_hbm.at[pl.ds(src, n)], xbuf.at[s, pl.ds(dst, n)], gsem.at[s]).start())

    def start_scatters(blk, s):
        for_segments(blk, lambda dst, src, n: pltpu.make_async_copy(
            obuf.at[s, pl.ds(src, n)], ys_hbm.at[pl.ds(dst, n)], ssem.at[s]).start())

    def wait_rows(blk, s, sem):
        n = tiles(blk_nv[blk])

        @pl.when(n > 0)
        def _():
            pltpu.make_async_copy(xs_hbm.at[pl.ds(0, n)], xbuf.at[s, pl.ds(0, n)], sem.at[s]).wait()

    @pl.when(i == 0)
    def _():
        xbuf[...] = jnp.zeros_like(xbuf)
        zeros[...] = jnp.zeros_like(zeros)

        def tail(c):
            n = tiles(MOE_LC - chunk_rows[c])
            return n, pltpu.make_async_copy(zeros.at[pl.ds(0, n)],
                                            ys_hbm.at[pl.ds(tiles(c * MOE_LC + chunk_rows[c]), n)], zsem)

        def fill(c, carry):
            n, copy = tail(c)
            pl.when(n > 0)(copy.start)
            return carry

        def drain(c, carry):
            n, copy = tail(c)
            pl.when(n > 0)(copy.wait)
            return carry

        lax.fori_loop(0, NB_ALL, fill, 0)
        lax.fori_loop(0, NB_ALL, drain, 0)
        start_gathers(0, 0)

    @pl.when(i < last)
    def _():
        start_gathers(i + 1, 1 - slot)

    @pl.when(i >= 2)
    def _():
        wait_rows(i - 2, slot, ssem)

    @pl.when(blk_nv[i] > 0)
    def _():
        @pl.when((i == 0) | (blk_e[i] != blk_e[jnp.maximum(i - 1, 0)]))
        def _():
            w1b[...] = w1_ref[...].astype(BF16)
            w3b[...] = w3_ref[...].astype(BF16)
            w2b[...] = w2_ref[...].astype(BF16)

        wait_rows(i, slot, gsem)
        xb = xbuf[slot].astype(BF16)
        a = jnp.dot(xb, w1b[...], preferred_element_type=F32)
        b = jnp.dot(xb, w3b[...], preferred_element_type=F32)
        mid = (a * _sigmoid(a) * b).astype(BF16)
        obuf[slot] = jnp.dot(mid, w2b[...], preferred_element_type=F32)
        start_scatters(i, slot)

    @pl.when(i == last)
    def _():
        wait_rows(i - 1, 1 - slot, ssem)
        wait_rows(i, slot, ssem)


def _moe(xs, w1, w3, w2, layer, blk_e, blk_lo, blk_nv, blk_c0, blk_c1, seg_off, seg_len, seg_row,
         chunk_rows):
    def w_spec(rows, cols):
        return pl.BlockSpec((None, None, rows, cols), lambda i, be, *_: (layer, be[i], 0, 0))

    grid_spec = pltpu.PrefetchScalarGridSpec(
        num_scalar_prefetch=9,
        grid=(MOE_NBLK,),
        in_specs=[pl.BlockSpec(memory_space=pl.ANY), w_spec(D_MODEL, D_EXPERT),
                  w_spec(D_MODEL, D_EXPERT), w_spec(D_EXPERT, D_MODEL)],
        out_specs=pl.BlockSpec(memory_space=pl.ANY),
        scratch_shapes=[pltpu.VMEM((2, MOE_BLK, D_MODEL), F32), pltpu.VMEM((2, MOE_BLK, D_MODEL), F32),
                        pltpu.VMEM((MOE_LC - 2 * TM, D_MODEL), F32),
                        pltpu.VMEM((D_MODEL, D_EXPERT), BF16), pltpu.VMEM((D_MODEL, D_EXPERT), BF16),
                        pltpu.VMEM((D_EXPERT, D_MODEL), BF16),
                        pltpu.SemaphoreType.DMA((2,)), pltpu.SemaphoreType.DMA((2,)),
                        pltpu.SemaphoreType.DMA])
    return pl.pallas_call(
        _moe_kernel,
        grid_spec=grid_spec,
        out_shape=jax.ShapeDtypeStruct((NB_ALL * MOE_LC, D_MODEL), F32),
        compiler_params=_cparams(("arbitrary",)),
        name="moe_experts",
    )(blk_e, blk_lo, blk_nv, blk_c0, blk_c1, seg_off, seg_len, seg_row, chunk_rows, xs, w1, w3, w2)


def _final_kernel(x_ref, ys_ref, r_ref, mod_ref, g_ref, o_ref):
    x = _moe_residual(x_ref, ys_ref, r_ref, mod_ref)
    ms = jnp.mean(x * x, axis=-1, keepdims=True)
    o_ref[...] = x * lax.rsqrt(ms + EPS) * g_ref[...]


def _final(x, ys, route, mod, g, block0, nblocks):
    return pl.pallas_call(
        _final_kernel,
        grid=(nblocks,),
        in_specs=[pl.BlockSpec((TM, D_MODEL), lambda i: (block0 + i, 0)),
                  pl.BlockSpec((MOE_LC, D_MODEL), lambda i: (block0 + i, 0)),
                  pl.BlockSpec((TM, ROUTE_COLS), lambda i: (block0 + i, 0)),
                  pl.BlockSpec((None, 6, D_MODEL), lambda i: (_cond_row(block0 + i), 0, 0)),
                  _full_spec((1, D_MODEL))],
        out_specs=_tok_spec(D_MODEL),
        out_shape=jax.ShapeDtypeStruct((nblocks * TM, D_MODEL), F32),
        compiler_params=_cparams(("arbitrary",)),
        name="final_norm",
    )(x, ys, route, mod, g)


def kernel(x_prompt, x_sample, c, cache_k, cache_v, state_ret_f, state_ret_b, c_ctx, w_ada, b_ada, norm1_g, norm2_g, w_in, w_out, conv_w, conv_b, conv_ln_g, conv_ln_b, na_rpb, ret_lg_f, ret_lg_b, ret_gn_g, w_route_g, b_route_g, w_route_e, b_route_e, w1, w3, w2, final_g):
    cv = jnp.zeros((COND_ROWS, D_MODEL), F32).at[0].set(c_ctx).at[1:N_COND].set(c)
    mods = _ada(cv, w_ada, b_ada).reshape(DEPTH, COND_ROWS, 6, D_MODEL)
    w_in_b = w_in.astype(BF16)
    w_out_b = w_out.astype(BF16)
    pad = ROUTE_COLS - N_GROUPS - N_EXPERTS
    w_route = jnp.pad(jnp.concatenate([w_route_g, w_route_e], axis=-1), ((0, 0), (0, 0), (0, pad)))
    b_route = jnp.pad(jnp.concatenate([b_route_g, b_route_e], axis=-1), ((0, 0), (0, pad)))
    w_route_hi = w_route.astype(BF16)
    w_route_lo = (w_route - w_route_hi.astype(F32)).astype(BF16)
    w_route = jnp.concatenate([w_route_hi, w_route_lo], axis=-1)
    na_bias = _na_bias_tables(na_rpb)
    rope = _rope_tables()
    lg = jnp.stack([ret_lg_f, ret_lg_b], axis=1)

    x_ctx = x_prompt.reshape(T_CTX, D_MODEL)
    x_lat = x_sample.reshape(T_LAT, D_MODEL)
    x = y = route = new_k = new_v = None
    sf_list, sb_list = [], []
    for l in range(DEPTH):
        g1 = norm1_g[l].reshape(1, D_MODEL)
        if l == 0:
            z, x = _inproj_first(x_ctx, x_lat, mods[l], g1, w_in_b[l])
        else:
            z, x = _inproj_next(x, y, route, mods[l - 1], mods[l], g1, w_in_b[l])
        conv_args = (conv_w[l], conv_b[l].reshape(1, -1), conv_ln_g[l].reshape(1, -1),
                     conv_ln_b[l].reshape(1, -1))
        yc_c = _conv(z, 0, BATCH, SEQ, *conv_args)
        yc_l = _conv(z, T_CTX // DEC_SEQ, DEC_BATCH, DEC_SEQ, *conv_args)
        yn_c, new_k, new_v = _ctx_attn(z, l, new_k, new_v)
        yn_l = _na_attn(z, cache_k, cache_v, na_bias, l)
        gn = ret_gn_g[l].reshape(1, RET_WIDTH)
        yr_c, sf_l, sb_l = _retention(z, lg[l], gn, latent=False)
        yr_l = _retention(z, lg[l], gn, latent=True, layer=l, rope=rope,
                          s0_f=state_ret_f, s0_b=state_ret_b)
        x, xs, route, seg = _outproj((yc_c, yc_l), (yn_c, yn_l), (yr_c, yr_l), x, mods[l],
                                     norm2_g[l].reshape(1, D_MODEL), w_out_b[l], w_route[l],
                                     b_route[l].reshape(1, ROUTE_COLS))
        y = _moe(xs, w1, w3, w2, l, *_dispatch_tables(seg))
        sf_list.append(sf_l)
        sb_list.append(sb_l)
    fg = final_g.reshape(1, D_MODEL)
    y_prompt = _final(x, y, route, mods[DEPTH - 1], fg, 0, NB_CTX).reshape(BATCH, SEQ, D_MODEL)
    y_sample = _final(x, y, route, mods[DEPTH - 1], fg, NB_CTX, NB_LAT).reshape(DEC_BATCH, DEC_SEQ, D_MODEL)
    return (y_prompt, y_sample, new_k, new_v, jnp.stack(sf_list, axis=1), jnp.stack(sb_list, axis=1))
```

```python
import functools

import numpy as np
import jax
import jax.numpy as jnp
from jax import lax
from jax.experimental import pallas as pl
from jax.experimental.pallas import tpu as pltpu

D_MODEL = 1024
BATCH = 32
SEQ = 256
DEPTH = 2
DEC_BATCH = 4
DEC_SEQ = 4096
PAST_LEN = 512
GRID_W = 64
GRID_H = DEC_SEQ // GRID_W
CONV_CH = 256
CONV_K = 31
NA_HEADS = 8
NA_DIM = 64
NA_WIDTH = NA_HEADS * NA_DIM
NA_KH = 8
NA_KW = 16
RET_HEADS = 4
RET_DIM = 64
RET_WIDTH = RET_HEADS * RET_DIM
RET_CHUNK = 128
ROPE_BASE = 10000.0
N_GROUPS = 4
EXPERTS_PER_GROUP = 8
N_EXPERTS = N_GROUPS * EXPERTS_PER_GROUP
D_EXPERT = 512
IN_COLS = 2 * CONV_CH + 3 * NA_WIDTH + 4 * RET_WIDTH
EPS = 1e-6
NEG_INF = -1e30

F32 = jnp.float32
BF16 = jnp.bfloat16
HIGHEST = lax.Precision.HIGHEST

T_CTX = BATCH * SEQ
T_LAT = DEC_BATCH * DEC_SEQ
T_ALL = T_CTX + T_LAT
N_COND = 1 + DEC_BATCH
COND_ROWS = 8

TM = 512
NB_CTX = T_CTX // TM
NB_LAT = T_LAT // TM
NB_ALL = NB_CTX + NB_LAT
LAT_BLOCKS_PER_REQ = DEC_SEQ // TM

LANES = 128
SUBLANES = 8
ROUTE_COLS = LANES

COL_CONV = 0
COL_NA_Q = 2 * CONV_CH
COL_NA_K = COL_NA_Q + NA_WIDTH
COL_NA_V = COL_NA_K + NA_WIDTH
COL_RET = COL_NA_V + NA_WIDTH

NA_ROWS = 8
NA_Q = NA_ROWS * GRID_W
NA_KROWS = 2 * NA_ROWS
NA_KEYS = NA_KROWS * GRID_W
NA_RB = GRID_H // NA_ROWS

MOE_BLK = 512
MOE_LC = -(-(2 * TM + N_EXPERTS * (SUBLANES - 1)) // LANES) * LANES
MOE_NBLK = -(-(NB_ALL * MOE_LC) // MOE_BLK) + N_EXPERTS
N_SEG = NB_ALL * N_EXPERTS

VMEM_LIMIT = 56 * 1024 * 1024


def _cparams(sem):
    return pltpu.CompilerParams(dimension_semantics=sem, vmem_limit_bytes=VMEM_LIMIT)


def _sigmoid(x):
    return 1.0 / (1.0 + jnp.exp(-x))


def _cond_row(i):
    return jnp.where(i < NB_CTX, 0, 1 + (i - NB_CTX) // LAT_BLOCKS_PER_REQ)


ADA_TN = 1536


def _ada_kernel(cv_ref, w_ref, b_ref, o_ref):
    cv = cv_ref[...]
    s = cv * _sigmoid(cv)
    o_ref[...] = jnp.dot(s, w_ref[...], precision=HIGHEST, preferred_element_type=F32) + b_ref[...]


def _ada(cv, w_ada, b_ada):
    n = 6 * D_MODEL
    return pl.pallas_call(
        _ada_kernel,
        grid=(DEPTH, n // ADA_TN),
        in_specs=[
            pl.BlockSpec((COND_ROWS, D_MODEL), lambda l, j: (0, 0)),
            pl.BlockSpec((None, D_MODEL, ADA_TN), lambda l, j: (l, 0, j)),
            pl.BlockSpec((None, 1, ADA_TN), lambda l, j: (l, 0, j)),
        ],
        out_specs=pl.BlockSpec((None, COND_ROWS, ADA_TN), lambda l, j: (l, 0, j)),
        out_shape=jax.ShapeDtypeStruct((DEPTH, COND_ROWS, n), F32),
        compiler_params=_cparams(("arbitrary", "arbitrary")),
        name="ada_mod",
    )(cv, w_ada, b_ada.reshape(DEPTH, 1, n))


IN_TN = 768


def _norm_mod(x, g, shift, scale):
    ms = jnp.mean(x * x, axis=-1, keepdims=True)
    return (x * lax.rsqrt(ms + EPS) * g) * (1.0 + scale) + shift


def _inproj_body(x, mod_ref, g_ref, w_ref, z_ref):
    h = _norm_mod(x, g_ref[...], mod_ref[0:1, :], mod_ref[1:2, :]).astype(BF16)
    for c in range(IN_COLS // IN_TN):
        cols = slice(c * IN_TN, (c + 1) * IN_TN)
        z_ref[:, cols] = jnp.dot(h, w_ref[:, cols], preferred_element_type=F32).astype(BF16)


def _inproj_first_kernel(xc_ref, xl_ref, mod_ref, g_ref, w_ref, z_ref, xo_ref):
    i = pl.program_id(0)
    x = jnp.where(i < NB_CTX, xc_ref[...], xl_ref[...])
    xo_ref[...] = x
    _inproj_body(x, mod_ref, g_ref, w_ref, z_ref)


def _slot_onehot(route, slot):
    pos = route[:, 4 + slot:5 + slot].astype(jnp.int32)
    return lax.broadcasted_iota(jnp.int32, (route.shape[0], MOE_LC), 1) == pos


def _moe_residual(x_ref, ys_ref, r_ref, mod_ref):
    r = r_ref[...]
    sel = jnp.where(_slot_onehot(r, 0), r[:, 2:3], jnp.where(_slot_onehot(r, 1), r[:, 3:4], 0.0))
    y = jnp.dot(sel.astype(BF16), ys_ref[...].astype(BF16), preferred_element_type=F32)
    return x_ref[...] + mod_ref[5:6, :] * y


def _inproj_next_kernel(x_ref, ys_ref, r_ref, modp_ref, mod_ref, g_ref, w_ref, z_ref, xo_ref):
    x = _moe_residual(x_ref, ys_ref, r_ref, modp_ref)
    xo_ref[...] = x
    _inproj_body(x, mod_ref, g_ref, w_ref, z_ref)


def _tok_spec(cols):
    return pl.BlockSpec((TM, cols), lambda i: (i, 0))


def _mod_spec():
    return pl.BlockSpec((None, 6, D_MODEL), lambda i: (_cond_row(i), 0, 0))


def _full_spec(shape):
    return pl.BlockSpec(shape, lambda i: (0,) * len(shape))


def _ctx_lat_specs(cols):
    return [pl.BlockSpec((TM, cols), lambda i: (jnp.minimum(i, NB_CTX - 1), 0)),
            pl.BlockSpec((TM, cols), lambda i: (jnp.maximum(i - NB_CTX, 0), 0))]


def _inproj_first(x_ctx, x_lat, mod, g, w_bf16):
    return pl.pallas_call(
        _inproj_first_kernel,
        grid=(NB_ALL,),
        in_specs=_ctx_lat_specs(D_MODEL) + [_mod_spec(), _full_spec((1, D_MODEL)),
                                            _full_spec((D_MODEL, IN_COLS))],
        out_specs=[_tok_spec(IN_COLS), _tok_spec(D_MODEL)],
        out_shape=[jax.ShapeDtypeStruct((T_ALL, IN_COLS), BF16),
                   jax.ShapeDtypeStruct((T_ALL, D_MODEL), F32)],
        compiler_params=_cparams(("arbitrary",)),
        name="inproj_first",
    )(x_ctx, x_lat, mod, g, w_bf16)


def _inproj_next(x, ys, route, mod_prev, mod, g, w_bf16):
    return pl.pallas_call(
        _inproj_next_kernel,
        grid=(NB_ALL,),
        in_specs=[_tok_spec(D_MODEL),
                  pl.BlockSpec((MOE_LC, D_MODEL), lambda i: (i, 0)),
                  _tok_spec(ROUTE_COLS),
                  _mod_spec(), _mod_spec(), _full_spec((1, D_MODEL)),
                  _full_spec((D_MODEL, IN_COLS))],
        out_specs=[_tok_spec(IN_COLS), _tok_spec(D_MODEL)],
        out_shape=[jax.ShapeDtypeStruct((T_ALL, IN_COLS), BF16),
                   jax.ShapeDtypeStruct((T_ALL, D_MODEL), F32)],
        compiler_params=_cparams(("arbitrary",)),
        name="inproj_next",
    )(x, ys, route, mod_prev, mod, g, w_bf16)


CONV_PAD = 16
CONV_CHUNK = 64


CONV_SPAN = CONV_CHUNK + 2 * CONV_PAD - SUBLANES


def _conv_kernel(seq, z_ref, w_ref, b_ref, g_ref, be_ref, o_ref, upad_ref, shift_ref):
    zeros = jnp.zeros((CONV_PAD, CONV_CH), F32)
    upad_ref[0:CONV_PAD, :] = zeros
    upad_ref[seq + CONV_PAD:seq + 2 * CONV_PAD, :] = zeros

    def glu(ci, carry):
        base = pl.multiple_of(ci * 256, 256)
        zc = z_ref[pl.ds(base, 256), :].astype(F32)
        upad_ref[pl.ds(base + CONV_PAD, 256), :] = zc[:, :CONV_CH] * _sigmoid(zc[:, CONV_CH:])
        return carry

    lax.fori_loop(0, seq // 256, glu, 0)

    shift = CONV_PAD - CONV_K // 2

    def chunk(ci, carry):
        base = pl.multiple_of(ci * CONV_CHUNK, CONV_CHUNK)
        win = upad_ref[pl.ds(base, CONV_CHUNK + 2 * CONV_PAD), :]
        acc = jnp.zeros((CONV_CHUNK, CONV_CH), F32)
        for sub in range(SUBLANES):
            shift_ref[sub] = win[sub:sub + CONV_SPAN, :]
            for k in range(CONV_K):
                if (k + shift) % SUBLANES == sub:
                    lo = k + shift - sub
                    acc = acc + w_ref[k:k + 1, :] * shift_ref[sub, lo:lo + CONV_CHUNK, :]
        acc = acc + b_ref[...]
        mu = jnp.mean(acc, axis=-1, keepdims=True)
        d = acc - mu
        var = jnp.mean(d * d, axis=-1, keepdims=True)
        n = d * lax.rsqrt(var + EPS) * g_ref[...] + be_ref[...]
        o_ref[pl.ds(base, CONV_CHUNK), :] = (n * _sigmoid(n)).astype(BF16)
        return carry

    lax.fori_loop(0, seq // CONV_CHUNK, chunk, 0)


def _conv(z, row_block0, nseq, seq, w, b, g, be):
    return pl.pallas_call(
        functools.partial(_conv_kernel, seq),
        grid=(nseq,),
        in_specs=[pl.BlockSpec((seq, 2 * CONV_CH), lambda s: (row_block0 + s, 0)),
                  _full_spec((CONV_K, CONV_CH)), _full_spec((1, CONV_CH)),
                  _full_spec((1, CONV_CH)), _full_spec((1, CONV_CH))],
        out_specs=pl.BlockSpec((seq, CONV_CH), lambda s: (s, 0)),
        out_shape=jax.ShapeDtypeStruct((nseq * seq, CONV_CH), BF16),
        scratch_shapes=[pltpu.VMEM((seq + 2 * CONV_PAD, CONV_CH), F32),
                        pltpu.VMEM((SUBLANES, CONV_SPAN, CONV_CH), F32)],
        compiler_params=_cparams(("arbitrary",)),
        name="conv_seq%d" % seq,
    )(z, w, b, g, be)


def _dot_nt(a, b):
    return lax.dot_general(a, b, (((1,), (1,)), ((), ())), preferred_element_type=F32)


NA_SCALE = NA_DIM ** -0.5
assert NA_SCALE == 2.0 ** round(np.log2(NA_SCALE)), "query pre-scaling assumes a power-of-two scale"


def _ctx_attn_kernel(layer, q_ref, k_ref, v_ref, *refs):
    if layer:
        kprev_ref, vprev_ref, o_ref, ko_ref, vo_ref = refs
    else:
        o_ref, ko_ref, vo_ref = refs
    for j in range(DEPTH):
        if j < layer:
            ko_ref[j] = kprev_ref[j]
            vo_ref[j] = vprev_ref[j]
        elif j > layer:
            ko_ref[j] = jnp.zeros(ko_ref.shape[1:], F32)
            vo_ref[j] = jnp.zeros(vo_ref.shape[1:], F32)
    scale = NA_SCALE
    outs = []
    for h in range(NA_HEADS):
        cols = slice(h * NA_DIM, (h + 1) * NA_DIM)
        qh, kh, vh = q_ref[:, cols], k_ref[:, cols], v_ref[:, cols]
        ko_ref[layer, h] = kh.astype(F32)
        vo_ref[layer, h] = vh.astype(F32)
        s = _dot_nt(qh, kh) * scale
        m = jnp.max(s, axis=-1, keepdims=True)
        p = jnp.exp(s - m)
        den = jnp.sum(p, axis=-1, keepdims=True)
        o = jnp.dot(p.astype(BF16), vh, preferred_element_type=F32)
        outs.append(o / den)
    o_ref[...] = jnp.concatenate(outs, axis=-1).astype(BF16)


def _ctx_attn(z, layer, k_prev=None, v_prev=None):
    qb, kb, vb = COL_NA_Q // NA_WIDTH, COL_NA_K // NA_WIDTH, COL_NA_V // NA_WIDTH
    head_shape = jax.ShapeDtypeStruct((BATCH, DEPTH, NA_HEADS, SEQ, NA_DIM), F32)
    head_spec = pl.BlockSpec((None, DEPTH, NA_HEADS, SEQ, NA_DIM), lambda b: (b, 0, 0, 0, 0))
    in_specs = [pl.BlockSpec((SEQ, NA_WIDTH), lambda b: (b, qb)),
                pl.BlockSpec((SEQ, NA_WIDTH), lambda b: (b, kb)),
                pl.BlockSpec((SEQ, NA_WIDTH), lambda b: (b, vb))]
    args = [z, z, z]
    aliases = {}
    if layer:
        in_specs += [head_spec, head_spec]
        args += [k_prev, v_prev]
        aliases = {3: 1, 4: 2}
    return pl.pallas_call(
        functools.partial(_ctx_attn_kernel, layer),
        grid=(BATCH,),
        in_specs=in_specs,
        out_specs=[pl.BlockSpec((SEQ, NA_WIDTH), lambda b: (b, 0)), head_spec, head_spec],
        out_shape=[jax.ShapeDtypeStruct((T_CTX, NA_WIDTH), BF16), head_shape, head_shape],
        input_output_aliases=aliases,
        compiler_params=_cparams(("arbitrary",)),
        name="ctx_attn",
    )(*args)


NA_KINDS = (0, NA_ROWS, GRID_H - NA_ROWS)
N_DR = 2 * NA_KH - 1
N_DC = 2 * NA_KW - 1


def _na_row_offset(r0, i, j):
    ks = min(max(r0 - NA_KH // 2, 0), GRID_H - NA_KROWS)
    r, kr = r0 + i, ks + j
    rs = min(max(r - NA_KH // 2, 0), GRID_H - NA_KH)
    return kr - r + NA_KH - 1 if rs <= kr < rs + NA_KH else None


def _na_bias_kernel(rpb_ref, o_ref):
    lh = pl.program_id(0)
    shape = (GRID_W, 2 * GRID_W)
    qc = lax.broadcasted_iota(jnp.int32, shape, 0)
    lane = lax.broadcasted_iota(jnp.int32, shape, 1)
    kc = lane % GRID_W
    dc = jnp.clip(kc - qc, -(NA_KW - 1), NA_KW - 1) + NA_KW - 1
    cs = jnp.clip(qc - NA_KW // 2, 0, GRID_W - NA_KW)
    col_ok = (kc >= cs) & (kc < cs + NA_KW)
    neg = jnp.full(shape, NEG_INF, F32)
    tiles = []
    for dr in range(N_DR):
        base = (lh * N_DR + dr) * N_DC
        val = jnp.zeros(shape, F32)
        for d in range(N_DC):
            val = jnp.where(dc == d, rpb_ref[base + d], val)
        tiles.append(jnp.where(col_ok, val, neg))
    left = lane < GRID_W
    for kind, r0 in enumerate(NA_KINDS):
        for i in range(NA_ROWS):
            for jp in range(NA_KROWS // 2):
                dl, dr_ = _na_row_offset(r0, i, 2 * jp), _na_row_offset(r0, i, 2 * jp + 1)
                tl = neg if dl is None else tiles[dl]
                tr = neg if dr_ is None else tiles[dr_]
                o_ref[kind, i * GRID_W:(i + 1) * GRID_W, jp * 2 * GRID_W:(jp + 1) * 2 * GRID_W] = (
                    jnp.where(left, tl, tr))


def _na_bias_tables(rpb):
    return pl.pallas_call(
        _na_bias_kernel,
        grid=(DEPTH * NA_HEADS,),
        in_specs=[pl.BlockSpec(memory_space=pltpu.SMEM)],
        out_specs=pl.BlockSpec((None, len(NA_KINDS), NA_Q, NA_KEYS), lambda i: (i, 0, 0, 0)),
        out_shape=jax.ShapeDtypeStruct((DEPTH * NA_HEADS, len(NA_KINDS), NA_Q, NA_KEYS), F32),
        compiler_params=_cparams(("arbitrary",)),
        name="nbr_bias",
    )(rpb.reshape(-1))


NA_G = 4


def _na_kernel(q_ref, k_ref, v_ref, kc_ref, vc_ref, bias_ref, o_ref):
    rb = pl.program_id(2)
    ks = jnp.clip(rb * NA_ROWS - NA_KH // 2, 0, GRID_H - NA_KROWS)
    start = pl.multiple_of(ks * GRID_W, GRID_W)
    q = q_ref[...] * NA_SCALE
    kl = k_ref[pl.ds(start, NA_KEYS), :]
    vl = v_ref[pl.ds(start, NA_KEYS), :]
    ones_loc = jnp.ones((NA_KEYS, NA_DIM), BF16)
    ones_ctx = jnp.ones((PAST_LEN, NA_DIM), BF16)

    def scores(hh):
        cols = slice(hh * NA_DIM, (hh + 1) * NA_DIM)
        qh = q[:, cols]
        return _dot_nt(qh, kl[:, cols]) + bias_ref[hh], _dot_nt(qh, kc_ref[hh].astype(BF16))

    outs = []
    nxt = scores(0)
    for hh in range(NA_G):
        s_loc, s_ctx = nxt
        if hh + 1 < NA_G:
            nxt = scores(hh + 1)
        cols = slice(hh * NA_DIM, (hh + 1) * NA_DIM)
        m = jnp.maximum(jnp.max(s_loc, axis=-1, keepdims=True), jnp.max(s_ctx, axis=-1, keepdims=True))
        p_loc = jnp.exp(s_loc - m).astype(BF16)
        p_ctx = jnp.exp(s_ctx - m).astype(BF16)
        v_ext = jnp.concatenate([vl[:, cols], ones_loc], axis=-1)
        vc_ext = jnp.concatenate([vc_ref[hh].astype(BF16), ones_ctx], axis=-1)
        o = (jnp.dot(p_loc, v_ext, preferred_element_type=F32)
             + jnp.dot(p_ctx, vc_ext, preferred_element_type=F32))
        outs.append(o[:, :NA_DIM] / o[:, NA_DIM:])
    o_ref[...] = jnp.concatenate(outs, axis=-1).astype(BF16)


def _na_attn(z, cache_k, cache_v, bias, layer):
    lat_q0 = T_CTX // NA_Q
    lat_s0 = T_CTX // DEC_SEQ
    width = NA_G * NA_DIM
    qc, kc, vc = COL_NA_Q // width, COL_NA_K // width, COL_NA_V // width
    groups = NA_HEADS // NA_G

    def kind(rb):
        return jnp.where(rb == 0, 0, jnp.where(rb == NA_RB - 1, 2, 1))

    ctx_spec = pl.BlockSpec((None, None, NA_G, PAST_LEN, NA_DIM), lambda b, hg, rb: (b, layer, hg, 0, 0))
    return pl.pallas_call(
        _na_kernel,
        grid=(DEC_BATCH, groups, NA_RB),
        in_specs=[pl.BlockSpec((NA_Q, width), lambda b, hg, rb: (lat_q0 + b * NA_RB + rb, qc + hg)),
                  pl.BlockSpec((DEC_SEQ, width), lambda b, hg, rb: (lat_s0 + b, kc + hg)),
                  pl.BlockSpec((DEC_SEQ, width), lambda b, hg, rb: (lat_s0 + b, vc + hg)),
                  ctx_spec, ctx_spec,
                  pl.BlockSpec((NA_G, None, NA_Q, NA_KEYS),
                               lambda b, hg, rb: (layer * groups + hg, kind(rb), 0, 0))],
        out_specs=pl.BlockSpec((NA_Q, width), lambda b, hg, rb: (b * NA_RB + rb, hg)),
        out_shape=jax.ShapeDtypeStruct((T_LAT, NA_WIDTH), BF16),
        compiler_params=_cparams(("arbitrary", "arbitrary", "arbitrary")),
        name="nbr_attn",
    )(z, z, z, cache_k, cache_v, bias)


def _rope_tables():
    n_freq = RET_DIM // 4
    t = np.arange(DEC_SEQ)
    inv = jnp.asarray(ROPE_BASE, F32) ** (-jnp.arange(n_freq, dtype=F32) / n_freq)
    ang_r = jnp.asarray(t // GRID_W, F32)[:, None] * inv[None, :]
    ang_c = jnp.asarray(t % GRID_W, F32)[:, None] * inv[None, :]
    cos = jnp.concatenate([jnp.cos(ang_r)] * 2 + [jnp.cos(ang_c)] * 2, axis=-1)
    sin = jnp.concatenate([-jnp.sin(ang_r), jnp.sin(ang_r), -jnp.sin(ang_c), jnp.sin(ang_c)], axis=-1)
    return jnp.tile(cos, (1, RET_HEADS)), jnp.tile(sin, (1, RET_HEADS))


def _ret_kernel(seq, latent, *refs):
    if latent:
        (lg_ref, z_ref, gn_ref, cos_ref, sin_ref, s0f_ref, s0b_ref, y_ref,
         q_s, k_s, kv_s, st_s) = refs
    else:
        lg_ref, z_ref, gn_ref, y_ref, sf_ref, sb_ref, q_s, k_s, kv_s, st_s = refs
    nc = seq // RET_CHUNK
    ch, hd = RET_CHUNK, RET_DIM
    half = RET_DIM // 4

    row = lax.broadcasted_iota(jnp.int32, (ch, ch), 0).astype(F32)
    col = lax.broadcasted_iota(jnp.int32, (ch, ch), 1).astype(F32)
    pos = lax.broadcasted_iota(jnp.int32, (ch, hd), 0).astype(F32)
    decay, q_dec, k_dec, c_dec_f, c_dec_b = [], [], [], [], []
    for h in range(RET_HEADS):
        lf, lb = lg_ref[0, h], lg_ref[1, h]
        d_f = jnp.where(row >= col, jnp.exp(jnp.maximum(row - col, 0.0) * lf), 0.0)
        d_b = jnp.where(col >= row, jnp.exp(jnp.maximum(col - row, 0.0) * lb), 0.0)
        decay.append(d_f + d_b)
        q_dec.append(jnp.concatenate([jnp.exp((pos + 1.0) * lf), jnp.exp((ch - pos) * lb)], axis=-1))
        k_dec.append(jnp.concatenate([jnp.exp((ch - 1.0 - pos) * lf), jnp.exp(pos * lb)], axis=-1))
        c_dec_f.append(jnp.exp(jnp.zeros((hd, hd), F32) + ch * lf))
        c_dec_b.append(jnp.exp(jnp.zeros((hd, hd), F32) + ch * lb))

    if latent:
        lane = lax.broadcasted_iota(jnp.int32, (ch, RET_WIDTH), 1)
        first_half = (lane % (2 * half)) < half

    def rope(x, base):
        if not latent:
            return x
        swapped = jnp.where(first_half, pltpu.roll(x, RET_WIDTH - half, 1), pltpu.roll(x, half, 1))
        return x * cos_ref[pl.ds(base, ch), :] + swapped * sin_ref[pl.ds(base, ch), :]

    def pass1(n, carry):
        base = pl.multiple_of(n * ch, ch)
        zc = z_ref[pl.ds(base, ch), :]
        q = rope(zc[:, 0:RET_WIDTH].astype(F32), base)
        k = rope(zc[:, RET_WIDTH:2 * RET_WIDTH].astype(F32) * (RET_DIM ** -0.5), base)
        q_s[pl.ds(base, ch), :] = q.astype(BF16)
        k_s[pl.ds(base, ch), :] = k.astype(BF16)
        v = zc[:, 2 * RET_WIDTH:3 * RET_WIDTH]
        for h in range(RET_HEADS):
            cols = slice(h * hd, (h + 1) * hd)
            kh = k[:, cols]
            k2 = (jnp.concatenate([kh, kh], axis=-1) * k_dec[h]).astype(BF16)
            kv_s[n, h] = lax.dot_general(k2, v[:, cols], (((0,), (0,)), ((), ())),
                                         preferred_element_type=F32)
        return carry

    lax.fori_loop(0, nc, pass1, 0, unroll=2)

    for h in range(RET_HEADS):
        if latent:
            s_f, s_b = s0f_ref[h], s0b_ref[h]
        else:
            s_f = s_b = jnp.zeros((hd, hd), F32)

        def fwd(n, s, h=h):
            st_s[n, h, 0:hd, :] = s
            return c_dec_f[h] * s + kv_s[n, h, 0:hd, :]

        def bwd(i, s, h=h):
            n = nc - 1 - i
            st_s[n, h, hd:2 * hd, :] = s
            return c_dec_b[h] * s + kv_s[n, h, hd:2 * hd, :]

        s_f = lax.fori_loop(0, nc, fwd, s_f)
        s_b = lax.fori_loop(0, nc, bwd, s_b)
        if not latent:
            sf_ref[h] = s_f
            sb_ref[h] = s_b

    def pass3(n, carry):
        base = pl.multiple_of(n * ch, ch)
        zc = z_ref[pl.ds(base, ch), :]
        q = q_s[pl.ds(base, ch), :]
        k = k_s[pl.ds(base, ch), :]
        v = zc[:, 2 * RET_WIDTH:3 * RET_WIDTH]
        gate = zc[:, 3 * RET_WIDTH:4 * RET_WIDTH].astype(F32)
        outs = []
        for h in range(RET_HEADS):
            cols = slice(h * hd, (h + 1) * hd)
            qh = q[:, cols]
            s = _dot_nt(qh, k[:, cols]) * decay[h]
            o = jnp.dot(s.astype(BF16), v[:, cols], preferred_element_type=F32)
            qf = qh.astype(F32)
            q2 = (jnp.concatenate([qf, qf], axis=-1) * q_dec[h]).astype(BF16)
            o = o + jnp.dot(q2, st_s[n, h].astype(BF16), preferred_element_type=F32)
            mu = jnp.mean(o, axis=-1, keepdims=True)
            d = o - mu
            var = jnp.mean(d * d, axis=-1, keepdims=True)
            outs.append(d * lax.rsqrt(var + EPS))
        nrm = jnp.concatenate(outs, axis=-1)
        y_ref[pl.ds(base, ch), :] = (nrm * gn_ref[...] * (gate * _sigmoid(gate))).astype(BF16)
        return carry

    lax.fori_loop(0, nc, pass3, 0, unroll=2)


def _retention(z, lg, gn_g, latent, layer=None, rope=None, s0_f=None, s0_b=None):
    seq = DEC_SEQ if latent else SEQ
    nseq = DEC_BATCH if latent else BATCH
    nc = seq // RET_CHUNK
    row0 = (T_CTX // DEC_SEQ) if latent else 0
    cb = COL_RET // (4 * RET_WIDTH)
    in_specs = [pl.BlockSpec(memory_space=pltpu.SMEM),
                pl.BlockSpec((seq, 4 * RET_WIDTH), lambda s: (row0 + s, cb)),
                _full_spec((1, RET_WIDTH))]
    args = [lg, z, gn_g]
    state_shape = jax.ShapeDtypeStruct((nseq, RET_HEADS, RET_DIM, RET_DIM), F32)
    y_spec = pl.BlockSpec((seq, RET_WIDTH), lambda s: (s, 0))
    y_shape = jax.ShapeDtypeStruct((nseq * seq, RET_WIDTH), BF16)
    if latent:
        st_spec = pl.BlockSpec((None, None, RET_HEADS, RET_DIM, RET_DIM), lambda s: (s, layer, 0, 0, 0))
        in_specs += [_full_spec((seq, RET_WIDTH)), _full_spec((seq, RET_WIDTH)), st_spec, st_spec]
        args += [rope[0], rope[1], s0_f, s0_b]
        out_specs, out_shape = y_spec, y_shape
    else:
        so_spec = pl.BlockSpec((None, RET_HEADS, RET_DIM, RET_DIM), lambda s: (s, 0, 0, 0))
        out_specs, out_shape = [y_spec, so_spec, so_spec], [y_shape, state_shape, state_shape]
    return pl.pallas_call(
        functools.partial(_ret_kernel, seq, latent),
        grid=(nseq,),
        in_specs=in_specs,
        out_specs=out_specs,
        out_shape=out_shape,
        scratch_shapes=[pltpu.VMEM((seq, RET_WIDTH), BF16), pltpu.VMEM((seq, RET_WIDTH), BF16),
                        pltpu.VMEM((nc, RET_HEADS, 2 * RET_DIM, RET_DIM), F32),
                        pltpu.VMEM((nc, RET_HEADS, 2 * RET_DIM, RET_DIM), F32)],
        compiler_params=_cparams(("arbitrary",)),
        name="retention_lat" if latent else "retention_ctx",
    )(*args)


def _route(logits):
    lane = lax.broadcasted_iota(jnp.int32, logits.shape, 1)
    lane_f = lane.astype(F32)
    big = float(ROUTE_COLS)
    neg = -jnp.inf
    is_grp = lane < N_GROUPS
    gl = jnp.where(is_grp, logits, neg)
    gmax = jnp.max(gl, axis=-1, keepdims=True)
    grp = jnp.min(jnp.where(gl == gmax, lane_f, big), axis=-1, keepdims=True)
    p_grp = 1.0 / jnp.sum(jnp.exp(gl - gmax), axis=-1, keepdims=True)
    e_f = lane_f - N_GROUPS
    lo = grp * EXPERTS_PER_GROUP
    in_grp = (e_f >= lo) & (e_f < lo + EXPERTS_PER_GROUP)
    el = jnp.where(in_grp, logits, neg)
    m1 = jnp.max(el, axis=-1, keepdims=True)
    i1 = jnp.min(jnp.where(el == m1, lane_f, big), axis=-1, keepdims=True)
    el2 = jnp.where(lane_f == i1, neg, el)
    m2 = jnp.max(el2, axis=-1, keepdims=True)
    i2 = jnp.min(jnp.where(el2 == m2, lane_f, big), axis=-1, keepdims=True)
    t = jnp.exp(m2 - m1)
    g1 = p_grp / (1.0 + t)
    g2 = p_grp * t / (1.0 + t)
    rows = logits.shape[0]
    oh1, oh2 = lane_f == i1, lane_f == i2
    oh = jnp.where(oh1 | oh2, 1.0, 0.0)
    tri = (lax.broadcasted_iota(jnp.int32, (rows, rows), 0)
           > lax.broadcasted_iota(jnp.int32, (rows, rows), 1))
    rank = jnp.dot(jnp.where(tri, 1.0, 0.0).astype(BF16), oh.astype(BF16), preferred_element_type=F32)
    tiles = jnp.floor((jnp.sum(oh, axis=0, keepdims=True) + (SUBLANES - 1)) * (1.0 / SUBLANES))
    upper = (lax.broadcasted_iota(jnp.int32, (ROUTE_COLS, ROUTE_COLS), 0)
             < lax.broadcasted_iota(jnp.int32, (ROUTE_COLS, ROUTE_COLS), 1))
    start = SUBLANES * jnp.dot(jnp.broadcast_to(tiles, (SUBLANES, ROUTE_COLS)).astype(BF16),
                               jnp.where(upper, 1.0, 0.0).astype(BF16),
                               preferred_element_type=F32)[0:1, :]
    pos = start + rank
    p1 = jnp.sum(jnp.where(oh1, pos, 0.0), axis=-1, keepdims=True)
    p2 = jnp.sum(jnp.where(oh2, pos, 0.0), axis=-1, keepdims=True)
    out = jnp.zeros(logits.shape, F32)
    for k, val in enumerate((i1 - N_GROUPS, i2 - N_GROUPS, g1, g2, p1, p2)):
        out = jnp.where(lane == k, val, out)
    return out, SUBLANES * tiles


def _outproj_kernel(ycc, ycl, ync, ynl, yrc, yrl, x_ref, mod_ref, g_ref, w_ref, wr_ref, br_ref,
                    xo_ref, xs_ref, r_ref, seg_ref):
    is_ctx = pl.program_id(0) < NB_CTX
    yc = jnp.where(is_ctx, ycc[...], ycl[...])
    yn = jnp.where(is_ctx, ync[...], ynl[...])
    yr = jnp.where(is_ctx, yrc[...], yrl[...])
    y = (jnp.dot(yc, w_ref[0:CONV_CH, :], preferred_element_type=F32)
         + jnp.dot(yn, w_ref[CONV_CH:CONV_CH + NA_WIDTH, :], preferred_element_type=F32)
         + jnp.dot(yr, w_ref[CONV_CH + NA_WIDTH:, :], preferred_element_type=F32))
    x = x_ref[...] + mod_ref[2:3, :] * y
    xo_ref[...] = x
    h = _norm_mod(x, g_ref[...], mod_ref[3:4, :], mod_ref[4:5, :])
    h_hi = h.astype(BF16)
    h_lo = (h - h_hi.astype(F32)).astype(BF16)
    hw = jnp.dot(h_hi, wr_ref[...], preferred_element_type=F32)
    logits = (hw[:, :ROUTE_COLS] + hw[:, ROUTE_COLS:]
              + jnp.dot(h_lo, wr_ref[:, :ROUTE_COLS], preferred_element_type=F32) + br_ref[...])
    route, seg = _route(logits)
    r_ref[...] = route
    seg_ref[...] = jnp.broadcast_to(seg, seg_ref.shape)
    sel = _slot_onehot(route, 0) | _slot_onehot(route, 1)
    xs_ref[...] = lax.dot_general(jnp.where(sel, 1.0, 0.0).astype(BF16), h_hi, (((0,), (0,)), ((), ())),
                                  preferred_element_type=F32)


def _outproj(y_conv, y_na, y_ret, x, mod, g, w_bf16, w_route, b_route):
    return pl.pallas_call(
        _outproj_kernel,
        grid=(NB_ALL,),
        in_specs=(_ctx_lat_specs(CONV_CH) + _ctx_lat_specs(NA_WIDTH) + _ctx_lat_specs(RET_WIDTH)
                  + [_tok_spec(D_MODEL), _mod_spec(), _full_spec((1, D_MODEL)),
                     _full_spec((D_MODEL, D_MODEL)), _full_spec((D_MODEL, 2 * ROUTE_COLS)),
                     _full_spec((1, ROUTE_COLS))]),
        out_specs=[_tok_spec(D_MODEL), pl.BlockSpec((MOE_LC, D_MODEL), lambda i: (i, 0)),
                   _tok_spec(ROUTE_COLS), pl.BlockSpec((None, SUBLANES, ROUTE_COLS), lambda i: (i, 0, 0))],
        out_shape=[jax.ShapeDtypeStruct((T_ALL, D_MODEL), F32),
                   jax.ShapeDtypeStruct((NB_ALL * MOE_LC, D_MODEL), F32),
                   jax.ShapeDtypeStruct((T_ALL, ROUTE_COLS), F32),
                   jax.ShapeDtypeStruct((NB_ALL, SUBLANES, ROUTE_COLS), F32)],
        compiler_params=_cparams(("arbitrary",)),
        name="outproj_route",
    )(y_conv[0], y_conv[1], y_na[0], y_na[1], y_ret[0], y_ret[1], x, mod, g, w_bf16, w_route, b_route)


def _dispatch_tables(seg):
    seg_len = seg[:, 0, N_GROUPS:N_GROUPS + N_EXPERTS].astype(jnp.int32)
    experts = jnp.arange(N_EXPERTS, dtype=jnp.int32)
    in_chunk = jnp.cumsum(seg_len, axis=1) - seg_len
    seg_row = in_chunk + MOE_LC * jnp.arange(NB_ALL, dtype=jnp.int32)[:, None]
    seg_off = jnp.cumsum(seg_len, axis=0) - seg_len
    rows_e = jnp.sum(seg_len, axis=0)
    chunk_rows = jnp.sum(seg_len, axis=1)
    nblk = (rows_e + MOE_BLK - 1) // MOE_BLK
    blk_end = jnp.cumsum(nblk)
    blk_start = blk_end - nblk
    blk = jnp.arange(MOE_NBLK, dtype=jnp.int32)
    n_active = blk_end[-1]
    blk_e = jnp.minimum(jnp.sum((blk_end[None, :] <= jnp.minimum(blk, n_active - 1)[:, None]).astype(jnp.int32),
                                axis=-1), N_EXPERTS - 1)
    mine = blk_e[:, None] == experts[None, :]
    blk_lo = (blk - jnp.sum(jnp.where(mine, blk_start[None, :], 0), axis=-1)) * MOE_BLK
    left = jnp.sum(jnp.where(mine, rows_e[None, :], 0), axis=-1) - blk_lo
    blk_nv = jnp.where(blk < n_active, jnp.clip(left, 0, MOE_BLK), 0).astype(jnp.int32)
    off_b = jnp.sum(jnp.where(mine[:, None, :], seg_off[None, :, :], 0), axis=-1)
    end_b = off_b + jnp.sum(jnp.where(mine[:, None, :], seg_len[None, :, :], 0), axis=-1)
    blk_c0 = jnp.sum((end_b <= blk_lo[:, None]).astype(jnp.int32), axis=-1)
    blk_c1 = jnp.sum((off_b < (blk_lo + blk_nv)[:, None]).astype(jnp.int32), axis=-1)
    after = jnp.sum(jnp.where(mine, blk_end[None, :], 0), axis=-1)
    blk_next_e = jnp.where(after < n_active, jnp.take(blk_e, jnp.minimum(after, MOE_NBLK - 1)), -1)
    return (blk_e, blk_next_e.astype(jnp.int32), blk_lo.astype(jnp.int32), blk_nv, blk_c0, blk_c1,
            seg_off.reshape(-1), seg_len.reshape(-1), seg_row.reshape(-1), chunk_rows)


def _moe_kernel(layer, blk_e, blk_next_e, blk_lo, blk_nv, blk_c0, blk_c1, seg_off, seg_len, seg_row,
                chunk_rows, xs_hbm, w1_hbm, w3_hbm, w2_hbm, ys_hbm, xbuf, obuf, zeros,
                w1f, w3f, w2f, w1b, w3b, w2b, gsem, ssem, zsem, wsem):
    i = pl.program_id(0)
    last = pl.num_programs(0) - 1
    slot = i % 2

    def tiles(v):
        return pl.multiple_of(v, SUBLANES)

    def for_segments(blk, fn):
        lo = blk_lo[blk]
        hi = lo + blk_nv[blk]

        def body(c, carry):
            k = c * N_EXPERTS + blk_e[blk]
            s_lo = seg_off[k]
            a = jnp.maximum(s_lo, lo)
            n = jnp.minimum(s_lo + seg_len[k], hi) - a

            @pl.when(n > 0)
            def _():
                fn(tiles(seg_row[k] + a - s_lo), tiles(a - lo), tiles(n))

            return carry

        lax.fori_loop(blk_c0[blk], blk_c1[blk], body, 0)

    def weight_copies(e):
        return [pltpu.make_async_copy(src.at[layer, e], dst, wsem)
                for src, dst in ((w1_hbm, w1f), (w3_hbm, w3f), (w2_hbm, w2f))]

    def start_gathers(blk, s):
        for_segments(blk, lambda src, dst, n: pltpu.make_async_copy(
            xs_hbm.at[pl.ds(src, n)], xbuf.at[s, pl.ds(dst, n)], gsem.at[s]).start())

    def start_scatters(blk, s):
        for_segments(blk, lambda dst, src, n: pltpu.make_async_copy(
            obuf.at[s, pl.ds(src, n)], ys_hbm.at[pl.ds(dst, n)], ssem.at[s]).start())

    def wait_rows(blk, s, sem):
        n = tiles(blk_nv[blk])

        @pl.when(n > 0)
        def _():
            pltpu.make_async_copy(xs_hbm.at[pl.ds(0, n)], xbuf.at[s, pl.ds(0, n)], sem.at[s]).wait()

    @pl.when(i == 0)
    def _():
        xbuf[...] = jnp.zeros_like(xbuf)
        zeros[...] = jnp.zeros_like(zeros)

        def tail(c):
            n = tiles(MOE_LC - chunk_rows[c])
            return n, pltpu.make_async_copy(zeros.at[pl.ds(0, n)],
                                            ys_hbm.at[pl.ds(tiles(c * MOE_LC + chunk_rows[c]), n)], zsem)

        def fill(c, carry):
            n, copy = tail(c)
            pl.when(n > 0)(copy.start)
            return carry

        def drain(c, carry):
            n, copy = tail(c)
            pl.when(n > 0)(copy.wait)
            return carry

        lax.fori_loop(0, NB_ALL, fill, 0)
        lax.fori_loop(0, NB_ALL, drain, 0)
        start_gathers(0, 0)
        for copy in weight_copies(blk_e[0]):
            copy.start()

    @pl.when(i < last)
    def _():
        start_gathers(i + 1, 1 - slot)

    @pl.when(i >= 2)
    def _():
        wait_rows(i - 2, slot, ssem)

    @pl.when(blk_nv[i] > 0)
    def _():
        @pl.when((i == 0) | (blk_e[i] != blk_e[jnp.maximum(i - 1, 0)]))
        def _():
            for copy in weight_copies(blk_e[i]):
                copy.wait()
            w1b[...] = w1f[...].astype(BF16)
            w3b[...] = w3f[...].astype(BF16)
            w2b[...] = w2f[...].astype(BF16)

            @pl.when(blk_next_e[i] >= 0)
            def _():
                for copy in weight_copies(blk_next_e[i]):
                    copy.start()

        wait_rows(i, slot, gsem)
        xb = xbuf[slot].astype(BF16)
        a = jnp.dot(xb, w1b[...], preferred_element_type=F32)
        b = jnp.dot(xb, w3b[...], preferred_element_type=F32)
        mid = (a * _sigmoid(a) * b).astype(BF16)
        obuf[slot] = jnp.dot(mid, w2b[...], preferred_element_type=F32)
        start_scatters(i, slot)

    @pl.when(i == last)
    def _():
        wait_rows(i - 1, 1 - slot, ssem)
        wait_rows(i, slot, ssem)


def _moe(xs, w1, w3, w2, layer, blk_e, blk_next_e, blk_lo, blk_nv, blk_c0, blk_c1, seg_off, seg_len,
         seg_row, chunk_rows):
    any_spec = pl.BlockSpec(memory_space=pl.ANY)
    grid_spec = pltpu.PrefetchScalarGridSpec(
        num_scalar_prefetch=10,
        grid=(MOE_NBLK,),
        in_specs=[any_spec, any_spec, any_spec, any_spec],
        out_specs=any_spec,
        scratch_shapes=[pltpu.VMEM((2, MOE_BLK, D_MODEL), F32), pltpu.VMEM((2, MOE_BLK, D_MODEL), F32),
                        pltpu.VMEM((MOE_LC - 2 * TM, D_MODEL), F32),
                        pltpu.VMEM((D_MODEL, D_EXPERT), F32), pltpu.VMEM((D_MODEL, D_EXPERT), F32),
                        pltpu.VMEM((D_EXPERT, D_MODEL), F32),
                        pltpu.VMEM((D_MODEL, D_EXPERT), BF16), pltpu.VMEM((D_MODEL, D_EXPERT), BF16),
                        pltpu.VMEM((D_EXPERT, D_MODEL), BF16),
                        pltpu.SemaphoreType.DMA((2,)), pltpu.SemaphoreType.DMA((2,)),
                        pltpu.SemaphoreType.DMA, pltpu.SemaphoreType.DMA])
    return pl.pallas_call(
        functools.partial(_moe_kernel, layer),
        grid_spec=grid_spec,
        out_shape=jax.ShapeDtypeStruct((NB_ALL * MOE_LC, D_MODEL), F32),
        compiler_params=_cparams(("arbitrary",)),
        name="moe_experts",
    )(blk_e, blk_next_e, blk_lo, blk_nv, blk_c0, blk_c1, seg_off, seg_len, seg_row, chunk_rows,
      xs, w1, w3, w2)


def _final_kernel(x_ref, ys_ref, r_ref, mod_ref, g_ref, o_ref):
    x = _moe_residual(x_ref, ys_ref, r_ref, mod_ref)
    ms = jnp.mean(x * x, axis=-1, keepdims=True)
    o_ref[...] = x * lax.rsqrt(ms + EPS) * g_ref[...]


def _final(x, ys, route, mod, g, block0, nblocks):
    return pl.pallas_call(
        _final_kernel,
        grid=(nblocks,),
        in_specs=[pl.BlockSpec((TM, D_MODEL), lambda i: (block0 + i, 0)),
                  pl.BlockSpec((MOE_LC, D_MODEL), lambda i: (block0 + i, 0)),
                  pl.BlockSpec((TM, ROUTE_COLS), lambda i: (block0 + i, 0)),
                  pl.BlockSpec((None, 6, D_MODEL), lambda i: (_cond_row(block0 + i), 0, 0)),
                  _full_spec((1, D_MODEL))],
        out_specs=_tok_spec(D_MODEL),
        out_shape=jax.ShapeDtypeStruct((nblocks * TM, D_MODEL), F32),
        compiler_params=_cparams(("arbitrary",)),
        name="final_norm",
    )(x, ys, route, mod, g)


def kernel(x_prompt, x_sample, c, cache_k, cache_v, state_ret_f, state_ret_b, c_ctx, w_ada, b_ada, norm1_g, norm2_g, w_in, w_out, conv_w, conv_b, conv_ln_g, conv_ln_b, na_rpb, ret_lg_f, ret_lg_b, ret_gn_g, w_route_g, b_route_g, w_route_e, b_route_e, w1, w3, w2, final_g):
    cv = jnp.zeros((COND_ROWS, D_MODEL), F32).at[0].set(c_ctx).at[1:N_COND].set(c)
    mods = _ada(cv, w_ada, b_ada).reshape(DEPTH, COND_ROWS, 6, D_MODEL)
    w_in_b = w_in.astype(BF16)
    w_out_b = w_out.astype(BF16)
    pad = ROUTE_COLS - N_GROUPS - N_EXPERTS
    w_route = jnp.pad(jnp.concatenate([w_route_g, w_route_e], axis=-1), ((0, 0), (0, 0), (0, pad)))
    b_route = jnp.pad(jnp.concatenate([b_route_g, b_route_e], axis=-1), ((0, 0), (0, pad)))
    w_route_hi = w_route.astype(BF16)
    w_route_lo = (w_route - w_route_hi.astype(F32)).astype(BF16)
    w_route = jnp.concatenate([w_route_hi, w_route_lo], axis=-1)
    na_bias = _na_bias_tables(na_rpb)
    rope = _rope_tables()
    lg = jnp.stack([ret_lg_f, ret_lg_b], axis=1)

    x_ctx = x_prompt.reshape(T_CTX, D_MODEL)
    x_lat = x_sample.reshape(T_LAT, D_MODEL)
    x = y = route = new_k = new_v = None
    sf_list, sb_list = [], []
    for l in range(DEPTH):
        g1 = norm1_g[l].reshape(1, D_MODEL)
        if l == 0:
            z, x = _inproj_first(x_ctx, x_lat, mods[l], g1, w_in_b[l])
        else:
            z, x = _inproj_next(x, y, route, mods[l - 1], mods[l], g1, w_in_b[l])
        conv_args = (conv_w[l], conv_b[l].reshape(1, -1), conv_ln_g[l].reshape(1, -1),
                     conv_ln_b[l].reshape(1, -1))
        yc_c = _conv(z, 0, BATCH, SEQ, *conv_args)
        yc_l = _conv(z, T_CTX // DEC_SEQ, DEC_BATCH, DEC_SEQ, *conv_args)
        yn_c, new_k, new_v = _ctx_attn(z, l, new_k, new_v)
        yn_l = _na_attn(z, cache_k, cache_v, na_bias, l)
        gn = ret_gn_g[l].reshape(1, RET_WIDTH)
        yr_c, sf_l, sb_l = _retention(z, lg[l], gn, latent=False)
        yr_l = _retention(z, lg[l], gn, latent=True, layer=l, rope=rope,
                          s0_f=state_ret_f, s0_b=state_ret_b)
        x, xs, route, seg = _outproj((yc_c, yc_l), (yn_c, yn_l), (yr_c, yr_l), x, mods[l],
                                     norm2_g[l].reshape(1, D_MODEL), w_out_b[l], w_route[l],
                                     b_route[l].reshape(1, ROUTE_COLS))
        y = _moe(xs, w1, w3, w2, l, *_dispatch_tables(seg))
        sf_list.append(sf_l)
        sb_list.append(sb_l)
    fg = final_g.reshape(1, D_MODEL)
    y_prompt = _final(x, y, route, mods[DEPTH - 1], fg, 0, NB_CTX).reshape(BATCH, SEQ, D_MODEL)
    y_sample = _final(x, y, route, mods[DEPTH - 1], fg, NB_CTX, NB_LAT).reshape(DEC_BATCH, DEC_SEQ, D_MODEL)
    return (y_prompt, y_sample, new_k, new_v, jnp.stack(sf_list, axis=1), jnp.stack(sb_list, axis=1))
```

```python
import functools

import numpy as np
import jax
import jax.numpy as jnp
from jax import lax
from jax.experimental import pallas as pl
from jax.experimental.pallas import tpu as pltpu

D_MODEL = 1024
BATCH = 32
SEQ = 256
DEPTH = 2
DEC_BATCH = 4
DEC_SEQ = 4096
PAST_LEN = 512
GRID_W = 64
GRID_H = DEC_SEQ // GRID_W
CONV_CH = 256
CONV_K = 31
NA_HEADS = 8
NA_DIM = 64
NA_WIDTH = NA_HEADS * NA_DIM
NA_KH = 8
NA_KW = 16
RET_HEADS = 4
RET_DIM = 64
RET_WIDTH = RET_HEADS * RET_DIM
RET_CHUNK = 128
ROPE_BASE = 10000.0
N_GROUPS = 4
EXPERTS_PER_GROUP = 8
N_EXPERTS = N_GROUPS * EXPERTS_PER_GROUP
D_EXPERT = 512
IN_COLS = 2 * CONV_CH + 3 * NA_WIDTH + 4 * RET_WIDTH
EPS = 1e-6
NEG_INF = -1e30

F32 = jnp.float32
BF16 = jnp.bfloat16
HIGHEST = lax.Precision.HIGHEST

T_CTX = BATCH * SEQ
T_LAT = DEC_BATCH * DEC_SEQ
T_ALL = T_CTX + T_LAT
N_COND = 1 + DEC_BATCH
COND_ROWS = 8

TM = 512
NB_CTX = T_CTX // TM
NB_LAT = T_LAT // TM
NB_ALL = NB_CTX + NB_LAT
LAT_BLOCKS_PER_REQ = DEC_SEQ // TM

LANES = 128
SUBLANES = 8
ROUTE_COLS = LANES

COL_CONV = 0
COL_NA_Q = 2 * CONV_CH
COL_NA_K = COL_NA_Q + NA_WIDTH
COL_NA_V = COL_NA_K + NA_WIDTH
COL_RET = COL_NA_V + NA_WIDTH

NA_ROWS = 8
NA_Q = NA_ROWS * GRID_W
NA_KROWS = 2 * NA_ROWS
NA_KEYS = NA_KROWS * GRID_W
NA_RB = GRID_H // NA_ROWS

MOE_BLK = 512
MOE_LC = -(-(2 * TM + N_EXPERTS * (SUBLANES - 1)) // LANES) * LANES
MOE_NBLK = -(-(NB_ALL * MOE_LC) // MOE_BLK) + N_EXPERTS
N_SEG = NB_ALL * N_EXPERTS

VMEM_LIMIT = 56 * 1024 * 1024


def _cparams(sem):
    return pltpu.CompilerParams(dimension_semantics=sem, vmem_limit_bytes=VMEM_LIMIT)


def _sigmoid(x):
    return 1.0 / (1.0 + jnp.exp(-x))


def _cond_row(i):
    return jnp.where(i < NB_CTX, 0, 1 + (i - NB_CTX) // LAT_BLOCKS_PER_REQ)


ADA_TN = 1536


def _ada_kernel(cv_ref, w_ref, b_ref, o_ref):
    cv = cv_ref[...]
    s = cv * _sigmoid(cv)
    o_ref[...] = jnp.dot(s, w_ref[...], precision=HIGHEST, preferred_element_type=F32) + b_ref[...]


def _ada(cv, w_ada, b_ada):
    n = 6 * D_MODEL
    return pl.pallas_call(
        _ada_kernel,
        grid=(DEPTH, n // ADA_TN),
        in_specs=[
            pl.BlockSpec((COND_ROWS, D_MODEL), lambda l, j: (0, 0)),
            pl.BlockSpec((None, D_MODEL, ADA_TN), lambda l, j: (l, 0, j)),
            pl.BlockSpec((None, 1, ADA_TN), lambda l, j: (l, 0, j)),
        ],
        out_specs=pl.BlockSpec((None, COND_ROWS, ADA_TN), lambda l, j: (l, 0, j)),
        out_shape=jax.ShapeDtypeStruct((DEPTH, COND_ROWS, n), F32),
        compiler_params=_cparams(("arbitrary", "arbitrary")),
        name="ada_mod",
    )(cv, w_ada, b_ada.reshape(DEPTH, 1, n))


IN_TN = 768


def _norm_mod(x, g, shift, scale):
    ms = jnp.mean(x * x, axis=-1, keepdims=True)
    return (x * lax.rsqrt(ms + EPS) * g) * (1.0 + scale) + shift


def _inproj_body(x, mod_ref, g_ref, w_ref, z_ref):
    h = _norm_mod(x, g_ref[...], mod_ref[0:1, :], mod_ref[1:2, :]).astype(BF16)
    for c in range(IN_COLS // IN_TN):
        cols = slice(c * IN_TN, (c + 1) * IN_TN)
        z_ref[:, cols] = jnp.dot(h, w_ref[:, cols], preferred_element_type=F32).astype(BF16)


def _inproj_first_kernel(xc_ref, xl_ref, mod_ref, g_ref, w_ref, z_ref, xo_ref):
    i = pl.program_id(0)
    x = jnp.where(i < NB_CTX, xc_ref[...], xl_ref[...])
    xo_ref[...] = x
    _inproj_body(x, mod_ref, g_ref, w_ref, z_ref)


def _slot_onehot(route, slot):
    pos = route[:, 4 + slot:5 + slot].astype(jnp.int32)
    return lax.broadcasted_iota(jnp.int32, (route.shape[0], MOE_LC), 1) == pos


def _moe_residual(x_ref, ys_ref, r_ref, mod_ref):
    r = r_ref[...]
    sel = jnp.where(_slot_onehot(r, 0), r[:, 2:3], jnp.where(_slot_onehot(r, 1), r[:, 3:4], 0.0))
    y = jnp.dot(sel.astype(BF16), ys_ref[...].astype(BF16), preferred_element_type=F32)
    return x_ref[...] + mod_ref[5:6, :] * y


def _inproj_next_kernel(x_ref, ys_ref, r_ref, modp_ref, mod_ref, g_ref, w_ref, z_ref, xo_ref):
    x = _moe_residual(x_ref, ys_ref, r_ref, modp_ref)
    xo_ref[...] = x
    _inproj_body(x, mod_ref, g_ref, w_ref, z_ref)


def _tok_spec(cols):
    return pl.BlockSpec((TM, cols), lambda i: (i, 0))


def _mod_spec():
    return pl.BlockSpec((None, 6, D_MODEL), lambda i: (_cond_row(i), 0, 0))


def _full_spec(shape):
    return pl.BlockSpec(shape, lambda i: (0,) * len(shape))


def _ctx_lat_specs(cols):
    return [pl.BlockSpec((TM, cols), lambda i: (jnp.minimum(i, NB_CTX - 1), 0)),
            pl.BlockSpec((TM, cols), lambda i: (jnp.maximum(i - NB_CTX, 0), 0))]


def _inproj_first(x_ctx, x_lat, mod, g, w_bf16):
    return pl.pallas_call(
        _inproj_first_kernel,
        grid=(NB_ALL,),
        in_specs=_ctx_lat_specs(D_MODEL) + [_mod_spec(), _full_spec((1, D_MODEL)),
                                            _full_spec((D_MODEL, IN_COLS))],
        out_specs=[_tok_spec(IN_COLS), _tok_spec(D_MODEL)],
        out_shape=[jax.ShapeDtypeStruct((T_ALL, IN_COLS), BF16),
                   jax.ShapeDtypeStruct((T_ALL, D_MODEL), F32)],
        compiler_params=_cparams(("arbitrary",)),
        name="inproj_first",
    )(x_ctx, x_lat, mod, g, w_bf16)


def _inproj_next(x, ys, route, mod_prev, mod, g, w_bf16):
    return pl.pallas_call(
        _inproj_next_kernel,
        grid=(NB_ALL,),
        in_specs=[_tok_spec(D_MODEL),
                  pl.BlockSpec((MOE_LC, D_MODEL), lambda i: (i, 0)),
                  _tok_spec(ROUTE_COLS),
                  _mod_spec(), _mod_spec(), _full_spec((1, D_MODEL)),
                  _full_spec((D_MODEL, IN_COLS))],
        out_specs=[_tok_spec(IN_COLS), _tok_spec(D_MODEL)],
        out_shape=[jax.ShapeDtypeStruct((T_ALL, IN_COLS), BF16),
                   jax.ShapeDtypeStruct((T_ALL, D_MODEL), F32)],
        compiler_params=_cparams(("arbitrary",)),
        name="inproj_next",
    )(x, ys, route, mod_prev, mod, g, w_bf16)


CONV_PAD = 16
CONV_CHUNK = 64


CONV_SPAN = CONV_CHUNK + 2 * CONV_PAD - SUBLANES


def _conv_kernel(seq, z_ref, w_ref, b_ref, g_ref, be_ref, o_ref, upad_ref, shift_ref):
    zeros = jnp.zeros((CONV_PAD, CONV_CH), F32)
    upad_ref[0:CONV_PAD, :] = zeros
    upad_ref[seq + CONV_PAD:seq + 2 * CONV_PAD, :] = zeros

    def glu(ci, carry):
        base = pl.multiple_of(ci * 256, 256)
        zc = z_ref[pl.ds(base, 256), :].astype(F32)
        upad_ref[pl.ds(base + CONV_PAD, 256), :] = zc[:, :CONV_CH] * _sigmoid(zc[:, CONV_CH:])
        return carry

    lax.fori_loop(0, seq // 256, glu, 0)

    shift = CONV_PAD - CONV_K // 2

    def chunk(ci, carry):
        base = pl.multiple_of(ci * CONV_CHUNK, CONV_CHUNK)
        win = upad_ref[pl.ds(base, CONV_CHUNK + 2 * CONV_PAD), :]
        acc = jnp.zeros((CONV_CHUNK, CONV_CH), F32)
        for sub in range(SUBLANES):
            shift_ref[sub] = win[sub:sub + CONV_SPAN, :]
            for k in range(CONV_K):
                if (k + shift) % SUBLANES == sub:
                    lo = k + shift - sub
                    acc = acc + w_ref[k:k + 1, :] * shift_ref[sub, lo:lo + CONV_CHUNK, :]
        acc = acc + b_ref[...]
        mu = jnp.mean(acc, axis=-1, keepdims=True)
        d = acc - mu
        var = jnp.mean(d * d, axis=-1, keepdims=True)
        n = d * lax.rsqrt(var + EPS) * g_ref[...] + be_ref[...]
        o_ref[pl.ds(base, CONV_CHUNK), :] = (n * _sigmoid(n)).astype(BF16)
        return carry

    lax.fori_loop(0, seq // CONV_CHUNK, chunk, 0)


def _conv(z, row_block0, nseq, seq, w, b, g, be):
    return pl.pallas_call(
        functools.partial(_conv_kernel, seq),
        grid=(nseq,),
        in_specs=[pl.BlockSpec((seq, 2 * CONV_CH), lambda s: (row_block0 + s, 0)),
                  _full_spec((CONV_K, CONV_CH)), _full_spec((1, CONV_CH)),
                  _full_spec((1, CONV_CH)), _full_spec((1, CONV_CH))],
        out_specs=pl.BlockSpec((seq, CONV_CH), lambda s: (s, 0)),
        out_shape=jax.ShapeDtypeStruct((nseq * seq, CONV_CH), BF16),
        scratch_shapes=[pltpu.VMEM((seq + 2 * CONV_PAD, CONV_CH), F32),
                        pltpu.VMEM((SUBLANES, CONV_SPAN, CONV_CH), F32)],
        compiler_params=_cparams(("arbitrary",)),
        name="conv_seq%d" % seq,
    )(z, w, b, g, be)


def _dot_nt(a, b):
    return lax.dot_general(a, b, (((1,), (1,)), ((), ())), preferred_element_type=F32)


NA_SCALE = NA_DIM ** -0.5
assert NA_SCALE == 2.0 ** round(np.log2(NA_SCALE)), "query pre-scaling assumes a power-of-two scale"


def _ctx_attn_kernel(layer, q_ref, k_ref, v_ref, *refs):
    if layer:
        kprev_ref, vprev_ref, o_ref, ko_ref, vo_ref = refs
    else:
        o_ref, ko_ref, vo_ref = refs
    for j in range(DEPTH):
        if j < layer:
            ko_ref[j] = kprev_ref[j]
            vo_ref[j] = vprev_ref[j]
        elif j > layer:
            ko_ref[j] = jnp.zeros(ko_ref.shape[1:], F32)
            vo_ref[j] = jnp.zeros(vo_ref.shape[1:], F32)
    scale = NA_SCALE
    outs = []
    for h in range(NA_HEADS):
        cols = slice(h * NA_DIM, (h + 1) * NA_DIM)
        qh, kh, vh = q_ref[:, cols], k_ref[:, cols], v_ref[:, cols]
        ko_ref[layer, h] = kh.astype(F32)
        vo_ref[layer, h] = vh.astype(F32)
        s = _dot_nt(qh, kh) * scale
        m = jnp.max(s, axis=-1, keepdims=True)
        p = jnp.exp(s - m)
        den = jnp.sum(p, axis=-1, keepdims=True)
        o = jnp.dot(p.astype(BF16), vh, preferred_element_type=F32)
        outs.append(o / den)
    o_ref[...] = jnp.concatenate(outs, axis=-1).astype(BF16)


def _ctx_attn(z, layer, k_prev=None, v_prev=None):
    qb, kb, vb = COL_NA_Q // NA_WIDTH, COL_NA_K // NA_WIDTH, COL_NA_V // NA_WIDTH
    head_shape = jax.ShapeDtypeStruct((BATCH, DEPTH, NA_HEADS, SEQ, NA_DIM), F32)
    head_spec = pl.BlockSpec((None, DEPTH, NA_HEADS, SEQ, NA_DIM), lambda b: (b, 0, 0, 0, 0))
    in_specs = [pl.BlockSpec((SEQ, NA_WIDTH), lambda b: (b, qb)),
                pl.BlockSpec((SEQ, NA_WIDTH), lambda b: (b, kb)),
                pl.BlockSpec((SEQ, NA_WIDTH), lambda b: (b, vb))]
    args = [z, z, z]
    aliases = {}
    if layer:
        in_specs += [head_spec, head_spec]
        args += [k_prev, v_prev]
        aliases = {3: 1, 4: 2}
    return pl.pallas_call(
        functools.partial(_ctx_attn_kernel, layer),
        grid=(BATCH,),
        in_specs=in_specs,
        out_specs=[pl.BlockSpec((SEQ, NA_WIDTH), lambda b: (b, 0)), head_spec, head_spec],
        out_shape=[jax.ShapeDtypeStruct((T_CTX, NA_WIDTH), BF16), head_shape, head_shape],
        input_output_aliases=aliases,
        compiler_params=_cparams(("arbitrary",)),
        name="ctx_attn",
    )(*args)


NA_KINDS = (0, NA_ROWS, GRID_H - NA_ROWS)
N_DR = 2 * NA_KH - 1
N_DC = 2 * NA_KW - 1


def _na_row_offset(r0, i, j):
    ks = min(max(r0 - NA_KH // 2, 0), GRID_H - NA_KROWS)
    r, kr = r0 + i, ks + j
    rs = min(max(r - NA_KH // 2, 0), GRID_H - NA_KH)
    return kr - r + NA_KH - 1 if rs <= kr < rs + NA_KH else None


def _na_bias_kernel(rpb_ref, o_ref):
    lh = pl.program_id(0)
    shape = (GRID_W, 2 * GRID_W)
    qc = lax.broadcasted_iota(jnp.int32, shape, 0)
    lane = lax.broadcasted_iota(jnp.int32, shape, 1)
    kc = lane % GRID_W
    dc = jnp.clip(kc - qc, -(NA_KW - 1), NA_KW - 1) + NA_KW - 1
    cs = jnp.clip(qc - NA_KW // 2, 0, GRID_W - NA_KW)
    col_ok = (kc >= cs) & (kc < cs + NA_KW)
    neg = jnp.full(shape, NEG_INF, F32)
    tiles = []
    for dr in range(N_DR):
        base = (lh * N_DR + dr) * N_DC
        val = jnp.zeros(shape, F32)
        for d in range(N_DC):
            val = jnp.where(dc == d, rpb_ref[base + d], val)
        tiles.append(jnp.where(col_ok, val, neg))
    left = lane < GRID_W
    for kind, r0 in enumerate(NA_KINDS):
        for i in range(NA_ROWS):
            for jp in range(NA_KROWS // 2):
                dl, dr_ = _na_row_offset(r0, i, 2 * jp), _na_row_offset(r0, i, 2 * jp + 1)
                tl = neg if dl is None else tiles[dl]
                tr = neg if dr_ is None else tiles[dr_]
                o_ref[kind, i * GRID_W:(i + 1) * GRID_W, jp * 2 * GRID_W:(jp + 1) * 2 * GRID_W] = (
                    jnp.where(left, tl, tr))


def _na_bias_tables(rpb):
    return pl.pallas_call(
        _na_bias_kernel,
        grid=(DEPTH * NA_HEADS,),
        in_specs=[pl.BlockSpec(memory_space=pltpu.SMEM)],
        out_specs=pl.BlockSpec((None, len(NA_KINDS), NA_Q, NA_KEYS), lambda i: (i, 0, 0, 0)),
        out_shape=jax.ShapeDtypeStruct((DEPTH * NA_HEADS, len(NA_KINDS), NA_Q, NA_KEYS), F32),
        compiler_params=_cparams(("arbitrary",)),
        name="nbr_bias",
    )(rpb.reshape(-1))


NA_G = 4


def _na_kernel(q_ref, k_ref, v_ref, kc_ref, vc_ref, bias_ref, o_ref):
    rb = pl.program_id(2)
    ks = jnp.clip(rb * NA_ROWS - NA_KH // 2, 0, GRID_H - NA_KROWS)
    start = pl.multiple_of(ks * GRID_W, GRID_W)
    q = q_ref[...] * NA_SCALE
    kl = k_ref[pl.ds(start, NA_KEYS), :]
    vl = v_ref[pl.ds(start, NA_KEYS), :]
    ones_loc = jnp.ones((NA_KEYS, NA_DIM), BF16)
    ones_ctx = jnp.ones((PAST_LEN, NA_DIM), BF16)

    def scores(hh):
        cols = slice(hh * NA_DIM, (hh + 1) * NA_DIM)
        qh = q[:, cols]
        return _dot_nt(qh, kl[:, cols]) + bias_ref[hh], _dot_nt(qh, kc_ref[hh].astype(BF16))

    outs = []
    nxt = scores(0)
    for hh in range(NA_G):
        s_loc, s_ctx = nxt
        if hh + 1 < NA_G:
            nxt = scores(hh + 1)
        cols = slice(hh * NA_DIM, (hh + 1) * NA_DIM)
        m = jnp.maximum(jnp.max(s_loc, axis=-1, keepdims=True), jnp.max(s_ctx, axis=-1, keepdims=True))
        p_loc = jnp.exp(s_loc - m).astype(BF16)
        p_ctx = jnp.exp(s_ctx - m).astype(BF16)
        v_ext = jnp.concatenate([vl[:, cols], ones_loc], axis=-1)
        vc_ext = jnp.concatenate([vc_ref[hh].astype(BF16), ones_ctx], axis=-1)
        o = (jnp.dot(p_loc, v_ext, preferred_element_type=F32)
             + jnp.dot(p_ctx, vc_ext, preferred_element_type=F32))
        outs.append(o[:, :NA_DIM] / o[:, NA_DIM:])
    o_ref[...] = jnp.concatenate(outs, axis=-1).astype(BF16)


def _na_attn(z, cache_k, cache_v, bias, layer):
    lat_q0 = T_CTX // NA_Q
    lat_s0 = T_CTX // DEC_SEQ
    width = NA_G * NA_DIM
    qc, kc, vc = COL_NA_Q // width, COL_NA_K // width, COL_NA_V // width
    groups = NA_HEADS // NA_G

    def kind(rb):
        return jnp.where(rb == 0, 0, jnp.where(rb == NA_RB - 1, 2, 1))

    ctx_spec = pl.BlockSpec((None, None, NA_G, PAST_LEN, NA_DIM), lambda b, hg, rb: (b, layer, hg, 0, 0))
    return pl.pallas_call(
        _na_kernel,
        grid=(DEC_BATCH, groups, NA_RB),
        in_specs=[pl.BlockSpec((NA_Q, width), lambda b, hg, rb: (lat_q0 + b * NA_RB + rb, qc + hg)),
                  pl.BlockSpec((DEC_SEQ, width), lambda b, hg, rb: (lat_s0 + b, kc + hg)),
                  pl.BlockSpec((DEC_SEQ, width), lambda b, hg, rb: (lat_s0 + b, vc + hg)),
                  ctx_spec, ctx_spec,
                  pl.BlockSpec((NA_G, None, NA_Q, NA_KEYS),
                               lambda b, hg, rb: (layer * groups + hg, kind(rb), 0, 0))],
        out_specs=pl.BlockSpec((NA_Q, width), lambda b, hg, rb: (b * NA_RB + rb, hg)),
        out_shape=jax.ShapeDtypeStruct((T_LAT, NA_WIDTH), BF16),
        compiler_params=_cparams(("arbitrary", "arbitrary", "arbitrary")),
        name="nbr_attn",
    )(z, z, z, cache_k, cache_v, bias)


RET_PAIR = 2 * RET_DIM
RET_NPAIR = RET_HEADS // 2
assert RET_PAIR == LANES and RET_CHUNK == LANES
RET_UNROLL = 8


def _rope_tables():
    n_freq = RET_DIM // 4
    t = np.arange(DEC_SEQ)
    inv = jnp.asarray(ROPE_BASE, F32) ** (-jnp.arange(n_freq, dtype=F32) / n_freq)
    ang_r = jnp.asarray(t // GRID_W, F32)[:, None] * inv[None, :]
    ang_c = jnp.asarray(t % GRID_W, F32)[:, None] * inv[None, :]
    cos = jnp.concatenate([jnp.cos(ang_r)] * 2 + [jnp.cos(ang_c)] * 2, axis=-1)
    sin = jnp.concatenate([-jnp.sin(ang_r), jnp.sin(ang_r), -jnp.sin(ang_c), jnp.sin(ang_c)], axis=-1)
    lane = np.arange(RET_WIDTH)
    src = np.where(lane % (2 * n_freq) < n_freq, lane + n_freq, lane - n_freq)
    swap = np.zeros((RET_WIDTH, RET_WIDTH), np.float32)
    swap[src, lane] = 1.0
    return jnp.tile(cos, (1, RET_HEADS)), jnp.tile(sin, (1, RET_HEADS)), jnp.asarray(swap, BF16)


def _ret_kernel(seq, latent, *refs):
    if latent:
        (lg_ref, z_ref, gn_ref, cos_ref, sin_ref, swap_ref, s0f_ref, s0b_ref, y_ref,
         q_s, k_s, kv_s, st_s) = refs
    else:
        lg_ref, z_ref, gn_ref, y_ref, sf_ref, sb_ref, q_s, k_s, kv_s, st_s = refs
    nc = seq // RET_CHUNK
    ch, hd, pw = RET_CHUNK, RET_DIM, RET_PAIR

    row = lax.broadcasted_iota(jnp.int32, (ch, ch), 0).astype(F32)
    col = lax.broadcasted_iota(jnp.int32, (ch, ch), 1).astype(F32)
    pos = lax.broadcasted_iota(jnp.int32, (ch, pw), 0).astype(F32)
    left = lax.broadcasted_iota(jnp.int32, (ch, pw), 1) < hd
    top = lax.broadcasted_iota(jnp.int32, (pw, pw), 0) < hd
    same_head = top == (lax.broadcasted_iota(jnp.int32, (pw, pw), 1) < hd)
    same_head2 = jnp.concatenate([same_head, same_head], axis=0)

    def per_head(mask, fn, p):
        return jnp.where(mask, fn(2 * p), fn(2 * p + 1))

    decay = []
    for h in range(RET_HEADS):
        lf, lb = lg_ref[0, h], lg_ref[1, h]
        d_f = jnp.where(row >= col, jnp.exp(jnp.maximum(row - col, 0.0) * lf), 0.0)
        d_b = jnp.where(col >= row, jnp.exp(jnp.maximum(col - row, 0.0) * lb), 0.0)
        decay.append(d_f + d_b)
    q_dec, k_dec, c_dec_f, c_dec_b = [], [], [], []
    for p in range(RET_NPAIR):
        q_dec.append(jnp.concatenate(
            [per_head(left, lambda h: jnp.exp((pos + 1.0) * lg_ref[0, h]), p),
             per_head(left, lambda h: jnp.exp((ch - pos) * lg_ref[1, h]), p)], axis=-1))
        k_dec.append(jnp.concatenate(
            [per_head(left, lambda h: jnp.exp((ch - 1.0 - pos) * lg_ref[0, h]), p),
             per_head(left, lambda h: jnp.exp(pos * lg_ref[1, h]), p)], axis=-1))
        zero = jnp.zeros((pw, pw), F32)
        c_dec_f.append(per_head(top, lambda h: jnp.exp(zero + ch * lg_ref[0, h]), p))
        c_dec_b.append(per_head(top, lambda h: jnp.exp(zero + ch * lg_ref[1, h]), p))

    def rope(x, base):
        xf = x.astype(F32)
        if not latent:
            return xf
        swapped = jnp.dot(x, swap_ref[...], preferred_element_type=F32)
        return xf * cos_ref[pl.ds(base, ch), :] + swapped * sin_ref[pl.ds(base, ch), :]

    def pass1(n, carry):
        base = pl.multiple_of(n * ch, ch)
        zc = z_ref[pl.ds(base, ch), :]
        q = rope(zc[:, 0:RET_WIDTH], base)
        k = rope(zc[:, RET_WIDTH:2 * RET_WIDTH], base) * (RET_DIM ** -0.5)
        q_s[pl.ds(base, ch), :] = q.astype(BF16)
        k_s[pl.ds(base, ch), :] = k.astype(BF16)
        v = zc[:, 2 * RET_WIDTH:3 * RET_WIDTH]
        for p in range(RET_NPAIR):
            lanes = slice(p * pw, (p + 1) * pw)
            kp = k[:, lanes]
            k2 = (jnp.concatenate([kp, kp], axis=-1) * k_dec[p]).astype(BF16)
            kv = lax.dot_general(k2, v[:, lanes], (((0,), (0,)), ((), ())), preferred_element_type=F32)
            kv_s[n, p] = jnp.where(same_head2, kv, 0.0)
        return carry

    lax.fori_loop(0, nc, pass1, 0, unroll=min(RET_UNROLL, nc))

    def block_diag(a, b):
        z = jnp.zeros((hd, hd), F32)
        return jnp.concatenate([jnp.concatenate([a, z], axis=1), jnp.concatenate([z, b], axis=1)], axis=0)

    for p in range(RET_NPAIR):
        if latent:
            s_f = block_diag(s0f_ref[2 * p], s0f_ref[2 * p + 1])
            s_b = block_diag(s0b_ref[2 * p], s0b_ref[2 * p + 1])
        else:
            s_f = s_b = jnp.zeros((pw, pw), F32)

        def fwd(n, s, p=p):
            st_s[n, p, 0:pw, :] = s.astype(BF16)
            return c_dec_f[p] * s + kv_s[n, p, 0:pw, :]

        def bwd(i, s, p=p):
            n = nc - 1 - i
            st_s[n, p, pw:2 * pw, :] = s.astype(BF16)
            return c_dec_b[p] * s + kv_s[n, p, pw:2 * pw, :]

        s_f = lax.fori_loop(0, nc, fwd, s_f)
        s_b = lax.fori_loop(0, nc, bwd, s_b)
        if not latent:
            for hh in range(2):
                blk = slice(hh * hd, (hh + 1) * hd)
                sf_ref[2 * p + hh] = s_f[blk, blk]
                sb_ref[2 * p + hh] = s_b[blk, blk]

    def pass3(n, carry):
        base = pl.multiple_of(n * ch, ch)
        zc = z_ref[pl.ds(base, ch), :]
        q = q_s[pl.ds(base, ch), :]
        k = k_s[pl.ds(base, ch), :]
        v = zc[:, 2 * RET_WIDTH:3 * RET_WIDTH]
        gate = zc[:, 3 * RET_WIDTH:4 * RET_WIDTH].astype(F32)
        outs = []
        for p in range(RET_NPAIR):
            lanes = slice(p * pw, (p + 1) * pw)
            qp, kp, vp = q[:, lanes], k[:, lanes], v[:, lanes]
            o_h = []
            for hh in range(2):
                qm = jnp.where(left == (hh == 0), qp, jnp.zeros_like(qp))
                s = _dot_nt(qm, kp) * decay[2 * p + hh]
                o_h.append(jnp.dot(s.astype(BF16), vp, preferred_element_type=F32))
            qf = qp.astype(F32)
            q2 = (jnp.concatenate([qf, qf], axis=-1) * q_dec[p]).astype(BF16)
            o = jnp.where(left, o_h[0], o_h[1]) + jnp.dot(q2, st_s[n, p], preferred_element_type=F32)

            def half_mean(t):
                s_l = jnp.sum(jnp.where(left, t, 0.0), axis=-1, keepdims=True)
                s_r = jnp.sum(jnp.where(left, 0.0, t), axis=-1, keepdims=True)
                return jnp.where(left, s_l, s_r) * (1.0 / hd)

            d = o - half_mean(o)
            outs.append(d * lax.rsqrt(half_mean(d * d) + EPS))
        nrm = jnp.concatenate(outs, axis=-1)
        y_ref[pl.ds(base, ch), :] = (nrm * gn_ref[...] * (gate * _sigmoid(gate))).astype(BF16)
        return carry

    lax.fori_loop(0, nc, pass3, 0, unroll=min(RET_UNROLL, nc))


def _retention(z, lg, gn_g, latent, layer=None, rope=None, s0_f=None, s0_b=None):
    seq = DEC_SEQ if latent else SEQ
    nseq = DEC_BATCH if latent else BATCH
    nc = seq // RET_CHUNK
    row0 = (T_CTX // DEC_SEQ) if latent else 0
    cb = COL_RET // (4 * RET_WIDTH)
    in_specs = [pl.BlockSpec(memory_space=pltpu.SMEM),
                pl.BlockSpec((seq, 4 * RET_WIDTH), lambda s: (row0 + s, cb)),
                _full_spec((1, RET_WIDTH))]
    args = [lg, z, gn_g]
    state_shape = jax.ShapeDtypeStruct((nseq, RET_HEADS, RET_DIM, RET_DIM), F32)
    y_spec = pl.BlockSpec((seq, RET_WIDTH), lambda s: (s, 0))
    y_shape = jax.ShapeDtypeStruct((nseq * seq, RET_WIDTH), BF16)
    if latent:
        st_spec = pl.BlockSpec((None, None, RET_HEADS, RET_DIM, RET_DIM), lambda s: (s, layer, 0, 0, 0))

        def const_spec(shape):
            return pl.BlockSpec(shape, lambda s: (0,) * len(shape), pipeline_mode=pl.Buffered(1))

        in_specs += [const_spec((seq, RET_WIDTH)), const_spec((seq, RET_WIDTH)),
                     const_spec((RET_WIDTH, RET_WIDTH)), st_spec, st_spec]
        args += [rope[0], rope[1], rope[2], s0_f, s0_b]
        out_specs, out_shape = y_spec, y_shape
    else:
        so_spec = pl.BlockSpec((None, RET_HEADS, RET_DIM, RET_DIM), lambda s: (s, 0, 0, 0))
        out_specs, out_shape = [y_spec, so_spec, so_spec], [y_shape, state_shape, state_shape]
    return pl.pallas_call(
        functools.partial(_ret_kernel, seq, latent),
        grid=(nseq,),
        in_specs=in_specs,
        out_specs=out_specs,
        out_shape=out_shape,
        scratch_shapes=[pltpu.VMEM((seq, RET_WIDTH), BF16), pltpu.VMEM((seq, RET_WIDTH), BF16),
                        pltpu.VMEM((nc, RET_NPAIR, 2 * RET_PAIR, RET_PAIR), F32),
                        pltpu.VMEM((nc, RET_NPAIR, 2 * RET_PAIR, RET_PAIR), BF16)],
        compiler_params=_cparams(("arbitrary",)),
        name="retention_lat" if latent else "retention_ctx",
    )(*args)


def _route(logits):
    lane = lax.broadcasted_iota(jnp.int32, logits.shape, 1)
    lane_f = lane.astype(F32)
    big = float(ROUTE_COLS)
    neg = -jnp.inf
    is_grp = lane < N_GROUPS
    gl = jnp.where(is_grp, logits, neg)
    gmax = jnp.max(gl, axis=-1, keepdims=True)
    grp = jnp.min(jnp.where(gl == gmax, lane_f, big), axis=-1, keepdims=True)
    p_grp = 1.0 / jnp.sum(jnp.exp(gl - gmax), axis=-1, keepdims=True)
    e_f = lane_f - N_GROUPS
    lo = grp * EXPERTS_PER_GROUP
    in_grp = (e_f >= lo) & (e_f < lo + EXPERTS_PER_GROUP)
    el = jnp.where(in_grp, logits, neg)
    m1 = jnp.max(el, axis=-1, keepdims=True)
    i1 = jnp.min(jnp.where(el == m1, lane_f, big), axis=-1, keepdims=True)
    el2 = jnp.where(lane_f == i1, neg, el)
    m2 = jnp.max(el2, axis=-1, keepdims=True)
    i2 = jnp.min(jnp.where(el2 == m2, lane_f, big), axis=-1, keepdims=True)
    t = jnp.exp(m2 - m1)
    g1 = p_grp / (1.0 + t)
    g2 = p_grp * t / (1.0 + t)
    rows = logits.shape[0]
    oh1, oh2 = lane_f == i1, lane_f == i2
    oh = jnp.where(oh1 | oh2, 1.0, 0.0)
    tri = (lax.broadcasted_iota(jnp.int32, (rows, rows), 0)
           > lax.broadcasted_iota(jnp.int32, (rows, rows), 1))
    rank = jnp.dot(jnp.where(tri, 1.0, 0.0).astype(BF16), oh.astype(BF16), preferred_element_type=F32)
    tiles = jnp.floor((jnp.sum(oh, axis=0, keepdims=True) + (SUBLANES - 1)) * (1.0 / SUBLANES))
    upper = (lax.broadcasted_iota(jnp.int32, (ROUTE_COLS, ROUTE_COLS), 0)
             < lax.broadcasted_iota(jnp.int32, (ROUTE_COLS, ROUTE_COLS), 1))
    start = SUBLANES * jnp.dot(jnp.broadcast_to(tiles, (SUBLANES, ROUTE_COLS)).astype(BF16),
                               jnp.where(upper, 1.0, 0.0).astype(BF16),
                               preferred_element_type=F32)[0:1, :]
    pos = start + rank
    p1 = jnp.sum(jnp.where(oh1, pos, 0.0), axis=-1, keepdims=True)
    p2 = jnp.sum(jnp.where(oh2, pos, 0.0), axis=-1, keepdims=True)
    out = jnp.zeros(logits.shape, F32)
    for k, val in enumerate((i1 - N_GROUPS, i2 - N_GROUPS, g1, g2, p1, p2)):
        out = jnp.where(lane == k, val, out)
    return out, SUBLANES * tiles


def _outproj_kernel(ycc, ycl, ync, ynl, yrc, yrl, x_ref, mod_ref, g_ref, w_ref, wr_ref, br_ref,
                    xo_ref, xs_ref, r_ref, seg_ref):
    is_ctx = pl.program_id(0) < NB_CTX
    yc = jnp.where(is_ctx, ycc[...], ycl[...])
    yn = jnp.where(is_ctx, ync[...], ynl[...])
    yr = jnp.where(is_ctx, yrc[...], yrl[...])
    y = (jnp.dot(yc, w_ref[0:CONV_CH, :], preferred_element_type=F32)
         + jnp.dot(yn, w_ref[CONV_CH:CONV_CH + NA_WIDTH, :], preferred_element_type=F32)
         + jnp.dot(yr, w_ref[CONV_CH + NA_WIDTH:, :], preferred_element_type=F32))
    x = x_ref[...] + mod_ref[2:3, :] * y
    xo_ref[...] = x
    h = _norm_mod(x, g_ref[...], mod_ref[3:4, :], mod_ref[4:5, :])
    h_hi = h.astype(BF16)
    h_lo = (h - h_hi.astype(F32)).astype(BF16)
    hw = jnp.dot(h_hi, wr_ref[...], preferred_element_type=F32)
    logits = (hw[:, :ROUTE_COLS] + hw[:, ROUTE_COLS:]
              + jnp.dot(h_lo, wr_ref[:, :ROUTE_COLS], preferred_element_type=F32) + br_ref[...])
    route, seg = _route(logits)
    r_ref[...] = route
    seg_ref[...] = jnp.broadcast_to(seg, seg_ref.shape)
    sel = _slot_onehot(route, 0) | _slot_onehot(route, 1)
    xs_ref[...] = lax.dot_general(jnp.where(sel, 1.0, 0.0).astype(BF16), h_hi, (((0,), (0,)), ((), ())),
                                  preferred_element_type=F32)


def _outproj(y_conv, y_na, y_ret, x, mod, g, w_bf16, w_route, b_route):
    return pl.pallas_call(
        _outproj_kernel,
        grid=(NB_ALL,),
        in_specs=(_ctx_lat_specs(CONV_CH) + _ctx_lat_specs(NA_WIDTH) + _ctx_lat_specs(RET_WIDTH)
                  + [_tok_spec(D_MODEL), _mod_spec(), _full_spec((1, D_MODEL)),
                     _full_spec((D_MODEL, D_MODEL)), _full_spec((D_MODEL, 2 * ROUTE_COLS)),
                     _full_spec((1, ROUTE_COLS))]),
        out_specs=[_tok_spec(D_MODEL), pl.BlockSpec((MOE_LC, D_MODEL), lambda i: (i, 0)),
                   _tok_spec(ROUTE_COLS), pl.BlockSpec((None, SUBLANES, ROUTE_COLS), lambda i: (i, 0, 0))],
        out_shape=[jax.ShapeDtypeStruct((T_ALL, D_MODEL), F32),
                   jax.ShapeDtypeStruct((NB_ALL * MOE_LC, D_MODEL), F32),
                   jax.ShapeDtypeStruct((T_ALL, ROUTE_COLS), F32),
                   jax.ShapeDtypeStruct((NB_ALL, SUBLANES, ROUTE_COLS), F32)],
        compiler_params=_cparams(("arbitrary",)),
        name="outproj_route",
    )(y_conv[0], y_conv[1], y_na[0], y_na[1], y_ret[0], y_ret[1], x, mod, g, w_bf16, w_route, b_route)


def _dispatch_tables(seg):
    seg_len = seg[:, 0, N_GROUPS:N_GROUPS + N_EXPERTS].astype(jnp.int32)
    experts = jnp.arange(N_EXPERTS, dtype=jnp.int32)
    in_chunk = jnp.cumsum(seg_len, axis=1) - seg_len
    seg_row = in_chunk + MOE_LC * jnp.arange(NB_ALL, dtype=jnp.int32)[:, None]
    seg_off = jnp.cumsum(seg_len, axis=0) - seg_len
    rows_e = jnp.sum(seg_len, axis=0)
    chunk_rows = jnp.sum(seg_len, axis=1)
    nblk = (rows_e + MOE_BLK - 1) // MOE_BLK
    blk_end = jnp.cumsum(nblk)
    blk_start = blk_end - nblk
    blk = jnp.arange(MOE_NBLK, dtype=jnp.int32)
    n_active = blk_end[-1]
    blk_e = jnp.minimum(jnp.sum((blk_end[None, :] <= jnp.minimum(blk, n_active - 1)[:, None]).astype(jnp.int32),
                                axis=-1), N_EXPERTS - 1)
    mine = blk_e[:, None] == experts[None, :]
    blk_lo = (blk - jnp.sum(jnp.where(mine, blk_start[None, :], 0), axis=-1)) * MOE_BLK
    left = jnp.sum(jnp.where(mine, rows_e[None, :], 0), axis=-1) - blk_lo
    blk_nv = jnp.where(blk < n_active, jnp.clip(left, 0, MOE_BLK), 0).astype(jnp.int32)
    off_b = jnp.sum(jnp.where(mine[:, None, :], seg_off[None, :, :], 0), axis=-1)
    end_b = off_b + jnp.sum(jnp.where(mine[:, None, :], seg_len[None, :, :], 0), axis=-1)
    blk_c0 = jnp.sum((end_b <= blk_lo[:, None]).astype(jnp.int32), axis=-1)
    blk_c1 = jnp.sum((off_b < (blk_lo + blk_nv)[:, None]).astype(jnp.int32), axis=-1)
    after = jnp.sum(jnp.where(mine, blk_end[None, :], 0), axis=-1)
    blk_next_e = jnp.where(after < n_active, jnp.take(blk_e, jnp.minimum(after, MOE_NBLK - 1)), -1)
    return (blk_e, blk_next_e.astype(jnp.int32), blk_lo.astype(jnp.int32), blk_nv, blk_c0, blk_c1,
            seg_off.reshape(-1), seg_len.reshape(-1), seg_row.reshape(-1), chunk_rows)


def _moe_kernel(layer, blk_e, blk_next_e, blk_lo, blk_nv, blk_c0, blk_c1, seg_off, seg_len, seg_row,
                chunk_rows, xs_hbm, w1_hbm, w3_hbm, w2_hbm, ys_hbm, xbuf, obuf, zeros,
                w1f, w3f, w2f, w1b, w3b, w2b, gsem, ssem, zsem, wsem):
    i = pl.program_id(0)
    last = pl.num_programs(0) - 1
    slot = i % 2

    def tiles(v):
        return pl.multiple_of(v, SUBLANES)

    def for_segments(blk, fn):
        lo = blk_lo[blk]
        hi = lo + blk_nv[blk]

        def body(c, carry):
            k = c * N_EXPERTS + blk_e[blk]
            s_lo = seg_off[k]
            a = jnp.maximum(s_lo, lo)
            n = jnp.minimum(s_lo + seg_len[k], hi) - a

            @pl.when(n > 0)
            def _():
                fn(tiles(seg_row[k] + a - s_lo), tiles(a - lo), tiles(n))

            return carry

        lax.fori_loop(blk_c0[blk], blk_c1[blk], body, 0)

    def weight_copies(e):
        return [pltpu.make_async_copy(src.at[layer, e], dst, wsem)
                for src, dst in ((w1_hbm, w1f), (w3_hbm, w3f), (w2_hbm, w2f))]

    def start_gathers(blk, s):
        for_segments(blk, lambda src, dst, n: pltpu.make_async_copy(
            xs_hbm.at[pl.ds(src, n)], xbuf.at[s, pl.ds(dst, n)], gsem.at[s]).start())

    def start_scatters(blk, s):
        for_segments(blk, lambda dst, src, n: pltpu.make_async_copy(
            obuf.at[s, pl.ds(src, n)], ys_hbm.at[pl.ds(dst, n)], ssem.at[s]).start())

    def wait_rows(blk, s, sem):
        n = tiles(blk_nv[blk])

        @pl.when(n > 0)
        def _():
            pltpu.make_async_copy(xs_hbm.at[pl.ds(0, n)], xbuf.at[s, pl.ds(0, n)], sem.at[s]).wait()

    @pl.when(i == 0)
    def _():
        xbuf[...] = jnp.zeros_like(xbuf)
        zeros[...] = jnp.zeros_like(zeros)

        def tail(c):
            n = tiles(MOE_LC - chunk_rows[c])
            return n, pltpu.make_async_copy(zeros.at[pl.ds(0, n)],
                                            ys_hbm.at[pl.ds(tiles(c * MOE_LC + chunk_rows[c]), n)], zsem)

        def fill(c, carry):
            n, copy = tail(c)
            pl.when(n > 0)(copy.start)
            return carry

        def drain(c, carry):
            n, copy = tail(c)
            pl.when(n > 0)(copy.wait)
            return carry

        lax.fori_loop(0, NB_ALL, fill, 0)
        lax.fori_loop(0, NB_ALL, drain, 0)
        start_gathers(0, 0)
        for copy in weight_copies(blk_e[0]):
            copy.start()

    @pl.when(i < last)
    def _():
        start_gathers(i + 1, 1 - slot)

    @pl.when(i >= 2)
    def _():
        wait_rows(i - 2, slot, ssem)

    @pl.when(blk_nv[i] > 0)
    def _():
        @pl.when((i == 0) | (blk_e[i] != blk_e[jnp.maximum(i - 1, 0)]))
        def _():
            for copy in weight_copies(blk_e[i]):
                copy.wait()
            w1b[...] = w1f[...].astype(BF16)
            w3b[...] = w3f[...].astype(BF16)
            w2b[...] = w2f[...].astype(BF16)

            @pl.when(blk_next_e[i] >= 0)
            def _():
                for copy in weight_copies(blk_next_e[i]):
                    copy.start()

        wait_rows(i, slot, gsem)
        xb = xbuf[slot].astype(BF16)
        a = jnp.dot(xb, w1b[...], preferred_element_type=F32)
        b = jnp.dot(xb, w3b[...], preferred_element_type=F32)
        mid = (a * _sigmoid(a) * b).astype(BF16)
        obuf[slot] = jnp.dot(mid, w2b[...], preferred_element_type=F32)
        start_scatters(i, slot)

    @pl.when(i == last)
    def _():
        wait_rows(i - 1, 1 - slot, ssem)
        wait_rows(i, slot, ssem)


def _moe(xs, w1, w3, w2, layer, blk_e, blk_next_e, blk_lo, blk_nv, blk_c0, blk_c1, seg_off, seg_len,
         seg_row, chunk_rows):
    any_spec = pl.BlockSpec(memory_space=pl.ANY)
    grid_spec = pltpu.PrefetchScalarGridSpec(
        num_scalar_prefetch=10,
        grid=(MOE_NBLK,),
        in_specs=[any_spec, any_spec, any_spec, any_spec],
        out_specs=any_spec,
        scratch_shapes=[pltpu.VMEM((2, MOE_BLK, D_MODEL), F32), pltpu.VMEM((2, MOE_BLK, D_MODEL), F32),
                        pltpu.VMEM((MOE_LC - 2 * TM, D_MODEL), F32),
                        pltpu.VMEM((D_MODEL, D_EXPERT), F32), pltpu.VMEM((D_MODEL, D_EXPERT), F32),
                        pltpu.VMEM((D_EXPERT, D_MODEL), F32),
                        pltpu.VMEM((D_MODEL, D_EXPERT), BF16), pltpu.VMEM((D_MODEL, D_EXPERT), BF16),
                        pltpu.VMEM((D_EXPERT, D_MODEL), BF16),
                        pltpu.SemaphoreType.DMA((2,)), pltpu.SemaphoreType.DMA((2,)),
                        pltpu.SemaphoreType.DMA, pltpu.SemaphoreType.DMA])
    return pl.pallas_call(
        functools.partial(_moe_kernel, layer),
        grid_spec=grid_spec,
        out_shape=jax.ShapeDtypeStruct((NB_ALL * MOE_LC, D_MODEL), F32),
        compiler_params=_cparams(("arbitrary",)),
        name="moe_experts",
    )(blk_e, blk_next_e, blk_lo, blk_nv, blk_c0, blk_c1, seg_off, seg_len, seg_row, chunk_rows,
      xs, w1, w3, w2)


def _final_kernel(x_ref, ys_ref, r_ref, mod_ref, g_ref, o_ref):
    x = _moe_residual(x_ref, ys_ref, r_ref, mod_ref)
    ms = jnp.mean(x * x, axis=-1, keepdims=True)
    o_ref[...] = x * lax.rsqrt(ms + EPS) * g_ref[...]


def _final(x, ys, route, mod, g, block0, nblocks):
    return pl.pallas_call(
        _final_kernel,
        grid=(nblocks,),
        in_specs=[pl.BlockSpec((TM, D_MODEL), lambda i: (block0 + i, 0)),
                  pl.BlockSpec((MOE_LC, D_MODEL), lambda i: (block0 + i, 0)),
                  pl.BlockSpec((TM, ROUTE_COLS), lambda i: (block0 + i, 0)),
                  pl.BlockSpec((None, 6, D_MODEL), lambda i: (_cond_row(block0 + i), 0, 0)),
                  _full_spec((1, D_MODEL))],
        out_specs=_tok_spec(D_MODEL),
        out_shape=jax.ShapeDtypeStruct((nblocks * TM, D_MODEL), F32),
        compiler_params=_cparams(("arbitrary",)),
        name="final_norm",
    )(x, ys, route, mod, g)


def kernel(x_prompt, x_sample, c, cache_k, cache_v, state_ret_f, state_ret_b, c_ctx, w_ada, b_ada, norm1_g, norm2_g, w_in, w_out, conv_w, conv_b, conv_ln_g, conv_ln_b, na_rpb, ret_lg_f, ret_lg_b, ret_gn_g, w_route_g, b_route_g, w_route_e, b_route_e, w1, w3, w2, final_g):
    cv = jnp.zeros((COND_ROWS, D_MODEL), F32).at[0].set(c_ctx).at[1:N_COND].set(c)
    mods = _ada(cv, w_ada, b_ada).reshape(DEPTH, COND_ROWS, 6, D_MODEL)
    w_in_b = w_in.astype(BF16)
    w_out_b = w_out.astype(BF16)
    pad = ROUTE_COLS - N_GROUPS - N_EXPERTS
    w_route = jnp.pad(jnp.concatenate([w_route_g, w_route_e], axis=-1), ((0, 0), (0, 0), (0, pad)))
    b_route = jnp.pad(jnp.concatenate([b_route_g, b_route_e], axis=-1), ((0, 0), (0, pad)))
    w_route_hi = w_route.astype(BF16)
    w_route_lo = (w_route - w_route_hi.astype(F32)).astype(BF16)
    w_route = jnp.concatenate([w_route_hi, w_route_lo], axis=-1)
    na_bias = _na_bias_tables(na_rpb)
    rope = _rope_tables()
    lg = jnp.stack([ret_lg_f, ret_lg_b], axis=1)

    x_ctx = x_prompt.reshape(T_CTX, D_MODEL)
    x_lat = x_sample.reshape(T_LAT, D_MODEL)
    x = y = route = new_k = new_v = None
    sf_list, sb_list = [], []
    for l in range(DEPTH):
        g1 = norm1_g[l].reshape(1, D_MODEL)
        if l == 0:
            z, x = _inproj_first(x_ctx, x_lat, mods[l], g1, w_in_b[l])
        else:
            z, x = _inproj_next(x, y, route, mods[l - 1], mods[l], g1, w_in_b[l])
        conv_args = (conv_w[l], conv_b[l].reshape(1, -1), conv_ln_g[l].reshape(1, -1),
                     conv_ln_b[l].reshape(1, -1))
        yc_c = _conv(z, 0, BATCH, SEQ, *conv_args)
        yc_l = _conv(z, T_CTX // DEC_SEQ, DEC_BATCH, DEC_SEQ, *conv_args)
        yn_c, new_k, new_v = _ctx_attn(z, l, new_k, new_v)
        yn_l = _na_attn(z, cache_k, cache_v, na_bias, l)
        gn = ret_gn_g[l].reshape(1, RET_WIDTH)
        yr_c, sf_l, sb_l = _retention(z, lg[l], gn, latent=False)
        yr_l = _retention(z, lg[l], gn, latent=True, layer=l, rope=rope,
                          s0_f=state_ret_f, s0_b=state_ret_b)
        x, xs, route, seg = _outproj((yc_c, yc_l), (yn_c, yn_l), (yr_c, yr_l), x, mods[l],
                                     norm2_g[l].reshape(1, D_MODEL), w_out_b[l], w_route[l],
                                     b_route[l].reshape(1, ROUTE_COLS))
        y = _moe(xs, w1, w3, w2, l, *_dispatch_tables(seg))
        sf_list.append(sf_l)
        sb_list.append(sb_l)
    fg = final_g.reshape(1, D_MODEL)
    y_prompt = _final(x, y, route, mods[DEPTH - 1], fg, 0, NB_CTX).reshape(BATCH, SEQ, D_MODEL)
    y_sample = _final(x, y, route, mods[DEPTH - 1], fg, NB_CTX, NB_LAT).reshape(DEC_BATCH, DEC_SEQ, D_MODEL)
    return (y_prompt, y_sample, new_k, new_v, jnp.stack(sf_list, axis=1), jnp.stack(sb_list, axis=1))
```

```python
import functools

import numpy as np
import jax
import jax.numpy as jnp
from jax import lax
from jax.experimental import pallas as pl
from jax.experimental.pallas import tpu as pltpu

D_MODEL = 1024
BATCH = 32
SEQ = 256
DEPTH = 2
DEC_BATCH = 4
DEC_SEQ = 4096
PAST_LEN = 512
GRID_W = 64
GRID_H = DEC_SEQ // GRID_W
CONV_CH = 256
CONV_K = 31
NA_HEADS = 8
NA_DIM = 64
NA_WIDTH = NA_HEADS * NA_DIM
NA_KH = 8
NA_KW = 16
RET_HEADS = 4
RET_DIM = 64
RET_WIDTH = RET_HEADS * RET_DIM
RET_CHUNK = 128
ROPE_BASE = 10000.0
N_GROUPS = 4
EXPERTS_PER_GROUP = 8
N_EXPERTS = N_GROUPS * EXPERTS_PER_GROUP
D_EXPERT = 512
IN_COLS = 2 * CONV_CH + 3 * NA_WIDTH + 4 * RET_WIDTH
EPS = 1e-6
NEG_INF = -1e30

F32 = jnp.float32
BF16 = jnp.bfloat16
HIGHEST = lax.Precision.HIGHEST

T_CTX = BATCH * SEQ
T_LAT = DEC_BATCH * DEC_SEQ
T_ALL = T_CTX + T_LAT
N_COND = 1 + DEC_BATCH
COND_ROWS = 8

TM = 512
NB_CTX = T_CTX // TM
NB_LAT = T_LAT // TM
NB_ALL = NB_CTX + NB_LAT
LAT_BLOCKS_PER_REQ = DEC_SEQ // TM

LANES = 128
SUBLANES = 8
ROUTE_COLS = LANES

COL_CONV = 0
COL_NA_Q = 2 * CONV_CH
COL_NA_K = COL_NA_Q + NA_WIDTH
COL_NA_V = COL_NA_K + NA_WIDTH
COL_RET = COL_NA_V + NA_WIDTH

NA_ROWS = 8
NA_Q = NA_ROWS * GRID_W
NA_KROWS = 2 * NA_ROWS
NA_KEYS = NA_KROWS * GRID_W
NA_RB = GRID_H // NA_ROWS

MOE_BLK = 512
MOE_LC = -(-(2 * TM + N_EXPERTS * (SUBLANES - 1)) // LANES) * LANES
MOE_NBLK = -(-(NB_ALL * MOE_LC) // MOE_BLK) + N_EXPERTS
N_SEG = NB_ALL * N_EXPERTS

VMEM_LIMIT = 56 * 1024 * 1024


def _cparams(sem):
    return pltpu.CompilerParams(dimension_semantics=sem, vmem_limit_bytes=VMEM_LIMIT)


def _sigmoid(x):
    return 1.0 / (1.0 + jnp.exp(-x))


def _cond_row(i):
    return jnp.where(i < NB_CTX, 0, 1 + (i - NB_CTX) // LAT_BLOCKS_PER_REQ)


ADA_TN = 1536


def _ada_kernel(cv_ref, w_ref, b_ref, o_ref):
    cv = cv_ref[...]
    s = cv * _sigmoid(cv)
    o_ref[...] = jnp.dot(s, w_ref[...], precision=HIGHEST, preferred_element_type=F32) + b_ref[...]


def _ada(cv, w_ada, b_ada):
    n = 6 * D_MODEL
    return pl.pallas_call(
        _ada_kernel,
        grid=(DEPTH, n // ADA_TN),
        in_specs=[
            pl.BlockSpec((COND_ROWS, D_MODEL), lambda l, j: (0, 0)),
            pl.BlockSpec((None, D_MODEL, ADA_TN), lambda l, j: (l, 0, j)),
            pl.BlockSpec((None, 1, ADA_TN), lambda l, j: (l, 0, j)),
        ],
        out_specs=pl.BlockSpec((None, COND_ROWS, ADA_TN), lambda l, j: (l, 0, j)),
        out_shape=jax.ShapeDtypeStruct((DEPTH, COND_ROWS, n), F32),
        compiler_params=_cparams(("arbitrary", "arbitrary")),
        name="ada_mod",
    )(cv, w_ada, b_ada.reshape(DEPTH, 1, n))


IN_TN = 768


def _norm_mod(x, g, shift, scale):
    ms = jnp.mean(x * x, axis=-1, keepdims=True)
    return (x * lax.rsqrt(ms + EPS) * g) * (1.0 + scale) + shift


def _inproj_body(x, mod_ref, g_ref, w_ref, z_ref):
    h = _norm_mod(x, g_ref[...], mod_ref[0:1, :], mod_ref[1:2, :]).astype(BF16)
    for c in range(IN_COLS // IN_TN):
        cols = slice(c * IN_TN, (c + 1) * IN_TN)
        z_ref[:, cols] = jnp.dot(h, w_ref[:, cols], preferred_element_type=F32).astype(BF16)


def _inproj_first_kernel(xc_ref, xl_ref, mod_ref, g_ref, w_ref, z_ref, xo_ref):
    i = pl.program_id(0)
    x = jnp.where(i < NB_CTX, xc_ref[...], xl_ref[...])
    xo_ref[...] = x
    _inproj_body(x, mod_ref, g_ref, w_ref, z_ref)


U32 = jnp.uint32
D_HALF = D_MODEL // 2
_HI_MASK = np.uint32(0xFFFF0000)


def _pack_bf16_pairs(x):
    xb = x.astype(BF16).astype(F32)
    lo = lax.bitcast_convert_type(xb[:, :D_HALF], U32) >> 16
    hi = lax.bitcast_convert_type(xb[:, D_HALF:], U32) & _HI_MASK
    return lo | hi


def _unpack_bf16_pairs(w):
    lo = lax.bitcast_convert_type(w << 16, F32).astype(BF16)
    hi = lax.bitcast_convert_type(w & _HI_MASK, F32).astype(BF16)
    return lo, hi


def _slot_onehot(route, slot):
    pos = route[:, 4 + slot:5 + slot].astype(jnp.int32)
    return lax.broadcasted_iota(jnp.int32, (route.shape[0], MOE_LC), 1) == pos


def _moe_residual(x_ref, ys_ref, r_ref, mod_ref):
    r = r_ref[...]
    sel = jnp.where(_slot_onehot(r, 0), r[:, 2:3], jnp.where(_slot_onehot(r, 1), r[:, 3:4], 0.0))
    sel = sel.astype(BF16)
    y = jnp.concatenate([jnp.dot(sel, half, preferred_element_type=F32)
                         for half in _unpack_bf16_pairs(ys_ref[...])], axis=-1)
    return x_ref[...] + mod_ref[5:6, :] * y


def _inproj_next_kernel(x_ref, ys_ref, r_ref, modp_ref, mod_ref, g_ref, w_ref, z_ref, xo_ref):
    x = _moe_residual(x_ref, ys_ref, r_ref, modp_ref)
    xo_ref[...] = x
    _inproj_body(x, mod_ref, g_ref, w_ref, z_ref)


def _tok_spec(cols):
    return pl.BlockSpec((TM, cols), lambda i: (i, 0))


def _mod_spec():
    return pl.BlockSpec((None, 6, D_MODEL), lambda i: (_cond_row(i), 0, 0))


def _full_spec(shape):
    return pl.BlockSpec(shape, lambda i: (0,) * len(shape))


def _ctx_lat_specs(cols):
    return [pl.BlockSpec((TM, cols), lambda i: (jnp.minimum(i, NB_CTX - 1), 0)),
            pl.BlockSpec((TM, cols), lambda i: (jnp.maximum(i - NB_CTX, 0), 0))]


def _inproj_first(x_ctx, x_lat, mod, g, w_bf16):
    return pl.pallas_call(
        _inproj_first_kernel,
        grid=(NB_ALL,),
        in_specs=_ctx_lat_specs(D_MODEL) + [_mod_spec(), _full_spec((1, D_MODEL)),
                                            _full_spec((D_MODEL, IN_COLS))],
        out_specs=[_tok_spec(IN_COLS), _tok_spec(D_MODEL)],
        out_shape=[jax.ShapeDtypeStruct((T_ALL, IN_COLS), BF16),
                   jax.ShapeDtypeStruct((T_ALL, D_MODEL), F32)],
        compiler_params=_cparams(("arbitrary",)),
        name="inproj_first",
    )(x_ctx, x_lat, mod, g, w_bf16)


def _inproj_next(x, ys, route, mod_prev, mod, g, w_bf16):
    return pl.pallas_call(
        _inproj_next_kernel,
        grid=(NB_ALL,),
        in_specs=[_tok_spec(D_MODEL),
                  pl.BlockSpec((MOE_LC, D_HALF), lambda i: (i, 0)),
                  _tok_spec(ROUTE_COLS),
                  _mod_spec(), _mod_spec(), _full_spec((1, D_MODEL)),
                  _full_spec((D_MODEL, IN_COLS))],
        out_specs=[_tok_spec(IN_COLS), _tok_spec(D_MODEL)],
        out_shape=[jax.ShapeDtypeStruct((T_ALL, IN_COLS), BF16),
                   jax.ShapeDtypeStruct((T_ALL, D_MODEL), F32)],
        compiler_params=_cparams(("arbitrary",)),
        name="inproj_next",
    )(x, ys, route, mod_prev, mod, g, w_bf16)


CONV_PAD = 16
CONV_CHUNK = 64


CONV_SPAN = CONV_CHUNK + 2 * CONV_PAD - SUBLANES


CONV_UNROLL = 2


def _conv_kernel(seq, z_ref, w_ref, b_ref, g_ref, be_ref, o_ref, upad_ref, shift_refs):
    zeros = jnp.zeros((CONV_PAD, CONV_CH), F32)
    upad_ref[0:CONV_PAD, :] = zeros
    upad_ref[seq + CONV_PAD:seq + 2 * CONV_PAD, :] = zeros

    def glu(ci, carry):
        base = pl.multiple_of(ci * 256, 256)
        zc = z_ref[pl.ds(base, 256), :].astype(F32)
        upad_ref[pl.ds(base + CONV_PAD, 256), :] = zc[:, :CONV_CH] * _sigmoid(zc[:, CONV_CH:])
        return carry

    lax.fori_loop(0, seq // 256, glu, 0)

    shift = CONV_PAD - CONV_K // 2

    def chunk(ci, shift_ref):
        base = pl.multiple_of(ci * CONV_CHUNK, CONV_CHUNK)
        win = upad_ref[pl.ds(base, CONV_CHUNK + 2 * CONV_PAD), :]
        acc = jnp.zeros((CONV_CHUNK, CONV_CH), F32)
        for sub in range(SUBLANES):
            shift_ref[sub] = win[sub:sub + CONV_SPAN, :]
            for k in range(CONV_K):
                if (k + shift) % SUBLANES == sub:
                    lo = k + shift - sub
                    acc = acc + w_ref[k:k + 1, :] * shift_ref[sub, lo:lo + CONV_CHUNK, :]
        acc = acc + b_ref[...]
        mu = jnp.mean(acc, axis=-1, keepdims=True)
        d = acc - mu
        var = jnp.mean(d * d, axis=-1, keepdims=True)
        n = d * lax.rsqrt(var + EPS) * g_ref[...] + be_ref[...]
        o_ref[pl.ds(base, CONV_CHUNK), :] = (n * _sigmoid(n)).astype(BF16)

    def chunks(cj, carry):
        for u in range(CONV_UNROLL):
            chunk(cj * CONV_UNROLL + u, shift_refs.at[u])
        return carry

    lax.fori_loop(0, seq // (CONV_CHUNK * CONV_UNROLL), chunks, 0)


def _conv(z, row_block0, nseq, seq, w, b, g, be):
    return pl.pallas_call(
        functools.partial(_conv_kernel, seq),
        grid=(nseq,),
        in_specs=[pl.BlockSpec((seq, 2 * CONV_CH), lambda s: (row_block0 + s, 0)),
                  _full_spec((CONV_K, CONV_CH)), _full_spec((1, CONV_CH)),
                  _full_spec((1, CONV_CH)), _full_spec((1, CONV_CH))],
        out_specs=pl.BlockSpec((seq, CONV_CH), lambda s: (s, 0)),
        out_shape=jax.ShapeDtypeStruct((nseq * seq, CONV_CH), BF16),
        scratch_shapes=[pltpu.VMEM((seq + 2 * CONV_PAD, CONV_CH), F32),
                        pltpu.VMEM((CONV_UNROLL, SUBLANES, CONV_SPAN, CONV_CH), F32)],
        compiler_params=_cparams(("arbitrary",)),
        name="conv_seq%d" % seq,
    )(z, w, b, g, be)


def _dot_nt(a, b):
    return lax.dot_general(a, b, (((1,), (1,)), ((), ())), preferred_element_type=F32)


NA_SCALE = NA_DIM ** -0.5
assert NA_SCALE == 2.0 ** round(np.log2(NA_SCALE)), "query pre-scaling assumes a power-of-two scale"


def _ctx_attn_kernel(layer, q_ref, k_ref, v_ref, *refs):
    if layer:
        kprev_ref, vprev_ref, o_ref, ko_ref, vo_ref = refs
    else:
        o_ref, ko_ref, vo_ref = refs
    for j in range(DEPTH):
        if j < layer:
            ko_ref[j] = kprev_ref[j]
            vo_ref[j] = vprev_ref[j]
        elif j > layer:
            ko_ref[j] = jnp.zeros(ko_ref.shape[1:], F32)
            vo_ref[j] = jnp.zeros(vo_ref.shape[1:], F32)
    q = q_ref[...] * NA_SCALE
    ones = jnp.ones((SEQ, NA_DIM), BF16)

    def scores(h):
        cols = slice(h * NA_DIM, (h + 1) * NA_DIM)
        return _dot_nt(q[:, cols], k_ref[:, cols])

    outs = []
    nxt = scores(0)
    for h in range(NA_HEADS):
        s = nxt
        if h + 1 < NA_HEADS:
            nxt = scores(h + 1)
        cols = slice(h * NA_DIM, (h + 1) * NA_DIM)
        vh = v_ref[:, cols]
        ko_ref[layer, h] = k_ref[:, cols].astype(F32)
        vo_ref[layer, h] = vh.astype(F32)
        m = jnp.max(s, axis=-1, keepdims=True)
        p = jnp.exp(s - m).astype(BF16)
        o = jnp.dot(p, jnp.concatenate([vh, ones], axis=-1), preferred_element_type=F32)
        outs.append(o[:, :NA_DIM] / o[:, NA_DIM:])
    o_ref[...] = jnp.concatenate(outs, axis=-1).astype(BF16)


def _ctx_attn(z, layer, k_prev=None, v_prev=None):
    qb, kb, vb = COL_NA_Q // NA_WIDTH, COL_NA_K // NA_WIDTH, COL_NA_V // NA_WIDTH
    head_shape = jax.ShapeDtypeStruct((BATCH, DEPTH, NA_HEADS, SEQ, NA_DIM), F32)
    head_spec = pl.BlockSpec((None, DEPTH, NA_HEADS, SEQ, NA_DIM), lambda b: (b, 0, 0, 0, 0))
    in_specs = [pl.BlockSpec((SEQ, NA_WIDTH), lambda b: (b, qb)),
                pl.BlockSpec((SEQ, NA_WIDTH), lambda b: (b, kb)),
                pl.BlockSpec((SEQ, NA_WIDTH), lambda b: (b, vb))]
    args = [z, z, z]
    aliases = {}
    if layer:
        in_specs += [head_spec, head_spec]
        args += [k_prev, v_prev]
        aliases = {3: 1, 4: 2}
    return pl.pallas_call(
        functools.partial(_ctx_attn_kernel, layer),
        grid=(BATCH,),
        in_specs=in_specs,
        out_specs=[pl.BlockSpec((SEQ, NA_WIDTH), lambda b: (b, 0)), head_spec, head_spec],
        out_shape=[jax.ShapeDtypeStruct((T_CTX, NA_WIDTH), BF16), head_shape, head_shape],
        input_output_aliases=aliases,
        compiler_params=_cparams(("arbitrary",)),
        name="ctx_attn",
    )(*args)


NA_KINDS = (0, NA_ROWS, GRID_H - NA_ROWS)
N_DR = 2 * NA_KH - 1
N_DC = 2 * NA_KW - 1


def _na_row_offset(r0, i, j):
    ks = min(max(r0 - NA_KH // 2, 0), GRID_H - NA_KROWS)
    r, kr = r0 + i, ks + j
    rs = min(max(r - NA_KH // 2, 0), GRID_H - NA_KH)
    return kr - r + NA_KH - 1 if rs <= kr < rs + NA_KH else None


def _na_bias_kernel(rpb_ref, o_ref):
    lh = pl.program_id(0)
    shape = (GRID_W, 2 * GRID_W)
    qc = lax.broadcasted_iota(jnp.int32, shape, 0)
    lane = lax.broadcasted_iota(jnp.int32, shape, 1)
    kc = lane % GRID_W
    dc = jnp.clip(kc - qc, -(NA_KW - 1), NA_KW - 1) + NA_KW - 1
    cs = jnp.clip(qc - NA_KW // 2, 0, GRID_W - NA_KW)
    col_ok = (kc >= cs) & (kc < cs + NA_KW)
    neg = jnp.full(shape, NEG_INF, F32)
    tiles = []
    for dr in range(N_DR):
        base = (lh * N_DR + dr) * N_DC
        val = jnp.zeros(shape, F32)
        for d in range(N_DC):
            val = jnp.where(dc == d, rpb_ref[base + d], val)
        tiles.append(jnp.where(col_ok, val, neg))
    left = lane < GRID_W
    for kind, r0 in enumerate(NA_KINDS):
        for i in range(NA_ROWS):
            for jp in range(NA_KROWS // 2):
                dl, dr_ = _na_row_offset(r0, i, 2 * jp), _na_row_offset(r0, i, 2 * jp + 1)
                tl = neg if dl is None else tiles[dl]
                tr = neg if dr_ is None else tiles[dr_]
                o_ref[kind, i * GRID_W:(i + 1) * GRID_W, jp * 2 * GRID_W:(jp + 1) * 2 * GRID_W] = (
                    jnp.where(left, tl, tr))


def _na_bias_tables(rpb):
    return pl.pallas_call(
        _na_bias_kernel,
        grid=(DEPTH * NA_HEADS,),
        in_specs=[pl.BlockSpec(memory_space=pltpu.SMEM)],
        out_specs=pl.BlockSpec((None, len(NA_KINDS), NA_Q, NA_KEYS), lambda i: (i, 0, 0, 0)),
        out_shape=jax.ShapeDtypeStruct((DEPTH * NA_HEADS, len(NA_KINDS), NA_Q, NA_KEYS), F32),
        compiler_params=_cparams(("arbitrary",)),
        name="nbr_bias",
    )(rpb.reshape(-1))


NA_G = 4


def _na_kernel(q_ref, k_ref, v_ref, kc_ref, vc_ref, bias_ref, o_ref):
    rb = pl.program_id(2)
    ks = jnp.clip(rb * NA_ROWS - NA_KH // 2, 0, GRID_H - NA_KROWS)
    start = pl.multiple_of(ks * GRID_W, GRID_W)
    q = q_ref[...] * NA_SCALE
    kl = k_ref[pl.ds(start, NA_KEYS), :]
    vl = v_ref[pl.ds(start, NA_KEYS), :]
    ones_loc = jnp.ones((NA_KEYS, NA_DIM), BF16)
    ones_ctx = jnp.ones((PAST_LEN, NA_DIM), BF16)

    def scores(hh):
        cols = slice(hh * NA_DIM, (hh + 1) * NA_DIM)
        qh = q[:, cols]
        return _dot_nt(qh, kl[:, cols]) + bias_ref[hh], _dot_nt(qh, kc_ref[hh].astype(BF16))

    outs = []
    nxt = scores(0)
    for hh in range(NA_G):
        s_loc, s_ctx = nxt
        if hh + 1 < NA_G:
            nxt = scores(hh + 1)
        cols = slice(hh * NA_DIM, (hh + 1) * NA_DIM)
        m = jnp.maximum(jnp.max(s_loc, axis=-1, keepdims=True), jnp.max(s_ctx, axis=-1, keepdims=True))
        p_loc = jnp.exp(s_loc - m).astype(BF16)
        p_ctx = jnp.exp(s_ctx - m).astype(BF16)
        v_ext = jnp.concatenate([vl[:, cols], ones_loc], axis=-1)
        vc_ext = jnp.concatenate([vc_ref[hh].astype(BF16), ones_ctx], axis=-1)
        o = (jnp.dot(p_loc, v_ext, preferred_element_type=F32)
             + jnp.dot(p_ctx, vc_ext, preferred_element_type=F32))
        outs.append(o[:, :NA_DIM] / o[:, NA_DIM:])
    o_ref[...] = jnp.concatenate(outs, axis=-1).astype(BF16)


def _na_attn(z, cache_k, cache_v, bias, layer):
    lat_q0 = T_CTX // NA_Q
    lat_s0 = T_CTX // DEC_SEQ
    width = NA_G * NA_DIM
    qc, kc, vc = COL_NA_Q // width, COL_NA_K // width, COL_NA_V // width
    groups = NA_HEADS // NA_G

    def kind(rb):
        return jnp.where(rb == 0, 0, jnp.where(rb == NA_RB - 1, 2, 1))

    ctx_spec = pl.BlockSpec((None, None, NA_G, PAST_LEN, NA_DIM), lambda b, hg, rb: (b, layer, hg, 0, 0))
    return pl.pallas_call(
        _na_kernel,
        grid=(DEC_BATCH, groups, NA_RB),
        in_specs=[pl.BlockSpec((NA_Q, width), lambda b, hg, rb: (lat_q0 + b * NA_RB + rb, qc + hg)),
                  pl.BlockSpec((DEC_SEQ, width), lambda b, hg, rb: (lat_s0 + b, kc + hg)),
                  pl.BlockSpec((DEC_SEQ, width), lambda b, hg, rb: (lat_s0 + b, vc + hg)),
                  ctx_spec, ctx_spec,
                  pl.BlockSpec((NA_G, None, NA_Q, NA_KEYS),
                               lambda b, hg, rb: (layer * groups + hg, kind(rb), 0, 0))],
        out_specs=pl.BlockSpec((NA_Q, width), lambda b, hg, rb: (b * NA_RB + rb, hg)),
        out_shape=jax.ShapeDtypeStruct((T_LAT, NA_WIDTH), BF16),
        compiler_params=_cparams(("arbitrary", "arbitrary", "arbitrary")),
        name="nbr_attn",
    )(z, z, z, cache_k, cache_v, bias)


RET_PAIR = 2 * RET_DIM
RET_NPAIR = RET_HEADS // 2
assert RET_PAIR == LANES and RET_CHUNK == LANES
RET_UNROLL = 8


def _rope_tables():
    n_freq = RET_DIM // 4
    t = np.arange(DEC_SEQ)
    inv = jnp.asarray(ROPE_BASE, F32) ** (-jnp.arange(n_freq, dtype=F32) / n_freq)
    ang_r = jnp.asarray(t // GRID_W, F32)[:, None] * inv[None, :]
    ang_c = jnp.asarray(t % GRID_W, F32)[:, None] * inv[None, :]
    cos = jnp.concatenate([jnp.cos(ang_r)] * 2 + [jnp.cos(ang_c)] * 2, axis=-1)
    sin = jnp.concatenate([-jnp.sin(ang_r), jnp.sin(ang_r), -jnp.sin(ang_c), jnp.sin(ang_c)], axis=-1)
    lane = np.arange(RET_WIDTH)
    src = np.where(lane % (2 * n_freq) < n_freq, lane + n_freq, lane - n_freq)
    swap = np.zeros((RET_WIDTH, RET_WIDTH), np.float32)
    swap[src, lane] = 1.0
    return jnp.tile(cos, (1, RET_HEADS)), jnp.tile(sin, (1, RET_HEADS)), jnp.asarray(swap, BF16)


def _ret_kernel(seq, latent, *refs):
    if latent:
        (lg_ref, z_ref, gn_ref, cos_ref, sin_ref, swap_ref, s0f_ref, s0b_ref, y_ref,
         q_s, k_s, kv_s, st_s) = refs
    else:
        lg_ref, z_ref, gn_ref, y_ref, sf_ref, sb_ref, q_s, k_s, kv_s, st_s = refs
    nc = seq // RET_CHUNK
    ch, hd, pw = RET_CHUNK, RET_DIM, RET_PAIR

    row = lax.broadcasted_iota(jnp.int32, (ch, ch), 0).astype(F32)
    col = lax.broadcasted_iota(jnp.int32, (ch, ch), 1).astype(F32)
    pos = lax.broadcasted_iota(jnp.int32, (ch, pw), 0).astype(F32)
    left = lax.broadcasted_iota(jnp.int32, (ch, pw), 1) < hd
    top = lax.broadcasted_iota(jnp.int32, (pw, pw), 0) < hd
    same_head = top == (lax.broadcasted_iota(jnp.int32, (pw, pw), 1) < hd)
    same_head2 = jnp.concatenate([same_head, same_head], axis=0)

    def per_head(mask, fn, p):
        return jnp.where(mask, fn(2 * p), fn(2 * p + 1))

    decay = []
    for h in range(RET_HEADS):
        lf, lb = lg_ref[0, h], lg_ref[1, h]
        d_f = jnp.where(row >= col, jnp.exp(jnp.maximum(row - col, 0.0) * lf), 0.0)
        d_b = jnp.where(col >= row, jnp.exp(jnp.maximum(col - row, 0.0) * lb), 0.0)
        decay.append(d_f + d_b)
    q_dec, k_dec, c_dec_f, c_dec_b = [], [], [], []
    for p in range(RET_NPAIR):
        q_dec.append(jnp.concatenate(
            [per_head(left, lambda h: jnp.exp((pos + 1.0) * lg_ref[0, h]), p),
             per_head(left, lambda h: jnp.exp((ch - pos) * lg_ref[1, h]), p)], axis=-1))
        k_dec.append(jnp.concatenate(
            [per_head(left, lambda h: jnp.exp((ch - 1.0 - pos) * lg_ref[0, h]), p),
             per_head(left, lambda h: jnp.exp(pos * lg_ref[1, h]), p)], axis=-1))
        zero = jnp.zeros((pw, pw), F32)
        c_dec_f.append(per_head(top, lambda h: jnp.exp(zero + ch * lg_ref[0, h]), p))
        c_dec_b.append(per_head(top, lambda h: jnp.exp(zero + ch * lg_ref[1, h]), p))

    def rope(x, base):
        xf = x.astype(F32)
        if not latent:
            return xf
        swapped = jnp.dot(x, swap_ref[...], preferred_element_type=F32)
        return xf * cos_ref[pl.ds(base, ch), :] + swapped * sin_ref[pl.ds(base, ch), :]

    def pass1(n, carry):
        base = pl.multiple_of(n * ch, ch)
        zc = z_ref[pl.ds(base, ch), :]
        q = rope(zc[:, 0:RET_WIDTH], base)
        k = rope(zc[:, RET_WIDTH:2 * RET_WIDTH], base) * (RET_DIM ** -0.5)
        q_s[pl.ds(base, ch), :] = q.astype(BF16)
        k_s[pl.ds(base, ch), :] = k.astype(BF16)
        v = zc[:, 2 * RET_WIDTH:3 * RET_WIDTH]
        for p in range(RET_NPAIR):
            lanes = slice(p * pw, (p + 1) * pw)
            kp = k[:, lanes]
            k2 = (jnp.concatenate([kp, kp], axis=-1) * k_dec[p]).astype(BF16)
            kv = lax.dot_general(k2, v[:, lanes], (((0,), (0,)), ((), ())), preferred_element_type=F32)
            kv_s[n, p] = jnp.where(same_head2, kv, 0.0)
        return carry

    lax.fori_loop(0, nc, pass1, 0, unroll=min(RET_UNROLL, nc))

    def block_diag(a, b):
        z = jnp.zeros((hd, hd), F32)
        return jnp.concatenate([jnp.concatenate([a, z], axis=1), jnp.concatenate([z, b], axis=1)], axis=0)

    for p in range(RET_NPAIR):
        if latent:
            s_f = block_diag(s0f_ref[2 * p], s0f_ref[2 * p + 1])
            s_b = block_diag(s0b_ref[2 * p], s0b_ref[2 * p + 1])
        else:
            s_f = s_b = jnp.zeros((pw, pw), F32)

        def fwd(n, s, p=p):
            st_s[n, p, 0:pw, :] = s.astype(BF16)
            return c_dec_f[p] * s + kv_s[n, p, 0:pw, :]

        def bwd(i, s, p=p):
            n = nc - 1 - i
            st_s[n, p, pw:2 * pw, :] = s.astype(BF16)
            return c_dec_b[p] * s + kv_s[n, p, pw:2 * pw, :]

        s_f = lax.fori_loop(0, nc, fwd, s_f)
        s_b = lax.fori_loop(0, nc, bwd, s_b)
        if not latent:
            for hh in range(2):
                blk = slice(hh * hd, (hh + 1) * hd)
                sf_ref[2 * p + hh] = s_f[blk, blk]
                sb_ref[2 * p + hh] = s_b[blk, blk]

    def pass3(n, carry):
        base = pl.multiple_of(n * ch, ch)
        zc = z_ref[pl.ds(base, ch), :]
        q = q_s[pl.ds(base, ch), :]
        k = k_s[pl.ds(base, ch), :]
        v = zc[:, 2 * RET_WIDTH:3 * RET_WIDTH]
        gate = zc[:, 3 * RET_WIDTH:4 * RET_WIDTH].astype(F32)
        outs = []
        for p in range(RET_NPAIR):
            lanes = slice(p * pw, (p + 1) * pw)
            qp, kp, vp = q[:, lanes], k[:, lanes], v[:, lanes]
            o_h = []
            for hh in range(2):
                qm = jnp.where(left == (hh == 0), qp, jnp.zeros_like(qp))
                s = _dot_nt(qm, kp) * decay[2 * p + hh]
                o_h.append(jnp.dot(s.astype(BF16), vp, preferred_element_type=F32))
            qf = qp.astype(F32)
            q2 = (jnp.concatenate([qf, qf], axis=-1) * q_dec[p]).astype(BF16)
            o = jnp.where(left, o_h[0], o_h[1]) + jnp.dot(q2, st_s[n, p], preferred_element_type=F32)

            def half_mean(t):
                s_l = jnp.sum(jnp.where(left, t, 0.0), axis=-1, keepdims=True)
                s_r = jnp.sum(jnp.where(left, 0.0, t), axis=-1, keepdims=True)
                return jnp.where(left, s_l, s_r) * (1.0 / hd)

            d = o - half_mean(o)
            outs.append(d * lax.rsqrt(half_mean(d * d) + EPS))
        nrm = jnp.concatenate(outs, axis=-1)
        y_ref[pl.ds(base, ch), :] = (nrm * gn_ref[...] * (gate * _sigmoid(gate))).astype(BF16)
        return carry

    lax.fori_loop(0, nc, pass3, 0, unroll=min(RET_UNROLL, nc))


def _retention(z, lg, gn_g, latent, layer=None, rope=None, s0_f=None, s0_b=None):
    seq = DEC_SEQ if latent else SEQ
    nseq = DEC_BATCH if latent else BATCH
    nc = seq // RET_CHUNK
    row0 = (T_CTX // DEC_SEQ) if latent else 0
    cb = COL_RET // (4 * RET_WIDTH)
    in_specs = [pl.BlockSpec(memory_space=pltpu.SMEM),
                pl.BlockSpec((seq, 4 * RET_WIDTH), lambda s: (row0 + s, cb)),
                _full_spec((1, RET_WIDTH))]
    args = [lg, z, gn_g]
    state_shape = jax.ShapeDtypeStruct((nseq, RET_HEADS, RET_DIM, RET_DIM), F32)
    y_spec = pl.BlockSpec((seq, RET_WIDTH), lambda s: (s, 0))
    y_shape = jax.ShapeDtypeStruct((nseq * seq, RET_WIDTH), BF16)
    if latent:
        st_spec = pl.BlockSpec((None, None, RET_HEADS, RET_DIM, RET_DIM), lambda s: (s, layer, 0, 0, 0))

        def const_spec(shape):
            return pl.BlockSpec(shape, lambda s: (0,) * len(shape), pipeline_mode=pl.Buffered(1))

        in_specs += [const_spec((seq, RET_WIDTH)), const_spec((seq, RET_WIDTH)),
                     const_spec((RET_WIDTH, RET_WIDTH)), st_spec, st_spec]
        args += [rope[0], rope[1], rope[2], s0_f, s0_b]
        out_specs, out_shape = y_spec, y_shape
    else:
        so_spec = pl.BlockSpec((None, RET_HEADS, RET_DIM, RET_DIM), lambda s: (s, 0, 0, 0))
        out_specs, out_shape = [y_spec, so_spec, so_spec], [y_shape, state_shape, state_shape]
    return pl.pallas_call(
        functools.partial(_ret_kernel, seq, latent),
        grid=(nseq,),
        in_specs=in_specs,
        out_specs=out_specs,
        out_shape=out_shape,
        scratch_shapes=[pltpu.VMEM((seq, RET_WIDTH), BF16), pltpu.VMEM((seq, RET_WIDTH), BF16),
                        pltpu.VMEM((nc, RET_NPAIR, 2 * RET_PAIR, RET_PAIR), F32),
                        pltpu.VMEM((nc, RET_NPAIR, 2 * RET_PAIR, RET_PAIR), BF16)],
        compiler_params=_cparams(("arbitrary",)),
        name="retention_lat" if latent else "retention_ctx",
    )(*args)


def _route(logits):
    lane = lax.broadcasted_iota(jnp.int32, logits.shape, 1)
    lane_f = lane.astype(F32)
    big = float(ROUTE_COLS)
    neg = -jnp.inf
    is_grp = lane < N_GROUPS
    gl = jnp.where(is_grp, logits, neg)
    gmax = jnp.max(gl, axis=-1, keepdims=True)
    grp = jnp.min(jnp.where(gl == gmax, lane_f, big), axis=-1, keepdims=True)
    p_grp = 1.0 / jnp.sum(jnp.exp(gl - gmax), axis=-1, keepdims=True)
    e_f = lane_f - N_GROUPS
    lo = grp * EXPERTS_PER_GROUP
    in_grp = (e_f >= lo) & (e_f < lo + EXPERTS_PER_GROUP)
    el = jnp.where(in_grp, logits, neg)
    m1 = jnp.max(el, axis=-1, keepdims=True)
    i1 = jnp.min(jnp.where(el == m1, lane_f, big), axis=-1, keepdims=True)
    el2 = jnp.where(lane_f == i1, neg, el)
    m2 = jnp.max(el2, axis=-1, keepdims=True)
    i2 = jnp.min(jnp.where(el2 == m2, lane_f, big), axis=-1, keepdims=True)
    t = jnp.exp(m2 - m1)
    g1 = p_grp / (1.0 + t)
    g2 = p_grp * t / (1.0 + t)
    rows = logits.shape[0]
    oh1, oh2 = lane_f == i1, lane_f == i2
    oh = jnp.where(oh1 | oh2, 1.0, 0.0)
    tri = (lax.broadcasted_iota(jnp.int32, (rows, rows), 0)
           > lax.broadcasted_iota(jnp.int32, (rows, rows), 1))
    rank = jnp.dot(jnp.where(tri, 1.0, 0.0).astype(BF16), oh.astype(BF16), preferred_element_type=F32)
    tiles = jnp.floor((jnp.sum(oh, axis=0, keepdims=True) + (SUBLANES - 1)) * (1.0 / SUBLANES))
    upper = (lax.broadcasted_iota(jnp.int32, (ROUTE_COLS, ROUTE_COLS), 0)
             < lax.broadcasted_iota(jnp.int32, (ROUTE_COLS, ROUTE_COLS), 1))
    start = SUBLANES * jnp.dot(jnp.broadcast_to(tiles, (SUBLANES, ROUTE_COLS)).astype(BF16),
                               jnp.where(upper, 1.0, 0.0).astype(BF16),
                               preferred_element_type=F32)[0:1, :]
    pos = start + rank
    p1 = jnp.sum(jnp.where(oh1, pos, 0.0), axis=-1, keepdims=True)
    p2 = jnp.sum(jnp.where(oh2, pos, 0.0), axis=-1, keepdims=True)
    out = jnp.zeros(logits.shape, F32)
    for k, val in enumerate((i1 - N_GROUPS, i2 - N_GROUPS, g1, g2, p1, p2)):
        out = jnp.where(lane == k, val, out)
    return out, SUBLANES * tiles


def _outproj_kernel(ycc, ycl, ync, ynl, yrc, yrl, x_ref, mod_ref, g_ref, w_ref, wr_ref, br_ref,
                    xo_ref, xs_ref, r_ref, seg_ref):
    is_ctx = pl.program_id(0) < NB_CTX
    yc = jnp.where(is_ctx, ycc[...], ycl[...])
    yn = jnp.where(is_ctx, ync[...], ynl[...])
    yr = jnp.where(is_ctx, yrc[...], yrl[...])
    y = (jnp.dot(yc, w_ref[0:CONV_CH, :], preferred_element_type=F32)
         + jnp.dot(yn, w_ref[CONV_CH:CONV_CH + NA_WIDTH, :], preferred_element_type=F32)
         + jnp.dot(yr, w_ref[CONV_CH + NA_WIDTH:, :], preferred_element_type=F32))
    x = x_ref[...] + mod_ref[2:3, :] * y
    xo_ref[...] = x
    h = _norm_mod(x, g_ref[...], mod_ref[3:4, :], mod_ref[4:5, :])
    h_hi = h.astype(BF16)
    h_lo = (h - h_hi.astype(F32)).astype(BF16)
    hw = jnp.dot(h_hi, wr_ref[...], preferred_element_type=F32)
    logits = (hw[:, :ROUTE_COLS] + hw[:, ROUTE_COLS:]
              + jnp.dot(h_lo, wr_ref[:, :ROUTE_COLS], preferred_element_type=F32) + br_ref[...])
    route, seg = _route(logits)
    r_ref[...] = route
    seg_ref[...] = jnp.broadcast_to(seg, seg_ref.shape)
    sel = _slot_onehot(route, 0) | _slot_onehot(route, 1)
    xs_ref[...] = _pack_bf16_pairs(lax.dot_general(jnp.where(sel, 1.0, 0.0).astype(BF16), h_hi,
                                                   (((0,), (0,)), ((), ())), preferred_element_type=F32))


def _outproj(y_conv, y_na, y_ret, x, mod, g, w_bf16, w_route, b_route):
    return pl.pallas_call(
        _outproj_kernel,
        grid=(NB_ALL,),
        in_specs=(_ctx_lat_specs(CONV_CH) + _ctx_lat_specs(NA_WIDTH) + _ctx_lat_specs(RET_WIDTH)
                  + [_tok_spec(D_MODEL), _mod_spec(), _full_spec((1, D_MODEL)),
                     _full_spec((D_MODEL, D_MODEL)), _full_spec((D_MODEL, 2 * ROUTE_COLS)),
                     _full_spec((1, ROUTE_COLS))]),
        out_specs=[_tok_spec(D_MODEL), pl.BlockSpec((MOE_LC, D_HALF), lambda i: (i, 0)),
                   _tok_spec(ROUTE_COLS), pl.BlockSpec((None, SUBLANES, ROUTE_COLS), lambda i: (i, 0, 0))],
        out_shape=[jax.ShapeDtypeStruct((T_ALL, D_MODEL), F32),
                   jax.ShapeDtypeStruct((NB_ALL * MOE_LC, D_HALF), U32),
                   jax.ShapeDtypeStruct((T_ALL, ROUTE_COLS), F32),
                   jax.ShapeDtypeStruct((NB_ALL, SUBLANES, ROUTE_COLS), F32)],
        compiler_params=_cparams(("arbitrary",)),
        name="outproj_route",
    )(y_conv[0], y_conv[1], y_na[0], y_na[1], y_ret[0], y_ret[1], x, mod, g, w_bf16, w_route, b_route)


def _dispatch_tables(seg):
    seg_len = seg[:, 0, N_GROUPS:N_GROUPS + N_EXPERTS].astype(jnp.int32)
    experts = jnp.arange(N_EXPERTS, dtype=jnp.int32)
    in_chunk = jnp.cumsum(seg_len, axis=1) - seg_len
    seg_row = in_chunk + MOE_LC * jnp.arange(NB_ALL, dtype=jnp.int32)[:, None]
    seg_off = jnp.cumsum(seg_len, axis=0) - seg_len
    rows_e = jnp.sum(seg_len, axis=0)
    chunk_rows = jnp.sum(seg_len, axis=1)
    nblk = (rows_e + MOE_BLK - 1) // MOE_BLK
    blk_end = jnp.cumsum(nblk)
    blk_start = blk_end - nblk
    blk = jnp.arange(MOE_NBLK, dtype=jnp.int32)
    n_active = blk_end[-1]
    blk_e = jnp.minimum(jnp.sum((blk_end[None, :] <= jnp.minimum(blk, n_active - 1)[:, None]).astype(jnp.int32),
                                axis=-1), N_EXPERTS - 1)
    mine = blk_e[:, None] == experts[None, :]
    blk_lo = (blk - jnp.sum(jnp.where(mine, blk_start[None, :], 0), axis=-1)) * MOE_BLK
    left = jnp.sum(jnp.where(mine, rows_e[None, :], 0), axis=-1) - blk_lo
    blk_nv = jnp.where(blk < n_active, jnp.clip(left, 0, MOE_BLK), 0).astype(jnp.int32)
    off_b = jnp.sum(jnp.where(mine[:, None, :], seg_off[None, :, :], 0), axis=-1)
    end_b = off_b + jnp.sum(jnp.where(mine[:, None, :], seg_len[None, :, :], 0), axis=-1)
    blk_c0 = jnp.sum((end_b <= blk_lo[:, None]).astype(jnp.int32), axis=-1)
    blk_c1 = jnp.sum((off_b < (blk_lo + blk_nv)[:, None]).astype(jnp.int32), axis=-1)
    after = jnp.sum(jnp.where(mine, blk_end[None, :], 0), axis=-1)
    blk_next_e = jnp.where(after < n_active, jnp.take(blk_e, jnp.minimum(after, MOE_NBLK - 1)), -1)
    return (blk_e, blk_next_e.astype(jnp.int32), blk_lo.astype(jnp.int32), blk_nv, blk_c0, blk_c1,
            seg_off.reshape(-1), seg_len.reshape(-1), seg_row.reshape(-1), chunk_rows)


def _moe_kernel(layer, blk_e, blk_next_e, blk_lo, blk_nv, blk_c0, blk_c1, seg_off, seg_len, seg_row,
                chunk_rows, xs_hbm, w1_hbm, w3_hbm, w2_hbm, ys_hbm, xbuf, obuf, zeros,
                w1f, w3f, w2f, w1b, w3b, w2b, gsem, ssem, zsem, wsem):
    i = pl.program_id(0)
    last = pl.num_programs(0) - 1
    slot = i % 2

    def tiles(v):
        return pl.multiple_of(v, SUBLANES)

    def for_segments(blk, fn):
        lo = blk_lo[blk]
        hi = lo + blk_nv[blk]

        def body(c, carry):
            k = c * N_EXPERTS + blk_e[blk]
            s_lo = seg_off[k]
            a = jnp.maximum(s_lo, lo)
            n = jnp.minimum(s_lo + seg_len[k], hi) - a

            @pl.when(n > 0)
            def _():
                fn(tiles(seg_row[k] + a - s_lo), tiles(a - lo), tiles(n))

            return carry

        lax.fori_loop(blk_c0[blk], blk_c1[blk], body, 0)

    def weight_copies(e):
        return [pltpu.make_async_copy(src.at[layer, e], dst, wsem)
                for src, dst in ((w1_hbm, w1f), (w3_hbm, w3f), (w2_hbm, w2f))]

    def start_gathers(blk, s):
        for_segments(blk, lambda src, dst, n: pltpu.make_async_copy(
            xs_hbm.at[pl.ds(src, n)], xbuf.at[s, pl.ds(dst, n)], gsem.at[s]).start())

    def start_scatters(blk, s):
        for_segments(blk, lambda dst, src, n: pltpu.make_async_copy(
            obuf.at[s, pl.ds(src, n)], ys_hbm.at[pl.ds(dst, n)], ssem.at[s]).start())

    def wait_rows(blk, s, sem):
        n = tiles(blk_nv[blk])

        @pl.when(n > 0)
        def _():
            pltpu.make_async_copy(xs_hbm.at[pl.ds(0, n)], xbuf.at[s, pl.ds(0, n)], sem.at[s]).wait()

    @pl.when(i == 0)
    def _():
        xbuf[...] = jnp.zeros_like(xbuf)
        zeros[...] = jnp.zeros_like(zeros)

        def tail(c):
            n = tiles(MOE_LC - chunk_rows[c])
            return n, pltpu.make_async_copy(zeros.at[pl.ds(0, n)],
                                            ys_hbm.at[pl.ds(tiles(c * MOE_LC + chunk_rows[c]), n)], zsem)

        def fill(c, carry):
            n, copy = tail(c)
            pl.when(n > 0)(copy.start)
            return carry

        def drain(c, carry):
            n, copy = tail(c)
            pl.when(n > 0)(copy.wait)
            return carry

        lax.fori_loop(0, NB_ALL, fill, 0)
        lax.fori_loop(0, NB_ALL, drain, 0)
        start_gathers(0, 0)
        for copy in weight_copies(blk_e[0]):
            copy.start()

    @pl.when(i < last)
    def _():
        start_gathers(i + 1, 1 - slot)

    @pl.when(i >= 2)
    def _():
        wait_rows(i - 2, slot, ssem)

    @pl.when(blk_nv[i] > 0)
    def _():
        @pl.when((i == 0) | (blk_e[i] != blk_e[jnp.maximum(i - 1, 0)]))
        def _():
            for copy in weight_copies(blk_e[i]):
                copy.wait()
            w1b[...] = w1f[...].astype(BF16)
            w3b[...] = w3f[...].astype(BF16)
            w2b[...] = w2f[...].astype(BF16)

            @pl.when(blk_next_e[i] >= 0)
            def _():
                for copy in weight_copies(blk_next_e[i]):
                    copy.start()

        wait_rows(i, slot, gsem)
        x_lo, x_hi = _unpack_bf16_pairs(xbuf[slot])

        def in_dot(w):
            return (jnp.dot(x_lo, w[:D_HALF, :], preferred_element_type=F32)
                    + jnp.dot(x_hi, w[D_HALF:, :], preferred_element_type=F32))

        a = in_dot(w1b)
        b = in_dot(w3b)
        mid = (a * _sigmoid(a) * b).astype(BF16)
        obuf[slot] = _pack_bf16_pairs(jnp.dot(mid, w2b[...], preferred_element_type=F32))
        start_scatters(i, slot)

    @pl.when(i == last)
    def _():
        wait_rows(i - 1, 1 - slot, ssem)
        wait_rows(i, slot, ssem)


def _moe(xs, w1, w3, w2, layer, blk_e, blk_next_e, blk_lo, blk_nv, blk_c0, blk_c1, seg_off, seg_len,
         seg_row, chunk_rows):
    any_spec = pl.BlockSpec(memory_space=pl.ANY)
    grid_spec = pltpu.PrefetchScalarGridSpec(
        num_scalar_prefetch=10,
        grid=(MOE_NBLK,),
        in_specs=[any_spec, any_spec, any_spec, any_spec],
        out_specs=any_spec,
        scratch_shapes=[pltpu.VMEM((2, MOE_BLK, D_HALF), U32), pltpu.VMEM((2, MOE_BLK, D_HALF), U32),
                        pltpu.VMEM((MOE_LC - 2 * TM, D_HALF), U32),
                        pltpu.VMEM((D_MODEL, D_EXPERT), F32), pltpu.VMEM((D_MODEL, D_EXPERT), F32),
                        pltpu.VMEM((D_EXPERT, D_MODEL), F32),
                        pltpu.VMEM((D_MODEL, D_EXPERT), BF16), pltpu.VMEM((D_MODEL, D_EXPERT), BF16),
                        pltpu.VMEM((D_EXPERT, D_MODEL), BF16),
                        pltpu.SemaphoreType.DMA((2,)), pltpu.SemaphoreType.DMA((2,)),
                        pltpu.SemaphoreType.DMA, pltpu.SemaphoreType.DMA])
    return pl.pallas_call(
        functools.partial(_moe_kernel, layer),
        grid_spec=grid_spec,
        out_shape=jax.ShapeDtypeStruct((NB_ALL * MOE_LC, D_HALF), U32),
        compiler_params=_cparams(("arbitrary",)),
        name="moe_experts",
    )(blk_e, blk_next_e, blk_lo, blk_nv, blk_c0, blk_c1, seg_off, seg_len, seg_row, chunk_rows,
      xs, w1, w3, w2)


def _final_kernel(x_ref, ys_ref, r_ref, mod_ref, g_ref, o_ref):
    x = _moe_residual(x_ref, ys_ref, r_ref, mod_ref)
    ms = jnp.mean(x * x, axis=-1, keepdims=True)
    o_ref[...] = x * lax.rsqrt(ms + EPS) * g_ref[...]


def _final(x, ys, route, mod, g, block0, nblocks):
    return pl.pallas_call(
        _final_kernel,
        grid=(nblocks,),
        in_specs=[pl.BlockSpec((TM, D_MODEL), lambda i: (block0 + i, 0)),
                  pl.BlockSpec((MOE_LC, D_HALF), lambda i: (block0 + i, 0)),
                  pl.BlockSpec((TM, ROUTE_COLS), lambda i: (block0 + i, 0)),
                  pl.BlockSpec((None, 6, D_MODEL), lambda i: (_cond_row(block0 + i), 0, 0)),
                  _full_spec((1, D_MODEL))],
        out_specs=_tok_spec(D_MODEL),
        out_shape=jax.ShapeDtypeStruct((nblocks * TM, D_MODEL), F32),
        compiler_params=_cparams(("arbitrary",)),
        name="final_norm",
    )(x, ys, route, mod, g)


def kernel(x_prompt, x_sample, c, cache_k, cache_v, state_ret_f, state_ret_b, c_ctx, w_ada, b_ada, norm1_g, norm2_g, w_in, w_out, conv_w, conv_b, conv_ln_g, conv_ln_b, na_rpb, ret_lg_f, ret_lg_b, ret_gn_g, w_route_g, b_route_g, w_route_e, b_route_e, w1, w3, w2, final_g):
    cv = jnp.zeros((COND_ROWS, D_MODEL), F32).at[0].set(c_ctx).at[1:N_COND].set(c)
    mods = _ada(cv, w_ada, b_ada).reshape(DEPTH, COND_ROWS, 6, D_MODEL)
    w_in_b = w_in.astype(BF16)
    w_out_b = w_out.astype(BF16)
    pad = ROUTE_COLS - N_GROUPS - N_EXPERTS
    w_route = jnp.pad(jnp.concatenate([w_route_g, w_route_e], axis=-1), ((0, 0), (0, 0), (0, pad)))
    b_route = jnp.pad(jnp.concatenate([b_route_g, b_route_e], axis=-1), ((0, 0), (0, pad)))
    w_route_hi = w_route.astype(BF16)
    w_route_lo = (w_route - w_route_hi.astype(F32)).astype(BF16)
    w_route = jnp.concatenate([w_route_hi, w_route_lo], axis=-1)
    na_bias = _na_bias_tables(na_rpb)
    rope = _rope_tables()
    lg = jnp.stack([ret_lg_f, ret_lg_b], axis=1)

    x_ctx = x_prompt.reshape(T_CTX, D_MODEL)
    x_lat = x_sample.reshape(T_LAT, D_MODEL)
    x = y = route = new_k = new_v = None
    sf_list, sb_list = [], []
    for l in range(DEPTH):
        g1 = norm1_g[l].reshape(1, D_MODEL)
        if l == 0:
            z, x = _inproj_first(x_ctx, x_lat, mods[l], g1, w_in_b[l])
        else:
            z, x = _inproj_next(x, y, route, mods[l - 1], mods[l], g1, w_in_b[l])
        conv_args = (conv_w[l], conv_b[l].reshape(1, -1), conv_ln_g[l].reshape(1, -1),
                     conv_ln_b[l].reshape(1, -1))
        yc_c = _conv(z, 0, BATCH, SEQ, *conv_args)
        yc_l = _conv(z, T_CTX // DEC_SEQ, DEC_BATCH, DEC_SEQ, *conv_args)
        yn_c, new_k, new_v = _ctx_attn(z, l, new_k, new_v)
        yn_l = _na_attn(z, cache_k, cache_v, na_bias, l)
        gn = ret_gn_g[l].reshape(1, RET_WIDTH)
        yr_c, sf_l, sb_l = _retention(z, lg[l], gn, latent=False)
        yr_l = _retention(z, lg[l], gn, latent=True, layer=l, rope=rope,
                          s0_f=state_ret_f, s0_b=state_ret_b)
        x, xs, route, seg = _outproj((yc_c, yc_l), (yn_c, yn_l), (yr_c, yr_l), x, mods[l],
                                     norm2_g[l].reshape(1, D_MODEL), w_out_b[l], w_route[l],
                                     b_route[l].reshape(1, ROUTE_COLS))
        y = _moe(xs, w1, w3, w2, l, *_dispatch_tables(seg))
        sf_list.append(sf_l)
        sb_list.append(sb_l)
    fg = final_g.reshape(1, D_MODEL)
    y_prompt = _final(x, y, route, mods[DEPTH - 1], fg, 0, NB_CTX).reshape(BATCH, SEQ, D_MODEL)
    y_sample = _final(x, y, route, mods[DEPTH - 1], fg, NB_CTX, NB_LAT).reshape(DEC_BATCH, DEC_SEQ, D_MODEL)
    return (y_prompt, y_sample, new_k, new_v, jnp.stack(sf_list, axis=1), jnp.stack(sb_list, axis=1))
```

```python
import functools

import numpy as np
import jax
import jax.numpy as jnp
from jax import lax
from jax.experimental import pallas as pl
from jax.experimental.pallas import tpu as pltpu

D_MODEL = 1024
BATCH = 32
SEQ = 256
DEPTH = 2
DEC_BATCH = 4
DEC_SEQ = 4096
PAST_LEN = 512
GRID_W = 64
GRID_H = DEC_SEQ // GRID_W
CONV_CH = 256
CONV_K = 31
NA_HEADS = 8
NA_DIM = 64
NA_WIDTH = NA_HEADS * NA_DIM
NA_KH = 8
NA_KW = 16
RET_HEADS = 4
RET_DIM = 64
RET_WIDTH = RET_HEADS * RET_DIM
RET_CHUNK = 128
ROPE_BASE = 10000.0
N_GROUPS = 4
EXPERTS_PER_GROUP = 8
N_EXPERTS = N_GROUPS * EXPERTS_PER_GROUP
D_EXPERT = 512
IN_COLS = 2 * CONV_CH + 3 * NA_WIDTH + 4 * RET_WIDTH
EPS = 1e-6
NEG_INF = -1e30

F32 = jnp.float32
BF16 = jnp.bfloat16
HIGHEST = lax.Precision.HIGHEST

T_CTX = BATCH * SEQ
T_LAT = DEC_BATCH * DEC_SEQ
T_ALL = T_CTX + T_LAT
N_COND = 1 + DEC_BATCH
COND_ROWS = 8

TM = 512
NB_CTX = T_CTX // TM
NB_LAT = T_LAT // TM
NB_ALL = NB_CTX + NB_LAT
LAT_BLOCKS_PER_REQ = DEC_SEQ // TM

LANES = 128
SUBLANES = 8
ROUTE_COLS = LANES

COL_CONV = 0
COL_NA_Q = 2 * CONV_CH
COL_NA_K = COL_NA_Q + NA_WIDTH
COL_NA_V = COL_NA_K + NA_WIDTH
COL_RET = COL_NA_V + NA_WIDTH

NA_ROWS = 8
NA_Q = NA_ROWS * GRID_W
NA_KROWS = NA_ROWS + NA_KH
NA_KEYS = NA_KROWS * GRID_W
NA_RB = GRID_H // NA_ROWS

MOE_BLK = 512
MOE_LC = -(-(2 * TM + N_EXPERTS * (SUBLANES - 1)) // LANES) * LANES
MOE_NBLK = -(-(NB_ALL * MOE_LC) // MOE_BLK) + N_EXPERTS
N_SEG = NB_ALL * N_EXPERTS

VMEM_LIMIT = 56 * 1024 * 1024


def _cparams(sem):
    return pltpu.CompilerParams(dimension_semantics=sem, vmem_limit_bytes=VMEM_LIMIT)


def _sigmoid(x):
    return 1.0 / (1.0 + jnp.exp(-x))


def _cond_row(i):
    return jnp.where(i < NB_CTX, 0, 1 + (i - NB_CTX) // LAT_BLOCKS_PER_REQ)


ADA_TN = 1536


def _ada_kernel(cv_ref, w_ref, b_ref, o_ref):
    cv = cv_ref[...]
    s = cv * _sigmoid(cv)
    o_ref[...] = jnp.dot(s, w_ref[...], precision=HIGHEST, preferred_element_type=F32) + b_ref[...]


def _ada(cv, w_ada, b_ada):
    n = 6 * D_MODEL
    return pl.pallas_call(
        _ada_kernel,
        grid=(DEPTH, n // ADA_TN),
        in_specs=[
            pl.BlockSpec((COND_ROWS, D_MODEL), lambda l, j: (0, 0)),
            pl.BlockSpec((None, D_MODEL, ADA_TN), lambda l, j: (l, 0, j)),
            pl.BlockSpec((None, 1, ADA_TN), lambda l, j: (l, 0, j)),
        ],
        out_specs=pl.BlockSpec((None, COND_ROWS, ADA_TN), lambda l, j: (l, 0, j)),
        out_shape=jax.ShapeDtypeStruct((DEPTH, COND_ROWS, n), F32),
        compiler_params=_cparams(("arbitrary", "arbitrary")),
        name="ada_mod",
    )(cv, w_ada, b_ada.reshape(DEPTH, 1, n))


IN_TN = 768


def _norm_mod(x, g, shift, scale):
    ms = jnp.mean(x * x, axis=-1, keepdims=True)
    return (x * lax.rsqrt(ms + EPS) * g) * (1.0 + scale) + shift


def _inproj_body(x, mod_ref, g_ref, w_ref, z_ref):
    h = _norm_mod(x, g_ref[...], mod_ref[0:1, :], mod_ref[1:2, :]).astype(BF16)
    for c in range(IN_COLS // IN_TN):
        cols = slice(c * IN_TN, (c + 1) * IN_TN)
        z_ref[:, cols] = jnp.dot(h, w_ref[:, cols], preferred_element_type=F32).astype(BF16)


def _inproj_first_kernel(xc_ref, xl_ref, mod_ref, g_ref, w_ref, z_ref, xo_ref):
    i = pl.program_id(0)
    x = jnp.where(i < NB_CTX, xc_ref[...], xl_ref[...])
    xo_ref[...] = x
    _inproj_body(x, mod_ref, g_ref, w_ref, z_ref)


U32 = jnp.uint32
D_HALF = D_MODEL // 2
_HI_MASK = np.uint32(0xFFFF0000)


def _pack_bf16_pairs(x):
    xb = x.astype(BF16).astype(F32)
    lo = lax.bitcast_convert_type(xb[:, :D_HALF], U32) >> 16
    hi = lax.bitcast_convert_type(xb[:, D_HALF:], U32) & _HI_MASK
    return lo | hi


def _unpack_bf16_pairs(w):
    lo = lax.bitcast_convert_type(w << 16, F32).astype(BF16)
    hi = lax.bitcast_convert_type(w & _HI_MASK, F32).astype(BF16)
    return lo, hi


def _slot_onehot(route, slot):
    pos = route[:, 4 + slot:5 + slot].astype(jnp.int32)
    return lax.broadcasted_iota(jnp.int32, (route.shape[0], MOE_LC), 1) == pos


def _moe_residual(x_ref, ys_ref, r_ref, mod_ref):
    r = r_ref[...]
    sel = jnp.where(_slot_onehot(r, 0), r[:, 2:3], jnp.where(_slot_onehot(r, 1), r[:, 3:4], 0.0))
    sel = sel.astype(BF16)
    y = jnp.concatenate([jnp.dot(sel, half, preferred_element_type=F32)
                         for half in _unpack_bf16_pairs(ys_ref[...])], axis=-1)
    return x_ref[...] + mod_ref[5:6, :] * y


def _inproj_next_kernel(x_ref, ys_ref, r_ref, modp_ref, mod_ref, g_ref, w_ref, z_ref, xo_ref):
    x = _moe_residual(x_ref, ys_ref, r_ref, modp_ref)
    xo_ref[...] = x
    _inproj_body(x, mod_ref, g_ref, w_ref, z_ref)


def _tok_spec(cols):
    return pl.BlockSpec((TM, cols), lambda i: (i, 0))


def _mod_spec():
    return pl.BlockSpec((None, 6, D_MODEL), lambda i: (_cond_row(i), 0, 0))


def _full_spec(shape):
    return pl.BlockSpec(shape, lambda i: (0,) * len(shape))


def _ctx_lat_specs(cols):
    return [pl.BlockSpec((TM, cols), lambda i: (jnp.minimum(i, NB_CTX - 1), 0)),
            pl.BlockSpec((TM, cols), lambda i: (jnp.maximum(i - NB_CTX, 0), 0))]


def _inproj_first(x_ctx, x_lat, mod, g, w_bf16):
    return pl.pallas_call(
        _inproj_first_kernel,
        grid=(NB_ALL,),
        in_specs=_ctx_lat_specs(D_MODEL) + [_mod_spec(), _full_spec((1, D_MODEL)),
                                            _full_spec((D_MODEL, IN_COLS))],
        out_specs=[_tok_spec(IN_COLS), _tok_spec(D_MODEL)],
        out_shape=[jax.ShapeDtypeStruct((T_ALL, IN_COLS), BF16),
                   jax.ShapeDtypeStruct((T_ALL, D_MODEL), F32)],
        compiler_params=_cparams(("arbitrary",)),
        name="inproj_first",
    )(x_ctx, x_lat, mod, g, w_bf16)


def _inproj_next(x, ys, route, mod_prev, mod, g, w_bf16):
    return pl.pallas_call(
        _inproj_next_kernel,
        grid=(NB_ALL,),
        in_specs=[_tok_spec(D_MODEL),
                  pl.BlockSpec((MOE_LC, D_HALF), lambda i: (i, 0)),
                  _tok_spec(ROUTE_COLS),
                  _mod_spec(), _mod_spec(), _full_spec((1, D_MODEL)),
                  _full_spec((D_MODEL, IN_COLS))],
        out_specs=[_tok_spec(IN_COLS), _tok_spec(D_MODEL)],
        out_shape=[jax.ShapeDtypeStruct((T_ALL, IN_COLS), BF16),
                   jax.ShapeDtypeStruct((T_ALL, D_MODEL), F32)],
        compiler_params=_cparams(("arbitrary",)),
        name="inproj_next",
    )(x, ys, route, mod_prev, mod, g, w_bf16)


CONV_PAD = 16
CONV_CHUNK = 64


CONV_SPAN = CONV_CHUNK + 2 * CONV_PAD - SUBLANES


CONV_UNROLL = 4


def _conv_kernel(seq, z_ref, w_ref, b_ref, g_ref, be_ref, o_ref, upad_ref, shift_refs):
    zeros = jnp.zeros((CONV_PAD, CONV_CH), F32)
    upad_ref[0:CONV_PAD, :] = zeros
    upad_ref[seq + CONV_PAD:seq + 2 * CONV_PAD, :] = zeros

    def glu(ci, carry):
        base = pl.multiple_of(ci * 256, 256)
        zc = z_ref[pl.ds(base, 256), :].astype(F32)
        upad_ref[pl.ds(base + CONV_PAD, 256), :] = zc[:, :CONV_CH] * _sigmoid(zc[:, CONV_CH:])
        return carry

    lax.fori_loop(0, seq // 256, glu, 0)

    shift = CONV_PAD - CONV_K // 2

    def chunk(ci, shift_ref):
        base = pl.multiple_of(ci * CONV_CHUNK, CONV_CHUNK)
        win = upad_ref[pl.ds(base, CONV_CHUNK + 2 * CONV_PAD), :]
        acc = jnp.zeros((CONV_CHUNK, CONV_CH), F32)
        for sub in range(SUBLANES):
            shift_ref[sub] = win[sub:sub + CONV_SPAN, :]
            for k in range(CONV_K):
                if (k + shift) % SUBLANES == sub:
                    lo = k + shift - sub
                    acc = acc + w_ref[k:k + 1, :] * shift_ref[sub, lo:lo + CONV_CHUNK, :]
        acc = acc + b_ref[...]
        mu = jnp.mean(acc, axis=-1, keepdims=True)
        d = acc - mu
        var = jnp.mean(d * d, axis=-1, keepdims=True)
        n = d * lax.rsqrt(var + EPS) * g_ref[...] + be_ref[...]
        o_ref[pl.ds(base, CONV_CHUNK), :] = (n * _sigmoid(n)).astype(BF16)

    def chunks(cj, carry):
        for u in range(CONV_UNROLL):
            chunk(cj * CONV_UNROLL + u, shift_refs.at[u])
        return carry

    lax.fori_loop(0, seq // (CONV_CHUNK * CONV_UNROLL), chunks, 0)


def _conv(z, row_block0, nseq, seq, w, b, g, be):
    return pl.pallas_call(
        functools.partial(_conv_kernel, seq),
        grid=(nseq,),
        in_specs=[pl.BlockSpec((seq, 2 * CONV_CH), lambda s: (row_block0 + s, 0)),
                  _full_spec((CONV_K, CONV_CH)), _full_spec((1, CONV_CH)),
                  _full_spec((1, CONV_CH)), _full_spec((1, CONV_CH))],
        out_specs=pl.BlockSpec((seq, CONV_CH), lambda s: (s, 0)),
        out_shape=jax.ShapeDtypeStruct((nseq * seq, CONV_CH), BF16),
        scratch_shapes=[pltpu.VMEM((seq + 2 * CONV_PAD, CONV_CH), F32),
                        pltpu.VMEM((CONV_UNROLL, SUBLANES, CONV_SPAN, CONV_CH), F32)],
        compiler_params=_cparams(("arbitrary",)),
        name="conv_seq%d" % seq,
    )(z, w, b, g, be)


def _dot_nt(a, b):
    return lax.dot_general(a, b, (((1,), (1,)), ((), ())), preferred_element_type=F32)


NA_SCALE = NA_DIM ** -0.5
assert NA_SCALE == 2.0 ** round(np.log2(NA_SCALE)), "query pre-scaling assumes a power-of-two scale"


def _ctx_attn_kernel(layer, q_ref, k_ref, v_ref, *refs):
    if layer:
        kprev_ref, vprev_ref, o_ref, ko_ref, vo_ref = refs
    else:
        o_ref, ko_ref, vo_ref = refs
    for j in range(DEPTH):
        if j < layer:
            ko_ref[j] = kprev_ref[j]
            vo_ref[j] = vprev_ref[j]
        elif j > layer:
            ko_ref[j] = jnp.zeros(ko_ref.shape[1:], F32)
            vo_ref[j] = jnp.zeros(vo_ref.shape[1:], F32)
    outs = []
    for h in range(NA_HEADS):
        cols = slice(h * NA_DIM, (h + 1) * NA_DIM)
        qh, kh, vh = q_ref[:, cols], k_ref[:, cols], v_ref[:, cols]
        ko_ref[layer, h] = kh.astype(F32)
        vo_ref[layer, h] = vh.astype(F32)
        s = _dot_nt(qh, kh) * NA_SCALE
        m = jnp.max(s, axis=-1, keepdims=True)
        p = jnp.exp(s - m)
        den = jnp.sum(p, axis=-1, keepdims=True)
        o = jnp.dot(p.astype(BF16), vh, preferred_element_type=F32)
        outs.append(o / den)
    o_ref[...] = jnp.concatenate(outs, axis=-1).astype(BF16)


def _ctx_attn(z, layer, k_prev=None, v_prev=None):
    qb, kb, vb = COL_NA_Q // NA_WIDTH, COL_NA_K // NA_WIDTH, COL_NA_V // NA_WIDTH
    head_shape = jax.ShapeDtypeStruct((BATCH, DEPTH, NA_HEADS, SEQ, NA_DIM), F32)
    head_spec = pl.BlockSpec((None, DEPTH, NA_HEADS, SEQ, NA_DIM), lambda b: (b, 0, 0, 0, 0))
    in_specs = [pl.BlockSpec((SEQ, NA_WIDTH), lambda b: (b, qb)),
                pl.BlockSpec((SEQ, NA_WIDTH), lambda b: (b, kb)),
                pl.BlockSpec((SEQ, NA_WIDTH), lambda b: (b, vb))]
    args = [z, z, z]
    aliases = {}
    if layer:
        in_specs += [head_spec, head_spec]
        args += [k_prev, v_prev]
        aliases = {3: 1, 4: 2}
    return pl.pallas_call(
        functools.partial(_ctx_attn_kernel, layer),
        grid=(BATCH,),
        in_specs=in_specs,
        out_specs=[pl.BlockSpec((SEQ, NA_WIDTH), lambda b: (b, 0)), head_spec, head_spec],
        out_shape=[jax.ShapeDtypeStruct((T_CTX, NA_WIDTH), BF16), head_shape, head_shape],
        input_output_aliases=aliases,
        compiler_params=_cparams(("arbitrary",)),
        name="ctx_attn",
    )(*args)


NA_KINDS = (0, NA_ROWS, GRID_H - NA_ROWS)
N_DR = 2 * NA_KH - 1
N_DC = 2 * NA_KW - 1


def _na_row_offset(r0, i, j):
    ks = min(max(r0 - NA_KH // 2, 0), GRID_H - NA_KROWS)
    r, kr = r0 + i, ks + j
    rs = min(max(r - NA_KH // 2, 0), GRID_H - NA_KH)
    return kr - r + NA_KH - 1 if rs <= kr < rs + NA_KH else None


def _na_bias_kernel(rpb_ref, o_ref):
    lh = pl.program_id(0)
    shape = (GRID_W, 2 * GRID_W)
    qc = lax.broadcasted_iota(jnp.int32, shape, 0)
    lane = lax.broadcasted_iota(jnp.int32, shape, 1)
    kc = lane % GRID_W
    dc = jnp.clip(kc - qc, -(NA_KW - 1), NA_KW - 1) + NA_KW - 1
    cs = jnp.clip(qc - NA_KW // 2, 0, GRID_W - NA_KW)
    col_ok = (kc >= cs) & (kc < cs + NA_KW)
    neg = jnp.full(shape, NEG_INF, F32)
    tiles = []
    for dr in range(N_DR):
        base = (lh * N_DR + dr) * N_DC
        val = jnp.zeros(shape, F32)
        for d in range(N_DC):
            val = jnp.where(dc == d, rpb_ref[base + d], val)
        tiles.append(jnp.where(col_ok, val, neg))
    left = lane < GRID_W
    for kind, r0 in enumerate(NA_KINDS):
        for i in range(NA_ROWS):
            for jp in range(NA_KROWS // 2):
                dl, dr_ = _na_row_offset(r0, i, 2 * jp), _na_row_offset(r0, i, 2 * jp + 1)
                tl = neg if dl is None else tiles[dl]
                tr = neg if dr_ is None else tiles[dr_]
                o_ref[kind, i * GRID_W:(i + 1) * GRID_W, jp * 2 * GRID_W:(jp + 1) * 2 * GRID_W] = (
                    jnp.where(left, tl, tr))


def _na_bias_tables(rpb):
    return pl.pallas_call(
        _na_bias_kernel,
        grid=(DEPTH * NA_HEADS,),
        in_specs=[pl.BlockSpec(memory_space=pltpu.SMEM)],
        out_specs=pl.BlockSpec((None, len(NA_KINDS), NA_Q, NA_KEYS), lambda i: (i, 0, 0, 0)),
        out_shape=jax.ShapeDtypeStruct((DEPTH * NA_HEADS, len(NA_KINDS), NA_Q, NA_KEYS), F32),
        compiler_params=_cparams(("arbitrary",)),
        name="nbr_bias",
    )(rpb.reshape(-1))


NA_G = 4


def _na_kernel(q_ref, k_ref, v_ref, kc_ref, vc_ref, bias_ref, o_ref):
    rb = pl.program_id(2)
    ks = jnp.clip(rb * NA_ROWS - NA_KH // 2, 0, GRID_H - NA_KROWS)
    start = pl.multiple_of(ks * GRID_W, GRID_W)
    q = q_ref[...] * NA_SCALE
    kl = k_ref[pl.ds(start, NA_KEYS), :]
    vl = v_ref[pl.ds(start, NA_KEYS), :]
    ones_loc = jnp.ones((NA_KEYS, NA_DIM), BF16)
    ones_ctx = jnp.ones((PAST_LEN, NA_DIM), BF16)

    def scores(hh):
        cols = slice(hh * NA_DIM, (hh + 1) * NA_DIM)
        qh = q[:, cols]
        return _dot_nt(qh, kl[:, cols]) + bias_ref[hh], _dot_nt(qh, kc_ref[hh].astype(BF16))

    outs = []
    nxt = scores(0)
    for hh in range(NA_G):
        s_loc, s_ctx = nxt
        if hh + 1 < NA_G:
            nxt = scores(hh + 1)
        cols = slice(hh * NA_DIM, (hh + 1) * NA_DIM)
        m = jnp.maximum(jnp.max(s_loc, axis=-1, keepdims=True), jnp.max(s_ctx, axis=-1, keepdims=True))
        p_loc = jnp.exp(s_loc - m).astype(BF16)
        p_ctx = jnp.exp(s_ctx - m).astype(BF16)
        v_ext = jnp.concatenate([vl[:, cols], ones_loc], axis=-1)
        vc_ext = jnp.concatenate([vc_ref[hh].astype(BF16), ones_ctx], axis=-1)
        o = (jnp.dot(p_loc, v_ext, preferred_element_type=F32)
             + jnp.dot(p_ctx, vc_ext, preferred_element_type=F32))
        outs.append(o[:, :NA_DIM] / o[:, NA_DIM:])
    o_ref[...] = jnp.concatenate(outs, axis=-1).astype(BF16)


def _na_attn(z, cache_k, cache_v, bias, layer):
    lat_q0 = T_CTX // NA_Q
    lat_s0 = T_CTX // DEC_SEQ
    width = NA_G * NA_DIM
    qc, kc, vc = COL_NA_Q // width, COL_NA_K // width, COL_NA_V // width
    groups = NA_HEADS // NA_G

    def kind(rb):
        return jnp.where(rb == 0, 0, jnp.where(rb == NA_RB - 1, 2, 1))

    ctx_spec = pl.BlockSpec((None, None, NA_G, PAST_LEN, NA_DIM), lambda b, hg, rb: (b, layer, hg, 0, 0))
    return pl.pallas_call(
        _na_kernel,
        grid=(DEC_BATCH, groups, NA_RB),
        in_specs=[pl.BlockSpec((NA_Q, width), lambda b, hg, rb: (lat_q0 + b * NA_RB + rb, qc + hg)),
                  pl.BlockSpec((DEC_SEQ, width), lambda b, hg, rb: (lat_s0 + b, kc + hg)),
                  pl.BlockSpec((DEC_SEQ, width), lambda b, hg, rb: (lat_s0 + b, vc + hg)),
                  ctx_spec, ctx_spec,
                  pl.BlockSpec((NA_G, None, NA_Q, NA_KEYS),
                               lambda b, hg, rb: (layer * groups + hg, kind(rb), 0, 0))],
        out_specs=pl.BlockSpec((NA_Q, width), lambda b, hg, rb: (b * NA_RB + rb, hg)),
        out_shape=jax.ShapeDtypeStruct((T_LAT, NA_WIDTH), BF16),
        compiler_params=_cparams(("arbitrary", "arbitrary", "arbitrary")),
        name="nbr_attn",
    )(z, z, z, cache_k, cache_v, bias)


RET_PAIR = 2 * RET_DIM
RET_NPAIR = RET_HEADS // 2
assert RET_PAIR == LANES and RET_CHUNK == LANES
RET_UNROLL = 8


def _rope_tables():
    n_freq = RET_DIM // 4
    t = np.arange(DEC_SEQ)
    inv = jnp.asarray(ROPE_BASE, F32) ** (-jnp.arange(n_freq, dtype=F32) / n_freq)
    ang_r = jnp.asarray(t // GRID_W, F32)[:, None] * inv[None, :]
    ang_c = jnp.asarray(t % GRID_W, F32)[:, None] * inv[None, :]
    cos = jnp.concatenate([jnp.cos(ang_r)] * 2 + [jnp.cos(ang_c)] * 2, axis=-1)
    sin = jnp.concatenate([-jnp.sin(ang_r), jnp.sin(ang_r), -jnp.sin(ang_c), jnp.sin(ang_c)], axis=-1)
    lane = np.arange(RET_WIDTH)
    src = np.where(lane % (2 * n_freq) < n_freq, lane + n_freq, lane - n_freq)
    swap = np.zeros((RET_WIDTH, RET_WIDTH), np.float32)
    swap[src, lane] = 1.0
    return jnp.tile(cos, (1, RET_HEADS)), jnp.tile(sin, (1, RET_HEADS)), jnp.asarray(swap, BF16)


def _ret_kernel(seq, latent, *refs):
    if latent:
        (lg_ref, z_ref, gn_ref, cos_ref, sin_ref, swap_ref, s0f_ref, s0b_ref, y_ref,
         q_s, k_s, kv_s, st_s) = refs
    else:
        lg_ref, z_ref, gn_ref, y_ref, sf_ref, sb_ref, q_s, k_s, kv_s, st_s = refs
    nc = seq // RET_CHUNK
    ch, hd, pw = RET_CHUNK, RET_DIM, RET_PAIR

    row = lax.broadcasted_iota(jnp.int32, (ch, ch), 0).astype(F32)
    col = lax.broadcasted_iota(jnp.int32, (ch, ch), 1).astype(F32)
    pos = lax.broadcasted_iota(jnp.int32, (ch, pw), 0).astype(F32)
    left = lax.broadcasted_iota(jnp.int32, (ch, pw), 1) < hd
    top = lax.broadcasted_iota(jnp.int32, (pw, pw), 0) < hd
    same_head = top == (lax.broadcasted_iota(jnp.int32, (pw, pw), 1) < hd)
    same_head2 = jnp.concatenate([same_head, same_head], axis=0)

    def per_head(mask, fn, p):
        return jnp.where(mask, fn(2 * p), fn(2 * p + 1))

    decay = []
    for h in range(RET_HEADS):
        lf, lb = lg_ref[0, h], lg_ref[1, h]
        d_f = jnp.where(row >= col, jnp.exp(jnp.maximum(row - col, 0.0) * lf), 0.0)
        d_b = jnp.where(col >= row, jnp.exp(jnp.maximum(col - row, 0.0) * lb), 0.0)
        decay.append(d_f + d_b)
    q_dec, k_dec, c_dec_f, c_dec_b = [], [], [], []
    for p in range(RET_NPAIR):
        q_dec.append(jnp.concatenate(
            [per_head(left, lambda h: jnp.exp((pos + 1.0) * lg_ref[0, h]), p),
             per_head(left, lambda h: jnp.exp((ch - pos) * lg_ref[1, h]), p)], axis=-1))
        k_dec.append(jnp.concatenate(
            [per_head(left, lambda h: jnp.exp((ch - 1.0 - pos) * lg_ref[0, h]), p),
             per_head(left, lambda h: jnp.exp(pos * lg_ref[1, h]), p)], axis=-1))
        zero = jnp.zeros((pw, pw), F32)
        c_dec_f.append(per_head(top, lambda h: jnp.exp(zero + ch * lg_ref[0, h]), p))
        c_dec_b.append(per_head(top, lambda h: jnp.exp(zero + ch * lg_ref[1, h]), p))

    def rope(x, base):
        xf = x.astype(F32)
        if not latent:
            return xf
        swapped = jnp.dot(x, swap_ref[...], preferred_element_type=F32)
        return xf * cos_ref[pl.ds(base, ch), :] + swapped * sin_ref[pl.ds(base, ch), :]

    def pass1(n, carry):
        base = pl.multiple_of(n * ch, ch)
        zc = z_ref[pl.ds(base, ch), :]
        q = rope(zc[:, 0:RET_WIDTH], base)
        k = rope(zc[:, RET_WIDTH:2 * RET_WIDTH], base) * (RET_DIM ** -0.5)
        q_s[pl.ds(base, ch), :] = q.astype(BF16)
        k_s[pl.ds(base, ch), :] = k.astype(BF16)
        v = zc[:, 2 * RET_WIDTH:3 * RET_WIDTH]
        for p in range(RET_NPAIR):
            lanes = slice(p * pw, (p + 1) * pw)
            kp = k[:, lanes]
            k2 = (jnp.concatenate([kp, kp], axis=-1) * k_dec[p]).astype(BF16)
            kv = lax.dot_general(k2, v[:, lanes], (((0,), (0,)), ((), ())), preferred_element_type=F32)
            kv_s[n, p] = jnp.where(same_head2, kv, 0.0)
        return carry

    lax.fori_loop(0, nc, pass1, 0, unroll=min(RET_UNROLL, nc))

    def block_diag(a, b):
        z = jnp.zeros((hd, hd), F32)
        return jnp.concatenate([jnp.concatenate([a, z], axis=1), jnp.concatenate([z, b], axis=1)], axis=0)

    for p in range(RET_NPAIR):
        if latent:
            s_f = block_diag(s0f_ref[2 * p], s0f_ref[2 * p + 1])
            s_b = block_diag(s0b_ref[2 * p], s0b_ref[2 * p + 1])
        else:
            s_f = s_b = jnp.zeros((pw, pw), F32)

        def fwd(n, s, p=p):
            st_s[n, p, 0:pw, :] = s.astype(BF16)
            return c_dec_f[p] * s + kv_s[n, p, 0:pw, :]

        def bwd(i, s, p=p):
            n = nc - 1 - i
            st_s[n, p, pw:2 * pw, :] = s.astype(BF16)
            return c_dec_b[p] * s + kv_s[n, p, pw:2 * pw, :]

        s_f = lax.fori_loop(0, nc, fwd, s_f)
        s_b = lax.fori_loop(0, nc, bwd, s_b)
        if not latent:
            for hh in range(2):
                blk = slice(hh * hd, (hh + 1) * hd)
                sf_ref[2 * p + hh] = s_f[blk, blk]
                sb_ref[2 * p + hh] = s_b[blk, blk]

    def pass3(n, carry):
        base = pl.multiple_of(n * ch, ch)
        zc = z_ref[pl.ds(base, ch), :]
        q = q_s[pl.ds(base, ch), :]
        k = k_s[pl.ds(base, ch), :]
        v = zc[:, 2 * RET_WIDTH:3 * RET_WIDTH]
        gate = zc[:, 3 * RET_WIDTH:4 * RET_WIDTH].astype(F32)
        outs = []
        for p in range(RET_NPAIR):
            lanes = slice(p * pw, (p + 1) * pw)
            qp, kp, vp = q[:, lanes], k[:, lanes], v[:, lanes]
            o_h = []
            for hh in range(2):
                qm = jnp.where(left == (hh == 0), qp, jnp.zeros_like(qp))
                s = _dot_nt(qm, kp) * decay[2 * p + hh]
                o_h.append(jnp.dot(s.astype(BF16), vp, preferred_element_type=F32))
            qf = qp.astype(F32)
            q2 = (jnp.concatenate([qf, qf], axis=-1) * q_dec[p]).astype(BF16)
            o = jnp.where(left, o_h[0], o_h[1]) + jnp.dot(q2, st_s[n, p], preferred_element_type=F32)

            def half_mean(t):
                s_l = jnp.sum(jnp.where(left, t, 0.0), axis=-1, keepdims=True)
                s_r = jnp.sum(jnp.where(left, 0.0, t), axis=-1, keepdims=True)
                return jnp.where(left, s_l, s_r) * (1.0 / hd)

            d = o - half_mean(o)
            outs.append(d * lax.rsqrt(half_mean(d * d) + EPS))
        nrm = jnp.concatenate(outs, axis=-1)
        y_ref[pl.ds(base, ch), :] = (nrm * gn_ref[...] * (gate * _sigmoid(gate))).astype(BF16)
        return carry

    lax.fori_loop(0, nc, pass3, 0, unroll=min(RET_UNROLL, nc))


def _retention(z, lg, gn_g, latent, layer=None, rope=None, s0_f=None, s0_b=None):
    seq = DEC_SEQ if latent else SEQ
    nseq = DEC_BATCH if latent else BATCH
    nc = seq // RET_CHUNK
    row0 = (T_CTX // DEC_SEQ) if latent else 0
    cb = COL_RET // (4 * RET_WIDTH)
    in_specs = [pl.BlockSpec(memory_space=pltpu.SMEM),
                pl.BlockSpec((seq, 4 * RET_WIDTH), lambda s: (row0 + s, cb)),
                _full_spec((1, RET_WIDTH))]
    args = [lg, z, gn_g]
    state_shape = jax.ShapeDtypeStruct((nseq, RET_HEADS, RET_DIM, RET_DIM), F32)
    y_spec = pl.BlockSpec((seq, RET_WIDTH), lambda s: (s, 0))
    y_shape = jax.ShapeDtypeStruct((nseq * seq, RET_WIDTH), BF16)
    if latent:
        st_spec = pl.BlockSpec((None, None, RET_HEADS, RET_DIM, RET_DIM), lambda s: (s, layer, 0, 0, 0))

        def const_spec(shape):
            return pl.BlockSpec(shape, lambda s: (0,) * len(shape), pipeline_mode=pl.Buffered(1))

        in_specs += [const_spec((seq, RET_WIDTH)), const_spec((seq, RET_WIDTH)),
                     const_spec((RET_WIDTH, RET_WIDTH)), st_spec, st_spec]
        args += [rope[0], rope[1], rope[2], s0_f, s0_b]
        out_specs, out_shape = y_spec, y_shape
    else:
        so_spec = pl.BlockSpec((None, RET_HEADS, RET_DIM, RET_DIM), lambda s: (s, 0, 0, 0))
        out_specs, out_shape = [y_spec, so_spec, so_spec], [y_shape, state_shape, state_shape]
    return pl.pallas_call(
        functools.partial(_ret_kernel, seq, latent),
        grid=(nseq,),
        in_specs=in_specs,
        out_specs=out_specs,
        out_shape=out_shape,
        scratch_shapes=[pltpu.VMEM((seq, RET_WIDTH), BF16), pltpu.VMEM((seq, RET_WIDTH), BF16),
                        pltpu.VMEM((nc, RET_NPAIR, 2 * RET_PAIR, RET_PAIR), F32),
                        pltpu.VMEM((nc, RET_NPAIR, 2 * RET_PAIR, RET_PAIR), BF16)],
        compiler_params=_cparams(("arbitrary",)),
        name="retention_lat" if latent else "retention_ctx",
    )(*args)


def _route(logits):
    lane = lax.broadcasted_iota(jnp.int32, logits.shape, 1)
    lane_f = lane.astype(F32)
    big = float(ROUTE_COLS)
    neg = -jnp.inf
    is_grp = lane < N_GROUPS
    gl = jnp.where(is_grp, logits, neg)
    gmax = jnp.max(gl, axis=-1, keepdims=True)
    grp = jnp.min(jnp.where(gl == gmax, lane_f, big), axis=-1, keepdims=True)
    p_grp = 1.0 / jnp.sum(jnp.exp(gl - gmax), axis=-1, keepdims=True)
    e_f = lane_f - N_GROUPS
    lo = grp * EXPERTS_PER_GROUP
    in_grp = (e_f >= lo) & (e_f < lo + EXPERTS_PER_GROUP)
    el = jnp.where(in_grp, logits, neg)
    m1 = jnp.max(el, axis=-1, keepdims=True)
    i1 = jnp.min(jnp.where(el == m1, lane_f, big), axis=-1, keepdims=True)
    el2 = jnp.where(lane_f == i1, neg, el)
    m2 = jnp.max(el2, axis=-1, keepdims=True)
    i2 = jnp.min(jnp.where(el2 == m2, lane_f, big), axis=-1, keepdims=True)
    t = jnp.exp(m2 - m1)
    g1 = p_grp / (1.0 + t)
    g2 = p_grp * t / (1.0 + t)
    rows = logits.shape[0]
    oh1, oh2 = lane_f == i1, lane_f == i2
    oh = jnp.where(oh1 | oh2, 1.0, 0.0)
    tri = (lax.broadcasted_iota(jnp.int32, (rows, rows), 0)
           > lax.broadcasted_iota(jnp.int32, (rows, rows), 1))
    rank = jnp.dot(jnp.where(tri, 1.0, 0.0).astype(BF16), oh.astype(BF16), preferred_element_type=F32)
    tiles = jnp.floor((jnp.sum(oh, axis=0, keepdims=True) + (SUBLANES - 1)) * (1.0 / SUBLANES))
    upper = (lax.broadcasted_iota(jnp.int32, (ROUTE_COLS, ROUTE_COLS), 0)
             < lax.broadcasted_iota(jnp.int32, (ROUTE_COLS, ROUTE_COLS), 1))
    start = SUBLANES * jnp.dot(jnp.broadcast_to(tiles, (SUBLANES, ROUTE_COLS)).astype(BF16),
                               jnp.where(upper, 1.0, 0.0).astype(BF16),
                               preferred_element_type=F32)[0:1, :]
    pos = start + rank
    p1 = jnp.sum(jnp.where(oh1, pos, 0.0), axis=-1, keepdims=True)
    p2 = jnp.sum(jnp.where(oh2, pos, 0.0), axis=-1, keepdims=True)
    out = jnp.zeros(logits.shape, F32)
    for k, val in enumerate((i1 - N_GROUPS, i2 - N_GROUPS, g1, g2, p1, p2)):
        out = jnp.where(lane == k, val, out)
    return out, SUBLANES * tiles


def _outproj_kernel(ycc, ycl, ync, ynl, yrc, yrl, x_ref, mod_ref, g_ref, w_ref, wr_ref, br_ref,
                    xo_ref, xs_ref, r_ref, seg_ref):
    is_ctx = pl.program_id(0) < NB_CTX
    yc = jnp.where(is_ctx, ycc[...], ycl[...])
    yn = jnp.where(is_ctx, ync[...], ynl[...])
    yr = jnp.where(is_ctx, yrc[...], yrl[...])
    y = (jnp.dot(yc, w_ref[0:CONV_CH, :], preferred_element_type=F32)
         + jnp.dot(yn, w_ref[CONV_CH:CONV_CH + NA_WIDTH, :], preferred_element_type=F32)
         + jnp.dot(yr, w_ref[CONV_CH + NA_WIDTH:, :], preferred_element_type=F32))
    x = x_ref[...] + mod_ref[2:3, :] * y
    xo_ref[...] = x
    h = _norm_mod(x, g_ref[...], mod_ref[3:4, :], mod_ref[4:5, :])
    h_hi = h.astype(BF16)
    h_lo = (h - h_hi.astype(F32)).astype(BF16)
    hw = jnp.dot(h_hi, wr_ref[...], preferred_element_type=F32)
    logits = (hw[:, :ROUTE_COLS] + hw[:, ROUTE_COLS:]
              + jnp.dot(h_lo, wr_ref[:, :ROUTE_COLS], preferred_element_type=F32) + br_ref[...])
    route, seg = _route(logits)
    r_ref[...] = route
    seg_ref[...] = jnp.broadcast_to(seg, seg_ref.shape)
    sel = _slot_onehot(route, 0) | _slot_onehot(route, 1)
    xs_ref[...] = _pack_bf16_pairs(lax.dot_general(jnp.where(sel, 1.0, 0.0).astype(BF16), h_hi,
                                                   (((0,), (0,)), ((), ())), preferred_element_type=F32))


def _outproj(y_conv, y_na, y_ret, x, mod, g, w_bf16, w_route, b_route):
    return pl.pallas_call(
        _outproj_kernel,
        grid=(NB_ALL,),
        in_specs=(_ctx_lat_specs(CONV_CH) + _ctx_lat_specs(NA_WIDTH) + _ctx_lat_specs(RET_WIDTH)
                  + [_tok_spec(D_MODEL), _mod_spec(), _full_spec((1, D_MODEL)),
                     _full_spec((D_MODEL, D_MODEL)), _full_spec((D_MODEL, 2 * ROUTE_COLS)),
                     _full_spec((1, ROUTE_COLS))]),
        out_specs=[_tok_spec(D_MODEL), pl.BlockSpec((MOE_LC, D_HALF), lambda i: (i, 0)),
                   _tok_spec(ROUTE_COLS), pl.BlockSpec((None, SUBLANES, ROUTE_COLS), lambda i: (i, 0, 0))],
        out_shape=[jax.ShapeDtypeStruct((T_ALL, D_MODEL), F32),
                   jax.ShapeDtypeStruct((NB_ALL * MOE_LC, D_HALF), U32),
                   jax.ShapeDtypeStruct((T_ALL, ROUTE_COLS), F32),
                   jax.ShapeDtypeStruct((NB_ALL, SUBLANES, ROUTE_COLS), F32)],
        compiler_params=_cparams(("arbitrary",)),
        name="outproj_route",
    )(y_conv[0], y_conv[1], y_na[0], y_na[1], y_ret[0], y_ret[1], x, mod, g, w_bf16, w_route, b_route)


def _dispatch_tables(seg):
    seg_len = seg[:, 0, N_GROUPS:N_GROUPS + N_EXPERTS].astype(jnp.int32)
    experts = jnp.arange(N_EXPERTS, dtype=jnp.int32)
    in_chunk = jnp.cumsum(seg_len, axis=1) - seg_len
    seg_row = in_chunk + MOE_LC * jnp.arange(NB_ALL, dtype=jnp.int32)[:, None]
    seg_off = jnp.cumsum(seg_len, axis=0) - seg_len
    rows_e = jnp.sum(seg_len, axis=0)
    chunk_rows = jnp.sum(seg_len, axis=1)
    nblk = (rows_e + MOE_BLK - 1) // MOE_BLK
    blk_end = jnp.cumsum(nblk)
    blk_start = blk_end - nblk
    blk = jnp.arange(MOE_NBLK, dtype=jnp.int32)
    n_active = blk_end[-1]
    blk_e = jnp.minimum(jnp.sum((blk_end[None, :] <= jnp.minimum(blk, n_active - 1)[:, None]).astype(jnp.int32),
                                axis=-1), N_EXPERTS - 1)
    mine = blk_e[:, None] == experts[None, :]
    blk_lo = (blk - jnp.sum(jnp.where(mine, blk_start[None, :], 0), axis=-1)) * MOE_BLK
    left = jnp.sum(jnp.where(mine, rows_e[None, :], 0), axis=-1) - blk_lo
    blk_nv = jnp.where(blk < n_active, jnp.clip(left, 0, MOE_BLK), 0).astype(jnp.int32)
    off_b = jnp.sum(jnp.where(mine[:, None, :], seg_off[None, :, :], 0), axis=-1)
    end_b = off_b + jnp.sum(jnp.where(mine[:, None, :], seg_len[None, :, :], 0), axis=-1)
    blk_c0 = jnp.sum((end_b <= blk_lo[:, None]).astype(jnp.int32), axis=-1)
    blk_c1 = jnp.sum((off_b < (blk_lo + blk_nv)[:, None]).astype(jnp.int32), axis=-1)
    after = jnp.sum(jnp.where(mine, blk_end[None, :], 0), axis=-1)
    blk_next_e = jnp.where(after < n_active, jnp.take(blk_e, jnp.minimum(after, MOE_NBLK - 1)), -1)
    row_b = jnp.sum(jnp.where(mine[:, None, :], seg_row[None, :, :], 0), axis=-1)
    first = jnp.maximum(off_b, blk_lo[:, None])
    piece_n = jnp.minimum(end_b, (blk_lo + blk_nv)[:, None]) - first
    piece_src = row_b + first - off_b
    piece_dst = first - blk_lo[:, None]
    return (blk_e, blk_next_e.astype(jnp.int32), blk_nv, blk_c0, blk_c1, piece_src.reshape(-1),
            piece_dst.reshape(-1), piece_n.reshape(-1), chunk_rows)


def _moe_kernel(layer, blk_e, blk_next_e, blk_nv, blk_c0, blk_c1, piece_src, piece_dst, piece_n,
                chunk_rows, xs_hbm, w1_hbm, w3_hbm, w2_hbm, ys_hbm, xbuf, obuf, zeros,
                w1f, w3f, w2f, w1b, w3b, w2b, gsem, ssem, zsem, wsem):
    i = pl.program_id(0)
    last = pl.num_programs(0) - 1
    slot = i % 2

    def tiles(v):
        return pl.multiple_of(v, SUBLANES)

    def for_segments(blk, fn):
        def body(c, carry):
            k = blk * NB_ALL + c
            n = piece_n[k]

            @pl.when(n > 0)
            def _():
                fn(tiles(piece_src[k]), tiles(piece_dst[k]), tiles(n))

            return carry

        lax.fori_loop(blk_c0[blk], blk_c1[blk], body, 0)

    def weight_copies(e):
        return [pltpu.make_async_copy(src.at[layer, e], dst, wsem)
                for src, dst in ((w1_hbm, w1f), (w3_hbm, w3f), (w2_hbm, w2f))]

    def start_gathers(blk, s):
        for_segments(blk, lambda src, dst, n: pltpu.make_async_copy(
            xs_hbm.at[pl.ds(src, n)], xbuf.at[s, pl.ds(dst, n)], gsem.at[s]).start())

    def start_scatters(blk, s):
        for_segments(blk, lambda dst, src, n: pltpu.make_async_copy(
            obuf.at[s, pl.ds(src, n)], ys_hbm.at[pl.ds(dst, n)], ssem.at[s]).start())

    def wait_rows(blk, s, sem):
        n = tiles(blk_nv[blk])

        @pl.when(n > 0)
        def _():
            pltpu.make_async_copy(xs_hbm.at[pl.ds(0, n)], xbuf.at[s, pl.ds(0, n)], sem.at[s]).wait()

    @pl.when(i == 0)
    def _():
        xbuf[...] = jnp.zeros_like(xbuf)
        zeros[...] = jnp.zeros_like(zeros)

        def tail(c):
            n = tiles(MOE_LC - chunk_rows[c])
            return n, pltpu.make_async_copy(zeros.at[pl.ds(0, n)],
                                            ys_hbm.at[pl.ds(tiles(c * MOE_LC + chunk_rows[c]), n)], zsem)

        def fill(c, carry):
            n, copy = tail(c)
            pl.when(n > 0)(copy.start)
            return carry

        def drain(c, carry):
            n, copy = tail(c)
            pl.when(n > 0)(copy.wait)
            return carry

        lax.fori_loop(0, NB_ALL, fill, 0)
        lax.fori_loop(0, NB_ALL, drain, 0)
        start_gathers(0, 0)
        for copy in weight_copies(blk_e[0]):
            copy.start()

    @pl.when(i < last)
    def _():
        start_gathers(i + 1, 1 - slot)

    @pl.when(i >= 2)
    def _():
        wait_rows(i - 2, slot, ssem)

    @pl.when(blk_nv[i] > 0)
    def _():
        @pl.when((i == 0) | (blk_e[i] != blk_e[jnp.maximum(i - 1, 0)]))
        def _():
            for copy in weight_copies(blk_e[i]):
                copy.wait()
            w1b[...] = w1f[...].astype(BF16)
            w3b[...] = w3f[...].astype(BF16)
            w2b[...] = w2f[...].astype(BF16)

            @pl.when(blk_next_e[i] >= 0)
            def _():
                for copy in weight_copies(blk_next_e[i]):
                    copy.start()

        wait_rows(i, slot, gsem)
        x_lo, x_hi = _unpack_bf16_pairs(xbuf[slot])

        def in_dot(w):
            return (jnp.dot(x_lo, w[:D_HALF, :], preferred_element_type=F32)
                    + jnp.dot(x_hi, w[D_HALF:, :], preferred_element_type=F32))

        a = in_dot(w1b)
        b = in_dot(w3b)
        mid = (a * _sigmoid(a) * b).astype(BF16)
        obuf[slot] = _pack_bf16_pairs(jnp.dot(mid, w2b[...], preferred_element_type=F32))
        start_scatters(i, slot)

    @pl.when(i == last)
    def _():
        wait_rows(i - 1, 1 - slot, ssem)
        wait_rows(i, slot, ssem)


def _moe(xs, w1, w3, w2, layer, blk_e, blk_next_e, blk_nv, blk_c0, blk_c1, piece_src, piece_dst,
         piece_n, chunk_rows):
    any_spec = pl.BlockSpec(memory_space=pl.ANY)
    grid_spec = pltpu.PrefetchScalarGridSpec(
        num_scalar_prefetch=9,
        grid=(MOE_NBLK,),
        in_specs=[any_spec, any_spec, any_spec, any_spec],
        out_specs=any_spec,
        scratch_shapes=[pltpu.VMEM((2, MOE_BLK, D_HALF), U32), pltpu.VMEM((2, MOE_BLK, D_HALF), U32),
                        pltpu.VMEM((MOE_LC - 2 * TM, D_HALF), U32),
                        pltpu.VMEM((D_MODEL, D_EXPERT), F32), pltpu.VMEM((D_MODEL, D_EXPERT), F32),
                        pltpu.VMEM((D_EXPERT, D_MODEL), F32),
                        pltpu.VMEM((D_MODEL, D_EXPERT), BF16), pltpu.VMEM((D_MODEL, D_EXPERT), BF16),
                        pltpu.VMEM((D_EXPERT, D_MODEL), BF16),
                        pltpu.SemaphoreType.DMA((2,)), pltpu.SemaphoreType.DMA((2,)),
                        pltpu.SemaphoreType.DMA, pltpu.SemaphoreType.DMA])
    return pl.pallas_call(
        functools.partial(_moe_kernel, layer),
        grid_spec=grid_spec,
        out_shape=jax.ShapeDtypeStruct((NB_ALL * MOE_LC, D_HALF), U32),
        compiler_params=_cparams(("arbitrary",)),
        name="moe_experts",
    )(blk_e, blk_next_e, blk_nv, blk_c0, blk_c1, piece_src, piece_dst, piece_n, chunk_rows,
      xs, w1, w3, w2)


def _final_kernel(x_ref, ys_ref, r_ref, mod_ref, g_ref, o_ref):
    x = _moe_residual(x_ref, ys_ref, r_ref, mod_ref)
    ms = jnp.mean(x * x, axis=-1, keepdims=True)
    o_ref[...] = x * lax.rsqrt(ms + EPS) * g_ref[...]


def _final(x, ys, route, mod, g, block0, nblocks):
    return pl.pallas_call(
        _final_kernel,
        grid=(nblocks,),
        in_specs=[pl.BlockSpec((TM, D_MODEL), lambda i: (block0 + i, 0)),
                  pl.BlockSpec((MOE_LC, D_HALF), lambda i: (block0 + i, 0)),
                  pl.BlockSpec((TM, ROUTE_COLS), lambda i: (block0 + i, 0)),
                  pl.BlockSpec((None, 6, D_MODEL), lambda i: (_cond_row(block0 + i), 0, 0)),
                  _full_spec((1, D_MODEL))],
        out_specs=_tok_spec(D_MODEL),
        out_shape=jax.ShapeDtypeStruct((nblocks * TM, D_MODEL), F32),
        compiler_params=_cparams(("arbitrary",)),
        name="final_norm",
    )(x, ys, route, mod, g)


def kernel(x_prompt, x_sample, c, cache_k, cache_v, state_ret_f, state_ret_b, c_ctx, w_ada, b_ada, norm1_g, norm2_g, w_in, w_out, conv_w, conv_b, conv_ln_g, conv_ln_b, na_rpb, ret_lg_f, ret_lg_b, ret_gn_g, w_route_g, b_route_g, w_route_e, b_route_e, w1, w3, w2, final_g):
    cv = jnp.zeros((COND_ROWS, D_MODEL), F32).at[0].set(c_ctx).at[1:N_COND].set(c)
    mods = _ada(cv, w_ada, b_ada).reshape(DEPTH, COND_ROWS, 6, D_MODEL)
    w_in_b = w_in.astype(BF16)
    w_out_b = w_out.astype(BF16)
    pad = ROUTE_COLS - N_GROUPS - N_EXPERTS
    w_route = jnp.pad(jnp.concatenate([w_route_g, w_route_e], axis=-1), ((0, 0), (0, 0), (0, pad)))
    b_route = jnp.pad(jnp.concatenate([b_route_g, b_route_e], axis=-1), ((0, 0), (0, pad)))
    w_route_hi = w_route.astype(BF16)
    w_route_lo = (w_route - w_route_hi.astype(F32)).astype(BF16)
    w_route = jnp.concatenate([w_route_hi, w_route_lo], axis=-1)
    na_bias = _na_bias_tables(na_rpb)
    rope = _rope_tables()
    lg = jnp.stack([ret_lg_f, ret_lg_b], axis=1)

    x_ctx = x_prompt.reshape(T_CTX, D_MODEL)
    x_lat = x_sample.reshape(T_LAT, D_MODEL)
    x = y = route = new_k = new_v = None
    sf_list, sb_list = [], []
    for l in range(DEPTH):
        g1 = norm1_g[l].reshape(1, D_MODEL)
        if l == 0:
            z, x = _inproj_first(x_ctx, x_lat, mods[l], g1, w_in_b[l])
        else:
            z, x = _inproj_next(x, y, route, mods[l - 1], mods[l], g1, w_in_b[l])
        conv_args = (conv_w[l], conv_b[l].reshape(1, -1), conv_ln_g[l].reshape(1, -1),
                     conv_ln_b[l].reshape(1, -1))
        yc_c = _conv(z, 0, BATCH, SEQ, *conv_args)
        yc_l = _conv(z, T_CTX // DEC_SEQ, DEC_BATCH, DEC_SEQ, *conv_args)
        yn_c, new_k, new_v = _ctx_attn(z, l, new_k, new_v)
        yn_l = _na_attn(z, cache_k, cache_v, na_bias, l)
        gn = ret_gn_g[l].reshape(1, RET_WIDTH)
        yr_c, sf_l, sb_l = _retention(z, lg[l], gn, latent=False)
        yr_l = _retention(z, lg[l], gn, latent=True, layer=l, rope=rope,
                          s0_f=state_ret_f, s0_b=state_ret_b)
        x, xs, route, seg = _outproj((yc_c, yc_l), (yn_c, yn_l), (yr_c, yr_l), x, mods[l],
                                     norm2_g[l].reshape(1, D_MODEL), w_out_b[l], w_route[l],
                                     b_route[l].reshape(1, ROUTE_COLS))
        y = _moe(xs, w1, w3, w2, l, *_dispatch_tables(seg))
        sf_list.append(sf_l)
        sb_list.append(sb_l)
    fg = final_g.reshape(1, D_MODEL)
    y_prompt = _final(x, y, route, mods[DEPTH - 1], fg, 0, NB_CTX).reshape(BATCH, SEQ, D_MODEL)
    y_sample = _final(x, y, route, mods[DEPTH - 1], fg, NB_CTX, NB_LAT).reshape(DEC_BATCH, DEC_SEQ, D_MODEL)
    return (y_prompt, y_sample, new_k, new_v, jnp.stack(sf_list, axis=1), jnp.stack(sb_list, axis=1))
```

```python
import functools

import numpy as np
import jax
import jax.numpy as jnp
from jax import lax
from jax.experimental import pallas as pl
from jax.experimental.pallas import tpu as pltpu

D_MODEL = 1024
BATCH = 32
SEQ = 256
DEPTH = 2
DEC_BATCH = 4
DEC_SEQ = 4096
PAST_LEN = 512
GRID_W = 64
GRID_H = DEC_SEQ // GRID_W
CONV_CH = 256
CONV_K = 31
NA_HEADS = 8
NA_DIM = 64
NA_WIDTH = NA_HEADS * NA_DIM
NA_KH = 8
NA_KW = 16
RET_HEADS = 4
RET_DIM = 64
RET_WIDTH = RET_HEADS * RET_DIM
RET_CHUNK = 128
ROPE_BASE = 10000.0
N_GROUPS = 4
EXPERTS_PER_GROUP = 8
N_EXPERTS = N_GROUPS * EXPERTS_PER_GROUP
D_EXPERT = 512
IN_COLS = 2 * CONV_CH + 3 * NA_WIDTH + 4 * RET_WIDTH
EPS = 1e-6
NEG_INF = -1e30

F32 = jnp.float32
BF16 = jnp.bfloat16
HIGHEST = lax.Precision.HIGHEST

T_CTX = BATCH * SEQ
T_LAT = DEC_BATCH * DEC_SEQ
T_ALL = T_CTX + T_LAT
N_COND = 1 + DEC_BATCH
COND_ROWS = 8

TM = 512
NB_CTX = T_CTX // TM
NB_LAT = T_LAT // TM
NB_ALL = NB_CTX + NB_LAT
LAT_BLOCKS_PER_REQ = DEC_SEQ // TM

LANES = 128
SUBLANES = 8
ROUTE_COLS = LANES

COL_CONV = 0
COL_NA_Q = 2 * CONV_CH
COL_NA_K = COL_NA_Q + NA_WIDTH
COL_NA_V = COL_NA_K + NA_WIDTH
COL_RET = COL_NA_V + NA_WIDTH

NA_ROWS = 8
NA_Q = NA_ROWS * GRID_W
NA_KROWS = NA_ROWS + NA_KH
NA_KEYS = NA_KROWS * GRID_W
NA_RB = GRID_H // NA_ROWS

MOE_BLK = 512
MOE_SUB = 2
MOE_TS = TM // MOE_SUB
MOE_LS = -(-(2 * MOE_TS + N_EXPERTS * (SUBLANES - 1)) // LANES) * LANES
MOE_LC = MOE_SUB * MOE_LS
N_CHUNK = NB_ALL * MOE_SUB
MOE_NBLK = -(-(N_CHUNK * MOE_LS) // MOE_BLK) + N_EXPERTS
assert MOE_SUB <= SUBLANES

VMEM_LIMIT = 56 * 1024 * 1024


def _cparams(sem):
    return pltpu.CompilerParams(dimension_semantics=sem, vmem_limit_bytes=VMEM_LIMIT)


def _sigmoid(x):
    return 1.0 / (1.0 + jnp.exp(-x))


def _cond_row(i):
    return jnp.where(i < NB_CTX, 0, 1 + (i - NB_CTX) // LAT_BLOCKS_PER_REQ)


ADA_TN = 1536


def _ada_kernel(cv_ref, w_ref, b_ref, o_ref):
    cv = cv_ref[...]
    s = cv * _sigmoid(cv)
    o_ref[...] = jnp.dot(s, w_ref[...], precision=HIGHEST, preferred_element_type=F32) + b_ref[...]


def _ada(cv, w_ada, b_ada):
    n = 6 * D_MODEL
    return pl.pallas_call(
        _ada_kernel,
        grid=(DEPTH, n // ADA_TN),
        in_specs=[
            pl.BlockSpec((COND_ROWS, D_MODEL), lambda l, j: (0, 0)),
            pl.BlockSpec((None, D_MODEL, ADA_TN), lambda l, j: (l, 0, j)),
            pl.BlockSpec((None, 1, ADA_TN), lambda l, j: (l, 0, j)),
        ],
        out_specs=pl.BlockSpec((None, COND_ROWS, ADA_TN), lambda l, j: (l, 0, j)),
        out_shape=jax.ShapeDtypeStruct((DEPTH, COND_ROWS, n), F32),
        compiler_params=_cparams(("arbitrary", "arbitrary")),
        name="ada_mod",
    )(cv, w_ada, b_ada.reshape(DEPTH, 1, n))


IN_TN = 768


def _norm_mod(x, g, shift, scale):
    ms = jnp.mean(x * x, axis=-1, keepdims=True)
    return (x * lax.rsqrt(ms + EPS) * g) * (1.0 + scale) + shift


def _inproj_body(x, mod_ref, g_ref, w_ref, z_ref):
    h = _norm_mod(x, g_ref[...], mod_ref[0:1, :], mod_ref[1:2, :]).astype(BF16)
    for c in range(IN_COLS // IN_TN):
        cols = slice(c * IN_TN, (c + 1) * IN_TN)
        z_ref[:, cols] = jnp.dot(h, w_ref[:, cols], preferred_element_type=F32).astype(BF16)


def _inproj_first_kernel(xc_ref, xl_ref, mod_ref, g_ref, w_ref, z_ref, xo_ref):
    i = pl.program_id(0)
    x = jnp.where(i < NB_CTX, xc_ref[...], xl_ref[...])
    xo_ref[...] = x
    _inproj_body(x, mod_ref, g_ref, w_ref, z_ref)


U32 = jnp.uint32
D_HALF = D_MODEL // 2
_HI_MASK = np.uint32(0xFFFF0000)


def _pack_bf16_pairs(x):
    xb = x.astype(BF16).astype(F32)
    lo = lax.bitcast_convert_type(xb[:, :D_HALF], U32) >> 16
    hi = lax.bitcast_convert_type(xb[:, D_HALF:], U32) & _HI_MASK
    return lo | hi


def _unpack_bf16_pairs(w):
    lo = lax.bitcast_convert_type(w << 16, F32).astype(BF16)
    hi = lax.bitcast_convert_type(w & _HI_MASK, F32).astype(BF16)
    return lo, hi


def _slot_onehot(route, slot):
    pos = route[:, 4 + slot:5 + slot].astype(jnp.int32)
    return lax.broadcasted_iota(jnp.int32, (route.shape[0], MOE_LS), 1) == pos


def _moe_residual(x_ref, ys_ref, r_ref, mod_ref):
    ys = []
    for c in range(MOE_SUB):
        r = r_ref[c * MOE_TS:(c + 1) * MOE_TS, :]
        sel = jnp.where(_slot_onehot(r, 0), r[:, 2:3], jnp.where(_slot_onehot(r, 1), r[:, 3:4], 0.0))
        sel = sel.astype(BF16)
        halves = _unpack_bf16_pairs(ys_ref[c * MOE_LS:(c + 1) * MOE_LS, :])
        ys.append(jnp.concatenate([jnp.dot(sel, half, preferred_element_type=F32) for half in halves],
                                  axis=-1))
    return x_ref[...] + mod_ref[5:6, :] * jnp.concatenate(ys, axis=0)


def _inproj_next_kernel(x_ref, ys_ref, r_ref, modp_ref, mod_ref, g_ref, w_ref, z_ref, xo_ref):
    x = _moe_residual(x_ref, ys_ref, r_ref, modp_ref)
    xo_ref[...] = x
    _inproj_body(x, mod_ref, g_ref, w_ref, z_ref)


def _tok_spec(cols):
    return pl.BlockSpec((TM, cols), lambda i: (i, 0))


def _mod_spec():
    return pl.BlockSpec((None, 6, D_MODEL), lambda i: (_cond_row(i), 0, 0))


def _full_spec(shape):
    return pl.BlockSpec(shape, lambda i: (0,) * len(shape))


def _ctx_lat_specs(cols):
    return [pl.BlockSpec((TM, cols), lambda i: (jnp.minimum(i, NB_CTX - 1), 0)),
            pl.BlockSpec((TM, cols), lambda i: (jnp.maximum(i - NB_CTX, 0), 0))]


def _inproj_first(x_ctx, x_lat, mod, g, w_bf16):
    return pl.pallas_call(
        _inproj_first_kernel,
        grid=(NB_ALL,),
        in_specs=_ctx_lat_specs(D_MODEL) + [_mod_spec(), _full_spec((1, D_MODEL)),
                                            _full_spec((D_MODEL, IN_COLS))],
        out_specs=[_tok_spec(IN_COLS), _tok_spec(D_MODEL)],
        out_shape=[jax.ShapeDtypeStruct((T_ALL, IN_COLS), BF16),
                   jax.ShapeDtypeStruct((T_ALL, D_MODEL), F32)],
        compiler_params=_cparams(("arbitrary",)),
        name="inproj_first",
    )(x_ctx, x_lat, mod, g, w_bf16)


def _inproj_next(x, ys, route, mod_prev, mod, g, w_bf16):
    return pl.pallas_call(
        _inproj_next_kernel,
        grid=(NB_ALL,),
        in_specs=[_tok_spec(D_MODEL),
                  pl.BlockSpec((MOE_LC, D_HALF), lambda i: (i, 0)),
                  _tok_spec(ROUTE_COLS),
                  _mod_spec(), _mod_spec(), _full_spec((1, D_MODEL)),
                  _full_spec((D_MODEL, IN_COLS))],
        out_specs=[_tok_spec(IN_COLS), _tok_spec(D_MODEL)],
        out_shape=[jax.ShapeDtypeStruct((T_ALL, IN_COLS), BF16),
                   jax.ShapeDtypeStruct((T_ALL, D_MODEL), F32)],
        compiler_params=_cparams(("arbitrary",)),
        name="inproj_next",
    )(x, ys, route, mod_prev, mod, g, w_bf16)


CONV_PAD = 16
CONV_CHUNK = 64


CONV_SPAN = CONV_CHUNK + 2 * CONV_PAD - SUBLANES


CONV_UNROLL = 4


def _conv_kernel(seq, z_ref, w_ref, b_ref, g_ref, be_ref, o_ref, upad_ref, shift_refs):
    zeros = jnp.zeros((CONV_PAD, CONV_CH), F32)
    upad_ref[0:CONV_PAD, :] = zeros
    upad_ref[seq + CONV_PAD:seq + 2 * CONV_PAD, :] = zeros

    def glu(ci, carry):
        base = pl.multiple_of(ci * 256, 256)
        zc = z_ref[pl.ds(base, 256), :].astype(F32)
        upad_ref[pl.ds(base + CONV_PAD, 256), :] = zc[:, :CONV_CH] * _sigmoid(zc[:, CONV_CH:])
        return carry

    lax.fori_loop(0, seq // 256, glu, 0)

    shift = CONV_PAD - CONV_K // 2

    def chunk(ci, shift_ref):
        base = pl.multiple_of(ci * CONV_CHUNK, CONV_CHUNK)
        win = upad_ref[pl.ds(base, CONV_CHUNK + 2 * CONV_PAD), :]
        acc = jnp.zeros((CONV_CHUNK, CONV_CH), F32)
        for sub in range(SUBLANES):
            shift_ref[sub] = win[sub:sub + CONV_SPAN, :]
            for k in range(CONV_K):
                if (k + shift) % SUBLANES == sub:
                    lo = k + shift - sub
                    acc = acc + w_ref[k:k + 1, :] * shift_ref[sub, lo:lo + CONV_CHUNK, :]
        acc = acc + b_ref[...]
        mu = jnp.mean(acc, axis=-1, keepdims=True)
        d = acc - mu
        var = jnp.mean(d * d, axis=-1, keepdims=True)
        n = d * lax.rsqrt(var + EPS) * g_ref[...] + be_ref[...]
        o_ref[pl.ds(base, CONV_CHUNK), :] = (n * _sigmoid(n)).astype(BF16)

    def chunks(cj, carry):
        for u in range(CONV_UNROLL):
            chunk(cj * CONV_UNROLL + u, shift_refs.at[u])
        return carry

    lax.fori_loop(0, seq // (CONV_CHUNK * CONV_UNROLL), chunks, 0)


def _conv(z, row_block0, nseq, seq, w, b, g, be):
    return pl.pallas_call(
        functools.partial(_conv_kernel, seq),
        grid=(nseq,),
        in_specs=[pl.BlockSpec((seq, 2 * CONV_CH), lambda s: (row_block0 + s, 0)),
                  _full_spec((CONV_K, CONV_CH)), _full_spec((1, CONV_CH)),
                  _full_spec((1, CONV_CH)), _full_spec((1, CONV_CH))],
        out_specs=pl.BlockSpec((seq, CONV_CH), lambda s: (s, 0)),
        out_shape=jax.ShapeDtypeStruct((nseq * seq, CONV_CH), BF16),
        scratch_shapes=[pltpu.VMEM((seq + 2 * CONV_PAD, CONV_CH), F32),
                        pltpu.VMEM((CONV_UNROLL, SUBLANES, CONV_SPAN, CONV_CH), F32)],
        compiler_params=_cparams(("arbitrary",)),
        name="conv_seq%d" % seq,
    )(z, w, b, g, be)


def _dot_nt(a, b):
    return lax.dot_general(a, b, (((1,), (1,)), ((), ())), preferred_element_type=F32)


NA_SCALE = NA_DIM ** -0.5
assert NA_SCALE == 2.0 ** round(np.log2(NA_SCALE)), "query pre-scaling assumes a power-of-two scale"


def _ctx_attn_kernel(layer, q_ref, k_ref, v_ref, *refs):
    if layer:
        kprev_ref, vprev_ref, o_ref, ko_ref, vo_ref = refs
    else:
        o_ref, ko_ref, vo_ref = refs
    for j in range(DEPTH):
        if j < layer:
            ko_ref[j] = kprev_ref[j]
            vo_ref[j] = vprev_ref[j]
        elif j > layer:
            ko_ref[j] = jnp.zeros(ko_ref.shape[1:], F32)
            vo_ref[j] = jnp.zeros(vo_ref.shape[1:], F32)
    outs = []
    for h in range(NA_HEADS):
        cols = slice(h * NA_DIM, (h + 1) * NA_DIM)
        qh, kh, vh = q_ref[:, cols], k_ref[:, cols], v_ref[:, cols]
        ko_ref[layer, h] = kh.astype(F32)
        vo_ref[layer, h] = vh.astype(F32)
        s = _dot_nt(qh, kh) * NA_SCALE
        m = jnp.max(s, axis=-1, keepdims=True)
        p = jnp.exp(s - m)
        den = jnp.sum(p, axis=-1, keepdims=True)
        o = jnp.dot(p.astype(BF16), vh, preferred_element_type=F32)
        outs.append(o / den)
    o_ref[...] = jnp.concatenate(outs, axis=-1).astype(BF16)


def _ctx_attn(z, layer, k_prev=None, v_prev=None):
    qb, kb, vb = COL_NA_Q // NA_WIDTH, COL_NA_K // NA_WIDTH, COL_NA_V // NA_WIDTH
    head_shape = jax.ShapeDtypeStruct((BATCH, DEPTH, NA_HEADS, SEQ, NA_DIM), F32)
    head_spec = pl.BlockSpec((None, DEPTH, NA_HEADS, SEQ, NA_DIM), lambda b: (b, 0, 0, 0, 0))
    in_specs = [pl.BlockSpec((SEQ, NA_WIDTH), lambda b: (b, qb)),
                pl.BlockSpec((SEQ, NA_WIDTH), lambda b: (b, kb)),
                pl.BlockSpec((SEQ, NA_WIDTH), lambda b: (b, vb))]
    args = [z, z, z]
    aliases = {}
    if layer:
        in_specs += [head_spec, head_spec]
        args += [k_prev, v_prev]
        aliases = {3: 1, 4: 2}
    return pl.pallas_call(
        functools.partial(_ctx_attn_kernel, layer),
        grid=(BATCH,),
        in_specs=in_specs,
        out_specs=[pl.BlockSpec((SEQ, NA_WIDTH), lambda b: (b, 0)), head_spec, head_spec],
        out_shape=[jax.ShapeDtypeStruct((T_CTX, NA_WIDTH), BF16), head_shape, head_shape],
        input_output_aliases=aliases,
        compiler_params=_cparams(("arbitrary",)),
        name="ctx_attn",
    )(*args)


NA_KINDS = (0, NA_ROWS, GRID_H - NA_ROWS)
N_DR = 2 * NA_KH - 1
N_DC = 2 * NA_KW - 1


def _na_row_offset(r0, i, j):
    ks = min(max(r0 - NA_KH // 2, 0), GRID_H - NA_KROWS)
    r, kr = r0 + i, ks + j
    rs = min(max(r - NA_KH // 2, 0), GRID_H - NA_KH)
    return kr - r + NA_KH - 1 if rs <= kr < rs + NA_KH else None


def _na_bias_kernel(rpb_ref, o_ref):
    lh = pl.program_id(0)
    shape = (GRID_W, 2 * GRID_W)
    qc = lax.broadcasted_iota(jnp.int32, shape, 0)
    lane = lax.broadcasted_iota(jnp.int32, shape, 1)
    kc = lane % GRID_W
    dc = jnp.clip(kc - qc, -(NA_KW - 1), NA_KW - 1) + NA_KW - 1
    cs = jnp.clip(qc - NA_KW // 2, 0, GRID_W - NA_KW)
    col_ok = (kc >= cs) & (kc < cs + NA_KW)
    neg = jnp.full(shape, NEG_INF, F32)
    tiles = []
    for dr in range(N_DR):
        base = (lh * N_DR + dr) * N_DC
        val = jnp.zeros(shape, F32)
        for d in range(N_DC):
            val = jnp.where(dc == d, rpb_ref[base + d], val)
        tiles.append(jnp.where(col_ok, val, neg))
    left = lane < GRID_W
    for kind, r0 in enumerate(NA_KINDS):
        for i in range(NA_ROWS):
            for jp in range(NA_KROWS // 2):
                dl, dr_ = _na_row_offset(r0, i, 2 * jp), _na_row_offset(r0, i, 2 * jp + 1)
                tl = neg if dl is None else tiles[dl]
                tr = neg if dr_ is None else tiles[dr_]
                o_ref[kind, i * GRID_W:(i + 1) * GRID_W, jp * 2 * GRID_W:(jp + 1) * 2 * GRID_W] = (
                    jnp.where(left, tl, tr))


def _na_bias_tables(rpb):
    return pl.pallas_call(
        _na_bias_kernel,
        grid=(DEPTH * NA_HEADS,),
        in_specs=[pl.BlockSpec(memory_space=pltpu.SMEM)],
        out_specs=pl.BlockSpec((None, len(NA_KINDS), NA_Q, NA_KEYS), lambda i: (i, 0, 0, 0)),
        out_shape=jax.ShapeDtypeStruct((DEPTH * NA_HEADS, len(NA_KINDS), NA_Q, NA_KEYS), F32),
        compiler_params=_cparams(("arbitrary",)),
        name="nbr_bias",
    )(rpb.reshape(-1))


NA_G = 4


def _na_kernel(q_ref, k_ref, v_ref, kc_ref, vc_ref, bias_ref, o_ref):
    rb = pl.program_id(2)
    ks = jnp.clip(rb * NA_ROWS - NA_KH // 2, 0, GRID_H - NA_KROWS)
    start = pl.multiple_of(ks * GRID_W, GRID_W)
    q = q_ref[...] * NA_SCALE
    kl = k_ref[pl.ds(start, NA_KEYS), :]
    vl = v_ref[pl.ds(start, NA_KEYS), :]
    ones_loc = jnp.ones((NA_KEYS, NA_DIM), BF16)
    ones_ctx = jnp.ones((PAST_LEN, NA_DIM), BF16)

    def scores(hh):
        cols = slice(hh * NA_DIM, (hh + 1) * NA_DIM)
        qh = q[:, cols]
        return _dot_nt(qh, kl[:, cols]) + bias_ref[hh], _dot_nt(qh, kc_ref[hh].astype(BF16))

    outs = []
    nxt = scores(0)
    for hh in range(NA_G):
        s_loc, s_ctx = nxt
        if hh + 1 < NA_G:
            nxt = scores(hh + 1)
        cols = slice(hh * NA_DIM, (hh + 1) * NA_DIM)
        m = jnp.maximum(jnp.max(s_loc, axis=-1, keepdims=True), jnp.max(s_ctx, axis=-1, keepdims=True))
        p_loc = jnp.exp(s_loc - m).astype(BF16)
        p_ctx = jnp.exp(s_ctx - m).astype(BF16)
        v_ext = jnp.concatenate([vl[:, cols], ones_loc], axis=-1)
        vc_ext = jnp.concatenate([vc_ref[hh].astype(BF16), ones_ctx], axis=-1)
        o = (jnp.dot(p_loc, v_ext, preferred_element_type=F32)
             + jnp.dot(p_ctx, vc_ext, preferred_element_type=F32))
        outs.append(o[:, :NA_DIM] / o[:, NA_DIM:])
    o_ref[...] = jnp.concatenate(outs, axis=-1).astype(BF16)


def _na_attn(z, cache_k, cache_v, bias, layer):
    lat_q0 = T_CTX // NA_Q
    lat_s0 = T_CTX // DEC_SEQ
    width = NA_G * NA_DIM
    qc, kc, vc = COL_NA_Q // width, COL_NA_K // width, COL_NA_V // width
    groups = NA_HEADS // NA_G

    def kind(rb):
        return jnp.where(rb == 0, 0, jnp.where(rb == NA_RB - 1, 2, 1))

    ctx_spec = pl.BlockSpec((None, None, NA_G, PAST_LEN, NA_DIM), lambda b, hg, rb: (b, layer, hg, 0, 0))
    return pl.pallas_call(
        _na_kernel,
        grid=(DEC_BATCH, groups, NA_RB),
        in_specs=[pl.BlockSpec((NA_Q, width), lambda b, hg, rb: (lat_q0 + b * NA_RB + rb, qc + hg)),
                  pl.BlockSpec((DEC_SEQ, width), lambda b, hg, rb: (lat_s0 + b, kc + hg)),
                  pl.BlockSpec((DEC_SEQ, width), lambda b, hg, rb: (lat_s0 + b, vc + hg)),
                  ctx_spec, ctx_spec,
                  pl.BlockSpec((NA_G, None, NA_Q, NA_KEYS),
                               lambda b, hg, rb: (layer * groups + hg, kind(rb), 0, 0))],
        out_specs=pl.BlockSpec((NA_Q, width), lambda b, hg, rb: (b * NA_RB + rb, hg)),
        out_shape=jax.ShapeDtypeStruct((T_LAT, NA_WIDTH), BF16),
        compiler_params=_cparams(("arbitrary", "arbitrary", "arbitrary")),
        name="nbr_attn",
    )(z, z, z, cache_k, cache_v, bias)


RET_PAIR = 2 * RET_DIM
RET_NPAIR = RET_HEADS // 2
assert RET_PAIR == LANES and RET_CHUNK == LANES
RET_UNROLL = 8


def _rope_tables():
    n_freq = RET_DIM // 4
    t = np.arange(DEC_SEQ)
    inv = jnp.asarray(ROPE_BASE, F32) ** (-jnp.arange(n_freq, dtype=F32) / n_freq)
    ang_r = jnp.asarray(t // GRID_W, F32)[:, None] * inv[None, :]
    ang_c = jnp.asarray(t % GRID_W, F32)[:, None] * inv[None, :]
    cos = jnp.concatenate([jnp.cos(ang_r)] * 2 + [jnp.cos(ang_c)] * 2, axis=-1)
    sin = jnp.concatenate([-jnp.sin(ang_r), jnp.sin(ang_r), -jnp.sin(ang_c), jnp.sin(ang_c)], axis=-1)
    lane = np.arange(RET_WIDTH)
    src = np.where(lane % (2 * n_freq) < n_freq, lane + n_freq, lane - n_freq)
    swap = np.zeros((RET_WIDTH, RET_WIDTH), np.float32)
    swap[src, lane] = 1.0
    return jnp.tile(cos, (1, RET_HEADS)), jnp.tile(sin, (1, RET_HEADS)), jnp.asarray(swap, BF16)


def _ret_kernel(seq, latent, *refs):
    if latent:
        (lg_ref, z_ref, gn_ref, cos_ref, sin_ref, swap_ref, s0f_ref, s0b_ref, y_ref,
         q_s, k_s, kv_s, st_s) = refs
    else:
        lg_ref, z_ref, gn_ref, y_ref, sf_ref, sb_ref, q_s, k_s, kv_s, st_s = refs
    nc = seq // RET_CHUNK
    ch, hd, pw = RET_CHUNK, RET_DIM, RET_PAIR

    row = lax.broadcasted_iota(jnp.int32, (ch, ch), 0).astype(F32)
    col = lax.broadcasted_iota(jnp.int32, (ch, ch), 1).astype(F32)
    pos = lax.broadcasted_iota(jnp.int32, (ch, pw), 0).astype(F32)
    left = lax.broadcasted_iota(jnp.int32, (ch, pw), 1) < hd
    top = lax.broadcasted_iota(jnp.int32, (pw, pw), 0) < hd
    same_head = top == (lax.broadcasted_iota(jnp.int32, (pw, pw), 1) < hd)
    same_head2 = jnp.concatenate([same_head, same_head], axis=0)

    def per_head(mask, fn, p):
        return jnp.where(mask, fn(2 * p), fn(2 * p + 1))

    decay = []
    for h in range(RET_HEADS):
        lf, lb = lg_ref[0, h], lg_ref[1, h]
        d_f = jnp.where(row >= col, jnp.exp(jnp.maximum(row - col, 0.0) * lf), 0.0)
        d_b = jnp.where(col >= row, jnp.exp(jnp.maximum(col - row, 0.0) * lb), 0.0)
        decay.append(d_f + d_b)
    q_dec, k_dec, c_dec_f, c_dec_b = [], [], [], []
    for p in range(RET_NPAIR):
        q_dec.append(jnp.concatenate(
            [per_head(left, lambda h: jnp.exp((pos + 1.0) * lg_ref[0, h]), p),
             per_head(left, lambda h: jnp.exp((ch - pos) * lg_ref[1, h]), p)], axis=-1))
        k_dec.append(jnp.concatenate(
            [per_head(left, lambda h: jnp.exp((ch - 1.0 - pos) * lg_ref[0, h]), p),
             per_head(left, lambda h: jnp.exp(pos * lg_ref[1, h]), p)], axis=-1))
        zero = jnp.zeros((pw, pw), F32)
        c_dec_f.append(per_head(top, lambda h: jnp.exp(zero + ch * lg_ref[0, h]), p))
        c_dec_b.append(per_head(top, lambda h: jnp.exp(zero + ch * lg_ref[1, h]), p))

    def rope(x, base):
        xf = x.astype(F32)
        if not latent:
            return xf
        swapped = jnp.dot(x, swap_ref[...], preferred_element_type=F32)
        return xf * cos_ref[pl.ds(base, ch), :] + swapped * sin_ref[pl.ds(base, ch), :]

    def pass1(n, carry):
        base = pl.multiple_of(n * ch, ch)
        zc = z_ref[pl.ds(base, ch), :]
        q = rope(zc[:, 0:RET_WIDTH], base)
        k = rope(zc[:, RET_WIDTH:2 * RET_WIDTH], base) * (RET_DIM ** -0.5)
        q_s[pl.ds(base, ch), :] = q.astype(BF16)
        k_s[pl.ds(base, ch), :] = k.astype(BF16)
        v = zc[:, 2 * RET_WIDTH:3 * RET_WIDTH]
        for p in range(RET_NPAIR):
            lanes = slice(p * pw, (p + 1) * pw)
            kp = k[:, lanes]
            k2 = (jnp.concatenate([kp, kp], axis=-1) * k_dec[p]).astype(BF16)
            kv = lax.dot_general(k2, v[:, lanes], (((0,), (0,)), ((), ())), preferred_element_type=F32)
            kv_s[n, p] = jnp.where(same_head2, kv, 0.0)
        return carry

    lax.fori_loop(0, nc, pass1, 0, unroll=min(RET_UNROLL, nc))

    def block_diag(a, b):
        z = jnp.zeros((hd, hd), F32)
        return jnp.concatenate([jnp.concatenate([a, z], axis=1), jnp.concatenate([z, b], axis=1)], axis=0)

    for p in range(RET_NPAIR):
        if latent:
            s_f = block_diag(s0f_ref[2 * p], s0f_ref[2 * p + 1])
            s_b = block_diag(s0b_ref[2 * p], s0b_ref[2 * p + 1])
        else:
            s_f = s_b = jnp.zeros((pw, pw), F32)

        def fwd(n, s, p=p):
            st_s[n, p, 0:pw, :] = s.astype(BF16)
            return c_dec_f[p] * s + kv_s[n, p, 0:pw, :]

        def bwd(i, s, p=p):
            n = nc - 1 - i
            st_s[n, p, pw:2 * pw, :] = s.astype(BF16)
            return c_dec_b[p] * s + kv_s[n, p, pw:2 * pw, :]

        s_f = lax.fori_loop(0, nc, fwd, s_f)
        s_b = lax.fori_loop(0, nc, bwd, s_b)
        if not latent:
            for hh in range(2):
                blk = slice(hh * hd, (hh + 1) * hd)
                sf_ref[2 * p + hh] = s_f[blk, blk]
                sb_ref[2 * p + hh] = s_b[blk, blk]

    def pass3(n, carry):
        base = pl.multiple_of(n * ch, ch)
        zc = z_ref[pl.ds(base, ch), :]
        q = q_s[pl.ds(base, ch), :]
        k = k_s[pl.ds(base, ch), :]
        v = zc[:, 2 * RET_WIDTH:3 * RET_WIDTH]
        gate = zc[:, 3 * RET_WIDTH:4 * RET_WIDTH].astype(F32)
        outs = []
        for p in range(RET_NPAIR):
            lanes = slice(p * pw, (p + 1) * pw)
            qp, kp, vp = q[:, lanes], k[:, lanes], v[:, lanes]
            o_h = []
            for hh in range(2):
                qm = jnp.where(left == (hh == 0), qp, jnp.zeros_like(qp))
                s = _dot_nt(qm, kp) * decay[2 * p + hh]
                o_h.append(jnp.dot(s.astype(BF16), vp, preferred_element_type=F32))
            qf = qp.astype(F32)
            q2 = (jnp.concatenate([qf, qf], axis=-1) * q_dec[p]).astype(BF16)
            o = jnp.where(left, o_h[0], o_h[1]) + jnp.dot(q2, st_s[n, p], preferred_element_type=F32)

            def half_mean(t):
                s_l = jnp.sum(jnp.where(left, t, 0.0), axis=-1, keepdims=True)
                s_r = jnp.sum(jnp.where(left, 0.0, t), axis=-1, keepdims=True)
                return jnp.where(left, s_l, s_r) * (1.0 / hd)

            d = o - half_mean(o)
            outs.append(d * lax.rsqrt(half_mean(d * d) + EPS))
        nrm = jnp.concatenate(outs, axis=-1)
        y_ref[pl.ds(base, ch), :] = (nrm * gn_ref[...] * (gate * _sigmoid(gate))).astype(BF16)
        return carry

    lax.fori_loop(0, nc, pass3, 0, unroll=min(RET_UNROLL, nc))


def _retention(z, lg, gn_g, latent, layer=None, rope=None, s0_f=None, s0_b=None):
    seq = DEC_SEQ if latent else SEQ
    nseq = DEC_BATCH if latent else BATCH
    nc = seq // RET_CHUNK
    row0 = (T_CTX // DEC_SEQ) if latent else 0
    cb = COL_RET // (4 * RET_WIDTH)
    in_specs = [pl.BlockSpec(memory_space=pltpu.SMEM),
                pl.BlockSpec((seq, 4 * RET_WIDTH), lambda s: (row0 + s, cb)),
                _full_spec((1, RET_WIDTH))]
    args = [lg, z, gn_g]
    state_shape = jax.ShapeDtypeStruct((nseq, RET_HEADS, RET_DIM, RET_DIM), F32)
    y_spec = pl.BlockSpec((seq, RET_WIDTH), lambda s: (s, 0))
    y_shape = jax.ShapeDtypeStruct((nseq * seq, RET_WIDTH), BF16)
    if latent:
        st_spec = pl.BlockSpec((None, None, RET_HEADS, RET_DIM, RET_DIM), lambda s: (s, layer, 0, 0, 0))

        def const_spec(shape):
            return pl.BlockSpec(shape, lambda s: (0,) * len(shape), pipeline_mode=pl.Buffered(1))

        in_specs += [const_spec((seq, RET_WIDTH)), const_spec((seq, RET_WIDTH)),
                     const_spec((RET_WIDTH, RET_WIDTH)), st_spec, st_spec]
        args += [rope[0], rope[1], rope[2], s0_f, s0_b]
        out_specs, out_shape = y_spec, y_shape
    else:
        so_spec = pl.BlockSpec((None, RET_HEADS, RET_DIM, RET_DIM), lambda s: (s, 0, 0, 0))
        out_specs, out_shape = [y_spec, so_spec, so_spec], [y_shape, state_shape, state_shape]
    return pl.pallas_call(
        functools.partial(_ret_kernel, seq, latent),
        grid=(nseq,),
        in_specs=in_specs,
        out_specs=out_specs,
        out_shape=out_shape,
        scratch_shapes=[pltpu.VMEM((seq, RET_WIDTH), BF16), pltpu.VMEM((seq, RET_WIDTH), BF16),
                        pltpu.VMEM((nc, RET_NPAIR, 2 * RET_PAIR, RET_PAIR), F32),
                        pltpu.VMEM((nc, RET_NPAIR, 2 * RET_PAIR, RET_PAIR), BF16)],
        compiler_params=_cparams(("arbitrary",)),
        name="retention_lat" if latent else "retention_ctx",
    )(*args)


def _route(logits):
    lane = lax.broadcasted_iota(jnp.int32, logits.shape, 1)
    lane_f = lane.astype(F32)
    big = float(ROUTE_COLS)
    neg = -jnp.inf
    is_grp = lane < N_GROUPS
    gl = jnp.where(is_grp, logits, neg)
    gmax = jnp.max(gl, axis=-1, keepdims=True)
    grp = jnp.min(jnp.where(gl == gmax, lane_f, big), axis=-1, keepdims=True)
    p_grp = 1.0 / jnp.sum(jnp.exp(gl - gmax), axis=-1, keepdims=True)
    e_f = lane_f - N_GROUPS
    lo = grp * EXPERTS_PER_GROUP
    in_grp = (e_f >= lo) & (e_f < lo + EXPERTS_PER_GROUP)
    el = jnp.where(in_grp, logits, neg)
    m1 = jnp.max(el, axis=-1, keepdims=True)
    i1 = jnp.min(jnp.where(el == m1, lane_f, big), axis=-1, keepdims=True)
    el2 = jnp.where(lane_f == i1, neg, el)
    m2 = jnp.max(el2, axis=-1, keepdims=True)
    i2 = jnp.min(jnp.where(el2 == m2, lane_f, big), axis=-1, keepdims=True)
    t = jnp.exp(m2 - m1)
    g1 = p_grp / (1.0 + t)
    g2 = p_grp * t / (1.0 + t)
    rows = logits.shape[0]
    oh1, oh2 = lane_f == i1, lane_f == i2
    oh = jnp.where(oh1 | oh2, 1.0, 0.0)
    ri = lax.broadcasted_iota(jnp.int32, (rows, rows), 0)
    ci = lax.broadcasted_iota(jnp.int32, (rows, rows), 1)
    tri = (ri > ci) & ((ri // MOE_TS) == (ci // MOE_TS))
    rank = jnp.dot(jnp.where(tri, 1.0, 0.0).astype(BF16), oh.astype(BF16), preferred_element_type=F32)
    chunk_of_row = lax.broadcasted_iota(jnp.int32, logits.shape, 0) // MOE_TS
    srow = lax.broadcasted_iota(jnp.int32, (SUBLANES, ROUTE_COLS), 0)
    tiles = jnp.zeros((SUBLANES, ROUTE_COLS), F32)
    for c in range(MOE_SUB):
        count = jnp.sum(jnp.where(chunk_of_row == c, oh, 0.0), axis=0, keepdims=True)
        tiles = jnp.where(srow == c, jnp.floor((count + (SUBLANES - 1)) * (1.0 / SUBLANES)), tiles)
    upper = (lax.broadcasted_iota(jnp.int32, (ROUTE_COLS, ROUTE_COLS), 0)
             < lax.broadcasted_iota(jnp.int32, (ROUTE_COLS, ROUTE_COLS), 1))
    start = SUBLANES * jnp.dot(tiles.astype(BF16), jnp.where(upper, 1.0, 0.0).astype(BF16),
                               preferred_element_type=F32)
    pos = rank
    for c in range(MOE_SUB):
        pos = pos + jnp.where(chunk_of_row == c, start[c:c + 1, :], 0.0)
    p1 = jnp.sum(jnp.where(oh1, pos, 0.0), axis=-1, keepdims=True)
    p2 = jnp.sum(jnp.where(oh2, pos, 0.0), axis=-1, keepdims=True)
    out = jnp.zeros(logits.shape, F32)
    for k, val in enumerate((i1 - N_GROUPS, i2 - N_GROUPS, g1, g2, p1, p2)):
        out = jnp.where(lane == k, val, out)
    return out, SUBLANES * tiles


def _outproj_kernel(ycc, ycl, ync, ynl, yrc, yrl, x_ref, mod_ref, g_ref, w_ref, wr_ref, br_ref,
                    xo_ref, xs_ref, r_ref, seg_ref):
    is_ctx = pl.program_id(0) < NB_CTX
    yc = jnp.where(is_ctx, ycc[...], ycl[...])
    yn = jnp.where(is_ctx, ync[...], ynl[...])
    yr = jnp.where(is_ctx, yrc[...], yrl[...])
    y = (jnp.dot(yc, w_ref[0:CONV_CH, :], preferred_element_type=F32)
         + jnp.dot(yn, w_ref[CONV_CH:CONV_CH + NA_WIDTH, :], preferred_element_type=F32)
         + jnp.dot(yr, w_ref[CONV_CH + NA_WIDTH:, :], preferred_element_type=F32))
    x = x_ref[...] + mod_ref[2:3, :] * y
    xo_ref[...] = x
    h = _norm_mod(x, g_ref[...], mod_ref[3:4, :], mod_ref[4:5, :])
    h_hi = h.astype(BF16)
    h_lo = (h - h_hi.astype(F32)).astype(BF16)
    hw = jnp.dot(h_hi, wr_ref[...], preferred_element_type=F32)
    logits = (hw[:, :ROUTE_COLS] + hw[:, ROUTE_COLS:]
              + jnp.dot(h_lo, wr_ref[:, :ROUTE_COLS], preferred_element_type=F32) + br_ref[...])
    route, seg = _route(logits)
    r_ref[...] = route
    seg_ref[...] = seg
    for c in range(MOE_SUB):
        rows = slice(c * MOE_TS, (c + 1) * MOE_TS)
        sel = _slot_onehot(route[rows], 0) | _slot_onehot(route[rows], 1)
        xs_ref[c * MOE_LS:(c + 1) * MOE_LS, :] = _pack_bf16_pairs(
            lax.dot_general(jnp.where(sel, 1.0, 0.0).astype(BF16), h_hi[rows], (((0,), (0,)), ((), ())),
                            preferred_element_type=F32))


def _outproj(y_conv, y_na, y_ret, x, mod, g, w_bf16, w_route, b_route):
    return pl.pallas_call(
        _outproj_kernel,
        grid=(NB_ALL,),
        in_specs=(_ctx_lat_specs(CONV_CH) + _ctx_lat_specs(NA_WIDTH) + _ctx_lat_specs(RET_WIDTH)
                  + [_tok_spec(D_MODEL), _mod_spec(), _full_spec((1, D_MODEL)),
                     _full_spec((D_MODEL, D_MODEL)), _full_spec((D_MODEL, 2 * ROUTE_COLS)),
                     _full_spec((1, ROUTE_COLS))]),
        out_specs=[_tok_spec(D_MODEL), pl.BlockSpec((MOE_LC, D_HALF), lambda i: (i, 0)),
                   _tok_spec(ROUTE_COLS), pl.BlockSpec((None, SUBLANES, ROUTE_COLS), lambda i: (i, 0, 0))],
        out_shape=[jax.ShapeDtypeStruct((T_ALL, D_MODEL), F32),
                   jax.ShapeDtypeStruct((NB_ALL * MOE_LC, D_HALF), U32),
                   jax.ShapeDtypeStruct((T_ALL, ROUTE_COLS), F32),
                   jax.ShapeDtypeStruct((NB_ALL, SUBLANES, ROUTE_COLS), F32)],
        compiler_params=_cparams(("arbitrary",)),
        name="outproj_route",
    )(y_conv[0], y_conv[1], y_na[0], y_na[1], y_ret[0], y_ret[1], x, mod, g, w_bf16, w_route, b_route)


def _dispatch_tables(seg):
    seg_len = seg[:, :MOE_SUB, N_GROUPS:N_GROUPS + N_EXPERTS].astype(jnp.int32)
    seg_len = seg_len.reshape(N_CHUNK, N_EXPERTS)
    experts = jnp.arange(N_EXPERTS, dtype=jnp.int32)
    in_chunk = jnp.cumsum(seg_len, axis=1) - seg_len
    seg_row = in_chunk + MOE_LS * jnp.arange(N_CHUNK, dtype=jnp.int32)[:, None]
    seg_off = jnp.cumsum(seg_len, axis=0) - seg_len
    rows_e = jnp.sum(seg_len, axis=0)
    chunk_rows = jnp.sum(seg_len, axis=1)
    nblk = (rows_e + MOE_BLK - 1) // MOE_BLK
    blk_end = jnp.cumsum(nblk)
    blk_start = blk_end - nblk
    blk = jnp.arange(MOE_NBLK, dtype=jnp.int32)
    n_active = blk_end[-1]
    blk_e = jnp.minimum(jnp.sum((blk_end[None, :] <= jnp.minimum(blk, n_active - 1)[:, None]).astype(jnp.int32),
                                axis=-1), N_EXPERTS - 1)
    mine = blk_e[:, None] == experts[None, :]
    blk_lo = (blk - jnp.sum(jnp.where(mine, blk_start[None, :], 0), axis=-1)) * MOE_BLK
    left = jnp.sum(jnp.where(mine, rows_e[None, :], 0), axis=-1) - blk_lo
    blk_nv = jnp.where(blk < n_active, jnp.clip(left, 0, MOE_BLK), 0).astype(jnp.int32)
    off_b = jnp.sum(jnp.where(mine[:, None, :], seg_off[None, :, :], 0), axis=-1)
    end_b = off_b + jnp.sum(jnp.where(mine[:, None, :], seg_len[None, :, :], 0), axis=-1)
    blk_c0 = jnp.sum((end_b <= blk_lo[:, None]).astype(jnp.int32), axis=-1)
    blk_c1 = jnp.sum((off_b < (blk_lo + blk_nv)[:, None]).astype(jnp.int32), axis=-1)
    after = jnp.sum(jnp.where(mine, blk_end[None, :], 0), axis=-1)
    blk_next_e = jnp.where(after < n_active, jnp.take(blk_e, jnp.minimum(after, MOE_NBLK - 1)), -1)
    row_b = jnp.sum(jnp.where(mine[:, None, :], seg_row[None, :, :], 0), axis=-1)
    first = jnp.maximum(off_b, blk_lo[:, None])
    piece_n = jnp.minimum(end_b, (blk_lo + blk_nv)[:, None]) - first
    piece_src = row_b + first - off_b
    piece_dst = first - blk_lo[:, None]
    return (blk_e, blk_next_e.astype(jnp.int32), blk_nv, blk_c0, blk_c1, piece_src.reshape(-1),
            piece_dst.reshape(-1), piece_n.reshape(-1), chunk_rows)


def _moe_kernel(layer, blk_e, blk_next_e, blk_nv, blk_c0, blk_c1, piece_src, piece_dst, piece_n,
                chunk_rows, xs_hbm, w1_hbm, w3_hbm, w2_hbm, ys_hbm, xbuf, obuf, zeros,
                w1f, w3f, w2f, w1b, w3b, w2b, gsem, ssem, zsem, wsem):
    i = pl.program_id(0)
    last = pl.num_programs(0) - 1
    slot = i % 2

    def tiles(v):
        return pl.multiple_of(v, SUBLANES)

    def for_segments(blk, fn):
        def body(c, carry):
            k = blk * N_CHUNK + c
            n = piece_n[k]

            @pl.when(n > 0)
            def _():
                fn(tiles(piece_src[k]), tiles(piece_dst[k]), tiles(n))

            return carry

        lax.fori_loop(blk_c0[blk], blk_c1[blk], body, 0)

    def weight_copies(e):
        return [pltpu.make_async_copy(src.at[layer, e], dst, wsem)
                for src, dst in ((w1_hbm, w1f), (w3_hbm, w3f), (w2_hbm, w2f))]

    def start_gathers(blk, s):
        for_segments(blk, lambda src, dst, n: pltpu.make_async_copy(
            xs_hbm.at[pl.ds(src, n)], xbuf.at[s, pl.ds(dst, n)], gsem.at[s]).start())

    def start_scatters(blk, s):
        for_segments(blk, lambda dst, src, n: pltpu.make_async_copy(
            obuf.at[s, pl.ds(src, n)], ys_hbm.at[pl.ds(dst, n)], ssem.at[s]).start())

    def wait_rows(blk, s, sem):
        n = tiles(blk_nv[blk])

        @pl.when(n > 0)
        def _():
            pltpu.make_async_copy(xs_hbm.at[pl.ds(0, n)], xbuf.at[s, pl.ds(0, n)], sem.at[s]).wait()

    @pl.when(i == 0)
    def _():
        xbuf[...] = jnp.zeros_like(xbuf)
        zeros[...] = jnp.zeros_like(zeros)

        def tail(c):
            n = tiles(MOE_LS - chunk_rows[c])
            return n, pltpu.make_async_copy(zeros.at[pl.ds(0, n)],
                                            ys_hbm.at[pl.ds(tiles(c * MOE_LS + chunk_rows[c]), n)], zsem)

        def fill(c, carry):
            n, copy = tail(c)
            pl.when(n > 0)(copy.start)
            return carry

        def drain(c, carry):
            n, copy = tail(c)
            pl.when(n > 0)(copy.wait)
            return carry

        lax.fori_loop(0, N_CHUNK, fill, 0)
        lax.fori_loop(0, N_CHUNK, drain, 0)
        start_gathers(0, 0)
        for copy in weight_copies(blk_e[0]):
            copy.start()

    @pl.when(i < last)
    def _():
        start_gathers(i + 1, 1 - slot)

    @pl.when(i >= 2)
    def _():
        wait_rows(i - 2, slot, ssem)

    @pl.when(blk_nv[i] > 0)
    def _():
        @pl.when((i == 0) | (blk_e[i] != blk_e[jnp.maximum(i - 1, 0)]))
        def _():
            for copy in weight_copies(blk_e[i]):
                copy.wait()
            w1b[...] = w1f[...].astype(BF16)
            w3b[...] = w3f[...].astype(BF16)
            w2b[...] = w2f[...].astype(BF16)

            @pl.when(blk_next_e[i] >= 0)
            def _():
                for copy in weight_copies(blk_next_e[i]):
                    copy.start()

        wait_rows(i, slot, gsem)
        x_lo, x_hi = _unpack_bf16_pairs(xbuf[slot])

        def in_dot(w):
            return (jnp.dot(x_lo, w[:D_HALF, :], preferred_element_type=F32)
                    + jnp.dot(x_hi, w[D_HALF:, :], preferred_element_type=F32))

        a = in_dot(w1b)
        b = in_dot(w3b)
        mid = (a * _sigmoid(a) * b).astype(BF16)
        obuf[slot] = _pack_bf16_pairs(jnp.dot(mid, w2b[...], preferred_element_type=F32))
        start_scatters(i, slot)

    @pl.when(i == last)
    def _():
        wait_rows(i - 1, 1 - slot, ssem)
        wait_rows(i, slot, ssem)


def _moe(xs, w1, w3, w2, layer, blk_e, blk_next_e, blk_nv, blk_c0, blk_c1, piece_src, piece_dst,
         piece_n, chunk_rows):
    any_spec = pl.BlockSpec(memory_space=pl.ANY)
    grid_spec = pltpu.PrefetchScalarGridSpec(
        num_scalar_prefetch=9,
        grid=(MOE_NBLK,),
        in_specs=[any_spec, any_spec, any_spec, any_spec],
        out_specs=any_spec,
        scratch_shapes=[pltpu.VMEM((2, MOE_BLK, D_HALF), U32), pltpu.VMEM((2, MOE_BLK, D_HALF), U32),
                        pltpu.VMEM((MOE_LS - 2 * MOE_TS, D_HALF), U32),
                        pltpu.VMEM((D_MODEL, D_EXPERT), F32), pltpu.VMEM((D_MODEL, D_EXPERT), F32),
                        pltpu.VMEM((D_EXPERT, D_MODEL), F32),
                        pltpu.VMEM((D_MODEL, D_EXPERT), BF16), pltpu.VMEM((D_MODEL, D_EXPERT), BF16),
                        pltpu.VMEM((D_EXPERT, D_MODEL), BF16),
                        pltpu.SemaphoreType.DMA((2,)), pltpu.SemaphoreType.DMA((2,)),
                        pltpu.SemaphoreType.DMA, pltpu.SemaphoreType.DMA])
    return pl.pallas_call(
        functools.partial(_moe_kernel, layer),
        grid_spec=grid_spec,
        out_shape=jax.ShapeDtypeStruct((NB_ALL * MOE_LC, D_HALF), U32),
        compiler_params=_cparams(("arbitrary",)),
        name="moe_experts",
    )(blk_e, blk_next_e, blk_nv, blk_c0, blk_c1, piece_src, piece_dst, piece_n, chunk_rows,
      xs, w1, w3, w2)


def _final_kernel(x_ref, ys_ref, r_ref, mod_ref, g_ref, o_ref):
    x = _moe_residual(x_ref, ys_ref, r_ref, mod_ref)
    ms = jnp.mean(x * x, axis=-1, keepdims=True)
    o_ref[...] = x * lax.rsqrt(ms + EPS) * g_ref[...]


def _final(x, ys, route, mod, g, block0, nblocks):
    return pl.pallas_call(
        _final_kernel,
        grid=(nblocks,),
        in_specs=[pl.BlockSpec((TM, D_MODEL), lambda i: (block0 + i, 0)),
                  pl.BlockSpec((MOE_LC, D_HALF), lambda i: (block0 + i, 0)),
                  pl.BlockSpec((TM, ROUTE_COLS), lambda i: (block0 + i, 0)),
                  pl.BlockSpec((None, 6, D_MODEL), lambda i: (_cond_row(block0 + i), 0, 0)),
                  _full_spec((1, D_MODEL))],
        out_specs=_tok_spec(D_MODEL),
        out_shape=jax.ShapeDtypeStruct((nblocks * TM, D_MODEL), F32),
        compiler_params=_cparams(("arbitrary",)),
        name="final_norm",
    )(x, ys, route, mod, g)


def kernel(x_prompt, x_sample, c, cache_k, cache_v, state_ret_f, state_ret_b, c_ctx, w_ada, b_ada, norm1_g, norm2_g, w_in, w_out, conv_w, conv_b, conv_ln_g, conv_ln_b, na_rpb, ret_lg_f, ret_lg_b, ret_gn_g, w_route_g, b_route_g, w_route_e, b_route_e, w1, w3, w2, final_g):
    cv = jnp.zeros((COND_ROWS, D_MODEL), F32).at[0].set(c_ctx).at[1:N_COND].set(c)
    mods = _ada(cv, w_ada, b_ada).reshape(DEPTH, COND_ROWS, 6, D_MODEL)
    w_in_b = w_in.astype(BF16)
    w_out_b = w_out.astype(BF16)
    pad = ROUTE_COLS - N_GROUPS - N_EXPERTS
    w_route = jnp.pad(jnp.concatenate([w_route_g, w_route_e], axis=-1), ((0, 0), (0, 0), (0, pad)))
    b_route = jnp.pad(jnp.concatenate([b_route_g, b_route_e], axis=-1), ((0, 0), (0, pad)))
    w_route_hi = w_route.astype(BF16)
    w_route_lo = (w_route - w_route_hi.astype(F32)).astype(BF16)
    w_route = jnp.concatenate([w_route_hi, w_route_lo], axis=-1)
    na_bias = _na_bias_tables(na_rpb)
    rope = _rope_tables()
    lg = jnp.stack([ret_lg_f, ret_lg_b], axis=1)

    x_ctx = x_prompt.reshape(T_CTX, D_MODEL)
    x_lat = x_sample.reshape(T_LAT, D_MODEL)
    x = y = route = new_k = new_v = None
    sf_list, sb_list = [], []
    for l in range(DEPTH):
        g1 = norm1_g[l].reshape(1, D_MODEL)
        if l == 0:
            z, x = _inproj_first(x_ctx, x_lat, mods[l], g1, w_in_b[l])
        else:
            z, x = _inproj_next(x, y, route, mods[l - 1], mods[l], g1, w_in_b[l])
        conv_args = (conv_w[l], conv_b[l].reshape(1, -1), conv_ln_g[l].reshape(1, -1),
                     conv_ln_b[l].reshape(1, -1))
        yc_c = _conv(z, 0, BATCH, SEQ, *conv_args)
        yc_l = _conv(z, T_CTX // DEC_SEQ, DEC_BATCH, DEC_SEQ, *conv_args)
        yn_c, new_k, new_v = _ctx_attn(z, l, new_k, new_v)
        yn_l = _na_attn(z, cache_k, cache_v, na_bias, l)
        gn = ret_gn_g[l].reshape(1, RET_WIDTH)
        yr_c, sf_l, sb_l = _retention(z, lg[l], gn, latent=False)
        yr_l = _retention(z, lg[l], gn, latent=True, layer=l, rope=rope,
                          s0_f=state_ret_f, s0_b=state_ret_b)
        x, xs, route, seg = _outproj((yc_c, yc_l), (yn_c, yn_l), (yr_c, yr_l), x, mods[l],
                                     norm2_g[l].reshape(1, D_MODEL), w_out_b[l], w_route[l],
                                     b_route[l].reshape(1, ROUTE_COLS))
        y = _moe(xs, w1, w3, w2, l, *_dispatch_tables(seg))
        sf_list.append(sf_l)
        sb_list.append(sb_l)
    fg = final_g.reshape(1, D_MODEL)
    y_prompt = _final(x, y, route, mods[DEPTH - 1], fg, 0, NB_CTX).reshape(BATCH, SEQ, D_MODEL)
    y_sample = _final(x, y, route, mods[DEPTH - 1], fg, NB_CTX, NB_LAT).reshape(DEC_BATCH, DEC_SEQ, D_MODEL)
    return (y_prompt, y_sample, new_k, new_v, jnp.stack(sf_list, axis=1), jnp.stack(sb_list, axis=1))
```

```python
import functools

import numpy as np
import jax
import jax.numpy as jnp
from jax import lax
from jax.experimental import pallas as pl
from jax.experimental.pallas import tpu as pltpu

D_MODEL = 1024
BATCH = 32
SEQ = 256
DEPTH = 2
DEC_BATCH = 4
DEC_SEQ = 4096
PAST_LEN = 512
GRID_W = 64
GRID_H = DEC_SEQ // GRID_W
CONV_CH = 256
CONV_K = 31
NA_HEADS = 8
NA_DIM = 64
NA_WIDTH = NA_HEADS * NA_DIM
NA_KH = 8
NA_KW = 16
RET_HEADS = 4
RET_DIM = 64
RET_WIDTH = RET_HEADS * RET_DIM
RET_CHUNK = 128
ROPE_BASE = 10000.0
N_GROUPS = 4
EXPERTS_PER_GROUP = 8
N_EXPERTS = N_GROUPS * EXPERTS_PER_GROUP
D_EXPERT = 512
IN_COLS = 2 * CONV_CH + 3 * NA_WIDTH + 4 * RET_WIDTH
EPS = 1e-6
NEG_INF = -1e30

F32 = jnp.float32
BF16 = jnp.bfloat16
HIGHEST = lax.Precision.HIGHEST

T_CTX = BATCH * SEQ
T_LAT = DEC_BATCH * DEC_SEQ
T_ALL = T_CTX + T_LAT
N_COND = 1 + DEC_BATCH
COND_ROWS = 8

TM = 512
NB_CTX = T_CTX // TM
NB_LAT = T_LAT // TM
NB_ALL = NB_CTX + NB_LAT
LAT_BLOCKS_PER_REQ = DEC_SEQ // TM

LANES = 128
SUBLANES = 8
MXU_TILE = 256
ROUTE_COLS = LANES

COL_CONV = 0
COL_NA_Q = 2 * CONV_CH
COL_NA_K = COL_NA_Q + NA_WIDTH
COL_NA_V = COL_NA_K + NA_WIDTH
COL_RET = COL_NA_V + NA_WIDTH

NA_ROWS = 8
NA_Q = NA_ROWS * GRID_W
NA_KROWS = NA_ROWS + NA_KH
NA_KEYS = NA_KROWS * GRID_W
NA_RB = GRID_H // NA_ROWS

MOE_BLK = 512
MOE_ROW_STEP = 128
MOE_SUB = 1
MOE_TS = TM // MOE_SUB
MOE_LS = -(-(2 * MOE_TS + N_EXPERTS * (SUBLANES - 1)) // LANES) * LANES
MOE_LC = MOE_SUB * MOE_LS
N_CHUNK = NB_ALL * MOE_SUB
MOE_NBLK = -(-(N_CHUNK * MOE_LS) // MOE_BLK) + N_EXPERTS
assert MOE_SUB <= SUBLANES

VMEM_LIMIT = 56 * 1024 * 1024


def _cparams(sem):
    return pltpu.CompilerParams(dimension_semantics=sem, vmem_limit_bytes=VMEM_LIMIT)


def _sigmoid(x):
    return 1.0 / (1.0 + jnp.exp(-x))


def _cond_row(i):
    return jnp.where(i < NB_CTX, 0, 1 + (i - NB_CTX) // LAT_BLOCKS_PER_REQ)


ADA_TN = 1536


def _ada_kernel(cv_ref, w_ref, b_ref, o_ref):
    cv = cv_ref[...]
    s = cv * _sigmoid(cv)
    o_ref[...] = jnp.dot(s, w_ref[...], precision=HIGHEST, preferred_element_type=F32) + b_ref[...]


def _ada(cv, w_ada, b_ada):
    n = 6 * D_MODEL
    return pl.pallas_call(
        _ada_kernel,
        grid=(DEPTH, n // ADA_TN),
        in_specs=[
            pl.BlockSpec((COND_ROWS, D_MODEL), lambda l, j: (0, 0)),
            pl.BlockSpec((None, D_MODEL, ADA_TN), lambda l, j: (l, 0, j)),
            pl.BlockSpec((None, 1, ADA_TN), lambda l, j: (l, 0, j)),
        ],
        out_specs=pl.BlockSpec((None, COND_ROWS, ADA_TN), lambda l, j: (l, 0, j)),
        out_shape=jax.ShapeDtypeStruct((DEPTH, COND_ROWS, n), F32),
        compiler_params=_cparams(("arbitrary", "arbitrary")),
        name="ada_mod",
    )(cv, w_ada, b_ada.reshape(DEPTH, 1, n))


IN_TN = 768


def _norm_mod(x, g, shift, scale):
    ms = jnp.mean(x * x, axis=-1, keepdims=True)
    return (x * lax.rsqrt(ms + EPS) * g) * (1.0 + scale) + shift


def _inproj_body(x, mod_ref, g_ref, w_ref, z_ref):
    h = _norm_mod(x, g_ref[...], mod_ref[0:1, :], mod_ref[1:2, :]).astype(BF16)
    for c in range(IN_COLS // IN_TN):
        cols = slice(c * IN_TN, (c + 1) * IN_TN)
        z_ref[:, cols] = jnp.dot(h, w_ref[:, cols], preferred_element_type=F32).astype(BF16)


def _inproj_first_kernel(xc_ref, xl_ref, mod_ref, g_ref, w_ref, z_ref, xo_ref):
    i = pl.program_id(0)
    x = jnp.where(i < NB_CTX, xc_ref[...], xl_ref[...])
    xo_ref[...] = x
    _inproj_body(x, mod_ref, g_ref, w_ref, z_ref)


U32 = jnp.uint32
D_HALF = D_MODEL // 2
_HI_MASK = np.uint32(0xFFFF0000)


def _pack_words(lo, hi):
    lo = lax.bitcast_convert_type(lo.astype(BF16).astype(F32), U32) >> 16
    hi = lax.bitcast_convert_type(hi.astype(BF16).astype(F32), U32) & _HI_MASK
    return lo | hi


def _pack_bf16_pairs(x):
    return _pack_words(x[:, :D_HALF], x[:, D_HALF:])


def _unpack_bf16_pairs(w):
    lo = lax.bitcast_convert_type(w << 16, F32).astype(BF16)
    hi = lax.bitcast_convert_type(w & _HI_MASK, F32).astype(BF16)
    return lo, hi


def _slot_onehot(route, slot):
    pos = route[:, 4 + slot:5 + slot].astype(jnp.int32)
    return lax.broadcasted_iota(jnp.int32, (route.shape[0], MOE_LS), 1) == pos


def _moe_residual(x_ref, ys_ref, r_ref, mod_ref):
    ys = []
    for c in range(MOE_SUB):
        r = r_ref[c * MOE_TS:(c + 1) * MOE_TS, :]
        sel = jnp.where(_slot_onehot(r, 0), r[:, 2:3], jnp.where(_slot_onehot(r, 1), r[:, 3:4], 0.0))
        sel = sel.astype(BF16)
        halves = _unpack_bf16_pairs(ys_ref[c * MOE_LS:(c + 1) * MOE_LS, :])
        ys.append(jnp.concatenate([jnp.dot(sel, half, preferred_element_type=F32) for half in halves],
                                  axis=-1))
    return x_ref[...] + mod_ref[5:6, :] * jnp.concatenate(ys, axis=0)


def _inproj_next_kernel(x_ref, ys_ref, r_ref, modp_ref, mod_ref, g_ref, w_ref, z_ref, xo_ref):
    x = _moe_residual(x_ref, ys_ref, r_ref, modp_ref)
    xo_ref[...] = x
    _inproj_body(x, mod_ref, g_ref, w_ref, z_ref)


def _tok_spec(cols):
    return pl.BlockSpec((TM, cols), lambda i: (i, 0))


def _mod_spec():
    return pl.BlockSpec((None, 6, D_MODEL), lambda i: (_cond_row(i), 0, 0))


def _full_spec(shape):
    return pl.BlockSpec(shape, lambda i: (0,) * len(shape))


def _ctx_lat_specs(cols):
    return [pl.BlockSpec((TM, cols), lambda i: (jnp.minimum(i, NB_CTX - 1), 0)),
            pl.BlockSpec((TM, cols), lambda i: (jnp.maximum(i - NB_CTX, 0), 0))]


def _inproj_first(x_ctx, x_lat, mod, g, w_bf16):
    return pl.pallas_call(
        _inproj_first_kernel,
        grid=(NB_ALL,),
        in_specs=_ctx_lat_specs(D_MODEL) + [_mod_spec(), _full_spec((1, D_MODEL)),
                                            _full_spec((D_MODEL, IN_COLS))],
        out_specs=[_tok_spec(IN_COLS), _tok_spec(D_MODEL)],
        out_shape=[jax.ShapeDtypeStruct((T_ALL, IN_COLS), BF16),
                   jax.ShapeDtypeStruct((T_ALL, D_MODEL), F32)],
        compiler_params=_cparams(("arbitrary",)),
        name="inproj_first",
    )(x_ctx, x_lat, mod, g, w_bf16)


def _inproj_next(x, ys, route, mod_prev, mod, g, w_bf16):
    return pl.pallas_call(
        _inproj_next_kernel,
        grid=(NB_ALL,),
        in_specs=[_tok_spec(D_MODEL),
                  pl.BlockSpec((MOE_LC, D_HALF), lambda i: (i, 0)),
                  _tok_spec(ROUTE_COLS),
                  _mod_spec(), _mod_spec(), _full_spec((1, D_MODEL)),
                  _full_spec((D_MODEL, IN_COLS))],
        out_specs=[_tok_spec(IN_COLS), _tok_spec(D_MODEL)],
        out_shape=[jax.ShapeDtypeStruct((T_ALL, IN_COLS), BF16),
                   jax.ShapeDtypeStruct((T_ALL, D_MODEL), F32)],
        compiler_params=_cparams(("arbitrary",)),
        name="inproj_next",
    )(x, ys, route, mod_prev, mod, g, w_bf16)


CONV_PAD = 16
CONV_CHUNK = 64


CONV_SPAN = CONV_CHUNK + 2 * CONV_PAD - SUBLANES


CONV_UNROLL = 4


def _conv_kernel(seq, z_ref, w_ref, b_ref, g_ref, be_ref, o_ref, upad_ref, shift_refs):
    zeros = jnp.zeros((CONV_PAD, CONV_CH), F32)
    upad_ref[0:CONV_PAD, :] = zeros
    upad_ref[seq + CONV_PAD:seq + 2 * CONV_PAD, :] = zeros

    def glu(ci, carry):
        base = pl.multiple_of(ci * 256, 256)
        zc = z_ref[pl.ds(base, 256), :].astype(F32)
        upad_ref[pl.ds(base + CONV_PAD, 256), :] = zc[:, :CONV_CH] * _sigmoid(zc[:, CONV_CH:])
        return carry

    lax.fori_loop(0, seq // 256, glu, 0)

    shift = CONV_PAD - CONV_K // 2

    def chunk(ci, shift_ref):
        base = pl.multiple_of(ci * CONV_CHUNK, CONV_CHUNK)
        win = upad_ref[pl.ds(base, CONV_CHUNK + 2 * CONV_PAD), :]
        acc = jnp.zeros((CONV_CHUNK, CONV_CH), F32)
        for sub in range(SUBLANES):
            shift_ref[sub] = win[sub:sub + CONV_SPAN, :]
            for k in range(CONV_K):
                if (k + shift) % SUBLANES == sub:
                    lo = k + shift - sub
                    acc = acc + w_ref[k:k + 1, :] * shift_ref[sub, lo:lo + CONV_CHUNK, :]
        acc = acc + b_ref[...]
        mu = jnp.mean(acc, axis=-1, keepdims=True)
        d = acc - mu
        var = jnp.mean(d * d, axis=-1, keepdims=True)
        n = d * lax.rsqrt(var + EPS) * g_ref[...] + be_ref[...]
        o_ref[pl.ds(base, CONV_CHUNK), :] = (n * _sigmoid(n)).astype(BF16)

    def chunks(cj, carry):
        for u in range(CONV_UNROLL):
            chunk(cj * CONV_UNROLL + u, shift_refs.at[u])
        return carry

    lax.fori_loop(0, seq // (CONV_CHUNK * CONV_UNROLL), chunks, 0)


def _conv(z, row_block0, nseq, seq, w, b, g, be):
    return pl.pallas_call(
        functools.partial(_conv_kernel, seq),
        grid=(nseq,),
        in_specs=[pl.BlockSpec((seq, 2 * CONV_CH), lambda s: (row_block0 + s, 0)),
                  _full_spec((CONV_K, CONV_CH)), _full_spec((1, CONV_CH)),
                  _full_spec((1, CONV_CH)), _full_spec((1, CONV_CH))],
        out_specs=pl.BlockSpec((seq, CONV_CH), lambda s: (s, 0)),
        out_shape=jax.ShapeDtypeStruct((nseq * seq, CONV_CH), BF16),
        scratch_shapes=[pltpu.VMEM((seq + 2 * CONV_PAD, CONV_CH), F32),
                        pltpu.VMEM((CONV_UNROLL, SUBLANES, CONV_SPAN, CONV_CH), F32)],
        compiler_params=_cparams(("arbitrary",)),
        name="conv_seq%d" % seq,
    )(z, w, b, g, be)


def _dot_nt(a, b):
    return lax.dot_general(a, b, (((1,), (1,)), ((), ())), preferred_element_type=F32)


NA_SCALE = NA_DIM ** -0.5
assert NA_SCALE == 2.0 ** round(np.log2(NA_SCALE)), "query pre-scaling assumes a power-of-two scale"


def _ctx_attn_kernel(layer, q_ref, k_ref, v_ref, *refs):
    if layer:
        kprev_ref, vprev_ref, o_ref, ko_ref, vo_ref = refs
    else:
        o_ref, ko_ref, vo_ref = refs
    for j in range(DEPTH):
        if j < layer:
            ko_ref[j] = kprev_ref[j]
            vo_ref[j] = vprev_ref[j]
        elif j > layer:
            ko_ref[j] = jnp.zeros(ko_ref.shape[1:], F32)
            vo_ref[j] = jnp.zeros(vo_ref.shape[1:], F32)
    outs = []
    for h in range(NA_HEADS):
        cols = slice(h * NA_DIM, (h + 1) * NA_DIM)
        qh, kh, vh = q_ref[:, cols], k_ref[:, cols], v_ref[:, cols]
        ko_ref[layer, h] = kh.astype(F32)
        vo_ref[layer, h] = vh.astype(F32)
        s = _dot_nt(qh, kh) * NA_SCALE
        m = jnp.max(s, axis=-1, keepdims=True)
        p = jnp.exp(s - m)
        den = jnp.sum(p, axis=-1, keepdims=True)
        o = jnp.dot(p.astype(BF16), vh, preferred_element_type=F32)
        outs.append(o / den)
    o_ref[...] = jnp.concatenate(outs, axis=-1).astype(BF16)


def _ctx_attn(z, layer, k_prev=None, v_prev=None):
    qb, kb, vb = COL_NA_Q // NA_WIDTH, COL_NA_K // NA_WIDTH, COL_NA_V // NA_WIDTH
    head_shape = jax.ShapeDtypeStruct((BATCH, DEPTH, NA_HEADS, SEQ, NA_DIM), F32)
    head_spec = pl.BlockSpec((None, DEPTH, NA_HEADS, SEQ, NA_DIM), lambda b: (b, 0, 0, 0, 0))
    in_specs = [pl.BlockSpec((SEQ, NA_WIDTH), lambda b: (b, qb)),
                pl.BlockSpec((SEQ, NA_WIDTH), lambda b: (b, kb)),
                pl.BlockSpec((SEQ, NA_WIDTH), lambda b: (b, vb))]
    args = [z, z, z]
    aliases = {}
    if layer:
        in_specs += [head_spec, head_spec]
        args += [k_prev, v_prev]
        aliases = {3: 1, 4: 2}
    return pl.pallas_call(
        functools.partial(_ctx_attn_kernel, layer),
        grid=(BATCH,),
        in_specs=in_specs,
        out_specs=[pl.BlockSpec((SEQ, NA_WIDTH), lambda b: (b, 0)), head_spec, head_spec],
        out_shape=[jax.ShapeDtypeStruct((T_CTX, NA_WIDTH), BF16), head_shape, head_shape],
        input_output_aliases=aliases,
        compiler_params=_cparams(("arbitrary",)),
        name="ctx_attn",
    )(*args)


NA_KINDS = (0, NA_ROWS, GRID_H - NA_ROWS)
N_DR = 2 * NA_KH - 1
N_DC = 2 * NA_KW - 1


def _na_row_offset(r0, i, j):
    ks = min(max(r0 - NA_KH // 2, 0), GRID_H - NA_KROWS)
    r, kr = r0 + i, ks + j
    rs = min(max(r - NA_KH // 2, 0), GRID_H - NA_KH)
    return kr - r + NA_KH - 1 if rs <= kr < rs + NA_KH else None


def _na_bias_kernel(rpb_ref, o_ref):
    lh = pl.program_id(0)
    shape = (GRID_W, 2 * GRID_W)
    qc = lax.broadcasted_iota(jnp.int32, shape, 0)
    lane = lax.broadcasted_iota(jnp.int32, shape, 1)
    kc = lane % GRID_W
    dc = jnp.clip(kc - qc, -(NA_KW - 1), NA_KW - 1) + NA_KW - 1
    cs = jnp.clip(qc - NA_KW // 2, 0, GRID_W - NA_KW)
    col_ok = (kc >= cs) & (kc < cs + NA_KW)
    neg = jnp.full(shape, NEG_INF, F32)
    tiles = []
    for dr in range(N_DR):
        base = (lh * N_DR + dr) * N_DC
        val = jnp.zeros(shape, F32)
        for d in range(N_DC):
            val = jnp.where(dc == d, rpb_ref[base + d], val)
        tiles.append(jnp.where(col_ok, val, neg))
    left = lane < GRID_W
    for kind, r0 in enumerate(NA_KINDS):
        for i in range(NA_ROWS):
            for jp in range(NA_KROWS // 2):
                dl, dr_ = _na_row_offset(r0, i, 2 * jp), _na_row_offset(r0, i, 2 * jp + 1)
                tl = neg if dl is None else tiles[dl]
                tr = neg if dr_ is None else tiles[dr_]
                o_ref[kind, i * GRID_W:(i + 1) * GRID_W, jp * 2 * GRID_W:(jp + 1) * 2 * GRID_W] = (
                    jnp.where(left, tl, tr))


def _na_bias_tables(rpb):
    return pl.pallas_call(
        _na_bias_kernel,
        grid=(DEPTH * NA_HEADS,),
        in_specs=[pl.BlockSpec(memory_space=pltpu.SMEM)],
        out_specs=pl.BlockSpec((None, len(NA_KINDS), NA_Q, NA_KEYS), lambda i: (i, 0, 0, 0)),
        out_shape=jax.ShapeDtypeStruct((DEPTH * NA_HEADS, len(NA_KINDS), NA_Q, NA_KEYS), F32),
        compiler_params=_cparams(("arbitrary",)),
        name="nbr_bias",
    )(rpb.reshape(-1))


NA_G = 4


def _na_kernel(q_ref, k_ref, v_ref, kc_ref, vc_ref, bias_ref, o_ref):
    rb = pl.program_id(2)
    ks = jnp.clip(rb * NA_ROWS - NA_KH // 2, 0, GRID_H - NA_KROWS)
    start = pl.multiple_of(ks * GRID_W, GRID_W)
    q = q_ref[...] * NA_SCALE
    kl = k_ref[pl.ds(start, NA_KEYS), :]
    vl = v_ref[pl.ds(start, NA_KEYS), :]
    ones_loc = jnp.ones((NA_KEYS, NA_DIM), BF16)
    ones_ctx = jnp.ones((PAST_LEN, NA_DIM), BF16)

    def scores(hh):
        cols = slice(hh * NA_DIM, (hh + 1) * NA_DIM)
        qh = q[:, cols]
        return _dot_nt(qh, kl[:, cols]) + bias_ref[hh], _dot_nt(qh, kc_ref[hh].astype(BF16))

    outs = []
    nxt = scores(0)
    for hh in range(NA_G):
        s_loc, s_ctx = nxt
        if hh + 1 < NA_G:
            nxt = scores(hh + 1)
        cols = slice(hh * NA_DIM, (hh + 1) * NA_DIM)
        m = jnp.maximum(jnp.max(s_loc, axis=-1, keepdims=True), jnp.max(s_ctx, axis=-1, keepdims=True))
        p_loc = jnp.exp(s_loc - m).astype(BF16)
        p_ctx = jnp.exp(s_ctx - m).astype(BF16)
        v_ext = jnp.concatenate([vl[:, cols], ones_loc], axis=-1)
        vc_ext = jnp.concatenate([vc_ref[hh].astype(BF16), ones_ctx], axis=-1)
        o = (jnp.dot(p_loc, v_ext, preferred_element_type=F32)
             + jnp.dot(p_ctx, vc_ext, preferred_element_type=F32))
        outs.append(o[:, :NA_DIM] / o[:, NA_DIM:])
    o_ref[...] = jnp.concatenate(outs, axis=-1).astype(BF16)


def _na_attn(z, cache_k, cache_v, bias, layer):
    lat_q0 = T_CTX // NA_Q
    lat_s0 = T_CTX // DEC_SEQ
    width = NA_G * NA_DIM
    qc, kc, vc = COL_NA_Q // width, COL_NA_K // width, COL_NA_V // width
    groups = NA_HEADS // NA_G

    def kind(rb):
        return jnp.where(rb == 0, 0, jnp.where(rb == NA_RB - 1, 2, 1))

    ctx_spec = pl.BlockSpec((None, None, NA_G, PAST_LEN, NA_DIM), lambda b, hg, rb: (b, layer, hg, 0, 0))
    return pl.pallas_call(
        _na_kernel,
        grid=(DEC_BATCH, groups, NA_RB),
        in_specs=[pl.BlockSpec((NA_Q, width), lambda b, hg, rb: (lat_q0 + b * NA_RB + rb, qc + hg)),
                  pl.BlockSpec((DEC_SEQ, width), lambda b, hg, rb: (lat_s0 + b, kc + hg)),
                  pl.BlockSpec((DEC_SEQ, width), lambda b, hg, rb: (lat_s0 + b, vc + hg)),
                  ctx_spec, ctx_spec,
                  pl.BlockSpec((NA_G, None, NA_Q, NA_KEYS),
                               lambda b, hg, rb: (layer * groups + hg, kind(rb), 0, 0))],
        out_specs=pl.BlockSpec((NA_Q, width), lambda b, hg, rb: (b * NA_RB + rb, hg)),
        out_shape=jax.ShapeDtypeStruct((T_LAT, NA_WIDTH), BF16),
        compiler_params=_cparams(("arbitrary", "arbitrary", "arbitrary")),
        name="nbr_attn",
    )(z, z, z, cache_k, cache_v, bias)


RET_PAIR = 2 * RET_DIM
RET_NPAIR = RET_HEADS // 2
assert RET_PAIR == LANES and RET_CHUNK == LANES
RET_UNROLL = 8


def _rope_tables():
    n_freq = RET_DIM // 4
    t = np.arange(DEC_SEQ)
    inv = jnp.asarray(ROPE_BASE, F32) ** (-jnp.arange(n_freq, dtype=F32) / n_freq)
    ang_r = jnp.asarray(t // GRID_W, F32)[:, None] * inv[None, :]
    ang_c = jnp.asarray(t % GRID_W, F32)[:, None] * inv[None, :]
    cos = jnp.concatenate([jnp.cos(ang_r)] * 2 + [jnp.cos(ang_c)] * 2, axis=-1)
    sin = jnp.concatenate([-jnp.sin(ang_r), jnp.sin(ang_r), -jnp.sin(ang_c), jnp.sin(ang_c)], axis=-1)
    lane = np.arange(RET_WIDTH)
    src = np.where(lane % (2 * n_freq) < n_freq, lane + n_freq, lane - n_freq)
    swap = np.zeros((RET_WIDTH, RET_WIDTH), np.float32)
    swap[src, lane] = 1.0
    return jnp.tile(cos, (1, RET_HEADS)), jnp.tile(sin, (1, RET_HEADS)), jnp.asarray(swap, BF16)


def _ret_kernel(seq, latent, *refs):
    if latent:
        (lg_ref, z_ref, gn_ref, cos_ref, sin_ref, swap_ref, s0f_ref, s0b_ref, y_ref,
         q_s, k_s, kv_s, st_s) = refs
    else:
        lg_ref, z_ref, gn_ref, y_ref, sf_ref, sb_ref, q_s, k_s, kv_s, st_s = refs
    nc = seq // RET_CHUNK
    ch, hd, pw = RET_CHUNK, RET_DIM, RET_PAIR

    row = lax.broadcasted_iota(jnp.int32, (ch, ch), 0).astype(F32)
    col = lax.broadcasted_iota(jnp.int32, (ch, ch), 1).astype(F32)
    pos = lax.broadcasted_iota(jnp.int32, (ch, pw), 0).astype(F32)
    left = lax.broadcasted_iota(jnp.int32, (ch, pw), 1) < hd
    top = lax.broadcasted_iota(jnp.int32, (pw, pw), 0) < hd
    same_head = top == (lax.broadcasted_iota(jnp.int32, (pw, pw), 1) < hd)
    same_head2 = jnp.concatenate([same_head, same_head], axis=0)

    def per_head(mask, fn, p):
        return jnp.where(mask, fn(2 * p), fn(2 * p + 1))

    decay = []
    for h in range(RET_HEADS):
        lf, lb = lg_ref[0, h], lg_ref[1, h]
        d_f = jnp.where(row >= col, jnp.exp(jnp.maximum(row - col, 0.0) * lf), 0.0)
        d_b = jnp.where(col >= row, jnp.exp(jnp.maximum(col - row, 0.0) * lb), 0.0)
        decay.append(d_f + d_b)
    q_dec, k_dec, c_dec_f, c_dec_b = [], [], [], []
    for p in range(RET_NPAIR):
        q_dec.append(jnp.concatenate(
            [per_head(left, lambda h: jnp.exp((pos + 1.0) * lg_ref[0, h]), p),
             per_head(left, lambda h: jnp.exp((ch - pos) * lg_ref[1, h]), p)], axis=-1))
        k_dec.append(jnp.concatenate(
            [per_head(left, lambda h: jnp.exp((ch - 1.0 - pos) * lg_ref[0, h]), p),
             per_head(left, lambda h: jnp.exp(pos * lg_ref[1, h]), p)], axis=-1))
        zero = jnp.zeros((pw, pw), F32)
        c_dec_f.append(per_head(top, lambda h: jnp.exp(zero + ch * lg_ref[0, h]), p))
        c_dec_b.append(per_head(top, lambda h: jnp.exp(zero + ch * lg_ref[1, h]), p))

    def rope(x, base):
        xf = x.astype(F32)
        if not latent:
            return xf
        swapped = jnp.dot(x, swap_ref[...], preferred_element_type=F32)
        return xf * cos_ref[pl.ds(base, ch), :] + swapped * sin_ref[pl.ds(base, ch), :]

    def pass1(n, carry):
        base = pl.multiple_of(n * ch, ch)
        zc = z_ref[pl.ds(base, ch), :]
        q = rope(zc[:, 0:RET_WIDTH], base)
        k = rope(zc[:, RET_WIDTH:2 * RET_WIDTH], base) * (RET_DIM ** -0.5)
        q_s[pl.ds(base, ch), :] = q.astype(BF16)
        k_s[pl.ds(base, ch), :] = k.astype(BF16)
        v = zc[:, 2 * RET_WIDTH:3 * RET_WIDTH]
        for p in range(RET_NPAIR):
            lanes = slice(p * pw, (p + 1) * pw)
            kp = k[:, lanes]
            k2 = (jnp.concatenate([kp, kp], axis=-1) * k_dec[p]).astype(BF16)
            kv = lax.dot_general(k2, v[:, lanes], (((0,), (0,)), ((), ())), preferred_element_type=F32)
            kv_s[n, p] = jnp.where(same_head2, kv, 0.0)
        return carry

    lax.fori_loop(0, nc, pass1, 0, unroll=min(RET_UNROLL, nc))

    def block_diag(a, b):
        z = jnp.zeros((hd, hd), F32)
        return jnp.concatenate([jnp.concatenate([a, z], axis=1), jnp.concatenate([z, b], axis=1)], axis=0)

    for p in range(RET_NPAIR):
        if latent:
            s_f = block_diag(s0f_ref[2 * p], s0f_ref[2 * p + 1])
            s_b = block_diag(s0b_ref[2 * p], s0b_ref[2 * p + 1])
        else:
            s_f = s_b = jnp.zeros((pw, pw), F32)

        def fwd(n, s, p=p):
            st_s[n, p, 0:pw, :] = s.astype(BF16)
            return c_dec_f[p] * s + kv_s[n, p, 0:pw, :]

        def bwd(i, s, p=p):
            n = nc - 1 - i
            st_s[n, p, pw:2 * pw, :] = s.astype(BF16)
            return c_dec_b[p] * s + kv_s[n, p, pw:2 * pw, :]

        s_f = lax.fori_loop(0, nc, fwd, s_f)
        s_b = lax.fori_loop(0, nc, bwd, s_b)
        if not latent:
            for hh in range(2):
                blk = slice(hh * hd, (hh + 1) * hd)
                sf_ref[2 * p + hh] = s_f[blk, blk]
                sb_ref[2 * p + hh] = s_b[blk, blk]

    def pass3(n, carry):
        base = pl.multiple_of(n * ch, ch)
        zc = z_ref[pl.ds(base, ch), :]
        q = q_s[pl.ds(base, ch), :]
        k = k_s[pl.ds(base, ch), :]
        v = zc[:, 2 * RET_WIDTH:3 * RET_WIDTH]
        gate = zc[:, 3 * RET_WIDTH:4 * RET_WIDTH].astype(F32)
        outs = []
        for p in range(RET_NPAIR):
            lanes = slice(p * pw, (p + 1) * pw)
            qp, kp, vp = q[:, lanes], k[:, lanes], v[:, lanes]
            o_h = []
            for hh in range(2):
                qm = jnp.where(left == (hh == 0), qp, jnp.zeros_like(qp))
                s = _dot_nt(qm, kp) * decay[2 * p + hh]
                o_h.append(jnp.dot(s.astype(BF16), vp, preferred_element_type=F32))
            qf = qp.astype(F32)
            q2 = (jnp.concatenate([qf, qf], axis=-1) * q_dec[p]).astype(BF16)
            o = jnp.where(left, o_h[0], o_h[1]) + jnp.dot(q2, st_s[n, p], preferred_element_type=F32)

            def half_mean(t):
                s_l = jnp.sum(jnp.where(left, t, 0.0), axis=-1, keepdims=True)
                s_r = jnp.sum(jnp.where(left, 0.0, t), axis=-1, keepdims=True)
                return jnp.where(left, s_l, s_r) * (1.0 / hd)

            d = o - half_mean(o)
            outs.append(d * lax.rsqrt(half_mean(d * d) + EPS))
        nrm = jnp.concatenate(outs, axis=-1)
        y_ref[pl.ds(base, ch), :] = (nrm * gn_ref[...] * (gate * _sigmoid(gate))).astype(BF16)
        return carry

    lax.fori_loop(0, nc, pass3, 0, unroll=min(RET_UNROLL, nc))


def _retention(z, lg, gn_g, latent, layer=None, rope=None, s0_f=None, s0_b=None):
    seq = DEC_SEQ if latent else SEQ
    nseq = DEC_BATCH if latent else BATCH
    nc = seq // RET_CHUNK
    row0 = (T_CTX // DEC_SEQ) if latent else 0
    cb = COL_RET // (4 * RET_WIDTH)
    in_specs = [pl.BlockSpec(memory_space=pltpu.SMEM),
                pl.BlockSpec((seq, 4 * RET_WIDTH), lambda s: (row0 + s, cb)),
                _full_spec((1, RET_WIDTH))]
    args = [lg, z, gn_g]
    state_shape = jax.ShapeDtypeStruct((nseq, RET_HEADS, RET_DIM, RET_DIM), F32)
    y_spec = pl.BlockSpec((seq, RET_WIDTH), lambda s: (s, 0))
    y_shape = jax.ShapeDtypeStruct((nseq * seq, RET_WIDTH), BF16)
    if latent:
        st_spec = pl.BlockSpec((None, None, RET_HEADS, RET_DIM, RET_DIM), lambda s: (s, layer, 0, 0, 0))

        def const_spec(shape):
            return pl.BlockSpec(shape, lambda s: (0,) * len(shape), pipeline_mode=pl.Buffered(1))

        in_specs += [const_spec((seq, RET_WIDTH)), const_spec((seq, RET_WIDTH)),
                     const_spec((RET_WIDTH, RET_WIDTH)), st_spec, st_spec]
        args += [rope[0], rope[1], rope[2], s0_f, s0_b]
        out_specs, out_shape = y_spec, y_shape
    else:
        so_spec = pl.BlockSpec((None, RET_HEADS, RET_DIM, RET_DIM), lambda s: (s, 0, 0, 0))
        out_specs, out_shape = [y_spec, so_spec, so_spec], [y_shape, state_shape, state_shape]
    return pl.pallas_call(
        functools.partial(_ret_kernel, seq, latent),
        grid=(nseq,),
        in_specs=in_specs,
        out_specs=out_specs,
        out_shape=out_shape,
        scratch_shapes=[pltpu.VMEM((seq, RET_WIDTH), BF16), pltpu.VMEM((seq, RET_WIDTH), BF16),
                        pltpu.VMEM((nc, RET_NPAIR, 2 * RET_PAIR, RET_PAIR), F32),
                        pltpu.VMEM((nc, RET_NPAIR, 2 * RET_PAIR, RET_PAIR), BF16)],
        compiler_params=_cparams(("arbitrary",)),
        name="retention_lat" if latent else "retention_ctx",
    )(*args)


def _route(logits):
    lane = lax.broadcasted_iota(jnp.int32, logits.shape, 1)
    lane_f = lane.astype(F32)
    big = float(ROUTE_COLS)
    neg = -jnp.inf
    is_grp = lane < N_GROUPS
    gl = jnp.where(is_grp, logits, neg)
    gmax = jnp.max(gl, axis=-1, keepdims=True)
    grp = jnp.min(jnp.where(gl == gmax, lane_f, big), axis=-1, keepdims=True)
    p_grp = 1.0 / jnp.sum(jnp.exp(gl - gmax), axis=-1, keepdims=True)
    e_f = lane_f - N_GROUPS
    lo = grp * EXPERTS_PER_GROUP
    in_grp = (e_f >= lo) & (e_f < lo + EXPERTS_PER_GROUP)
    el = jnp.where(in_grp, logits, neg)
    m1 = jnp.max(el, axis=-1, keepdims=True)
    i1 = jnp.min(jnp.where(el == m1, lane_f, big), axis=-1, keepdims=True)
    el2 = jnp.where(lane_f == i1, neg, el)
    m2 = jnp.max(el2, axis=-1, keepdims=True)
    i2 = jnp.min(jnp.where(el2 == m2, lane_f, big), axis=-1, keepdims=True)
    t = jnp.exp(m2 - m1)
    g1 = p_grp / (1.0 + t)
    g2 = p_grp * t / (1.0 + t)
    rows = logits.shape[0]
    oh1, oh2 = lane_f == i1, lane_f == i2
    oh = jnp.where(oh1 | oh2, 1.0, 0.0)
    ri = lax.broadcasted_iota(jnp.int32, (rows, rows), 0)
    ci = lax.broadcasted_iota(jnp.int32, (rows, rows), 1)
    tri = (ri > ci) & ((ri // MOE_TS) == (ci // MOE_TS))
    rank = jnp.dot(jnp.where(tri, 1.0, 0.0).astype(BF16), oh.astype(BF16), preferred_element_type=F32)
    chunk_of_row = lax.broadcasted_iota(jnp.int32, logits.shape, 0) // MOE_TS
    srow = lax.broadcasted_iota(jnp.int32, (SUBLANES, ROUTE_COLS), 0)
    tiles = jnp.zeros((SUBLANES, ROUTE_COLS), F32)
    for c in range(MOE_SUB):
        count = jnp.sum(jnp.where(chunk_of_row == c, oh, 0.0), axis=0, keepdims=True)
        tiles = jnp.where(srow == c, jnp.floor((count + (SUBLANES - 1)) * (1.0 / SUBLANES)), tiles)
    upper = (lax.broadcasted_iota(jnp.int32, (ROUTE_COLS, ROUTE_COLS), 0)
             < lax.broadcasted_iota(jnp.int32, (ROUTE_COLS, ROUTE_COLS), 1))
    start = SUBLANES * jnp.dot(tiles.astype(BF16), jnp.where(upper, 1.0, 0.0).astype(BF16),
                               preferred_element_type=F32)
    pos = rank
    for c in range(MOE_SUB):
        pos = pos + jnp.where(chunk_of_row == c, start[c:c + 1, :], 0.0)
    p1 = jnp.sum(jnp.where(oh1, pos, 0.0), axis=-1, keepdims=True)
    p2 = jnp.sum(jnp.where(oh2, pos, 0.0), axis=-1, keepdims=True)
    out = jnp.zeros(logits.shape, F32)
    for k, val in enumerate((i1 - N_GROUPS, i2 - N_GROUPS, g1, g2, p1, p2)):
        out = jnp.where(lane == k, val, out)
    return out, SUBLANES * tiles


def _outproj_kernel(ycc, ycl, ync, ynl, yrc, yrl, x_ref, mod_ref, g_ref, w_ref, wr_ref, br_ref,
                    xo_ref, xs_ref, r_ref, seg_ref):
    is_ctx = pl.program_id(0) < NB_CTX
    yc = jnp.where(is_ctx, ycc[...], ycl[...])
    yn = jnp.where(is_ctx, ync[...], ynl[...])
    yr = jnp.where(is_ctx, yrc[...], yrl[...])
    y = (jnp.dot(yc, w_ref[0:CONV_CH, :], preferred_element_type=F32)
         + jnp.dot(yn, w_ref[CONV_CH:CONV_CH + NA_WIDTH, :], preferred_element_type=F32)
         + jnp.dot(yr, w_ref[CONV_CH + NA_WIDTH:, :], preferred_element_type=F32))
    x = x_ref[...] + mod_ref[2:3, :] * y
    xo_ref[...] = x
    h = _norm_mod(x, g_ref[...], mod_ref[3:4, :], mod_ref[4:5, :])
    h_hi = h.astype(BF16)
    h_lo = (h - h_hi.astype(F32)).astype(BF16)
    hw = jnp.dot(h_hi, wr_ref[...], preferred_element_type=F32)
    logits = (hw[:, :ROUTE_COLS] + hw[:, ROUTE_COLS:]
              + jnp.dot(h_lo, wr_ref[:, :ROUTE_COLS], preferred_element_type=F32) + br_ref[...])
    route, seg = _route(logits)
    r_ref[...] = route
    seg_ref[...] = seg
    for c in range(MOE_SUB):
        rows = slice(c * MOE_TS, (c + 1) * MOE_TS)
        sel = _slot_onehot(route[rows], 0) | _slot_onehot(route[rows], 1)
        xs_ref[c * MOE_LS:(c + 1) * MOE_LS, :] = _pack_bf16_pairs(
            lax.dot_general(jnp.where(sel, 1.0, 0.0).astype(BF16), h_hi[rows], (((0,), (0,)), ((), ())),
                            preferred_element_type=F32))


def _outproj(y_conv, y_na, y_ret, x, mod, g, w_bf16, w_route, b_route):
    return pl.pallas_call(
        _outproj_kernel,
        grid=(NB_ALL,),
        in_specs=(_ctx_lat_specs(CONV_CH) + _ctx_lat_specs(NA_WIDTH) + _ctx_lat_specs(RET_WIDTH)
                  + [_tok_spec(D_MODEL), _mod_spec(), _full_spec((1, D_MODEL)),
                     _full_spec((D_MODEL, D_MODEL)), _full_spec((D_MODEL, 2 * ROUTE_COLS)),
                     _full_spec((1, ROUTE_COLS))]),
        out_specs=[_tok_spec(D_MODEL), pl.BlockSpec((MOE_LC, D_HALF), lambda i: (i, 0)),
                   _tok_spec(ROUTE_COLS), pl.BlockSpec((None, SUBLANES, ROUTE_COLS), lambda i: (i, 0, 0))],
        out_shape=[jax.ShapeDtypeStruct((T_ALL, D_MODEL), F32),
                   jax.ShapeDtypeStruct((NB_ALL * MOE_LC, D_HALF), U32),
                   jax.ShapeDtypeStruct((T_ALL, ROUTE_COLS), F32),
                   jax.ShapeDtypeStruct((NB_ALL, SUBLANES, ROUTE_COLS), F32)],
        compiler_params=_cparams(("arbitrary",)),
        name="outproj_route",
    )(y_conv[0], y_conv[1], y_na[0], y_na[1], y_ret[0], y_ret[1], x, mod, g, w_bf16, w_route, b_route)


def _dispatch_tables(seg):
    seg_len = seg[:, :MOE_SUB, N_GROUPS:N_GROUPS + N_EXPERTS].astype(jnp.int32)
    seg_len = seg_len.reshape(N_CHUNK, N_EXPERTS)
    experts = jnp.arange(N_EXPERTS, dtype=jnp.int32)
    in_chunk = jnp.cumsum(seg_len, axis=1) - seg_len
    seg_row = in_chunk + MOE_LS * jnp.arange(N_CHUNK, dtype=jnp.int32)[:, None]
    seg_off = jnp.cumsum(seg_len, axis=0) - seg_len
    rows_e = jnp.sum(seg_len, axis=0)
    chunk_rows = jnp.sum(seg_len, axis=1)
    nblk = (rows_e + MOE_BLK - 1) // MOE_BLK
    blk_end = jnp.cumsum(nblk)
    blk_start = blk_end - nblk
    blk = jnp.arange(MOE_NBLK, dtype=jnp.int32)
    n_active = blk_end[-1]
    blk_e = jnp.minimum(jnp.sum((blk_end[None, :] <= jnp.minimum(blk, n_active - 1)[:, None]).astype(jnp.int32),
                                axis=-1), N_EXPERTS - 1)
    mine = blk_e[:, None] == experts[None, :]
    blk_lo = (blk - jnp.sum(jnp.where(mine, blk_start[None, :], 0), axis=-1)) * MOE_BLK
    left = jnp.sum(jnp.where(mine, rows_e[None, :], 0), axis=-1) - blk_lo
    blk_nv = jnp.where(blk < n_active, jnp.clip(left, 0, MOE_BLK), 0).astype(jnp.int32)
    off_b = jnp.sum(jnp.where(mine[:, None, :], seg_off[None, :, :], 0), axis=-1)
    end_b = off_b + jnp.sum(jnp.where(mine[:, None, :], seg_len[None, :, :], 0), axis=-1)
    blk_c0 = jnp.sum((end_b <= blk_lo[:, None]).astype(jnp.int32), axis=-1)
    blk_c1 = jnp.sum((off_b < (blk_lo + blk_nv)[:, None]).astype(jnp.int32), axis=-1)
    after = jnp.sum(jnp.where(mine, blk_end[None, :], 0), axis=-1)
    blk_next_e = jnp.where(after < n_active, jnp.take(blk_e, jnp.minimum(after, MOE_NBLK - 1)), -1)
    row_b = jnp.sum(jnp.where(mine[:, None, :], seg_row[None, :, :], 0), axis=-1)
    first = jnp.maximum(off_b, blk_lo[:, None])
    piece_n = jnp.minimum(end_b, (blk_lo + blk_nv)[:, None]) - first
    piece_src = row_b + first - off_b
    piece_dst = first - blk_lo[:, None]
    return (blk_e, blk_next_e.astype(jnp.int32), blk_nv, blk_c0, blk_c1, piece_src.reshape(-1),
            piece_dst.reshape(-1), piece_n.reshape(-1), chunk_rows)


def _moe_kernel(layer, blk_e, blk_next_e, blk_nv, blk_c0, blk_c1, piece_src, piece_dst, piece_n,
                chunk_rows, xs_hbm, w1_hbm, w3_hbm, w2_hbm, ys_hbm, xbuf, obuf, zeros,
                w1f, w3f, w2f, w1b, w3b, w2b, gsem, ssem, zsem, wsem):
    i = pl.program_id(0)
    last = pl.num_programs(0) - 1
    slot = i % 2

    def tiles(v):
        return pl.multiple_of(v, SUBLANES)

    def for_segments(blk, fn):
        def body(c, carry):
            k = blk * N_CHUNK + c
            n = piece_n[k]

            @pl.when(n > 0)
            def _():
                fn(tiles(piece_src[k]), tiles(piece_dst[k]), tiles(n))

            return carry

        lax.fori_loop(blk_c0[blk], blk_c1[blk], body, 0)

    def weight_copies(e):
        return [pltpu.make_async_copy(src.at[layer, e], dst, wsem)
                for src, dst in ((w1_hbm, w1f), (w3_hbm, w3f), (w2_hbm, w2f))]

    def start_gathers(blk, s):
        for_segments(blk, lambda src, dst, n: pltpu.make_async_copy(
            xs_hbm.at[pl.ds(src, n)], xbuf.at[s, pl.ds(dst, n)], gsem.at[s]).start())

    def start_scatters(blk, s):
        for_segments(blk, lambda dst, src, n: pltpu.make_async_copy(
            obuf.at[s, pl.ds(src, n)], ys_hbm.at[pl.ds(dst, n)], ssem.at[s]).start())

    def wait_rows(blk, s, sem):
        n = tiles(blk_nv[blk])

        @pl.when(n > 0)
        def _():
            pltpu.make_async_copy(xs_hbm.at[pl.ds(0, n)], xbuf.at[s, pl.ds(0, n)], sem.at[s]).wait()

    @pl.when(i == 0)
    def _():
        xbuf[...] = jnp.zeros_like(xbuf)
        zeros[...] = jnp.zeros_like(zeros)

        def tail(c):
            n = tiles(MOE_LS - chunk_rows[c])
            return n, pltpu.make_async_copy(zeros.at[pl.ds(0, n)],
                                            ys_hbm.at[pl.ds(tiles(c * MOE_LS + chunk_rows[c]), n)], zsem)

        def fill(c, carry):
            n, copy = tail(c)
            pl.when(n > 0)(copy.start)
            return carry

        def drain(c, carry):
            n, copy = tail(c)
            pl.when(n > 0)(copy.wait)
            return carry

        lax.fori_loop(0, N_CHUNK, fill, 0)
        lax.fori_loop(0, N_CHUNK, drain, 0)
        start_gathers(0, 0)
        for copy in weight_copies(blk_e[0]):
            copy.start()

    @pl.when(i < last)
    def _():
        start_gathers(i + 1, 1 - slot)

    @pl.when(i >= 2)
    def _():
        wait_rows(i - 2, slot, ssem)

    @pl.when(blk_nv[i] > 0)
    def _():
        @pl.when((i == 0) | (blk_e[i] != blk_e[jnp.maximum(i - 1, 0)]))
        def _():
            for copy in weight_copies(blk_e[i]):
                copy.wait()
            w1b[...] = w1f[...].astype(BF16)
            w3b[...] = w3f[...].astype(BF16)
            w2b[...] = w2f[...].astype(BF16)

            @pl.when(blk_next_e[i] >= 0)
            def _():
                for copy in weight_copies(blk_next_e[i]):
                    copy.start()

        wait_rows(i, slot, gsem)

        def expert_mlp(rows):
            x_lo, x_hi = _unpack_bf16_pairs(xbuf[slot, 0:rows, :])
            n_hid = D_EXPERT // MXU_TILE

            def in_dot(w, t):
                cols = slice(t * MXU_TILE, (t + 1) * MXU_TILE)
                return (jnp.dot(x_lo, w[:D_HALF, cols], preferred_element_type=F32)
                        + jnp.dot(x_hi, w[D_HALF:, cols], preferred_element_type=F32))

            ab = [(in_dot(w1b, t), in_dot(w3b, t)) for t in range(n_hid)]
            mid = [(a * _sigmoid(a) * b).astype(BF16) for a, b in ab]

            def out_dot(t):
                cols = slice(t * MXU_TILE, (t + 1) * MXU_TILE)
                return sum(jnp.dot(mid[j], w2b[j * MXU_TILE:(j + 1) * MXU_TILE, cols],
                                   preferred_element_type=F32) for j in range(n_hid))

            n_word = D_HALF // MXU_TILE
            for t in range(n_word):
                obuf[slot, 0:rows, t * MXU_TILE:(t + 1) * MXU_TILE] = _pack_words(
                    out_dot(t), out_dot(t + n_word))

        for rows in range(MOE_ROW_STEP, MOE_BLK + 1, MOE_ROW_STEP):
            @pl.when((blk_nv[i] > rows - MOE_ROW_STEP) & (blk_nv[i] <= rows))
            def _(rows=rows):
                expert_mlp(rows)

        start_scatters(i, slot)

    @pl.when(i == last)
    def _():
        wait_rows(i - 1, 1 - slot, ssem)
        wait_rows(i, slot, ssem)


def _moe(xs, w1, w3, w2, layer, blk_e, blk_next_e, blk_nv, blk_c0, blk_c1, piece_src, piece_dst,
         piece_n, chunk_rows):
    any_spec = pl.BlockSpec(memory_space=pl.ANY)
    grid_spec = pltpu.PrefetchScalarGridSpec(
        num_scalar_prefetch=9,
        grid=(MOE_NBLK,),
        in_specs=[any_spec, any_spec, any_spec, any_spec],
        out_specs=any_spec,
        scratch_shapes=[pltpu.VMEM((2, MOE_BLK, D_HALF), U32), pltpu.VMEM((2, MOE_BLK, D_HALF), U32),
                        pltpu.VMEM((MOE_LS - 2 * MOE_TS, D_HALF), U32),
                        pltpu.VMEM((D_MODEL, D_EXPERT), F32), pltpu.VMEM((D_MODEL, D_EXPERT), F32),
                        pltpu.VMEM((D_EXPERT, D_MODEL), F32),
                        pltpu.VMEM((D_MODEL, D_EXPERT), BF16), pltpu.VMEM((D_MODEL, D_EXPERT), BF16),
                        pltpu.VMEM((D_EXPERT, D_MODEL), BF16),
                        pltpu.SemaphoreType.DMA((2,)), pltpu.SemaphoreType.DMA((2,)),
                        pltpu.SemaphoreType.DMA, pltpu.SemaphoreType.DMA])
    return pl.pallas_call(
        functools.partial(_moe_kernel, layer),
        grid_spec=grid_spec,
        out_shape=jax.ShapeDtypeStruct((NB_ALL * MOE_LC, D_HALF), U32),
        compiler_params=_cparams(("arbitrary",)),
        name="moe_experts",
    )(blk_e, blk_next_e, blk_nv, blk_c0, blk_c1, piece_src, piece_dst, piece_n, chunk_rows,
      xs, w1, w3, w2)


def _final_kernel(x_ref, ys_ref, r_ref, mod_ref, g_ref, o_ref):
    x = _moe_residual(x_ref, ys_ref, r_ref, mod_ref)
    ms = jnp.mean(x * x, axis=-1, keepdims=True)
    o_ref[...] = x * lax.rsqrt(ms + EPS) * g_ref[...]


def _final(x, ys, route, mod, g, block0, nblocks):
    return pl.pallas_call(
        _final_kernel,
        grid=(nblocks,),
        in_specs=[pl.BlockSpec((TM, D_MODEL), lambda i: (block0 + i, 0)),
                  pl.BlockSpec((MOE_LC, D_HALF), lambda i: (block0 + i, 0)),
                  pl.BlockSpec((TM, ROUTE_COLS), lambda i: (block0 + i, 0)),
                  pl.BlockSpec((None, 6, D_MODEL), lambda i: (_cond_row(block0 + i), 0, 0)),
                  _full_spec((1, D_MODEL))],
        out_specs=_tok_spec(D_MODEL),
        out_shape=jax.ShapeDtypeStruct((nblocks * TM, D_MODEL), F32),
        compiler_params=_cparams(("arbitrary",)),
        name="final_norm",
    )(x, ys, route, mod, g)


def kernel(x_prompt, x_sample, c, cache_k, cache_v, state_ret_f, state_ret_b, c_ctx, w_ada, b_ada, norm1_g, norm2_g, w_in, w_out, conv_w, conv_b, conv_ln_g, conv_ln_b, na_rpb, ret_lg_f, ret_lg_b, ret_gn_g, w_route_g, b_route_g, w_route_e, b_route_e, w1, w3, w2, final_g):
    cv = jnp.zeros((COND_ROWS, D_MODEL), F32).at[0].set(c_ctx).at[1:N_COND].set(c)
    mods = _ada(cv, w_ada, b_ada).reshape(DEPTH, COND_ROWS, 6, D_MODEL)
    w_in_b = w_in.astype(BF16)
    w_out_b = w_out.astype(BF16)
    pad = ROUTE_COLS - N_GROUPS - N_EXPERTS
    w_route = jnp.pad(jnp.concatenate([w_route_g, w_route_e], axis=-1), ((0, 0), (0, 0), (0, pad)))
    b_route = jnp.pad(jnp.concatenate([b_route_g, b_route_e], axis=-1), ((0, 0), (0, pad)))
    w_route_hi = w_route.astype(BF16)
    w_route_lo = (w_route - w_route_hi.astype(F32)).astype(BF16)
    w_route = jnp.concatenate([w_route_hi, w_route_lo], axis=-1)
    na_bias = _na_bias_tables(na_rpb)
    rope = _rope_tables()
    lg = jnp.stack([ret_lg_f, ret_lg_b], axis=1)

    x_ctx = x_prompt.reshape(T_CTX, D_MODEL)
    x_lat = x_sample.reshape(T_LAT, D_MODEL)
    x = y = route = new_k = new_v = None
    sf_list, sb_list = [], []
    for l in range(DEPTH):
        g1 = norm1_g[l].reshape(1, D_MODEL)
        if l == 0:
            z, x = _inproj_first(x_ctx, x_lat, mods[l], g1, w_in_b[l])
        else:
            z, x = _inproj_next(x, y, route, mods[l - 1], mods[l], g1, w_in_b[l])
        conv_args = (conv_w[l], conv_b[l].reshape(1, -1), conv_ln_g[l].reshape(1, -1),
                     conv_ln_b[l].reshape(1, -1))
        yc_c = _conv(z, 0, BATCH, SEQ, *conv_args)
        yc_l = _conv(z, T_CTX // DEC_SEQ, DEC_BATCH, DEC_SEQ, *conv_args)
        yn_c, new_k, new_v = _ctx_attn(z, l, new_k, new_v)
        yn_l = _na_attn(z, cache_k, cache_v, na_bias, l)
        gn = ret_gn_g[l].reshape(1, RET_WIDTH)
        yr_c, sf_l, sb_l = _retention(z, lg[l], gn, latent=False)
        yr_l = _retention(z, lg[l], gn, latent=True, layer=l, rope=rope,
                          s0_f=state_ret_f, s0_b=state_ret_b)
        x, xs, route, seg = _outproj((yc_c, yc_l), (yn_c, yn_l), (yr_c, yr_l), x, mods[l],
                                     norm2_g[l].reshape(1, D_MODEL), w_out_b[l], w_route[l],
                                     b_route[l].reshape(1, ROUTE_COLS))
        y = _moe(xs, w1, w3, w2, l, *_dispatch_tables(seg))
        sf_list.append(sf_l)
        sb_list.append(sb_l)
    fg = final_g.reshape(1, D_MODEL)
    y_prompt = _final(x, y, route, mods[DEPTH - 1], fg, 0, NB_CTX).reshape(BATCH, SEQ, D_MODEL)
    y_sample = _final(x, y, route, mods[DEPTH - 1], fg, NB_CTX, NB_LAT).reshape(DEC_BATCH, DEC_SEQ, D_MODEL)
    return (y_prompt, y_sample, new_k, new_v, jnp.stack(sf_list, axis=1), jnp.stack(sb_list, axis=1))
```

```python
import functools

import numpy as np
import jax
import jax.numpy as jnp
from jax import lax
from jax.experimental import pallas as pl
from jax.experimental.pallas import tpu as pltpu

D_MODEL = 1024
BATCH = 32
SEQ = 256
DEPTH = 2
DEC_BATCH = 4
DEC_SEQ = 4096
PAST_LEN = 512
GRID_W = 64
GRID_H = DEC_SEQ // GRID_W
CONV_CH = 256
CONV_K = 31
NA_HEADS = 8
NA_DIM = 64
NA_WIDTH = NA_HEADS * NA_DIM
NA_KH = 8
NA_KW = 16
RET_HEADS = 4
RET_DIM = 64
RET_WIDTH = RET_HEADS * RET_DIM
RET_CHUNK = 128
ROPE_BASE = 10000.0
N_GROUPS = 4
EXPERTS_PER_GROUP = 8
N_EXPERTS = N_GROUPS * EXPERTS_PER_GROUP
D_EXPERT = 512
IN_COLS = 2 * CONV_CH + 3 * NA_WIDTH + 4 * RET_WIDTH
EPS = 1e-6
NEG_INF = -1e30

F32 = jnp.float32
BF16 = jnp.bfloat16
HIGHEST = lax.Precision.HIGHEST

T_CTX = BATCH * SEQ
T_LAT = DEC_BATCH * DEC_SEQ
T_ALL = T_CTX + T_LAT
N_COND = 1 + DEC_BATCH
COND_ROWS = 8

TM = 512
NB_CTX = T_CTX // TM
NB_LAT = T_LAT // TM
NB_ALL = NB_CTX + NB_LAT
LAT_BLOCKS_PER_REQ = DEC_SEQ // TM

LANES = 128
SUBLANES = 8
MXU_TILE = 256
ROUTE_COLS = LANES

COL_CONV = 0
COL_NA_Q = 2 * CONV_CH
COL_NA_K = COL_NA_Q + NA_WIDTH
COL_NA_V = COL_NA_K + NA_WIDTH
COL_RET = COL_NA_V + NA_WIDTH

NA_ROWS = 8
NA_Q = NA_ROWS * GRID_W
NA_KROWS = NA_ROWS + NA_KH
NA_KEYS = NA_KROWS * GRID_W
NA_RB = GRID_H // NA_ROWS

MOE_BLK = 512
MOE_ROW_STEP = 128
MOE_LC = -(-(2 * TM + N_EXPERTS * (SUBLANES - 1)) // LANES) * LANES
N_CHUNK = NB_ALL
MOE_NBLK = -(-(N_CHUNK * MOE_LC) // MOE_BLK) + N_EXPERTS

VMEM_LIMIT = 56 * 1024 * 1024


def _cparams(sem):
    return pltpu.CompilerParams(dimension_semantics=sem, vmem_limit_bytes=VMEM_LIMIT)


def _sigmoid(x):
    return 1.0 / (1.0 + jnp.exp(-x))


def _cond_row(i):
    return jnp.where(i < NB_CTX, 0, 1 + (i - NB_CTX) // LAT_BLOCKS_PER_REQ)


ADA_TN = 1536


def _ada_kernel(cv_ref, w_ref, b_ref, o_ref):
    cv = cv_ref[...]
    s = cv * _sigmoid(cv)
    o_ref[...] = jnp.dot(s, w_ref[...], precision=HIGHEST, preferred_element_type=F32) + b_ref[...]


def _ada(cv, w_ada, b_ada):
    n = 6 * D_MODEL
    return pl.pallas_call(
        _ada_kernel,
        grid=(DEPTH, n // ADA_TN),
        in_specs=[
            pl.BlockSpec((COND_ROWS, D_MODEL), lambda l, j: (0, 0)),
            pl.BlockSpec((None, D_MODEL, ADA_TN), lambda l, j: (l, 0, j)),
            pl.BlockSpec((None, 1, ADA_TN), lambda l, j: (l, 0, j)),
        ],
        out_specs=pl.BlockSpec((None, COND_ROWS, ADA_TN), lambda l, j: (l, 0, j)),
        out_shape=jax.ShapeDtypeStruct((DEPTH, COND_ROWS, n), F32),
        compiler_params=_cparams(("arbitrary", "arbitrary")),
        name="ada_mod",
    )(cv, w_ada, b_ada.reshape(DEPTH, 1, n))


IN_TN = 768


def _norm_mod(x, g, shift, scale):
    ms = jnp.mean(x * x, axis=-1, keepdims=True)
    return (x * lax.rsqrt(ms + EPS) * g) * (1.0 + scale) + shift


def _inproj_body(x, mod_ref, g_ref, w_ref, z_ref):
    h = _norm_mod(x, g_ref[...], mod_ref[0:1, :], mod_ref[1:2, :]).astype(BF16)
    for c in range(IN_COLS // IN_TN):
        cols = slice(c * IN_TN, (c + 1) * IN_TN)
        z_ref[:, cols] = jnp.dot(h, w_ref[:, cols], preferred_element_type=F32).astype(BF16)


def _inproj_first_kernel(xc_ref, xl_ref, mod_ref, g_ref, w_ref, z_ref, xo_ref):
    i = pl.program_id(0)
    x = jnp.where(i < NB_CTX, xc_ref[...], xl_ref[...])
    xo_ref[...] = x
    _inproj_body(x, mod_ref, g_ref, w_ref, z_ref)


U32 = jnp.uint32
D_HALF = D_MODEL // 2
_HI_MASK = np.uint32(0xFFFF0000)


def _pack_words(lo, hi):
    lo = lax.bitcast_convert_type(lo.astype(BF16).astype(F32), U32) >> 16
    hi = lax.bitcast_convert_type(hi.astype(BF16).astype(F32), U32) & _HI_MASK
    return lo | hi


def _pack_bf16_pairs(x):
    return _pack_words(x[:, :D_HALF], x[:, D_HALF:])


def _unpack_bf16_pairs(w):
    lo = lax.bitcast_convert_type(w << 16, F32).astype(BF16)
    hi = lax.bitcast_convert_type(w & _HI_MASK, F32).astype(BF16)
    return lo, hi


def _slot_onehot(route, slot):
    pos = route[:, 4 + slot:5 + slot].astype(jnp.int32)
    return lax.broadcasted_iota(jnp.int32, (route.shape[0], MOE_LC), 1) == pos


def _moe_residual(x_ref, ys_ref, r_ref, mod_ref):
    r = r_ref[...]
    sel = jnp.where(_slot_onehot(r, 0), r[:, 2:3], jnp.where(_slot_onehot(r, 1), r[:, 3:4], 0.0))
    sel = sel.astype(BF16)
    y = jnp.concatenate([jnp.dot(sel, half, preferred_element_type=F32)
                         for half in _unpack_bf16_pairs(ys_ref[...])], axis=-1)
    return x_ref[...] + mod_ref[5:6, :] * y


def _inproj_next_kernel(x_ref, ys_ref, r_ref, modp_ref, mod_ref, g_ref, w_ref, z_ref, xo_ref):
    x = _moe_residual(x_ref, ys_ref, r_ref, modp_ref)
    xo_ref[...] = x
    _inproj_body(x, mod_ref, g_ref, w_ref, z_ref)


def _tok_spec(cols):
    return pl.BlockSpec((TM, cols), lambda i: (i, 0))


def _mod_spec():
    return pl.BlockSpec((None, 6, D_MODEL), lambda i: (_cond_row(i), 0, 0))


def _full_spec(shape):
    return pl.BlockSpec(shape, lambda i: (0,) * len(shape))


def _ctx_lat_specs(cols):
    return [pl.BlockSpec((TM, cols), lambda i: (jnp.minimum(i, NB_CTX - 1), 0)),
            pl.BlockSpec((TM, cols), lambda i: (jnp.maximum(i - NB_CTX, 0), 0))]


def _inproj_first(x_ctx, x_lat, mod, g, w_bf16):
    return pl.pallas_call(
        _inproj_first_kernel,
        grid=(NB_ALL,),
        in_specs=_ctx_lat_specs(D_MODEL) + [_mod_spec(), _full_spec((1, D_MODEL)),
                                            _full_spec((D_MODEL, IN_COLS))],
        out_specs=[_tok_spec(IN_COLS), _tok_spec(D_MODEL)],
        out_shape=[jax.ShapeDtypeStruct((T_ALL, IN_COLS), BF16),
                   jax.ShapeDtypeStruct((T_ALL, D_MODEL), F32)],
        compiler_params=_cparams(("arbitrary",)),
        name="inproj_first",
    )(x_ctx, x_lat, mod, g, w_bf16)


def _inproj_next(x, ys, route, mod_prev, mod, g, w_bf16):
    return pl.pallas_call(
        _inproj_next_kernel,
        grid=(NB_ALL,),
        in_specs=[_tok_spec(D_MODEL),
                  pl.BlockSpec((MOE_LC, D_HALF), lambda i: (i, 0)),
                  _tok_spec(ROUTE_COLS),
                  _mod_spec(), _mod_spec(), _full_spec((1, D_MODEL)),
                  _full_spec((D_MODEL, IN_COLS))],
        out_specs=[_tok_spec(IN_COLS), _tok_spec(D_MODEL)],
        out_shape=[jax.ShapeDtypeStruct((T_ALL, IN_COLS), BF16),
                   jax.ShapeDtypeStruct((T_ALL, D_MODEL), F32)],
        compiler_params=_cparams(("arbitrary",)),
        name="inproj_next",
    )(x, ys, route, mod_prev, mod, g, w_bf16)


CONV_PAD = 16
CONV_CHUNK = 64


CONV_SPAN = CONV_CHUNK + 2 * CONV_PAD - SUBLANES


CONV_UNROLL = 4


def _conv_kernel(seq, z_ref, w_ref, b_ref, g_ref, be_ref, o_ref, upad_ref, shift_refs):
    zeros = jnp.zeros((CONV_PAD, CONV_CH), F32)
    upad_ref[0:CONV_PAD, :] = zeros
    upad_ref[seq + CONV_PAD:seq + 2 * CONV_PAD, :] = zeros

    def glu(ci, carry):
        base = pl.multiple_of(ci * 256, 256)
        zc = z_ref[pl.ds(base, 256), :].astype(F32)
        upad_ref[pl.ds(base + CONV_PAD, 256), :] = zc[:, :CONV_CH] * _sigmoid(zc[:, CONV_CH:])
        return carry

    lax.fori_loop(0, seq // 256, glu, 0)

    shift = CONV_PAD - CONV_K // 2

    def chunk(ci, shift_ref):
        base = pl.multiple_of(ci * CONV_CHUNK, CONV_CHUNK)
        win = upad_ref[pl.ds(base, CONV_CHUNK + 2 * CONV_PAD), :]
        acc = jnp.zeros((CONV_CHUNK, CONV_CH), F32)
        for sub in range(SUBLANES):
            shift_ref[sub] = win[sub:sub + CONV_SPAN, :]
            for k in range(CONV_K):
                if (k + shift) % SUBLANES == sub:
                    lo = k + shift - sub
                    acc = acc + w_ref[k:k + 1, :] * shift_ref[sub, lo:lo + CONV_CHUNK, :]
        acc = acc + b_ref[...]
        mu = jnp.mean(acc, axis=-1, keepdims=True)
        d = acc - mu
        var = jnp.mean(d * d, axis=-1, keepdims=True)
        n = d * lax.rsqrt(var + EPS) * g_ref[...] + be_ref[...]
        o_ref[pl.ds(base, CONV_CHUNK), :] = (n * _sigmoid(n)).astype(BF16)

    def chunks(cj, carry):
        for u in range(CONV_UNROLL):
            chunk(cj * CONV_UNROLL + u, shift_refs.at[u])
        return carry

    lax.fori_loop(0, seq // (CONV_CHUNK * CONV_UNROLL), chunks, 0)


def _conv(z, row_block0, nseq, seq, w, b, g, be):
    return pl.pallas_call(
        functools.partial(_conv_kernel, seq),
        grid=(nseq,),
        in_specs=[pl.BlockSpec((seq, 2 * CONV_CH), lambda s: (row_block0 + s, 0)),
                  _full_spec((CONV_K, CONV_CH)), _full_spec((1, CONV_CH)),
                  _full_spec((1, CONV_CH)), _full_spec((1, CONV_CH))],
        out_specs=pl.BlockSpec((seq, CONV_CH), lambda s: (s, 0)),
        out_shape=jax.ShapeDtypeStruct((nseq * seq, CONV_CH), BF16),
        scratch_shapes=[pltpu.VMEM((seq + 2 * CONV_PAD, CONV_CH), F32),
                        pltpu.VMEM((CONV_UNROLL, SUBLANES, CONV_SPAN, CONV_CH), F32)],
        compiler_params=_cparams(("arbitrary",)),
        name="conv_seq%d" % seq,
    )(z, w, b, g, be)


def _dot_nt(a, b):
    return lax.dot_general(a, b, (((1,), (1,)), ((), ())), preferred_element_type=F32)


NA_SCALE = NA_DIM ** -0.5
assert NA_SCALE == 2.0 ** round(np.log2(NA_SCALE)), "query pre-scaling assumes a power-of-two scale"


def _ctx_attn_kernel(layer, q_ref, k_ref, v_ref, *refs):
    if layer:
        kprev_ref, vprev_ref, o_ref, ko_ref, vo_ref = refs
    else:
        o_ref, ko_ref, vo_ref = refs
    for j in range(DEPTH):
        if j < layer:
            ko_ref[j] = kprev_ref[j]
            vo_ref[j] = vprev_ref[j]
        elif j > layer:
            ko_ref[j] = jnp.zeros(ko_ref.shape[1:], F32)
            vo_ref[j] = jnp.zeros(vo_ref.shape[1:], F32)
    outs = []
    for h in range(NA_HEADS):
        cols = slice(h * NA_DIM, (h + 1) * NA_DIM)
        qh, kh, vh = q_ref[:, cols], k_ref[:, cols], v_ref[:, cols]
        ko_ref[layer, h] = kh.astype(F32)
        vo_ref[layer, h] = vh.astype(F32)
        s = _dot_nt(qh, kh) * NA_SCALE
        m = jnp.max(s, axis=-1, keepdims=True)
        p = jnp.exp(s - m)
        den = jnp.sum(p, axis=-1, keepdims=True)
        o = jnp.dot(p.astype(BF16), vh, preferred_element_type=F32)
        outs.append(o / den)
    o_ref[...] = jnp.concatenate(outs, axis=-1).astype(BF16)


def _ctx_attn(z, layer, k_prev=None, v_prev=None):
    qb, kb, vb = COL_NA_Q // NA_WIDTH, COL_NA_K // NA_WIDTH, COL_NA_V // NA_WIDTH
    head_shape = jax.ShapeDtypeStruct((BATCH, DEPTH, NA_HEADS, SEQ, NA_DIM), F32)
    head_spec = pl.BlockSpec((None, DEPTH, NA_HEADS, SEQ, NA_DIM), lambda b: (b, 0, 0, 0, 0))
    in_specs = [pl.BlockSpec((SEQ, NA_WIDTH), lambda b: (b, qb)),
                pl.BlockSpec((SEQ, NA_WIDTH), lambda b: (b, kb)),
                pl.BlockSpec((SEQ, NA_WIDTH), lambda b: (b, vb))]
    args = [z, z, z]
    aliases = {}
    if layer:
        in_specs += [head_spec, head_spec]
        args += [k_prev, v_prev]
        aliases = {3: 1, 4: 2}
    return pl.pallas_call(
        functools.partial(_ctx_attn_kernel, layer),
        grid=(BATCH,),
        in_specs=in_specs,
        out_specs=[pl.BlockSpec((SEQ, NA_WIDTH), lambda b: (b, 0)), head_spec, head_spec],
        out_shape=[jax.ShapeDtypeStruct((T_CTX, NA_WIDTH), BF16), head_shape, head_shape],
        input_output_aliases=aliases,
        compiler_params=_cparams(("arbitrary",)),
        name="ctx_attn",
    )(*args)


NA_KINDS = (0, NA_ROWS, GRID_H - NA_ROWS)
N_DR = 2 * NA_KH - 1
N_DC = 2 * NA_KW - 1


def _na_row_offset(r0, i, j):
    ks = min(max(r0 - NA_KH // 2, 0), GRID_H - NA_KROWS)
    r, kr = r0 + i, ks + j
    rs = min(max(r - NA_KH // 2, 0), GRID_H - NA_KH)
    return kr - r + NA_KH - 1 if rs <= kr < rs + NA_KH else None


def _na_bias_kernel(rpb_ref, o_ref):
    lh = pl.program_id(0)
    shape = (GRID_W, 2 * GRID_W)
    qc = lax.broadcasted_iota(jnp.int32, shape, 0)
    lane = lax.broadcasted_iota(jnp.int32, shape, 1)
    kc = lane % GRID_W
    dc = jnp.clip(kc - qc, -(NA_KW - 1), NA_KW - 1) + NA_KW - 1
    cs = jnp.clip(qc - NA_KW // 2, 0, GRID_W - NA_KW)
    col_ok = (kc >= cs) & (kc < cs + NA_KW)
    neg = jnp.full(shape, NEG_INF, F32)
    tiles = []
    for dr in range(N_DR):
        base = (lh * N_DR + dr) * N_DC
        val = jnp.zeros(shape, F32)
        for d in range(N_DC):
            val = jnp.where(dc == d, rpb_ref[base + d], val)
        tiles.append(jnp.where(col_ok, val, neg))
    left = lane < GRID_W
    for kind, r0 in enumerate(NA_KINDS):
        for i in range(NA_ROWS):
            for jp in range(NA_KROWS // 2):
                dl, dr_ = _na_row_offset(r0, i, 2 * jp), _na_row_offset(r0, i, 2 * jp + 1)
                tl = neg if dl is None else tiles[dl]
                tr = neg if dr_ is None else tiles[dr_]
                o_ref[kind, i * GRID_W:(i + 1) * GRID_W, jp * 2 * GRID_W:(jp + 1) * 2 * GRID_W] = (
                    jnp.where(left, tl, tr))


def _na_bias_tables(rpb):
    return pl.pallas_call(
        _na_bias_kernel,
        grid=(DEPTH * NA_HEADS,),
        in_specs=[pl.BlockSpec(memory_space=pltpu.SMEM)],
        out_specs=pl.BlockSpec((None, len(NA_KINDS), NA_Q, NA_KEYS), lambda i: (i, 0, 0, 0)),
        out_shape=jax.ShapeDtypeStruct((DEPTH * NA_HEADS, len(NA_KINDS), NA_Q, NA_KEYS), F32),
        compiler_params=_cparams(("arbitrary",)),
        name="nbr_bias",
    )(rpb.reshape(-1))


NA_G = 4


def _na_kernel(q_ref, k_ref, v_ref, kc_ref, vc_ref, bias_ref, o_ref):
    rb = pl.program_id(2)
    ks = jnp.clip(rb * NA_ROWS - NA_KH // 2, 0, GRID_H - NA_KROWS)
    start = pl.multiple_of(ks * GRID_W, GRID_W)
    q = q_ref[...] * NA_SCALE
    kl = k_ref[pl.ds(start, NA_KEYS), :]
    vl = v_ref[pl.ds(start, NA_KEYS), :]
    ones_loc = jnp.ones((NA_KEYS, NA_DIM), BF16)
    ones_ctx = jnp.ones((PAST_LEN, NA_DIM), BF16)

    def scores(hh):
        cols = slice(hh * NA_DIM, (hh + 1) * NA_DIM)
        qh = q[:, cols]
        return _dot_nt(qh, kl[:, cols]) + bias_ref[hh], _dot_nt(qh, kc_ref[hh].astype(BF16))

    outs = []
    nxt = scores(0)
    for hh in range(NA_G):
        s_loc, s_ctx = nxt
        if hh + 1 < NA_G:
            nxt = scores(hh + 1)
        cols = slice(hh * NA_DIM, (hh + 1) * NA_DIM)
        m = jnp.maximum(jnp.max(s_loc, axis=-1, keepdims=True), jnp.max(s_ctx, axis=-1, keepdims=True))
        p_loc = jnp.exp(s_loc - m).astype(BF16)
        p_ctx = jnp.exp(s_ctx - m).astype(BF16)
        v_ext = jnp.concatenate([vl[:, cols], ones_loc], axis=-1)
        vc_ext = jnp.concatenate([vc_ref[hh].astype(BF16), ones_ctx], axis=-1)
        o = (jnp.dot(p_loc, v_ext, preferred_element_type=F32)
             + jnp.dot(p_ctx, vc_ext, preferred_element_type=F32))
        outs.append(o[:, :NA_DIM] / o[:, NA_DIM:])
    o_ref[...] = jnp.concatenate(outs, axis=-1).astype(BF16)


def _na_attn(z, cache_k, cache_v, bias, layer):
    lat_q0 = T_CTX // NA_Q
    lat_s0 = T_CTX // DEC_SEQ
    width = NA_G * NA_DIM
    qc, kc, vc = COL_NA_Q // width, COL_NA_K // width, COL_NA_V // width
    groups = NA_HEADS // NA_G

    def kind(rb):
        return jnp.where(rb == 0, 0, jnp.where(rb == NA_RB - 1, 2, 1))

    ctx_spec = pl.BlockSpec((None, None, NA_G, PAST_LEN, NA_DIM), lambda b, hg, rb: (b, layer, hg, 0, 0))
    return pl.pallas_call(
        _na_kernel,
        grid=(DEC_BATCH, groups, NA_RB),
        in_specs=[pl.BlockSpec((NA_Q, width), lambda b, hg, rb: (lat_q0 + b * NA_RB + rb, qc + hg)),
                  pl.BlockSpec((DEC_SEQ, width), lambda b, hg, rb: (lat_s0 + b, kc + hg)),
                  pl.BlockSpec((DEC_SEQ, width), lambda b, hg, rb: (lat_s0 + b, vc + hg)),
                  ctx_spec, ctx_spec,
                  pl.BlockSpec((NA_G, None, NA_Q, NA_KEYS),
                               lambda b, hg, rb: (layer * groups + hg, kind(rb), 0, 0))],
        out_specs=pl.BlockSpec((NA_Q, width), lambda b, hg, rb: (b * NA_RB + rb, hg)),
        out_shape=jax.ShapeDtypeStruct((T_LAT, NA_WIDTH), BF16),
        compiler_params=_cparams(("arbitrary", "arbitrary", "arbitrary")),
        name="nbr_attn",
    )(z, z, z, cache_k, cache_v, bias)


RET_PAIR = 2 * RET_DIM
RET_NPAIR = RET_HEADS // 2
assert RET_PAIR == LANES and RET_CHUNK == LANES
RET_UNROLL = 8


def _rope_tables():
    n_freq = RET_DIM // 4
    t = np.arange(DEC_SEQ)
    inv = jnp.asarray(ROPE_BASE, F32) ** (-jnp.arange(n_freq, dtype=F32) / n_freq)
    ang_r = jnp.asarray(t // GRID_W, F32)[:, None] * inv[None, :]
    ang_c = jnp.asarray(t % GRID_W, F32)[:, None] * inv[None, :]
    cos = jnp.concatenate([jnp.cos(ang_r)] * 2 + [jnp.cos(ang_c)] * 2, axis=-1)
    sin = jnp.concatenate([-jnp.sin(ang_r), jnp.sin(ang_r), -jnp.sin(ang_c), jnp.sin(ang_c)], axis=-1)
    lane = np.arange(RET_WIDTH)
    src = np.where(lane % (2 * n_freq) < n_freq, lane + n_freq, lane - n_freq)
    swap = np.zeros((RET_WIDTH, RET_WIDTH), np.float32)
    swap[src, lane] = 1.0
    return jnp.tile(cos, (1, RET_HEADS)), jnp.tile(sin, (1, RET_HEADS)), jnp.asarray(swap, BF16)


def _ret_kernel(seq, latent, *refs):
    if latent:
        (lg_ref, z_ref, gn_ref, cos_ref, sin_ref, swap_ref, s0f_ref, s0b_ref, y_ref,
         q_s, k_s, kv_s, st_s) = refs
    else:
        lg_ref, z_ref, gn_ref, y_ref, sf_ref, sb_ref, q_s, k_s, kv_s, st_s = refs
    nc = seq // RET_CHUNK
    ch, hd, pw = RET_CHUNK, RET_DIM, RET_PAIR

    row = lax.broadcasted_iota(jnp.int32, (ch, ch), 0).astype(F32)
    col = lax.broadcasted_iota(jnp.int32, (ch, ch), 1).astype(F32)
    pos = lax.broadcasted_iota(jnp.int32, (ch, pw), 0).astype(F32)
    left = lax.broadcasted_iota(jnp.int32, (ch, pw), 1) < hd
    top = lax.broadcasted_iota(jnp.int32, (pw, pw), 0) < hd
    same_head = top == (lax.broadcasted_iota(jnp.int32, (pw, pw), 1) < hd)
    same_head2 = jnp.concatenate([same_head, same_head], axis=0)

    def per_head(mask, fn, p):
        return jnp.where(mask, fn(2 * p), fn(2 * p + 1))

    decay = []
    for h in range(RET_HEADS):
        lf, lb = lg_ref[0, h], lg_ref[1, h]
        d_f = jnp.where(row >= col, jnp.exp(jnp.maximum(row - col, 0.0) * lf), 0.0)
        d_b = jnp.where(col >= row, jnp.exp(jnp.maximum(col - row, 0.0) * lb), 0.0)
        decay.append(d_f + d_b)
    q_dec, k_dec, c_dec_f, c_dec_b = [], [], [], []
    for p in range(RET_NPAIR):
        q_dec.append(jnp.concatenate(
            [per_head(left, lambda h: jnp.exp((pos + 1.0) * lg_ref[0, h]), p),
             per_head(left, lambda h: jnp.exp((ch - pos) * lg_ref[1, h]), p)], axis=-1))
        k_dec.append(jnp.concatenate(
            [per_head(left, lambda h: jnp.exp((ch - 1.0 - pos) * lg_ref[0, h]), p),
             per_head(left, lambda h: jnp.exp(pos * lg_ref[1, h]), p)], axis=-1))
        zero = jnp.zeros((pw, pw), F32)
        c_dec_f.append(per_head(top, lambda h: jnp.exp(zero + ch * lg_ref[0, h]), p))
        c_dec_b.append(per_head(top, lambda h: jnp.exp(zero + ch * lg_ref[1, h]), p))

    def rope(x, base):
        xf = x.astype(F32)
        if not latent:
            return xf
        swapped = jnp.dot(x, swap_ref[...], preferred_element_type=F32)
        return xf * cos_ref[pl.ds(base, ch), :] + swapped * sin_ref[pl.ds(base, ch), :]

    def pass1(n, carry):
        base = pl.multiple_of(n * ch, ch)
        zc = z_ref[pl.ds(base, ch), :]
        q = rope(zc[:, 0:RET_WIDTH], base)
        k = rope(zc[:, RET_WIDTH:2 * RET_WIDTH], base) * (RET_DIM ** -0.5)
        q_s[pl.ds(base, ch), :] = q.astype(BF16)
        k_s[pl.ds(base, ch), :] = k.astype(BF16)
        v = zc[:, 2 * RET_WIDTH:3 * RET_WIDTH]
        for p in range(RET_NPAIR):
            lanes = slice(p * pw, (p + 1) * pw)
            kp = k[:, lanes]
            k2 = (jnp.concatenate([kp, kp], axis=-1) * k_dec[p]).astype(BF16)
            kv = lax.dot_general(k2, v[:, lanes], (((0,), (0,)), ((), ())), preferred_element_type=F32)
            kv_s[n, p] = jnp.where(same_head2, kv, 0.0)
        return carry

    lax.fori_loop(0, nc, pass1, 0, unroll=min(RET_UNROLL, nc))

    def block_diag(a, b):
        z = jnp.zeros((hd, hd), F32)
        return jnp.concatenate([jnp.concatenate([a, z], axis=1), jnp.concatenate([z, b], axis=1)], axis=0)

    for p in range(RET_NPAIR):
        if latent:
            s_f = block_diag(s0f_ref[2 * p], s0f_ref[2 * p + 1])
            s_b = block_diag(s0b_ref[2 * p], s0b_ref[2 * p + 1])
        else:
            s_f = s_b = jnp.zeros((pw, pw), F32)

        def fwd(n, s, p=p):
            st_s[n, p, 0:pw, :] = s.astype(BF16)
            return c_dec_f[p] * s + kv_s[n, p, 0:pw, :]

        def bwd(i, s, p=p):
            n = nc - 1 - i
            st_s[n, p, pw:2 * pw, :] = s.astype(BF16)
            return c_dec_b[p] * s + kv_s[n, p, pw:2 * pw, :]

        s_f = lax.fori_loop(0, nc, fwd, s_f)
        s_b = lax.fori_loop(0, nc, bwd, s_b)
        if not latent:
            for hh in range(2):
                blk = slice(hh * hd, (hh + 1) * hd)
                sf_ref[2 * p + hh] = s_f[blk, blk]
                sb_ref[2 * p + hh] = s_b[blk, blk]

    def pass3(n, carry):
        base = pl.multiple_of(n * ch, ch)
        zc = z_ref[pl.ds(base, ch), :]
        q = q_s[pl.ds(base, ch), :]
        k = k_s[pl.ds(base, ch), :]
        v = zc[:, 2 * RET_WIDTH:3 * RET_WIDTH]
        gate = zc[:, 3 * RET_WIDTH:4 * RET_WIDTH].astype(F32)
        outs = []
        for p in range(RET_NPAIR):
            lanes = slice(p * pw, (p + 1) * pw)
            qp, kp, vp = q[:, lanes], k[:, lanes], v[:, lanes]
            o_h = []
            for hh in range(2):
                qm = jnp.where(left == (hh == 0), qp, jnp.zeros_like(qp))
                s = _dot_nt(qm, kp) * decay[2 * p + hh]
                o_h.append(jnp.dot(s.astype(BF16), vp, preferred_element_type=F32))
            qf = qp.astype(F32)
            q2 = (jnp.concatenate([qf, qf], axis=-1) * q_dec[p]).astype(BF16)
            o = jnp.where(left, o_h[0], o_h[1]) + jnp.dot(q2, st_s[n, p], preferred_element_type=F32)

            def half_mean(t):
                s_l = jnp.sum(jnp.where(left, t, 0.0), axis=-1, keepdims=True)
                s_r = jnp.sum(jnp.where(left, 0.0, t), axis=-1, keepdims=True)
                return jnp.where(left, s_l, s_r) * (1.0 / hd)

            d = o - half_mean(o)
            outs.append(d * lax.rsqrt(half_mean(d * d) + EPS))
        nrm = jnp.concatenate(outs, axis=-1)
        y_ref[pl.ds(base, ch), :] = (nrm * gn_ref[...] * (gate * _sigmoid(gate))).astype(BF16)
        return carry

    lax.fori_loop(0, nc, pass3, 0, unroll=min(RET_UNROLL, nc))


def _retention(z, lg, gn_g, latent, layer=None, rope=None, s0_f=None, s0_b=None):
    seq = DEC_SEQ if latent else SEQ
    nseq = DEC_BATCH if latent else BATCH
    nc = seq // RET_CHUNK
    row0 = (T_CTX // DEC_SEQ) if latent else 0
    cb = COL_RET // (4 * RET_WIDTH)
    in_specs = [pl.BlockSpec(memory_space=pltpu.SMEM),
                pl.BlockSpec((seq, 4 * RET_WIDTH), lambda s: (row0 + s, cb)),
                _full_spec((1, RET_WIDTH))]
    args = [lg, z, gn_g]
    state_shape = jax.ShapeDtypeStruct((nseq, RET_HEADS, RET_DIM, RET_DIM), F32)
    y_spec = pl.BlockSpec((seq, RET_WIDTH), lambda s: (s, 0))
    y_shape = jax.ShapeDtypeStruct((nseq * seq, RET_WIDTH), BF16)
    if latent:
        st_spec = pl.BlockSpec((None, None, RET_HEADS, RET_DIM, RET_DIM), lambda s: (s, layer, 0, 0, 0))

        def const_spec(shape):
            return pl.BlockSpec(shape, lambda s: (0,) * len(shape), pipeline_mode=pl.Buffered(1))

        in_specs += [const_spec((seq, RET_WIDTH)), const_spec((seq, RET_WIDTH)),
                     const_spec((RET_WIDTH, RET_WIDTH)), st_spec, st_spec]
        args += [rope[0], rope[1], rope[2], s0_f, s0_b]
        out_specs, out_shape = y_spec, y_shape
    else:
        so_spec = pl.BlockSpec((None, RET_HEADS, RET_DIM, RET_DIM), lambda s: (s, 0, 0, 0))
        out_specs, out_shape = [y_spec, so_spec, so_spec], [y_shape, state_shape, state_shape]
    return pl.pallas_call(
        functools.partial(_ret_kernel, seq, latent),
        grid=(nseq,),
        in_specs=in_specs,
        out_specs=out_specs,
        out_shape=out_shape,
        scratch_shapes=[pltpu.VMEM((seq, RET_WIDTH), BF16), pltpu.VMEM((seq, RET_WIDTH), BF16),
                        pltpu.VMEM((nc, RET_NPAIR, 2 * RET_PAIR, RET_PAIR), F32),
                        pltpu.VMEM((nc, RET_NPAIR, 2 * RET_PAIR, RET_PAIR), BF16)],
        compiler_params=_cparams(("arbitrary",)),
        name="retention_lat" if latent else "retention_ctx",
    )(*args)


def _route(logits):
    lane = lax.broadcasted_iota(jnp.int32, logits.shape, 1)
    lane_f = lane.astype(F32)
    big = float(ROUTE_COLS)
    neg = -jnp.inf
    is_grp = lane < N_GROUPS
    gl = jnp.where(is_grp, logits, neg)
    gmax = jnp.max(gl, axis=-1, keepdims=True)
    grp = jnp.min(jnp.where(gl == gmax, lane_f, big), axis=-1, keepdims=True)
    p_grp = 1.0 / jnp.sum(jnp.exp(gl - gmax), axis=-1, keepdims=True)
    e_f = lane_f - N_GROUPS
    lo = grp * EXPERTS_PER_GROUP
    in_grp = (e_f >= lo) & (e_f < lo + EXPERTS_PER_GROUP)
    el = jnp.where(in_grp, logits, neg)
    m1 = jnp.max(el, axis=-1, keepdims=True)
    i1 = jnp.min(jnp.where(el == m1, lane_f, big), axis=-1, keepdims=True)
    el2 = jnp.where(lane_f == i1, neg, el)
    m2 = jnp.max(el2, axis=-1, keepdims=True)
    i2 = jnp.min(jnp.where(el2 == m2, lane_f, big), axis=-1, keepdims=True)
    t = jnp.exp(m2 - m1)
    g1 = p_grp / (1.0 + t)
    g2 = p_grp * t / (1.0 + t)
    rows = logits.shape[0]
    oh1, oh2 = lane_f == i1, lane_f == i2
    oh = jnp.where(oh1 | oh2, 1.0, 0.0)
    tri = (lax.broadcasted_iota(jnp.int32, (rows, rows), 0)
           > lax.broadcasted_iota(jnp.int32, (rows, rows), 1))
    rank = jnp.dot(jnp.where(tri, 1.0, 0.0).astype(BF16), oh.astype(BF16), preferred_element_type=F32)
    tiles = jnp.floor((jnp.sum(oh, axis=0, keepdims=True) + (SUBLANES - 1)) * (1.0 / SUBLANES))
    upper = (lax.broadcasted_iota(jnp.int32, (ROUTE_COLS, ROUTE_COLS), 0)
             < lax.broadcasted_iota(jnp.int32, (ROUTE_COLS, ROUTE_COLS), 1))
    start = SUBLANES * jnp.dot(jnp.broadcast_to(tiles, (SUBLANES, ROUTE_COLS)).astype(BF16),
                               jnp.where(upper, 1.0, 0.0).astype(BF16),
                               preferred_element_type=F32)[0:1, :]
    pos = start + rank
    p1 = jnp.sum(jnp.where(oh1, pos, 0.0), axis=-1, keepdims=True)
    p2 = jnp.sum(jnp.where(oh2, pos, 0.0), axis=-1, keepdims=True)
    out = jnp.zeros(logits.shape, F32)
    for k, val in enumerate((i1 - N_GROUPS, i2 - N_GROUPS, g1, g2, p1, p2)):
        out = jnp.where(lane == k, val, out)
    return out, SUBLANES * tiles


def _outproj_kernel(ycc, ycl, ync, ynl, yrc, yrl, x_ref, mod_ref, g_ref, w_ref, wr_ref, br_ref,
                    xo_ref, xs_ref, r_ref, seg_ref):
    is_ctx = pl.program_id(0) < NB_CTX
    yc = jnp.where(is_ctx, ycc[...], ycl[...])
    yn = jnp.where(is_ctx, ync[...], ynl[...])
    yr = jnp.where(is_ctx, yrc[...], yrl[...])
    y = (jnp.dot(yc, w_ref[0:CONV_CH, :], preferred_element_type=F32)
         + jnp.dot(yn, w_ref[CONV_CH:CONV_CH + NA_WIDTH, :], preferred_element_type=F32)
         + jnp.dot(yr, w_ref[CONV_CH + NA_WIDTH:, :], preferred_element_type=F32))
    x = x_ref[...] + mod_ref[2:3, :] * y
    xo_ref[...] = x
    h = _norm_mod(x, g_ref[...], mod_ref[3:4, :], mod_ref[4:5, :])
    h_hi = h.astype(BF16)
    h_lo = (h - h_hi.astype(F32)).astype(BF16)
    hw = jnp.dot(h_hi, wr_ref[...], preferred_element_type=F32)
    logits = (hw[:, :ROUTE_COLS] + hw[:, ROUTE_COLS:]
              + jnp.dot(h_lo, wr_ref[:, :ROUTE_COLS], preferred_element_type=F32) + br_ref[...])
    route, seg = _route(logits)
    r_ref[...] = route
    seg_ref[...] = jnp.broadcast_to(seg, seg_ref.shape)
    sel = _slot_onehot(route, 0) | _slot_onehot(route, 1)
    xs_ref[...] = _pack_bf16_pairs(lax.dot_general(jnp.where(sel, 1.0, 0.0).astype(BF16), h_hi,
                                                   (((0,), (0,)), ((), ())), preferred_element_type=F32))


def _outproj(y_conv, y_na, y_ret, x, mod, g, w_bf16, w_route, b_route):
    return pl.pallas_call(
        _outproj_kernel,
        grid=(NB_ALL,),
        in_specs=(_ctx_lat_specs(CONV_CH) + _ctx_lat_specs(NA_WIDTH) + _ctx_lat_specs(RET_WIDTH)
                  + [_tok_spec(D_MODEL), _mod_spec(), _full_spec((1, D_MODEL)),
                     _full_spec((D_MODEL, D_MODEL)), _full_spec((D_MODEL, 2 * ROUTE_COLS)),
                     _full_spec((1, ROUTE_COLS))]),
        out_specs=[_tok_spec(D_MODEL), pl.BlockSpec((MOE_LC, D_HALF), lambda i: (i, 0)),
                   _tok_spec(ROUTE_COLS), pl.BlockSpec((None, SUBLANES, ROUTE_COLS), lambda i: (i, 0, 0))],
        out_shape=[jax.ShapeDtypeStruct((T_ALL, D_MODEL), F32),
                   jax.ShapeDtypeStruct((NB_ALL * MOE_LC, D_HALF), U32),
                   jax.ShapeDtypeStruct((T_ALL, ROUTE_COLS), F32),
                   jax.ShapeDtypeStruct((NB_ALL, SUBLANES, ROUTE_COLS), F32)],
        compiler_params=_cparams(("arbitrary",)),
        name="outproj_route",
    )(y_conv[0], y_conv[1], y_na[0], y_na[1], y_ret[0], y_ret[1], x, mod, g, w_bf16, w_route, b_route)


def _dispatch_tables(seg):
    seg_len = seg[:, 0, N_GROUPS:N_GROUPS + N_EXPERTS].astype(jnp.int32)
    experts = jnp.arange(N_EXPERTS, dtype=jnp.int32)
    in_chunk = jnp.cumsum(seg_len, axis=1) - seg_len
    seg_row = in_chunk + MOE_LC * jnp.arange(N_CHUNK, dtype=jnp.int32)[:, None]
    seg_off = jnp.cumsum(seg_len, axis=0) - seg_len
    rows_e = jnp.sum(seg_len, axis=0)
    chunk_rows = jnp.sum(seg_len, axis=1)
    nblk = (rows_e + MOE_BLK - 1) // MOE_BLK
    blk_end = jnp.cumsum(nblk)
    blk_start = blk_end - nblk
    blk = jnp.arange(MOE_NBLK, dtype=jnp.int32)
    n_active = blk_end[-1]
    blk_e = jnp.minimum(jnp.sum((blk_end[None, :] <= jnp.minimum(blk, n_active - 1)[:, None]).astype(jnp.int32),
                                axis=-1), N_EXPERTS - 1)
    mine = blk_e[:, None] == experts[None, :]
    blk_lo = (blk - jnp.sum(jnp.where(mine, blk_start[None, :], 0), axis=-1)) * MOE_BLK
    left = jnp.sum(jnp.where(mine, rows_e[None, :], 0), axis=-1) - blk_lo
    blk_nv = jnp.where(blk < n_active, jnp.clip(left, 0, MOE_BLK), 0).astype(jnp.int32)
    off_b = jnp.sum(jnp.where(mine[:, None, :], seg_off[None, :, :], 0), axis=-1)
    end_b = off_b + jnp.sum(jnp.where(mine[:, None, :], seg_len[None, :, :], 0), axis=-1)
    blk_c0 = jnp.sum((end_b <= blk_lo[:, None]).astype(jnp.int32), axis=-1)
    blk_c1 = jnp.sum((off_b < (blk_lo + blk_nv)[:, None]).astype(jnp.int32), axis=-1)
    after = jnp.sum(jnp.where(mine, blk_end[None, :], 0), axis=-1)
    blk_next_e = jnp.where(after < n_active, jnp.take(blk_e, jnp.minimum(after, MOE_NBLK - 1)), -1)
    row_b = jnp.sum(jnp.where(mine[:, None, :], seg_row[None, :, :], 0), axis=-1)
    first = jnp.maximum(off_b, blk_lo[:, None])
    piece_n = jnp.minimum(end_b, (blk_lo + blk_nv)[:, None]) - first
    piece_src = row_b + first - off_b
    piece_dst = first - blk_lo[:, None]
    return (blk_e, blk_next_e.astype(jnp.int32), blk_nv, blk_c0, blk_c1, piece_src.reshape(-1),
            piece_dst.reshape(-1), piece_n.reshape(-1), chunk_rows)


def _moe_kernel(layer, blk_e, blk_next_e, blk_nv, blk_c0, blk_c1, piece_src, piece_dst, piece_n,
                chunk_rows, xs_hbm, w1_hbm, w3_hbm, w2_hbm, ys_hbm, xbuf, obuf, zeros,
                w1f, w3f, w2f, w1b, w3b, w2b, gsem, ssem, zsem, wsem):
    i = pl.program_id(0)
    last = pl.num_programs(0) - 1
    slot = i % 2

    def tiles(v):
        return pl.multiple_of(v, SUBLANES)

    def for_segments(blk, fn):
        def body(c, carry):
            k = blk * N_CHUNK + c
            n = piece_n[k]

            @pl.when(n > 0)
            def _():
                fn(tiles(piece_src[k]), tiles(piece_dst[k]), tiles(n))

            return carry

        lax.fori_loop(blk_c0[blk], blk_c1[blk], body, 0)

    def weight_copies(e):
        return [pltpu.make_async_copy(src.at[layer, e], dst, wsem)
                for src, dst in ((w1_hbm, w1f), (w3_hbm, w3f), (w2_hbm, w2f))]

    def start_gathers(blk, s):
        for_segments(blk, lambda src, dst, n: pltpu.make_async_copy(
            xs_hbm.at[pl.ds(src, n)], xbuf.at[s, pl.ds(dst, n)], gsem.at[s]).start())

    def start_scatters(blk, s):
        for_segments(blk, lambda dst, src, n: pltpu.make_async_copy(
            obuf.at[s, pl.ds(src, n)], ys_hbm.at[pl.ds(dst, n)], ssem.at[s]).start())

    def wait_rows(blk, s, sem):
        n = tiles(blk_nv[blk])

        @pl.when(n > 0)
        def _():
            pltpu.make_async_copy(xs_hbm.at[pl.ds(0, n)], xbuf.at[s, pl.ds(0, n)], sem.at[s]).wait()

    @pl.when(i == 0)
    def _():
        xbuf[...] = jnp.zeros_like(xbuf)
        zeros[...] = jnp.zeros_like(zeros)

        def tail(c):
            n = tiles(MOE_LC - chunk_rows[c])
            return n, pltpu.make_async_copy(zeros.at[pl.ds(0, n)],
                                            ys_hbm.at[pl.ds(tiles(c * MOE_LC + chunk_rows[c]), n)], zsem)

        def fill(c, carry):
            n, copy = tail(c)
            pl.when(n > 0)(copy.start)
            return carry

        def drain(c, carry):
            n, copy = tail(c)
            pl.when(n > 0)(copy.wait)
            return carry

        lax.fori_loop(0, N_CHUNK, fill, 0)
        lax.fori_loop(0, N_CHUNK, drain, 0)
        start_gathers(0, 0)
        for copy in weight_copies(blk_e[0]):
            copy.start()

    @pl.when(i < last)
    def _():
        start_gathers(i + 1, 1 - slot)

    @pl.when(i >= 2)
    def _():
        wait_rows(i - 2, slot, ssem)

    @pl.when(blk_nv[i] > 0)
    def _():
        @pl.when((i == 0) | (blk_e[i] != blk_e[jnp.maximum(i - 1, 0)]))
        def _():
            for copy in weight_copies(blk_e[i]):
                copy.wait()
            w1b[...] = w1f[...].astype(BF16)
            w3b[...] = w3f[...].astype(BF16)
            w2b[...] = w2f[...].astype(BF16)

            @pl.when(blk_next_e[i] >= 0)
            def _():
                for copy in weight_copies(blk_next_e[i]):
                    copy.start()

        wait_rows(i, slot, gsem)

        def expert_mlp(rows):
            x_lo, x_hi = _unpack_bf16_pairs(xbuf[slot, 0:rows, :])
            n_hid = D_EXPERT // MXU_TILE

            def in_dot(w, t):
                cols = slice(t * MXU_TILE, (t + 1) * MXU_TILE)
                return (jnp.dot(x_lo, w[:D_HALF, cols], preferred_element_type=F32)
                        + jnp.dot(x_hi, w[D_HALF:, cols], preferred_element_type=F32))

            ab = [(in_dot(w1b, t), in_dot(w3b, t)) for t in range(n_hid)]
            mid = [(a * _sigmoid(a) * b).astype(BF16) for a, b in ab]

            def out_dot(t):
                cols = slice(t * MXU_TILE, (t + 1) * MXU_TILE)
                return sum(jnp.dot(mid[j], w2b[j * MXU_TILE:(j + 1) * MXU_TILE, cols],
                                   preferred_element_type=F32) for j in range(n_hid))

            n_word = D_HALF // MXU_TILE
            for t in range(n_word):
                obuf[slot, 0:rows, t * MXU_TILE:(t + 1) * MXU_TILE] = _pack_words(
                    out_dot(t), out_dot(t + n_word))

        for rows in range(MOE_ROW_STEP, MOE_BLK + 1, MOE_ROW_STEP):
            @pl.when((blk_nv[i] > rows - MOE_ROW_STEP) & (blk_nv[i] <= rows))
            def _(rows=rows):
                expert_mlp(rows)

        start_scatters(i, slot)

    @pl.when(i == last)
    def _():
        wait_rows(i - 1, 1 - slot, ssem)
        wait_rows(i, slot, ssem)


def _moe(xs, w1, w3, w2, layer, blk_e, blk_next_e, blk_nv, blk_c0, blk_c1, piece_src, piece_dst,
         piece_n, chunk_rows):
    any_spec = pl.BlockSpec(memory_space=pl.ANY)
    grid_spec = pltpu.PrefetchScalarGridSpec(
        num_scalar_prefetch=9,
        grid=(MOE_NBLK,),
        in_specs=[any_spec, any_spec, any_spec, any_spec],
        out_specs=any_spec,
        scratch_shapes=[pltpu.VMEM((2, MOE_BLK, D_HALF), U32), pltpu.VMEM((2, MOE_BLK, D_HALF), U32),
                        pltpu.VMEM((MOE_LC - 2 * TM, D_HALF), U32),
                        pltpu.VMEM((D_MODEL, D_EXPERT), F32), pltpu.VMEM((D_MODEL, D_EXPERT), F32),
                        pltpu.VMEM((D_EXPERT, D_MODEL), F32),
                        pltpu.VMEM((D_MODEL, D_EXPERT), BF16), pltpu.VMEM((D_MODEL, D_EXPERT), BF16),
                        pltpu.VMEM((D_EXPERT, D_MODEL), BF16),
                        pltpu.SemaphoreType.DMA((2,)), pltpu.SemaphoreType.DMA((2,)),
                        pltpu.SemaphoreType.DMA, pltpu.SemaphoreType.DMA])
    return pl.pallas_call(
        functools.partial(_moe_kernel, layer),
        grid_spec=grid_spec,
        out_shape=jax.ShapeDtypeStruct((NB_ALL * MOE_LC, D_HALF), U32),
        compiler_params=_cparams(("arbitrary",)),
        name="moe_experts",
    )(blk_e, blk_next_e, blk_nv, blk_c0, blk_c1, piece_src, piece_dst, piece_n, chunk_rows,
      xs, w1, w3, w2)


def _final_kernel(x_ref, ys_ref, r_ref, mod_ref, g_ref, o_ref):
    x = _moe_residual(x_ref, ys_ref, r_ref, mod_ref)
    ms = jnp.mean(x * x, axis=-1, keepdims=True)
    o_ref[...] = x * lax.rsqrt(ms + EPS) * g_ref[...]


def _final(x, ys, route, mod, g, block0, nblocks):
    return pl.pallas_call(
        _final_kernel,
        grid=(nblocks,),
        in_specs=[pl.BlockSpec((TM, D_MODEL), lambda i: (block0 + i, 0)),
                  pl.BlockSpec((MOE_LC, D_HALF), lambda i: (block0 + i, 0)),
                  pl.BlockSpec((TM, ROUTE_COLS), lambda i: (block0 + i, 0)),
                  pl.BlockSpec((None, 6, D_MODEL), lambda i: (_cond_row(block0 + i), 0, 0)),
                  _full_spec((1, D_MODEL))],
        out_specs=_tok_spec(D_MODEL),
        out_shape=jax.ShapeDtypeStruct((nblocks * TM, D_MODEL), F32),
        compiler_params=_cparams(("arbitrary",)),
        name="final_norm",
    )(x, ys, route, mod, g)


def kernel(x_prompt, x_sample, c, cache_k, cache_v, state_ret_f, state_ret_b, c_ctx, w_ada, b_ada, norm1_g, norm2_g, w_in, w_out, conv_w, conv_b, conv_ln_g, conv_ln_b, na_rpb, ret_lg_f, ret_lg_b, ret_gn_g, w_route_g, b_route_g, w_route_e, b_route_e, w1, w3, w2, final_g):
    cv = jnp.zeros((COND_ROWS, D_MODEL), F32).at[0].set(c_ctx).at[1:N_COND].set(c)
    mods = _ada(cv, w_ada, b_ada).reshape(DEPTH, COND_ROWS, 6, D_MODEL)
    w_in_b = w_in.astype(BF16)
    w_out_b = w_out.astype(BF16)
    pad = ROUTE_COLS - N_GROUPS - N_EXPERTS
    w_route = jnp.pad(jnp.concatenate([w_route_g, w_route_e], axis=-1), ((0, 0), (0, 0), (0, pad)))
    b_route = jnp.pad(jnp.concatenate([b_route_g, b_route_e], axis=-1), ((0, 0), (0, pad)))
    w_route_hi = w_route.astype(BF16)
    w_route_lo = (w_route - w_route_hi.astype(F32)).astype(BF16)
    w_route = jnp.concatenate([w_route_hi, w_route_lo], axis=-1)
    na_bias = _na_bias_tables(na_rpb)
    rope = _rope_tables()
    lg = jnp.stack([ret_lg_f, ret_lg_b], axis=1)

    x_ctx = x_prompt.reshape(T_CTX, D_MODEL)
    x_lat = x_sample.reshape(T_LAT, D_MODEL)
    x = y = route = new_k = new_v = None
    sf_list, sb_list = [], []
    for l in range(DEPTH):
        g1 = norm1_g[l].reshape(1, D_MODEL)
        if l == 0:
            z, x = _inproj_first(x_ctx, x_lat, mods[l], g1, w_in_b[l])
        else:
            z, x = _inproj_next(x, y, route, mods[l - 1], mods[l], g1, w_in_b[l])
        conv_args = (conv_w[l], conv_b[l].reshape(1, -1), conv_ln_g[l].reshape(1, -1),
                     conv_ln_b[l].reshape(1, -1))
        yc_c = _conv(z, 0, BATCH, SEQ, *conv_args)
        yc_l = _conv(z, T_CTX // DEC_SEQ, DEC_BATCH, DEC_SEQ, *conv_args)
        yn_c, new_k, new_v = _ctx_attn(z, l, new_k, new_v)
        yn_l = _na_attn(z, cache_k, cache_v, na_bias, l)
        gn = ret_gn_g[l].reshape(1, RET_WIDTH)
        yr_c, sf_l, sb_l = _retention(z, lg[l], gn, latent=False)
        yr_l = _retention(z, lg[l], gn, latent=True, layer=l, rope=rope,
                          s0_f=state_ret_f, s0_b=state_ret_b)
        x, xs, route, seg = _outproj((yc_c, yc_l), (yn_c, yn_l), (yr_c, yr_l), x, mods[l],
                                     norm2_g[l].reshape(1, D_MODEL), w_out_b[l], w_route[l],
                                     b_route[l].reshape(1, ROUTE_COLS))
        y = _moe(xs, w1, w3, w2, l, *_dispatch_tables(seg))
        sf_list.append(sf_l)
        sb_list.append(sb_l)
    fg = final_g.reshape(1, D_MODEL)
    y_prompt = _final(x, y, route, mods[DEPTH - 1], fg, 0, NB_CTX).reshape(BATCH, SEQ, D_MODEL)
    y_sample = _final(x, y, route, mods[DEPTH - 1], fg, NB_CTX, NB_LAT).reshape(DEC_BATCH, DEC_SEQ, D_MODEL)
    return (y_prompt, y_sample, new_k, new_v, jnp.stack(sf_list, axis=1), jnp.stack(sb_list, axis=1))
```

```python
import functools

import numpy as np
import jax
import jax.numpy as jnp
from jax import lax
from jax.experimental import pallas as pl
from jax.experimental.pallas import tpu as pltpu

D_MODEL = 1024
BATCH = 32
SEQ = 256
DEPTH = 2
DEC_BATCH = 4
DEC_SEQ = 4096
PAST_LEN = 512
GRID_W = 64
GRID_H = DEC_SEQ // GRID_W
CONV_CH = 256
CONV_K = 31
NA_HEADS = 8
NA_DIM = 64
NA_WIDTH = NA_HEADS * NA_DIM
NA_KH = 8
NA_KW = 16
RET_HEADS = 4
RET_DIM = 64
RET_WIDTH = RET_HEADS * RET_DIM
RET_CHUNK = 128
ROPE_BASE = 10000.0
N_GROUPS = 4
EXPERTS_PER_GROUP = 8
N_EXPERTS = N_GROUPS * EXPERTS_PER_GROUP
D_EXPERT = 512
IN_COLS = 2 * CONV_CH + 3 * NA_WIDTH + 4 * RET_WIDTH
EPS = 1e-6
NEG_INF = -1e30

F32 = jnp.float32
BF16 = jnp.bfloat16
HIGHEST = lax.Precision.HIGHEST

T_CTX = BATCH * SEQ
T_LAT = DEC_BATCH * DEC_SEQ
T_ALL = T_CTX + T_LAT
N_COND = 1 + DEC_BATCH
COND_ROWS = 8

TM = 512
NB_CTX = T_CTX // TM
NB_LAT = T_LAT // TM
NB_ALL = NB_CTX + NB_LAT
LAT_BLOCKS_PER_REQ = DEC_SEQ // TM

LANES = 128
SUBLANES = 8
MXU_TILE = 256
ROUTE_COLS = LANES

COL_CONV = 0
COL_NA_Q = 2 * CONV_CH
COL_NA_K = COL_NA_Q + NA_WIDTH
COL_NA_V = COL_NA_K + NA_WIDTH
COL_RET = COL_NA_V + NA_WIDTH

NA_ROWS = 8
NA_Q = NA_ROWS * GRID_W
NA_KROWS = NA_ROWS + NA_KH
NA_KEYS = NA_KROWS * GRID_W
NA_RB = GRID_H // NA_ROWS

MOE_BLK = 512
MOE_ROW_STEP = 128
MOE_LC = -(-(2 * TM + N_EXPERTS * (SUBLANES - 1)) // LANES) * LANES
N_CHUNK = NB_ALL
MOE_NBLK = -(-(N_CHUNK * MOE_LC) // MOE_BLK) + N_EXPERTS

VMEM_LIMIT = 56 * 1024 * 1024


def _cparams(sem):
    return pltpu.CompilerParams(dimension_semantics=sem, vmem_limit_bytes=VMEM_LIMIT)


def _sigmoid(x):
    return 1.0 / (1.0 + jnp.exp(-x))


def _cond_row(i):
    return jnp.where(i < NB_CTX, 0, 1 + (i - NB_CTX) // LAT_BLOCKS_PER_REQ)


ADA_TN = 1536


def _ada_kernel(cv_ref, w_ref, b_ref, o_ref):
    cv = cv_ref[...]
    s = cv * _sigmoid(cv)
    o_ref[...] = jnp.dot(s, w_ref[...], precision=HIGHEST, preferred_element_type=F32) + b_ref[...]


def _ada(cv, w_ada, b_ada):
    n = 6 * D_MODEL
    return pl.pallas_call(
        _ada_kernel,
        grid=(DEPTH, n // ADA_TN),
        in_specs=[
            pl.BlockSpec((COND_ROWS, D_MODEL), lambda l, j: (0, 0)),
            pl.BlockSpec((None, D_MODEL, ADA_TN), lambda l, j: (l, 0, j)),
            pl.BlockSpec((None, 1, ADA_TN), lambda l, j: (l, 0, j)),
        ],
        out_specs=pl.BlockSpec((None, COND_ROWS, ADA_TN), lambda l, j: (l, 0, j)),
        out_shape=jax.ShapeDtypeStruct((DEPTH, COND_ROWS, n), F32),
        compiler_params=_cparams(("arbitrary", "arbitrary")),
        name="ada_mod",
    )(cv, w_ada, b_ada.reshape(DEPTH, 1, n))


IN_TN = 768


def _norm_mod(x, g, shift, scale):
    ms = jnp.mean(x * x, axis=-1, keepdims=True)
    return (x * lax.rsqrt(ms + EPS) * g) * (1.0 + scale) + shift


def _inproj_body(x, mod_ref, g_ref, w_ref, z_ref):
    h = _norm_mod(x, g_ref[...], mod_ref[0:1, :], mod_ref[1:2, :]).astype(BF16)
    for c in range(IN_COLS // IN_TN):
        cols = slice(c * IN_TN, (c + 1) * IN_TN)
        z_ref[:, cols] = jnp.dot(h, w_ref[:, cols], preferred_element_type=F32).astype(BF16)


def _inproj_first_kernel(xc_ref, xl_ref, mod_ref, g_ref, w_ref, z_ref, xo_ref):
    i = pl.program_id(0)
    x = jnp.where(i < NB_CTX, xc_ref[...], xl_ref[...])
    xo_ref[...] = x
    _inproj_body(x, mod_ref, g_ref, w_ref, z_ref)


U32 = jnp.uint32
D_HALF = D_MODEL // 2
_HI_MASK = np.uint32(0xFFFF0000)


def _pack_words(lo, hi):
    lo = lax.bitcast_convert_type(lo.astype(BF16).astype(F32), U32) >> 16
    hi = lax.bitcast_convert_type(hi.astype(BF16).astype(F32), U32) & _HI_MASK
    return lo | hi


def _pack_bf16_pairs(x):
    return _pack_words(x[:, :D_HALF], x[:, D_HALF:])


def _unpack_bf16_pairs(w):
    lo = lax.bitcast_convert_type(w << 16, F32).astype(BF16)
    hi = lax.bitcast_convert_type(w & _HI_MASK, F32).astype(BF16)
    return lo, hi


def _slot_onehot(route, slot):
    pos = route[:, 4 + slot:5 + slot].astype(jnp.int32)
    return lax.broadcasted_iota(jnp.int32, (route.shape[0], MOE_LC), 1) == pos


def _moe_residual(x_ref, ys_ref, r_ref, mod_ref):
    r = r_ref[...]
    sel = jnp.where(_slot_onehot(r, 0), r[:, 2:3], jnp.where(_slot_onehot(r, 1), r[:, 3:4], 0.0))
    sel = sel.astype(BF16)
    y = jnp.concatenate([jnp.dot(sel, half, preferred_element_type=F32)
                         for half in _unpack_bf16_pairs(ys_ref[...])], axis=-1)
    return x_ref[...] + mod_ref[5:6, :] * y


def _inproj_next_kernel(x_ref, ys_ref, r_ref, modp_ref, mod_ref, g_ref, w_ref, z_ref, xo_ref):
    x = _moe_residual(x_ref, ys_ref, r_ref, modp_ref)
    xo_ref[...] = x
    _inproj_body(x, mod_ref, g_ref, w_ref, z_ref)


def _tok_spec(cols):
    return pl.BlockSpec((TM, cols), lambda i: (i, 0))


def _mod_spec():
    return pl.BlockSpec((None, 6, D_MODEL), lambda i: (_cond_row(i), 0, 0))


def _full_spec(shape):
    return pl.BlockSpec(shape, lambda i: (0,) * len(shape))


def _ctx_lat_specs(cols):
    return [pl.BlockSpec((TM, cols), lambda i: (jnp.minimum(i, NB_CTX - 1), 0)),
            pl.BlockSpec((TM, cols), lambda i: (jnp.maximum(i - NB_CTX, 0), 0))]


def _inproj_first(x_ctx, x_lat, mod, g, w_bf16):
    return pl.pallas_call(
        _inproj_first_kernel,
        grid=(NB_ALL,),
        in_specs=_ctx_lat_specs(D_MODEL) + [_mod_spec(), _full_spec((1, D_MODEL)),
                                            _full_spec((D_MODEL, IN_COLS))],
        out_specs=[_tok_spec(IN_COLS), _tok_spec(D_MODEL)],
        out_shape=[jax.ShapeDtypeStruct((T_ALL, IN_COLS), BF16),
                   jax.ShapeDtypeStruct((T_ALL, D_MODEL), F32)],
        compiler_params=_cparams(("arbitrary",)),
        name="inproj_first",
    )(x_ctx, x_lat, mod, g, w_bf16)


def _inproj_next(x, ys, route, mod_prev, mod, g, w_bf16):
    return pl.pallas_call(
        _inproj_next_kernel,
        grid=(NB_ALL,),
        in_specs=[_tok_spec(D_MODEL),
                  pl.BlockSpec((MOE_LC, D_HALF), lambda i: (i, 0)),
                  _tok_spec(ROUTE_COLS),
                  _mod_spec(), _mod_spec(), _full_spec((1, D_MODEL)),
                  _full_spec((D_MODEL, IN_COLS))],
        out_specs=[_tok_spec(IN_COLS), _tok_spec(D_MODEL)],
        out_shape=[jax.ShapeDtypeStruct((T_ALL, IN_COLS), BF16),
                   jax.ShapeDtypeStruct((T_ALL, D_MODEL), F32)],
        compiler_params=_cparams(("arbitrary",)),
        name="inproj_next",
    )(x, ys, route, mod_prev, mod, g, w_bf16)


CONV_PAD = 16
CONV_CHUNK = 64


CONV_SPAN = CONV_CHUNK + 2 * CONV_PAD - SUBLANES


CONV_UNROLL = 4


def _conv_kernel(seq, z_ref, w_ref, b_ref, g_ref, be_ref, o_ref, upad_ref, shift_refs):
    zeros = jnp.zeros((CONV_PAD, CONV_CH), F32)
    upad_ref[0:CONV_PAD, :] = zeros
    upad_ref[seq + CONV_PAD:seq + 2 * CONV_PAD, :] = zeros

    def glu(ci, carry):
        base = pl.multiple_of(ci * 256, 256)
        zc = z_ref[pl.ds(base, 256), :].astype(F32)
        upad_ref[pl.ds(base + CONV_PAD, 256), :] = zc[:, :CONV_CH] * _sigmoid(zc[:, CONV_CH:])
        return carry

    lax.fori_loop(0, seq // 256, glu, 0)

    shift = CONV_PAD - CONV_K // 2

    def chunk(ci, shift_ref):
        base = pl.multiple_of(ci * CONV_CHUNK, CONV_CHUNK)
        win = upad_ref[pl.ds(base, CONV_CHUNK + 2 * CONV_PAD), :]
        acc = jnp.zeros((CONV_CHUNK, CONV_CH), F32)
        for sub in range(SUBLANES):
            shift_ref[sub] = win[sub:sub + CONV_SPAN, :]
            for k in range(CONV_K):
                if (k + shift) % SUBLANES == sub:
                    lo = k + shift - sub
                    acc = acc + w_ref[k:k + 1, :] * shift_ref[sub, lo:lo + CONV_CHUNK, :]
        acc = acc + b_ref[...]
        mu = jnp.mean(acc, axis=-1, keepdims=True)
        d = acc - mu
        var = jnp.mean(d * d, axis=-1, keepdims=True)
        n = d * lax.rsqrt(var + EPS) * g_ref[...] + be_ref[...]
        o_ref[pl.ds(base, CONV_CHUNK), :] = (n * _sigmoid(n)).astype(BF16)

    def chunks(cj, carry):
        for u in range(CONV_UNROLL):
            chunk(cj * CONV_UNROLL + u, shift_refs.at[u])
        return carry

    lax.fori_loop(0, seq // (CONV_CHUNK * CONV_UNROLL), chunks, 0)


def _conv(z, row_block0, nseq, seq, w, b, g, be):
    return pl.pallas_call(
        functools.partial(_conv_kernel, seq),
        grid=(nseq,),
        in_specs=[pl.BlockSpec((seq, 2 * CONV_CH), lambda s: (row_block0 + s, 0)),
                  _full_spec((CONV_K, CONV_CH)), _full_spec((1, CONV_CH)),
                  _full_spec((1, CONV_CH)), _full_spec((1, CONV_CH))],
        out_specs=pl.BlockSpec((seq, CONV_CH), lambda s: (s, 0)),
        out_shape=jax.ShapeDtypeStruct((nseq * seq, CONV_CH), BF16),
        scratch_shapes=[pltpu.VMEM((seq + 2 * CONV_PAD, CONV_CH), F32),
                        pltpu.VMEM((CONV_UNROLL, SUBLANES, CONV_SPAN, CONV_CH), F32)],
        compiler_params=_cparams(("arbitrary",)),
        name="conv_seq%d" % seq,
    )(z, w, b, g, be)


def _dot_nt(a, b):
    return lax.dot_general(a, b, (((1,), (1,)), ((), ())), preferred_element_type=F32)


NA_SCALE = NA_DIM ** -0.5
assert NA_SCALE == 2.0 ** round(np.log2(NA_SCALE)), "query pre-scaling assumes a power-of-two scale"


def _ctx_attn_kernel(layer, q_ref, k_ref, v_ref, *refs):
    if layer:
        kprev_ref, vprev_ref, o_ref, ko_ref, vo_ref = refs
    else:
        o_ref, ko_ref, vo_ref = refs
    for j in range(DEPTH):
        if j < layer:
            ko_ref[j] = kprev_ref[j]
            vo_ref[j] = vprev_ref[j]
        elif j > layer:
            ko_ref[j] = jnp.zeros(ko_ref.shape[1:], F32)
            vo_ref[j] = jnp.zeros(vo_ref.shape[1:], F32)
    pair = 2 * NA_DIM
    left = lax.broadcasted_iota(jnp.int32, (SEQ, pair), 1) < NA_DIM
    outs = []
    for p in range(NA_HEADS // 2):
        lanes = slice(p * pair, (p + 1) * pair)
        qp, kp, vp = q_ref[:, lanes], k_ref[:, lanes], v_ref[:, lanes]
        kf, vf = kp.astype(F32), vp.astype(F32)
        o_h = []
        for hh in range(2):
            cols = slice(hh * NA_DIM, (hh + 1) * NA_DIM)
            ko_ref[layer, 2 * p + hh] = kf[:, cols]
            vo_ref[layer, 2 * p + hh] = vf[:, cols]
            qm = jnp.where(left == (hh == 0), qp, jnp.zeros_like(qp))
            s = _dot_nt(qm, kp) * NA_SCALE
            m = jnp.max(s, axis=-1, keepdims=True)
            e = jnp.exp(s - m)
            den = jnp.sum(e, axis=-1, keepdims=True)
            o_h.append(jnp.dot(e.astype(BF16), vp, preferred_element_type=F32) / den)
        outs.append(jnp.where(left, o_h[0], o_h[1]))
    o_ref[...] = jnp.concatenate(outs, axis=-1).astype(BF16)


def _ctx_attn(z, layer, k_prev=None, v_prev=None):
    qb, kb, vb = COL_NA_Q // NA_WIDTH, COL_NA_K // NA_WIDTH, COL_NA_V // NA_WIDTH
    head_shape = jax.ShapeDtypeStruct((BATCH, DEPTH, NA_HEADS, SEQ, NA_DIM), F32)
    head_spec = pl.BlockSpec((None, DEPTH, NA_HEADS, SEQ, NA_DIM), lambda b: (b, 0, 0, 0, 0))
    in_specs = [pl.BlockSpec((SEQ, NA_WIDTH), lambda b: (b, qb)),
                pl.BlockSpec((SEQ, NA_WIDTH), lambda b: (b, kb)),
                pl.BlockSpec((SEQ, NA_WIDTH), lambda b: (b, vb))]
    args = [z, z, z]
    aliases = {}
    if layer:
        in_specs += [head_spec, head_spec]
        args += [k_prev, v_prev]
        aliases = {3: 1, 4: 2}
    return pl.pallas_call(
        functools.partial(_ctx_attn_kernel, layer),
        grid=(BATCH,),
        in_specs=in_specs,
        out_specs=[pl.BlockSpec((SEQ, NA_WIDTH), lambda b: (b, 0)), head_spec, head_spec],
        out_shape=[jax.ShapeDtypeStruct((T_CTX, NA_WIDTH), BF16), head_shape, head_shape],
        input_output_aliases=aliases,
        compiler_params=_cparams(("arbitrary",)),
        name="ctx_attn",
    )(*args)


NA_KINDS = (0, NA_ROWS, GRID_H - NA_ROWS)
N_DR = 2 * NA_KH - 1
N_DC = 2 * NA_KW - 1


def _na_row_offset(r0, i, j):
    ks = min(max(r0 - NA_KH // 2, 0), GRID_H - NA_KROWS)
    r, kr = r0 + i, ks + j
    rs = min(max(r - NA_KH // 2, 0), GRID_H - NA_KH)
    return kr - r + NA_KH - 1 if rs <= kr < rs + NA_KH else None


def _na_bias_kernel(rpb_ref, o_ref):
    lh = pl.program_id(0)
    shape = (GRID_W, 2 * GRID_W)
    qc = lax.broadcasted_iota(jnp.int32, shape, 0)
    lane = lax.broadcasted_iota(jnp.int32, shape, 1)
    kc = lane % GRID_W
    dc = jnp.clip(kc - qc, -(NA_KW - 1), NA_KW - 1) + NA_KW - 1
    cs = jnp.clip(qc - NA_KW // 2, 0, GRID_W - NA_KW)
    col_ok = (kc >= cs) & (kc < cs + NA_KW)
    neg = jnp.full(shape, NEG_INF, F32)
    tiles = []
    for dr in range(N_DR):
        base = (lh * N_DR + dr) * N_DC
        val = jnp.zeros(shape, F32)
        for d in range(N_DC):
            val = jnp.where(dc == d, rpb_ref[base + d], val)
        tiles.append(jnp.where(col_ok, val, neg))
    left = lane < GRID_W
    for kind, r0 in enumerate(NA_KINDS):
        for i in range(NA_ROWS):
            for jp in range(NA_KROWS // 2):
                dl, dr_ = _na_row_offset(r0, i, 2 * jp), _na_row_offset(r0, i, 2 * jp + 1)
                tl = neg if dl is None else tiles[dl]
                tr = neg if dr_ is None else tiles[dr_]
                o_ref[kind, i * GRID_W:(i + 1) * GRID_W, jp * 2 * GRID_W:(jp + 1) * 2 * GRID_W] = (
                    jnp.where(left, tl, tr))


def _na_bias_tables(rpb):
    return pl.pallas_call(
        _na_bias_kernel,
        grid=(DEPTH * NA_HEADS,),
        in_specs=[pl.BlockSpec(memory_space=pltpu.SMEM)],
        out_specs=pl.BlockSpec((None, len(NA_KINDS), NA_Q, NA_KEYS), lambda i: (i, 0, 0, 0)),
        out_shape=jax.ShapeDtypeStruct((DEPTH * NA_HEADS, len(NA_KINDS), NA_Q, NA_KEYS), F32),
        compiler_params=_cparams(("arbitrary",)),
        name="nbr_bias",
    )(rpb.reshape(-1))


NA_G = 4


def _na_kernel(q_ref, k_ref, v_ref, kc_ref, vc_ref, bias_ref, o_ref):
    rb = pl.program_id(2)
    ks = jnp.clip(rb * NA_ROWS - NA_KH // 2, 0, GRID_H - NA_KROWS)
    start = pl.multiple_of(ks * GRID_W, GRID_W)
    q = q_ref[...] * NA_SCALE
    kl = k_ref[pl.ds(start, NA_KEYS), :]
    vl = v_ref[pl.ds(start, NA_KEYS), :]
    pair = 2 * NA_DIM
    left = lax.broadcasted_iota(jnp.int32, (NA_Q, pair), 1) < NA_DIM
    ones_loc = jnp.ones((NA_KEYS, pair), BF16)
    ones_ctx = jnp.ones((PAST_LEN, pair), BF16)
    kc, v_ext, vc_ext = [], [], []
    for p in range(NA_G // 2):
        lanes = slice(p * pair, (p + 1) * pair)
        kc.append(jnp.concatenate([kc_ref[2 * p].astype(BF16), kc_ref[2 * p + 1].astype(BF16)], axis=-1))
        vc = jnp.concatenate([vc_ref[2 * p].astype(BF16), vc_ref[2 * p + 1].astype(BF16)], axis=-1)
        v_ext.append(jnp.concatenate([vl[:, lanes], ones_loc], axis=-1))
        vc_ext.append(jnp.concatenate([vc, ones_ctx], axis=-1))

    def scores(hh):
        p = hh // 2
        lanes = slice(p * pair, (p + 1) * pair)
        qm = jnp.where(left == (hh % 2 == 0), q[:, lanes], jnp.zeros((NA_Q, pair), BF16))
        return _dot_nt(qm, kl[:, lanes]) + bias_ref[hh], _dot_nt(qm, kc[p])

    outs = []
    nxt = scores(0)
    for hh in range(NA_G):
        s_loc, s_ctx = nxt
        if hh + 1 < NA_G:
            nxt = scores(hh + 1)
        m = jnp.maximum(jnp.max(s_loc, axis=-1, keepdims=True), jnp.max(s_ctx, axis=-1, keepdims=True))
        p_loc = jnp.exp(s_loc - m).astype(BF16)
        p_ctx = jnp.exp(s_ctx - m).astype(BF16)
        o = (jnp.dot(p_loc, v_ext[hh // 2], preferred_element_type=F32)
             + jnp.dot(p_ctx, vc_ext[hh // 2], preferred_element_type=F32))
        outs.append(o[:, :pair] / o[:, pair:])
    o_ref[...] = jnp.concatenate([jnp.where(left, outs[2 * p], outs[2 * p + 1]) for p in range(NA_G // 2)],
                                 axis=-1).astype(BF16)


def _na_attn(z, cache_k, cache_v, bias, layer):
    lat_q0 = T_CTX // NA_Q
    lat_s0 = T_CTX // DEC_SEQ
    width = NA_G * NA_DIM
    qc, kc, vc = COL_NA_Q // width, COL_NA_K // width, COL_NA_V // width
    groups = NA_HEADS // NA_G

    def kind(rb):
        return jnp.where(rb == 0, 0, jnp.where(rb == NA_RB - 1, 2, 1))

    ctx_spec = pl.BlockSpec((None, None, NA_G, PAST_LEN, NA_DIM), lambda b, hg, rb: (b, layer, hg, 0, 0))
    return pl.pallas_call(
        _na_kernel,
        grid=(DEC_BATCH, groups, NA_RB),
        in_specs=[pl.BlockSpec((NA_Q, width), lambda b, hg, rb: (lat_q0 + b * NA_RB + rb, qc + hg)),
                  pl.BlockSpec((DEC_SEQ, width), lambda b, hg, rb: (lat_s0 + b, kc + hg)),
                  pl.BlockSpec((DEC_SEQ, width), lambda b, hg, rb: (lat_s0 + b, vc + hg)),
                  ctx_spec, ctx_spec,
                  pl.BlockSpec((NA_G, None, NA_Q, NA_KEYS),
                               lambda b, hg, rb: (layer * groups + hg, kind(rb), 0, 0))],
        out_specs=pl.BlockSpec((NA_Q, width), lambda b, hg, rb: (b * NA_RB + rb, hg)),
        out_shape=jax.ShapeDtypeStruct((T_LAT, NA_WIDTH), BF16),
        compiler_params=_cparams(("arbitrary", "arbitrary", "arbitrary")),
        name="nbr_attn",
    )(z, z, z, cache_k, cache_v, bias)


RET_PAIR = 2 * RET_DIM
RET_NPAIR = RET_HEADS // 2
assert RET_PAIR == LANES and RET_CHUNK == LANES
RET_UNROLL = 8


def _rope_tables():
    n_freq = RET_DIM // 4
    t = np.arange(DEC_SEQ)
    inv = jnp.asarray(ROPE_BASE, F32) ** (-jnp.arange(n_freq, dtype=F32) / n_freq)
    ang_r = jnp.asarray(t // GRID_W, F32)[:, None] * inv[None, :]
    ang_c = jnp.asarray(t % GRID_W, F32)[:, None] * inv[None, :]
    cos = jnp.concatenate([jnp.cos(ang_r)] * 2 + [jnp.cos(ang_c)] * 2, axis=-1)
    sin = jnp.concatenate([-jnp.sin(ang_r), jnp.sin(ang_r), -jnp.sin(ang_c), jnp.sin(ang_c)], axis=-1)
    lane = np.arange(RET_WIDTH)
    src = np.where(lane % (2 * n_freq) < n_freq, lane + n_freq, lane - n_freq)
    swap = np.zeros((RET_WIDTH, RET_WIDTH), np.float32)
    swap[src, lane] = 1.0
    return jnp.tile(cos, (1, RET_HEADS)), jnp.tile(sin, (1, RET_HEADS)), jnp.asarray(swap, BF16)


def _ret_kernel(seq, latent, *refs):
    if latent:
        (lg_ref, z_ref, gn_ref, cos_ref, sin_ref, swap_ref, s0f_ref, s0b_ref, y_ref,
         q_s, k_s, kv_s, st_s) = refs
    else:
        lg_ref, z_ref, gn_ref, y_ref, sf_ref, sb_ref, q_s, k_s, kv_s, st_s = refs
    nc = seq // RET_CHUNK
    ch, hd, pw = RET_CHUNK, RET_DIM, RET_PAIR

    row = lax.broadcasted_iota(jnp.int32, (ch, ch), 0).astype(F32)
    col = lax.broadcasted_iota(jnp.int32, (ch, ch), 1).astype(F32)
    pos = lax.broadcasted_iota(jnp.int32, (ch, pw), 0).astype(F32)
    left = lax.broadcasted_iota(jnp.int32, (ch, pw), 1) < hd
    top = lax.broadcasted_iota(jnp.int32, (pw, pw), 0) < hd
    same_head = top == (lax.broadcasted_iota(jnp.int32, (pw, pw), 1) < hd)
    same_head2 = jnp.concatenate([same_head, same_head], axis=0)

    def per_head(mask, fn, p):
        return jnp.where(mask, fn(2 * p), fn(2 * p + 1))

    decay = []
    for h in range(RET_HEADS):
        lf, lb = lg_ref[0, h], lg_ref[1, h]
        d_f = jnp.where(row >= col, jnp.exp(jnp.maximum(row - col, 0.0) * lf), 0.0)
        d_b = jnp.where(col >= row, jnp.exp(jnp.maximum(col - row, 0.0) * lb), 0.0)
        decay.append(d_f + d_b)
    q_dec, k_dec, c_dec_f, c_dec_b = [], [], [], []
    for p in range(RET_NPAIR):
        q_dec.append(jnp.concatenate(
            [per_head(left, lambda h: jnp.exp((pos + 1.0) * lg_ref[0, h]), p),
             per_head(left, lambda h: jnp.exp((ch - pos) * lg_ref[1, h]), p)], axis=-1))
        k_dec.append(jnp.concatenate(
            [per_head(left, lambda h: jnp.exp((ch - 1.0 - pos) * lg_ref[0, h]), p),
             per_head(left, lambda h: jnp.exp(pos * lg_ref[1, h]), p)], axis=-1))
        zero = jnp.zeros((pw, pw), F32)
        c_dec_f.append(per_head(top, lambda h: jnp.exp(zero + ch * lg_ref[0, h]), p))
        c_dec_b.append(per_head(top, lambda h: jnp.exp(zero + ch * lg_ref[1, h]), p))

    def rope(x, base):
        xf = x.astype(F32)
        if not latent:
            return xf
        swapped = jnp.dot(x, swap_ref[...], preferred_element_type=F32)
        return xf * cos_ref[pl.ds(base, ch), :] + swapped * sin_ref[pl.ds(base, ch), :]

    def pass1(n, carry):
        base = pl.multiple_of(n * ch, ch)
        zc = z_ref[pl.ds(base, ch), :]
        q = rope(zc[:, 0:RET_WIDTH], base)
        k = rope(zc[:, RET_WIDTH:2 * RET_WIDTH], base) * (RET_DIM ** -0.5)
        q_s[pl.ds(base, ch), :] = q.astype(BF16)
        k_s[pl.ds(base, ch), :] = k.astype(BF16)
        v = zc[:, 2 * RET_WIDTH:3 * RET_WIDTH]
        for p in range(RET_NPAIR):
            lanes = slice(p * pw, (p + 1) * pw)
            kp = k[:, lanes]
            k2 = (jnp.concatenate([kp, kp], axis=-1) * k_dec[p]).astype(BF16)
            kv = lax.dot_general(k2, v[:, lanes], (((0,), (0,)), ((), ())), preferred_element_type=F32)
            kv_s[n, p] = jnp.where(same_head2, kv, 0.0)
        return carry

    lax.fori_loop(0, nc, pass1, 0, unroll=min(RET_UNROLL, nc))

    def block_diag(a, b):
        z = jnp.zeros((hd, hd), F32)
        return jnp.concatenate([jnp.concatenate([a, z], axis=1), jnp.concatenate([z, b], axis=1)], axis=0)

    for p in range(RET_NPAIR):
        if latent:
            s_f = block_diag(s0f_ref[2 * p], s0f_ref[2 * p + 1])
            s_b = block_diag(s0b_ref[2 * p], s0b_ref[2 * p + 1])
        else:
            s_f = s_b = jnp.zeros((pw, pw), F32)

        def fwd(n, s, p=p):
            st_s[n, p, 0:pw, :] = s.astype(BF16)
            return c_dec_f[p] * s + kv_s[n, p, 0:pw, :]

        def bwd(i, s, p=p):
            n = nc - 1 - i
            st_s[n, p, pw:2 * pw, :] = s.astype(BF16)
            return c_dec_b[p] * s + kv_s[n, p, pw:2 * pw, :]

        s_f = lax.fori_loop(0, nc, fwd, s_f)
        s_b = lax.fori_loop(0, nc, bwd, s_b)
        if not latent:
            for hh in range(2):
                blk = slice(hh * hd, (hh + 1) * hd)
                sf_ref[2 * p + hh] = s_f[blk, blk]
                sb_ref[2 * p + hh] = s_b[blk, blk]

    def pass3(n, carry):
        base = pl.multiple_of(n * ch, ch)
        zc = z_ref[pl.ds(base, ch), :]
        q = q_s[pl.ds(base, ch), :]
        k = k_s[pl.ds(base, ch), :]
        v = zc[:, 2 * RET_WIDTH:3 * RET_WIDTH]
        gate = zc[:, 3 * RET_WIDTH:4 * RET_WIDTH].astype(F32)
        outs = []
        for p in range(RET_NPAIR):
            lanes = slice(p * pw, (p + 1) * pw)
            qp, kp, vp = q[:, lanes], k[:, lanes], v[:, lanes]
            o_h = []
            for hh in range(2):
                qm = jnp.where(left == (hh == 0), qp, jnp.zeros_like(qp))
                s = _dot_nt(qm, kp) * decay[2 * p + hh]
                o_h.append(jnp.dot(s.astype(BF16), vp, preferred_element_type=F32))
            qf = qp.astype(F32)
            q2 = (jnp.concatenate([qf, qf], axis=-1) * q_dec[p]).astype(BF16)
            o = jnp.where(left, o_h[0], o_h[1]) + jnp.dot(q2, st_s[n, p], preferred_element_type=F32)

            def half_mean(t):
                s_l = jnp.sum(jnp.where(left, t, 0.0), axis=-1, keepdims=True)
                s_r = jnp.sum(jnp.where(left, 0.0, t), axis=-1, keepdims=True)
                return jnp.where(left, s_l, s_r) * (1.0 / hd)

            d = o - half_mean(o)
            outs.append(d * lax.rsqrt(half_mean(d * d) + EPS))
        nrm = jnp.concatenate(outs, axis=-1)
        y_ref[pl.ds(base, ch), :] = (nrm * gn_ref[...] * (gate * _sigmoid(gate))).astype(BF16)
        return carry

    lax.fori_loop(0, nc, pass3, 0, unroll=min(RET_UNROLL, nc))


def _retention(z, lg, gn_g, latent, layer=None, rope=None, s0_f=None, s0_b=None):
    seq = DEC_SEQ if latent else SEQ
    nseq = DEC_BATCH if latent else BATCH
    nc = seq // RET_CHUNK
    row0 = (T_CTX // DEC_SEQ) if latent else 0
    cb = COL_RET // (4 * RET_WIDTH)
    in_specs = [pl.BlockSpec(memory_space=pltpu.SMEM),
                pl.BlockSpec((seq, 4 * RET_WIDTH), lambda s: (row0 + s, cb)),
                _full_spec((1, RET_WIDTH))]
    args = [lg, z, gn_g]
    state_shape = jax.ShapeDtypeStruct((nseq, RET_HEADS, RET_DIM, RET_DIM), F32)
    y_spec = pl.BlockSpec((seq, RET_WIDTH), lambda s: (s, 0))
    y_shape = jax.ShapeDtypeStruct((nseq * seq, RET_WIDTH), BF16)
    if latent:
        st_spec = pl.BlockSpec((None, None, RET_HEADS, RET_DIM, RET_DIM), lambda s: (s, layer, 0, 0, 0))

        def const_spec(shape):
            return pl.BlockSpec(shape, lambda s: (0,) * len(shape), pipeline_mode=pl.Buffered(1))

        in_specs += [const_spec((seq, RET_WIDTH)), const_spec((seq, RET_WIDTH)),
                     const_spec((RET_WIDTH, RET_WIDTH)), st_spec, st_spec]
        args += [rope[0], rope[1], rope[2], s0_f, s0_b]
        out_specs, out_shape = y_spec, y_shape
    else:
        so_spec = pl.BlockSpec((None, RET_HEADS, RET_DIM, RET_DIM), lambda s: (s, 0, 0, 0))
        out_specs, out_shape = [y_spec, so_spec, so_spec], [y_shape, state_shape, state_shape]
    return pl.pallas_call(
        functools.partial(_ret_kernel, seq, latent),
        grid=(nseq,),
        in_specs=in_specs,
        out_specs=out_specs,
        out_shape=out_shape,
        scratch_shapes=[pltpu.VMEM((seq, RET_WIDTH), BF16), pltpu.VMEM((seq, RET_WIDTH), BF16),
                        pltpu.VMEM((nc, RET_NPAIR, 2 * RET_PAIR, RET_PAIR), F32),
                        pltpu.VMEM((nc, RET_NPAIR, 2 * RET_PAIR, RET_PAIR), BF16)],
        compiler_params=_cparams(("arbitrary",)),
        name="retention_lat" if latent else "retention_ctx",
    )(*args)


def _route(logits):
    lane = lax.broadcasted_iota(jnp.int32, logits.shape, 1)
    lane_f = lane.astype(F32)
    big = float(ROUTE_COLS)
    neg = -jnp.inf
    is_grp = lane < N_GROUPS
    gl = jnp.where(is_grp, logits, neg)
    gmax = jnp.max(gl, axis=-1, keepdims=True)
    grp = jnp.min(jnp.where(gl == gmax, lane_f, big), axis=-1, keepdims=True)
    p_grp = 1.0 / jnp.sum(jnp.exp(gl - gmax), axis=-1, keepdims=True)
    e_f = lane_f - N_GROUPS
    lo = grp * EXPERTS_PER_GROUP
    in_grp = (e_f >= lo) & (e_f < lo + EXPERTS_PER_GROUP)
    el = jnp.where(in_grp, logits, neg)
    m1 = jnp.max(el, axis=-1, keepdims=True)
    i1 = jnp.min(jnp.where(el == m1, lane_f, big), axis=-1, keepdims=True)
    el2 = jnp.where(lane_f == i1, neg, el)
    m2 = jnp.max(el2, axis=-1, keepdims=True)
    i2 = jnp.min(jnp.where(el2 == m2, lane_f, big), axis=-1, keepdims=True)
    t = jnp.exp(m2 - m1)
    g1 = p_grp / (1.0 + t)
    g2 = p_grp * t / (1.0 + t)
    rows = logits.shape[0]
    oh1, oh2 = lane_f == i1, lane_f == i2
    oh = jnp.where(oh1 | oh2, 1.0, 0.0)
    tri = (lax.broadcasted_iota(jnp.int32, (rows, rows), 0)
           > lax.broadcasted_iota(jnp.int32, (rows, rows), 1))
    rank = jnp.dot(jnp.where(tri, 1.0, 0.0).astype(BF16), oh.astype(BF16), preferred_element_type=F32)
    tiles = jnp.floor((jnp.sum(oh, axis=0, keepdims=True) + (SUBLANES - 1)) * (1.0 / SUBLANES))
    upper = (lax.broadcasted_iota(jnp.int32, (ROUTE_COLS, ROUTE_COLS), 0)
             < lax.broadcasted_iota(jnp.int32, (ROUTE_COLS, ROUTE_COLS), 1))
    start = SUBLANES * jnp.dot(jnp.broadcast_to(tiles, (SUBLANES, ROUTE_COLS)).astype(BF16),
                               jnp.where(upper, 1.0, 0.0).astype(BF16),
                               preferred_element_type=F32)[0:1, :]
    pos = start + rank
    p1 = jnp.sum(jnp.where(oh1, pos, 0.0), axis=-1, keepdims=True)
    p2 = jnp.sum(jnp.where(oh2, pos, 0.0), axis=-1, keepdims=True)
    out = jnp.zeros(logits.shape, F32)
    for k, val in enumerate((i1 - N_GROUPS, i2 - N_GROUPS, g1, g2, p1, p2)):
        out = jnp.where(lane == k, val, out)
    return out, SUBLANES * tiles


def _outproj_kernel(ycc, ycl, ync, ynl, yrc, yrl, x_ref, mod_ref, g_ref, w_ref, wr_ref, br_ref,
                    xo_ref, xs_ref, r_ref, seg_ref):
    is_ctx = pl.program_id(0) < NB_CTX
    yc = jnp.where(is_ctx, ycc[...], ycl[...])
    yn = jnp.where(is_ctx, ync[...], ynl[...])
    yr = jnp.where(is_ctx, yrc[...], yrl[...])
    y = (jnp.dot(yc, w_ref[0:CONV_CH, :], preferred_element_type=F32)
         + jnp.dot(yn, w_ref[CONV_CH:CONV_CH + NA_WIDTH, :], preferred_element_type=F32)
         + jnp.dot(yr, w_ref[CONV_CH + NA_WIDTH:, :], preferred_element_type=F32))
    x = x_ref[...] + mod_ref[2:3, :] * y
    xo_ref[...] = x
    h = _norm_mod(x, g_ref[...], mod_ref[3:4, :], mod_ref[4:5, :])
    h_hi = h.astype(BF16)
    h_lo = (h - h_hi.astype(F32)).astype(BF16)
    hw = jnp.dot(h_hi, wr_ref[...], preferred_element_type=F32)
    logits = (hw[:, :ROUTE_COLS] + hw[:, ROUTE_COLS:]
              + jnp.dot(h_lo, wr_ref[:, :ROUTE_COLS], preferred_element_type=F32) + br_ref[...])
    route, seg = _route(logits)
    r_ref[...] = route
    seg_ref[...] = jnp.broadcast_to(seg, seg_ref.shape)
    sel = _slot_onehot(route, 0) | _slot_onehot(route, 1)
    xs_ref[...] = _pack_bf16_pairs(lax.dot_general(jnp.where(sel, 1.0, 0.0).astype(BF16), h_hi,
                                                   (((0,), (0,)), ((), ())), preferred_element_type=F32))


def _outproj(y_conv, y_na, y_ret, x, mod, g, w_bf16, w_route, b_route):
    return pl.pallas_call(
        _outproj_kernel,
        grid=(NB_ALL,),
        in_specs=(_ctx_lat_specs(CONV_CH) + _ctx_lat_specs(NA_WIDTH) + _ctx_lat_specs(RET_WIDTH)
                  + [_tok_spec(D_MODEL), _mod_spec(), _full_spec((1, D_MODEL)),
                     _full_spec((D_MODEL, D_MODEL)), _full_spec((D_MODEL, 2 * ROUTE_COLS)),
                     _full_spec((1, ROUTE_COLS))]),
        out_specs=[_tok_spec(D_MODEL), pl.BlockSpec((MOE_LC, D_HALF), lambda i: (i, 0)),
                   _tok_spec(ROUTE_COLS), pl.BlockSpec((None, SUBLANES, ROUTE_COLS), lambda i: (i, 0, 0))],
        out_shape=[jax.ShapeDtypeStruct((T_ALL, D_MODEL), F32),
                   jax.ShapeDtypeStruct((NB_ALL * MOE_LC, D_HALF), U32),
                   jax.ShapeDtypeStruct((T_ALL, ROUTE_COLS), F32),
                   jax.ShapeDtypeStruct((NB_ALL, SUBLANES, ROUTE_COLS), F32)],
        compiler_params=_cparams(("arbitrary",)),
        name="outproj_route",
    )(y_conv[0], y_conv[1], y_na[0], y_na[1], y_ret[0], y_ret[1], x, mod, g, w_bf16, w_route, b_route)


def _dispatch_tables(seg):
    seg_len = seg[:, 0, N_GROUPS:N_GROUPS + N_EXPERTS].astype(jnp.int32)
    experts = jnp.arange(N_EXPERTS, dtype=jnp.int32)
    in_chunk = jnp.cumsum(seg_len, axis=1) - seg_len
    seg_row = in_chunk + MOE_LC * jnp.arange(N_CHUNK, dtype=jnp.int32)[:, None]
    seg_off = jnp.cumsum(seg_len, axis=0) - seg_len
    rows_e = jnp.sum(seg_len, axis=0)
    chunk_rows = jnp.sum(seg_len, axis=1)
    nblk = (rows_e + MOE_BLK - 1) // MOE_BLK
    blk_end = jnp.cumsum(nblk)
    blk_start = blk_end - nblk
    blk = jnp.arange(MOE_NBLK, dtype=jnp.int32)
    n_active = blk_end[-1]
    blk_e = jnp.minimum(jnp.sum((blk_end[None, :] <= jnp.minimum(blk, n_active - 1)[:, None]).astype(jnp.int32),
                                axis=-1), N_EXPERTS - 1)
    mine = blk_e[:, None] == experts[None, :]
    blk_lo = (blk - jnp.sum(jnp.where(mine, blk_start[None, :], 0), axis=-1)) * MOE_BLK
    left = jnp.sum(jnp.where(mine, rows_e[None, :], 0), axis=-1) - blk_lo
    blk_nv = jnp.where(blk < n_active, jnp.clip(left, 0, MOE_BLK), 0).astype(jnp.int32)
    off_b = jnp.sum(jnp.where(mine[:, None, :], seg_off[None, :, :], 0), axis=-1)
    end_b = off_b + jnp.sum(jnp.where(mine[:, None, :], seg_len[None, :, :], 0), axis=-1)
    blk_c0 = jnp.sum((end_b <= blk_lo[:, None]).astype(jnp.int32), axis=-1)
    blk_c1 = jnp.sum((off_b < (blk_lo + blk_nv)[:, None]).astype(jnp.int32), axis=-1)
    after = jnp.sum(jnp.where(mine, blk_end[None, :], 0), axis=-1)
    blk_next_e = jnp.where(after < n_active, jnp.take(blk_e, jnp.minimum(after, MOE_NBLK - 1)), -1)
    row_b = jnp.sum(jnp.where(mine[:, None, :], seg_row[None, :, :], 0), axis=-1)
    first = jnp.maximum(off_b, blk_lo[:, None])
    piece_n = jnp.minimum(end_b, (blk_lo + blk_nv)[:, None]) - first
    piece_src = row_b + first - off_b
    piece_dst = first - blk_lo[:, None]
    return (blk_e, blk_next_e.astype(jnp.int32), blk_nv, blk_c0, blk_c1, piece_src.reshape(-1),
            piece_dst.reshape(-1), piece_n.reshape(-1), chunk_rows)


def _moe_kernel(layer, blk_e, blk_next_e, blk_nv, blk_c0, blk_c1, piece_src, piece_dst, piece_n,
                chunk_rows, xs_hbm, w1_hbm, w3_hbm, w2_hbm, ys_hbm, xbuf, obuf, zeros,
                w1f, w3f, w2f, w1b, w3b, w2b, gsem, ssem, zsem, wsem):
    i = pl.program_id(0)
    last = pl.num_programs(0) - 1
    slot = i % 2

    def tiles(v):
        return pl.multiple_of(v, SUBLANES)

    def for_segments(blk, fn):
        def body(c, carry):
            k = blk * N_CHUNK + c
            n = piece_n[k]

            @pl.when(n > 0)
            def _():
                fn(tiles(piece_src[k]), tiles(piece_dst[k]), tiles(n))

            return carry

        lax.fori_loop(blk_c0[blk], blk_c1[blk], body, 0)

    def weight_copies(e):
        return [pltpu.make_async_copy(src.at[layer, e], dst, wsem)
                for src, dst in ((w1_hbm, w1f), (w3_hbm, w3f), (w2_hbm, w2f))]

    def start_gathers(blk, s):
        for_segments(blk, lambda src, dst, n: pltpu.make_async_copy(
            xs_hbm.at[pl.ds(src, n)], xbuf.at[s, pl.ds(dst, n)], gsem.at[s]).start())

    def start_scatters(blk, s):
        for_segments(blk, lambda dst, src, n: pltpu.make_async_copy(
            obuf.at[s, pl.ds(src, n)], ys_hbm.at[pl.ds(dst, n)], ssem.at[s]).start())

    def wait_rows(blk, s, sem):
        n = tiles(blk_nv[blk])

        @pl.when(n > 0)
        def _():
            pltpu.make_async_copy(xs_hbm.at[pl.ds(0, n)], xbuf.at[s, pl.ds(0, n)], sem.at[s]).wait()

    @pl.when(i == 0)
    def _():
        xbuf[...] = jnp.zeros_like(xbuf)
        zeros[...] = jnp.zeros_like(zeros)

        def tail(c):
            n = tiles(MOE_LC - chunk_rows[c])
            return n, pltpu.make_async_copy(zeros.at[pl.ds(0, n)],
                                            ys_hbm.at[pl.ds(tiles(c * MOE_LC + chunk_rows[c]), n)], zsem)

        def fill(c, carry):
            n, copy = tail(c)
            pl.when(n > 0)(copy.start)
            return carry

        def drain(c, carry):
            n, copy = tail(c)
            pl.when(n > 0)(copy.wait)
            return carry

        lax.fori_loop(0, N_CHUNK, fill, 0)
        lax.fori_loop(0, N_CHUNK, drain, 0)
        start_gathers(0, 0)
        for copy in weight_copies(blk_e[0]):
            copy.start()

    @pl.when(i < last)
    def _():
        start_gathers(i + 1, 1 - slot)

    @pl.when(i >= 2)
    def _():
        wait_rows(i - 2, slot, ssem)

    @pl.when(blk_nv[i] > 0)
    def _():
        @pl.when((i == 0) | (blk_e[i] != blk_e[jnp.maximum(i - 1, 0)]))
        def _():
            for copy in weight_copies(blk_e[i]):
                copy.wait()
            w1b[...] = w1f[...].astype(BF16)
            w3b[...] = w3f[...].astype(BF16)
            w2b[...] = w2f[...].astype(BF16)

            @pl.when(blk_next_e[i] >= 0)
            def _():
                for copy in weight_copies(blk_next_e[i]):
                    copy.start()

        wait_rows(i, slot, gsem)

        def expert_mlp(rows):
            x_lo, x_hi = _unpack_bf16_pairs(xbuf[slot, 0:rows, :])
            n_hid = D_EXPERT // MXU_TILE

            def in_dot(w, t):
                cols = slice(t * MXU_TILE, (t + 1) * MXU_TILE)
                return (jnp.dot(x_lo, w[:D_HALF, cols], preferred_element_type=F32)
                        + jnp.dot(x_hi, w[D_HALF:, cols], preferred_element_type=F32))

            ab = [(in_dot(w1b, t), in_dot(w3b, t)) for t in range(n_hid)]
            mid = [(a * _sigmoid(a) * b).astype(BF16) for a, b in ab]

            def out_dot(t):
                cols = slice(t * MXU_TILE, (t + 1) * MXU_TILE)
                return sum(jnp.dot(mid[j], w2b[j * MXU_TILE:(j + 1) * MXU_TILE, cols],
                                   preferred_element_type=F32) for j in range(n_hid))

            n_word = D_HALF // MXU_TILE
            for t in range(n_word):
                obuf[slot, 0:rows, t * MXU_TILE:(t + 1) * MXU_TILE] = _pack_words(
                    out_dot(t), out_dot(t + n_word))

        for rows in range(MOE_ROW_STEP, MOE_BLK + 1, MOE_ROW_STEP):
            @pl.when((blk_nv[i] > rows - MOE_ROW_STEP) & (blk_nv[i] <= rows))
            def _(rows=rows):
                expert_mlp(rows)

        start_scatters(i, slot)

    @pl.when(i == last)
    def _():
        wait_rows(i - 1, 1 - slot, ssem)
        wait_rows(i, slot, ssem)


def _moe(xs, w1, w3, w2, layer, blk_e, blk_next_e, blk_nv, blk_c0, blk_c1, piece_src, piece_dst,
         piece_n, chunk_rows):
    any_spec = pl.BlockSpec(memory_space=pl.ANY)
    grid_spec = pltpu.PrefetchScalarGridSpec(
        num_scalar_prefetch=9,
        grid=(MOE_NBLK,),
        in_specs=[any_spec, any_spec, any_spec, any_spec],
        out_specs=any_spec,
        scratch_shapes=[pltpu.VMEM((2, MOE_BLK, D_HALF), U32), pltpu.VMEM((2, MOE_BLK, D_HALF), U32),
                        pltpu.VMEM((MOE_LC - 2 * TM, D_HALF), U32),
                        pltpu.VMEM((D_MODEL, D_EXPERT), F32), pltpu.VMEM((D_MODEL, D_EXPERT), F32),
                        pltpu.VMEM((D_EXPERT, D_MODEL), F32),
                        pltpu.VMEM((D_MODEL, D_EXPERT), BF16), pltpu.VMEM((D_MODEL, D_EXPERT), BF16),
                        pltpu.VMEM((D_EXPERT, D_MODEL), BF16),
                        pltpu.SemaphoreType.DMA((2,)), pltpu.SemaphoreType.DMA((2,)),
                        pltpu.SemaphoreType.DMA, pltpu.SemaphoreType.DMA])
    return pl.pallas_call(
        functools.partial(_moe_kernel, layer),
        grid_spec=grid_spec,
        out_shape=jax.ShapeDtypeStruct((NB_ALL * MOE_LC, D_HALF), U32),
        compiler_params=_cparams(("arbitrary",)),
        name="moe_experts",
    )(blk_e, blk_next_e, blk_nv, blk_c0, blk_c1, piece_src, piece_dst, piece_n, chunk_rows,
      xs, w1, w3, w2)


def _final_kernel(x_ref, ys_ref, r_ref, mod_ref, g_ref, o_ref):
    x = _moe_residual(x_ref, ys_ref, r_ref, mod_ref)
    ms = jnp.mean(x * x, axis=-1, keepdims=True)
    o_ref[...] = x * lax.rsqrt(ms + EPS) * g_ref[...]


def _final(x, ys, route, mod, g, block0, nblocks):
    return pl.pallas_call(
        _final_kernel,
        grid=(nblocks,),
        in_specs=[pl.BlockSpec((TM, D_MODEL), lambda i: (block0 + i, 0)),
                  pl.BlockSpec((MOE_LC, D_HALF), lambda i: (block0 + i, 0)),
                  pl.BlockSpec((TM, ROUTE_COLS), lambda i: (block0 + i, 0)),
                  pl.BlockSpec((None, 6, D_MODEL), lambda i: (_cond_row(block0 + i), 0, 0)),
                  _full_spec((1, D_MODEL))],
        out_specs=_tok_spec(D_MODEL),
        out_shape=jax.ShapeDtypeStruct((nblocks * TM, D_MODEL), F32),
        compiler_params=_cparams(("arbitrary",)),
        name="final_norm",
    )(x, ys, route, mod, g)


def kernel(x_prompt, x_sample, c, cache_k, cache_v, state_ret_f, state_ret_b, c_ctx, w_ada, b_ada, norm1_g, norm2_g, w_in, w_out, conv_w, conv_b, conv_ln_g, conv_ln_b, na_rpb, ret_lg_f, ret_lg_b, ret_gn_g, w_route_g, b_route_g, w_route_e, b_route_e, w1, w3, w2, final_g):
    cv = jnp.zeros((COND_ROWS, D_MODEL), F32).at[0].set(c_ctx).at[1:N_COND].set(c)
    mods = _ada(cv, w_ada, b_ada).reshape(DEPTH, COND_ROWS, 6, D_MODEL)
    w_in_b = w_in.astype(BF16)
    w_out_b = w_out.astype(BF16)
    pad = ROUTE_COLS - N_GROUPS - N_EXPERTS
    w_route = jnp.pad(jnp.concatenate([w_route_g, w_route_e], axis=-1), ((0, 0), (0, 0), (0, pad)))
    b_route = jnp.pad(jnp.concatenate([b_route_g, b_route_e], axis=-1), ((0, 0), (0, pad)))
    w_route_hi = w_route.astype(BF16)
    w_route_lo = (w_route - w_route_hi.astype(F32)).astype(BF16)
    w_route = jnp.concatenate([w_route_hi, w_route_lo], axis=-1)
    na_bias = _na_bias_tables(na_rpb)
    rope = _rope_tables()
    lg = jnp.stack([ret_lg_f, ret_lg_b], axis=1)

    x_ctx = x_prompt.reshape(T_CTX, D_MODEL)
    x_lat = x_sample.reshape(T_LAT, D_MODEL)
    x = y = route = new_k = new_v = None
    sf_list, sb_list = [], []
    for l in range(DEPTH):
        g1 = norm1_g[l].reshape(1, D_MODEL)
        if l == 0:
            z, x = _inproj_first(x_ctx, x_lat, mods[l], g1, w_in_b[l])
        else:
            z, x = _inproj_next(x, y, route, mods[l - 1], mods[l], g1, w_in_b[l])
        conv_args = (conv_w[l], conv_b[l].reshape(1, -1), conv_ln_g[l].reshape(1, -1),
                     conv_ln_b[l].reshape(1, -1))
        yc_c = _conv(z, 0, BATCH, SEQ, *conv_args)
        yc_l = _conv(z, T_CTX // DEC_SEQ, DEC_BATCH, DEC_SEQ, *conv_args)
        yn_c, new_k, new_v = _ctx_attn(z, l, new_k, new_v)
        yn_l = _na_attn(z, cache_k, cache_v, na_bias, l)
        gn = ret_gn_g[l].reshape(1, RET_WIDTH)
        yr_c, sf_l, sb_l = _retention(z, lg[l], gn, latent=False)
        yr_l = _retention(z, lg[l], gn, latent=True, layer=l, rope=rope,
                          s0_f=state_ret_f, s0_b=state_ret_b)
        x, xs, route, seg = _outproj((yc_c, yc_l), (yn_c, yn_l), (yr_c, yr_l), x, mods[l],
                                     norm2_g[l].reshape(1, D_MODEL), w_out_b[l], w_route[l],
                                     b_route[l].reshape(1, ROUTE_COLS))
        y = _moe(xs, w1, w3, w2, l, *_dispatch_tables(seg))
        sf_list.append(sf_l)
        sb_list.append(sb_l)
    fg = final_g.reshape(1, D_MODEL)
    y_prompt = _final(x, y, route, mods[DEPTH - 1], fg, 0, NB_CTX).reshape(BATCH, SEQ, D_MODEL)
    y_sample = _final(x, y, route, mods[DEPTH - 1], fg, NB_CTX, NB_LAT).reshape(DEC_BATCH, DEC_SEQ, D_MODEL)
    return (y_prompt, y_sample, new_k, new_v, jnp.stack(sf_list, axis=1), jnp.stack(sb_list, axis=1))
```

```python
import functools

import numpy as np
import jax
import jax.numpy as jnp
from jax import lax
from jax.experimental import pallas as pl
from jax.experimental.pallas import tpu as pltpu

D_MODEL = 1024
BATCH = 32
SEQ = 256
DEPTH = 2
DEC_BATCH = 4
DEC_SEQ = 4096
PAST_LEN = 512
GRID_W = 64
GRID_H = DEC_SEQ // GRID_W
CONV_CH = 256
CONV_K = 31
NA_HEADS = 8
NA_DIM = 64
NA_WIDTH = NA_HEADS * NA_DIM
NA_KH = 8
NA_KW = 16
RET_HEADS = 4
RET_DIM = 64
RET_WIDTH = RET_HEADS * RET_DIM
RET_CHUNK = 128
ROPE_BASE = 10000.0
N_GROUPS = 4
EXPERTS_PER_GROUP = 8
N_EXPERTS = N_GROUPS * EXPERTS_PER_GROUP
D_EXPERT = 512
IN_COLS = 2 * CONV_CH + 3 * NA_WIDTH + 4 * RET_WIDTH
EPS = 1e-6
NEG_INF = -1e30

F32 = jnp.float32
BF16 = jnp.bfloat16
HIGHEST = lax.Precision.HIGHEST

T_CTX = BATCH * SEQ
T_LAT = DEC_BATCH * DEC_SEQ
T_ALL = T_CTX + T_LAT
N_COND = 1 + DEC_BATCH
COND_ROWS = 8

TM = 512
NB_CTX = T_CTX // TM
NB_LAT = T_LAT // TM
NB_ALL = NB_CTX + NB_LAT
LAT_BLOCKS_PER_REQ = DEC_SEQ // TM

LANES = 128
SUBLANES = 8
MXU_TILE = 256
ROUTE_COLS = LANES

COL_CONV = 0
COL_NA_Q = 2 * CONV_CH
COL_NA_K = COL_NA_Q + NA_WIDTH
COL_NA_V = COL_NA_K + NA_WIDTH
COL_RET = COL_NA_V + NA_WIDTH

NA_ROWS = 8
NA_Q = NA_ROWS * GRID_W
NA_KROWS = NA_ROWS + NA_KH
NA_KEYS = NA_KROWS * GRID_W
NA_RB = GRID_H // NA_ROWS

MOE_BLK = 512
MOE_ROW_STEP = 128
MOE_LC = -(-(2 * TM + N_EXPERTS * (SUBLANES - 1)) // LANES) * LANES
N_CHUNK = NB_ALL
MOE_NBLK = -(-(N_CHUNK * MOE_LC) // MOE_BLK) + N_EXPERTS

VMEM_LIMIT = 56 * 1024 * 1024


def _cparams(sem):
    return pltpu.CompilerParams(dimension_semantics=sem, vmem_limit_bytes=VMEM_LIMIT)


def _sigmoid(x):
    return 1.0 / (1.0 + jnp.exp(-x))


def _cond_row(i):
    return jnp.where(i < NB_CTX, 0, 1 + (i - NB_CTX) // LAT_BLOCKS_PER_REQ)


ADA_TN = 1536


def _ada_kernel(cv_ref, w_ref, b_ref, o_ref):
    cv = cv_ref[...]
    s = cv * _sigmoid(cv)
    o_ref[...] = jnp.dot(s, w_ref[...], precision=HIGHEST, preferred_element_type=F32) + b_ref[...]


def _ada(cv, w_ada, b_ada):
    n = 6 * D_MODEL
    return pl.pallas_call(
        _ada_kernel,
        grid=(DEPTH, n // ADA_TN),
        in_specs=[
            pl.BlockSpec((COND_ROWS, D_MODEL), lambda l, j: (0, 0)),
            pl.BlockSpec((None, D_MODEL, ADA_TN), lambda l, j: (l, 0, j)),
            pl.BlockSpec((None, 1, ADA_TN), lambda l, j: (l, 0, j)),
        ],
        out_specs=pl.BlockSpec((None, COND_ROWS, ADA_TN), lambda l, j: (l, 0, j)),
        out_shape=jax.ShapeDtypeStruct((DEPTH, COND_ROWS, n), F32),
        compiler_params=_cparams(("arbitrary", "arbitrary")),
        name="ada_mod",
    )(cv, w_ada, b_ada.reshape(DEPTH, 1, n))


IN_TN = 768


def _norm_mod(x, g, shift, scale):
    ms = jnp.mean(x * x, axis=-1, keepdims=True)
    return (x * lax.rsqrt(ms + EPS) * g) * (1.0 + scale) + shift


def _inproj_body(x, mod_ref, g_ref, w_ref, z_ref):
    h = _norm_mod(x, g_ref[...], mod_ref[0:1, :], mod_ref[1:2, :]).astype(BF16)
    for c in range(IN_COLS // IN_TN):
        cols = slice(c * IN_TN, (c + 1) * IN_TN)
        z_ref[:, cols] = jnp.dot(h, w_ref[:, cols], preferred_element_type=F32).astype(BF16)


def _inproj_first_kernel(xc_ref, xl_ref, mod_ref, g_ref, w_ref, z_ref, xo_ref):
    i = pl.program_id(0)
    x = jnp.where(i < NB_CTX, xc_ref[...], xl_ref[...])
    xo_ref[...] = x
    _inproj_body(x, mod_ref, g_ref, w_ref, z_ref)


U32 = jnp.uint32
D_HALF = D_MODEL // 2
_HI_MASK = np.uint32(0xFFFF0000)


def _pack_words(lo, hi):
    lo = lax.bitcast_convert_type(lo.astype(BF16).astype(F32), U32) >> 16
    hi = lax.bitcast_convert_type(hi.astype(BF16).astype(F32), U32) & _HI_MASK
    return lo | hi


def _pack_bf16_pairs(x):
    return _pack_words(x[:, :D_HALF], x[:, D_HALF:])


def _unpack_bf16_pairs(w):
    lo = lax.bitcast_convert_type(w << 16, F32).astype(BF16)
    hi = lax.bitcast_convert_type(w & _HI_MASK, F32).astype(BF16)
    return lo, hi


def _slot_onehot(route, slot):
    pos = route[:, 4 + slot:5 + slot].astype(jnp.int32)
    return lax.broadcasted_iota(jnp.int32, (route.shape[0], MOE_LC), 1) == pos


def _moe_residual(x_ref, ys_ref, r_ref, mod_ref):
    r = r_ref[...]
    sel = jnp.where(_slot_onehot(r, 0), r[:, 2:3], jnp.where(_slot_onehot(r, 1), r[:, 3:4], 0.0))
    sel = sel.astype(BF16)
    y = jnp.concatenate([jnp.dot(sel, half, preferred_element_type=F32)
                         for half in _unpack_bf16_pairs(ys_ref[...])], axis=-1)
    return x_ref[...] + mod_ref[5:6, :] * y


def _inproj_next_kernel(x_ref, ys_ref, r_ref, modp_ref, mod_ref, g_ref, w_ref, z_ref, xo_ref):
    x = _moe_residual(x_ref, ys_ref, r_ref, modp_ref)
    xo_ref[...] = x
    _inproj_body(x, mod_ref, g_ref, w_ref, z_ref)


def _tok_spec(cols):
    return pl.BlockSpec((TM, cols), lambda i: (i, 0))


def _mod_spec():
    return pl.BlockSpec((None, 6, D_MODEL), lambda i: (_cond_row(i), 0, 0))


def _full_spec(shape):
    return pl.BlockSpec(shape, lambda i: (0,) * len(shape))


def _ctx_lat_specs(cols):
    return [pl.BlockSpec((TM, cols), lambda i: (jnp.minimum(i, NB_CTX - 1), 0)),
            pl.BlockSpec((TM, cols), lambda i: (jnp.maximum(i - NB_CTX, 0), 0))]


def _inproj_first(x_ctx, x_lat, mod, g, w_bf16):
    return pl.pallas_call(
        _inproj_first_kernel,
        grid=(NB_ALL,),
        in_specs=_ctx_lat_specs(D_MODEL) + [_mod_spec(), _full_spec((1, D_MODEL)),
                                            _full_spec((D_MODEL, IN_COLS))],
        out_specs=[_tok_spec(IN_COLS), _tok_spec(D_MODEL)],
        out_shape=[jax.ShapeDtypeStruct((T_ALL, IN_COLS), BF16),
                   jax.ShapeDtypeStruct((T_ALL, D_MODEL), F32)],
        compiler_params=_cparams(("arbitrary",)),
        name="inproj_first",
    )(x_ctx, x_lat, mod, g, w_bf16)


def _inproj_next(x, ys, route, mod_prev, mod, g, w_bf16):
    return pl.pallas_call(
        _inproj_next_kernel,
        grid=(NB_ALL,),
        in_specs=[_tok_spec(D_MODEL),
                  pl.BlockSpec((MOE_LC, D_HALF), lambda i: (i, 0)),
                  _tok_spec(ROUTE_COLS),
                  _mod_spec(), _mod_spec(), _full_spec((1, D_MODEL)),
                  _full_spec((D_MODEL, IN_COLS))],
        out_specs=[_tok_spec(IN_COLS), _tok_spec(D_MODEL)],
        out_shape=[jax.ShapeDtypeStruct((T_ALL, IN_COLS), BF16),
                   jax.ShapeDtypeStruct((T_ALL, D_MODEL), F32)],
        compiler_params=_cparams(("arbitrary",)),
        name="inproj_next",
    )(x, ys, route, mod_prev, mod, g, w_bf16)


CONV_PAD = 16
CONV_CHUNK = 64


CONV_SPAN = CONV_CHUNK + 2 * CONV_PAD - SUBLANES


CONV_UNROLL = 4


def _conv_kernel(seq, z_ref, w_ref, b_ref, g_ref, be_ref, o_ref, upad_ref, shift_refs):
    zeros = jnp.zeros((CONV_PAD, CONV_CH), F32)
    upad_ref[0:CONV_PAD, :] = zeros
    upad_ref[seq + CONV_PAD:seq + 2 * CONV_PAD, :] = zeros

    def glu(ci, carry):
        base = pl.multiple_of(ci * 256, 256)
        zc = z_ref[pl.ds(base, 256), :].astype(F32)
        upad_ref[pl.ds(base + CONV_PAD, 256), :] = zc[:, :CONV_CH] * _sigmoid(zc[:, CONV_CH:])
        return carry

    lax.fori_loop(0, seq // 256, glu, 0)

    shift = CONV_PAD - CONV_K // 2

    def chunk(ci, shift_ref):
        base = pl.multiple_of(ci * CONV_CHUNK, CONV_CHUNK)
        win = upad_ref[pl.ds(base, CONV_CHUNK + 2 * CONV_PAD), :]
        acc = jnp.zeros((CONV_CHUNK, CONV_CH), F32)
        for sub in range(SUBLANES):
            shift_ref[sub] = win[sub:sub + CONV_SPAN, :]
            for k in range(CONV_K):
                if (k + shift) % SUBLANES == sub:
                    lo = k + shift - sub
                    acc = acc + w_ref[k:k + 1, :] * shift_ref[sub, lo:lo + CONV_CHUNK, :]
        acc = acc + b_ref[...]
        mu = jnp.mean(acc, axis=-1, keepdims=True)
        d = acc - mu
        var = jnp.mean(d * d, axis=-1, keepdims=True)
        n = d * lax.rsqrt(var + EPS) * g_ref[...] + be_ref[...]
        o_ref[pl.ds(base, CONV_CHUNK), :] = (n * _sigmoid(n)).astype(BF16)

    def chunks(cj, carry):
        for u in range(CONV_UNROLL):
            chunk(cj * CONV_UNROLL + u, shift_refs.at[u])
        return carry

    lax.fori_loop(0, seq // (CONV_CHUNK * CONV_UNROLL), chunks, 0)


def _conv(z, row_block0, nseq, seq, w, b, g, be):
    return pl.pallas_call(
        functools.partial(_conv_kernel, seq),
        grid=(nseq,),
        in_specs=[pl.BlockSpec((seq, 2 * CONV_CH), lambda s: (row_block0 + s, 0)),
                  _full_spec((CONV_K, CONV_CH)), _full_spec((1, CONV_CH)),
                  _full_spec((1, CONV_CH)), _full_spec((1, CONV_CH))],
        out_specs=pl.BlockSpec((seq, CONV_CH), lambda s: (s, 0)),
        out_shape=jax.ShapeDtypeStruct((nseq * seq, CONV_CH), BF16),
        scratch_shapes=[pltpu.VMEM((seq + 2 * CONV_PAD, CONV_CH), F32),
                        pltpu.VMEM((CONV_UNROLL, SUBLANES, CONV_SPAN, CONV_CH), F32)],
        compiler_params=_cparams(("arbitrary",)),
        name="conv_seq%d" % seq,
    )(z, w, b, g, be)


def _dot_nt(a, b):
    return lax.dot_general(a, b, (((1,), (1,)), ((), ())), preferred_element_type=F32)


NA_SCALE = NA_DIM ** -0.5
assert NA_SCALE == 2.0 ** round(np.log2(NA_SCALE)), "query pre-scaling assumes a power-of-two scale"


def _ctx_attn_kernel(layer, q_ref, k_ref, v_ref, *refs):
    if layer:
        _, _, o_ref, ko_ref, vo_ref = refs
    else:
        o_ref, ko_full, vo_full = refs
        ko_ref, vo_ref = ko_full.at[0], vo_full.at[0]
        for j in range(1, DEPTH):
            ko_full[j] = jnp.zeros(ko_full.shape[1:], F32)
            vo_full[j] = jnp.zeros(vo_full.shape[1:], F32)
    pair = 2 * NA_DIM
    left = lax.broadcasted_iota(jnp.int32, (SEQ, pair), 1) < NA_DIM
    outs = []
    for p in range(NA_HEADS // 2):
        lanes = slice(p * pair, (p + 1) * pair)
        qp, kp, vp = q_ref[:, lanes], k_ref[:, lanes], v_ref[:, lanes]
        kf, vf = kp.astype(F32), vp.astype(F32)
        o_h = []
        for hh in range(2):
            cols = slice(hh * NA_DIM, (hh + 1) * NA_DIM)
            ko_ref[2 * p + hh] = kf[:, cols]
            vo_ref[2 * p + hh] = vf[:, cols]
            qm = jnp.where(left == (hh == 0), qp, jnp.zeros_like(qp))
            s = _dot_nt(qm, kp) * NA_SCALE
            m = jnp.max(s, axis=-1, keepdims=True)
            e = jnp.exp(s - m)
            den = jnp.sum(e, axis=-1, keepdims=True)
            o_h.append(jnp.dot(e.astype(BF16), vp, preferred_element_type=F32) / den)
        outs.append(jnp.where(left, o_h[0], o_h[1]))
    o_ref[...] = jnp.concatenate(outs, axis=-1).astype(BF16)


def _ctx_attn(z, layer, k_prev=None, v_prev=None):
    qb, kb, vb = COL_NA_Q // NA_WIDTH, COL_NA_K // NA_WIDTH, COL_NA_V // NA_WIDTH
    head_shape = jax.ShapeDtypeStruct((BATCH, DEPTH, NA_HEADS, SEQ, NA_DIM), F32)
    head_spec = pl.BlockSpec((None, DEPTH, NA_HEADS, SEQ, NA_DIM), lambda b: (b, 0, 0, 0, 0))
    in_specs = [pl.BlockSpec((SEQ, NA_WIDTH), lambda b: (b, qb)),
                pl.BlockSpec((SEQ, NA_WIDTH), lambda b: (b, kb)),
                pl.BlockSpec((SEQ, NA_WIDTH), lambda b: (b, vb))]
    args = [z, z, z]
    aliases = {}
    if layer:
        any_spec = pl.BlockSpec(memory_space=pl.ANY)
        in_specs += [any_spec, any_spec]
        args += [k_prev, v_prev]
        aliases = {3: 1, 4: 2}
        head_spec = pl.BlockSpec((None, None, NA_HEADS, SEQ, NA_DIM), lambda b: (b, layer, 0, 0, 0))
    return pl.pallas_call(
        functools.partial(_ctx_attn_kernel, layer),
        grid=(BATCH,),
        in_specs=in_specs,
        out_specs=[pl.BlockSpec((SEQ, NA_WIDTH), lambda b: (b, 0)), head_spec, head_spec],
        out_shape=[jax.ShapeDtypeStruct((T_CTX, NA_WIDTH), BF16), head_shape, head_shape],
        input_output_aliases=aliases,
        compiler_params=_cparams(("arbitrary",)),
        name="ctx_attn",
    )(*args)


NA_KINDS = (0, NA_ROWS, GRID_H - NA_ROWS)
N_DR = 2 * NA_KH - 1
N_DC = 2 * NA_KW - 1


def _na_row_offset(r0, i, j):
    ks = min(max(r0 - NA_KH // 2, 0), GRID_H - NA_KROWS)
    r, kr = r0 + i, ks + j
    rs = min(max(r - NA_KH // 2, 0), GRID_H - NA_KH)
    return kr - r + NA_KH - 1 if rs <= kr < rs + NA_KH else None


def _na_bias_kernel(rpb_ref, o_ref):
    lh = pl.program_id(0)
    shape = (GRID_W, 2 * GRID_W)
    qc = lax.broadcasted_iota(jnp.int32, shape, 0)
    lane = lax.broadcasted_iota(jnp.int32, shape, 1)
    kc = lane % GRID_W
    dc = jnp.clip(kc - qc, -(NA_KW - 1), NA_KW - 1) + NA_KW - 1
    cs = jnp.clip(qc - NA_KW // 2, 0, GRID_W - NA_KW)
    col_ok = (kc >= cs) & (kc < cs + NA_KW)
    neg = jnp.full(shape, NEG_INF, F32)
    tiles = []
    for dr in range(N_DR):
        base = (lh * N_DR + dr) * N_DC
        val = jnp.zeros(shape, F32)
        for d in range(N_DC):
            val = jnp.where(dc == d, rpb_ref[base + d], val)
        tiles.append(jnp.where(col_ok, val, neg))
    left = lane < GRID_W
    for kind, r0 in enumerate(NA_KINDS):
        for i in range(NA_ROWS):
            for jp in range(NA_KROWS // 2):
                dl, dr_ = _na_row_offset(r0, i, 2 * jp), _na_row_offset(r0, i, 2 * jp + 1)
                tl = neg if dl is None else tiles[dl]
                tr = neg if dr_ is None else tiles[dr_]
                o_ref[kind, i * GRID_W:(i + 1) * GRID_W, jp * 2 * GRID_W:(jp + 1) * 2 * GRID_W] = (
                    jnp.where(left, tl, tr))


def _na_bias_tables(rpb):
    return pl.pallas_call(
        _na_bias_kernel,
        grid=(DEPTH * NA_HEADS,),
        in_specs=[pl.BlockSpec(memory_space=pltpu.SMEM)],
        out_specs=pl.BlockSpec((None, len(NA_KINDS), NA_Q, NA_KEYS), lambda i: (i, 0, 0, 0)),
        out_shape=jax.ShapeDtypeStruct((DEPTH * NA_HEADS, len(NA_KINDS), NA_Q, NA_KEYS), F32),
        compiler_params=_cparams(("arbitrary",)),
        name="nbr_bias",
    )(rpb.reshape(-1))


NA_G = 4


def _na_kernel(q_ref, k_ref, v_ref, kc_ref, vc_ref, bias_ref, o_ref):
    rb = pl.program_id(2)
    ks = jnp.clip(rb * NA_ROWS - NA_KH // 2, 0, GRID_H - NA_KROWS)
    start = pl.multiple_of(ks * GRID_W, GRID_W)
    q = q_ref[...] * NA_SCALE
    kl = k_ref[pl.ds(start, NA_KEYS), :]
    vl = v_ref[pl.ds(start, NA_KEYS), :]
    pair = 2 * NA_DIM
    left = lax.broadcasted_iota(jnp.int32, (NA_Q, pair), 1) < NA_DIM
    ones_loc = jnp.ones((NA_KEYS, pair), BF16)
    ones_ctx = jnp.ones((PAST_LEN, pair), BF16)
    kc, v_ext, vc_ext = [], [], []
    for p in range(NA_G // 2):
        lanes = slice(p * pair, (p + 1) * pair)
        kc.append(jnp.concatenate([kc_ref[2 * p].astype(BF16), kc_ref[2 * p + 1].astype(BF16)], axis=-1))
        vc = jnp.concatenate([vc_ref[2 * p].astype(BF16), vc_ref[2 * p + 1].astype(BF16)], axis=-1)
        v_ext.append(jnp.concatenate([vl[:, lanes], ones_loc], axis=-1))
        vc_ext.append(jnp.concatenate([vc, ones_ctx], axis=-1))

    def scores(hh):
        p = hh // 2
        lanes = slice(p * pair, (p + 1) * pair)
        qm = jnp.where(left == (hh % 2 == 0), q[:, lanes], jnp.zeros((NA_Q, pair), BF16))
        return _dot_nt(qm, kl[:, lanes]) + bias_ref[hh], _dot_nt(qm, kc[p])

    outs = []
    nxt = scores(0)
    for hh in range(NA_G):
        s_loc, s_ctx = nxt
        if hh + 1 < NA_G:
            nxt = scores(hh + 1)
        m = jnp.maximum(jnp.max(s_loc, axis=-1, keepdims=True), jnp.max(s_ctx, axis=-1, keepdims=True))
        p_loc = jnp.exp(s_loc - m).astype(BF16)
        p_ctx = jnp.exp(s_ctx - m).astype(BF16)
        o = (jnp.dot(p_loc, v_ext[hh // 2], preferred_element_type=F32)
             + jnp.dot(p_ctx, vc_ext[hh // 2], preferred_element_type=F32))
        outs.append(o[:, :pair] / o[:, pair:])
    o_ref[...] = jnp.concatenate([jnp.where(left, outs[2 * p], outs[2 * p + 1]) for p in range(NA_G // 2)],
                                 axis=-1).astype(BF16)


def _na_attn(z, cache_k, cache_v, bias, layer):
    lat_q0 = T_CTX // NA_Q
    lat_s0 = T_CTX // DEC_SEQ
    width = NA_G * NA_DIM
    qc, kc, vc = COL_NA_Q // width, COL_NA_K // width, COL_NA_V // width
    groups = NA_HEADS // NA_G

    def kind(rb):
        return jnp.where(rb == 0, 0, jnp.where(rb == NA_RB - 1, 2, 1))

    ctx_spec = pl.BlockSpec((None, None, NA_G, PAST_LEN, NA_DIM), lambda b, hg, rb: (b, layer, hg, 0, 0))
    return pl.pallas_call(
        _na_kernel,
        grid=(DEC_BATCH, groups, NA_RB),
        in_specs=[pl.BlockSpec((NA_Q, width), lambda b, hg, rb: (lat_q0 + b * NA_RB + rb, qc + hg)),
                  pl.BlockSpec((DEC_SEQ, width), lambda b, hg, rb: (lat_s0 + b, kc + hg)),
                  pl.BlockSpec((DEC_SEQ, width), lambda b, hg, rb: (lat_s0 + b, vc + hg)),
                  ctx_spec, ctx_spec,
                  pl.BlockSpec((NA_G, None, NA_Q, NA_KEYS),
                               lambda b, hg, rb: (layer * groups + hg, kind(rb), 0, 0))],
        out_specs=pl.BlockSpec((NA_Q, width), lambda b, hg, rb: (b * NA_RB + rb, hg)),
        out_shape=jax.ShapeDtypeStruct((T_LAT, NA_WIDTH), BF16),
        compiler_params=_cparams(("arbitrary", "arbitrary", "arbitrary")),
        name="nbr_attn",
    )(z, z, z, cache_k, cache_v, bias)


RET_PAIR = 2 * RET_DIM
RET_NPAIR = RET_HEADS // 2
assert RET_PAIR == LANES and RET_CHUNK == LANES
RET_UNROLL = 8


def _rope_tables():
    n_freq = RET_DIM // 4
    t = np.arange(DEC_SEQ)
    inv = jnp.asarray(ROPE_BASE, F32) ** (-jnp.arange(n_freq, dtype=F32) / n_freq)
    ang_r = jnp.asarray(t // GRID_W, F32)[:, None] * inv[None, :]
    ang_c = jnp.asarray(t % GRID_W, F32)[:, None] * inv[None, :]
    cos = jnp.concatenate([jnp.cos(ang_r)] * 2 + [jnp.cos(ang_c)] * 2, axis=-1)
    sin = jnp.concatenate([-jnp.sin(ang_r), jnp.sin(ang_r), -jnp.sin(ang_c), jnp.sin(ang_c)], axis=-1)
    lane = np.arange(RET_WIDTH)
    src = np.where(lane % (2 * n_freq) < n_freq, lane + n_freq, lane - n_freq)
    swap = np.zeros((RET_WIDTH, RET_WIDTH), np.float32)
    swap[src, lane] = 1.0
    return jnp.tile(cos, (1, RET_HEADS)), jnp.tile(sin, (1, RET_HEADS)), jnp.asarray(swap, BF16)


def _ret_kernel(seq, latent, *refs):
    if latent:
        (lg_ref, z_ref, gn_ref, cos_ref, sin_ref, swap_ref, s0f_ref, s0b_ref, y_ref,
         q_s, k_s, kv_s, st_s) = refs
    else:
        lg_ref, z_ref, gn_ref, y_ref, sf_ref, sb_ref, q_s, k_s, kv_s, st_s = refs
    nc = seq // RET_CHUNK
    ch, hd, pw = RET_CHUNK, RET_DIM, RET_PAIR

    row = lax.broadcasted_iota(jnp.int32, (ch, ch), 0).astype(F32)
    col = lax.broadcasted_iota(jnp.int32, (ch, ch), 1).astype(F32)
    pos = lax.broadcasted_iota(jnp.int32, (ch, pw), 0).astype(F32)
    left = lax.broadcasted_iota(jnp.int32, (ch, pw), 1) < hd
    top = lax.broadcasted_iota(jnp.int32, (pw, pw), 0) < hd
    same_head = top == (lax.broadcasted_iota(jnp.int32, (pw, pw), 1) < hd)
    same_head2 = jnp.concatenate([same_head, same_head], axis=0)

    def per_head(mask, fn, p):
        return jnp.where(mask, fn(2 * p), fn(2 * p + 1))

    decay = []
    for h in range(RET_HEADS):
        lf, lb = lg_ref[0, h], lg_ref[1, h]
        d_f = jnp.where(row >= col, jnp.exp(jnp.maximum(row - col, 0.0) * lf), 0.0)
        d_b = jnp.where(col >= row, jnp.exp(jnp.maximum(col - row, 0.0) * lb), 0.0)
        decay.append(d_f + d_b)
    q_dec, k_dec, c_dec_f, c_dec_b = [], [], [], []
    for p in range(RET_NPAIR):
        q_dec.append(jnp.concatenate(
            [per_head(left, lambda h: jnp.exp((pos + 1.0) * lg_ref[0, h]), p),
             per_head(left, lambda h: jnp.exp((ch - pos) * lg_ref[1, h]), p)], axis=-1))
        k_dec.append(jnp.concatenate(
            [per_head(left, lambda h: jnp.exp((ch - 1.0 - pos) * lg_ref[0, h]), p),
             per_head(left, lambda h: jnp.exp(pos * lg_ref[1, h]), p)], axis=-1))
        zero = jnp.zeros((pw, pw), F32)
        c_dec_f.append(per_head(top, lambda h: jnp.exp(zero + ch * lg_ref[0, h]), p))
        c_dec_b.append(per_head(top, lambda h: jnp.exp(zero + ch * lg_ref[1, h]), p))

    def rope(x, base):
        xf = x.astype(F32)
        if not latent:
            return xf
        swapped = jnp.dot(x, swap_ref[...], preferred_element_type=F32)
        return xf * cos_ref[pl.ds(base, ch), :] + swapped * sin_ref[pl.ds(base, ch), :]

    def pass1(n, carry):
        base = pl.multiple_of(n * ch, ch)
        zc = z_ref[pl.ds(base, ch), :]
        q = rope(zc[:, 0:RET_WIDTH], base)
        k = rope(zc[:, RET_WIDTH:2 * RET_WIDTH], base) * (RET_DIM ** -0.5)
        q_s[pl.ds(base, ch), :] = q.astype(BF16)
        k_s[pl.ds(base, ch), :] = k.astype(BF16)
        v = zc[:, 2 * RET_WIDTH:3 * RET_WIDTH]
        for p in range(RET_NPAIR):
            lanes = slice(p * pw, (p + 1) * pw)
            kp = k[:, lanes]
            k2 = (jnp.concatenate([kp, kp], axis=-1) * k_dec[p]).astype(BF16)
            kv = lax.dot_general(k2, v[:, lanes], (((0,), (0,)), ((), ())), preferred_element_type=F32)
            kv_s[n, p] = jnp.where(same_head2, kv, 0.0)
        return carry

    lax.fori_loop(0, nc, pass1, 0, unroll=min(RET_UNROLL, nc))

    def block_diag(a, b):
        z = jnp.zeros((hd, hd), F32)
        return jnp.concatenate([jnp.concatenate([a, z], axis=1), jnp.concatenate([z, b], axis=1)], axis=0)

    for p in range(RET_NPAIR):
        if latent:
            s_f = block_diag(s0f_ref[2 * p], s0f_ref[2 * p + 1])
            s_b = block_diag(s0b_ref[2 * p], s0b_ref[2 * p + 1])
        else:
            s_f = s_b = jnp.zeros((pw, pw), F32)

        def fwd(n, s, p=p):
            st_s[n, p, 0:pw, :] = s.astype(BF16)
            return c_dec_f[p] * s + kv_s[n, p, 0:pw, :]

        def bwd(i, s, p=p):
            n = nc - 1 - i
            st_s[n, p, pw:2 * pw, :] = s.astype(BF16)
            return c_dec_b[p] * s + kv_s[n, p, pw:2 * pw, :]

        s_f = lax.fori_loop(0, nc, fwd, s_f)
        s_b = lax.fori_loop(0, nc, bwd, s_b)
        if not latent:
            for hh in range(2):
                blk = slice(hh * hd, (hh + 1) * hd)
                sf_ref[2 * p + hh] = s_f[blk, blk]
                sb_ref[2 * p + hh] = s_b[blk, blk]

    def pass3(n, carry):
        base = pl.multiple_of(n * ch, ch)
        zc = z_ref[pl.ds(base, ch), :]
        q = q_s[pl.ds(base, ch), :]
        k = k_s[pl.ds(base, ch), :]
        v = zc[:, 2 * RET_WIDTH:3 * RET_WIDTH]
        gate = zc[:, 3 * RET_WIDTH:4 * RET_WIDTH].astype(F32)
        outs = []
        for p in range(RET_NPAIR):
            lanes = slice(p * pw, (p + 1) * pw)
            qp, kp, vp = q[:, lanes], k[:, lanes], v[:, lanes]
            o_h = []
            for hh in range(2):
                qm = jnp.where(left == (hh == 0), qp, jnp.zeros_like(qp))
                s = _dot_nt(qm, kp) * decay[2 * p + hh]
                o_h.append(jnp.dot(s.astype(BF16), vp, preferred_element_type=F32))
            qf = qp.astype(F32)
            q2 = (jnp.concatenate([qf, qf], axis=-1) * q_dec[p]).astype(BF16)
            o = jnp.where(left, o_h[0], o_h[1]) + jnp.dot(q2, st_s[n, p], preferred_element_type=F32)

            def half_mean(t):
                s_l = jnp.sum(jnp.where(left, t, 0.0), axis=-1, keepdims=True)
                s_r = jnp.sum(jnp.where(left, 0.0, t), axis=-1, keepdims=True)
                return jnp.where(left, s_l, s_r) * (1.0 / hd)

            d = o - half_mean(o)
            outs.append(d * lax.rsqrt(half_mean(d * d) + EPS))
        nrm = jnp.concatenate(outs, axis=-1)
        y_ref[pl.ds(base, ch), :] = (nrm * gn_ref[...] * (gate * _sigmoid(gate))).astype(BF16)
        return carry

    lax.fori_loop(0, nc, pass3, 0, unroll=min(RET_UNROLL, nc))


def _retention(z, lg, gn_g, latent, layer=None, rope=None, s0_f=None, s0_b=None):
    seq = DEC_SEQ if latent else SEQ
    nseq = DEC_BATCH if latent else BATCH
    nc = seq // RET_CHUNK
    row0 = (T_CTX // DEC_SEQ) if latent else 0
    cb = COL_RET // (4 * RET_WIDTH)
    in_specs = [pl.BlockSpec(memory_space=pltpu.SMEM),
                pl.BlockSpec((seq, 4 * RET_WIDTH), lambda s: (row0 + s, cb)),
                _full_spec((1, RET_WIDTH))]
    args = [lg, z, gn_g]
    state_shape = jax.ShapeDtypeStruct((nseq, RET_HEADS, RET_DIM, RET_DIM), F32)
    y_spec = pl.BlockSpec((seq, RET_WIDTH), lambda s: (s, 0))
    y_shape = jax.ShapeDtypeStruct((nseq * seq, RET_WIDTH), BF16)
    if latent:
        st_spec = pl.BlockSpec((None, None, RET_HEADS, RET_DIM, RET_DIM), lambda s: (s, layer, 0, 0, 0))

        def const_spec(shape):
            return pl.BlockSpec(shape, lambda s: (0,) * len(shape), pipeline_mode=pl.Buffered(1))

        in_specs += [const_spec((seq, RET_WIDTH)), const_spec((seq, RET_WIDTH)),
                     const_spec((RET_WIDTH, RET_WIDTH)), st_spec, st_spec]
        args += [rope[0], rope[1], rope[2], s0_f, s0_b]
        out_specs, out_shape = y_spec, y_shape
    else:
        so_spec = pl.BlockSpec((None, RET_HEADS, RET_DIM, RET_DIM), lambda s: (s, 0, 0, 0))
        out_specs, out_shape = [y_spec, so_spec, so_spec], [y_shape, state_shape, state_shape]
    return pl.pallas_call(
        functools.partial(_ret_kernel, seq, latent),
        grid=(nseq,),
        in_specs=in_specs,
        out_specs=out_specs,
        out_shape=out_shape,
        scratch_shapes=[pltpu.VMEM((seq, RET_WIDTH), BF16), pltpu.VMEM((seq, RET_WIDTH), BF16),
                        pltpu.VMEM((nc, RET_NPAIR, 2 * RET_PAIR, RET_PAIR), F32),
                        pltpu.VMEM((nc, RET_NPAIR, 2 * RET_PAIR, RET_PAIR), BF16)],
        compiler_params=_cparams(("arbitrary",)),
        name="retention_lat" if latent else "retention_ctx",
    )(*args)


def _route(logits):
    lane = lax.broadcasted_iota(jnp.int32, logits.shape, 1)
    lane_f = lane.astype(F32)
    big = float(ROUTE_COLS)
    neg = -jnp.inf
    is_grp = lane < N_GROUPS
    gl = jnp.where(is_grp, logits, neg)
    gmax = jnp.max(gl, axis=-1, keepdims=True)
    grp = jnp.min(jnp.where(gl == gmax, lane_f, big), axis=-1, keepdims=True)
    p_grp = 1.0 / jnp.sum(jnp.exp(gl - gmax), axis=-1, keepdims=True)
    e_f = lane_f - N_GROUPS
    lo = grp * EXPERTS_PER_GROUP
    in_grp = (e_f >= lo) & (e_f < lo + EXPERTS_PER_GROUP)
    el = jnp.where(in_grp, logits, neg)
    m1 = jnp.max(el, axis=-1, keepdims=True)
    i1 = jnp.min(jnp.where(el == m1, lane_f, big), axis=-1, keepdims=True)
    el2 = jnp.where(lane_f == i1, neg, el)
    m2 = jnp.max(el2, axis=-1, keepdims=True)
    i2 = jnp.min(jnp.where(el2 == m2, lane_f, big), axis=-1, keepdims=True)
    t = jnp.exp(m2 - m1)
    g1 = p_grp / (1.0 + t)
    g2 = p_grp * t / (1.0 + t)
    rows = logits.shape[0]
    oh1, oh2 = lane_f == i1, lane_f == i2
    oh = jnp.where(oh1 | oh2, 1.0, 0.0)
    tri = (lax.broadcasted_iota(jnp.int32, (rows, rows), 0)
           > lax.broadcasted_iota(jnp.int32, (rows, rows), 1))
    rank = jnp.dot(jnp.where(tri, 1.0, 0.0).astype(BF16), oh.astype(BF16), preferred_element_type=F32)
    tiles = jnp.floor((jnp.sum(oh, axis=0, keepdims=True) + (SUBLANES - 1)) * (1.0 / SUBLANES))
    upper = (lax.broadcasted_iota(jnp.int32, (ROUTE_COLS, ROUTE_COLS), 0)
             < lax.broadcasted_iota(jnp.int32, (ROUTE_COLS, ROUTE_COLS), 1))
    start = SUBLANES * jnp.dot(jnp.broadcast_to(tiles, (SUBLANES, ROUTE_COLS)).astype(BF16),
                               jnp.where(upper, 1.0, 0.0).astype(BF16),
                               preferred_element_type=F32)[0:1, :]
    pos = start + rank
    p1 = jnp.sum(jnp.where(oh1, pos, 0.0), axis=-1, keepdims=True)
    p2 = jnp.sum(jnp.where(oh2, pos, 0.0), axis=-1, keepdims=True)
    out = jnp.zeros(logits.shape, F32)
    for k, val in enumerate((i1 - N_GROUPS, i2 - N_GROUPS, g1, g2, p1, p2)):
        out = jnp.where(lane == k, val, out)
    return out, SUBLANES * tiles


def _outproj_kernel(ycc, ycl, ync, ynl, yrc, yrl, x_ref, mod_ref, g_ref, w_ref, wr_ref, br_ref,
                    xo_ref, xs_ref, r_ref, seg_ref):
    is_ctx = pl.program_id(0) < NB_CTX
    yc = jnp.where(is_ctx, ycc[...], ycl[...])
    yn = jnp.where(is_ctx, ync[...], ynl[...])
    yr = jnp.where(is_ctx, yrc[...], yrl[...])
    y = (jnp.dot(yc, w_ref[0:CONV_CH, :], preferred_element_type=F32)
         + jnp.dot(yn, w_ref[CONV_CH:CONV_CH + NA_WIDTH, :], preferred_element_type=F32)
         + jnp.dot(yr, w_ref[CONV_CH + NA_WIDTH:, :], preferred_element_type=F32))
    x = x_ref[...] + mod_ref[2:3, :] * y
    xo_ref[...] = x
    h = _norm_mod(x, g_ref[...], mod_ref[3:4, :], mod_ref[4:5, :])
    h_hi = h.astype(BF16)
    h_lo = (h - h_hi.astype(F32)).astype(BF16)
    hw = jnp.dot(h_hi, wr_ref[...], preferred_element_type=F32)
    logits = (hw[:, :ROUTE_COLS] + hw[:, ROUTE_COLS:]
              + jnp.dot(h_lo, wr_ref[:, :ROUTE_COLS], preferred_element_type=F32) + br_ref[...])
    route, seg = _route(logits)
    r_ref[...] = route
    seg_ref[...] = jnp.broadcast_to(seg, seg_ref.shape)
    sel = _slot_onehot(route, 0) | _slot_onehot(route, 1)
    xs_ref[...] = _pack_bf16_pairs(lax.dot_general(jnp.where(sel, 1.0, 0.0).astype(BF16), h_hi,
                                                   (((0,), (0,)), ((), ())), preferred_element_type=F32))


def _outproj(y_conv, y_na, y_ret, x, mod, g, w_bf16, w_route, b_route):
    return pl.pallas_call(
        _outproj_kernel,
        grid=(NB_ALL,),
        in_specs=(_ctx_lat_specs(CONV_CH) + _ctx_lat_specs(NA_WIDTH) + _ctx_lat_specs(RET_WIDTH)
                  + [_tok_spec(D_MODEL), _mod_spec(), _full_spec((1, D_MODEL)),
                     _full_spec((D_MODEL, D_MODEL)), _full_spec((D_MODEL, 2 * ROUTE_COLS)),
                     _full_spec((1, ROUTE_COLS))]),
        out_specs=[_tok_spec(D_MODEL), pl.BlockSpec((MOE_LC, D_HALF), lambda i: (i, 0)),
                   _tok_spec(ROUTE_COLS), pl.BlockSpec((None, SUBLANES, ROUTE_COLS), lambda i: (i, 0, 0))],
        out_shape=[jax.ShapeDtypeStruct((T_ALL, D_MODEL), F32),
                   jax.ShapeDtypeStruct((NB_ALL * MOE_LC, D_HALF), U32),
                   jax.ShapeDtypeStruct((T_ALL, ROUTE_COLS), F32),
                   jax.ShapeDtypeStruct((NB_ALL, SUBLANES, ROUTE_COLS), F32)],
        compiler_params=_cparams(("arbitrary",)),
        name="outproj_route",
    )(y_conv[0], y_conv[1], y_na[0], y_na[1], y_ret[0], y_ret[1], x, mod, g, w_bf16, w_route, b_route)


def _dispatch_tables(seg):
    seg_len = seg[:, 0, N_GROUPS:N_GROUPS + N_EXPERTS].astype(jnp.int32)
    experts = jnp.arange(N_EXPERTS, dtype=jnp.int32)
    in_chunk = jnp.cumsum(seg_len, axis=1) - seg_len
    seg_row = in_chunk + MOE_LC * jnp.arange(N_CHUNK, dtype=jnp.int32)[:, None]
    seg_off = jnp.cumsum(seg_len, axis=0) - seg_len
    rows_e = jnp.sum(seg_len, axis=0)
    chunk_rows = jnp.sum(seg_len, axis=1)
    nblk = (rows_e + MOE_BLK - 1) // MOE_BLK
    blk_end = jnp.cumsum(nblk)
    blk_start = blk_end - nblk
    blk = jnp.arange(MOE_NBLK, dtype=jnp.int32)
    n_active = blk_end[-1]
    blk_e = jnp.minimum(jnp.sum((blk_end[None, :] <= jnp.minimum(blk, n_active - 1)[:, None]).astype(jnp.int32),
                                axis=-1), N_EXPERTS - 1)
    mine = blk_e[:, None] == experts[None, :]
    blk_lo = (blk - jnp.sum(jnp.where(mine, blk_start[None, :], 0), axis=-1)) * MOE_BLK
    left = jnp.sum(jnp.where(mine, rows_e[None, :], 0), axis=-1) - blk_lo
    blk_nv = jnp.where(blk < n_active, jnp.clip(left, 0, MOE_BLK), 0).astype(jnp.int32)
    off_b = jnp.sum(jnp.where(mine[:, None, :], seg_off[None, :, :], 0), axis=-1)
    end_b = off_b + jnp.sum(jnp.where(mine[:, None, :], seg_len[None, :, :], 0), axis=-1)
    blk_c0 = jnp.sum((end_b <= blk_lo[:, None]).astype(jnp.int32), axis=-1)
    blk_c1 = jnp.sum((off_b < (blk_lo + blk_nv)[:, None]).astype(jnp.int32), axis=-1)
    after = jnp.sum(jnp.where(mine, blk_end[None, :], 0), axis=-1)
    blk_next_e = jnp.where(after < n_active, jnp.take(blk_e, jnp.minimum(after, MOE_NBLK - 1)), -1)
    row_b = jnp.sum(jnp.where(mine[:, None, :], seg_row[None, :, :], 0), axis=-1)
    first = jnp.maximum(off_b, blk_lo[:, None])
    piece_n = jnp.minimum(end_b, (blk_lo + blk_nv)[:, None]) - first
    piece_src = row_b + first - off_b
    piece_dst = first - blk_lo[:, None]
    return (blk_e, blk_next_e.astype(jnp.int32), blk_nv, blk_c0, blk_c1, piece_src.reshape(-1),
            piece_dst.reshape(-1), piece_n.reshape(-1), chunk_rows)


def _moe_kernel(layer, blk_e, blk_next_e, blk_nv, blk_c0, blk_c1, piece_src, piece_dst, piece_n,
                chunk_rows, xs_hbm, w1_hbm, w3_hbm, w2_hbm, ys_hbm, xbuf, obuf, zeros,
                w1f, w3f, w2f, w1b, w3b, w2b, gsem, ssem, zsem, wsem):
    i = pl.program_id(0)
    last = pl.num_programs(0) - 1
    slot = i % 2

    def tiles(v):
        return pl.multiple_of(v, SUBLANES)

    def for_segments(blk, fn):
        def body(c, carry):
            k = blk * N_CHUNK + c
            n = piece_n[k]

            @pl.when(n > 0)
            def _():
                fn(tiles(piece_src[k]), tiles(piece_dst[k]), tiles(n))

            return carry

        lax.fori_loop(blk_c0[blk], blk_c1[blk], body, 0)

    def weight_copies(e):
        return [pltpu.make_async_copy(src.at[layer, e], dst, wsem)
                for src, dst in ((w1_hbm, w1f), (w3_hbm, w3f), (w2_hbm, w2f))]

    def start_gathers(blk, s):
        for_segments(blk, lambda src, dst, n: pltpu.make_async_copy(
            xs_hbm.at[pl.ds(src, n)], xbuf.at[s, pl.ds(dst, n)], gsem.at[s]).start())

    def start_scatters(blk, s):
        for_segments(blk, lambda dst, src, n: pltpu.make_async_copy(
            obuf.at[s, pl.ds(src, n)], ys_hbm.at[pl.ds(dst, n)], ssem.at[s]).start())

    def wait_rows(blk, s, sem):
        n = tiles(blk_nv[blk])

        @pl.when(n > 0)
        def _():
            pltpu.make_async_copy(xs_hbm.at[pl.ds(0, n)], xbuf.at[s, pl.ds(0, n)], sem.at[s]).wait()

    @pl.when(i == 0)
    def _():
        xbuf[...] = jnp.zeros_like(xbuf)
        zeros[...] = jnp.zeros_like(zeros)

        def tail(c):
            n = tiles(MOE_LC - chunk_rows[c])
            return n, pltpu.make_async_copy(zeros.at[pl.ds(0, n)],
                                            ys_hbm.at[pl.ds(tiles(c * MOE_LC + chunk_rows[c]), n)], zsem)

        def fill(c, carry):
            n, copy = tail(c)
            pl.when(n > 0)(copy.start)
            return carry

        def drain(c, carry):
            n, copy = tail(c)
            pl.when(n > 0)(copy.wait)
            return carry

        lax.fori_loop(0, N_CHUNK, fill, 0)
        lax.fori_loop(0, N_CHUNK, drain, 0)
        start_gathers(0, 0)
        for copy in weight_copies(blk_e[0]):
            copy.start()

    @pl.when(i < last)
    def _():
        start_gathers(i + 1, 1 - slot)

    @pl.when(i >= 2)
    def _():
        wait_rows(i - 2, slot, ssem)

    @pl.when(blk_nv[i] > 0)
    def _():
        @pl.when((i == 0) | (blk_e[i] != blk_e[jnp.maximum(i - 1, 0)]))
        def _():
            for copy in weight_copies(blk_e[i]):
                copy.wait()
            w1b[...] = w1f[...].astype(BF16)
            w3b[...] = w3f[...].astype(BF16)
            w2b[...] = w2f[...].astype(BF16)

            @pl.when(blk_next_e[i] >= 0)
            def _():
                for copy in weight_copies(blk_next_e[i]):
                    copy.start()

        wait_rows(i, slot, gsem)

        def expert_mlp(rows):
            x_lo, x_hi = _unpack_bf16_pairs(xbuf[slot, 0:rows, :])
            n_hid = D_EXPERT // MXU_TILE

            def in_dot(w, t):
                cols = slice(t * MXU_TILE, (t + 1) * MXU_TILE)
                return (jnp.dot(x_lo, w[:D_HALF, cols], preferred_element_type=F32)
                        + jnp.dot(x_hi, w[D_HALF:, cols], preferred_element_type=F32))

            ab = [(in_dot(w1b, t), in_dot(w3b, t)) for t in range(n_hid)]
            mid = [(a * _sigmoid(a) * b).astype(BF16) for a, b in ab]

            def out_dot(t):
                cols = slice(t * MXU_TILE, (t + 1) * MXU_TILE)
                return sum(jnp.dot(mid[j], w2b[j * MXU_TILE:(j + 1) * MXU_TILE, cols],
                                   preferred_element_type=F32) for j in range(n_hid))

            n_word = D_HALF // MXU_TILE
            for t in range(n_word):
                obuf[slot, 0:rows, t * MXU_TILE:(t + 1) * MXU_TILE] = _pack_words(
                    out_dot(t), out_dot(t + n_word))

        for rows in range(MOE_ROW_STEP, MOE_BLK + 1, MOE_ROW_STEP):
            @pl.when((blk_nv[i] > rows - MOE_ROW_STEP) & (blk_nv[i] <= rows))
            def _(rows=rows):
                expert_mlp(rows)

        start_scatters(i, slot)

    @pl.when(i == last)
    def _():
        wait_rows(i - 1, 1 - slot, ssem)
        wait_rows(i, slot, ssem)


def _moe(xs, w1, w3, w2, layer, blk_e, blk_next_e, blk_nv, blk_c0, blk_c1, piece_src, piece_dst,
         piece_n, chunk_rows):
    any_spec = pl.BlockSpec(memory_space=pl.ANY)
    grid_spec = pltpu.PrefetchScalarGridSpec(
        num_scalar_prefetch=9,
        grid=(MOE_NBLK,),
        in_specs=[any_spec, any_spec, any_spec, any_spec],
        out_specs=any_spec,
        scratch_shapes=[pltpu.VMEM((2, MOE_BLK, D_HALF), U32), pltpu.VMEM((2, MOE_BLK, D_HALF), U32),
                        pltpu.VMEM((MOE_LC - 2 * TM, D_HALF), U32),
                        pltpu.VMEM((D_MODEL, D_EXPERT), F32), pltpu.VMEM((D_MODEL, D_EXPERT), F32),
                        pltpu.VMEM((D_EXPERT, D_MODEL), F32),
                        pltpu.VMEM((D_MODEL, D_EXPERT), BF16), pltpu.VMEM((D_MODEL, D_EXPERT), BF16),
                        pltpu.VMEM((D_EXPERT, D_MODEL), BF16),
                        pltpu.SemaphoreType.DMA((2,)), pltpu.SemaphoreType.DMA((2,)),
                        pltpu.SemaphoreType.DMA, pltpu.SemaphoreType.DMA])
    return pl.pallas_call(
        functools.partial(_moe_kernel, layer),
        grid_spec=grid_spec,
        out_shape=jax.ShapeDtypeStruct((NB_ALL * MOE_LC, D_HALF), U32),
        compiler_params=_cparams(("arbitrary",)),
        name="moe_experts",
    )(blk_e, blk_next_e, blk_nv, blk_c0, blk_c1, piece_src, piece_dst, piece_n, chunk_rows,
      xs, w1, w3, w2)


def _final_kernel(x_ref, ys_ref, r_ref, mod_ref, g_ref, o_ref):
    x = _moe_residual(x_ref, ys_ref, r_ref, mod_ref)
    ms = jnp.mean(x * x, axis=-1, keepdims=True)
    o_ref[...] = x * lax.rsqrt(ms + EPS) * g_ref[...]


def _final(x, ys, route, mod, g, block0, nblocks):
    return pl.pallas_call(
        _final_kernel,
        grid=(nblocks,),
        in_specs=[pl.BlockSpec((TM, D_MODEL), lambda i: (block0 + i, 0)),
                  pl.BlockSpec((MOE_LC, D_HALF), lambda i: (block0 + i, 0)),
                  pl.BlockSpec((TM, ROUTE_COLS), lambda i: (block0 + i, 0)),
                  pl.BlockSpec((None, 6, D_MODEL), lambda i: (_cond_row(block0 + i), 0, 0)),
                  _full_spec((1, D_MODEL))],
        out_specs=_tok_spec(D_MODEL),
        out_shape=jax.ShapeDtypeStruct((nblocks * TM, D_MODEL), F32),
        compiler_params=_cparams(("arbitrary",)),
        name="final_norm",
    )(x, ys, route, mod, g)


def kernel(x_prompt, x_sample, c, cache_k, cache_v, state_ret_f, state_ret_b, c_ctx, w_ada, b_ada, norm1_g, norm2_g, w_in, w_out, conv_w, conv_b, conv_ln_g, conv_ln_b, na_rpb, ret_lg_f, ret_lg_b, ret_gn_g, w_route_g, b_route_g, w_route_e, b_route_e, w1, w3, w2, final_g):
    cv = jnp.zeros((COND_ROWS, D_MODEL), F32).at[0].set(c_ctx).at[1:N_COND].set(c)
    mods = _ada(cv, w_ada, b_ada).reshape(DEPTH, COND_ROWS, 6, D_MODEL)
    w_in_b = w_in.astype(BF16)
    w_out_b = w_out.astype(BF16)
    pad = ROUTE_COLS - N_GROUPS - N_EXPERTS
    w_route = jnp.pad(jnp.concatenate([w_route_g, w_route_e], axis=-1), ((0, 0), (0, 0), (0, pad)))
    b_route = jnp.pad(jnp.concatenate([b_route_g, b_route_e], axis=-1), ((0, 0), (0, pad)))
    w_route_hi = w_route.astype(BF16)
    w_route_lo = (w_route - w_route_hi.astype(F32)).astype(BF16)
    w_route = jnp.concatenate([w_route_hi, w_route_lo], axis=-1)
    na_bias = _na_bias_tables(na_rpb)
    rope = _rope_tables()
    lg = jnp.stack([ret_lg_f, ret_lg_b], axis=1)

    x_ctx = x_prompt.reshape(T_CTX, D_MODEL)
    x_lat = x_sample.reshape(T_LAT, D_MODEL)
    x = y = route = new_k = new_v = None
    sf_list, sb_list = [], []
    for l in range(DEPTH):
        g1 = norm1_g[l].reshape(1, D_MODEL)
        if l == 0:
            z, x = _inproj_first(x_ctx, x_lat, mods[l], g1, w_in_b[l])
        else:
            z, x = _inproj_next(x, y, route, mods[l - 1], mods[l], g1, w_in_b[l])
        conv_args = (conv_w[l], conv_b[l].reshape(1, -1), conv_ln_g[l].reshape(1, -1),
                     conv_ln_b[l].reshape(1, -1))
        yc_c = _conv(z, 0, BATCH, SEQ, *conv_args)
        yc_l = _conv(z, T_CTX // DEC_SEQ, DEC_BATCH, DEC_SEQ, *conv_args)
        yn_c, new_k, new_v = _ctx_attn(z, l, new_k, new_v)
        yn_l = _na_attn(z, cache_k, cache_v, na_bias, l)
        gn = ret_gn_g[l].reshape(1, RET_WIDTH)
        yr_c, sf_l, sb_l = _retention(z, lg[l], gn, latent=False)
        yr_l = _retention(z, lg[l], gn, latent=True, layer=l, rope=rope,
                          s0_f=state_ret_f, s0_b=state_ret_b)
        x, xs, route, seg = _outproj((yc_c, yc_l), (yn_c, yn_l), (yr_c, yr_l), x, mods[l],
                                     norm2_g[l].reshape(1, D_MODEL), w_out_b[l], w_route[l],
                                     b_route[l].reshape(1, ROUTE_COLS))
        y = _moe(xs, w1, w3, w2, l, *_dispatch_tables(seg))
        sf_list.append(sf_l)
        sb_list.append(sb_l)
    fg = final_g.reshape(1, D_MODEL)
    y_prompt = _final(x, y, route, mods[DEPTH - 1], fg, 0, NB_CTX).reshape(BATCH, SEQ, D_MODEL)
    y_sample = _final(x, y, route, mods[DEPTH - 1], fg, NB_CTX, NB_LAT).reshape(DEC_BATCH, DEC_SEQ, D_MODEL)
    return (y_prompt, y_sample, new_k, new_v, jnp.stack(sf_list, axis=1), jnp.stack(sb_list, axis=1))
```

```python
import functools

import numpy as np
import jax
import jax.numpy as jnp
from jax import lax
from jax.experimental import pallas as pl
from jax.experimental.pallas import tpu as pltpu

D_MODEL = 1024
BATCH = 32
SEQ = 256
DEPTH = 2
DEC_BATCH = 4
DEC_SEQ = 4096
PAST_LEN = 512
GRID_W = 64
GRID_H = DEC_SEQ // GRID_W
CONV_CH = 256
CONV_K = 31
NA_HEADS = 8
NA_DIM = 64
NA_WIDTH = NA_HEADS * NA_DIM
NA_KH = 8
NA_KW = 16
RET_HEADS = 4
RET_DIM = 64
RET_WIDTH = RET_HEADS * RET_DIM
RET_CHUNK = 128
ROPE_BASE = 10000.0
N_GROUPS = 4
EXPERTS_PER_GROUP = 8
N_EXPERTS = N_GROUPS * EXPERTS_PER_GROUP
D_EXPERT = 512
IN_COLS = 2 * CONV_CH + 3 * NA_WIDTH + 4 * RET_WIDTH
EPS = 1e-6
NEG_INF = -1e30

F32 = jnp.float32
BF16 = jnp.bfloat16
HIGHEST = lax.Precision.HIGHEST

T_CTX = BATCH * SEQ
T_LAT = DEC_BATCH * DEC_SEQ
T_ALL = T_CTX + T_LAT
N_COND = 1 + DEC_BATCH
COND_ROWS = 8

TM = 512
NB_CTX = T_CTX // TM
NB_LAT = T_LAT // TM
NB_ALL = NB_CTX + NB_LAT
LAT_BLOCKS_PER_REQ = DEC_SEQ // TM

LANES = 128
SUBLANES = 8
MXU_TILE = 256
ROUTE_COLS = LANES

COL_CONV = 0
COL_NA_Q = 2 * CONV_CH
COL_NA_K = COL_NA_Q + NA_WIDTH
COL_NA_V = COL_NA_K + NA_WIDTH
COL_RET = COL_NA_V + NA_WIDTH

NA_ROWS = 8
NA_Q = NA_ROWS * GRID_W
NA_KROWS = NA_ROWS + NA_KH
NA_KEYS = NA_KROWS * GRID_W
NA_RB = GRID_H // NA_ROWS

MOE_BLK = 512
MOE_ROW_STEP = 128
MOE_LC = -(-(2 * TM + N_EXPERTS * (SUBLANES - 1)) // LANES) * LANES
N_CHUNK = NB_ALL
MOE_NBLK = -(-(N_CHUNK * MOE_LC) // MOE_BLK) + N_EXPERTS

VMEM_LIMIT = 56 * 1024 * 1024


def _cparams(sem):
    return pltpu.CompilerParams(dimension_semantics=sem, vmem_limit_bytes=VMEM_LIMIT)


def _sigmoid(x):
    return 1.0 / (1.0 + jnp.exp(-x))


def _cond_row(i):
    return jnp.where(i < NB_CTX, 0, 1 + (i - NB_CTX) // LAT_BLOCKS_PER_REQ)


ADA_TN = 1536


def _ada_kernel(cv_ref, w_ref, b_ref, o_ref):
    cv = cv_ref[...]
    s = cv * _sigmoid(cv)
    o_ref[...] = jnp.dot(s, w_ref[...], precision=HIGHEST, preferred_element_type=F32) + b_ref[...]


def _ada(cv, w_ada, b_ada):
    n = 6 * D_MODEL
    return pl.pallas_call(
        _ada_kernel,
        grid=(DEPTH, n // ADA_TN),
        in_specs=[
            pl.BlockSpec((COND_ROWS, D_MODEL), lambda l, j: (0, 0)),
            pl.BlockSpec((None, D_MODEL, ADA_TN), lambda l, j: (l, 0, j)),
            pl.BlockSpec((None, 1, ADA_TN), lambda l, j: (l, 0, j)),
        ],
        out_specs=pl.BlockSpec((None, COND_ROWS, ADA_TN), lambda l, j: (l, 0, j)),
        out_shape=jax.ShapeDtypeStruct((DEPTH, COND_ROWS, n), F32),
        compiler_params=_cparams(("arbitrary", "arbitrary")),
        name="ada_mod",
    )(cv, w_ada, b_ada.reshape(DEPTH, 1, n))


IN_TN = 768


def _norm_mod(x, g, shift, scale):
    ms = jnp.mean(x * x, axis=-1, keepdims=True)
    return (x * lax.rsqrt(ms + EPS) * g) * (1.0 + scale) + shift


def _cast_weights_once(w_ref, wb_ref):
    @pl.when(pl.program_id(0) == 0)
    def _():
        wb_ref[...] = w_ref[...].astype(BF16)


def _layer_weight_spec(layer, rows, cols):
    return pl.BlockSpec((None, rows, cols), lambda i: (layer, 0, 0), pipeline_mode=pl.Buffered(1))


def _inproj_body(x, mod_ref, g_ref, w_ref, wb_ref, z_ref):
    _cast_weights_once(w_ref, wb_ref)
    h = _norm_mod(x, g_ref[...], mod_ref[0:1, :], mod_ref[1:2, :]).astype(BF16)
    for c in range(IN_COLS // IN_TN):
        cols = slice(c * IN_TN, (c + 1) * IN_TN)
        z_ref[:, cols] = jnp.dot(h, wb_ref[:, cols], preferred_element_type=F32).astype(BF16)


def _inproj_first_kernel(xc_ref, xl_ref, mod_ref, g_ref, w_ref, z_ref, xo_ref, wb_ref):
    i = pl.program_id(0)
    x = jnp.where(i < NB_CTX, xc_ref[...], xl_ref[...])
    xo_ref[...] = x
    _inproj_body(x, mod_ref, g_ref, w_ref, wb_ref, z_ref)


U32 = jnp.uint32
D_HALF = D_MODEL // 2
_HI_MASK = np.uint32(0xFFFF0000)


def _pack_words(lo, hi):
    lo = lax.bitcast_convert_type(lo.astype(BF16).astype(F32), U32) >> 16
    hi = lax.bitcast_convert_type(hi.astype(BF16).astype(F32), U32) & _HI_MASK
    return lo | hi


def _pack_bf16_pairs(x):
    return _pack_words(x[:, :D_HALF], x[:, D_HALF:])


def _unpack_bf16_pairs(w):
    lo = lax.bitcast_convert_type(w << 16, F32).astype(BF16)
    hi = lax.bitcast_convert_type(w & _HI_MASK, F32).astype(BF16)
    return lo, hi


def _slot_onehot(route, slot):
    pos = route[:, 4 + slot:5 + slot].astype(jnp.int32)
    return lax.broadcasted_iota(jnp.int32, (route.shape[0], MOE_LC), 1) == pos


def _moe_residual(x_ref, ys_ref, r_ref, mod_ref):
    r = r_ref[...]
    sel = jnp.where(_slot_onehot(r, 0), r[:, 2:3], jnp.where(_slot_onehot(r, 1), r[:, 3:4], 0.0))
    sel = sel.astype(BF16)
    y = jnp.concatenate([jnp.dot(sel, half, preferred_element_type=F32)
                         for half in _unpack_bf16_pairs(ys_ref[...])], axis=-1)
    return x_ref[...] + mod_ref[5:6, :] * y


def _inproj_next_kernel(x_ref, ys_ref, r_ref, modp_ref, mod_ref, g_ref, w_ref, z_ref, xo_ref, wb_ref):
    x = _moe_residual(x_ref, ys_ref, r_ref, modp_ref)
    xo_ref[...] = x
    _inproj_body(x, mod_ref, g_ref, w_ref, wb_ref, z_ref)


def _tok_spec(cols):
    return pl.BlockSpec((TM, cols), lambda i: (i, 0))


def _mod_spec():
    return pl.BlockSpec((None, 6, D_MODEL), lambda i: (_cond_row(i), 0, 0))


def _full_spec(shape):
    return pl.BlockSpec(shape, lambda i: (0,) * len(shape))


def _ctx_lat_specs(cols):
    return [pl.BlockSpec((TM, cols), lambda i: (jnp.minimum(i, NB_CTX - 1), 0)),
            pl.BlockSpec((TM, cols), lambda i: (jnp.maximum(i - NB_CTX, 0), 0))]


def _inproj_first(x_ctx, x_lat, mod, g, w_in, layer):
    return pl.pallas_call(
        _inproj_first_kernel,
        grid=(NB_ALL,),
        in_specs=_ctx_lat_specs(D_MODEL) + [_mod_spec(), _full_spec((1, D_MODEL)),
                                            _layer_weight_spec(layer, D_MODEL, IN_COLS)],
        out_specs=[_tok_spec(IN_COLS), _tok_spec(D_MODEL)],
        out_shape=[jax.ShapeDtypeStruct((T_ALL, IN_COLS), BF16),
                   jax.ShapeDtypeStruct((T_ALL, D_MODEL), F32)],
        scratch_shapes=[pltpu.VMEM((D_MODEL, IN_COLS), BF16)],
        compiler_params=_cparams(("arbitrary",)),
        name="inproj_first",
    )(x_ctx, x_lat, mod, g, w_in)


def _inproj_next(x, ys, route, mod_prev, mod, g, w_in, layer):
    return pl.pallas_call(
        _inproj_next_kernel,
        grid=(NB_ALL,),
        in_specs=[_tok_spec(D_MODEL),
                  pl.BlockSpec((MOE_LC, D_HALF), lambda i: (i, 0)),
                  _tok_spec(ROUTE_COLS),
                  _mod_spec(), _mod_spec(), _full_spec((1, D_MODEL)),
                  _layer_weight_spec(layer, D_MODEL, IN_COLS)],
        out_specs=[_tok_spec(IN_COLS), _tok_spec(D_MODEL)],
        out_shape=[jax.ShapeDtypeStruct((T_ALL, IN_COLS), BF16),
                   jax.ShapeDtypeStruct((T_ALL, D_MODEL), F32)],
        scratch_shapes=[pltpu.VMEM((D_MODEL, IN_COLS), BF16)],
        compiler_params=_cparams(("arbitrary",)),
        name="inproj_next",
    )(x, ys, route, mod_prev, mod, g, w_in)


CONV_PAD = 16
CONV_CHUNK = 64


CONV_SPAN = CONV_CHUNK + 2 * CONV_PAD - SUBLANES


CONV_UNROLL = 4


def _conv_kernel(seq, z_ref, w_ref, b_ref, g_ref, be_ref, o_ref, upad_ref, shift_refs):
    zeros = jnp.zeros((CONV_PAD, CONV_CH), F32)
    upad_ref[0:CONV_PAD, :] = zeros
    upad_ref[seq + CONV_PAD:seq + 2 * CONV_PAD, :] = zeros

    def glu(ci, carry):
        base = pl.multiple_of(ci * 256, 256)
        zc = z_ref[pl.ds(base, 256), :].astype(F32)
        upad_ref[pl.ds(base + CONV_PAD, 256), :] = zc[:, :CONV_CH] * _sigmoid(zc[:, CONV_CH:])
        return carry

    lax.fori_loop(0, seq // 256, glu, 0)

    shift = CONV_PAD - CONV_K // 2

    def chunk(ci, shift_ref):
        base = pl.multiple_of(ci * CONV_CHUNK, CONV_CHUNK)
        win = upad_ref[pl.ds(base, CONV_CHUNK + 2 * CONV_PAD), :]
        acc = jnp.zeros((CONV_CHUNK, CONV_CH), F32)
        for sub in range(SUBLANES):
            shift_ref[sub] = win[sub:sub + CONV_SPAN, :]
            for k in range(CONV_K):
                if (k + shift) % SUBLANES == sub:
                    lo = k + shift - sub
                    acc = acc + w_ref[k:k + 1, :] * shift_ref[sub, lo:lo + CONV_CHUNK, :]
        acc = acc + b_ref[...]
        mu = jnp.mean(acc, axis=-1, keepdims=True)
        d = acc - mu
        var = jnp.mean(d * d, axis=-1, keepdims=True)
        n = d * lax.rsqrt(var + EPS) * g_ref[...] + be_ref[...]
        o_ref[pl.ds(base, CONV_CHUNK), :] = (n * _sigmoid(n)).astype(BF16)

    def chunks(cj, carry):
        for u in range(CONV_UNROLL):
            chunk(cj * CONV_UNROLL + u, shift_refs.at[u])
        return carry

    lax.fori_loop(0, seq // (CONV_CHUNK * CONV_UNROLL), chunks, 0)


def _conv(z, row_block0, nseq, seq, w, b, g, be):
    return pl.pallas_call(
        functools.partial(_conv_kernel, seq),
        grid=(nseq,),
        in_specs=[pl.BlockSpec((seq, 2 * CONV_CH), lambda s: (row_block0 + s, 0)),
                  _full_spec((CONV_K, CONV_CH)), _full_spec((1, CONV_CH)),
                  _full_spec((1, CONV_CH)), _full_spec((1, CONV_CH))],
        out_specs=pl.BlockSpec((seq, CONV_CH), lambda s: (s, 0)),
        out_shape=jax.ShapeDtypeStruct((nseq * seq, CONV_CH), BF16),
        scratch_shapes=[pltpu.VMEM((seq + 2 * CONV_PAD, CONV_CH), F32),
                        pltpu.VMEM((CONV_UNROLL, SUBLANES, CONV_SPAN, CONV_CH), F32)],
        compiler_params=_cparams(("arbitrary",)),
        name="conv_seq%d" % seq,
    )(z, w, b, g, be)


def _dot_nt(a, b):
    return lax.dot_general(a, b, (((1,), (1,)), ((), ())), preferred_element_type=F32)


NA_SCALE = NA_DIM ** -0.5
assert NA_SCALE == 2.0 ** round(np.log2(NA_SCALE)), "query pre-scaling assumes a power-of-two scale"


def _ctx_attn_kernel(layer, q_ref, k_ref, v_ref, *refs):
    if layer:
        _, _, o_ref, ko_ref, vo_ref = refs
    else:
        o_ref, ko_full, vo_full = refs
        ko_ref, vo_ref = ko_full.at[0], vo_full.at[0]
        for j in range(1, DEPTH):
            ko_full[j] = jnp.zeros(ko_full.shape[1:], F32)
            vo_full[j] = jnp.zeros(vo_full.shape[1:], F32)
    pair = 2 * NA_DIM
    left = lax.broadcasted_iota(jnp.int32, (SEQ, pair), 1) < NA_DIM
    outs = []
    for p in range(NA_HEADS // 2):
        lanes = slice(p * pair, (p + 1) * pair)
        qp, kp, vp = q_ref[:, lanes], k_ref[:, lanes], v_ref[:, lanes]
        kf, vf = kp.astype(F32), vp.astype(F32)
        o_h = []
        for hh in range(2):
            cols = slice(hh * NA_DIM, (hh + 1) * NA_DIM)
            ko_ref[2 * p + hh] = kf[:, cols]
            vo_ref[2 * p + hh] = vf[:, cols]
            qm = jnp.where(left == (hh == 0), qp, jnp.zeros_like(qp))
            s = _dot_nt(qm, kp) * NA_SCALE
            m = jnp.max(s, axis=-1, keepdims=True)
            e = jnp.exp(s - m)
            den = jnp.sum(e, axis=-1, keepdims=True)
            o_h.append(jnp.dot(e.astype(BF16), vp, preferred_element_type=F32) / den)
        outs.append(jnp.where(left, o_h[0], o_h[1]))
    o_ref[...] = jnp.concatenate(outs, axis=-1).astype(BF16)


def _ctx_attn(z, layer, k_prev=None, v_prev=None):
    qb, kb, vb = COL_NA_Q // NA_WIDTH, COL_NA_K // NA_WIDTH, COL_NA_V // NA_WIDTH
    head_shape = jax.ShapeDtypeStruct((BATCH, DEPTH, NA_HEADS, SEQ, NA_DIM), F32)
    head_spec = pl.BlockSpec((None, DEPTH, NA_HEADS, SEQ, NA_DIM), lambda b: (b, 0, 0, 0, 0))
    in_specs = [pl.BlockSpec((SEQ, NA_WIDTH), lambda b: (b, qb)),
                pl.BlockSpec((SEQ, NA_WIDTH), lambda b: (b, kb)),
                pl.BlockSpec((SEQ, NA_WIDTH), lambda b: (b, vb))]
    args = [z, z, z]
    aliases = {}
    if layer:
        any_spec = pl.BlockSpec(memory_space=pl.ANY)
        in_specs += [any_spec, any_spec]
        args += [k_prev, v_prev]
        aliases = {3: 1, 4: 2}
        head_spec = pl.BlockSpec((None, None, NA_HEADS, SEQ, NA_DIM), lambda b: (b, layer, 0, 0, 0))
    return pl.pallas_call(
        functools.partial(_ctx_attn_kernel, layer),
        grid=(BATCH,),
        in_specs=in_specs,
        out_specs=[pl.BlockSpec((SEQ, NA_WIDTH), lambda b: (b, 0)), head_spec, head_spec],
        out_shape=[jax.ShapeDtypeStruct((T_CTX, NA_WIDTH), BF16), head_shape, head_shape],
        input_output_aliases=aliases,
        compiler_params=_cparams(("arbitrary",)),
        name="ctx_attn",
    )(*args)


NA_KINDS = (0, NA_ROWS, GRID_H - NA_ROWS)
N_DR = 2 * NA_KH - 1
N_DC = 2 * NA_KW - 1


def _na_row_offset(r0, i, j):
    ks = min(max(r0 - NA_KH // 2, 0), GRID_H - NA_KROWS)
    r, kr = r0 + i, ks + j
    rs = min(max(r - NA_KH // 2, 0), GRID_H - NA_KH)
    return kr - r + NA_KH - 1 if rs <= kr < rs + NA_KH else None


def _na_bias_kernel(rpb_ref, o_ref):
    lh = pl.program_id(0)
    shape = (GRID_W, 2 * GRID_W)
    qc = lax.broadcasted_iota(jnp.int32, shape, 0)
    lane = lax.broadcasted_iota(jnp.int32, shape, 1)
    kc = lane % GRID_W
    dc = jnp.clip(kc - qc, -(NA_KW - 1), NA_KW - 1) + NA_KW - 1
    cs = jnp.clip(qc - NA_KW // 2, 0, GRID_W - NA_KW)
    col_ok = (kc >= cs) & (kc < cs + NA_KW)
    neg = jnp.full(shape, NEG_INF, F32)
    tiles = []
    for dr in range(N_DR):
        base = (lh * N_DR + dr) * N_DC
        val = jnp.zeros(shape, F32)
        for d in range(N_DC):
            val = jnp.where(dc == d, rpb_ref[base + d], val)
        tiles.append(jnp.where(col_ok, val, neg))
    left = lane < GRID_W
    for kind, r0 in enumerate(NA_KINDS):
        for i in range(NA_ROWS):
            for jp in range(NA_KROWS // 2):
                dl, dr_ = _na_row_offset(r0, i, 2 * jp), _na_row_offset(r0, i, 2 * jp + 1)
                tl = neg if dl is None else tiles[dl]
                tr = neg if dr_ is None else tiles[dr_]
                o_ref[kind, i * GRID_W:(i + 1) * GRID_W, jp * 2 * GRID_W:(jp + 1) * 2 * GRID_W] = (
                    jnp.where(left, tl, tr))


def _na_bias_tables(rpb):
    return pl.pallas_call(
        _na_bias_kernel,
        grid=(DEPTH * NA_HEADS,),
        in_specs=[pl.BlockSpec(memory_space=pltpu.SMEM)],
        out_specs=pl.BlockSpec((None, len(NA_KINDS), NA_Q, NA_KEYS), lambda i: (i, 0, 0, 0)),
        out_shape=jax.ShapeDtypeStruct((DEPTH * NA_HEADS, len(NA_KINDS), NA_Q, NA_KEYS), F32),
        compiler_params=_cparams(("arbitrary",)),
        name="nbr_bias",
    )(rpb.reshape(-1))


NA_G = 4


def _na_kernel(q_ref, k_ref, v_ref, kc_ref, vc_ref, bias_ref, o_ref):
    rb = pl.program_id(2)
    ks = jnp.clip(rb * NA_ROWS - NA_KH // 2, 0, GRID_H - NA_KROWS)
    start = pl.multiple_of(ks * GRID_W, GRID_W)
    q = q_ref[...] * NA_SCALE
    kl = k_ref[pl.ds(start, NA_KEYS), :]
    vl = v_ref[pl.ds(start, NA_KEYS), :]
    pair = 2 * NA_DIM
    left = lax.broadcasted_iota(jnp.int32, (NA_Q, pair), 1) < NA_DIM
    ones_loc = jnp.ones((NA_KEYS, pair), BF16)
    ones_ctx = jnp.ones((PAST_LEN, pair), BF16)
    kc, v_ext, vc_ext = [], [], []
    for p in range(NA_G // 2):
        lanes = slice(p * pair, (p + 1) * pair)
        kc.append(jnp.concatenate([kc_ref[2 * p].astype(BF16), kc_ref[2 * p + 1].astype(BF16)], axis=-1))
        vc = jnp.concatenate([vc_ref[2 * p].astype(BF16), vc_ref[2 * p + 1].astype(BF16)], axis=-1)
        v_ext.append(jnp.concatenate([vl[:, lanes], ones_loc], axis=-1))
        vc_ext.append(jnp.concatenate([vc, ones_ctx], axis=-1))

    def scores(hh):
        p = hh // 2
        lanes = slice(p * pair, (p + 1) * pair)
        qm = jnp.where(left == (hh % 2 == 0), q[:, lanes], jnp.zeros((NA_Q, pair), BF16))
        return _dot_nt(qm, kl[:, lanes]) + bias_ref[hh], _dot_nt(qm, kc[p])

    outs = []
    nxt = scores(0)
    for hh in range(NA_G):
        s_loc, s_ctx = nxt
        if hh + 1 < NA_G:
            nxt = scores(hh + 1)
        m = jnp.maximum(jnp.max(s_loc, axis=-1, keepdims=True), jnp.max(s_ctx, axis=-1, keepdims=True))
        p_loc = jnp.exp(s_loc - m).astype(BF16)
        p_ctx = jnp.exp(s_ctx - m).astype(BF16)
        o = (jnp.dot(p_loc, v_ext[hh // 2], preferred_element_type=F32)
             + jnp.dot(p_ctx, vc_ext[hh // 2], preferred_element_type=F32))
        outs.append(o[:, :pair] / o[:, pair:])
    o_ref[...] = jnp.concatenate([jnp.where(left, outs[2 * p], outs[2 * p + 1]) for p in range(NA_G // 2)],
                                 axis=-1).astype(BF16)


def _na_attn(z, cache_k, cache_v, bias, layer):
    lat_q0 = T_CTX // NA_Q
    lat_s0 = T_CTX // DEC_SEQ
    width = NA_G * NA_DIM
    qc, kc, vc = COL_NA_Q // width, COL_NA_K // width, COL_NA_V // width
    groups = NA_HEADS // NA_G

    def kind(rb):
        return jnp.where(rb == 0, 0, jnp.where(rb == NA_RB - 1, 2, 1))

    ctx_spec = pl.BlockSpec((None, None, NA_G, PAST_LEN, NA_DIM), lambda b, hg, rb: (b, layer, hg, 0, 0))
    return pl.pallas_call(
        _na_kernel,
        grid=(DEC_BATCH, groups, NA_RB),
        in_specs=[pl.BlockSpec((NA_Q, width), lambda b, hg, rb: (lat_q0 + b * NA_RB + rb, qc + hg)),
                  pl.BlockSpec((DEC_SEQ, width), lambda b, hg, rb: (lat_s0 + b, kc + hg)),
                  pl.BlockSpec((DEC_SEQ, width), lambda b, hg, rb: (lat_s0 + b, vc + hg)),
                  ctx_spec, ctx_spec,
                  pl.BlockSpec((NA_G, None, NA_Q, NA_KEYS),
                               lambda b, hg, rb: (layer * groups + hg, kind(rb), 0, 0))],
        out_specs=pl.BlockSpec((NA_Q, width), lambda b, hg, rb: (b * NA_RB + rb, hg)),
        out_shape=jax.ShapeDtypeStruct((T_LAT, NA_WIDTH), BF16),
        compiler_params=_cparams(("arbitrary", "arbitrary", "arbitrary")),
        name="nbr_attn",
    )(z, z, z, cache_k, cache_v, bias)


RET_PAIR = 2 * RET_DIM
RET_NPAIR = RET_HEADS // 2
assert RET_PAIR == LANES and RET_CHUNK == LANES
RET_UNROLL = 8


def _rope_tables():
    n_freq = RET_DIM // 4
    t = np.arange(DEC_SEQ)
    inv = jnp.asarray(ROPE_BASE, F32) ** (-jnp.arange(n_freq, dtype=F32) / n_freq)
    ang_r = jnp.asarray(t // GRID_W, F32)[:, None] * inv[None, :]
    ang_c = jnp.asarray(t % GRID_W, F32)[:, None] * inv[None, :]
    cos = jnp.concatenate([jnp.cos(ang_r)] * 2 + [jnp.cos(ang_c)] * 2, axis=-1)
    sin = jnp.concatenate([-jnp.sin(ang_r), jnp.sin(ang_r), -jnp.sin(ang_c), jnp.sin(ang_c)], axis=-1)
    lane = np.arange(RET_WIDTH)
    src = np.where(lane % (2 * n_freq) < n_freq, lane + n_freq, lane - n_freq)
    swap = np.zeros((RET_WIDTH, RET_WIDTH), np.float32)
    swap[src, lane] = 1.0
    return jnp.tile(cos, (1, RET_HEADS)), jnp.tile(sin, (1, RET_HEADS)), jnp.asarray(swap, BF16)


def _ret_kernel(seq, latent, *refs):
    if latent:
        (lg_ref, z_ref, gn_ref, cos_ref, sin_ref, swap_ref, s0f_ref, s0b_ref, y_ref,
         q_s, k_s, kv_s, st_s) = refs
    else:
        lg_ref, z_ref, gn_ref, y_ref, sf_ref, sb_ref, q_s, k_s, kv_s, st_s = refs
    nc = seq // RET_CHUNK
    ch, hd, pw = RET_CHUNK, RET_DIM, RET_PAIR

    row = lax.broadcasted_iota(jnp.int32, (ch, ch), 0).astype(F32)
    col = lax.broadcasted_iota(jnp.int32, (ch, ch), 1).astype(F32)
    pos = lax.broadcasted_iota(jnp.int32, (ch, pw), 0).astype(F32)
    left = lax.broadcasted_iota(jnp.int32, (ch, pw), 1) < hd
    top = lax.broadcasted_iota(jnp.int32, (pw, pw), 0) < hd
    same_head = top == (lax.broadcasted_iota(jnp.int32, (pw, pw), 1) < hd)
    same_head2 = jnp.concatenate([same_head, same_head], axis=0)

    def per_head(mask, fn, p):
        return jnp.where(mask, fn(2 * p), fn(2 * p + 1))

    decay = []
    for h in range(RET_HEADS):
        lf, lb = lg_ref[0, h], lg_ref[1, h]
        d_f = jnp.where(row >= col, jnp.exp(jnp.maximum(row - col, 0.0) * lf), 0.0)
        d_b = jnp.where(col >= row, jnp.exp(jnp.maximum(col - row, 0.0) * lb), 0.0)
        decay.append(d_f + d_b)
    q_dec, k_dec, c_dec_f, c_dec_b = [], [], [], []
    for p in range(RET_NPAIR):
        q_dec.append(jnp.concatenate(
            [per_head(left, lambda h: jnp.exp((pos + 1.0) * lg_ref[0, h]), p),
             per_head(left, lambda h: jnp.exp((ch - pos) * lg_ref[1, h]), p)], axis=-1))
        k_dec.append(jnp.concatenate(
            [per_head(left, lambda h: jnp.exp((ch - 1.0 - pos) * lg_ref[0, h]), p),
             per_head(left, lambda h: jnp.exp(pos * lg_ref[1, h]), p)], axis=-1))
        zero = jnp.zeros((pw, pw), F32)
        c_dec_f.append(per_head(top, lambda h: jnp.exp(zero + ch * lg_ref[0, h]), p))
        c_dec_b.append(per_head(top, lambda h: jnp.exp(zero + ch * lg_ref[1, h]), p))

    def rope(x, base):
        xf = x.astype(F32)
        if not latent:
            return xf
        swapped = jnp.dot(x, swap_ref[...], preferred_element_type=F32)
        return xf * cos_ref[pl.ds(base, ch), :] + swapped * sin_ref[pl.ds(base, ch), :]

    def pass1(n, carry):
        base = pl.multiple_of(n * ch, ch)
        zc = z_ref[pl.ds(base, ch), :]
        q = rope(zc[:, 0:RET_WIDTH], base)
        k = rope(zc[:, RET_WIDTH:2 * RET_WIDTH], base) * (RET_DIM ** -0.5)
        q_s[pl.ds(base, ch), :] = q.astype(BF16)
        k_s[pl.ds(base, ch), :] = k.astype(BF16)
        v = zc[:, 2 * RET_WIDTH:3 * RET_WIDTH]
        for p in range(RET_NPAIR):
            lanes = slice(p * pw, (p + 1) * pw)
            kp = k[:, lanes]
            k2 = (jnp.concatenate([kp, kp], axis=-1) * k_dec[p]).astype(BF16)
            kv = lax.dot_general(k2, v[:, lanes], (((0,), (0,)), ((), ())), preferred_element_type=F32)
            kv_s[n, p] = jnp.where(same_head2, kv, 0.0)
        return carry

    lax.fori_loop(0, nc, pass1, 0, unroll=min(RET_UNROLL, nc))

    def block_diag(a, b):
        z = jnp.zeros((hd, hd), F32)
        return jnp.concatenate([jnp.concatenate([a, z], axis=1), jnp.concatenate([z, b], axis=1)], axis=0)

    for p in range(RET_NPAIR):
        if latent:
            s_f = block_diag(s0f_ref[2 * p], s0f_ref[2 * p + 1])
            s_b = block_diag(s0b_ref[2 * p], s0b_ref[2 * p + 1])
        else:
            s_f = s_b = jnp.zeros((pw, pw), F32)

        def fwd(n, s, p=p):
            st_s[n, p, 0:pw, :] = s.astype(BF16)
            return c_dec_f[p] * s + kv_s[n, p, 0:pw, :]

        def bwd(i, s, p=p):
            n = nc - 1 - i
            st_s[n, p, pw:2 * pw, :] = s.astype(BF16)
            return c_dec_b[p] * s + kv_s[n, p, pw:2 * pw, :]

        s_f = lax.fori_loop(0, nc, fwd, s_f)
        s_b = lax.fori_loop(0, nc, bwd, s_b)
        if not latent:
            for hh in range(2):
                blk = slice(hh * hd, (hh + 1) * hd)
                sf_ref[2 * p + hh] = s_f[blk, blk]
                sb_ref[2 * p + hh] = s_b[blk, blk]

    def pass3(n, carry):
        base = pl.multiple_of(n * ch, ch)
        zc = z_ref[pl.ds(base, ch), :]
        q = q_s[pl.ds(base, ch), :]
        k = k_s[pl.ds(base, ch), :]
        v = zc[:, 2 * RET_WIDTH:3 * RET_WIDTH]
        gate = zc[:, 3 * RET_WIDTH:4 * RET_WIDTH].astype(F32)
        outs = []
        for p in range(RET_NPAIR):
            lanes = slice(p * pw, (p + 1) * pw)
            qp, kp, vp = q[:, lanes], k[:, lanes], v[:, lanes]
            o_h = []
            for hh in range(2):
                qm = jnp.where(left == (hh == 0), qp, jnp.zeros_like(qp))
                s = _dot_nt(qm, kp) * decay[2 * p + hh]
                o_h.append(jnp.dot(s.astype(BF16), vp, preferred_element_type=F32))
            qf = qp.astype(F32)
            q2 = (jnp.concatenate([qf, qf], axis=-1) * q_dec[p]).astype(BF16)
            o = jnp.where(left, o_h[0], o_h[1]) + jnp.dot(q2, st_s[n, p], preferred_element_type=F32)

            def half_mean(t):
                s_l = jnp.sum(jnp.where(left, t, 0.0), axis=-1, keepdims=True)
                s_r = jnp.sum(jnp.where(left, 0.0, t), axis=-1, keepdims=True)
                return jnp.where(left, s_l, s_r) * (1.0 / hd)

            d = o - half_mean(o)
            outs.append(d * lax.rsqrt(half_mean(d * d) + EPS))
        nrm = jnp.concatenate(outs, axis=-1)
        y_ref[pl.ds(base, ch), :] = (nrm * gn_ref[...] * (gate * _sigmoid(gate))).astype(BF16)
        return carry

    lax.fori_loop(0, nc, pass3, 0, unroll=min(RET_UNROLL, nc))


def _retention(z, lg, gn_g, latent, layer=None, rope=None, s0_f=None, s0_b=None):
    seq = DEC_SEQ if latent else SEQ
    nseq = DEC_BATCH if latent else BATCH
    nc = seq // RET_CHUNK
    row0 = (T_CTX // DEC_SEQ) if latent else 0
    cb = COL_RET // (4 * RET_WIDTH)
    in_specs = [pl.BlockSpec(memory_space=pltpu.SMEM),
                pl.BlockSpec((seq, 4 * RET_WIDTH), lambda s: (row0 + s, cb)),
                _full_spec((1, RET_WIDTH))]
    args = [lg, z, gn_g]
    state_shape = jax.ShapeDtypeStruct((nseq, RET_HEADS, RET_DIM, RET_DIM), F32)
    y_spec = pl.BlockSpec((seq, RET_WIDTH), lambda s: (s, 0))
    y_shape = jax.ShapeDtypeStruct((nseq * seq, RET_WIDTH), BF16)
    if latent:
        st_spec = pl.BlockSpec((None, None, RET_HEADS, RET_DIM, RET_DIM), lambda s: (s, layer, 0, 0, 0))

        def const_spec(shape):
            return pl.BlockSpec(shape, lambda s: (0,) * len(shape), pipeline_mode=pl.Buffered(1))

        in_specs += [const_spec((seq, RET_WIDTH)), const_spec((seq, RET_WIDTH)),
                     const_spec((RET_WIDTH, RET_WIDTH)), st_spec, st_spec]
        args += [rope[0], rope[1], rope[2], s0_f, s0_b]
        out_specs, out_shape = y_spec, y_shape
    else:
        so_spec = pl.BlockSpec((None, RET_HEADS, RET_DIM, RET_DIM), lambda s: (s, 0, 0, 0))
        out_specs, out_shape = [y_spec, so_spec, so_spec], [y_shape, state_shape, state_shape]
    return pl.pallas_call(
        functools.partial(_ret_kernel, seq, latent),
        grid=(nseq,),
        in_specs=in_specs,
        out_specs=out_specs,
        out_shape=out_shape,
        scratch_shapes=[pltpu.VMEM((seq, RET_WIDTH), BF16), pltpu.VMEM((seq, RET_WIDTH), BF16),
                        pltpu.VMEM((nc, RET_NPAIR, 2 * RET_PAIR, RET_PAIR), F32),
                        pltpu.VMEM((nc, RET_NPAIR, 2 * RET_PAIR, RET_PAIR), BF16)],
        compiler_params=_cparams(("arbitrary",)),
        name="retention_lat" if latent else "retention_ctx",
    )(*args)


def _route(logits):
    lane = lax.broadcasted_iota(jnp.int32, logits.shape, 1)
    lane_f = lane.astype(F32)
    big = float(ROUTE_COLS)
    neg = -jnp.inf
    is_grp = lane < N_GROUPS
    gl = jnp.where(is_grp, logits, neg)
    gmax = jnp.max(gl, axis=-1, keepdims=True)
    grp = jnp.min(jnp.where(gl == gmax, lane_f, big), axis=-1, keepdims=True)
    p_grp = 1.0 / jnp.sum(jnp.exp(gl - gmax), axis=-1, keepdims=True)
    e_f = lane_f - N_GROUPS
    lo = grp * EXPERTS_PER_GROUP
    in_grp = (e_f >= lo) & (e_f < lo + EXPERTS_PER_GROUP)
    el = jnp.where(in_grp, logits, neg)
    m1 = jnp.max(el, axis=-1, keepdims=True)
    i1 = jnp.min(jnp.where(el == m1, lane_f, big), axis=-1, keepdims=True)
    el2 = jnp.where(lane_f == i1, neg, el)
    m2 = jnp.max(el2, axis=-1, keepdims=True)
    i2 = jnp.min(jnp.where(el2 == m2, lane_f, big), axis=-1, keepdims=True)
    t = jnp.exp(m2 - m1)
    g1 = p_grp / (1.0 + t)
    g2 = p_grp * t / (1.0 + t)
    rows = logits.shape[0]
    oh1, oh2 = lane_f == i1, lane_f == i2
    oh = jnp.where(oh1 | oh2, 1.0, 0.0)
    tri = (lax.broadcasted_iota(jnp.int32, (rows, rows), 0)
           > lax.broadcasted_iota(jnp.int32, (rows, rows), 1))
    rank = jnp.dot(jnp.where(tri, 1.0, 0.0).astype(BF16), oh.astype(BF16), preferred_element_type=F32)
    tiles = jnp.floor((jnp.sum(oh, axis=0, keepdims=True) + (SUBLANES - 1)) * (1.0 / SUBLANES))
    upper = (lax.broadcasted_iota(jnp.int32, (ROUTE_COLS, ROUTE_COLS), 0)
             < lax.broadcasted_iota(jnp.int32, (ROUTE_COLS, ROUTE_COLS), 1))
    start = SUBLANES * jnp.dot(jnp.broadcast_to(tiles, (SUBLANES, ROUTE_COLS)).astype(BF16),
                               jnp.where(upper, 1.0, 0.0).astype(BF16),
                               preferred_element_type=F32)[0:1, :]
    pos = start + rank
    p1 = jnp.sum(jnp.where(oh1, pos, 0.0), axis=-1, keepdims=True)
    p2 = jnp.sum(jnp.where(oh2, pos, 0.0), axis=-1, keepdims=True)
    out = jnp.zeros(logits.shape, F32)
    for k, val in enumerate((i1 - N_GROUPS, i2 - N_GROUPS, g1, g2, p1, p2)):
        out = jnp.where(lane == k, val, out)
    return out, SUBLANES * tiles


def _outproj_kernel(ycc, ycl, ync, ynl, yrc, yrl, x_ref, mod_ref, g_ref, w_ref, wr_ref, br_ref,
                    xo_ref, xs_ref, r_ref, seg_ref, wb_ref):
    _cast_weights_once(w_ref, wb_ref)
    is_ctx = pl.program_id(0) < NB_CTX
    yc = jnp.where(is_ctx, ycc[...], ycl[...])
    yn = jnp.where(is_ctx, ync[...], ynl[...])
    yr = jnp.where(is_ctx, yrc[...], yrl[...])
    y = (jnp.dot(yc, wb_ref[0:CONV_CH, :], preferred_element_type=F32)
         + jnp.dot(yn, wb_ref[CONV_CH:CONV_CH + NA_WIDTH, :], preferred_element_type=F32)
         + jnp.dot(yr, wb_ref[CONV_CH + NA_WIDTH:, :], preferred_element_type=F32))
    x = x_ref[...] + mod_ref[2:3, :] * y
    xo_ref[...] = x
    h = _norm_mod(x, g_ref[...], mod_ref[3:4, :], mod_ref[4:5, :])
    h_hi = h.astype(BF16)
    h_lo = (h - h_hi.astype(F32)).astype(BF16)
    hw = jnp.dot(h_hi, wr_ref[...], preferred_element_type=F32)
    logits = (hw[:, :ROUTE_COLS] + hw[:, ROUTE_COLS:]
              + jnp.dot(h_lo, wr_ref[:, :ROUTE_COLS], preferred_element_type=F32) + br_ref[...])
    route, seg = _route(logits)
    r_ref[...] = route
    seg_ref[...] = jnp.broadcast_to(seg, seg_ref.shape)
    sel = _slot_onehot(route, 0) | _slot_onehot(route, 1)
    xs_ref[...] = _pack_bf16_pairs(lax.dot_general(jnp.where(sel, 1.0, 0.0).astype(BF16), h_hi,
                                                   (((0,), (0,)), ((), ())), preferred_element_type=F32))


def _outproj(y_conv, y_na, y_ret, x, mod, g, w_out, layer, w_route, b_route):
    return pl.pallas_call(
        _outproj_kernel,
        grid=(NB_ALL,),
        in_specs=(_ctx_lat_specs(CONV_CH) + _ctx_lat_specs(NA_WIDTH) + _ctx_lat_specs(RET_WIDTH)
                  + [_tok_spec(D_MODEL), _mod_spec(), _full_spec((1, D_MODEL)),
                     _layer_weight_spec(layer, D_MODEL, D_MODEL), _full_spec((D_MODEL, 2 * ROUTE_COLS)),
                     _full_spec((1, ROUTE_COLS))]),
        out_specs=[_tok_spec(D_MODEL), pl.BlockSpec((MOE_LC, D_HALF), lambda i: (i, 0)),
                   _tok_spec(ROUTE_COLS), pl.BlockSpec((None, SUBLANES, ROUTE_COLS), lambda i: (i, 0, 0))],
        out_shape=[jax.ShapeDtypeStruct((T_ALL, D_MODEL), F32),
                   jax.ShapeDtypeStruct((NB_ALL * MOE_LC, D_HALF), U32),
                   jax.ShapeDtypeStruct((T_ALL, ROUTE_COLS), F32),
                   jax.ShapeDtypeStruct((NB_ALL, SUBLANES, ROUTE_COLS), F32)],
        scratch_shapes=[pltpu.VMEM((D_MODEL, D_MODEL), BF16)],
        compiler_params=_cparams(("arbitrary",)),
        name="outproj_route",
    )(y_conv[0], y_conv[1], y_na[0], y_na[1], y_ret[0], y_ret[1], x, mod, g, w_out, w_route, b_route)


def _dispatch_tables(seg):
    seg_len = seg[:, 0, N_GROUPS:N_GROUPS + N_EXPERTS].astype(jnp.int32)
    experts = jnp.arange(N_EXPERTS, dtype=jnp.int32)
    in_chunk = jnp.cumsum(seg_len, axis=1) - seg_len
    seg_row = in_chunk + MOE_LC * jnp.arange(N_CHUNK, dtype=jnp.int32)[:, None]
    seg_off = jnp.cumsum(seg_len, axis=0) - seg_len
    rows_e = jnp.sum(seg_len, axis=0)
    chunk_rows = jnp.sum(seg_len, axis=1)
    nblk = (rows_e + MOE_BLK - 1) // MOE_BLK
    blk_end = jnp.cumsum(nblk)
    blk_start = blk_end - nblk
    blk = jnp.arange(MOE_NBLK, dtype=jnp.int32)
    n_active = blk_end[-1]
    blk_e = jnp.minimum(jnp.sum((blk_end[None, :] <= jnp.minimum(blk, n_active - 1)[:, None]).astype(jnp.int32),
                                axis=-1), N_EXPERTS - 1)
    mine = blk_e[:, None] == experts[None, :]
    blk_lo = (blk - jnp.sum(jnp.where(mine, blk_start[None, :], 0), axis=-1)) * MOE_BLK
    left = jnp.sum(jnp.where(mine, rows_e[None, :], 0), axis=-1) - blk_lo
    blk_nv = jnp.where(blk < n_active, jnp.clip(left, 0, MOE_BLK), 0).astype(jnp.int32)
    off_b = jnp.sum(jnp.where(mine[:, None, :], seg_off[None, :, :], 0), axis=-1)
    end_b = off_b + jnp.sum(jnp.where(mine[:, None, :], seg_len[None, :, :], 0), axis=-1)
    blk_c0 = jnp.sum((end_b <= blk_lo[:, None]).astype(jnp.int32), axis=-1)
    blk_c1 = jnp.sum((off_b < (blk_lo + blk_nv)[:, None]).astype(jnp.int32), axis=-1)
    after = jnp.sum(jnp.where(mine, blk_end[None, :], 0), axis=-1)
    blk_next_e = jnp.where(after < n_active, jnp.take(blk_e, jnp.minimum(after, MOE_NBLK - 1)), -1)
    row_b = jnp.sum(jnp.where(mine[:, None, :], seg_row[None, :, :], 0), axis=-1)
    first = jnp.maximum(off_b, blk_lo[:, None])
    piece_n = jnp.minimum(end_b, (blk_lo + blk_nv)[:, None]) - first
    piece_src = row_b + first - off_b
    piece_dst = first - blk_lo[:, None]
    return (blk_e, blk_next_e.astype(jnp.int32), blk_nv, blk_c0, blk_c1, piece_src.reshape(-1),
            piece_dst.reshape(-1), piece_n.reshape(-1), chunk_rows)


def _moe_kernel(layer, blk_e, blk_next_e, blk_nv, blk_c0, blk_c1, piece_src, piece_dst, piece_n,
                chunk_rows, xs_hbm, w1_hbm, w3_hbm, w2_hbm, ys_hbm, xbuf, obuf, zeros,
                w1f, w3f, w2f, w1b, w3b, w2b, gsem, ssem, zsem, wsem):
    i = pl.program_id(0)
    last = pl.num_programs(0) - 1
    slot = i % 2

    def tiles(v):
        return pl.multiple_of(v, SUBLANES)

    def for_segments(blk, fn):
        def body(c, carry):
            k = blk * N_CHUNK + c
            n = piece_n[k]

            @pl.when(n > 0)
            def _():
                fn(tiles(piece_src[k]), tiles(piece_dst[k]), tiles(n))

            return carry

        lax.fori_loop(blk_c0[blk], blk_c1[blk], body, 0)

    def weight_copies(e):
        return [pltpu.make_async_copy(src.at[layer, e], dst, wsem)
                for src, dst in ((w1_hbm, w1f), (w3_hbm, w3f), (w2_hbm, w2f))]

    def start_gathers(blk, s):
        for_segments(blk, lambda src, dst, n: pltpu.make_async_copy(
            xs_hbm.at[pl.ds(src, n)], xbuf.at[s, pl.ds(dst, n)], gsem.at[s]).start())

    def start_scatters(blk, s):
        for_segments(blk, lambda dst, src, n: pltpu.make_async_copy(
            obuf.at[s, pl.ds(src, n)], ys_hbm.at[pl.ds(dst, n)], ssem.at[s]).start())

    def wait_rows(blk, s, sem):
        n = tiles(blk_nv[blk])

        @pl.when(n > 0)
        def _():
            pltpu.make_async_copy(xs_hbm.at[pl.ds(0, n)], xbuf.at[s, pl.ds(0, n)], sem.at[s]).wait()

    @pl.when(i == 0)
    def _():
        xbuf[...] = jnp.zeros_like(xbuf)
        zeros[...] = jnp.zeros_like(zeros)

        def tail(c):
            n = tiles(MOE_LC - chunk_rows[c])
            return n, pltpu.make_async_copy(zeros.at[pl.ds(0, n)],
                                            ys_hbm.at[pl.ds(tiles(c * MOE_LC + chunk_rows[c]), n)], zsem)

        def fill(c, carry):
            n, copy = tail(c)
            pl.when(n > 0)(copy.start)
            return carry

        def drain(c, carry):
            n, copy = tail(c)
            pl.when(n > 0)(copy.wait)
            return carry

        lax.fori_loop(0, N_CHUNK, fill, 0)
        lax.fori_loop(0, N_CHUNK, drain, 0)
        start_gathers(0, 0)
        for copy in weight_copies(blk_e[0]):
            copy.start()

    @pl.when(i < last)
    def _():
        start_gathers(i + 1, 1 - slot)

    @pl.when(i >= 2)
    def _():
        wait_rows(i - 2, slot, ssem)

    @pl.when(blk_nv[i] > 0)
    def _():
        @pl.when((i == 0) | (blk_e[i] != blk_e[jnp.maximum(i - 1, 0)]))
        def _():
            for copy in weight_copies(blk_e[i]):
                copy.wait()
            w1b[...] = w1f[...].astype(BF16)
            w3b[...] = w3f[...].astype(BF16)
            w2b[...] = w2f[...].astype(BF16)

            @pl.when(blk_next_e[i] >= 0)
            def _():
                for copy in weight_copies(blk_next_e[i]):
                    copy.start()

        wait_rows(i, slot, gsem)

        def expert_mlp(rows):
            x_lo, x_hi = _unpack_bf16_pairs(xbuf[slot, 0:rows, :])
            n_hid = D_EXPERT // MXU_TILE

            def in_dot(w, t):
                cols = slice(t * MXU_TILE, (t + 1) * MXU_TILE)
                return (jnp.dot(x_lo, w[:D_HALF, cols], preferred_element_type=F32)
                        + jnp.dot(x_hi, w[D_HALF:, cols], preferred_element_type=F32))

            ab = [(in_dot(w1b, t), in_dot(w3b, t)) for t in range(n_hid)]
            mid = [(a * _sigmoid(a) * b).astype(BF16) for a, b in ab]

            def out_dot(t):
                cols = slice(t * MXU_TILE, (t + 1) * MXU_TILE)
                return sum(jnp.dot(mid[j], w2b[j * MXU_TILE:(j + 1) * MXU_TILE, cols],
                                   preferred_element_type=F32) for j in range(n_hid))

            n_word = D_HALF // MXU_TILE
            for t in range(n_word):
                obuf[slot, 0:rows, t * MXU_TILE:(t + 1) * MXU_TILE] = _pack_words(
                    out_dot(t), out_dot(t + n_word))

        for rows in range(MOE_ROW_STEP, MOE_BLK + 1, MOE_ROW_STEP):
            @pl.when((blk_nv[i] > rows - MOE_ROW_STEP) & (blk_nv[i] <= rows))
            def _(rows=rows):
                expert_mlp(rows)

        start_scatters(i, slot)

    @pl.when(i == last)
    def _():
        wait_rows(i - 1, 1 - slot, ssem)
        wait_rows(i, slot, ssem)


def _moe(xs, w1, w3, w2, layer, blk_e, blk_next_e, blk_nv, blk_c0, blk_c1, piece_src, piece_dst,
         piece_n, chunk_rows):
    any_spec = pl.BlockSpec(memory_space=pl.ANY)
    grid_spec = pltpu.PrefetchScalarGridSpec(
        num_scalar_prefetch=9,
        grid=(MOE_NBLK,),
        in_specs=[any_spec, any_spec, any_spec, any_spec],
        out_specs=any_spec,
        scratch_shapes=[pltpu.VMEM((2, MOE_BLK, D_HALF), U32), pltpu.VMEM((2, MOE_BLK, D_HALF), U32),
                        pltpu.VMEM((MOE_LC - 2 * TM, D_HALF), U32),
                        pltpu.VMEM((D_MODEL, D_EXPERT), F32), pltpu.VMEM((D_MODEL, D_EXPERT), F32),
                        pltpu.VMEM((D_EXPERT, D_MODEL), F32),
                        pltpu.VMEM((D_MODEL, D_EXPERT), BF16), pltpu.VMEM((D_MODEL, D_EXPERT), BF16),
                        pltpu.VMEM((D_EXPERT, D_MODEL), BF16),
                        pltpu.SemaphoreType.DMA((2,)), pltpu.SemaphoreType.DMA((2,)),
                        pltpu.SemaphoreType.DMA, pltpu.SemaphoreType.DMA])
    return pl.pallas_call(
        functools.partial(_moe_kernel, layer),
        grid_spec=grid_spec,
        out_shape=jax.ShapeDtypeStruct((NB_ALL * MOE_LC, D_HALF), U32),
        compiler_params=_cparams(("arbitrary",)),
        name="moe_experts",
    )(blk_e, blk_next_e, blk_nv, blk_c0, blk_c1, piece_src, piece_dst, piece_n, chunk_rows,
      xs, w1, w3, w2)


def _final_kernel(x_ref, ys_ref, r_ref, mod_ref, g_ref, o_ref):
    x = _moe_residual(x_ref, ys_ref, r_ref, mod_ref)
    ms = jnp.mean(x * x, axis=-1, keepdims=True)
    o_ref[...] = x * lax.rsqrt(ms + EPS) * g_ref[...]


def _final(x, ys, route, mod, g, block0, nblocks):
    return pl.pallas_call(
        _final_kernel,
        grid=(nblocks,),
        in_specs=[pl.BlockSpec((TM, D_MODEL), lambda i: (block0 + i, 0)),
                  pl.BlockSpec((MOE_LC, D_HALF), lambda i: (block0 + i, 0)),
                  pl.BlockSpec((TM, ROUTE_COLS), lambda i: (block0 + i, 0)),
                  pl.BlockSpec((None, 6, D_MODEL), lambda i: (_cond_row(block0 + i), 0, 0)),
                  _full_spec((1, D_MODEL))],
        out_specs=_tok_spec(D_MODEL),
        out_shape=jax.ShapeDtypeStruct((nblocks * TM, D_MODEL), F32),
        compiler_params=_cparams(("arbitrary",)),
        name="final_norm",
    )(x, ys, route, mod, g)


def kernel(x_prompt, x_sample, c, cache_k, cache_v, state_ret_f, state_ret_b, c_ctx, w_ada, b_ada, norm1_g, norm2_g, w_in, w_out, conv_w, conv_b, conv_ln_g, conv_ln_b, na_rpb, ret_lg_f, ret_lg_b, ret_gn_g, w_route_g, b_route_g, w_route_e, b_route_e, w1, w3, w2, final_g):
    cv = jnp.zeros((COND_ROWS, D_MODEL), F32).at[0].set(c_ctx).at[1:N_COND].set(c)
    mods = _ada(cv, w_ada, b_ada).reshape(DEPTH, COND_ROWS, 6, D_MODEL)
    pad = ROUTE_COLS - N_GROUPS - N_EXPERTS
    w_route = jnp.pad(jnp.concatenate([w_route_g, w_route_e], axis=-1), ((0, 0), (0, 0), (0, pad)))
    b_route = jnp.pad(jnp.concatenate([b_route_g, b_route_e], axis=-1), ((0, 0), (0, pad)))
    w_route_hi = w_route.astype(BF16)
    w_route_lo = (w_route - w_route_hi.astype(F32)).astype(BF16)
    w_route = jnp.concatenate([w_route_hi, w_route_lo], axis=-1)
    na_bias = _na_bias_tables(na_rpb)
    rope = _rope_tables()
    lg = jnp.stack([ret_lg_f, ret_lg_b], axis=1)

    x_ctx = x_prompt.reshape(T_CTX, D_MODEL)
    x_lat = x_sample.reshape(T_LAT, D_MODEL)
    x = y = route = new_k = new_v = None
    sf_list, sb_list = [], []
    for l in range(DEPTH):
        g1 = norm1_g[l].reshape(1, D_MODEL)
        if l == 0:
            z, x = _inproj_first(x_ctx, x_lat, mods[l], g1, w_in, l)
        else:
            z, x = _inproj_next(x, y, route, mods[l - 1], mods[l], g1, w_in, l)
        conv_args = (conv_w[l], conv_b[l].reshape(1, -1), conv_ln_g[l].reshape(1, -1),
                     conv_ln_b[l].reshape(1, -1))
        yc_c = _conv(z, 0, BATCH, SEQ, *conv_args)
        yc_l = _conv(z, T_CTX // DEC_SEQ, DEC_BATCH, DEC_SEQ, *conv_args)
        yn_c, new_k, new_v = _ctx_attn(z, l, new_k, new_v)
        yn_l = _na_attn(z, cache_k, cache_v, na_bias, l)
        gn = ret_gn_g[l].reshape(1, RET_WIDTH)
        yr_c, sf_l, sb_l = _retention(z, lg[l], gn, latent=False)
        yr_l = _retention(z, lg[l], gn, latent=True, layer=l, rope=rope,
                          s0_f=state_ret_f, s0_b=state_ret_b)
        x, xs, route, seg = _outproj((yc_c, yc_l), (yn_c, yn_l), (yr_c, yr_l), x, mods[l],
                                     norm2_g[l].reshape(1, D_MODEL), w_out, l, w_route[l],
                                     b_route[l].reshape(1, ROUTE_COLS))
        y = _moe(xs, w1, w3, w2, l, *_dispatch_tables(seg))
        sf_list.append(sf_l)
        sb_list.append(sb_l)
    fg = final_g.reshape(1, D_MODEL)
    y_prompt = _final(x, y, route, mods[DEPTH - 1], fg, 0, NB_CTX).reshape(BATCH, SEQ, D_MODEL)
    y_sample = _final(x, y, route, mods[DEPTH - 1], fg, NB_CTX, NB_LAT).reshape(DEC_BATCH, DEC_SEQ, D_MODEL)
    return (y_prompt, y_sample, new_k, new_v, jnp.stack(sf_list, axis=1), jnp.stack(sb_list, axis=1))
```

```python
import functools

import numpy as np
import jax
import jax.numpy as jnp
from jax import lax
from jax.experimental import pallas as pl
from jax.experimental.pallas import tpu as pltpu

D_MODEL = 1024
BATCH = 32
SEQ = 256
DEPTH = 2
DEC_BATCH = 4
DEC_SEQ = 4096
PAST_LEN = 512
GRID_W = 64
GRID_H = DEC_SEQ // GRID_W
CONV_CH = 256
CONV_K = 31
NA_HEADS = 8
NA_DIM = 64
NA_WIDTH = NA_HEADS * NA_DIM
NA_KH = 8
NA_KW = 16
RET_HEADS = 4
RET_DIM = 64
RET_WIDTH = RET_HEADS * RET_DIM
RET_CHUNK = 128
ROPE_BASE = 10000.0
N_GROUPS = 4
EXPERTS_PER_GROUP = 8
N_EXPERTS = N_GROUPS * EXPERTS_PER_GROUP
D_EXPERT = 512
IN_COLS = 2 * CONV_CH + 3 * NA_WIDTH + 4 * RET_WIDTH
EPS = 1e-6
NEG_INF = -1e30

F32 = jnp.float32
BF16 = jnp.bfloat16
HIGHEST = lax.Precision.HIGHEST

T_CTX = BATCH * SEQ
T_LAT = DEC_BATCH * DEC_SEQ
T_ALL = T_CTX + T_LAT
N_COND = 1 + DEC_BATCH
COND_ROWS = 8

TM = 512
NB_CTX = T_CTX // TM
NB_LAT = T_LAT // TM
NB_ALL = NB_CTX + NB_LAT
LAT_BLOCKS_PER_REQ = DEC_SEQ // TM

LANES = 128
SUBLANES = 8
MXU_TILE = 256
ROUTE_COLS = LANES

COL_CONV = 0
COL_NA_Q = 2 * CONV_CH
COL_NA_K = COL_NA_Q + NA_WIDTH
COL_NA_V = COL_NA_K + NA_WIDTH
COL_RET = COL_NA_V + NA_WIDTH

NA_ROWS = 8
NA_Q = NA_ROWS * GRID_W
NA_KROWS = NA_ROWS + NA_KH
NA_KEYS = NA_KROWS * GRID_W
NA_RB = GRID_H // NA_ROWS

MOE_BLK = 1024
MOE_ROW_STEP = 128
MOE_LC = -(-(2 * TM + N_EXPERTS * (SUBLANES - 1)) // LANES) * LANES
N_CHUNK = NB_ALL
MOE_NBLK = -(-(N_CHUNK * MOE_LC) // MOE_BLK) + N_EXPERTS

VMEM_LIMIT = 56 * 1024 * 1024


def _cparams(sem):
    return pltpu.CompilerParams(dimension_semantics=sem, vmem_limit_bytes=VMEM_LIMIT)


def _sigmoid(x):
    return 1.0 / (1.0 + jnp.exp(-x))


def _cond_row(i):
    return jnp.where(i < NB_CTX, 0, 1 + (i - NB_CTX) // LAT_BLOCKS_PER_REQ)


ADA_TN = 1536


def _ada_kernel(cv_ref, w_ref, b_ref, o_ref):
    cv = cv_ref[...]
    s = cv * _sigmoid(cv)
    o_ref[...] = jnp.dot(s, w_ref[...], precision=HIGHEST, preferred_element_type=F32) + b_ref[...]


def _ada(cv, w_ada, b_ada):
    n = 6 * D_MODEL
    return pl.pallas_call(
        _ada_kernel,
        grid=(DEPTH, n // ADA_TN),
        in_specs=[
            pl.BlockSpec((COND_ROWS, D_MODEL), lambda l, j: (0, 0)),
            pl.BlockSpec((None, D_MODEL, ADA_TN), lambda l, j: (l, 0, j)),
            pl.BlockSpec((None, 1, ADA_TN), lambda l, j: (l, 0, j)),
        ],
        out_specs=pl.BlockSpec((None, COND_ROWS, ADA_TN), lambda l, j: (l, 0, j)),
        out_shape=jax.ShapeDtypeStruct((DEPTH, COND_ROWS, n), F32),
        compiler_params=_cparams(("arbitrary", "arbitrary")),
        name="ada_mod",
    )(cv, w_ada, b_ada.reshape(DEPTH, 1, n))


IN_TN = 768


def _norm_mod(x, g, shift, scale):
    ms = jnp.mean(x * x, axis=-1, keepdims=True)
    return (x * lax.rsqrt(ms + EPS) * g) * (1.0 + scale) + shift


def _cast_weights_once(w_ref, wb_ref):
    @pl.when(pl.program_id(0) == 0)
    def _():
        wb_ref[...] = w_ref[...].astype(BF16)


def _layer_weight_spec(layer, rows, cols):
    return pl.BlockSpec((None, rows, cols), lambda i: (layer, 0, 0), pipeline_mode=pl.Buffered(1))


def _inproj_body(x, mod_ref, g_ref, w_ref, wb_ref, z_ref):
    _cast_weights_once(w_ref, wb_ref)
    h = _norm_mod(x, g_ref[...], mod_ref[0:1, :], mod_ref[1:2, :]).astype(BF16)
    for c in range(IN_COLS // IN_TN):
        cols = slice(c * IN_TN, (c + 1) * IN_TN)
        z_ref[:, cols] = jnp.dot(h, wb_ref[:, cols], preferred_element_type=F32).astype(BF16)


def _inproj_first_kernel(xc_ref, xl_ref, mod_ref, g_ref, w_ref, z_ref, xo_ref, wb_ref):
    i = pl.program_id(0)
    x = jnp.where(i < NB_CTX, xc_ref[...], xl_ref[...])
    xo_ref[...] = x
    _inproj_body(x, mod_ref, g_ref, w_ref, wb_ref, z_ref)


U32 = jnp.uint32
D_HALF = D_MODEL // 2
_HI_MASK = np.uint32(0xFFFF0000)


def _pack_words(lo, hi):
    lo = lax.bitcast_convert_type(lo.astype(BF16).astype(F32), U32) >> 16
    hi = lax.bitcast_convert_type(hi.astype(BF16).astype(F32), U32) & _HI_MASK
    return lo | hi


def _pack_bf16_pairs(x):
    return _pack_words(x[:, :D_HALF], x[:, D_HALF:])


def _unpack_bf16_pairs(w):
    lo = lax.bitcast_convert_type(w << 16, F32).astype(BF16)
    hi = lax.bitcast_convert_type(w & _HI_MASK, F32).astype(BF16)
    return lo, hi


def _slot_onehot(route, slot):
    pos = route[:, 4 + slot:5 + slot].astype(jnp.int32)
    return lax.broadcasted_iota(jnp.int32, (route.shape[0], MOE_LC), 1) == pos


def _moe_residual(x_ref, ys_ref, r_ref, mod_ref):
    r = r_ref[...]
    sel = jnp.where(_slot_onehot(r, 0), r[:, 2:3], jnp.where(_slot_onehot(r, 1), r[:, 3:4], 0.0))
    sel = sel.astype(BF16)
    y = jnp.concatenate([jnp.dot(sel, half, preferred_element_type=F32)
                         for half in _unpack_bf16_pairs(ys_ref[...])], axis=-1)
    return x_ref[...] + mod_ref[5:6, :] * y


def _inproj_next_kernel(x_ref, ys_ref, r_ref, modp_ref, mod_ref, g_ref, w_ref, z_ref, xo_ref, wb_ref):
    x = _moe_residual(x_ref, ys_ref, r_ref, modp_ref)
    xo_ref[...] = x
    _inproj_body(x, mod_ref, g_ref, w_ref, wb_ref, z_ref)


def _tok_spec(cols):
    return pl.BlockSpec((TM, cols), lambda i: (i, 0))


def _mod_spec():
    return pl.BlockSpec((None, 6, D_MODEL), lambda i: (_cond_row(i), 0, 0))


def _full_spec(shape):
    return pl.BlockSpec(shape, lambda i: (0,) * len(shape))


def _ctx_lat_specs(cols):
    return [pl.BlockSpec((TM, cols), lambda i: (jnp.minimum(i, NB_CTX - 1), 0)),
            pl.BlockSpec((TM, cols), lambda i: (jnp.maximum(i - NB_CTX, 0), 0))]


def _inproj_first(x_ctx, x_lat, mod, g, w_in, layer):
    return pl.pallas_call(
        _inproj_first_kernel,
        grid=(NB_ALL,),
        in_specs=_ctx_lat_specs(D_MODEL) + [_mod_spec(), _full_spec((1, D_MODEL)),
                                            _layer_weight_spec(layer, D_MODEL, IN_COLS)],
        out_specs=[_tok_spec(IN_COLS), _tok_spec(D_MODEL)],
        out_shape=[jax.ShapeDtypeStruct((T_ALL, IN_COLS), BF16),
                   jax.ShapeDtypeStruct((T_ALL, D_MODEL), F32)],
        scratch_shapes=[pltpu.VMEM((D_MODEL, IN_COLS), BF16)],
        compiler_params=_cparams(("arbitrary",)),
        name="inproj_first",
    )(x_ctx, x_lat, mod, g, w_in)


def _inproj_next(x, ys, route, mod_prev, mod, g, w_in, layer):
    return pl.pallas_call(
        _inproj_next_kernel,
        grid=(NB_ALL,),
        in_specs=[_tok_spec(D_MODEL),
                  pl.BlockSpec((MOE_LC, D_HALF), lambda i: (i, 0)),
                  _tok_spec(ROUTE_COLS),
                  _mod_spec(), _mod_spec(), _full_spec((1, D_MODEL)),
                  _layer_weight_spec(layer, D_MODEL, IN_COLS)],
        out_specs=[_tok_spec(IN_COLS), _tok_spec(D_MODEL)],
        out_shape=[jax.ShapeDtypeStruct((T_ALL, IN_COLS), BF16),
                   jax.ShapeDtypeStruct((T_ALL, D_MODEL), F32)],
        scratch_shapes=[pltpu.VMEM((D_MODEL, IN_COLS), BF16)],
        compiler_params=_cparams(("arbitrary",)),
        name="inproj_next",
    )(x, ys, route, mod_prev, mod, g, w_in)


CONV_PAD = 16
CONV_CHUNK = 64


CONV_SPAN = CONV_CHUNK + 2 * CONV_PAD - SUBLANES


CONV_UNROLL = 4


def _conv_kernel(seq, z_ref, w_ref, b_ref, g_ref, be_ref, o_ref, upad_ref, shift_refs):
    zeros = jnp.zeros((CONV_PAD, CONV_CH), F32)
    upad_ref[0:CONV_PAD, :] = zeros
    upad_ref[seq + CONV_PAD:seq + 2 * CONV_PAD, :] = zeros

    def glu(ci, carry):
        base = pl.multiple_of(ci * 256, 256)
        zc = z_ref[pl.ds(base, 256), :].astype(F32)
        upad_ref[pl.ds(base + CONV_PAD, 256), :] = zc[:, :CONV_CH] * _sigmoid(zc[:, CONV_CH:])
        return carry

    lax.fori_loop(0, seq // 256, glu, 0)

    shift = CONV_PAD - CONV_K // 2

    def chunk(ci, shift_ref):
        base = pl.multiple_of(ci * CONV_CHUNK, CONV_CHUNK)
        win = upad_ref[pl.ds(base, CONV_CHUNK + 2 * CONV_PAD), :]
        acc = jnp.zeros((CONV_CHUNK, CONV_CH), F32)
        for sub in range(SUBLANES):
            shift_ref[sub] = win[sub:sub + CONV_SPAN, :]
            for k in range(CONV_K):
                if (k + shift) % SUBLANES == sub:
                    lo = k + shift - sub
                    acc = acc + w_ref[k:k + 1, :] * shift_ref[sub, lo:lo + CONV_CHUNK, :]
        acc = acc + b_ref[...]
        mu = jnp.mean(acc, axis=-1, keepdims=True)
        d = acc - mu
        var = jnp.mean(d * d, axis=-1, keepdims=True)
        n = d * lax.rsqrt(var + EPS) * g_ref[...] + be_ref[...]
        o_ref[pl.ds(base, CONV_CHUNK), :] = (n * _sigmoid(n)).astype(BF16)

    def chunks(cj, carry):
        for u in range(CONV_UNROLL):
            chunk(cj * CONV_UNROLL + u, shift_refs.at[u])
        return carry

    lax.fori_loop(0, seq // (CONV_CHUNK * CONV_UNROLL), chunks, 0)


def _conv(z, row_block0, nseq, seq, w, b, g, be):
    return pl.pallas_call(
        functools.partial(_conv_kernel, seq),
        grid=(nseq,),
        in_specs=[pl.BlockSpec((seq, 2 * CONV_CH), lambda s: (row_block0 + s, 0)),
                  _full_spec((CONV_K, CONV_CH)), _full_spec((1, CONV_CH)),
                  _full_spec((1, CONV_CH)), _full_spec((1, CONV_CH))],
        out_specs=pl.BlockSpec((seq, CONV_CH), lambda s: (s, 0)),
        out_shape=jax.ShapeDtypeStruct((nseq * seq, CONV_CH), BF16),
        scratch_shapes=[pltpu.VMEM((seq + 2 * CONV_PAD, CONV_CH), F32),
                        pltpu.VMEM((CONV_UNROLL, SUBLANES, CONV_SPAN, CONV_CH), F32)],
        compiler_params=_cparams(("arbitrary",)),
        name="conv_seq%d" % seq,
    )(z, w, b, g, be)


def _dot_nt(a, b):
    return lax.dot_general(a, b, (((1,), (1,)), ((), ())), preferred_element_type=F32)


NA_SCALE = NA_DIM ** -0.5
assert NA_SCALE == 2.0 ** round(np.log2(NA_SCALE)), "query pre-scaling assumes a power-of-two scale"


def _ctx_attn_kernel(layer, q_ref, k_ref, v_ref, *refs):
    if layer:
        _, _, o_ref, ko_ref, vo_ref = refs
    else:
        o_ref, ko_full, vo_full = refs
        ko_ref, vo_ref = ko_full.at[0], vo_full.at[0]
        for j in range(1, DEPTH):
            ko_full[j] = jnp.zeros(ko_full.shape[1:], F32)
            vo_full[j] = jnp.zeros(vo_full.shape[1:], F32)
    pair = 2 * NA_DIM
    left = lax.broadcasted_iota(jnp.int32, (SEQ, pair), 1) < NA_DIM
    outs = []
    for p in range(NA_HEADS // 2):
        lanes = slice(p * pair, (p + 1) * pair)
        qp, kp, vp = q_ref[:, lanes], k_ref[:, lanes], v_ref[:, lanes]
        kf, vf = kp.astype(F32), vp.astype(F32)
        o_h = []
        for hh in range(2):
            cols = slice(hh * NA_DIM, (hh + 1) * NA_DIM)
            ko_ref[2 * p + hh] = kf[:, cols]
            vo_ref[2 * p + hh] = vf[:, cols]
            qm = jnp.where(left == (hh == 0), qp, jnp.zeros_like(qp))
            s = _dot_nt(qm, kp) * NA_SCALE
            m = jnp.max(s, axis=-1, keepdims=True)
            e = jnp.exp(s - m)
            den = jnp.sum(e, axis=-1, keepdims=True)
            o_h.append(jnp.dot(e.astype(BF16), vp, preferred_element_type=F32) / den)
        outs.append(jnp.where(left, o_h[0], o_h[1]))
    o_ref[...] = jnp.concatenate(outs, axis=-1).astype(BF16)


def _ctx_attn(z, layer, k_prev=None, v_prev=None):
    qb, kb, vb = COL_NA_Q // NA_WIDTH, COL_NA_K // NA_WIDTH, COL_NA_V // NA_WIDTH
    head_shape = jax.ShapeDtypeStruct((BATCH, DEPTH, NA_HEADS, SEQ, NA_DIM), F32)
    head_spec = pl.BlockSpec((None, DEPTH, NA_HEADS, SEQ, NA_DIM), lambda b: (b, 0, 0, 0, 0))
    in_specs = [pl.BlockSpec((SEQ, NA_WIDTH), lambda b: (b, qb)),
                pl.BlockSpec((SEQ, NA_WIDTH), lambda b: (b, kb)),
                pl.BlockSpec((SEQ, NA_WIDTH), lambda b: (b, vb))]
    args = [z, z, z]
    aliases = {}
    if layer:
        any_spec = pl.BlockSpec(memory_space=pl.ANY)
        in_specs += [any_spec, any_spec]
        args += [k_prev, v_prev]
        aliases = {3: 1, 4: 2}
        head_spec = pl.BlockSpec((None, None, NA_HEADS, SEQ, NA_DIM), lambda b: (b, layer, 0, 0, 0))
    return pl.pallas_call(
        functools.partial(_ctx_attn_kernel, layer),
        grid=(BATCH,),
        in_specs=in_specs,
        out_specs=[pl.BlockSpec((SEQ, NA_WIDTH), lambda b: (b, 0)), head_spec, head_spec],
        out_shape=[jax.ShapeDtypeStruct((T_CTX, NA_WIDTH), BF16), head_shape, head_shape],
        input_output_aliases=aliases,
        compiler_params=_cparams(("arbitrary",)),
        name="ctx_attn",
    )(*args)


NA_KINDS = (0, NA_ROWS, GRID_H - NA_ROWS)
N_DR = 2 * NA_KH - 1
N_DC = 2 * NA_KW - 1


def _na_row_offset(r0, i, j):
    ks = min(max(r0 - NA_KH // 2, 0), GRID_H - NA_KROWS)
    r, kr = r0 + i, ks + j
    rs = min(max(r - NA_KH // 2, 0), GRID_H - NA_KH)
    return kr - r + NA_KH - 1 if rs <= kr < rs + NA_KH else None


def _na_bias_kernel(rpb_ref, o_ref):
    lh = pl.program_id(0)
    shape = (GRID_W, 2 * GRID_W)
    qc = lax.broadcasted_iota(jnp.int32, shape, 0)
    lane = lax.broadcasted_iota(jnp.int32, shape, 1)
    kc = lane % GRID_W
    dc = jnp.clip(kc - qc, -(NA_KW - 1), NA_KW - 1) + NA_KW - 1
    cs = jnp.clip(qc - NA_KW // 2, 0, GRID_W - NA_KW)
    col_ok = (kc >= cs) & (kc < cs + NA_KW)
    neg = jnp.full(shape, NEG_INF, F32)
    tiles = []
    for dr in range(N_DR):
        base = (lh * N_DR + dr) * N_DC
        val = jnp.zeros(shape, F32)
        for d in range(N_DC):
            val = jnp.where(dc == d, rpb_ref[base + d], val)
        tiles.append(jnp.where(col_ok, val, neg))
    left = lane < GRID_W
    for kind, r0 in enumerate(NA_KINDS):
        for i in range(NA_ROWS):
            for jp in range(NA_KROWS // 2):
                dl, dr_ = _na_row_offset(r0, i, 2 * jp), _na_row_offset(r0, i, 2 * jp + 1)
                tl = neg if dl is None else tiles[dl]
                tr = neg if dr_ is None else tiles[dr_]
                o_ref[kind, i * GRID_W:(i + 1) * GRID_W, jp * 2 * GRID_W:(jp + 1) * 2 * GRID_W] = (
                    jnp.where(left, tl, tr))


def _na_bias_tables(rpb):
    return pl.pallas_call(
        _na_bias_kernel,
        grid=(DEPTH * NA_HEADS,),
        in_specs=[pl.BlockSpec(memory_space=pltpu.SMEM)],
        out_specs=pl.BlockSpec((None, len(NA_KINDS), NA_Q, NA_KEYS), lambda i: (i, 0, 0, 0)),
        out_shape=jax.ShapeDtypeStruct((DEPTH * NA_HEADS, len(NA_KINDS), NA_Q, NA_KEYS), F32),
        compiler_params=_cparams(("arbitrary",)),
        name="nbr_bias",
    )(rpb.reshape(-1))


NA_G = 4


def _na_kernel(q_ref, k_ref, v_ref, kc_ref, vc_ref, bias_ref, o_ref):
    rb = pl.program_id(2)
    ks = jnp.clip(rb * NA_ROWS - NA_KH // 2, 0, GRID_H - NA_KROWS)
    start = pl.multiple_of(ks * GRID_W, GRID_W)
    q = q_ref[...] * NA_SCALE
    kl = k_ref[pl.ds(start, NA_KEYS), :]
    vl = v_ref[pl.ds(start, NA_KEYS), :]
    pair = 2 * NA_DIM
    left = lax.broadcasted_iota(jnp.int32, (NA_Q, pair), 1) < NA_DIM
    ones_loc = jnp.ones((NA_KEYS, pair), BF16)
    ones_ctx = jnp.ones((PAST_LEN, pair), BF16)
    kc, v_ext, vc_ext = [], [], []
    for p in range(NA_G // 2):
        lanes = slice(p * pair, (p + 1) * pair)
        kc.append(jnp.concatenate([kc_ref[2 * p].astype(BF16), kc_ref[2 * p + 1].astype(BF16)], axis=-1))
        vc = jnp.concatenate([vc_ref[2 * p].astype(BF16), vc_ref[2 * p + 1].astype(BF16)], axis=-1)
        v_ext.append(jnp.concatenate([vl[:, lanes], ones_loc], axis=-1))
        vc_ext.append(jnp.concatenate([vc, ones_ctx], axis=-1))

    def scores(hh):
        p = hh // 2
        lanes = slice(p * pair, (p + 1) * pair)
        qm = jnp.where(left == (hh % 2 == 0), q[:, lanes], jnp.zeros((NA_Q, pair), BF16))
        return _dot_nt(qm, kl[:, lanes]) + bias_ref[hh], _dot_nt(qm, kc[p])

    outs = []
    nxt = scores(0)
    for hh in range(NA_G):
        s_loc, s_ctx = nxt
        if hh + 1 < NA_G:
            nxt = scores(hh + 1)
        m = jnp.maximum(jnp.max(s_loc, axis=-1, keepdims=True), jnp.max(s_ctx, axis=-1, keepdims=True))
        p_loc = jnp.exp(s_loc - m).astype(BF16)
        p_ctx = jnp.exp(s_ctx - m).astype(BF16)
        o = (jnp.dot(p_loc, v_ext[hh // 2], preferred_element_type=F32)
             + jnp.dot(p_ctx, vc_ext[hh // 2], preferred_element_type=F32))
        outs.append(o[:, :pair] / o[:, pair:])
    o_ref[...] = jnp.concatenate([jnp.where(left, outs[2 * p], outs[2 * p + 1]) for p in range(NA_G // 2)],
                                 axis=-1).astype(BF16)


def _na_attn(z, cache_k, cache_v, bias, layer):
    lat_q0 = T_CTX // NA_Q
    lat_s0 = T_CTX // DEC_SEQ
    width = NA_G * NA_DIM
    qc, kc, vc = COL_NA_Q // width, COL_NA_K // width, COL_NA_V // width
    groups = NA_HEADS // NA_G

    def kind(rb):
        return jnp.where(rb == 0, 0, jnp.where(rb == NA_RB - 1, 2, 1))

    ctx_spec = pl.BlockSpec((None, None, NA_G, PAST_LEN, NA_DIM), lambda b, hg, rb: (b, layer, hg, 0, 0))
    return pl.pallas_call(
        _na_kernel,
        grid=(DEC_BATCH, groups, NA_RB),
        in_specs=[pl.BlockSpec((NA_Q, width), lambda b, hg, rb: (lat_q0 + b * NA_RB + rb, qc + hg)),
                  pl.BlockSpec((DEC_SEQ, width), lambda b, hg, rb: (lat_s0 + b, kc + hg)),
                  pl.BlockSpec((DEC_SEQ, width), lambda b, hg, rb: (lat_s0 + b, vc + hg)),
                  ctx_spec, ctx_spec,
                  pl.BlockSpec((NA_G, None, NA_Q, NA_KEYS),
                               lambda b, hg, rb: (layer * groups + hg, kind(rb), 0, 0))],
        out_specs=pl.BlockSpec((NA_Q, width), lambda b, hg, rb: (b * NA_RB + rb, hg)),
        out_shape=jax.ShapeDtypeStruct((T_LAT, NA_WIDTH), BF16),
        compiler_params=_cparams(("arbitrary", "arbitrary", "arbitrary")),
        name="nbr_attn",
    )(z, z, z, cache_k, cache_v, bias)


RET_PAIR = 2 * RET_DIM
RET_NPAIR = RET_HEADS // 2
assert RET_PAIR == LANES and RET_CHUNK == LANES
RET_UNROLL = 8


def _rope_tables():
    n_freq = RET_DIM // 4
    t = np.arange(DEC_SEQ)
    inv = jnp.asarray(ROPE_BASE, F32) ** (-jnp.arange(n_freq, dtype=F32) / n_freq)
    ang_r = jnp.asarray(t // GRID_W, F32)[:, None] * inv[None, :]
    ang_c = jnp.asarray(t % GRID_W, F32)[:, None] * inv[None, :]
    cos = jnp.concatenate([jnp.cos(ang_r)] * 2 + [jnp.cos(ang_c)] * 2, axis=-1)
    sin = jnp.concatenate([-jnp.sin(ang_r), jnp.sin(ang_r), -jnp.sin(ang_c), jnp.sin(ang_c)], axis=-1)
    lane = np.arange(RET_WIDTH)
    src = np.where(lane % (2 * n_freq) < n_freq, lane + n_freq, lane - n_freq)
    swap = np.zeros((RET_WIDTH, RET_WIDTH), np.float32)
    swap[src, lane] = 1.0
    return jnp.tile(cos, (1, RET_HEADS)), jnp.tile(sin, (1, RET_HEADS)), jnp.asarray(swap, BF16)


def _ret_kernel(seq, latent, *refs):
    if latent:
        (lg_ref, z_ref, gn_ref, cos_ref, sin_ref, swap_ref, s0f_ref, s0b_ref, y_ref,
         q_s, k_s, kv_s, st_s) = refs
    else:
        lg_ref, z_ref, gn_ref, y_ref, sf_ref, sb_ref, q_s, k_s, kv_s, st_s = refs
    nc = seq // RET_CHUNK
    ch, hd, pw = RET_CHUNK, RET_DIM, RET_PAIR

    row = lax.broadcasted_iota(jnp.int32, (ch, ch), 0).astype(F32)
    col = lax.broadcasted_iota(jnp.int32, (ch, ch), 1).astype(F32)
    pos = lax.broadcasted_iota(jnp.int32, (ch, pw), 0).astype(F32)
    left = lax.broadcasted_iota(jnp.int32, (ch, pw), 1) < hd
    top = lax.broadcasted_iota(jnp.int32, (pw, pw), 0) < hd
    same_head = top == (lax.broadcasted_iota(jnp.int32, (pw, pw), 1) < hd)
    same_head2 = jnp.concatenate([same_head, same_head], axis=0)

    def per_head(mask, fn, p):
        return jnp.where(mask, fn(2 * p), fn(2 * p + 1))

    decay = []
    for h in range(RET_HEADS):
        lf, lb = lg_ref[0, h], lg_ref[1, h]
        d_f = jnp.where(row >= col, jnp.exp(jnp.maximum(row - col, 0.0) * lf), 0.0)
        d_b = jnp.where(col >= row, jnp.exp(jnp.maximum(col - row, 0.0) * lb), 0.0)
        decay.append(d_f + d_b)
    q_dec, k_dec, c_dec_f, c_dec_b = [], [], [], []
    for p in range(RET_NPAIR):
        q_dec.append(jnp.concatenate(
            [per_head(left, lambda h: jnp.exp((pos + 1.0) * lg_ref[0, h]), p),
             per_head(left, lambda h: jnp.exp((ch - pos) * lg_ref[1, h]), p)], axis=-1))
        k_dec.append(jnp.concatenate(
            [per_head(left, lambda h: jnp.exp((ch - 1.0 - pos) * lg_ref[0, h]), p),
             per_head(left, lambda h: jnp.exp(pos * lg_ref[1, h]), p)], axis=-1))
        zero = jnp.zeros((pw, pw), F32)
        c_dec_f.append(per_head(top, lambda h: jnp.exp(zero + ch * lg_ref[0, h]), p))
        c_dec_b.append(per_head(top, lambda h: jnp.exp(zero + ch * lg_ref[1, h]), p))

    def rope(x, base):
        xf = x.astype(F32)
        if not latent:
            return xf
        swapped = jnp.dot(x, swap_ref[...], preferred_element_type=F32)
        return xf * cos_ref[pl.ds(base, ch), :] + swapped * sin_ref[pl.ds(base, ch), :]

    def pass1(n, carry):
        base = pl.multiple_of(n * ch, ch)
        zc = z_ref[pl.ds(base, ch), :]
        q = rope(zc[:, 0:RET_WIDTH], base)
        k = rope(zc[:, RET_WIDTH:2 * RET_WIDTH], base) * (RET_DIM ** -0.5)
        q_s[pl.ds(base, ch), :] = q.astype(BF16)
        k_s[pl.ds(base, ch), :] = k.astype(BF16)
        v = zc[:, 2 * RET_WIDTH:3 * RET_WIDTH]
        for p in range(RET_NPAIR):
            lanes = slice(p * pw, (p + 1) * pw)
            kp = k[:, lanes]
            k2 = (jnp.concatenate([kp, kp], axis=-1) * k_dec[p]).astype(BF16)
            kv = lax.dot_general(k2, v[:, lanes], (((0,), (0,)), ((), ())), preferred_element_type=F32)
            kv_s[n, p] = jnp.where(same_head2, kv, 0.0)
        return carry

    lax.fori_loop(0, nc, pass1, 0, unroll=min(RET_UNROLL, nc))

    def block_diag(a, b):
        z = jnp.zeros((hd, hd), F32)
        return jnp.concatenate([jnp.concatenate([a, z], axis=1), jnp.concatenate([z, b], axis=1)], axis=0)

    for p in range(RET_NPAIR):
        if latent:
            s_f = block_diag(s0f_ref[2 * p], s0f_ref[2 * p + 1])
            s_b = block_diag(s0b_ref[2 * p], s0b_ref[2 * p + 1])
        else:
            s_f = s_b = jnp.zeros((pw, pw), F32)

        def fwd(n, s, p=p):
            st_s[n, p, 0:pw, :] = s.astype(BF16)
            return c_dec_f[p] * s + kv_s[n, p, 0:pw, :]

        def bwd(i, s, p=p):
            n = nc - 1 - i
            st_s[n, p, pw:2 * pw, :] = s.astype(BF16)
            return c_dec_b[p] * s + kv_s[n, p, pw:2 * pw, :]

        s_f = lax.fori_loop(0, nc, fwd, s_f)
        s_b = lax.fori_loop(0, nc, bwd, s_b)
        if not latent:
            for hh in range(2):
                blk = slice(hh * hd, (hh + 1) * hd)
                sf_ref[2 * p + hh] = s_f[blk, blk]
                sb_ref[2 * p + hh] = s_b[blk, blk]

    def pass3(n, carry):
        base = pl.multiple_of(n * ch, ch)
        zc = z_ref[pl.ds(base, ch), :]
        q = q_s[pl.ds(base, ch), :]
        k = k_s[pl.ds(base, ch), :]
        v = zc[:, 2 * RET_WIDTH:3 * RET_WIDTH]
        gate = zc[:, 3 * RET_WIDTH:4 * RET_WIDTH].astype(F32)
        outs = []
        for p in range(RET_NPAIR):
            lanes = slice(p * pw, (p + 1) * pw)
            qp, kp, vp = q[:, lanes], k[:, lanes], v[:, lanes]
            o_h = []
            for hh in range(2):
                qm = jnp.where(left == (hh == 0), qp, jnp.zeros_like(qp))
                s = _dot_nt(qm, kp) * decay[2 * p + hh]
                o_h.append(jnp.dot(s.astype(BF16), vp, preferred_element_type=F32))
            qf = qp.astype(F32)
            q2 = (jnp.concatenate([qf, qf], axis=-1) * q_dec[p]).astype(BF16)
            o = jnp.where(left, o_h[0], o_h[1]) + jnp.dot(q2, st_s[n, p], preferred_element_type=F32)

            def half_mean(t):
                s_l = jnp.sum(jnp.where(left, t, 0.0), axis=-1, keepdims=True)
                s_r = jnp.sum(jnp.where(left, 0.0, t), axis=-1, keepdims=True)
                return jnp.where(left, s_l, s_r) * (1.0 / hd)

            d = o - half_mean(o)
            outs.append(d * lax.rsqrt(half_mean(d * d) + EPS))
        nrm = jnp.concatenate(outs, axis=-1)
        y_ref[pl.ds(base, ch), :] = (nrm * gn_ref[...] * (gate * _sigmoid(gate))).astype(BF16)
        return carry

    lax.fori_loop(0, nc, pass3, 0, unroll=min(RET_UNROLL, nc))


def _retention(z, lg, gn_g, latent, layer=None, rope=None, s0_f=None, s0_b=None):
    seq = DEC_SEQ if latent else SEQ
    nseq = DEC_BATCH if latent else BATCH
    nc = seq // RET_CHUNK
    row0 = (T_CTX // DEC_SEQ) if latent else 0
    cb = COL_RET // (4 * RET_WIDTH)
    in_specs = [pl.BlockSpec(memory_space=pltpu.SMEM),
                pl.BlockSpec((seq, 4 * RET_WIDTH), lambda s: (row0 + s, cb)),
                _full_spec((1, RET_WIDTH))]
    args = [lg, z, gn_g]
    state_shape = jax.ShapeDtypeStruct((nseq, RET_HEADS, RET_DIM, RET_DIM), F32)
    y_spec = pl.BlockSpec((seq, RET_WIDTH), lambda s: (s, 0))
    y_shape = jax.ShapeDtypeStruct((nseq * seq, RET_WIDTH), BF16)
    if latent:
        st_spec = pl.BlockSpec((None, None, RET_HEADS, RET_DIM, RET_DIM), lambda s: (s, layer, 0, 0, 0))

        def const_spec(shape):
            return pl.BlockSpec(shape, lambda s: (0,) * len(shape), pipeline_mode=pl.Buffered(1))

        in_specs += [const_spec((seq, RET_WIDTH)), const_spec((seq, RET_WIDTH)),
                     const_spec((RET_WIDTH, RET_WIDTH)), st_spec, st_spec]
        args += [rope[0], rope[1], rope[2], s0_f, s0_b]
        out_specs, out_shape = y_spec, y_shape
    else:
        so_spec = pl.BlockSpec((None, RET_HEADS, RET_DIM, RET_DIM), lambda s: (s, 0, 0, 0))
        out_specs, out_shape = [y_spec, so_spec, so_spec], [y_shape, state_shape, state_shape]
    return pl.pallas_call(
        functools.partial(_ret_kernel, seq, latent),
        grid=(nseq,),
        in_specs=in_specs,
        out_specs=out_specs,
        out_shape=out_shape,
        scratch_shapes=[pltpu.VMEM((seq, RET_WIDTH), BF16), pltpu.VMEM((seq, RET_WIDTH), BF16),
                        pltpu.VMEM((nc, RET_NPAIR, 2 * RET_PAIR, RET_PAIR), F32),
                        pltpu.VMEM((nc, RET_NPAIR, 2 * RET_PAIR, RET_PAIR), BF16)],
        compiler_params=_cparams(("arbitrary",)),
        name="retention_lat" if latent else "retention_ctx",
    )(*args)


def _route(logits):
    lane = lax.broadcasted_iota(jnp.int32, logits.shape, 1)
    lane_f = lane.astype(F32)
    big = float(ROUTE_COLS)
    neg = -jnp.inf
    is_grp = lane < N_GROUPS
    gl = jnp.where(is_grp, logits, neg)
    gmax = jnp.max(gl, axis=-1, keepdims=True)
    grp = jnp.min(jnp.where(gl == gmax, lane_f, big), axis=-1, keepdims=True)
    p_grp = 1.0 / jnp.sum(jnp.exp(gl - gmax), axis=-1, keepdims=True)
    e_f = lane_f - N_GROUPS
    lo = grp * EXPERTS_PER_GROUP
    in_grp = (e_f >= lo) & (e_f < lo + EXPERTS_PER_GROUP)
    el = jnp.where(in_grp, logits, neg)
    m1 = jnp.max(el, axis=-1, keepdims=True)
    i1 = jnp.min(jnp.where(el == m1, lane_f, big), axis=-1, keepdims=True)
    el2 = jnp.where(lane_f == i1, neg, el)
    m2 = jnp.max(el2, axis=-1, keepdims=True)
    i2 = jnp.min(jnp.where(el2 == m2, lane_f, big), axis=-1, keepdims=True)
    t = jnp.exp(m2 - m1)
    g1 = p_grp / (1.0 + t)
    g2 = p_grp * t / (1.0 + t)
    rows = logits.shape[0]
    oh1, oh2 = lane_f == i1, lane_f == i2
    oh = jnp.where(oh1 | oh2, 1.0, 0.0)
    tri = (lax.broadcasted_iota(jnp.int32, (rows, rows), 0)
           > lax.broadcasted_iota(jnp.int32, (rows, rows), 1))
    rank = jnp.dot(jnp.where(tri, 1.0, 0.0).astype(BF16), oh.astype(BF16), preferred_element_type=F32)
    tiles = jnp.floor((jnp.sum(oh, axis=0, keepdims=True) + (SUBLANES - 1)) * (1.0 / SUBLANES))
    upper = (lax.broadcasted_iota(jnp.int32, (ROUTE_COLS, ROUTE_COLS), 0)
             < lax.broadcasted_iota(jnp.int32, (ROUTE_COLS, ROUTE_COLS), 1))
    start = SUBLANES * jnp.dot(jnp.broadcast_to(tiles, (SUBLANES, ROUTE_COLS)).astype(BF16),
                               jnp.where(upper, 1.0, 0.0).astype(BF16),
                               preferred_element_type=F32)[0:1, :]
    pos = start + rank
    p1 = jnp.sum(jnp.where(oh1, pos, 0.0), axis=-1, keepdims=True)
    p2 = jnp.sum(jnp.where(oh2, pos, 0.0), axis=-1, keepdims=True)
    out = jnp.zeros(logits.shape, F32)
    for k, val in enumerate((i1 - N_GROUPS, i2 - N_GROUPS, g1, g2, p1, p2)):
        out = jnp.where(lane == k, val, out)
    return out, SUBLANES * tiles


def _outproj_kernel(ycc, ycl, ync, ynl, yrc, yrl, x_ref, mod_ref, g_ref, w_ref, wr_ref, br_ref,
                    xo_ref, xs_ref, r_ref, seg_ref, wb_ref):
    _cast_weights_once(w_ref, wb_ref)
    is_ctx = pl.program_id(0) < NB_CTX
    yc = jnp.where(is_ctx, ycc[...], ycl[...])
    yn = jnp.where(is_ctx, ync[...], ynl[...])
    yr = jnp.where(is_ctx, yrc[...], yrl[...])
    y = (jnp.dot(yc, wb_ref[0:CONV_CH, :], preferred_element_type=F32)
         + jnp.dot(yn, wb_ref[CONV_CH:CONV_CH + NA_WIDTH, :], preferred_element_type=F32)
         + jnp.dot(yr, wb_ref[CONV_CH + NA_WIDTH:, :], preferred_element_type=F32))
    x = x_ref[...] + mod_ref[2:3, :] * y
    xo_ref[...] = x
    h = _norm_mod(x, g_ref[...], mod_ref[3:4, :], mod_ref[4:5, :])
    h_hi = h.astype(BF16)
    h_lo = (h - h_hi.astype(F32)).astype(BF16)
    hw = jnp.dot(h_hi, wr_ref[...], preferred_element_type=F32)
    logits = (hw[:, :ROUTE_COLS] + hw[:, ROUTE_COLS:]
              + jnp.dot(h_lo, wr_ref[:, :ROUTE_COLS], preferred_element_type=F32) + br_ref[...])
    route, seg = _route(logits)
    r_ref[...] = route
    seg_ref[...] = jnp.broadcast_to(seg, seg_ref.shape)
    sel = _slot_onehot(route, 0) | _slot_onehot(route, 1)
    xs_ref[...] = _pack_bf16_pairs(lax.dot_general(jnp.where(sel, 1.0, 0.0).astype(BF16), h_hi,
                                                   (((0,), (0,)), ((), ())), preferred_element_type=F32))


def _outproj(y_conv, y_na, y_ret, x, mod, g, w_out, layer, w_route, b_route):
    return pl.pallas_call(
        _outproj_kernel,
        grid=(NB_ALL,),
        in_specs=(_ctx_lat_specs(CONV_CH) + _ctx_lat_specs(NA_WIDTH) + _ctx_lat_specs(RET_WIDTH)
                  + [_tok_spec(D_MODEL), _mod_spec(), _full_spec((1, D_MODEL)),
                     _layer_weight_spec(layer, D_MODEL, D_MODEL), _full_spec((D_MODEL, 2 * ROUTE_COLS)),
                     _full_spec((1, ROUTE_COLS))]),
        out_specs=[_tok_spec(D_MODEL), pl.BlockSpec((MOE_LC, D_HALF), lambda i: (i, 0)),
                   _tok_spec(ROUTE_COLS), pl.BlockSpec((None, SUBLANES, ROUTE_COLS), lambda i: (i, 0, 0))],
        out_shape=[jax.ShapeDtypeStruct((T_ALL, D_MODEL), F32),
                   jax.ShapeDtypeStruct((NB_ALL * MOE_LC, D_HALF), U32),
                   jax.ShapeDtypeStruct((T_ALL, ROUTE_COLS), F32),
                   jax.ShapeDtypeStruct((NB_ALL, SUBLANES, ROUTE_COLS), F32)],
        scratch_shapes=[pltpu.VMEM((D_MODEL, D_MODEL), BF16)],
        compiler_params=_cparams(("arbitrary",)),
        name="outproj_route",
    )(y_conv[0], y_conv[1], y_na[0], y_na[1], y_ret[0], y_ret[1], x, mod, g, w_out, w_route, b_route)


def _dispatch_tables(seg):
    seg_len = seg[:, 0, N_GROUPS:N_GROUPS + N_EXPERTS].astype(jnp.int32)
    experts = jnp.arange(N_EXPERTS, dtype=jnp.int32)
    in_chunk = jnp.cumsum(seg_len, axis=1) - seg_len
    seg_row = in_chunk + MOE_LC * jnp.arange(N_CHUNK, dtype=jnp.int32)[:, None]
    seg_off = jnp.cumsum(seg_len, axis=0) - seg_len
    rows_e = jnp.sum(seg_len, axis=0)
    chunk_rows = jnp.sum(seg_len, axis=1)
    nblk = (rows_e + MOE_BLK - 1) // MOE_BLK
    blk_end = jnp.cumsum(nblk)
    blk_start = blk_end - nblk
    blk = jnp.arange(MOE_NBLK, dtype=jnp.int32)
    n_active = blk_end[-1]
    blk_e = jnp.minimum(jnp.sum((blk_end[None, :] <= jnp.minimum(blk, n_active - 1)[:, None]).astype(jnp.int32),
                                axis=-1), N_EXPERTS - 1)
    mine = blk_e[:, None] == experts[None, :]
    blk_lo = (blk - jnp.sum(jnp.where(mine, blk_start[None, :], 0), axis=-1)) * MOE_BLK
    left = jnp.sum(jnp.where(mine, rows_e[None, :], 0), axis=-1) - blk_lo
    blk_nv = jnp.where(blk < n_active, jnp.clip(left, 0, MOE_BLK), 0).astype(jnp.int32)
    off_b = jnp.sum(jnp.where(mine[:, None, :], seg_off[None, :, :], 0), axis=-1)
    end_b = off_b + jnp.sum(jnp.where(mine[:, None, :], seg_len[None, :, :], 0), axis=-1)
    blk_c0 = jnp.sum((end_b <= blk_lo[:, None]).astype(jnp.int32), axis=-1)
    blk_c1 = jnp.sum((off_b < (blk_lo + blk_nv)[:, None]).astype(jnp.int32), axis=-1)
    after = jnp.sum(jnp.where(mine, blk_end[None, :], 0), axis=-1)
    blk_next_e = jnp.where(after < n_active, jnp.take(blk_e, jnp.minimum(after, MOE_NBLK - 1)), -1)
    row_b = jnp.sum(jnp.where(mine[:, None, :], seg_row[None, :, :], 0), axis=-1)
    first = jnp.maximum(off_b, blk_lo[:, None])
    piece_n = jnp.minimum(end_b, (blk_lo + blk_nv)[:, None]) - first
    piece_src = row_b + first - off_b
    piece_dst = first - blk_lo[:, None]
    return (blk_e, blk_next_e.astype(jnp.int32), blk_nv, blk_c0, blk_c1, piece_src.reshape(-1),
            piece_dst.reshape(-1), piece_n.reshape(-1), chunk_rows)


def _moe_kernel(layer, blk_e, blk_next_e, blk_nv, blk_c0, blk_c1, piece_src, piece_dst, piece_n,
                chunk_rows, xs_hbm, w1_hbm, w3_hbm, w2_hbm, ys_hbm, xbuf, obuf, zeros,
                w1f, w3f, w2f, w1b, w3b, w2b, gsem, ssem, zsem, wsem):
    i = pl.program_id(0)
    last = pl.num_programs(0) - 1
    slot = i % 2

    def tiles(v):
        return pl.multiple_of(v, SUBLANES)

    def for_segments(blk, fn):
        def body(c, carry):
            k = blk * N_CHUNK + c
            n = piece_n[k]

            @pl.when(n > 0)
            def _():
                fn(tiles(piece_src[k]), tiles(piece_dst[k]), tiles(n))

            return carry

        lax.fori_loop(blk_c0[blk], blk_c1[blk], body, 0)

    def weight_copies(e):
        return [pltpu.make_async_copy(src.at[layer, e], dst, wsem)
                for src, dst in ((w1_hbm, w1f), (w3_hbm, w3f), (w2_hbm, w2f))]

    def start_gathers(blk, s):
        for_segments(blk, lambda src, dst, n: pltpu.make_async_copy(
            xs_hbm.at[pl.ds(src, n)], xbuf.at[s, pl.ds(dst, n)], gsem.at[s]).start())

    def start_scatters(blk, s):
        for_segments(blk, lambda dst, src, n: pltpu.make_async_copy(
            obuf.at[s, pl.ds(src, n)], ys_hbm.at[pl.ds(dst, n)], ssem.at[s]).start())

    def wait_rows(blk, s, sem):
        n = tiles(blk_nv[blk])

        @pl.when(n > 0)
        def _():
            pltpu.make_async_copy(xs_hbm.at[pl.ds(0, n)], xbuf.at[s, pl.ds(0, n)], sem.at[s]).wait()

    @pl.when(i == 0)
    def _():
        xbuf[...] = jnp.zeros_like(xbuf)
        zeros[...] = jnp.zeros_like(zeros)

        def tail(c):
            n = tiles(MOE_LC - chunk_rows[c])
            return n, pltpu.make_async_copy(zeros.at[pl.ds(0, n)],
                                            ys_hbm.at[pl.ds(tiles(c * MOE_LC + chunk_rows[c]), n)], zsem)

        def fill(c, carry):
            n, copy = tail(c)
            pl.when(n > 0)(copy.start)
            return carry

        def drain(c, carry):
            n, copy = tail(c)
            pl.when(n > 0)(copy.wait)
            return carry

        lax.fori_loop(0, N_CHUNK, fill, 0)
        lax.fori_loop(0, N_CHUNK, drain, 0)
        start_gathers(0, 0)
        for copy in weight_copies(blk_e[0]):
            copy.start()

    @pl.when(i < last)
    def _():
        start_gathers(i + 1, 1 - slot)

    @pl.when(i >= 2)
    def _():
        wait_rows(i - 2, slot, ssem)

    @pl.when(blk_nv[i] > 0)
    def _():
        @pl.when((i == 0) | (blk_e[i] != blk_e[jnp.maximum(i - 1, 0)]))
        def _():
            for copy in weight_copies(blk_e[i]):
                copy.wait()
            w1b[...] = w1f[...].astype(BF16)
            w3b[...] = w3f[...].astype(BF16)
            w2b[...] = w2f[...].astype(BF16)

            @pl.when(blk_next_e[i] >= 0)
            def _():
                for copy in weight_copies(blk_next_e[i]):
                    copy.start()

        wait_rows(i, slot, gsem)

        def expert_mlp(rows):
            x_lo, x_hi = _unpack_bf16_pairs(xbuf[slot, 0:rows, :])
            n_hid = D_EXPERT // MXU_TILE

            def in_dot(w, t):
                cols = slice(t * MXU_TILE, (t + 1) * MXU_TILE)
                return (jnp.dot(x_lo, w[:D_HALF, cols], preferred_element_type=F32)
                        + jnp.dot(x_hi, w[D_HALF:, cols], preferred_element_type=F32))

            ab = [(in_dot(w1b, t), in_dot(w3b, t)) for t in range(n_hid)]
            mid = [(a * _sigmoid(a) * b).astype(BF16) for a, b in ab]

            def out_dot(t):
                cols = slice(t * MXU_TILE, (t + 1) * MXU_TILE)
                return sum(jnp.dot(mid[j], w2b[j * MXU_TILE:(j + 1) * MXU_TILE, cols],
                                   preferred_element_type=F32) for j in range(n_hid))

            n_word = D_HALF // MXU_TILE
            for t in range(n_word):
                obuf[slot, 0:rows, t * MXU_TILE:(t + 1) * MXU_TILE] = _pack_words(
                    out_dot(t), out_dot(t + n_word))

        for rows in range(MOE_ROW_STEP, MOE_BLK + 1, MOE_ROW_STEP):
            @pl.when((blk_nv[i] > rows - MOE_ROW_STEP) & (blk_nv[i] <= rows))
            def _(rows=rows):
                expert_mlp(rows)

        start_scatters(i, slot)

    @pl.when(i == last)
    def _():
        wait_rows(i - 1, 1 - slot, ssem)
        wait_rows(i, slot, ssem)


def _moe(xs, w1, w3, w2, layer, blk_e, blk_next_e, blk_nv, blk_c0, blk_c1, piece_src, piece_dst,
         piece_n, chunk_rows):
    any_spec = pl.BlockSpec(memory_space=pl.ANY)
    grid_spec = pltpu.PrefetchScalarGridSpec(
        num_scalar_prefetch=9,
        grid=(MOE_NBLK,),
        in_specs=[any_spec, any_spec, any_spec, any_spec],
        out_specs=any_spec,
        scratch_shapes=[pltpu.VMEM((2, MOE_BLK, D_HALF), U32), pltpu.VMEM((2, MOE_BLK, D_HALF), U32),
                        pltpu.VMEM((MOE_LC - 2 * TM, D_HALF), U32),
                        pltpu.VMEM((D_MODEL, D_EXPERT), F32), pltpu.VMEM((D_MODEL, D_EXPERT), F32),
                        pltpu.VMEM((D_EXPERT, D_MODEL), F32),
                        pltpu.VMEM((D_MODEL, D_EXPERT), BF16), pltpu.VMEM((D_MODEL, D_EXPERT), BF16),
                        pltpu.VMEM((D_EXPERT, D_MODEL), BF16),
                        pltpu.SemaphoreType.DMA((2,)), pltpu.SemaphoreType.DMA((2,)),
                        pltpu.SemaphoreType.DMA, pltpu.SemaphoreType.DMA])
    return pl.pallas_call(
        functools.partial(_moe_kernel, layer),
        grid_spec=grid_spec,
        out_shape=jax.ShapeDtypeStruct((NB_ALL * MOE_LC, D_HALF), U32),
        compiler_params=_cparams(("arbitrary",)),
        name="moe_experts",
    )(blk_e, blk_next_e, blk_nv, blk_c0, blk_c1, piece_src, piece_dst, piece_n, chunk_rows,
      xs, w1, w3, w2)


def _final_kernel(x_ref, ys_ref, r_ref, mod_ref, g_ref, o_ref):
    x = _moe_residual(x_ref, ys_ref, r_ref, mod_ref)
    ms = jnp.mean(x * x, axis=-1, keepdims=True)
    o_ref[...] = x * lax.rsqrt(ms + EPS) * g_ref[...]


def _final(x, ys, route, mod, g, block0, nblocks):
    return pl.pallas_call(
        _final_kernel,
        grid=(nblocks,),
        in_specs=[pl.BlockSpec((TM, D_MODEL), lambda i: (block0 + i, 0)),
                  pl.BlockSpec((MOE_LC, D_HALF), lambda i: (block0 + i, 0)),
                  pl.BlockSpec((TM, ROUTE_COLS), lambda i: (block0 + i, 0)),
                  pl.BlockSpec((None, 6, D_MODEL), lambda i: (_cond_row(block0 + i), 0, 0)),
                  _full_spec((1, D_MODEL))],
        out_specs=_tok_spec(D_MODEL),
        out_shape=jax.ShapeDtypeStruct((nblocks * TM, D_MODEL), F32),
        compiler_params=_cparams(("arbitrary",)),
        name="final_norm",
    )(x, ys, route, mod, g)


def kernel(x_prompt, x_sample, c, cache_k, cache_v, state_ret_f, state_ret_b, c_ctx, w_ada, b_ada, norm1_g, norm2_g, w_in, w_out, conv_w, conv_b, conv_ln_g, conv_ln_b, na_rpb, ret_lg_f, ret_lg_b, ret_gn_g, w_route_g, b_route_g, w_route_e, b_route_e, w1, w3, w2, final_g):
    cv = jnp.zeros((COND_ROWS, D_MODEL), F32).at[0].set(c_ctx).at[1:N_COND].set(c)
    mods = _ada(cv, w_ada, b_ada).reshape(DEPTH, COND_ROWS, 6, D_MODEL)
    pad = ROUTE_COLS - N_GROUPS - N_EXPERTS
    w_route = jnp.pad(jnp.concatenate([w_route_g, w_route_e], axis=-1), ((0, 0), (0, 0), (0, pad)))
    b_route = jnp.pad(jnp.concatenate([b_route_g, b_route_e], axis=-1), ((0, 0), (0, pad)))
    w_route_hi = w_route.astype(BF16)
    w_route_lo = (w_route - w_route_hi.astype(F32)).astype(BF16)
    w_route = jnp.concatenate([w_route_hi, w_route_lo], axis=-1)
    na_bias = _na_bias_tables(na_rpb)
    rope = _rope_tables()
    lg = jnp.stack([ret_lg_f, ret_lg_b], axis=1)

    x_ctx = x_prompt.reshape(T_CTX, D_MODEL)
    x_lat = x_sample.reshape(T_LAT, D_MODEL)
    x = y = route = new_k = new_v = None
    sf_list, sb_list = [], []
    for l in range(DEPTH):
        g1 = norm1_g[l].reshape(1, D_MODEL)
        if l == 0:
            z, x = _inproj_first(x_ctx, x_lat, mods[l], g1, w_in, l)
        else:
            z, x = _inproj_next(x, y, route, mods[l - 1], mods[l], g1, w_in, l)
        conv_args = (conv_w[l], conv_b[l].reshape(1, -1), conv_ln_g[l].reshape(1, -1),
                     conv_ln_b[l].reshape(1, -1))
        yc_c = _conv(z, 0, BATCH, SEQ, *conv_args)
        yc_l = _conv(z, T_CTX // DEC_SEQ, DEC_BATCH, DEC_SEQ, *conv_args)
        yn_c, new_k, new_v = _ctx_attn(z, l, new_k, new_v)
        yn_l = _na_attn(z, cache_k, cache_v, na_bias, l)
        gn = ret_gn_g[l].reshape(1, RET_WIDTH)
        yr_c, sf_l, sb_l = _retention(z, lg[l], gn, latent=False)
        yr_l = _retention(z, lg[l], gn, latent=True, layer=l, rope=rope,
                          s0_f=state_ret_f, s0_b=state_ret_b)
        x, xs, route, seg = _outproj((yc_c, yc_l), (yn_c, yn_l), (yr_c, yr_l), x, mods[l],
                                     norm2_g[l].reshape(1, D_MODEL), w_out, l, w_route[l],
                                     b_route[l].reshape(1, ROUTE_COLS))
        y = _moe(xs, w1, w3, w2, l, *_dispatch_tables(seg))
        sf_list.append(sf_l)
        sb_list.append(sb_l)
    fg = final_g.reshape(1, D_MODEL)
    y_prompt = _final(x, y, route, mods[DEPTH - 1], fg, 0, NB_CTX).reshape(BATCH, SEQ, D_MODEL)
    y_sample = _final(x, y, route, mods[DEPTH - 1], fg, NB_CTX, NB_LAT).reshape(DEC_BATCH, DEC_SEQ, D_MODEL)
    return (y_prompt, y_sample, new_k, new_v, jnp.stack(sf_list, axis=1), jnp.stack(sb_list, axis=1))
```

```python
import functools

import numpy as np
import jax
import jax.numpy as jnp
from jax import lax
from jax.experimental import pallas as pl
from jax.experimental.pallas import tpu as pltpu

D_MODEL = 1024
BATCH = 32
SEQ = 256
DEPTH = 2
DEC_BATCH = 4
DEC_SEQ = 4096
PAST_LEN = 512
GRID_W = 64
GRID_H = DEC_SEQ // GRID_W
CONV_CH = 256
CONV_K = 31
NA_HEADS = 8
NA_DIM = 64
NA_WIDTH = NA_HEADS * NA_DIM
NA_KH = 8
NA_KW = 16
RET_HEADS = 4
RET_DIM = 64
RET_WIDTH = RET_HEADS * RET_DIM
RET_CHUNK = 128
ROPE_BASE = 10000.0
N_GROUPS = 4
EXPERTS_PER_GROUP = 8
N_EXPERTS = N_GROUPS * EXPERTS_PER_GROUP
D_EXPERT = 512
IN_COLS = 2 * CONV_CH + 3 * NA_WIDTH + 4 * RET_WIDTH
EPS = 1e-6
NEG_INF = -1e30

F32 = jnp.float32
BF16 = jnp.bfloat16
HIGHEST = lax.Precision.HIGHEST

T_CTX = BATCH * SEQ
T_LAT = DEC_BATCH * DEC_SEQ
T_ALL = T_CTX + T_LAT
N_COND = 1 + DEC_BATCH
COND_ROWS = 8

TM = 512
NB_CTX = T_CTX // TM
NB_LAT = T_LAT // TM
NB_ALL = NB_CTX + NB_LAT
LAT_BLOCKS_PER_REQ = DEC_SEQ // TM

LANES = 128
SUBLANES = 8
MXU_TILE = 256
ROUTE_COLS = LANES

COL_CONV = 0
COL_NA_Q = 2 * CONV_CH
COL_NA_K = COL_NA_Q + NA_WIDTH
COL_NA_V = COL_NA_K + NA_WIDTH
COL_RET = COL_NA_V + NA_WIDTH

NA_ROWS = 8
NA_Q = NA_ROWS * GRID_W
NA_KROWS = NA_ROWS + NA_KH
NA_KEYS = NA_KROWS * GRID_W
NA_RB = GRID_H // NA_ROWS

MOE_BLK = 2048
MOE_ROW_STEP = 128
MOE_LC = -(-(2 * TM + N_EXPERTS * (SUBLANES - 1)) // LANES) * LANES
N_CHUNK = NB_ALL
MOE_NBLK = -(-(N_CHUNK * MOE_LC) // MOE_BLK) + N_EXPERTS

VMEM_LIMIT = 56 * 1024 * 1024


def _cparams(sem):
    return pltpu.CompilerParams(dimension_semantics=sem, vmem_limit_bytes=VMEM_LIMIT)


def _sigmoid(x):
    return 1.0 / (1.0 + jnp.exp(-x))


def _cond_row(i):
    return jnp.where(i < NB_CTX, 0, 1 + (i - NB_CTX) // LAT_BLOCKS_PER_REQ)


ADA_TN = 1536


def _ada_kernel(cv_ref, w_ref, b_ref, o_ref):
    cv = cv_ref[...]
    s = cv * _sigmoid(cv)
    o_ref[...] = jnp.dot(s, w_ref[...], precision=HIGHEST, preferred_element_type=F32) + b_ref[...]


def _ada(cv, w_ada, b_ada):
    n = 6 * D_MODEL
    return pl.pallas_call(
        _ada_kernel,
        grid=(DEPTH, n // ADA_TN),
        in_specs=[
            pl.BlockSpec((COND_ROWS, D_MODEL), lambda l, j: (0, 0)),
            pl.BlockSpec((None, D_MODEL, ADA_TN), lambda l, j: (l, 0, j)),
            pl.BlockSpec((None, 1, ADA_TN), lambda l, j: (l, 0, j)),
        ],
        out_specs=pl.BlockSpec((None, COND_ROWS, ADA_TN), lambda l, j: (l, 0, j)),
        out_shape=jax.ShapeDtypeStruct((DEPTH, COND_ROWS, n), F32),
        compiler_params=_cparams(("arbitrary", "arbitrary")),
        name="ada_mod",
    )(cv, w_ada, b_ada.reshape(DEPTH, 1, n))


IN_TN = 768


def _norm_mod(x, g, shift, scale):
    ms = jnp.mean(x * x, axis=-1, keepdims=True)
    return (x * lax.rsqrt(ms + EPS) * g) * (1.0 + scale) + shift


def _cast_weights_once(w_ref, wb_ref):
    @pl.when(pl.program_id(0) == 0)
    def _():
        wb_ref[...] = w_ref[...].astype(BF16)


def _layer_weight_spec(layer, rows, cols):
    return pl.BlockSpec((None, rows, cols), lambda i: (layer, 0, 0), pipeline_mode=pl.Buffered(1))


def _inproj_body(x, mod_ref, g_ref, w_ref, wb_ref, z_ref):
    _cast_weights_once(w_ref, wb_ref)
    h = _norm_mod(x, g_ref[...], mod_ref[0:1, :], mod_ref[1:2, :]).astype(BF16)
    for c in range(IN_COLS // IN_TN):
        cols = slice(c * IN_TN, (c + 1) * IN_TN)
        z_ref[:, cols] = jnp.dot(h, wb_ref[:, cols], preferred_element_type=F32).astype(BF16)


def _inproj_first_kernel(xc_ref, xl_ref, mod_ref, g_ref, w_ref, z_ref, xo_ref, wb_ref):
    i = pl.program_id(0)
    x = jnp.where(i < NB_CTX, xc_ref[...], xl_ref[...])
    xo_ref[...] = x
    _inproj_body(x, mod_ref, g_ref, w_ref, wb_ref, z_ref)


U32 = jnp.uint32
D_HALF = D_MODEL // 2
_HI_MASK = np.uint32(0xFFFF0000)


def _pack_words(lo, hi):
    lo = lax.bitcast_convert_type(lo.astype(BF16).astype(F32), U32) >> 16
    hi = lax.bitcast_convert_type(hi.astype(BF16).astype(F32), U32) & _HI_MASK
    return lo | hi


def _pack_bf16_pairs(x):
    return _pack_words(x[:, :D_HALF], x[:, D_HALF:])


def _unpack_bf16_pairs(w):
    lo = lax.bitcast_convert_type(w << 16, F32).astype(BF16)
    hi = lax.bitcast_convert_type(w & _HI_MASK, F32).astype(BF16)
    return lo, hi


def _slot_onehot(route, slot):
    pos = route[:, 4 + slot:5 + slot].astype(jnp.int32)
    return lax.broadcasted_iota(jnp.int32, (route.shape[0], MOE_LC), 1) == pos


def _moe_residual(x_ref, ys_ref, r_ref, mod_ref):
    r = r_ref[...]
    sel = jnp.where(_slot_onehot(r, 0), r[:, 2:3], jnp.where(_slot_onehot(r, 1), r[:, 3:4], 0.0))
    sel = sel.astype(BF16)
    y = jnp.concatenate([jnp.dot(sel, half, preferred_element_type=F32)
                         for half in _unpack_bf16_pairs(ys_ref[...])], axis=-1)
    return x_ref[...] + mod_ref[5:6, :] * y


def _inproj_next_kernel(x_ref, ys_ref, r_ref, modp_ref, mod_ref, g_ref, w_ref, z_ref, xo_ref, wb_ref):
    x = _moe_residual(x_ref, ys_ref, r_ref, modp_ref)
    xo_ref[...] = x
    _inproj_body(x, mod_ref, g_ref, w_ref, wb_ref, z_ref)


def _tok_spec(cols):
    return pl.BlockSpec((TM, cols), lambda i: (i, 0))


def _mod_spec():
    return pl.BlockSpec((None, 6, D_MODEL), lambda i: (_cond_row(i), 0, 0))


def _full_spec(shape):
    return pl.BlockSpec(shape, lambda i: (0,) * len(shape))


def _ctx_lat_specs(cols):
    return [pl.BlockSpec((TM, cols), lambda i: (jnp.minimum(i, NB_CTX - 1), 0)),
            pl.BlockSpec((TM, cols), lambda i: (jnp.maximum(i - NB_CTX, 0), 0))]


def _inproj_first(x_ctx, x_lat, mod, g, w_in, layer):
    return pl.pallas_call(
        _inproj_first_kernel,
        grid=(NB_ALL,),
        in_specs=_ctx_lat_specs(D_MODEL) + [_mod_spec(), _full_spec((1, D_MODEL)),
                                            _layer_weight_spec(layer, D_MODEL, IN_COLS)],
        out_specs=[_tok_spec(IN_COLS), _tok_spec(D_MODEL)],
        out_shape=[jax.ShapeDtypeStruct((T_ALL, IN_COLS), BF16),
                   jax.ShapeDtypeStruct((T_ALL, D_MODEL), F32)],
        scratch_shapes=[pltpu.VMEM((D_MODEL, IN_COLS), BF16)],
        compiler_params=_cparams(("arbitrary",)),
        name="inproj_first",
    )(x_ctx, x_lat, mod, g, w_in)


def _inproj_next(x, ys, route, mod_prev, mod, g, w_in, layer):
    return pl.pallas_call(
        _inproj_next_kernel,
        grid=(NB_ALL,),
        in_specs=[_tok_spec(D_MODEL),
                  pl.BlockSpec((MOE_LC, D_HALF), lambda i: (i, 0)),
                  _tok_spec(ROUTE_COLS),
                  _mod_spec(), _mod_spec(), _full_spec((1, D_MODEL)),
                  _layer_weight_spec(layer, D_MODEL, IN_COLS)],
        out_specs=[_tok_spec(IN_COLS), _tok_spec(D_MODEL)],
        out_shape=[jax.ShapeDtypeStruct((T_ALL, IN_COLS), BF16),
                   jax.ShapeDtypeStruct((T_ALL, D_MODEL), F32)],
        scratch_shapes=[pltpu.VMEM((D_MODEL, IN_COLS), BF16)],
        compiler_params=_cparams(("arbitrary",)),
        name="inproj_next",
    )(x, ys, route, mod_prev, mod, g, w_in)


CONV_PAD = 16
CONV_CHUNK = 64


CONV_SPAN = CONV_CHUNK + 2 * CONV_PAD - SUBLANES


CONV_UNROLL = 4


def _conv_kernel(seq, z_ref, w_ref, b_ref, g_ref, be_ref, o_ref, upad_ref, shift_refs):
    zeros = jnp.zeros((CONV_PAD, CONV_CH), F32)
    upad_ref[0:CONV_PAD, :] = zeros
    upad_ref[seq + CONV_PAD:seq + 2 * CONV_PAD, :] = zeros

    def glu(ci, carry):
        base = pl.multiple_of(ci * 256, 256)
        zc = z_ref[pl.ds(base, 256), :].astype(F32)
        upad_ref[pl.ds(base + CONV_PAD, 256), :] = zc[:, :CONV_CH] * _sigmoid(zc[:, CONV_CH:])
        return carry

    lax.fori_loop(0, seq // 256, glu, 0)

    shift = CONV_PAD - CONV_K // 2

    def chunk(ci, shift_ref):
        base = pl.multiple_of(ci * CONV_CHUNK, CONV_CHUNK)
        win = upad_ref[pl.ds(base, CONV_CHUNK + 2 * CONV_PAD), :]
        acc = jnp.zeros((CONV_CHUNK, CONV_CH), F32)
        for sub in range(SUBLANES):
            shift_ref[sub] = win[sub:sub + CONV_SPAN, :]
            for k in range(CONV_K):
                if (k + shift) % SUBLANES == sub:
                    lo = k + shift - sub
                    acc = acc + w_ref[k:k + 1, :] * shift_ref[sub, lo:lo + CONV_CHUNK, :]
        acc = acc + b_ref[...]
        mu = jnp.mean(acc, axis=-1, keepdims=True)
        d = acc - mu
        var = jnp.mean(d * d, axis=-1, keepdims=True)
        n = d * lax.rsqrt(var + EPS) * g_ref[...] + be_ref[...]
        o_ref[pl.ds(base, CONV_CHUNK), :] = (n * _sigmoid(n)).astype(BF16)

    def chunks(cj, carry):
        for u in range(CONV_UNROLL):
            chunk(cj * CONV_UNROLL + u, shift_refs.at[u])
        return carry

    lax.fori_loop(0, seq // (CONV_CHUNK * CONV_UNROLL), chunks, 0)


def _conv(z, row_block0, nseq, seq, w, b, g, be):
    return pl.pallas_call(
        functools.partial(_conv_kernel, seq),
        grid=(nseq,),
        in_specs=[pl.BlockSpec((seq, 2 * CONV_CH), lambda s: (row_block0 + s, 0)),
                  _full_spec((CONV_K, CONV_CH)), _full_spec((1, CONV_CH)),
                  _full_spec((1, CONV_CH)), _full_spec((1, CONV_CH))],
        out_specs=pl.BlockSpec((seq, CONV_CH), lambda s: (s, 0)),
        out_shape=jax.ShapeDtypeStruct((nseq * seq, CONV_CH), BF16),
        scratch_shapes=[pltpu.VMEM((seq + 2 * CONV_PAD, CONV_CH), F32),
                        pltpu.VMEM((CONV_UNROLL, SUBLANES, CONV_SPAN, CONV_CH), F32)],
        compiler_params=_cparams(("arbitrary",)),
        name="conv_seq%d" % seq,
    )(z, w, b, g, be)


def _dot_nt(a, b):
    return lax.dot_general(a, b, (((1,), (1,)), ((), ())), preferred_element_type=F32)


NA_SCALE = NA_DIM ** -0.5
assert NA_SCALE == 2.0 ** round(np.log2(NA_SCALE)), "query pre-scaling assumes a power-of-two scale"


def _ctx_attn_kernel(layer, q_ref, k_ref, v_ref, *refs):
    if layer:
        _, _, o_ref, ko_ref, vo_ref = refs
    else:
        o_ref, ko_full, vo_full = refs
        ko_ref, vo_ref = ko_full.at[0], vo_full.at[0]
        for j in range(1, DEPTH):
            ko_full[j] = jnp.zeros(ko_full.shape[1:], F32)
            vo_full[j] = jnp.zeros(vo_full.shape[1:], F32)
    pair = 2 * NA_DIM
    left = lax.broadcasted_iota(jnp.int32, (SEQ, pair), 1) < NA_DIM
    outs = []
    for p in range(NA_HEADS // 2):
        lanes = slice(p * pair, (p + 1) * pair)
        qp, kp, vp = q_ref[:, lanes], k_ref[:, lanes], v_ref[:, lanes]
        kf, vf = kp.astype(F32), vp.astype(F32)
        o_h = []
        for hh in range(2):
            cols = slice(hh * NA_DIM, (hh + 1) * NA_DIM)
            ko_ref[2 * p + hh] = kf[:, cols]
            vo_ref[2 * p + hh] = vf[:, cols]
            qm = jnp.where(left == (hh == 0), qp, jnp.zeros_like(qp))
            s = _dot_nt(qm, kp) * NA_SCALE
            m = jnp.max(s, axis=-1, keepdims=True)
            e = jnp.exp(s - m)
            den = jnp.sum(e, axis=-1, keepdims=True)
            o_h.append(jnp.dot(e.astype(BF16), vp, preferred_element_type=F32) / den)
        outs.append(jnp.where(left, o_h[0], o_h[1]))
    o_ref[...] = jnp.concatenate(outs, axis=-1).astype(BF16)


def _ctx_attn(z, layer, k_prev=None, v_prev=None):
    qb, kb, vb = COL_NA_Q // NA_WIDTH, COL_NA_K // NA_WIDTH, COL_NA_V // NA_WIDTH
    head_shape = jax.ShapeDtypeStruct((BATCH, DEPTH, NA_HEADS, SEQ, NA_DIM), F32)
    head_spec = pl.BlockSpec((None, DEPTH, NA_HEADS, SEQ, NA_DIM), lambda b: (b, 0, 0, 0, 0))
    in_specs = [pl.BlockSpec((SEQ, NA_WIDTH), lambda b: (b, qb)),
                pl.BlockSpec((SEQ, NA_WIDTH), lambda b: (b, kb)),
                pl.BlockSpec((SEQ, NA_WIDTH), lambda b: (b, vb))]
    args = [z, z, z]
    aliases = {}
    if layer:
        any_spec = pl.BlockSpec(memory_space=pl.ANY)
        in_specs += [any_spec, any_spec]
        args += [k_prev, v_prev]
        aliases = {3: 1, 4: 2}
        head_spec = pl.BlockSpec((None, None, NA_HEADS, SEQ, NA_DIM), lambda b: (b, layer, 0, 0, 0))
    return pl.pallas_call(
        functools.partial(_ctx_attn_kernel, layer),
        grid=(BATCH,),
        in_specs=in_specs,
        out_specs=[pl.BlockSpec((SEQ, NA_WIDTH), lambda b: (b, 0)), head_spec, head_spec],
        out_shape=[jax.ShapeDtypeStruct((T_CTX, NA_WIDTH), BF16), head_shape, head_shape],
        input_output_aliases=aliases,
        compiler_params=_cparams(("arbitrary",)),
        name="ctx_attn",
    )(*args)


NA_KINDS = (0, NA_ROWS, GRID_H - NA_ROWS)
N_DR = 2 * NA_KH - 1
N_DC = 2 * NA_KW - 1


def _na_row_offset(r0, i, j):
    ks = min(max(r0 - NA_KH // 2, 0), GRID_H - NA_KROWS)
    r, kr = r0 + i, ks + j
    rs = min(max(r - NA_KH // 2, 0), GRID_H - NA_KH)
    return kr - r + NA_KH - 1 if rs <= kr < rs + NA_KH else None


def _na_bias_kernel(rpb_ref, o_ref):
    lh = pl.program_id(0)
    shape = (GRID_W, 2 * GRID_W)
    qc = lax.broadcasted_iota(jnp.int32, shape, 0)
    lane = lax.broadcasted_iota(jnp.int32, shape, 1)
    kc = lane % GRID_W
    dc = jnp.clip(kc - qc, -(NA_KW - 1), NA_KW - 1) + NA_KW - 1
    cs = jnp.clip(qc - NA_KW // 2, 0, GRID_W - NA_KW)
    col_ok = (kc >= cs) & (kc < cs + NA_KW)
    neg = jnp.full(shape, NEG_INF, F32)
    tiles = []
    for dr in range(N_DR):
        base = (lh * N_DR + dr) * N_DC
        val = jnp.zeros(shape, F32)
        for d in range(N_DC):
            val = jnp.where(dc == d, rpb_ref[base + d], val)
        tiles.append(jnp.where(col_ok, val, neg))
    left = lane < GRID_W
    for kind, r0 in enumerate(NA_KINDS):
        for i in range(NA_ROWS):
            for jp in range(NA_KROWS // 2):
                dl, dr_ = _na_row_offset(r0, i, 2 * jp), _na_row_offset(r0, i, 2 * jp + 1)
                tl = neg if dl is None else tiles[dl]
                tr = neg if dr_ is None else tiles[dr_]
                o_ref[kind, i * GRID_W:(i + 1) * GRID_W, jp * 2 * GRID_W:(jp + 1) * 2 * GRID_W] = (
                    jnp.where(left, tl, tr))


def _na_bias_tables(rpb):
    return pl.pallas_call(
        _na_bias_kernel,
        grid=(DEPTH * NA_HEADS,),
        in_specs=[pl.BlockSpec(memory_space=pltpu.SMEM)],
        out_specs=pl.BlockSpec((None, len(NA_KINDS), NA_Q, NA_KEYS), lambda i: (i, 0, 0, 0)),
        out_shape=jax.ShapeDtypeStruct((DEPTH * NA_HEADS, len(NA_KINDS), NA_Q, NA_KEYS), F32),
        compiler_params=_cparams(("arbitrary",)),
        name="nbr_bias",
    )(rpb.reshape(-1))


NA_G = 4


def _na_kernel(q_ref, k_ref, v_ref, kc_ref, vc_ref, bias_ref, o_ref):
    rb = pl.program_id(2)
    ks = jnp.clip(rb * NA_ROWS - NA_KH // 2, 0, GRID_H - NA_KROWS)
    start = pl.multiple_of(ks * GRID_W, GRID_W)
    q = q_ref[...] * NA_SCALE
    kl = k_ref[pl.ds(start, NA_KEYS), :]
    vl = v_ref[pl.ds(start, NA_KEYS), :]
    pair = 2 * NA_DIM
    left = lax.broadcasted_iota(jnp.int32, (NA_Q, pair), 1) < NA_DIM
    ones_loc = jnp.ones((NA_KEYS, pair), BF16)
    ones_ctx = jnp.ones((PAST_LEN, pair), BF16)
    kc, v_ext, vc_ext = [], [], []
    for p in range(NA_G // 2):
        lanes = slice(p * pair, (p + 1) * pair)
        kc.append(jnp.concatenate([kc_ref[2 * p].astype(BF16), kc_ref[2 * p + 1].astype(BF16)], axis=-1))
        vc = jnp.concatenate([vc_ref[2 * p].astype(BF16), vc_ref[2 * p + 1].astype(BF16)], axis=-1)
        v_ext.append(jnp.concatenate([vl[:, lanes], ones_loc], axis=-1))
        vc_ext.append(jnp.concatenate([vc, ones_ctx], axis=-1))

    def scores(hh):
        p = hh // 2
        lanes = slice(p * pair, (p + 1) * pair)
        qm = jnp.where(left == (hh % 2 == 0), q[:, lanes], jnp.zeros((NA_Q, pair), BF16))
        return _dot_nt(qm, kl[:, lanes]) + bias_ref[hh], _dot_nt(qm, kc[p])

    outs = []
    nxt = scores(0)
    for hh in range(NA_G):
        s_loc, s_ctx = nxt
        if hh + 1 < NA_G:
            nxt = scores(hh + 1)
        m = jnp.maximum(jnp.max(s_loc, axis=-1, keepdims=True), jnp.max(s_ctx, axis=-1, keepdims=True))
        p_loc = jnp.exp(s_loc - m).astype(BF16)
        p_ctx = jnp.exp(s_ctx - m).astype(BF16)
        o = (jnp.dot(p_loc, v_ext[hh // 2], preferred_element_type=F32)
             + jnp.dot(p_ctx, vc_ext[hh // 2], preferred_element_type=F32))
        outs.append(o[:, :pair] / o[:, pair:])
    o_ref[...] = jnp.concatenate([jnp.where(left, outs[2 * p], outs[2 * p + 1]) for p in range(NA_G // 2)],
                                 axis=-1).astype(BF16)


def _na_attn(z, cache_k, cache_v, bias, layer):
    lat_q0 = T_CTX // NA_Q
    lat_s0 = T_CTX // DEC_SEQ
    width = NA_G * NA_DIM
    qc, kc, vc = COL_NA_Q // width, COL_NA_K // width, COL_NA_V // width
    groups = NA_HEADS // NA_G

    def kind(rb):
        return jnp.where(rb == 0, 0, jnp.where(rb == NA_RB - 1, 2, 1))

    ctx_spec = pl.BlockSpec((None, None, NA_G, PAST_LEN, NA_DIM), lambda b, hg, rb: (b, layer, hg, 0, 0))
    return pl.pallas_call(
        _na_kernel,
        grid=(DEC_BATCH, groups, NA_RB),
        in_specs=[pl.BlockSpec((NA_Q, width), lambda b, hg, rb: (lat_q0 + b * NA_RB + rb, qc + hg)),
                  pl.BlockSpec((DEC_SEQ, width), lambda b, hg, rb: (lat_s0 + b, kc + hg)),
                  pl.BlockSpec((DEC_SEQ, width), lambda b, hg, rb: (lat_s0 + b, vc + hg)),
                  ctx_spec, ctx_spec,
                  pl.BlockSpec((NA_G, None, NA_Q, NA_KEYS),
                               lambda b, hg, rb: (layer * groups + hg, kind(rb), 0, 0))],
        out_specs=pl.BlockSpec((NA_Q, width), lambda b, hg, rb: (b * NA_RB + rb, hg)),
        out_shape=jax.ShapeDtypeStruct((T_LAT, NA_WIDTH), BF16),
        compiler_params=_cparams(("arbitrary", "arbitrary", "arbitrary")),
        name="nbr_attn",
    )(z, z, z, cache_k, cache_v, bias)


RET_PAIR = 2 * RET_DIM
RET_NPAIR = RET_HEADS // 2
assert RET_PAIR == LANES and RET_CHUNK == LANES
RET_UNROLL = 8


def _rope_tables():
    n_freq = RET_DIM // 4
    t = np.arange(DEC_SEQ)
    inv = jnp.asarray(ROPE_BASE, F32) ** (-jnp.arange(n_freq, dtype=F32) / n_freq)
    ang_r = jnp.asarray(t // GRID_W, F32)[:, None] * inv[None, :]
    ang_c = jnp.asarray(t % GRID_W, F32)[:, None] * inv[None, :]
    cos = jnp.concatenate([jnp.cos(ang_r)] * 2 + [jnp.cos(ang_c)] * 2, axis=-1)
    sin = jnp.concatenate([-jnp.sin(ang_r), jnp.sin(ang_r), -jnp.sin(ang_c), jnp.sin(ang_c)], axis=-1)
    lane = np.arange(RET_WIDTH)
    src = np.where(lane % (2 * n_freq) < n_freq, lane + n_freq, lane - n_freq)
    swap = np.zeros((RET_WIDTH, RET_WIDTH), np.float32)
    swap[src, lane] = 1.0
    return jnp.tile(cos, (1, RET_HEADS)), jnp.tile(sin, (1, RET_HEADS)), jnp.asarray(swap, BF16)


def _ret_kernel(seq, latent, *refs):
    if latent:
        (lg_ref, z_ref, gn_ref, cos_ref, sin_ref, swap_ref, s0f_ref, s0b_ref, y_ref,
         q_s, k_s, kv_s, st_s) = refs
    else:
        lg_ref, z_ref, gn_ref, y_ref, sf_ref, sb_ref, q_s, k_s, kv_s, st_s = refs
    nc = seq // RET_CHUNK
    ch, hd, pw = RET_CHUNK, RET_DIM, RET_PAIR

    row = lax.broadcasted_iota(jnp.int32, (ch, ch), 0).astype(F32)
    col = lax.broadcasted_iota(jnp.int32, (ch, ch), 1).astype(F32)
    pos = lax.broadcasted_iota(jnp.int32, (ch, pw), 0).astype(F32)
    left = lax.broadcasted_iota(jnp.int32, (ch, pw), 1) < hd
    top = lax.broadcasted_iota(jnp.int32, (pw, pw), 0) < hd
    same_head = top == (lax.broadcasted_iota(jnp.int32, (pw, pw), 1) < hd)
    same_head2 = jnp.concatenate([same_head, same_head], axis=0)

    def per_head(mask, fn, p):
        return jnp.where(mask, fn(2 * p), fn(2 * p + 1))

    decay = []
    for h in range(RET_HEADS):
        lf, lb = lg_ref[0, h], lg_ref[1, h]
        d_f = jnp.where(row >= col, jnp.exp(jnp.maximum(row - col, 0.0) * lf), 0.0)
        d_b = jnp.where(col >= row, jnp.exp(jnp.maximum(col - row, 0.0) * lb), 0.0)
        decay.append(d_f + d_b)
    q_dec, k_dec, c_dec_f, c_dec_b = [], [], [], []
    for p in range(RET_NPAIR):
        q_dec.append(jnp.concatenate(
            [per_head(left, lambda h: jnp.exp((pos + 1.0) * lg_ref[0, h]), p),
             per_head(left, lambda h: jnp.exp((ch - pos) * lg_ref[1, h]), p)], axis=-1))
        k_dec.append(jnp.concatenate(
            [per_head(left, lambda h: jnp.exp((ch - 1.0 - pos) * lg_ref[0, h]), p),
             per_head(left, lambda h: jnp.exp(pos * lg_ref[1, h]), p)], axis=-1))
        zero = jnp.zeros((pw, pw), F32)
        c_dec_f.append(per_head(top, lambda h: jnp.exp(zero + ch * lg_ref[0, h]), p))
        c_dec_b.append(per_head(top, lambda h: jnp.exp(zero + ch * lg_ref[1, h]), p))

    def rope(x, base):
        xf = x.astype(F32)
        if not latent:
            return xf
        swapped = jnp.dot(x, swap_ref[...], preferred_element_type=F32)
        return xf * cos_ref[pl.ds(base, ch), :] + swapped * sin_ref[pl.ds(base, ch), :]

    def pass1(n, carry):
        base = pl.multiple_of(n * ch, ch)
        zc = z_ref[pl.ds(base, ch), :]
        q = rope(zc[:, 0:RET_WIDTH], base)
        k = rope(zc[:, RET_WIDTH:2 * RET_WIDTH], base) * (RET_DIM ** -0.5)
        q_s[pl.ds(base, ch), :] = q.astype(BF16)
        k_s[pl.ds(base, ch), :] = k.astype(BF16)
        v = zc[:, 2 * RET_WIDTH:3 * RET_WIDTH]
        for p in range(RET_NPAIR):
            lanes = slice(p * pw, (p + 1) * pw)
            kp = k[:, lanes]
            k2 = (jnp.concatenate([kp, kp], axis=-1) * k_dec[p]).astype(BF16)
            kv = lax.dot_general(k2, v[:, lanes], (((0,), (0,)), ((), ())), preferred_element_type=F32)
            kv_s[n, p] = jnp.where(same_head2, kv, 0.0)
        return carry

    lax.fori_loop(0, nc, pass1, 0, unroll=min(RET_UNROLL, nc))

    def block_diag(a, b):
        z = jnp.zeros((hd, hd), F32)
        return jnp.concatenate([jnp.concatenate([a, z], axis=1), jnp.concatenate([z, b], axis=1)], axis=0)

    for p in range(RET_NPAIR):
        if latent:
            s_f = block_diag(s0f_ref[2 * p], s0f_ref[2 * p + 1])
            s_b = block_diag(s0b_ref[2 * p], s0b_ref[2 * p + 1])
        else:
            s_f = s_b = jnp.zeros((pw, pw), F32)

        def fwd(n, s, p=p):
            st_s[n, p, 0:pw, :] = s.astype(BF16)
            return c_dec_f[p] * s + kv_s[n, p, 0:pw, :]

        def bwd(i, s, p=p):
            n = nc - 1 - i
            st_s[n, p, pw:2 * pw, :] = s.astype(BF16)
            return c_dec_b[p] * s + kv_s[n, p, pw:2 * pw, :]

        s_f = lax.fori_loop(0, nc, fwd, s_f)
        s_b = lax.fori_loop(0, nc, bwd, s_b)
        if not latent:
            for hh in range(2):
                blk = slice(hh * hd, (hh + 1) * hd)
                sf_ref[2 * p + hh] = s_f[blk, blk]
                sb_ref[2 * p + hh] = s_b[blk, blk]

    def pass3(n, carry):
        base = pl.multiple_of(n * ch, ch)
        zc = z_ref[pl.ds(base, ch), :]
        q = q_s[pl.ds(base, ch), :]
        k = k_s[pl.ds(base, ch), :]
        v = zc[:, 2 * RET_WIDTH:3 * RET_WIDTH]
        gate = zc[:, 3 * RET_WIDTH:4 * RET_WIDTH].astype(F32)
        outs = []
        for p in range(RET_NPAIR):
            lanes = slice(p * pw, (p + 1) * pw)
            qp, kp, vp = q[:, lanes], k[:, lanes], v[:, lanes]
            o_h = []
            for hh in range(2):
                qm = jnp.where(left == (hh == 0), qp, jnp.zeros_like(qp))
                s = _dot_nt(qm, kp) * decay[2 * p + hh]
                o_h.append(jnp.dot(s.astype(BF16), vp, preferred_element_type=F32))
            qf = qp.astype(F32)
            q2 = (jnp.concatenate([qf, qf], axis=-1) * q_dec[p]).astype(BF16)
            o = jnp.where(left, o_h[0], o_h[1]) + jnp.dot(q2, st_s[n, p], preferred_element_type=F32)

            def half_mean(t):
                s_l = jnp.sum(jnp.where(left, t, 0.0), axis=-1, keepdims=True)
                s_r = jnp.sum(jnp.where(left, 0.0, t), axis=-1, keepdims=True)
                return jnp.where(left, s_l, s_r) * (1.0 / hd)

            d = o - half_mean(o)
            outs.append(d * lax.rsqrt(half_mean(d * d) + EPS))
        nrm = jnp.concatenate(outs, axis=-1)
        y_ref[pl.ds(base, ch), :] = (nrm * gn_ref[...] * (gate * _sigmoid(gate))).astype(BF16)
        return carry

    lax.fori_loop(0, nc, pass3, 0, unroll=min(RET_UNROLL, nc))


def _retention(z, lg, gn_g, latent, layer=None, rope=None, s0_f=None, s0_b=None):
    seq = DEC_SEQ if latent else SEQ
    nseq = DEC_BATCH if latent else BATCH
    nc = seq // RET_CHUNK
    row0 = (T_CTX // DEC_SEQ) if latent else 0
    cb = COL_RET // (4 * RET_WIDTH)
    in_specs = [pl.BlockSpec(memory_space=pltpu.SMEM),
                pl.BlockSpec((seq, 4 * RET_WIDTH), lambda s: (row0 + s, cb)),
                _full_spec((1, RET_WIDTH))]
    args = [lg, z, gn_g]
    state_shape = jax.ShapeDtypeStruct((nseq, RET_HEADS, RET_DIM, RET_DIM), F32)
    y_spec = pl.BlockSpec((seq, RET_WIDTH), lambda s: (s, 0))
    y_shape = jax.ShapeDtypeStruct((nseq * seq, RET_WIDTH), BF16)
    if latent:
        st_spec = pl.BlockSpec((None, None, RET_HEADS, RET_DIM, RET_DIM), lambda s: (s, layer, 0, 0, 0))

        def const_spec(shape):
            return pl.BlockSpec(shape, lambda s: (0,) * len(shape), pipeline_mode=pl.Buffered(1))

        in_specs += [const_spec((seq, RET_WIDTH)), const_spec((seq, RET_WIDTH)),
                     const_spec((RET_WIDTH, RET_WIDTH)), st_spec, st_spec]
        args += [rope[0], rope[1], rope[2], s0_f, s0_b]
        out_specs, out_shape = y_spec, y_shape
    else:
        so_spec = pl.BlockSpec((None, RET_HEADS, RET_DIM, RET_DIM), lambda s: (s, 0, 0, 0))
        out_specs, out_shape = [y_spec, so_spec, so_spec], [y_shape, state_shape, state_shape]
    return pl.pallas_call(
        functools.partial(_ret_kernel, seq, latent),
        grid=(nseq,),
        in_specs=in_specs,
        out_specs=out_specs,
        out_shape=out_shape,
        scratch_shapes=[pltpu.VMEM((seq, RET_WIDTH), BF16), pltpu.VMEM((seq, RET_WIDTH), BF16),
                        pltpu.VMEM((nc, RET_NPAIR, 2 * RET_PAIR, RET_PAIR), F32),
                        pltpu.VMEM((nc, RET_NPAIR, 2 * RET_PAIR, RET_PAIR), BF16)],
        compiler_params=_cparams(("arbitrary",)),
        name="retention_lat" if latent else "retention_ctx",
    )(*args)


def _route(logits):
    lane = lax.broadcasted_iota(jnp.int32, logits.shape, 1)
    lane_f = lane.astype(F32)
    big = float(ROUTE_COLS)
    neg = -jnp.inf
    is_grp = lane < N_GROUPS
    gl = jnp.where(is_grp, logits, neg)
    gmax = jnp.max(gl, axis=-1, keepdims=True)
    grp = jnp.min(jnp.where(gl == gmax, lane_f, big), axis=-1, keepdims=True)
    p_grp = 1.0 / jnp.sum(jnp.exp(gl - gmax), axis=-1, keepdims=True)
    e_f = lane_f - N_GROUPS
    lo = grp * EXPERTS_PER_GROUP
    in_grp = (e_f >= lo) & (e_f < lo + EXPERTS_PER_GROUP)
    el = jnp.where(in_grp, logits, neg)
    m1 = jnp.max(el, axis=-1, keepdims=True)
    i1 = jnp.min(jnp.where(el == m1, lane_f, big), axis=-1, keepdims=True)
    el2 = jnp.where(lane_f == i1, neg, el)
    m2 = jnp.max(el2, axis=-1, keepdims=True)
    i2 = jnp.min(jnp.where(el2 == m2, lane_f, big), axis=-1, keepdims=True)
    t = jnp.exp(m2 - m1)
    g1 = p_grp / (1.0 + t)
    g2 = p_grp * t / (1.0 + t)
    rows = logits.shape[0]
    oh1, oh2 = lane_f == i1, lane_f == i2
    oh = jnp.where(oh1 | oh2, 1.0, 0.0)
    tri = (lax.broadcasted_iota(jnp.int32, (rows, rows), 0)
           > lax.broadcasted_iota(jnp.int32, (rows, rows), 1))
    rank = jnp.dot(jnp.where(tri, 1.0, 0.0).astype(BF16), oh.astype(BF16), preferred_element_type=F32)
    tiles = jnp.floor((jnp.sum(oh, axis=0, keepdims=True) + (SUBLANES - 1)) * (1.0 / SUBLANES))
    upper = (lax.broadcasted_iota(jnp.int32, (ROUTE_COLS, ROUTE_COLS), 0)
             < lax.broadcasted_iota(jnp.int32, (ROUTE_COLS, ROUTE_COLS), 1))
    start = SUBLANES * jnp.dot(jnp.broadcast_to(tiles, (SUBLANES, ROUTE_COLS)).astype(BF16),
                               jnp.where(upper, 1.0, 0.0).astype(BF16),
                               preferred_element_type=F32)[0:1, :]
    pos = start + rank
    p1 = jnp.sum(jnp.where(oh1, pos, 0.0), axis=-1, keepdims=True)
    p2 = jnp.sum(jnp.where(oh2, pos, 0.0), axis=-1, keepdims=True)
    out = jnp.zeros(logits.shape, F32)
    for k, val in enumerate((i1 - N_GROUPS, i2 - N_GROUPS, g1, g2, p1, p2)):
        out = jnp.where(lane == k, val, out)
    return out, SUBLANES * tiles


def _outproj_kernel(ycc, ycl, ync, ynl, yrc, yrl, x_ref, mod_ref, g_ref, w_ref, wr_ref, br_ref,
                    xo_ref, xs_ref, r_ref, seg_ref, wb_ref):
    _cast_weights_once(w_ref, wb_ref)
    is_ctx = pl.program_id(0) < NB_CTX
    yc = jnp.where(is_ctx, ycc[...], ycl[...])
    yn = jnp.where(is_ctx, ync[...], ynl[...])
    yr = jnp.where(is_ctx, yrc[...], yrl[...])
    y = (jnp.dot(yc, wb_ref[0:CONV_CH, :], preferred_element_type=F32)
         + jnp.dot(yn, wb_ref[CONV_CH:CONV_CH + NA_WIDTH, :], preferred_element_type=F32)
         + jnp.dot(yr, wb_ref[CONV_CH + NA_WIDTH:, :], preferred_element_type=F32))
    x = x_ref[...] + mod_ref[2:3, :] * y
    xo_ref[...] = x
    h = _norm_mod(x, g_ref[...], mod_ref[3:4, :], mod_ref[4:5, :])
    h_hi = h.astype(BF16)
    h_lo = (h - h_hi.astype(F32)).astype(BF16)
    hw = jnp.dot(h_hi, wr_ref[...], preferred_element_type=F32)
    logits = (hw[:, :ROUTE_COLS] + hw[:, ROUTE_COLS:]
              + jnp.dot(h_lo, wr_ref[:, :ROUTE_COLS], preferred_element_type=F32) + br_ref[...])
    route, seg = _route(logits)
    r_ref[...] = route
    seg_ref[...] = jnp.broadcast_to(seg, seg_ref.shape)
    sel = _slot_onehot(route, 0) | _slot_onehot(route, 1)
    xs_ref[...] = _pack_bf16_pairs(lax.dot_general(jnp.where(sel, 1.0, 0.0).astype(BF16), h_hi,
                                                   (((0,), (0,)), ((), ())), preferred_element_type=F32))


def _outproj(y_conv, y_na, y_ret, x, mod, g, w_out, layer, w_route, b_route):
    return pl.pallas_call(
        _outproj_kernel,
        grid=(NB_ALL,),
        in_specs=(_ctx_lat_specs(CONV_CH) + _ctx_lat_specs(NA_WIDTH) + _ctx_lat_specs(RET_WIDTH)
                  + [_tok_spec(D_MODEL), _mod_spec(), _full_spec((1, D_MODEL)),
                     _layer_weight_spec(layer, D_MODEL, D_MODEL), _full_spec((D_MODEL, 2 * ROUTE_COLS)),
                     _full_spec((1, ROUTE_COLS))]),
        out_specs=[_tok_spec(D_MODEL), pl.BlockSpec((MOE_LC, D_HALF), lambda i: (i, 0)),
                   _tok_spec(ROUTE_COLS), pl.BlockSpec((None, SUBLANES, ROUTE_COLS), lambda i: (i, 0, 0))],
        out_shape=[jax.ShapeDtypeStruct((T_ALL, D_MODEL), F32),
                   jax.ShapeDtypeStruct((NB_ALL * MOE_LC, D_HALF), U32),
                   jax.ShapeDtypeStruct((T_ALL, ROUTE_COLS), F32),
                   jax.ShapeDtypeStruct((NB_ALL, SUBLANES, ROUTE_COLS), F32)],
        scratch_shapes=[pltpu.VMEM((D_MODEL, D_MODEL), BF16)],
        compiler_params=_cparams(("arbitrary",)),
        name="outproj_route",
    )(y_conv[0], y_conv[1], y_na[0], y_na[1], y_ret[0], y_ret[1], x, mod, g, w_out, w_route, b_route)


def _dispatch_tables(seg):
    seg_len = seg[:, 0, N_GROUPS:N_GROUPS + N_EXPERTS].astype(jnp.int32)
    experts = jnp.arange(N_EXPERTS, dtype=jnp.int32)
    in_chunk = jnp.cumsum(seg_len, axis=1) - seg_len
    seg_row = in_chunk + MOE_LC * jnp.arange(N_CHUNK, dtype=jnp.int32)[:, None]
    seg_off = jnp.cumsum(seg_len, axis=0) - seg_len
    rows_e = jnp.sum(seg_len, axis=0)
    chunk_rows = jnp.sum(seg_len, axis=1)
    nblk = (rows_e + MOE_BLK - 1) // MOE_BLK
    blk_end = jnp.cumsum(nblk)
    blk_start = blk_end - nblk
    blk = jnp.arange(MOE_NBLK, dtype=jnp.int32)
    n_active = blk_end[-1]
    blk_e = jnp.minimum(jnp.sum((blk_end[None, :] <= jnp.minimum(blk, n_active - 1)[:, None]).astype(jnp.int32),
                                axis=-1), N_EXPERTS - 1)
    mine = blk_e[:, None] == experts[None, :]
    blk_lo = (blk - jnp.sum(jnp.where(mine, blk_start[None, :], 0), axis=-1)) * MOE_BLK
    left = jnp.sum(jnp.where(mine, rows_e[None, :], 0), axis=-1) - blk_lo
    blk_nv = jnp.where(blk < n_active, jnp.clip(left, 0, MOE_BLK), 0).astype(jnp.int32)
    off_b = jnp.sum(jnp.where(mine[:, None, :], seg_off[None, :, :], 0), axis=-1)
    end_b = off_b + jnp.sum(jnp.where(mine[:, None, :], seg_len[None, :, :], 0), axis=-1)
    blk_c0 = jnp.sum((end_b <= blk_lo[:, None]).astype(jnp.int32), axis=-1)
    blk_c1 = jnp.sum((off_b < (blk_lo + blk_nv)[:, None]).astype(jnp.int32), axis=-1)
    after = jnp.sum(jnp.where(mine, blk_end[None, :], 0), axis=-1)
    blk_next_e = jnp.where(after < n_active, jnp.take(blk_e, jnp.minimum(after, MOE_NBLK - 1)), -1)
    row_b = jnp.sum(jnp.where(mine[:, None, :], seg_row[None, :, :], 0), axis=-1)
    first = jnp.maximum(off_b, blk_lo[:, None])
    piece_n = jnp.minimum(end_b, (blk_lo + blk_nv)[:, None]) - first
    piece_src = row_b + first - off_b
    piece_dst = first - blk_lo[:, None]
    return (blk_e, blk_next_e.astype(jnp.int32), blk_nv, blk_c0, blk_c1, piece_src.reshape(-1),
            piece_dst.reshape(-1), piece_n.reshape(-1), chunk_rows)


def _moe_kernel(layer, blk_e, blk_next_e, blk_nv, blk_c0, blk_c1, piece_src, piece_dst, piece_n,
                chunk_rows, xs_hbm, w1_hbm, w3_hbm, w2_hbm, ys_hbm, xbuf, obuf, zeros,
                w1f, w3f, w2f, w1b, w3b, w2b, gsem, ssem, zsem, wsem):
    i = pl.program_id(0)
    last = pl.num_programs(0) - 1
    slot = i % 2

    def tiles(v):
        return pl.multiple_of(v, SUBLANES)

    def for_segments(blk, fn):
        def body(c, carry):
            k = blk * N_CHUNK + c
            n = piece_n[k]

            @pl.when(n > 0)
            def _():
                fn(tiles(piece_src[k]), tiles(piece_dst[k]), tiles(n))

            return carry

        lax.fori_loop(blk_c0[blk], blk_c1[blk], body, 0)

    def weight_copies(e):
        return [pltpu.make_async_copy(src.at[layer, e], dst, wsem)
                for src, dst in ((w1_hbm, w1f), (w3_hbm, w3f), (w2_hbm, w2f))]

    def start_gathers(blk, s):
        for_segments(blk, lambda src, dst, n: pltpu.make_async_copy(
            xs_hbm.at[pl.ds(src, n)], xbuf.at[s, pl.ds(dst, n)], gsem.at[s]).start())

    def start_scatters(blk, s):
        for_segments(blk, lambda dst, src, n: pltpu.make_async_copy(
            obuf.at[s, pl.ds(src, n)], ys_hbm.at[pl.ds(dst, n)], ssem.at[s]).start())

    def wait_rows(blk, s, sem):
        n = tiles(blk_nv[blk])

        @pl.when(n > 0)
        def _():
            pltpu.make_async_copy(xs_hbm.at[pl.ds(0, n)], xbuf.at[s, pl.ds(0, n)], sem.at[s]).wait()

    @pl.when(i == 0)
    def _():
        xbuf[...] = jnp.zeros_like(xbuf)
        zeros[...] = jnp.zeros_like(zeros)

        def tail(c):
            n = tiles(MOE_LC - chunk_rows[c])
            return n, pltpu.make_async_copy(zeros.at[pl.ds(0, n)],
                                            ys_hbm.at[pl.ds(tiles(c * MOE_LC + chunk_rows[c]), n)], zsem)

        def fill(c, carry):
            n, copy = tail(c)
            pl.when(n > 0)(copy.start)
            return carry

        def drain(c, carry):
            n, copy = tail(c)
            pl.when(n > 0)(copy.wait)
            return carry

        lax.fori_loop(0, N_CHUNK, fill, 0)
        lax.fori_loop(0, N_CHUNK, drain, 0)
        start_gathers(0, 0)
        for copy in weight_copies(blk_e[0]):
            copy.start()

    @pl.when(i < last)
    def _():
        start_gathers(i + 1, 1 - slot)

    @pl.when(i >= 2)
    def _():
        wait_rows(i - 2, slot, ssem)

    @pl.when(blk_nv[i] > 0)
    def _():
        @pl.when((i == 0) | (blk_e[i] != blk_e[jnp.maximum(i - 1, 0)]))
        def _():
            for copy in weight_copies(blk_e[i]):
                copy.wait()
            w1b[...] = w1f[...].astype(BF16)
            w3b[...] = w3f[...].astype(BF16)
            w2b[...] = w2f[...].astype(BF16)

            @pl.when(blk_next_e[i] >= 0)
            def _():
                for copy in weight_copies(blk_next_e[i]):
                    copy.start()

        wait_rows(i, slot, gsem)

        def expert_mlp(rows):
            x_lo, x_hi = _unpack_bf16_pairs(xbuf[slot, 0:rows, :])
            n_hid = D_EXPERT // MXU_TILE

            def in_dot(w, t):
                cols = slice(t * MXU_TILE, (t + 1) * MXU_TILE)
                return (jnp.dot(x_lo, w[:D_HALF, cols], preferred_element_type=F32)
                        + jnp.dot(x_hi, w[D_HALF:, cols], preferred_element_type=F32))

            ab = [(in_dot(w1b, t), in_dot(w3b, t)) for t in range(n_hid)]
            mid = [(a * _sigmoid(a) * b).astype(BF16) for a, b in ab]

            def out_dot(t):
                cols = slice(t * MXU_TILE, (t + 1) * MXU_TILE)
                return sum(jnp.dot(mid[j], w2b[j * MXU_TILE:(j + 1) * MXU_TILE, cols],
                                   preferred_element_type=F32) for j in range(n_hid))

            n_word = D_HALF // MXU_TILE
            for t in range(n_word):
                obuf[slot, 0:rows, t * MXU_TILE:(t + 1) * MXU_TILE] = _pack_words(
                    out_dot(t), out_dot(t + n_word))

        for rows in range(MOE_ROW_STEP, MOE_BLK + 1, MOE_ROW_STEP):
            @pl.when((blk_nv[i] > rows - MOE_ROW_STEP) & (blk_nv[i] <= rows))
            def _(rows=rows):
                expert_mlp(rows)

        start_scatters(i, slot)

    @pl.when(i == last)
    def _():
        wait_rows(i - 1, 1 - slot, ssem)
        wait_rows(i, slot, ssem)


def _moe(xs, w1, w3, w2, layer, blk_e, blk_next_e, blk_nv, blk_c0, blk_c1, piece_src, piece_dst,
         piece_n, chunk_rows):
    any_spec = pl.BlockSpec(memory_space=pl.ANY)
    grid_spec = pltpu.PrefetchScalarGridSpec(
        num_scalar_prefetch=9,
        grid=(MOE_NBLK,),
        in_specs=[any_spec, any_spec, any_spec, any_spec],
        out_specs=any_spec,
        scratch_shapes=[pltpu.VMEM((2, MOE_BLK, D_HALF), U32), pltpu.VMEM((2, MOE_BLK, D_HALF), U32),
                        pltpu.VMEM((MOE_LC - 2 * TM, D_HALF), U32),
                        pltpu.VMEM((D_MODEL, D_EXPERT), F32), pltpu.VMEM((D_MODEL, D_EXPERT), F32),
                        pltpu.VMEM((D_EXPERT, D_MODEL), F32),
                        pltpu.VMEM((D_MODEL, D_EXPERT), BF16), pltpu.VMEM((D_MODEL, D_EXPERT), BF16),
                        pltpu.VMEM((D_EXPERT, D_MODEL), BF16),
                        pltpu.SemaphoreType.DMA((2,)), pltpu.SemaphoreType.DMA((2,)),
                        pltpu.SemaphoreType.DMA, pltpu.SemaphoreType.DMA])
    return pl.pallas_call(
        functools.partial(_moe_kernel, layer),
        grid_spec=grid_spec,
        out_shape=jax.ShapeDtypeStruct((NB_ALL * MOE_LC, D_HALF), U32),
        compiler_params=_cparams(("arbitrary",)),
        name="moe_experts",
    )(blk_e, blk_next_e, blk_nv, blk_c0, blk_c1, piece_src, piece_dst, piece_n, chunk_rows,
      xs, w1, w3, w2)


def _final_kernel(x_ref, ys_ref, r_ref, mod_ref, g_ref, o_ref):
    x = _moe_residual(x_ref, ys_ref, r_ref, mod_ref)
    ms = jnp.mean(x * x, axis=-1, keepdims=True)
    o_ref[...] = x * lax.rsqrt(ms + EPS) * g_ref[...]


def _final(x, ys, route, mod, g, block0, nblocks):
    return pl.pallas_call(
        _final_kernel,
        grid=(nblocks,),
        in_specs=[pl.BlockSpec((TM, D_MODEL), lambda i: (block0 + i, 0)),
                  pl.BlockSpec((MOE_LC, D_HALF), lambda i: (block0 + i, 0)),
                  pl.BlockSpec((TM, ROUTE_COLS), lambda i: (block0 + i, 0)),
                  pl.BlockSpec((None, 6, D_MODEL), lambda i: (_cond_row(block0 + i), 0, 0)),
                  _full_spec((1, D_MODEL))],
        out_specs=_tok_spec(D_MODEL),
        out_shape=jax.ShapeDtypeStruct((nblocks * TM, D_MODEL), F32),
        compiler_params=_cparams(("arbitrary",)),
        name="final_norm",
    )(x, ys, route, mod, g)


def kernel(x_prompt, x_sample, c, cache_k, cache_v, state_ret_f, state_ret_b, c_ctx, w_ada, b_ada, norm1_g, norm2_g, w_in, w_out, conv_w, conv_b, conv_ln_g, conv_ln_b, na_rpb, ret_lg_f, ret_lg_b, ret_gn_g, w_route_g, b_route_g, w_route_e, b_route_e, w1, w3, w2, final_g):
    cv = jnp.zeros((COND_ROWS, D_MODEL), F32).at[0].set(c_ctx).at[1:N_COND].set(c)
    mods = _ada(cv, w_ada, b_ada).reshape(DEPTH, COND_ROWS, 6, D_MODEL)
    pad = ROUTE_COLS - N_GROUPS - N_EXPERTS
    w_route = jnp.pad(jnp.concatenate([w_route_g, w_route_e], axis=-1), ((0, 0), (0, 0), (0, pad)))
    b_route = jnp.pad(jnp.concatenate([b_route_g, b_route_e], axis=-1), ((0, 0), (0, pad)))
    w_route_hi = w_route.astype(BF16)
    w_route_lo = (w_route - w_route_hi.astype(F32)).astype(BF16)
    w_route = jnp.concatenate([w_route_hi, w_route_lo], axis=-1)
    na_bias = _na_bias_tables(na_rpb)
    rope = _rope_tables()
    lg = jnp.stack([ret_lg_f, ret_lg_b], axis=1)

    x_ctx = x_prompt.reshape(T_CTX, D_MODEL)
    x_lat = x_sample.reshape(T_LAT, D_MODEL)
    x = y = route = new_k = new_v = None
    sf_list, sb_list = [], []
    for l in range(DEPTH):
        g1 = norm1_g[l].reshape(1, D_MODEL)
        if l == 0:
            z, x = _inproj_first(x_ctx, x_lat, mods[l], g1, w_in, l)
        else:
            z, x = _inproj_next(x, y, route, mods[l - 1], mods[l], g1, w_in, l)
        conv_args = (conv_w[l], conv_b[l].reshape(1, -1), conv_ln_g[l].reshape(1, -1),
                     conv_ln_b[l].reshape(1, -1))
        yc_c = _conv(z, 0, BATCH, SEQ, *conv_args)
        yc_l = _conv(z, T_CTX // DEC_SEQ, DEC_BATCH, DEC_SEQ, *conv_args)
        yn_c, new_k, new_v = _ctx_attn(z, l, new_k, new_v)
        yn_l = _na_attn(z, cache_k, cache_v, na_bias, l)
        gn = ret_gn_g[l].reshape(1, RET_WIDTH)
        yr_c, sf_l, sb_l = _retention(z, lg[l], gn, latent=False)
        yr_l = _retention(z, lg[l], gn, latent=True, layer=l, rope=rope,
                          s0_f=state_ret_f, s0_b=state_ret_b)
        x, xs, route, seg = _outproj((yc_c, yc_l), (yn_c, yn_l), (yr_c, yr_l), x, mods[l],
                                     norm2_g[l].reshape(1, D_MODEL), w_out, l, w_route[l],
                                     b_route[l].reshape(1, ROUTE_COLS))
        y = _moe(xs, w1, w3, w2, l, *_dispatch_tables(seg))
        sf_list.append(sf_l)
        sb_list.append(sb_l)
    fg = final_g.reshape(1, D_MODEL)
    y_prompt = _final(x, y, route, mods[DEPTH - 1], fg, 0, NB_CTX).reshape(BATCH, SEQ, D_MODEL)
    y_sample = _final(x, y, route, mods[DEPTH - 1], fg, NB_CTX, NB_LAT).reshape(DEC_BATCH, DEC_SEQ, D_MODEL)
    return (y_prompt, y_sample, new_k, new_v, jnp.stack(sf_list, axis=1), jnp.stack(sb_list, axis=1))
```

```python
import functools

import numpy as np
import jax
import jax.numpy as jnp
from jax import lax
from jax.experimental import pallas as pl
from jax.experimental.pallas import tpu as pltpu

D_MODEL = 1024
BATCH = 32
SEQ = 256
DEPTH = 2
DEC_BATCH = 4
DEC_SEQ = 4096
PAST_LEN = 512
GRID_W = 64
GRID_H = DEC_SEQ // GRID_W
CONV_CH = 256
CONV_K = 31
NA_HEADS = 8
NA_DIM = 64
NA_WIDTH = NA_HEADS * NA_DIM
NA_KH = 8
NA_KW = 16
RET_HEADS = 4
RET_DIM = 64
RET_WIDTH = RET_HEADS * RET_DIM
RET_CHUNK = 128
ROPE_BASE = 10000.0
N_GROUPS = 4
EXPERTS_PER_GROUP = 8
N_EXPERTS = N_GROUPS * EXPERTS_PER_GROUP
D_EXPERT = 512
IN_COLS = 2 * CONV_CH + 3 * NA_WIDTH + 4 * RET_WIDTH
EPS = 1e-6
NEG_INF = -1e30

F32 = jnp.float32
BF16 = jnp.bfloat16
HIGHEST = lax.Precision.HIGHEST

T_CTX = BATCH * SEQ
T_LAT = DEC_BATCH * DEC_SEQ
T_ALL = T_CTX + T_LAT
N_COND = 1 + DEC_BATCH
COND_ROWS = 8

TM = 512
NB_CTX = T_CTX // TM
NB_LAT = T_LAT // TM
NB_ALL = NB_CTX + NB_LAT
LAT_BLOCKS_PER_REQ = DEC_SEQ // TM

LANES = 128
SUBLANES = 8
MXU_TILE = 256
ROUTE_COLS = LANES

COL_CONV = 0
COL_NA_Q = 2 * CONV_CH
COL_NA_K = COL_NA_Q + NA_WIDTH
COL_NA_V = COL_NA_K + NA_WIDTH
COL_RET = COL_NA_V + NA_WIDTH

NA_ROWS = 8
NA_Q = NA_ROWS * GRID_W
NA_KROWS = NA_ROWS + NA_KH
NA_KEYS = NA_KROWS * GRID_W
NA_RB = GRID_H // NA_ROWS

MOE_BLK = 1024
MOE_ROW_STEP = 256
MOE_LC = -(-(2 * TM + N_EXPERTS * (SUBLANES - 1)) // LANES) * LANES
N_CHUNK = NB_ALL
MOE_NBLK = -(-(N_CHUNK * MOE_LC) // MOE_BLK) + N_EXPERTS

VMEM_LIMIT = 56 * 1024 * 1024


def _cparams(sem):
    return pltpu.CompilerParams(dimension_semantics=sem, vmem_limit_bytes=VMEM_LIMIT)


def _sigmoid(x):
    return 1.0 / (1.0 + jnp.exp(-x))


def _cond_row(i):
    return jnp.where(i < NB_CTX, 0, 1 + (i - NB_CTX) // LAT_BLOCKS_PER_REQ)


ADA_TN = 1536


def _ada_kernel(cv_ref, w_ref, b_ref, o_ref):
    cv = cv_ref[...]
    s = cv * _sigmoid(cv)
    o_ref[...] = jnp.dot(s, w_ref[...], precision=HIGHEST, preferred_element_type=F32) + b_ref[...]


def _ada(cv, w_ada, b_ada):
    n = 6 * D_MODEL
    return pl.pallas_call(
        _ada_kernel,
        grid=(DEPTH, n // ADA_TN),
        in_specs=[
            pl.BlockSpec((COND_ROWS, D_MODEL), lambda l, j: (0, 0)),
            pl.BlockSpec((None, D_MODEL, ADA_TN), lambda l, j: (l, 0, j)),
            pl.BlockSpec((None, 1, ADA_TN), lambda l, j: (l, 0, j)),
        ],
        out_specs=pl.BlockSpec((None, COND_ROWS, ADA_TN), lambda l, j: (l, 0, j)),
        out_shape=jax.ShapeDtypeStruct((DEPTH, COND_ROWS, n), F32),
        compiler_params=_cparams(("arbitrary", "arbitrary")),
        name="ada_mod",
    )(cv, w_ada, b_ada.reshape(DEPTH, 1, n))


IN_TN = 768


def _norm_mod(x, g, shift, scale):
    ms = jnp.mean(x * x, axis=-1, keepdims=True)
    return (x * lax.rsqrt(ms + EPS) * g) * (1.0 + scale) + shift


def _cast_weights_once(w_ref, wb_ref):
    @pl.when(pl.program_id(0) == 0)
    def _():
        wb_ref[...] = w_ref[...].astype(BF16)


def _layer_weight_spec(layer, rows, cols):
    return pl.BlockSpec((None, rows, cols), lambda i: (layer, 0, 0), pipeline_mode=pl.Buffered(1))


def _inproj_body(x, mod_ref, g_ref, w_ref, wb_ref, z_ref):
    _cast_weights_once(w_ref, wb_ref)
    h = _norm_mod(x, g_ref[...], mod_ref[0:1, :], mod_ref[1:2, :]).astype(BF16)
    for c in range(IN_COLS // IN_TN):
        cols = slice(c * IN_TN, (c + 1) * IN_TN)
        z_ref[:, cols] = jnp.dot(h, wb_ref[:, cols], preferred_element_type=F32).astype(BF16)


def _inproj_first_kernel(xc_ref, xl_ref, mod_ref, g_ref, w_ref, z_ref, xo_ref, wb_ref):
    i = pl.program_id(0)
    x = jnp.where(i < NB_CTX, xc_ref[...], xl_ref[...])
    xo_ref[...] = x
    _inproj_body(x, mod_ref, g_ref, w_ref, wb_ref, z_ref)


U32 = jnp.uint32
D_HALF = D_MODEL // 2
_HI_MASK = np.uint32(0xFFFF0000)


def _pack_words(lo, hi):
    lo = lax.bitcast_convert_type(lo.astype(BF16).astype(F32), U32) >> 16
    hi = lax.bitcast_convert_type(hi.astype(BF16).astype(F32), U32) & _HI_MASK
    return lo | hi


def _pack_bf16_pairs(x):
    return _pack_words(x[:, :D_HALF], x[:, D_HALF:])


def _unpack_bf16_pairs(w):
    lo = lax.bitcast_convert_type(w << 16, F32).astype(BF16)
    hi = lax.bitcast_convert_type(w & _HI_MASK, F32).astype(BF16)
    return lo, hi


def _slot_onehot(route, slot):
    pos = route[:, 4 + slot:5 + slot].astype(jnp.int32)
    return lax.broadcasted_iota(jnp.int32, (route.shape[0], MOE_LC), 1) == pos


def _moe_residual(x_ref, ys_ref, r_ref, mod_ref):
    r = r_ref[...]
    sel = jnp.where(_slot_onehot(r, 0), r[:, 2:3], jnp.where(_slot_onehot(r, 1), r[:, 3:4], 0.0))
    sel = sel.astype(BF16)
    y = jnp.concatenate([jnp.dot(sel, half, preferred_element_type=F32)
                         for half in _unpack_bf16_pairs(ys_ref[...])], axis=-1)
    return x_ref[...] + mod_ref[5:6, :] * y


def _inproj_next_kernel(x_ref, ys_ref, r_ref, modp_ref, mod_ref, g_ref, w_ref, z_ref, xo_ref, wb_ref):
    x = _moe_residual(x_ref, ys_ref, r_ref, modp_ref)
    xo_ref[...] = x
    _inproj_body(x, mod_ref, g_ref, w_ref, wb_ref, z_ref)


def _tok_spec(cols):
    return pl.BlockSpec((TM, cols), lambda i: (i, 0))


def _mod_spec():
    return pl.BlockSpec((None, 6, D_MODEL), lambda i: (_cond_row(i), 0, 0))


def _full_spec(shape):
    return pl.BlockSpec(shape, lambda i: (0,) * len(shape))


def _ctx_lat_specs(cols):
    return [pl.BlockSpec((TM, cols), lambda i: (jnp.minimum(i, NB_CTX - 1), 0)),
            pl.BlockSpec((TM, cols), lambda i: (jnp.maximum(i - NB_CTX, 0), 0))]


def _inproj_first(x_ctx, x_lat, mod, g, w_in, layer):
    return pl.pallas_call(
        _inproj_first_kernel,
        grid=(NB_ALL,),
        in_specs=_ctx_lat_specs(D_MODEL) + [_mod_spec(), _full_spec((1, D_MODEL)),
                                            _layer_weight_spec(layer, D_MODEL, IN_COLS)],
        out_specs=[_tok_spec(IN_COLS), _tok_spec(D_MODEL)],
        out_shape=[jax.ShapeDtypeStruct((T_ALL, IN_COLS), BF16),
                   jax.ShapeDtypeStruct((T_ALL, D_MODEL), F32)],
        scratch_shapes=[pltpu.VMEM((D_MODEL, IN_COLS), BF16)],
        compiler_params=_cparams(("arbitrary",)),
        name="inproj_first",
    )(x_ctx, x_lat, mod, g, w_in)


def _inproj_next(x, ys, route, mod_prev, mod, g, w_in, layer):
    return pl.pallas_call(
        _inproj_next_kernel,
        grid=(NB_ALL,),
        in_specs=[_tok_spec(D_MODEL),
                  pl.BlockSpec((MOE_LC, D_HALF), lambda i: (i, 0)),
                  _tok_spec(ROUTE_COLS),
                  _mod_spec(), _mod_spec(), _full_spec((1, D_MODEL)),
                  _layer_weight_spec(layer, D_MODEL, IN_COLS)],
        out_specs=[_tok_spec(IN_COLS), _tok_spec(D_MODEL)],
        out_shape=[jax.ShapeDtypeStruct((T_ALL, IN_COLS), BF16),
                   jax.ShapeDtypeStruct((T_ALL, D_MODEL), F32)],
        scratch_shapes=[pltpu.VMEM((D_MODEL, IN_COLS), BF16)],
        compiler_params=_cparams(("arbitrary",)),
        name="inproj_next",
    )(x, ys, route, mod_prev, mod, g, w_in)


CONV_PAD = 16
CONV_CHUNK = 64


CONV_SPAN = CONV_CHUNK + 2 * CONV_PAD - SUBLANES


CONV_UNROLL = 4


def _conv_kernel(seq, z_ref, w_ref, b_ref, g_ref, be_ref, o_ref, upad_ref, shift_refs):
    zeros = jnp.zeros((CONV_PAD, CONV_CH), F32)
    upad_ref[0:CONV_PAD, :] = zeros
    upad_ref[seq + CONV_PAD:seq + 2 * CONV_PAD, :] = zeros

    def glu(ci, carry):
        base = pl.multiple_of(ci * 256, 256)
        zc = z_ref[pl.ds(base, 256), :].astype(F32)
        upad_ref[pl.ds(base + CONV_PAD, 256), :] = zc[:, :CONV_CH] * _sigmoid(zc[:, CONV_CH:])
        return carry

    lax.fori_loop(0, seq // 256, glu, 0)

    shift = CONV_PAD - CONV_K // 2

    def chunk(ci, shift_ref):
        base = pl.multiple_of(ci * CONV_CHUNK, CONV_CHUNK)
        win = upad_ref[pl.ds(base, CONV_CHUNK + 2 * CONV_PAD), :]
        acc = jnp.zeros((CONV_CHUNK, CONV_CH), F32)
        for sub in range(SUBLANES):
            shift_ref[sub] = win[sub:sub + CONV_SPAN, :]
            for k in range(CONV_K):
                if (k + shift) % SUBLANES == sub:
                    lo = k + shift - sub
                    acc = acc + w_ref[k:k + 1, :] * shift_ref[sub, lo:lo + CONV_CHUNK, :]
        acc = acc + b_ref[...]
        mu = jnp.mean(acc, axis=-1, keepdims=True)
        d = acc - mu
        var = jnp.mean(d * d, axis=-1, keepdims=True)
        n = d * lax.rsqrt(var + EPS) * g_ref[...] + be_ref[...]
        o_ref[pl.ds(base, CONV_CHUNK), :] = (n * _sigmoid(n)).astype(BF16)

    def chunks(cj, carry):
        for u in range(CONV_UNROLL):
            chunk(cj * CONV_UNROLL + u, shift_refs.at[u])
        return carry

    lax.fori_loop(0, seq // (CONV_CHUNK * CONV_UNROLL), chunks, 0)


def _conv(z, row_block0, nseq, seq, w, b, g, be):
    return pl.pallas_call(
        functools.partial(_conv_kernel, seq),
        grid=(nseq,),
        in_specs=[pl.BlockSpec((seq, 2 * CONV_CH), lambda s: (row_block0 + s, 0)),
                  _full_spec((CONV_K, CONV_CH)), _full_spec((1, CONV_CH)),
                  _full_spec((1, CONV_CH)), _full_spec((1, CONV_CH))],
        out_specs=pl.BlockSpec((seq, CONV_CH), lambda s: (s, 0)),
        out_shape=jax.ShapeDtypeStruct((nseq * seq, CONV_CH), BF16),
        scratch_shapes=[pltpu.VMEM((seq + 2 * CONV_PAD, CONV_CH), F32),
                        pltpu.VMEM((CONV_UNROLL, SUBLANES, CONV_SPAN, CONV_CH), F32)],
        compiler_params=_cparams(("arbitrary",)),
        name="conv_seq%d" % seq,
    )(z, w, b, g, be)


def _dot_nt(a, b):
    return lax.dot_general(a, b, (((1,), (1,)), ((), ())), preferred_element_type=F32)


NA_SCALE = NA_DIM ** -0.5
assert NA_SCALE == 2.0 ** round(np.log2(NA_SCALE)), "query pre-scaling assumes a power-of-two scale"


def _ctx_attn_kernel(layer, q_ref, k_ref, v_ref, *refs):
    if layer:
        _, _, o_ref, ko_ref, vo_ref = refs
    else:
        o_ref, ko_full, vo_full = refs
        ko_ref, vo_ref = ko_full.at[0], vo_full.at[0]
        for j in range(1, DEPTH):
            ko_full[j] = jnp.zeros(ko_full.shape[1:], F32)
            vo_full[j] = jnp.zeros(vo_full.shape[1:], F32)
    pair = 2 * NA_DIM
    left = lax.broadcasted_iota(jnp.int32, (SEQ, pair), 1) < NA_DIM
    outs = []
    for p in range(NA_HEADS // 2):
        lanes = slice(p * pair, (p + 1) * pair)
        qp, kp, vp = q_ref[:, lanes], k_ref[:, lanes], v_ref[:, lanes]
        kf, vf = kp.astype(F32), vp.astype(F32)
        o_h = []
        for hh in range(2):
            cols = slice(hh * NA_DIM, (hh + 1) * NA_DIM)
            ko_ref[2 * p + hh] = kf[:, cols]
            vo_ref[2 * p + hh] = vf[:, cols]
            qm = jnp.where(left == (hh == 0), qp, jnp.zeros_like(qp))
            s = _dot_nt(qm, kp) * NA_SCALE
            m = jnp.max(s, axis=-1, keepdims=True)
            e = jnp.exp(s - m)
            den = jnp.sum(e, axis=-1, keepdims=True)
            o_h.append(jnp.dot(e.astype(BF16), vp, preferred_element_type=F32) / den)
        outs.append(jnp.where(left, o_h[0], o_h[1]))
    o_ref[...] = jnp.concatenate(outs, axis=-1).astype(BF16)


def _ctx_attn(z, layer, k_prev=None, v_prev=None):
    qb, kb, vb = COL_NA_Q // NA_WIDTH, COL_NA_K // NA_WIDTH, COL_NA_V // NA_WIDTH
    head_shape = jax.ShapeDtypeStruct((BATCH, DEPTH, NA_HEADS, SEQ, NA_DIM), F32)
    head_spec = pl.BlockSpec((None, DEPTH, NA_HEADS, SEQ, NA_DIM), lambda b: (b, 0, 0, 0, 0))
    in_specs = [pl.BlockSpec((SEQ, NA_WIDTH), lambda b: (b, qb)),
                pl.BlockSpec((SEQ, NA_WIDTH), lambda b: (b, kb)),
                pl.BlockSpec((SEQ, NA_WIDTH), lambda b: (b, vb))]
    args = [z, z, z]
    aliases = {}
    if layer:
        any_spec = pl.BlockSpec(memory_space=pl.ANY)
        in_specs += [any_spec, any_spec]
        args += [k_prev, v_prev]
        aliases = {3: 1, 4: 2}
        head_spec = pl.BlockSpec((None, None, NA_HEADS, SEQ, NA_DIM), lambda b: (b, layer, 0, 0, 0))
    return pl.pallas_call(
        functools.partial(_ctx_attn_kernel, layer),
        grid=(BATCH,),
        in_specs=in_specs,
        out_specs=[pl.BlockSpec((SEQ, NA_WIDTH), lambda b: (b, 0)), head_spec, head_spec],
        out_shape=[jax.ShapeDtypeStruct((T_CTX, NA_WIDTH), BF16), head_shape, head_shape],
        input_output_aliases=aliases,
        compiler_params=_cparams(("arbitrary",)),
        name="ctx_attn",
    )(*args)


NA_KINDS = (0, NA_ROWS, GRID_H - NA_ROWS)
N_DR = 2 * NA_KH - 1
N_DC = 2 * NA_KW - 1


def _na_row_offset(r0, i, j):
    ks = min(max(r0 - NA_KH // 2, 0), GRID_H - NA_KROWS)
    r, kr = r0 + i, ks + j
    rs = min(max(r - NA_KH // 2, 0), GRID_H - NA_KH)
    return kr - r + NA_KH - 1 if rs <= kr < rs + NA_KH else None


def _na_bias_kernel(rpb_ref, o_ref):
    lh = pl.program_id(0)
    shape = (GRID_W, 2 * GRID_W)
    qc = lax.broadcasted_iota(jnp.int32, shape, 0)
    lane = lax.broadcasted_iota(jnp.int32, shape, 1)
    kc = lane % GRID_W
    dc = jnp.clip(kc - qc, -(NA_KW - 1), NA_KW - 1) + NA_KW - 1
    cs = jnp.clip(qc - NA_KW // 2, 0, GRID_W - NA_KW)
    col_ok = (kc >= cs) & (kc < cs + NA_KW)
    neg = jnp.full(shape, NEG_INF, F32)
    tiles = []
    for dr in range(N_DR):
        base = (lh * N_DR + dr) * N_DC
        val = jnp.zeros(shape, F32)
        for d in range(N_DC):
            val = jnp.where(dc == d, rpb_ref[base + d], val)
        tiles.append(jnp.where(col_ok, val, neg))
    left = lane < GRID_W
    for kind, r0 in enumerate(NA_KINDS):
        for i in range(NA_ROWS):
            for jp in range(NA_KROWS // 2):
                dl, dr_ = _na_row_offset(r0, i, 2 * jp), _na_row_offset(r0, i, 2 * jp + 1)
                tl = neg if dl is None else tiles[dl]
                tr = neg if dr_ is None else tiles[dr_]
                o_ref[kind, i * GRID_W:(i + 1) * GRID_W, jp * 2 * GRID_W:(jp + 1) * 2 * GRID_W] = (
                    jnp.where(left, tl, tr))


def _na_bias_tables(rpb):
    return pl.pallas_call(
        _na_bias_kernel,
        grid=(DEPTH * NA_HEADS,),
        in_specs=[pl.BlockSpec(memory_space=pltpu.SMEM)],
        out_specs=pl.BlockSpec((None, len(NA_KINDS), NA_Q, NA_KEYS), lambda i: (i, 0, 0, 0)),
        out_shape=jax.ShapeDtypeStruct((DEPTH * NA_HEADS, len(NA_KINDS), NA_Q, NA_KEYS), F32),
        compiler_params=_cparams(("arbitrary",)),
        name="nbr_bias",
    )(rpb.reshape(-1))


NA_G = 4


def _na_kernel(q_ref, k_ref, v_ref, kc_ref, vc_ref, bias_ref, o_ref):
    rb = pl.program_id(2)
    ks = jnp.clip(rb * NA_ROWS - NA_KH // 2, 0, GRID_H - NA_KROWS)
    start = pl.multiple_of(ks * GRID_W, GRID_W)
    q = q_ref[...] * NA_SCALE
    kl = k_ref[pl.ds(start, NA_KEYS), :]
    vl = v_ref[pl.ds(start, NA_KEYS), :]
    pair = 2 * NA_DIM
    left = lax.broadcasted_iota(jnp.int32, (NA_Q, pair), 1) < NA_DIM
    ones_loc = jnp.ones((NA_KEYS, pair), BF16)
    ones_ctx = jnp.ones((PAST_LEN, pair), BF16)
    kc, v_ext, vc_ext = [], [], []
    for p in range(NA_G // 2):
        lanes = slice(p * pair, (p + 1) * pair)
        kc.append(jnp.concatenate([kc_ref[2 * p].astype(BF16), kc_ref[2 * p + 1].astype(BF16)], axis=-1))
        vc = jnp.concatenate([vc_ref[2 * p].astype(BF16), vc_ref[2 * p + 1].astype(BF16)], axis=-1)
        v_ext.append(jnp.concatenate([vl[:, lanes], ones_loc], axis=-1))
        vc_ext.append(jnp.concatenate([vc, ones_ctx], axis=-1))

    def scores(hh):
        p = hh // 2
        lanes = slice(p * pair, (p + 1) * pair)
        qm = jnp.where(left == (hh % 2 == 0), q[:, lanes], jnp.zeros((NA_Q, pair), BF16))
        return _dot_nt(qm, kl[:, lanes]) + bias_ref[hh], _dot_nt(qm, kc[p])

    outs = []
    nxt = scores(0)
    for hh in range(NA_G):
        s_loc, s_ctx = nxt
        if hh + 1 < NA_G:
            nxt = scores(hh + 1)
        m = jnp.maximum(jnp.max(s_loc, axis=-1, keepdims=True), jnp.max(s_ctx, axis=-1, keepdims=True))
        p_loc = jnp.exp(s_loc - m).astype(BF16)
        p_ctx = jnp.exp(s_ctx - m).astype(BF16)
        o = (jnp.dot(p_loc, v_ext[hh // 2], preferred_element_type=F32)
             + jnp.dot(p_ctx, vc_ext[hh // 2], preferred_element_type=F32))
        outs.append(o[:, :pair] / o[:, pair:])
    o_ref[...] = jnp.concatenate([jnp.where(left, outs[2 * p], outs[2 * p + 1]) for p in range(NA_G // 2)],
                                 axis=-1).astype(BF16)


def _na_attn(z, cache_k, cache_v, bias, layer):
    lat_q0 = T_CTX // NA_Q
    lat_s0 = T_CTX // DEC_SEQ
    width = NA_G * NA_DIM
    qc, kc, vc = COL_NA_Q // width, COL_NA_K // width, COL_NA_V // width
    groups = NA_HEADS // NA_G

    def kind(rb):
        return jnp.where(rb == 0, 0, jnp.where(rb == NA_RB - 1, 2, 1))

    ctx_spec = pl.BlockSpec((None, None, NA_G, PAST_LEN, NA_DIM), lambda b, hg, rb: (b, layer, hg, 0, 0))
    return pl.pallas_call(
        _na_kernel,
        grid=(DEC_BATCH, groups, NA_RB),
        in_specs=[pl.BlockSpec((NA_Q, width), lambda b, hg, rb: (lat_q0 + b * NA_RB + rb, qc + hg)),
                  pl.BlockSpec((DEC_SEQ, width), lambda b, hg, rb: (lat_s0 + b, kc + hg)),
                  pl.BlockSpec((DEC_SEQ, width), lambda b, hg, rb: (lat_s0 + b, vc + hg)),
                  ctx_spec, ctx_spec,
                  pl.BlockSpec((NA_G, None, NA_Q, NA_KEYS),
                               lambda b, hg, rb: (layer * groups + hg, kind(rb), 0, 0))],
        out_specs=pl.BlockSpec((NA_Q, width), lambda b, hg, rb: (b * NA_RB + rb, hg)),
        out_shape=jax.ShapeDtypeStruct((T_LAT, NA_WIDTH), BF16),
        compiler_params=_cparams(("arbitrary", "arbitrary", "arbitrary")),
        name="nbr_attn",
    )(z, z, z, cache_k, cache_v, bias)


RET_PAIR = 2 * RET_DIM
RET_NPAIR = RET_HEADS // 2
assert RET_PAIR == LANES and RET_CHUNK == LANES
RET_UNROLL = 8


def _rope_tables():
    n_freq = RET_DIM // 4
    t = np.arange(DEC_SEQ)
    inv = jnp.asarray(ROPE_BASE, F32) ** (-jnp.arange(n_freq, dtype=F32) / n_freq)
    ang_r = jnp.asarray(t // GRID_W, F32)[:, None] * inv[None, :]
    ang_c = jnp.asarray(t % GRID_W, F32)[:, None] * inv[None, :]
    cos = jnp.concatenate([jnp.cos(ang_r)] * 2 + [jnp.cos(ang_c)] * 2, axis=-1)
    sin = jnp.concatenate([-jnp.sin(ang_r), jnp.sin(ang_r), -jnp.sin(ang_c), jnp.sin(ang_c)], axis=-1)
    lane = np.arange(RET_WIDTH)
    src = np.where(lane % (2 * n_freq) < n_freq, lane + n_freq, lane - n_freq)
    swap = np.zeros((RET_WIDTH, RET_WIDTH), np.float32)
    swap[src, lane] = 1.0
    return jnp.tile(cos, (1, RET_HEADS)), jnp.tile(sin, (1, RET_HEADS)), jnp.asarray(swap, BF16)


def _ret_kernel(seq, latent, *refs):
    if latent:
        (lg_ref, z_ref, gn_ref, cos_ref, sin_ref, swap_ref, s0f_ref, s0b_ref, y_ref,
         q_s, k_s, kv_s, st_s) = refs
    else:
        lg_ref, z_ref, gn_ref, y_ref, sf_ref, sb_ref, q_s, k_s, kv_s, st_s = refs
    nc = seq // RET_CHUNK
    ch, hd, pw = RET_CHUNK, RET_DIM, RET_PAIR

    row = lax.broadcasted_iota(jnp.int32, (ch, ch), 0).astype(F32)
    col = lax.broadcasted_iota(jnp.int32, (ch, ch), 1).astype(F32)
    pos = lax.broadcasted_iota(jnp.int32, (ch, pw), 0).astype(F32)
    left = lax.broadcasted_iota(jnp.int32, (ch, pw), 1) < hd
    top = lax.broadcasted_iota(jnp.int32, (pw, pw), 0) < hd
    same_head = top == (lax.broadcasted_iota(jnp.int32, (pw, pw), 1) < hd)
    same_head2 = jnp.concatenate([same_head, same_head], axis=0)

    def per_head(mask, fn, p):
        return jnp.where(mask, fn(2 * p), fn(2 * p + 1))

    decay = []
    for h in range(RET_HEADS):
        lf, lb = lg_ref[0, h], lg_ref[1, h]
        d_f = jnp.where(row >= col, jnp.exp(jnp.maximum(row - col, 0.0) * lf), 0.0)
        d_b = jnp.where(col >= row, jnp.exp(jnp.maximum(col - row, 0.0) * lb), 0.0)
        decay.append(d_f + d_b)
    q_dec, k_dec, c_dec_f, c_dec_b = [], [], [], []
    for p in range(RET_NPAIR):
        q_dec.append(jnp.concatenate(
            [per_head(left, lambda h: jnp.exp((pos + 1.0) * lg_ref[0, h]), p),
             per_head(left, lambda h: jnp.exp((ch - pos) * lg_ref[1, h]), p)], axis=-1))
        k_dec.append(jnp.concatenate(
            [per_head(left, lambda h: jnp.exp((ch - 1.0 - pos) * lg_ref[0, h]), p),
             per_head(left, lambda h: jnp.exp(pos * lg_ref[1, h]), p)], axis=-1))
        zero = jnp.zeros((pw, pw), F32)
        c_dec_f.append(per_head(top, lambda h: jnp.exp(zero + ch * lg_ref[0, h]), p))
        c_dec_b.append(per_head(top, lambda h: jnp.exp(zero + ch * lg_ref[1, h]), p))

    def rope(x, base):
        xf = x.astype(F32)
        if not latent:
            return xf
        swapped = jnp.dot(x, swap_ref[...], preferred_element_type=F32)
        return xf * cos_ref[pl.ds(base, ch), :] + swapped * sin_ref[pl.ds(base, ch), :]

    def pass1(n, carry):
        base = pl.multiple_of(n * ch, ch)
        zc = z_ref[pl.ds(base, ch), :]
        q = rope(zc[:, 0:RET_WIDTH], base)
        k = rope(zc[:, RET_WIDTH:2 * RET_WIDTH], base) * (RET_DIM ** -0.5)
        q_s[pl.ds(base, ch), :] = q.astype(BF16)
        k_s[pl.ds(base, ch), :] = k.astype(BF16)
        v = zc[:, 2 * RET_WIDTH:3 * RET_WIDTH]
        for p in range(RET_NPAIR):
            lanes = slice(p * pw, (p + 1) * pw)
            kp = k[:, lanes]
            k2 = (jnp.concatenate([kp, kp], axis=-1) * k_dec[p]).astype(BF16)
            kv = lax.dot_general(k2, v[:, lanes], (((0,), (0,)), ((), ())), preferred_element_type=F32)
            kv_s[n, p] = jnp.where(same_head2, kv, 0.0)
        return carry

    lax.fori_loop(0, nc, pass1, 0, unroll=min(RET_UNROLL, nc))

    def block_diag(a, b):
        z = jnp.zeros((hd, hd), F32)
        return jnp.concatenate([jnp.concatenate([a, z], axis=1), jnp.concatenate([z, b], axis=1)], axis=0)

    for p in range(RET_NPAIR):
        if latent:
            s_f = block_diag(s0f_ref[2 * p], s0f_ref[2 * p + 1])
            s_b = block_diag(s0b_ref[2 * p], s0b_ref[2 * p + 1])
        else:
            s_f = s_b = jnp.zeros((pw, pw), F32)

        def fwd(n, s, p=p):
            st_s[n, p, 0:pw, :] = s.astype(BF16)
            return c_dec_f[p] * s + kv_s[n, p, 0:pw, :]

        def bwd(i, s, p=p):
            n = nc - 1 - i
            st_s[n, p, pw:2 * pw, :] = s.astype(BF16)
            return c_dec_b[p] * s + kv_s[n, p, pw:2 * pw, :]

        s_f = lax.fori_loop(0, nc, fwd, s_f)
        s_b = lax.fori_loop(0, nc, bwd, s_b)
        if not latent:
            for hh in range(2):
                blk = slice(hh * hd, (hh + 1) * hd)
                sf_ref[2 * p + hh] = s_f[blk, blk]
                sb_ref[2 * p + hh] = s_b[blk, blk]

    def pass3(n, carry):
        base = pl.multiple_of(n * ch, ch)
        zc = z_ref[pl.ds(base, ch), :]
        q = q_s[pl.ds(base, ch), :]
        k = k_s[pl.ds(base, ch), :]
        v = zc[:, 2 * RET_WIDTH:3 * RET_WIDTH]
        gate = zc[:, 3 * RET_WIDTH:4 * RET_WIDTH].astype(F32)
        outs = []
        for p in range(RET_NPAIR):
            lanes = slice(p * pw, (p + 1) * pw)
            qp, kp, vp = q[:, lanes], k[:, lanes], v[:, lanes]
            o_h = []
            for hh in range(2):
                qm = jnp.where(left == (hh == 0), qp, jnp.zeros_like(qp))
                s = _dot_nt(qm, kp) * decay[2 * p + hh]
                o_h.append(jnp.dot(s.astype(BF16), vp, preferred_element_type=F32))
            qf = qp.astype(F32)
            q2 = (jnp.concatenate([qf, qf], axis=-1) * q_dec[p]).astype(BF16)
            o = jnp.where(left, o_h[0], o_h[1]) + jnp.dot(q2, st_s[n, p], preferred_element_type=F32)

            def half_mean(t):
                s_l = jnp.sum(jnp.where(left, t, 0.0), axis=-1, keepdims=True)
                s_r = jnp.sum(jnp.where(left, 0.0, t), axis=-1, keepdims=True)
                return jnp.where(left, s_l, s_r) * (1.0 / hd)

            d = o - half_mean(o)
            outs.append(d * lax.rsqrt(half_mean(d * d) + EPS))
        nrm = jnp.concatenate(outs, axis=-1)
        y_ref[pl.ds(base, ch), :] = (nrm * gn_ref[...] * (gate * _sigmoid(gate))).astype(BF16)
        return carry

    lax.fori_loop(0, nc, pass3, 0, unroll=min(RET_UNROLL, nc))


def _retention(z, lg, gn_g, latent, layer=None, rope=None, s0_f=None, s0_b=None):
    seq = DEC_SEQ if latent else SEQ
    nseq = DEC_BATCH if latent else BATCH
    nc = seq // RET_CHUNK
    row0 = (T_CTX // DEC_SEQ) if latent else 0
    cb = COL_RET // (4 * RET_WIDTH)
    in_specs = [pl.BlockSpec(memory_space=pltpu.SMEM),
                pl.BlockSpec((seq, 4 * RET_WIDTH), lambda s: (row0 + s, cb)),
                _full_spec((1, RET_WIDTH))]
    args = [lg, z, gn_g]
    state_shape = jax.ShapeDtypeStruct((nseq, RET_HEADS, RET_DIM, RET_DIM), F32)
    y_spec = pl.BlockSpec((seq, RET_WIDTH), lambda s: (s, 0))
    y_shape = jax.ShapeDtypeStruct((nseq * seq, RET_WIDTH), BF16)
    if latent:
        st_spec = pl.BlockSpec((None, None, RET_HEADS, RET_DIM, RET_DIM), lambda s: (s, layer, 0, 0, 0))

        def const_spec(shape):
            return pl.BlockSpec(shape, lambda s: (0,) * len(shape), pipeline_mode=pl.Buffered(1))

        in_specs += [const_spec((seq, RET_WIDTH)), const_spec((seq, RET_WIDTH)),
                     const_spec((RET_WIDTH, RET_WIDTH)), st_spec, st_spec]
        args += [rope[0], rope[1], rope[2], s0_f, s0_b]
        out_specs, out_shape = y_spec, y_shape
    else:
        so_spec = pl.BlockSpec((None, RET_HEADS, RET_DIM, RET_DIM), lambda s: (s, 0, 0, 0))
        out_specs, out_shape = [y_spec, so_spec, so_spec], [y_shape, state_shape, state_shape]
    return pl.pallas_call(
        functools.partial(_ret_kernel, seq, latent),
        grid=(nseq,),
        in_specs=in_specs,
        out_specs=out_specs,
        out_shape=out_shape,
        scratch_shapes=[pltpu.VMEM((seq, RET_WIDTH), BF16), pltpu.VMEM((seq, RET_WIDTH), BF16),
                        pltpu.VMEM((nc, RET_NPAIR, 2 * RET_PAIR, RET_PAIR), F32),
                        pltpu.VMEM((nc, RET_NPAIR, 2 * RET_PAIR, RET_PAIR), BF16)],
        compiler_params=_cparams(("arbitrary",)),
        name="retention_lat" if latent else "retention_ctx",
    )(*args)


def _route(logits):
    lane = lax.broadcasted_iota(jnp.int32, logits.shape, 1)
    lane_f = lane.astype(F32)
    big = float(ROUTE_COLS)
    neg = -jnp.inf
    is_grp = lane < N_GROUPS
    gl = jnp.where(is_grp, logits, neg)
    gmax = jnp.max(gl, axis=-1, keepdims=True)
    grp = jnp.min(jnp.where(gl == gmax, lane_f, big), axis=-1, keepdims=True)
    p_grp = 1.0 / jnp.sum(jnp.exp(gl - gmax), axis=-1, keepdims=True)
    e_f = lane_f - N_GROUPS
    lo = grp * EXPERTS_PER_GROUP
    in_grp = (e_f >= lo) & (e_f < lo + EXPERTS_PER_GROUP)
    el = jnp.where(in_grp, logits, neg)
    m1 = jnp.max(el, axis=-1, keepdims=True)
    i1 = jnp.min(jnp.where(el == m1, lane_f, big), axis=-1, keepdims=True)
    el2 = jnp.where(lane_f == i1, neg, el)
    m2 = jnp.max(el2, axis=-1, keepdims=True)
    i2 = jnp.min(jnp.where(el2 == m2, lane_f, big), axis=-1, keepdims=True)
    t = jnp.exp(m2 - m1)
    g1 = p_grp / (1.0 + t)
    g2 = p_grp * t / (1.0 + t)
    rows = logits.shape[0]
    oh1, oh2 = lane_f == i1, lane_f == i2
    oh = jnp.where(oh1 | oh2, 1.0, 0.0)
    tri = (lax.broadcasted_iota(jnp.int32, (rows, rows), 0)
           > lax.broadcasted_iota(jnp.int32, (rows, rows), 1))
    rank = jnp.dot(jnp.where(tri, 1.0, 0.0).astype(BF16), oh.astype(BF16), preferred_element_type=F32)
    tiles = jnp.floor((jnp.sum(oh, axis=0, keepdims=True) + (SUBLANES - 1)) * (1.0 / SUBLANES))
    upper = (lax.broadcasted_iota(jnp.int32, (ROUTE_COLS, ROUTE_COLS), 0)
             < lax.broadcasted_iota(jnp.int32, (ROUTE_COLS, ROUTE_COLS), 1))
    start = SUBLANES * jnp.dot(jnp.broadcast_to(tiles, (SUBLANES, ROUTE_COLS)).astype(BF16),
                               jnp.where(upper, 1.0, 0.0).astype(BF16),
                               preferred_element_type=F32)[0:1, :]
    pos = start + rank
    p1 = jnp.sum(jnp.where(oh1, pos, 0.0), axis=-1, keepdims=True)
    p2 = jnp.sum(jnp.where(oh2, pos, 0.0), axis=-1, keepdims=True)
    out = jnp.zeros(logits.shape, F32)
    for k, val in enumerate((i1 - N_GROUPS, i2 - N_GROUPS, g1, g2, p1, p2)):
        out = jnp.where(lane == k, val, out)
    return out, SUBLANES * tiles


def _outproj_kernel(ycc, ycl, ync, ynl, yrc, yrl, x_ref, mod_ref, g_ref, w_ref, wr_ref, br_ref,
                    xo_ref, xs_ref, r_ref, seg_ref, wb_ref):
    _cast_weights_once(w_ref, wb_ref)
    is_ctx = pl.program_id(0) < NB_CTX
    yc = jnp.where(is_ctx, ycc[...], ycl[...])
    yn = jnp.where(is_ctx, ync[...], ynl[...])
    yr = jnp.where(is_ctx, yrc[...], yrl[...])
    y = (jnp.dot(yc, wb_ref[0:CONV_CH, :], preferred_element_type=F32)
         + jnp.dot(yn, wb_ref[CONV_CH:CONV_CH + NA_WIDTH, :], preferred_element_type=F32)
         + jnp.dot(yr, wb_ref[CONV_CH + NA_WIDTH:, :], preferred_element_type=F32))
    x = x_ref[...] + mod_ref[2:3, :] * y
    xo_ref[...] = x
    h = _norm_mod(x, g_ref[...], mod_ref[3:4, :], mod_ref[4:5, :])
    h_hi = h.astype(BF16)
    h_lo = (h - h_hi.astype(F32)).astype(BF16)
    hw = jnp.dot(h_hi, wr_ref[...], preferred_element_type=F32)
    logits = (hw[:, :ROUTE_COLS] + hw[:, ROUTE_COLS:]
              + jnp.dot(h_lo, wr_ref[:, :ROUTE_COLS], preferred_element_type=F32) + br_ref[...])
    route, seg = _route(logits)
    r_ref[...] = route
    seg_ref[...] = jnp.broadcast_to(seg, seg_ref.shape)
    sel = _slot_onehot(route, 0) | _slot_onehot(route, 1)
    xs_ref[...] = _pack_bf16_pairs(lax.dot_general(jnp.where(sel, 1.0, 0.0).astype(BF16), h_hi,
                                                   (((0,), (0,)), ((), ())), preferred_element_type=F32))


def _outproj(y_conv, y_na, y_ret, x, mod, g, w_out, layer, w_route, b_route):
    return pl.pallas_call(
        _outproj_kernel,
        grid=(NB_ALL,),
        in_specs=(_ctx_lat_specs(CONV_CH) + _ctx_lat_specs(NA_WIDTH) + _ctx_lat_specs(RET_WIDTH)
                  + [_tok_spec(D_MODEL), _mod_spec(), _full_spec((1, D_MODEL)),
                     _layer_weight_spec(layer, D_MODEL, D_MODEL), _full_spec((D_MODEL, 2 * ROUTE_COLS)),
                     _full_spec((1, ROUTE_COLS))]),
        out_specs=[_tok_spec(D_MODEL), pl.BlockSpec((MOE_LC, D_HALF), lambda i: (i, 0)),
                   _tok_spec(ROUTE_COLS), pl.BlockSpec((None, SUBLANES, ROUTE_COLS), lambda i: (i, 0, 0))],
        out_shape=[jax.ShapeDtypeStruct((T_ALL, D_MODEL), F32),
                   jax.ShapeDtypeStruct((NB_ALL * MOE_LC, D_HALF), U32),
                   jax.ShapeDtypeStruct((T_ALL, ROUTE_COLS), F32),
                   jax.ShapeDtypeStruct((NB_ALL, SUBLANES, ROUTE_COLS), F32)],
        scratch_shapes=[pltpu.VMEM((D_MODEL, D_MODEL), BF16)],
        compiler_params=_cparams(("arbitrary",)),
        name="outproj_route",
    )(y_conv[0], y_conv[1], y_na[0], y_na[1], y_ret[0], y_ret[1], x, mod, g, w_out, w_route, b_route)


def _dispatch_tables(seg):
    seg_len = seg[:, 0, N_GROUPS:N_GROUPS + N_EXPERTS].astype(jnp.int32)
    experts = jnp.arange(N_EXPERTS, dtype=jnp.int32)
    in_chunk = jnp.cumsum(seg_len, axis=1) - seg_len
    seg_row = in_chunk + MOE_LC * jnp.arange(N_CHUNK, dtype=jnp.int32)[:, None]
    seg_off = jnp.cumsum(seg_len, axis=0) - seg_len
    rows_e = jnp.sum(seg_len, axis=0)
    chunk_rows = jnp.sum(seg_len, axis=1)
    nblk = (rows_e + MOE_BLK - 1) // MOE_BLK
    blk_end = jnp.cumsum(nblk)
    blk_start = blk_end - nblk
    blk = jnp.arange(MOE_NBLK, dtype=jnp.int32)
    n_active = blk_end[-1]
    blk_e = jnp.minimum(jnp.sum((blk_end[None, :] <= jnp.minimum(blk, n_active - 1)[:, None]).astype(jnp.int32),
                                axis=-1), N_EXPERTS - 1)
    mine = blk_e[:, None] == experts[None, :]
    blk_lo = (blk - jnp.sum(jnp.where(mine, blk_start[None, :], 0), axis=-1)) * MOE_BLK
    left = jnp.sum(jnp.where(mine, rows_e[None, :], 0), axis=-1) - blk_lo
    blk_nv = jnp.where(blk < n_active, jnp.clip(left, 0, MOE_BLK), 0).astype(jnp.int32)
    off_b = jnp.sum(jnp.where(mine[:, None, :], seg_off[None, :, :], 0), axis=-1)
    end_b = off_b + jnp.sum(jnp.where(mine[:, None, :], seg_len[None, :, :], 0), axis=-1)
    blk_c0 = jnp.sum((end_b <= blk_lo[:, None]).astype(jnp.int32), axis=-1)
    blk_c1 = jnp.sum((off_b < (blk_lo + blk_nv)[:, None]).astype(jnp.int32), axis=-1)
    after = jnp.sum(jnp.where(mine, blk_end[None, :], 0), axis=-1)
    blk_next_e = jnp.where(after < n_active, jnp.take(blk_e, jnp.minimum(after, MOE_NBLK - 1)), -1)
    row_b = jnp.sum(jnp.where(mine[:, None, :], seg_row[None, :, :], 0), axis=-1)
    first = jnp.maximum(off_b, blk_lo[:, None])
    piece_n = jnp.minimum(end_b, (blk_lo + blk_nv)[:, None]) - first
    piece_src = row_b + first - off_b
    piece_dst = first - blk_lo[:, None]
    return (blk_e, blk_next_e.astype(jnp.int32), blk_nv, blk_c0, blk_c1, piece_src.reshape(-1),
            piece_dst.reshape(-1), piece_n.reshape(-1), chunk_rows)


def _moe_kernel(layer, blk_e, blk_next_e, blk_nv, blk_c0, blk_c1, piece_src, piece_dst, piece_n,
                chunk_rows, xs_hbm, w1_hbm, w3_hbm, w2_hbm, ys_hbm, xbuf, obuf, zeros,
                w1f, w3f, w2f, w1b, w3b, w2b, gsem, ssem, zsem, wsem):
    i = pl.program_id(0)
    last = pl.num_programs(0) - 1
    slot = i % 2

    def tiles(v):
        return pl.multiple_of(v, SUBLANES)

    def for_segments(blk, fn):
        def body(c, carry):
            k = blk * N_CHUNK + c
            n = piece_n[k]

            @pl.when(n > 0)
            def _():
                fn(tiles(piece_src[k]), tiles(piece_dst[k]), tiles(n))

            return carry

        lax.fori_loop(blk_c0[blk], blk_c1[blk], body, 0)

    def weight_copies(e):
        return [pltpu.make_async_copy(src.at[layer, e], dst, wsem)
                for src, dst in ((w1_hbm, w1f), (w3_hbm, w3f), (w2_hbm, w2f))]

    def start_gathers(blk, s):
        for_segments(blk, lambda src, dst, n: pltpu.make_async_copy(
            xs_hbm.at[pl.ds(src, n)], xbuf.at[s, pl.ds(dst, n)], gsem.at[s]).start())

    def start_scatters(blk, s):
        for_segments(blk, lambda dst, src, n: pltpu.make_async_copy(
            obuf.at[s, pl.ds(src, n)], ys_hbm.at[pl.ds(dst, n)], ssem.at[s]).start())

    def wait_rows(blk, s, sem):
        n = tiles(blk_nv[blk])

        @pl.when(n > 0)
        def _():
            pltpu.make_async_copy(xs_hbm.at[pl.ds(0, n)], xbuf.at[s, pl.ds(0, n)], sem.at[s]).wait()

    @pl.when(i == 0)
    def _():
        xbuf[...] = jnp.zeros_like(xbuf)
        zeros[...] = jnp.zeros_like(zeros)

        def tail(c):
            n = tiles(MOE_LC - chunk_rows[c])
            return n, pltpu.make_async_copy(zeros.at[pl.ds(0, n)],
                                            ys_hbm.at[pl.ds(tiles(c * MOE_LC + chunk_rows[c]), n)], zsem)

        def fill(c, carry):
            n, copy = tail(c)
            pl.when(n > 0)(copy.start)
            return carry

        def drain(c, carry):
            n, copy = tail(c)
            pl.when(n > 0)(copy.wait)
            return carry

        lax.fori_loop(0, N_CHUNK, fill, 0)
        lax.fori_loop(0, N_CHUNK, drain, 0)
        start_gathers(0, 0)
        for copy in weight_copies(blk_e[0]):
            copy.start()

    @pl.when(i < last)
    def _():
        start_gathers(i + 1, 1 - slot)

    @pl.when(i >= 2)
    def _():
        wait_rows(i - 2, slot, ssem)

    @pl.when(blk_nv[i] > 0)
    def _():
        @pl.when((i == 0) | (blk_e[i] != blk_e[jnp.maximum(i - 1, 0)]))
        def _():
            for copy in weight_copies(blk_e[i]):
                copy.wait()
            w1b[...] = w1f[...].astype(BF16)
            w3b[...] = w3f[...].astype(BF16)
            w2b[...] = w2f[...].astype(BF16)

            @pl.when(blk_next_e[i] >= 0)
            def _():
                for copy in weight_copies(blk_next_e[i]):
                    copy.start()

        wait_rows(i, slot, gsem)

        def expert_mlp(rows):
            x_lo, x_hi = _unpack_bf16_pairs(xbuf[slot, 0:rows, :])
            n_hid = D_EXPERT // MXU_TILE

            def in_dot(w, t):
                cols = slice(t * MXU_TILE, (t + 1) * MXU_TILE)
                return (jnp.dot(x_lo, w[:D_HALF, cols], preferred_element_type=F32)
                        + jnp.dot(x_hi, w[D_HALF:, cols], preferred_element_type=F32))

            ab = [(in_dot(w1b, t), in_dot(w3b, t)) for t in range(n_hid)]
            mid = [(a * _sigmoid(a) * b).astype(BF16) for a, b in ab]

            def out_dot(t):
                cols = slice(t * MXU_TILE, (t + 1) * MXU_TILE)
                return sum(jnp.dot(mid[j], w2b[j * MXU_TILE:(j + 1) * MXU_TILE, cols],
                                   preferred_element_type=F32) for j in range(n_hid))

            n_word = D_HALF // MXU_TILE
            for t in range(n_word):
                obuf[slot, 0:rows, t * MXU_TILE:(t + 1) * MXU_TILE] = _pack_words(
                    out_dot(t), out_dot(t + n_word))

        for rows in range(MOE_ROW_STEP, MOE_BLK + 1, MOE_ROW_STEP):
            @pl.when((blk_nv[i] > rows - MOE_ROW_STEP) & (blk_nv[i] <= rows))
            def _(rows=rows):
                expert_mlp(rows)

        start_scatters(i, slot)

    @pl.when(i == last)
    def _():
        wait_rows(i - 1, 1 - slot, ssem)
        wait_rows(i, slot, ssem)


def _moe(xs, w1, w3, w2, layer, blk_e, blk_next_e, blk_nv, blk_c0, blk_c1, piece_src, piece_dst,
         piece_n, chunk_rows):
    any_spec = pl.BlockSpec(memory_space=pl.ANY)
    grid_spec = pltpu.PrefetchScalarGridSpec(
        num_scalar_prefetch=9,
        grid=(MOE_NBLK,),
        in_specs=[any_spec, any_spec, any_spec, any_spec],
        out_specs=any_spec,
        scratch_shapes=[pltpu.VMEM((2, MOE_BLK, D_HALF), U32), pltpu.VMEM((2, MOE_BLK, D_HALF), U32),
                        pltpu.VMEM((MOE_LC - 2 * TM, D_HALF), U32),
                        pltpu.VMEM((D_MODEL, D_EXPERT), F32), pltpu.VMEM((D_MODEL, D_EXPERT), F32),
                        pltpu.VMEM((D_EXPERT, D_MODEL), F32),
                        pltpu.VMEM((D_MODEL, D_EXPERT), BF16), pltpu.VMEM((D_MODEL, D_EXPERT), BF16),
                        pltpu.VMEM((D_EXPERT, D_MODEL), BF16),
                        pltpu.SemaphoreType.DMA((2,)), pltpu.SemaphoreType.DMA((2,)),
                        pltpu.SemaphoreType.DMA, pltpu.SemaphoreType.DMA])
    return pl.pallas_call(
        functools.partial(_moe_kernel, layer),
        grid_spec=grid_spec,
        out_shape=jax.ShapeDtypeStruct((NB_ALL * MOE_LC, D_HALF), U32),
        compiler_params=_cparams(("arbitrary",)),
        name="moe_experts",
    )(blk_e, blk_next_e, blk_nv, blk_c0, blk_c1, piece_src, piece_dst, piece_n, chunk_rows,
      xs, w1, w3, w2)


def _final_kernel(x_ref, ys_ref, r_ref, mod_ref, g_ref, o_ref):
    x = _moe_residual(x_ref, ys_ref, r_ref, mod_ref)
    ms = jnp.mean(x * x, axis=-1, keepdims=True)
    o_ref[...] = x * lax.rsqrt(ms + EPS) * g_ref[...]


def _final(x, ys, route, mod, g, block0, nblocks):
    return pl.pallas_call(
        _final_kernel,
        grid=(nblocks,),
        in_specs=[pl.BlockSpec((TM, D_MODEL), lambda i: (block0 + i, 0)),
                  pl.BlockSpec((MOE_LC, D_HALF), lambda i: (block0 + i, 0)),
                  pl.BlockSpec((TM, ROUTE_COLS), lambda i: (block0 + i, 0)),
                  pl.BlockSpec((None, 6, D_MODEL), lambda i: (_cond_row(block0 + i), 0, 0)),
                  _full_spec((1, D_MODEL))],
        out_specs=_tok_spec(D_MODEL),
        out_shape=jax.ShapeDtypeStruct((nblocks * TM, D_MODEL), F32),
        compiler_params=_cparams(("arbitrary",)),
        name="final_norm",
    )(x, ys, route, mod, g)


def kernel(x_prompt, x_sample, c, cache_k, cache_v, state_ret_f, state_ret_b, c_ctx, w_ada, b_ada, norm1_g, norm2_g, w_in, w_out, conv_w, conv_b, conv_ln_g, conv_ln_b, na_rpb, ret_lg_f, ret_lg_b, ret_gn_g, w_route_g, b_route_g, w_route_e, b_route_e, w1, w3, w2, final_g):
    cv = jnp.zeros((COND_ROWS, D_MODEL), F32).at[0].set(c_ctx).at[1:N_COND].set(c)
    mods = _ada(cv, w_ada, b_ada).reshape(DEPTH, COND_ROWS, 6, D_MODEL)
    pad = ROUTE_COLS - N_GROUPS - N_EXPERTS
    w_route = jnp.pad(jnp.concatenate([w_route_g, w_route_e], axis=-1), ((0, 0), (0, 0), (0, pad)))
    b_route = jnp.pad(jnp.concatenate([b_route_g, b_route_e], axis=-1), ((0, 0), (0, pad)))
    w_route_hi = w_route.astype(BF16)
    w_route_lo = (w_route - w_route_hi.astype(F32)).astype(BF16)
    w_route = jnp.concatenate([w_route_hi, w_route_lo], axis=-1)
    na_bias = _na_bias_tables(na_rpb)
    rope = _rope_tables()
    lg = jnp.stack([ret_lg_f, ret_lg_b], axis=1)

    x_ctx = x_prompt.reshape(T_CTX, D_MODEL)
    x_lat = x_sample.reshape(T_LAT, D_MODEL)
    x = y = route = new_k = new_v = None
    sf_list, sb_list = [], []
    for l in range(DEPTH):
        g1 = norm1_g[l].reshape(1, D_MODEL)
        if l == 0:
            z, x = _inproj_first(x_ctx, x_lat, mods[l], g1, w_in, l)
        else:
            z, x = _inproj_next(x, y, route, mods[l - 1], mods[l], g1, w_in, l)
        conv_args = (conv_w[l], conv_b[l].reshape(1, -1), conv_ln_g[l].reshape(1, -1),
                     conv_ln_b[l].reshape(1, -1))
        yc_c = _conv(z, 0, BATCH, SEQ, *conv_args)
        yc_l = _conv(z, T_CTX // DEC_SEQ, DEC_BATCH, DEC_SEQ, *conv_args)
        yn_c, new_k, new_v = _ctx_attn(z, l, new_k, new_v)
        yn_l = _na_attn(z, cache_k, cache_v, na_bias, l)
        gn = ret_gn_g[l].reshape(1, RET_WIDTH)
        yr_c, sf_l, sb_l = _retention(z, lg[l], gn, latent=False)
        yr_l = _retention(z, lg[l], gn, latent=True, layer=l, rope=rope,
                          s0_f=state_ret_f, s0_b=state_ret_b)
        x, xs, route, seg = _outproj((yc_c, yc_l), (yn_c, yn_l), (yr_c, yr_l), x, mods[l],
                                     norm2_g[l].reshape(1, D_MODEL), w_out, l, w_route[l],
                                     b_route[l].reshape(1, ROUTE_COLS))
        y = _moe(xs, w1, w3, w2, l, *_dispatch_tables(seg))
        sf_list.append(sf_l)
        sb_list.append(sb_l)
    fg = final_g.reshape(1, D_MODEL)
    y_prompt = _final(x, y, route, mods[DEPTH - 1], fg, 0, NB_CTX).reshape(BATCH, SEQ, D_MODEL)
    y_sample = _final(x, y, route, mods[DEPTH - 1], fg, NB_CTX, NB_LAT).reshape(DEC_BATCH, DEC_SEQ, D_MODEL)
    return (y_prompt, y_sample, new_k, new_v, jnp.stack(sf_list, axis=1), jnp.stack(sb_list, axis=1))
```

```python
import functools

import numpy as np
import jax
import jax.numpy as jnp
from jax import lax
from jax.experimental import pallas as pl
from jax.experimental.pallas import tpu as pltpu

D_MODEL = 1024
BATCH = 32
SEQ = 256
DEPTH = 2
DEC_BATCH = 4
DEC_SEQ = 4096
PAST_LEN = 512
GRID_W = 64
GRID_H = DEC_SEQ // GRID_W
CONV_CH = 256
CONV_K = 31
NA_HEADS = 8
NA_DIM = 64
NA_WIDTH = NA_HEADS * NA_DIM
NA_KH = 8
NA_KW = 16
RET_HEADS = 4
RET_DIM = 64
RET_WIDTH = RET_HEADS * RET_DIM
RET_CHUNK = 128
ROPE_BASE = 10000.0
N_GROUPS = 4
EXPERTS_PER_GROUP = 8
N_EXPERTS = N_GROUPS * EXPERTS_PER_GROUP
D_EXPERT = 512
IN_COLS = 2 * CONV_CH + 3 * NA_WIDTH + 4 * RET_WIDTH
EPS = 1e-6
NEG_INF = -1e30

F32 = jnp.float32
BF16 = jnp.bfloat16
HIGHEST = lax.Precision.HIGHEST

T_CTX = BATCH * SEQ
T_LAT = DEC_BATCH * DEC_SEQ
T_ALL = T_CTX + T_LAT
N_COND = 1 + DEC_BATCH
COND_ROWS = 8

TM = 512
NB_CTX = T_CTX // TM
NB_LAT = T_LAT // TM
NB_ALL = NB_CTX + NB_LAT
LAT_BLOCKS_PER_REQ = DEC_SEQ // TM

LANES = 128
SUBLANES = 8
MXU_TILE = 256
ROUTE_COLS = LANES

COL_CONV = 0
COL_NA_Q = 2 * CONV_CH
COL_NA_K = COL_NA_Q + NA_WIDTH
COL_NA_V = COL_NA_K + NA_WIDTH
COL_RET = COL_NA_V + NA_WIDTH

NA_ROWS = 8
NA_Q = NA_ROWS * GRID_W
NA_KROWS = NA_ROWS + NA_KH
NA_KEYS = NA_KROWS * GRID_W
NA_RB = GRID_H // NA_ROWS

MOE_BLK = 1024
MOE_ROW_STEP = 128
MOE_LC = -(-(2 * TM + N_EXPERTS * (SUBLANES - 1)) // LANES) * LANES
N_CHUNK = NB_ALL
MOE_NBLK = -(-(N_CHUNK * MOE_LC) // MOE_BLK) + N_EXPERTS

VMEM_LIMIT = 56 * 1024 * 1024


def _cparams(sem):
    return pltpu.CompilerParams(dimension_semantics=sem, vmem_limit_bytes=VMEM_LIMIT)


def _sigmoid(x):
    return 1.0 / (1.0 + jnp.exp(-x))


def _cond_row(i):
    return jnp.where(i < NB_CTX, 0, 1 + (i - NB_CTX) // LAT_BLOCKS_PER_REQ)


ADA_TN = 1536


def _ada_kernel(cv_ref, w_ref, b_ref, o_ref):
    cv = cv_ref[...]
    s = cv * _sigmoid(cv)
    o_ref[...] = jnp.dot(s, w_ref[...], precision=HIGHEST, preferred_element_type=F32) + b_ref[...]


def _ada(cv, w_ada, b_ada):
    n = 6 * D_MODEL
    return pl.pallas_call(
        _ada_kernel,
        grid=(DEPTH, n // ADA_TN),
        in_specs=[
            pl.BlockSpec((COND_ROWS, D_MODEL), lambda l, j: (0, 0)),
            pl.BlockSpec((None, D_MODEL, ADA_TN), lambda l, j: (l, 0, j)),
            pl.BlockSpec((None, 1, ADA_TN), lambda l, j: (l, 0, j)),
        ],
        out_specs=pl.BlockSpec((None, COND_ROWS, ADA_TN), lambda l, j: (l, 0, j)),
        out_shape=jax.ShapeDtypeStruct((DEPTH, COND_ROWS, n), F32),
        compiler_params=_cparams(("arbitrary", "arbitrary")),
        name="ada_mod",
    )(cv, w_ada, b_ada.reshape(DEPTH, 1, n))


IN_TN = 768


def _norm_mod(x, g, shift, scale):
    ms = jnp.mean(x * x, axis=-1, keepdims=True)
    return (x * lax.rsqrt(ms + EPS) * g) * (1.0 + scale) + shift


def _cast_weights_once(w_ref, wb_ref):
    @pl.when(pl.program_id(0) == 0)
    def _():
        wb_ref[...] = w_ref[...].astype(BF16)


def _layer_weight_spec(layer, rows, cols):
    return pl.BlockSpec((None, rows, cols), lambda i: (layer, 0, 0), pipeline_mode=pl.Buffered(1))


def _inproj_body(x, mod_ref, g_ref, w_ref, wb_ref, z_ref):
    _cast_weights_once(w_ref, wb_ref)
    h = _norm_mod(x, g_ref[...], mod_ref[0:1, :], mod_ref[1:2, :]).astype(BF16)
    for c in range(IN_COLS // IN_TN):
        cols = slice(c * IN_TN, (c + 1) * IN_TN)
        z_ref[:, cols] = jnp.dot(h, wb_ref[:, cols], preferred_element_type=F32).astype(BF16)


def _inproj_first_kernel(xc_ref, xl_ref, mod_ref, g_ref, w_ref, z_ref, xo_ref, wb_ref):
    i = pl.program_id(0)
    x = jnp.where(i < NB_CTX, xc_ref[...], xl_ref[...])
    xo_ref[...] = x
    _inproj_body(x, mod_ref, g_ref, w_ref, wb_ref, z_ref)


U32 = jnp.uint32
D_HALF = D_MODEL // 2
_HI_MASK = np.uint32(0xFFFF0000)


def _pack_words(lo, hi):
    lo = lax.bitcast_convert_type(lo.astype(BF16).astype(F32), U32) >> 16
    hi = lax.bitcast_convert_type(hi.astype(BF16).astype(F32), U32) & _HI_MASK
    return lo | hi


def _pack_bf16_pairs(x):
    return _pack_words(x[:, :D_HALF], x[:, D_HALF:])


def _unpack_bf16_pairs(w):
    lo = lax.bitcast_convert_type(w << 16, F32).astype(BF16)
    hi = lax.bitcast_convert_type(w & _HI_MASK, F32).astype(BF16)
    return lo, hi


def _slot_onehot(route, slot):
    pos = route[:, 4 + slot:5 + slot].astype(jnp.int32)
    return lax.broadcasted_iota(jnp.int32, (route.shape[0], MOE_LC), 1) == pos


def _moe_residual(x_ref, ys_ref, r_ref, mod_ref):
    r = r_ref[...]
    sel = jnp.where(_slot_onehot(r, 0), r[:, 2:3], jnp.where(_slot_onehot(r, 1), r[:, 3:4], 0.0))
    sel = sel.astype(BF16)
    y = jnp.concatenate([jnp.dot(sel, half, preferred_element_type=F32)
                         for half in _unpack_bf16_pairs(ys_ref[...])], axis=-1)
    return x_ref[...] + mod_ref[5:6, :] * y


def _inproj_next_kernel(x_ref, ys_ref, r_ref, modp_ref, mod_ref, g_ref, w_ref, z_ref, xo_ref, wb_ref):
    x = _moe_residual(x_ref, ys_ref, r_ref, modp_ref)
    xo_ref[...] = x
    _inproj_body(x, mod_ref, g_ref, w_ref, wb_ref, z_ref)


def _tok_spec(cols):
    return pl.BlockSpec((TM, cols), lambda i: (i, 0))


def _mod_spec():
    return pl.BlockSpec((None, 6, D_MODEL), lambda i: (_cond_row(i), 0, 0))


def _full_spec(shape):
    return pl.BlockSpec(shape, lambda i: (0,) * len(shape))


def _ctx_lat_specs(cols):
    return [pl.BlockSpec((TM, cols), lambda i: (jnp.minimum(i, NB_CTX - 1), 0)),
            pl.BlockSpec((TM, cols), lambda i: (jnp.maximum(i - NB_CTX, 0), 0))]


def _inproj_first(x_ctx, x_lat, mod, g, w_in, layer):
    return pl.pallas_call(
        _inproj_first_kernel,
        grid=(NB_ALL,),
        in_specs=_ctx_lat_specs(D_MODEL) + [_mod_spec(), _full_spec((1, D_MODEL)),
                                            _layer_weight_spec(layer, D_MODEL, IN_COLS)],
        out_specs=[_tok_spec(IN_COLS), _tok_spec(D_MODEL)],
        out_shape=[jax.ShapeDtypeStruct((T_ALL, IN_COLS), BF16),
                   jax.ShapeDtypeStruct((T_ALL, D_MODEL), F32)],
        scratch_shapes=[pltpu.VMEM((D_MODEL, IN_COLS), BF16)],
        compiler_params=_cparams(("arbitrary",)),
        name="inproj_first",
    )(x_ctx, x_lat, mod, g, w_in)


def _inproj_next(x, ys, route, mod_prev, mod, g, w_in, layer):
    return pl.pallas_call(
        _inproj_next_kernel,
        grid=(NB_ALL,),
        in_specs=[_tok_spec(D_MODEL),
                  pl.BlockSpec((MOE_LC, D_HALF), lambda i: (i, 0)),
                  _tok_spec(ROUTE_COLS),
                  _mod_spec(), _mod_spec(), _full_spec((1, D_MODEL)),
                  _layer_weight_spec(layer, D_MODEL, IN_COLS)],
        out_specs=[_tok_spec(IN_COLS), _tok_spec(D_MODEL)],
        out_shape=[jax.ShapeDtypeStruct((T_ALL, IN_COLS), BF16),
                   jax.ShapeDtypeStruct((T_ALL, D_MODEL), F32)],
        scratch_shapes=[pltpu.VMEM((D_MODEL, IN_COLS), BF16)],
        compiler_params=_cparams(("arbitrary",)),
        name="inproj_next",
    )(x, ys, route, mod_prev, mod, g, w_in)


CONV_PAD = 16
CONV_CHUNK = 64


CONV_SPAN = CONV_CHUNK + 2 * CONV_PAD - SUBLANES


CONV_UNROLL = 4


def _conv_kernel(seq, z_ref, w_ref, b_ref, g_ref, be_ref, o_ref, upad_ref, shift_refs):
    zeros = jnp.zeros((CONV_PAD, CONV_CH), F32)
    upad_ref[0:CONV_PAD, :] = zeros
    upad_ref[seq + CONV_PAD:seq + 2 * CONV_PAD, :] = zeros

    def glu(ci, carry):
        base = pl.multiple_of(ci * 256, 256)
        zc = z_ref[pl.ds(base, 256), :].astype(F32)
        upad_ref[pl.ds(base + CONV_PAD, 256), :] = zc[:, :CONV_CH] * _sigmoid(zc[:, CONV_CH:])
        return carry

    lax.fori_loop(0, seq // 256, glu, 0)

    shift = CONV_PAD - CONV_K // 2

    def chunk(ci, shift_ref):
        base = pl.multiple_of(ci * CONV_CHUNK, CONV_CHUNK)
        win = upad_ref[pl.ds(base, CONV_CHUNK + 2 * CONV_PAD), :]
        acc = jnp.zeros((CONV_CHUNK, CONV_CH), F32)
        for sub in range(SUBLANES):
            shift_ref[sub] = win[sub:sub + CONV_SPAN, :]
            for k in range(CONV_K):
                if (k + shift) % SUBLANES == sub:
                    lo = k + shift - sub
                    acc = acc + w_ref[k:k + 1, :] * shift_ref[sub, lo:lo + CONV_CHUNK, :]
        acc = acc + b_ref[...]
        mu = jnp.mean(acc, axis=-1, keepdims=True)
        d = acc - mu
        var = jnp.mean(d * d, axis=-1, keepdims=True)
        n = d * lax.rsqrt(var + EPS) * g_ref[...] + be_ref[...]
        o_ref[pl.ds(base, CONV_CHUNK), :] = (n * _sigmoid(n)).astype(BF16)

    def chunks(cj, carry):
        for u in range(CONV_UNROLL):
            chunk(cj * CONV_UNROLL + u, shift_refs.at[u])
        return carry

    lax.fori_loop(0, seq // (CONV_CHUNK * CONV_UNROLL), chunks, 0)


def _conv(z, row_block0, nseq, seq, w, b, g, be):
    return pl.pallas_call(
        functools.partial(_conv_kernel, seq),
        grid=(nseq,),
        in_specs=[pl.BlockSpec((seq, 2 * CONV_CH), lambda s: (row_block0 + s, 0)),
                  _full_spec((CONV_K, CONV_CH)), _full_spec((1, CONV_CH)),
                  _full_spec((1, CONV_CH)), _full_spec((1, CONV_CH))],
        out_specs=pl.BlockSpec((seq, CONV_CH), lambda s: (s, 0)),
        out_shape=jax.ShapeDtypeStruct((nseq * seq, CONV_CH), BF16),
        scratch_shapes=[pltpu.VMEM((seq + 2 * CONV_PAD, CONV_CH), F32),
                        pltpu.VMEM((CONV_UNROLL, SUBLANES, CONV_SPAN, CONV_CH), F32)],
        compiler_params=_cparams(("arbitrary",)),
        name="conv_seq%d" % seq,
    )(z, w, b, g, be)


def _dot_nt(a, b):
    return lax.dot_general(a, b, (((1,), (1,)), ((), ())), preferred_element_type=F32)


NA_SCALE = NA_DIM ** -0.5
assert NA_SCALE == 2.0 ** round(np.log2(NA_SCALE)), "query pre-scaling assumes a power-of-two scale"


def _ctx_attn_kernel(layer, q_ref, k_ref, v_ref, *refs):
    if layer:
        _, _, o_ref, ko_ref, vo_ref = refs
    else:
        o_ref, ko_full, vo_full = refs
        ko_ref, vo_ref = ko_full.at[0], vo_full.at[0]
        for j in range(1, DEPTH):
            ko_full[j] = jnp.zeros(ko_full.shape[1:], F32)
            vo_full[j] = jnp.zeros(vo_full.shape[1:], F32)
    pair = 2 * NA_DIM
    left = lax.broadcasted_iota(jnp.int32, (SEQ, pair), 1) < NA_DIM
    outs = []
    for p in range(NA_HEADS // 2):
        lanes = slice(p * pair, (p + 1) * pair)
        qp, kp, vp = q_ref[:, lanes], k_ref[:, lanes], v_ref[:, lanes]
        kf, vf = kp.astype(F32), vp.astype(F32)
        o_h = []
        for hh in range(2):
            cols = slice(hh * NA_DIM, (hh + 1) * NA_DIM)
            ko_ref[2 * p + hh] = kf[:, cols]
            vo_ref[2 * p + hh] = vf[:, cols]
            qm = jnp.where(left == (hh == 0), qp, jnp.zeros_like(qp))
            s = _dot_nt(qm, kp) * NA_SCALE
            m = jnp.max(s, axis=-1, keepdims=True)
            e = jnp.exp(s - m)
            den = jnp.sum(e, axis=-1, keepdims=True)
            o_h.append(jnp.dot(e.astype(BF16), vp, preferred_element_type=F32) / den)
        outs.append(jnp.where(left, o_h[0], o_h[1]))
    o_ref[...] = jnp.concatenate(outs, axis=-1).astype(BF16)


def _ctx_attn(z, layer, k_prev=None, v_prev=None):
    qb, kb, vb = COL_NA_Q // NA_WIDTH, COL_NA_K // NA_WIDTH, COL_NA_V // NA_WIDTH
    head_shape = jax.ShapeDtypeStruct((BATCH, DEPTH, NA_HEADS, SEQ, NA_DIM), F32)
    head_spec = pl.BlockSpec((None, DEPTH, NA_HEADS, SEQ, NA_DIM), lambda b: (b, 0, 0, 0, 0))
    in_specs = [pl.BlockSpec((SEQ, NA_WIDTH), lambda b: (b, qb)),
                pl.BlockSpec((SEQ, NA_WIDTH), lambda b: (b, kb)),
                pl.BlockSpec((SEQ, NA_WIDTH), lambda b: (b, vb))]
    args = [z, z, z]
    aliases = {}
    if layer:
        any_spec = pl.BlockSpec(memory_space=pl.ANY)
        in_specs += [any_spec, any_spec]
        args += [k_prev, v_prev]
        aliases = {3: 1, 4: 2}
        head_spec = pl.BlockSpec((None, None, NA_HEADS, SEQ, NA_DIM), lambda b: (b, layer, 0, 0, 0))
    return pl.pallas_call(
        functools.partial(_ctx_attn_kernel, layer),
        grid=(BATCH,),
        in_specs=in_specs,
        out_specs=[pl.BlockSpec((SEQ, NA_WIDTH), lambda b: (b, 0)), head_spec, head_spec],
        out_shape=[jax.ShapeDtypeStruct((T_CTX, NA_WIDTH), BF16), head_shape, head_shape],
        input_output_aliases=aliases,
        compiler_params=_cparams(("arbitrary",)),
        name="ctx_attn",
    )(*args)


NA_KINDS = (0, NA_ROWS, GRID_H - NA_ROWS)
N_DR = 2 * NA_KH - 1
N_DC = 2 * NA_KW - 1


def _na_row_offset(r0, i, j):
    ks = min(max(r0 - NA_KH // 2, 0), GRID_H - NA_KROWS)
    r, kr = r0 + i, ks + j
    rs = min(max(r - NA_KH // 2, 0), GRID_H - NA_KH)
    return kr - r + NA_KH - 1 if rs <= kr < rs + NA_KH else None


def _na_bias_kernel(rpb_ref, o_ref):
    lh = pl.program_id(0)
    shape = (GRID_W, 2 * GRID_W)
    qc = lax.broadcasted_iota(jnp.int32, shape, 0)
    lane = lax.broadcasted_iota(jnp.int32, shape, 1)
    kc = lane % GRID_W
    dc = jnp.clip(kc - qc, -(NA_KW - 1), NA_KW - 1) + NA_KW - 1
    cs = jnp.clip(qc - NA_KW // 2, 0, GRID_W - NA_KW)
    col_ok = (kc >= cs) & (kc < cs + NA_KW)
    neg = jnp.full(shape, NEG_INF, F32)
    tiles = []
    for dr in range(N_DR):
        base = (lh * N_DR + dr) * N_DC
        val = jnp.zeros(shape, F32)
        for d in range(N_DC):
            val = jnp.where(dc == d, rpb_ref[base + d], val)
        tiles.append(jnp.where(col_ok, val, neg))
    left = lane < GRID_W
    for kind, r0 in enumerate(NA_KINDS):
        for i in range(NA_ROWS):
            for jp in range(NA_KROWS // 2):
                dl, dr_ = _na_row_offset(r0, i, 2 * jp), _na_row_offset(r0, i, 2 * jp + 1)
                tl = neg if dl is None else tiles[dl]
                tr = neg if dr_ is None else tiles[dr_]
                o_ref[kind, i * GRID_W:(i + 1) * GRID_W, jp * 2 * GRID_W:(jp + 1) * 2 * GRID_W] = (
                    jnp.where(left, tl, tr))


def _na_bias_tables(rpb):
    return pl.pallas_call(
        _na_bias_kernel,
        grid=(DEPTH * NA_HEADS,),
        in_specs=[pl.BlockSpec(memory_space=pltpu.SMEM)],
        out_specs=pl.BlockSpec((None, len(NA_KINDS), NA_Q, NA_KEYS), lambda i: (i, 0, 0, 0)),
        out_shape=jax.ShapeDtypeStruct((DEPTH * NA_HEADS, len(NA_KINDS), NA_Q, NA_KEYS), F32),
        compiler_params=_cparams(("arbitrary",)),
        name="nbr_bias",
    )(rpb.reshape(-1))


NA_G = 4


def _na_kernel(q_ref, k_ref, v_ref, kc_ref, vc_ref, bias_ref, o_ref):
    rb = pl.program_id(2)
    ks = jnp.clip(rb * NA_ROWS - NA_KH // 2, 0, GRID_H - NA_KROWS)
    start = pl.multiple_of(ks * GRID_W, GRID_W)
    q = q_ref[...] * NA_SCALE
    kl = k_ref[pl.ds(start, NA_KEYS), :]
    vl = v_ref[pl.ds(start, NA_KEYS), :]
    pair = 2 * NA_DIM
    left = lax.broadcasted_iota(jnp.int32, (NA_Q, pair), 1) < NA_DIM
    ones_loc = jnp.ones((NA_KEYS, pair), BF16)
    ones_ctx = jnp.ones((PAST_LEN, pair), BF16)
    kc, v_ext, vc_ext = [], [], []
    for p in range(NA_G // 2):
        lanes = slice(p * pair, (p + 1) * pair)
        kc.append(jnp.concatenate([kc_ref[2 * p].astype(BF16), kc_ref[2 * p + 1].astype(BF16)], axis=-1))
        vc = jnp.concatenate([vc_ref[2 * p].astype(BF16), vc_ref[2 * p + 1].astype(BF16)], axis=-1)
        v_ext.append(jnp.concatenate([vl[:, lanes], ones_loc], axis=-1))
        vc_ext.append(jnp.concatenate([vc, ones_ctx], axis=-1))

    def scores(hh):
        p = hh // 2
        lanes = slice(p * pair, (p + 1) * pair)
        qm = jnp.where(left == (hh % 2 == 0), q[:, lanes], jnp.zeros((NA_Q, pair), BF16))
        return _dot_nt(qm, kl[:, lanes]) + bias_ref[hh], _dot_nt(qm, kc[p])

    outs = []
    nxt = scores(0)
    for hh in range(NA_G):
        s_loc, s_ctx = nxt
        if hh + 1 < NA_G:
            nxt = scores(hh + 1)
        m = jnp.maximum(jnp.max(s_loc, axis=-1, keepdims=True), jnp.max(s_ctx, axis=-1, keepdims=True))
        p_loc = jnp.exp(s_loc - m).astype(BF16)
        p_ctx = jnp.exp(s_ctx - m).astype(BF16)
        o = (jnp.dot(p_loc, v_ext[hh // 2], preferred_element_type=F32)
             + jnp.dot(p_ctx, vc_ext[hh // 2], preferred_element_type=F32))
        outs.append(o[:, :pair] / o[:, pair:])
    o_ref[...] = jnp.concatenate([jnp.where(left, outs[2 * p], outs[2 * p + 1]) for p in range(NA_G // 2)],
                                 axis=-1).astype(BF16)


def _na_attn(z, cache_k, cache_v, bias, layer):
    lat_q0 = T_CTX // NA_Q
    lat_s0 = T_CTX // DEC_SEQ
    width = NA_G * NA_DIM
    qc, kc, vc = COL_NA_Q // width, COL_NA_K // width, COL_NA_V // width
    groups = NA_HEADS // NA_G

    def kind(rb):
        return jnp.where(rb == 0, 0, jnp.where(rb == NA_RB - 1, 2, 1))

    ctx_spec = pl.BlockSpec((None, None, NA_G, PAST_LEN, NA_DIM), lambda b, hg, rb: (b, layer, hg, 0, 0))
    return pl.pallas_call(
        _na_kernel,
        grid=(DEC_BATCH, groups, NA_RB),
        in_specs=[pl.BlockSpec((NA_Q, width), lambda b, hg, rb: (lat_q0 + b * NA_RB + rb, qc + hg)),
                  pl.BlockSpec((DEC_SEQ, width), lambda b, hg, rb: (lat_s0 + b, kc + hg)),
                  pl.BlockSpec((DEC_SEQ, width), lambda b, hg, rb: (lat_s0 + b, vc + hg)),
                  ctx_spec, ctx_spec,
                  pl.BlockSpec((NA_G, None, NA_Q, NA_KEYS),
                               lambda b, hg, rb: (layer * groups + hg, kind(rb), 0, 0))],
        out_specs=pl.BlockSpec((NA_Q, width), lambda b, hg, rb: (b * NA_RB + rb, hg)),
        out_shape=jax.ShapeDtypeStruct((T_LAT, NA_WIDTH), BF16),
        compiler_params=_cparams(("arbitrary", "arbitrary", "arbitrary")),
        name="nbr_attn",
    )(z, z, z, cache_k, cache_v, bias)


RET_PAIR = 2 * RET_DIM
RET_NPAIR = RET_HEADS // 2
assert RET_PAIR == LANES and RET_CHUNK == LANES
RET_UNROLL = 16


def _rope_tables():
    n_freq = RET_DIM // 4
    t = np.arange(DEC_SEQ)
    inv = jnp.asarray(ROPE_BASE, F32) ** (-jnp.arange(n_freq, dtype=F32) / n_freq)
    ang_r = jnp.asarray(t // GRID_W, F32)[:, None] * inv[None, :]
    ang_c = jnp.asarray(t % GRID_W, F32)[:, None] * inv[None, :]
    cos = jnp.concatenate([jnp.cos(ang_r)] * 2 + [jnp.cos(ang_c)] * 2, axis=-1)
    sin = jnp.concatenate([-jnp.sin(ang_r), jnp.sin(ang_r), -jnp.sin(ang_c), jnp.sin(ang_c)], axis=-1)
    lane = np.arange(RET_WIDTH)
    src = np.where(lane % (2 * n_freq) < n_freq, lane + n_freq, lane - n_freq)
    swap = np.zeros((RET_WIDTH, RET_WIDTH), np.float32)
    swap[src, lane] = 1.0
    return jnp.tile(cos, (1, RET_HEADS)), jnp.tile(sin, (1, RET_HEADS)), jnp.asarray(swap, BF16)


def _ret_kernel(seq, latent, *refs):
    if latent:
        (lg_ref, z_ref, gn_ref, cos_ref, sin_ref, swap_ref, s0f_ref, s0b_ref, y_ref,
         q_s, k_s, kv_s, st_s) = refs
    else:
        lg_ref, z_ref, gn_ref, y_ref, sf_ref, sb_ref, q_s, k_s, kv_s, st_s = refs
    nc = seq // RET_CHUNK
    ch, hd, pw = RET_CHUNK, RET_DIM, RET_PAIR

    row = lax.broadcasted_iota(jnp.int32, (ch, ch), 0).astype(F32)
    col = lax.broadcasted_iota(jnp.int32, (ch, ch), 1).astype(F32)
    pos = lax.broadcasted_iota(jnp.int32, (ch, pw), 0).astype(F32)
    left = lax.broadcasted_iota(jnp.int32, (ch, pw), 1) < hd
    top = lax.broadcasted_iota(jnp.int32, (pw, pw), 0) < hd
    same_head = top == (lax.broadcasted_iota(jnp.int32, (pw, pw), 1) < hd)
    same_head2 = jnp.concatenate([same_head, same_head], axis=0)

    def per_head(mask, fn, p):
        return jnp.where(mask, fn(2 * p), fn(2 * p + 1))

    decay = []
    for h in range(RET_HEADS):
        lf, lb = lg_ref[0, h], lg_ref[1, h]
        d_f = jnp.where(row >= col, jnp.exp(jnp.maximum(row - col, 0.0) * lf), 0.0)
        d_b = jnp.where(col >= row, jnp.exp(jnp.maximum(col - row, 0.0) * lb), 0.0)
        decay.append(d_f + d_b)
    q_dec, k_dec, c_dec_f, c_dec_b = [], [], [], []
    for p in range(RET_NPAIR):
        q_dec.append(jnp.concatenate(
            [per_head(left, lambda h: jnp.exp((pos + 1.0) * lg_ref[0, h]), p),
             per_head(left, lambda h: jnp.exp((ch - pos) * lg_ref[1, h]), p)], axis=-1))
        k_dec.append(jnp.concatenate(
            [per_head(left, lambda h: jnp.exp((ch - 1.0 - pos) * lg_ref[0, h]), p),
             per_head(left, lambda h: jnp.exp(pos * lg_ref[1, h]), p)], axis=-1))
        zero = jnp.zeros((pw, pw), F32)
        c_dec_f.append(per_head(top, lambda h: jnp.exp(zero + ch * lg_ref[0, h]), p))
        c_dec_b.append(per_head(top, lambda h: jnp.exp(zero + ch * lg_ref[1, h]), p))

    def rope(x, base):
        xf = x.astype(F32)
        if not latent:
            return xf
        swapped = jnp.dot(x, swap_ref[...], preferred_element_type=F32)
        return xf * cos_ref[pl.ds(base, ch), :] + swapped * sin_ref[pl.ds(base, ch), :]

    def pass1(n, carry):
        base = pl.multiple_of(n * ch, ch)
        zc = z_ref[pl.ds(base, ch), :]
        q = rope(zc[:, 0:RET_WIDTH], base)
        k = rope(zc[:, RET_WIDTH:2 * RET_WIDTH], base) * (RET_DIM ** -0.5)
        q_s[pl.ds(base, ch), :] = q.astype(BF16)
        k_s[pl.ds(base, ch), :] = k.astype(BF16)
        v = zc[:, 2 * RET_WIDTH:3 * RET_WIDTH]
        for p in range(RET_NPAIR):
            lanes = slice(p * pw, (p + 1) * pw)
            kp = k[:, lanes]
            k2 = (jnp.concatenate([kp, kp], axis=-1) * k_dec[p]).astype(BF16)
            kv = lax.dot_general(k2, v[:, lanes], (((0,), (0,)), ((), ())), preferred_element_type=F32)
            kv_s[n, p] = jnp.where(same_head2, kv, 0.0)
        return carry

    lax.fori_loop(0, nc, pass1, 0, unroll=min(RET_UNROLL, nc))

    def block_diag(a, b):
        z = jnp.zeros((hd, hd), F32)
        return jnp.concatenate([jnp.concatenate([a, z], axis=1), jnp.concatenate([z, b], axis=1)], axis=0)

    for p in range(RET_NPAIR):
        if latent:
            s_f = block_diag(s0f_ref[2 * p], s0f_ref[2 * p + 1])
            s_b = block_diag(s0b_ref[2 * p], s0b_ref[2 * p + 1])
        else:
            s_f = s_b = jnp.zeros((pw, pw), F32)

        def fwd(n, s, p=p):
            st_s[n, p, 0:pw, :] = s.astype(BF16)
            return c_dec_f[p] * s + kv_s[n, p, 0:pw, :]

        def bwd(i, s, p=p):
            n = nc - 1 - i
            st_s[n, p, pw:2 * pw, :] = s.astype(BF16)
            return c_dec_b[p] * s + kv_s[n, p, pw:2 * pw, :]

        s_f = lax.fori_loop(0, nc, fwd, s_f)
        s_b = lax.fori_loop(0, nc, bwd, s_b)
        if not latent:
            for hh in range(2):
                blk = slice(hh * hd, (hh + 1) * hd)
                sf_ref[2 * p + hh] = s_f[blk, blk]
                sb_ref[2 * p + hh] = s_b[blk, blk]

    def pass3(n, carry):
        base = pl.multiple_of(n * ch, ch)
        zc = z_ref[pl.ds(base, ch), :]
        q = q_s[pl.ds(base, ch), :]
        k = k_s[pl.ds(base, ch), :]
        v = zc[:, 2 * RET_WIDTH:3 * RET_WIDTH]
        gate = zc[:, 3 * RET_WIDTH:4 * RET_WIDTH].astype(F32)
        outs = []
        for p in range(RET_NPAIR):
            lanes = slice(p * pw, (p + 1) * pw)
            qp, kp, vp = q[:, lanes], k[:, lanes], v[:, lanes]
            o_h = []
            for hh in range(2):
                qm = jnp.where(left == (hh == 0), qp, jnp.zeros_like(qp))
                s = _dot_nt(qm, kp) * decay[2 * p + hh]
                o_h.append(jnp.dot(s.astype(BF16), vp, preferred_element_type=F32))
            qf = qp.astype(F32)
            q2 = (jnp.concatenate([qf, qf], axis=-1) * q_dec[p]).astype(BF16)
            o = jnp.where(left, o_h[0], o_h[1]) + jnp.dot(q2, st_s[n, p], preferred_element_type=F32)

            def half_mean(t):
                s_l = jnp.sum(jnp.where(left, t, 0.0), axis=-1, keepdims=True)
                s_r = jnp.sum(jnp.where(left, 0.0, t), axis=-1, keepdims=True)
                return jnp.where(left, s_l, s_r) * (1.0 / hd)

            d = o - half_mean(o)
            outs.append(d * lax.rsqrt(half_mean(d * d) + EPS))
        nrm = jnp.concatenate(outs, axis=-1)
        y_ref[pl.ds(base, ch), :] = (nrm * gn_ref[...] * (gate * _sigmoid(gate))).astype(BF16)
        return carry

    lax.fori_loop(0, nc, pass3, 0, unroll=min(RET_UNROLL, nc))


def _retention(z, lg, gn_g, latent, layer=None, rope=None, s0_f=None, s0_b=None):
    seq = DEC_SEQ if latent else SEQ
    nseq = DEC_BATCH if latent else BATCH
    nc = seq // RET_CHUNK
    row0 = (T_CTX // DEC_SEQ) if latent else 0
    cb = COL_RET // (4 * RET_WIDTH)
    in_specs = [pl.BlockSpec(memory_space=pltpu.SMEM),
                pl.BlockSpec((seq, 4 * RET_WIDTH), lambda s: (row0 + s, cb)),
                _full_spec((1, RET_WIDTH))]
    args = [lg, z, gn_g]
    state_shape = jax.ShapeDtypeStruct((nseq, RET_HEADS, RET_DIM, RET_DIM), F32)
    y_spec = pl.BlockSpec((seq, RET_WIDTH), lambda s: (s, 0))
    y_shape = jax.ShapeDtypeStruct((nseq * seq, RET_WIDTH), BF16)
    if latent:
        st_spec = pl.BlockSpec((None, None, RET_HEADS, RET_DIM, RET_DIM), lambda s: (s, layer, 0, 0, 0))

        def const_spec(shape):
            return pl.BlockSpec(shape, lambda s: (0,) * len(shape), pipeline_mode=pl.Buffered(1))

        in_specs += [const_spec((seq, RET_WIDTH)), const_spec((seq, RET_WIDTH)),
                     const_spec((RET_WIDTH, RET_WIDTH)), st_spec, st_spec]
        args += [rope[0], rope[1], rope[2], s0_f, s0_b]
        out_specs, out_shape = y_spec, y_shape
    else:
        so_spec = pl.BlockSpec((None, RET_HEADS, RET_DIM, RET_DIM), lambda s: (s, 0, 0, 0))
        out_specs, out_shape = [y_spec, so_spec, so_spec], [y_shape, state_shape, state_shape]
    return pl.pallas_call(
        functools.partial(_ret_kernel, seq, latent),
        grid=(nseq,),
        in_specs=in_specs,
        out_specs=out_specs,
        out_shape=out_shape,
        scratch_shapes=[pltpu.VMEM((seq, RET_WIDTH), BF16), pltpu.VMEM((seq, RET_WIDTH), BF16),
                        pltpu.VMEM((nc, RET_NPAIR, 2 * RET_PAIR, RET_PAIR), F32),
                        pltpu.VMEM((nc, RET_NPAIR, 2 * RET_PAIR, RET_PAIR), BF16)],
        compiler_params=_cparams(("arbitrary",)),
        name="retention_lat" if latent else "retention_ctx",
    )(*args)


def _route(logits):
    lane = lax.broadcasted_iota(jnp.int32, logits.shape, 1)
    lane_f = lane.astype(F32)
    big = float(ROUTE_COLS)
    neg = -jnp.inf
    is_grp = lane < N_GROUPS
    gl = jnp.where(is_grp, logits, neg)
    gmax = jnp.max(gl, axis=-1, keepdims=True)
    grp = jnp.min(jnp.where(gl == gmax, lane_f, big), axis=-1, keepdims=True)
    p_grp = 1.0 / jnp.sum(jnp.exp(gl - gmax), axis=-1, keepdims=True)
    e_f = lane_f - N_GROUPS
    lo = grp * EXPERTS_PER_GROUP
    in_grp = (e_f >= lo) & (e_f < lo + EXPERTS_PER_GROUP)
    el = jnp.where(in_grp, logits, neg)
    m1 = jnp.max(el, axis=-1, keepdims=True)
    i1 = jnp.min(jnp.where(el == m1, lane_f, big), axis=-1, keepdims=True)
    el2 = jnp.where(lane_f == i1, neg, el)
    m2 = jnp.max(el2, axis=-1, keepdims=True)
    i2 = jnp.min(jnp.where(el2 == m2, lane_f, big), axis=-1, keepdims=True)
    t = jnp.exp(m2 - m1)
    g1 = p_grp / (1.0 + t)
    g2 = p_grp * t / (1.0 + t)
    rows = logits.shape[0]
    oh1, oh2 = lane_f == i1, lane_f == i2
    oh = jnp.where(oh1 | oh2, 1.0, 0.0)
    tri = (lax.broadcasted_iota(jnp.int32, (rows, rows), 0)
           > lax.broadcasted_iota(jnp.int32, (rows, rows), 1))
    rank = jnp.dot(jnp.where(tri, 1.0, 0.0).astype(BF16), oh.astype(BF16), preferred_element_type=F32)
    tiles = jnp.floor((jnp.sum(oh, axis=0, keepdims=True) + (SUBLANES - 1)) * (1.0 / SUBLANES))
    upper = (lax.broadcasted_iota(jnp.int32, (ROUTE_COLS, ROUTE_COLS), 0)
             < lax.broadcasted_iota(jnp.int32, (ROUTE_COLS, ROUTE_COLS), 1))
    start = SUBLANES * jnp.dot(jnp.broadcast_to(tiles, (SUBLANES, ROUTE_COLS)).astype(BF16),
                               jnp.where(upper, 1.0, 0.0).astype(BF16),
                               preferred_element_type=F32)[0:1, :]
    pos = start + rank
    p1 = jnp.sum(jnp.where(oh1, pos, 0.0), axis=-1, keepdims=True)
    p2 = jnp.sum(jnp.where(oh2, pos, 0.0), axis=-1, keepdims=True)
    out = jnp.zeros(logits.shape, F32)
    for k, val in enumerate((i1 - N_GROUPS, i2 - N_GROUPS, g1, g2, p1, p2)):
        out = jnp.where(lane == k, val, out)
    return out, SUBLANES * tiles


def _outproj_kernel(ycc, ycl, ync, ynl, yrc, yrl, x_ref, mod_ref, g_ref, w_ref, wr_ref, br_ref,
                    xo_ref, xs_ref, r_ref, seg_ref, wb_ref):
    _cast_weights_once(w_ref, wb_ref)
    is_ctx = pl.program_id(0) < NB_CTX
    yc = jnp.where(is_ctx, ycc[...], ycl[...])
    yn = jnp.where(is_ctx, ync[...], ynl[...])
    yr = jnp.where(is_ctx, yrc[...], yrl[...])
    y = (jnp.dot(yc, wb_ref[0:CONV_CH, :], preferred_element_type=F32)
         + jnp.dot(yn, wb_ref[CONV_CH:CONV_CH + NA_WIDTH, :], preferred_element_type=F32)
         + jnp.dot(yr, wb_ref[CONV_CH + NA_WIDTH:, :], preferred_element_type=F32))
    x = x_ref[...] + mod_ref[2:3, :] * y
    xo_ref[...] = x
    h = _norm_mod(x, g_ref[...], mod_ref[3:4, :], mod_ref[4:5, :])
    h_hi = h.astype(BF16)
    h_lo = (h - h_hi.astype(F32)).astype(BF16)
    hw = jnp.dot(h_hi, wr_ref[...], preferred_element_type=F32)
    logits = (hw[:, :ROUTE_COLS] + hw[:, ROUTE_COLS:]
              + jnp.dot(h_lo, wr_ref[:, :ROUTE_COLS], preferred_element_type=F32) + br_ref[...])
    route, seg = _route(logits)
    r_ref[...] = route
    seg_ref[...] = jnp.broadcast_to(seg, seg_ref.shape)
    sel = _slot_onehot(route, 0) | _slot_onehot(route, 1)
    xs_ref[...] = _pack_bf16_pairs(lax.dot_general(jnp.where(sel, 1.0, 0.0).astype(BF16), h_hi,
                                                   (((0,), (0,)), ((), ())), preferred_element_type=F32))


def _outproj(y_conv, y_na, y_ret, x, mod, g, w_out, layer, w_route, b_route):
    return pl.pallas_call(
        _outproj_kernel,
        grid=(NB_ALL,),
        in_specs=(_ctx_lat_specs(CONV_CH) + _ctx_lat_specs(NA_WIDTH) + _ctx_lat_specs(RET_WIDTH)
                  + [_tok_spec(D_MODEL), _mod_spec(), _full_spec((1, D_MODEL)),
                     _layer_weight_spec(layer, D_MODEL, D_MODEL), _full_spec((D_MODEL, 2 * ROUTE_COLS)),
                     _full_spec((1, ROUTE_COLS))]),
        out_specs=[_tok_spec(D_MODEL), pl.BlockSpec((MOE_LC, D_HALF), lambda i: (i, 0)),
                   _tok_spec(ROUTE_COLS), pl.BlockSpec((None, SUBLANES, ROUTE_COLS), lambda i: (i, 0, 0))],
        out_shape=[jax.ShapeDtypeStruct((T_ALL, D_MODEL), F32),
                   jax.ShapeDtypeStruct((NB_ALL * MOE_LC, D_HALF), U32),
                   jax.ShapeDtypeStruct((T_ALL, ROUTE_COLS), F32),
                   jax.ShapeDtypeStruct((NB_ALL, SUBLANES, ROUTE_COLS), F32)],
        scratch_shapes=[pltpu.VMEM((D_MODEL, D_MODEL), BF16)],
        compiler_params=_cparams(("arbitrary",)),
        name="outproj_route",
    )(y_conv[0], y_conv[1], y_na[0], y_na[1], y_ret[0], y_ret[1], x, mod, g, w_out, w_route, b_route)


def _dispatch_tables(seg):
    seg_len = seg[:, 0, N_GROUPS:N_GROUPS + N_EXPERTS].astype(jnp.int32)
    experts = jnp.arange(N_EXPERTS, dtype=jnp.int32)
    in_chunk = jnp.cumsum(seg_len, axis=1) - seg_len
    seg_row = in_chunk + MOE_LC * jnp.arange(N_CHUNK, dtype=jnp.int32)[:, None]
    seg_off = jnp.cumsum(seg_len, axis=0) - seg_len
    rows_e = jnp.sum(seg_len, axis=0)
    chunk_rows = jnp.sum(seg_len, axis=1)
    nblk = (rows_e + MOE_BLK - 1) // MOE_BLK
    blk_end = jnp.cumsum(nblk)
    blk_start = blk_end - nblk
    blk = jnp.arange(MOE_NBLK, dtype=jnp.int32)
    n_active = blk_end[-1]
    blk_e = jnp.minimum(jnp.sum((blk_end[None, :] <= jnp.minimum(blk, n_active - 1)[:, None]).astype(jnp.int32),
                                axis=-1), N_EXPERTS - 1)
    mine = blk_e[:, None] == experts[None, :]
    blk_lo = (blk - jnp.sum(jnp.where(mine, blk_start[None, :], 0), axis=-1)) * MOE_BLK
    left = jnp.sum(jnp.where(mine, rows_e[None, :], 0), axis=-1) - blk_lo
    blk_nv = jnp.where(blk < n_active, jnp.clip(left, 0, MOE_BLK), 0).astype(jnp.int32)
    off_b = jnp.sum(jnp.where(mine[:, None, :], seg_off[None, :, :], 0), axis=-1)
    end_b = off_b + jnp.sum(jnp.where(mine[:, None, :], seg_len[None, :, :], 0), axis=-1)
    blk_c0 = jnp.sum((end_b <= blk_lo[:, None]).astype(jnp.int32), axis=-1)
    blk_c1 = jnp.sum((off_b < (blk_lo + blk_nv)[:, None]).astype(jnp.int32), axis=-1)
    after = jnp.sum(jnp.where(mine, blk_end[None, :], 0), axis=-1)
    blk_next_e = jnp.where(after < n_active, jnp.take(blk_e, jnp.minimum(after, MOE_NBLK - 1)), -1)
    row_b = jnp.sum(jnp.where(mine[:, None, :], seg_row[None, :, :], 0), axis=-1)
    first = jnp.maximum(off_b, blk_lo[:, None])
    piece_n = jnp.minimum(end_b, (blk_lo + blk_nv)[:, None]) - first
    piece_src = row_b + first - off_b
    piece_dst = first - blk_lo[:, None]
    piece = jnp.where(piece_n > 0,
                      (piece_src // SUBLANES) | ((piece_dst // SUBLANES) << PIECE_DST_SHIFT)
                      | ((piece_n // SUBLANES) << PIECE_N_SHIFT), 0).astype(jnp.int32)
    return (blk_e, blk_next_e.astype(jnp.int32), blk_nv, blk_c0, blk_c1, piece.reshape(-1), chunk_rows)


PIECE_DST_SHIFT = (N_CHUNK * MOE_LC // SUBLANES - 1).bit_length()
PIECE_N_SHIFT = PIECE_DST_SHIFT + (MOE_BLK // SUBLANES - 1).bit_length()
assert PIECE_N_SHIFT + (TM // SUBLANES).bit_length() <= 31


def _moe_kernel(layer, blk_e, blk_next_e, blk_nv, blk_c0, blk_c1, piece,
                chunk_rows, xs_hbm, w1_hbm, w3_hbm, w2_hbm, ys_hbm, xbuf, obuf, zeros,
                w1f, w3f, w2f, w1b, w3b, w2b, gsem, ssem, zsem, wsem):
    i = pl.program_id(0)
    last = pl.num_programs(0) - 1
    slot = i % 2

    def tiles(v):
        return pl.multiple_of(v, SUBLANES)

    def for_segments(blk, fn):
        def body(c, carry):
            word = piece[blk * N_CHUNK + c]
            src = (word & ((1 << PIECE_DST_SHIFT) - 1)) * SUBLANES
            dst = ((word >> PIECE_DST_SHIFT) & ((1 << (PIECE_N_SHIFT - PIECE_DST_SHIFT)) - 1)) * SUBLANES
            n = (word >> PIECE_N_SHIFT) * SUBLANES

            @pl.when(word != 0)
            def _():
                fn(tiles(src), tiles(dst), tiles(n))

            return carry

        lax.fori_loop(blk_c0[blk], blk_c1[blk], body, 0)

    def weight_copies(e):
        return [pltpu.make_async_copy(src.at[layer, e], dst, wsem)
                for src, dst in ((w1_hbm, w1f), (w3_hbm, w3f), (w2_hbm, w2f))]

    def start_gathers(blk, s):
        for_segments(blk, lambda src, dst, n: pltpu.make_async_copy(
            xs_hbm.at[pl.ds(src, n)], xbuf.at[s, pl.ds(dst, n)], gsem.at[s]).start())

    def start_scatters(blk, s):
        for_segments(blk, lambda dst, src, n: pltpu.make_async_copy(
            obuf.at[s, pl.ds(src, n)], ys_hbm.at[pl.ds(dst, n)], ssem.at[s]).start())

    def wait_rows(blk, s, sem):
        n = tiles(blk_nv[blk])

        @pl.when(n > 0)
        def _():
            pltpu.make_async_copy(xs_hbm.at[pl.ds(0, n)], xbuf.at[s, pl.ds(0, n)], sem.at[s]).wait()

    @pl.when(i == 0)
    def _():
        xbuf[...] = jnp.zeros_like(xbuf)
        zeros[...] = jnp.zeros_like(zeros)

        def tail(c):
            n = tiles(MOE_LC - chunk_rows[c])
            return n, pltpu.make_async_copy(zeros.at[pl.ds(0, n)],
                                            ys_hbm.at[pl.ds(tiles(c * MOE_LC + chunk_rows[c]), n)], zsem)

        def fill(c, carry):
            n, copy = tail(c)
            pl.when(n > 0)(copy.start)
            return carry

        def drain(c, carry):
            n, copy = tail(c)
            pl.when(n > 0)(copy.wait)
            return carry

        lax.fori_loop(0, N_CHUNK, fill, 0)
        lax.fori_loop(0, N_CHUNK, drain, 0)
        start_gathers(0, 0)
        for copy in weight_copies(blk_e[0]):
            copy.start()

    @pl.when(i < last)
    def _():
        start_gathers(i + 1, 1 - slot)

    @pl.when(i >= 2)
    def _():
        wait_rows(i - 2, slot, ssem)

    @pl.when(blk_nv[i] > 0)
    def _():
        @pl.when((i == 0) | (blk_e[i] != blk_e[jnp.maximum(i - 1, 0)]))
        def _():
            for copy in weight_copies(blk_e[i]):
                copy.wait()
            w1b[...] = w1f[...].astype(BF16)
            w3b[...] = w3f[...].astype(BF16)
            w2b[...] = w2f[...].astype(BF16)

            @pl.when(blk_next_e[i] >= 0)
            def _():
                for copy in weight_copies(blk_next_e[i]):
                    copy.start()

        wait_rows(i, slot, gsem)

        def expert_mlp(rows):
            x_lo, x_hi = _unpack_bf16_pairs(xbuf[slot, 0:rows, :])
            n_hid = D_EXPERT // MXU_TILE

            def in_dot(w, t):
                cols = slice(t * MXU_TILE, (t + 1) * MXU_TILE)
                return (jnp.dot(x_lo, w[:D_HALF, cols], preferred_element_type=F32)
                        + jnp.dot(x_hi, w[D_HALF:, cols], preferred_element_type=F32))

            ab = [(in_dot(w1b, t), in_dot(w3b, t)) for t in range(n_hid)]
            mid = [(a * _sigmoid(a) * b).astype(BF16) for a, b in ab]

            def out_dot(t):
                cols = slice(t * MXU_TILE, (t + 1) * MXU_TILE)
                return sum(jnp.dot(mid[j], w2b[j * MXU_TILE:(j + 1) * MXU_TILE, cols],
                                   preferred_element_type=F32) for j in range(n_hid))

            n_word = D_HALF // MXU_TILE
            for t in range(n_word):
                obuf[slot, 0:rows, t * MXU_TILE:(t + 1) * MXU_TILE] = _pack_words(
                    out_dot(t), out_dot(t + n_word))

        for rows in range(MOE_ROW_STEP, MOE_BLK + 1, MOE_ROW_STEP):
            @pl.when((blk_nv[i] > rows - MOE_ROW_STEP) & (blk_nv[i] <= rows))
            def _(rows=rows):
                expert_mlp(rows)

        start_scatters(i, slot)

    @pl.when(i == last)
    def _():
        wait_rows(i - 1, 1 - slot, ssem)
        wait_rows(i, slot, ssem)


def _moe(xs, w1, w3, w2, layer, blk_e, blk_next_e, blk_nv, blk_c0, blk_c1, piece, chunk_rows):
    any_spec = pl.BlockSpec(memory_space=pl.ANY)
    grid_spec = pltpu.PrefetchScalarGridSpec(
        num_scalar_prefetch=7,
        grid=(MOE_NBLK,),
        in_specs=[any_spec, any_spec, any_spec, any_spec],
        out_specs=any_spec,
        scratch_shapes=[pltpu.VMEM((2, MOE_BLK, D_HALF), U32), pltpu.VMEM((2, MOE_BLK, D_HALF), U32),
                        pltpu.VMEM((MOE_LC - 2 * TM, D_HALF), U32),
                        pltpu.VMEM((D_MODEL, D_EXPERT), F32), pltpu.VMEM((D_MODEL, D_EXPERT), F32),
                        pltpu.VMEM((D_EXPERT, D_MODEL), F32),
                        pltpu.VMEM((D_MODEL, D_EXPERT), BF16), pltpu.VMEM((D_MODEL, D_EXPERT), BF16),
                        pltpu.VMEM((D_EXPERT, D_MODEL), BF16),
                        pltpu.SemaphoreType.DMA((2,)), pltpu.SemaphoreType.DMA((2,)),
                        pltpu.SemaphoreType.DMA, pltpu.SemaphoreType.DMA])
    return pl.pallas_call(
        functools.partial(_moe_kernel, layer),
        grid_spec=grid_spec,
        out_shape=jax.ShapeDtypeStruct((NB_ALL * MOE_LC, D_HALF), U32),
        compiler_params=_cparams(("arbitrary",)),
        name="moe_experts",
    )(blk_e, blk_next_e, blk_nv, blk_c0, blk_c1, piece, chunk_rows, xs, w1, w3, w2)


def _final_kernel(x_ref, ys_ref, r_ref, mod_ref, g_ref, o_ref):
    x = _moe_residual(x_ref, ys_ref, r_ref, mod_ref)
    ms = jnp.mean(x * x, axis=-1, keepdims=True)
    o_ref[...] = x * lax.rsqrt(ms + EPS) * g_ref[...]


def _final(x, ys, route, mod, g, block0, nblocks):
    return pl.pallas_call(
        _final_kernel,
        grid=(nblocks,),
        in_specs=[pl.BlockSpec((TM, D_MODEL), lambda i: (block0 + i, 0)),
                  pl.BlockSpec((MOE_LC, D_HALF), lambda i: (block0 + i, 0)),
                  pl.BlockSpec((TM, ROUTE_COLS), lambda i: (block0 + i, 0)),
                  pl.BlockSpec((None, 6, D_MODEL), lambda i: (_cond_row(block0 + i), 0, 0)),
                  _full_spec((1, D_MODEL))],
        out_specs=_tok_spec(D_MODEL),
        out_shape=jax.ShapeDtypeStruct((nblocks * TM, D_MODEL), F32),
        compiler_params=_cparams(("arbitrary",)),
        name="final_norm",
    )(x, ys, route, mod, g)


def kernel(x_prompt, x_sample, c, cache_k, cache_v, state_ret_f, state_ret_b, c_ctx, w_ada, b_ada, norm1_g, norm2_g, w_in, w_out, conv_w, conv_b, conv_ln_g, conv_ln_b, na_rpb, ret_lg_f, ret_lg_b, ret_gn_g, w_route_g, b_route_g, w_route_e, b_route_e, w1, w3, w2, final_g):
    cv = jnp.zeros((COND_ROWS, D_MODEL), F32).at[0].set(c_ctx).at[1:N_COND].set(c)
    mods = _ada(cv, w_ada, b_ada).reshape(DEPTH, COND_ROWS, 6, D_MODEL)
    pad = ROUTE_COLS - N_GROUPS - N_EXPERTS
    w_route = jnp.pad(jnp.concatenate([w_route_g, w_route_e], axis=-1), ((0, 0), (0, 0), (0, pad)))
    b_route = jnp.pad(jnp.concatenate([b_route_g, b_route_e], axis=-1), ((0, 0), (0, pad)))
    w_route_hi = w_route.astype(BF16)
    w_route_lo = (w_route - w_route_hi.astype(F32)).astype(BF16)
    w_route = jnp.concatenate([w_route_hi, w_route_lo], axis=-1)
    na_bias = _na_bias_tables(na_rpb)
    rope = _rope_tables()
    lg = jnp.stack([ret_lg_f, ret_lg_b], axis=1)

    x_ctx = x_prompt.reshape(T_CTX, D_MODEL)
    x_lat = x_sample.reshape(T_LAT, D_MODEL)
    x = y = route = new_k = new_v = None
    sf_list, sb_list = [], []
    for l in range(DEPTH):
        g1 = norm1_g[l].reshape(1, D_MODEL)
        if l == 0:
            z, x = _inproj_first(x_ctx, x_lat, mods[l], g1, w_in, l)
        else:
            z, x = _inproj_next(x, y, route, mods[l - 1], mods[l], g1, w_in, l)
        conv_args = (conv_w[l], conv_b[l].reshape(1, -1), conv_ln_g[l].reshape(1, -1),
                     conv_ln_b[l].reshape(1, -1))
        yc_c = _conv(z, 0, BATCH, SEQ, *conv_args)
        yc_l = _conv(z, T_CTX // DEC_SEQ, DEC_BATCH, DEC_SEQ, *conv_args)
        yn_c, new_k, new_v = _ctx_attn(z, l, new_k, new_v)
        yn_l = _na_attn(z, cache_k, cache_v, na_bias, l)
        gn = ret_gn_g[l].reshape(1, RET_WIDTH)
        yr_c, sf_l, sb_l = _retention(z, lg[l], gn, latent=False)
        yr_l = _retention(z, lg[l], gn, latent=True, layer=l, rope=rope,
                          s0_f=state_ret_f, s0_b=state_ret_b)
        x, xs, route, seg = _outproj((yc_c, yc_l), (yn_c, yn_l), (yr_c, yr_l), x, mods[l],
                                     norm2_g[l].reshape(1, D_MODEL), w_out, l, w_route[l],
                                     b_route[l].reshape(1, ROUTE_COLS))
        y = _moe(xs, w1, w3, w2, l, *_dispatch_tables(seg))
        sf_list.append(sf_l)
        sb_list.append(sb_l)
    fg = final_g.reshape(1, D_MODEL)
    y_prompt = _final(x, y, route, mods[DEPTH - 1], fg, 0, NB_CTX).reshape(BATCH, SEQ, D_MODEL)
    y_sample = _final(x, y, route, mods[DEPTH - 1], fg, NB_CTX, NB_LAT).reshape(DEC_BATCH, DEC_SEQ, D_MODEL)
    return (y_prompt, y_sample, new_k, new_v, jnp.stack(sf_list, axis=1), jnp.stack(sb_list, axis=1))
```

```python
import functools

import numpy as np
import jax
import jax.numpy as jnp
from jax import lax
from jax.experimental import pallas as pl
from jax.experimental.pallas import tpu as pltpu

D_MODEL = 1024
BATCH = 32
SEQ = 256
DEPTH = 2
DEC_BATCH = 4
DEC_SEQ = 4096
PAST_LEN = 512
GRID_W = 64
GRID_H = DEC_SEQ // GRID_W
CONV_CH = 256
CONV_K = 31
NA_HEADS = 8
NA_DIM = 64
NA_WIDTH = NA_HEADS * NA_DIM
NA_KH = 8
NA_KW = 16
RET_HEADS = 4
RET_DIM = 64
RET_WIDTH = RET_HEADS * RET_DIM
RET_CHUNK = 128
ROPE_BASE = 10000.0
N_GROUPS = 4
EXPERTS_PER_GROUP = 8
N_EXPERTS = N_GROUPS * EXPERTS_PER_GROUP
D_EXPERT = 512
IN_COLS = 2 * CONV_CH + 3 * NA_WIDTH + 4 * RET_WIDTH
EPS = 1e-6
NEG_INF = -1e30

F32 = jnp.float32
BF16 = jnp.bfloat16
HIGHEST = lax.Precision.HIGHEST

T_CTX = BATCH * SEQ
T_LAT = DEC_BATCH * DEC_SEQ
T_ALL = T_CTX + T_LAT
N_COND = 1 + DEC_BATCH
COND_ROWS = 8

TM = 512
NB_CTX = T_CTX // TM
NB_LAT = T_LAT // TM
NB_ALL = NB_CTX + NB_LAT
LAT_BLOCKS_PER_REQ = DEC_SEQ // TM

LANES = 128
SUBLANES = 8
MXU_TILE = 256
ROUTE_COLS = LANES

COL_CONV = 0
COL_NA_Q = 2 * CONV_CH
COL_NA_K = COL_NA_Q + NA_WIDTH
COL_NA_V = COL_NA_K + NA_WIDTH
COL_RET = COL_NA_V + NA_WIDTH

NA_ROWS = 8
NA_Q = NA_ROWS * GRID_W
NA_KROWS = NA_ROWS + NA_KH
NA_KEYS = NA_KROWS * GRID_W
NA_RB = GRID_H // NA_ROWS

MOE_BLK = 1024
MOE_ROW_STEP = 128
MOE_LC = -(-(2 * TM + N_EXPERTS * (SUBLANES - 1)) // LANES) * LANES
N_CHUNK = NB_ALL
MOE_NBLK = -(-(N_CHUNK * MOE_LC) // MOE_BLK) + N_EXPERTS

VMEM_LIMIT = 56 * 1024 * 1024


def _cparams(sem):
    return pltpu.CompilerParams(dimension_semantics=sem, vmem_limit_bytes=VMEM_LIMIT)


def _sigmoid(x):
    return 1.0 / (1.0 + jnp.exp(-x))


def _cond_row(i):
    return jnp.where(i < NB_CTX, 0, 1 + (i - NB_CTX) // LAT_BLOCKS_PER_REQ)


ADA_TN = 1536


def _ada_kernel(cv_ref, w_ref, b_ref, o_ref):
    cv = cv_ref[...]
    s = cv * _sigmoid(cv)
    o_ref[...] = jnp.dot(s, w_ref[...], precision=HIGHEST, preferred_element_type=F32) + b_ref[...]


def _ada(cv, w_ada, b_ada):
    n = 6 * D_MODEL
    return pl.pallas_call(
        _ada_kernel,
        grid=(DEPTH, n // ADA_TN),
        in_specs=[
            pl.BlockSpec((COND_ROWS, D_MODEL), lambda l, j: (0, 0)),
            pl.BlockSpec((None, D_MODEL, ADA_TN), lambda l, j: (l, 0, j)),
            pl.BlockSpec((None, 1, ADA_TN), lambda l, j: (l, 0, j)),
        ],
        out_specs=pl.BlockSpec((None, COND_ROWS, ADA_TN), lambda l, j: (l, 0, j)),
        out_shape=jax.ShapeDtypeStruct((DEPTH, COND_ROWS, n), F32),
        compiler_params=_cparams(("arbitrary", "arbitrary")),
        name="ada_mod",
    )(cv, w_ada, b_ada.reshape(DEPTH, 1, n))


IN_TN = 768


def _norm_mod(x, g, shift, scale):
    ms = jnp.mean(x * x, axis=-1, keepdims=True)
    return (x * lax.rsqrt(ms + EPS) * g) * (1.0 + scale) + shift


def _cast_weights_once(w_ref, wb_ref):
    @pl.when(pl.program_id(0) == 0)
    def _():
        wb_ref[...] = w_ref[...].astype(BF16)


def _layer_weight_spec(layer, rows, cols):
    return pl.BlockSpec((None, rows, cols), lambda i: (layer, 0, 0), pipeline_mode=pl.Buffered(1))


def _inproj_body(x, mod_ref, g_ref, w_ref, wb_ref, z_ref):
    _cast_weights_once(w_ref, wb_ref)
    h = _norm_mod(x, g_ref[...], mod_ref[0:1, :], mod_ref[1:2, :]).astype(BF16)
    for c in range(IN_COLS // IN_TN):
        cols = slice(c * IN_TN, (c + 1) * IN_TN)
        z_ref[:, cols] = jnp.dot(h, wb_ref[:, cols], preferred_element_type=F32).astype(BF16)


def _inproj_first_kernel(xc_ref, xl_ref, mod_ref, g_ref, w_ref, z_ref, xo_ref, wb_ref):
    i = pl.program_id(0)
    x = jnp.where(i < NB_CTX, xc_ref[...], xl_ref[...])
    xo_ref[...] = x
    _inproj_body(x, mod_ref, g_ref, w_ref, wb_ref, z_ref)


U32 = jnp.uint32
D_HALF = D_MODEL // 2
_HI_MASK = np.uint32(0xFFFF0000)


def _pack_words(lo, hi):
    lo = lax.bitcast_convert_type(lo.astype(BF16).astype(F32), U32) >> 16
    hi = lax.bitcast_convert_type(hi.astype(BF16).astype(F32), U32) & _HI_MASK
    return lo | hi


def _pack_bf16_pairs(x):
    return _pack_words(x[:, :D_HALF], x[:, D_HALF:])


def _unpack_bf16_pairs(w):
    lo = lax.bitcast_convert_type(w << 16, F32).astype(BF16)
    hi = lax.bitcast_convert_type(w & _HI_MASK, F32).astype(BF16)
    return lo, hi


def _slot_onehot(route, slot):
    pos = route[:, 4 + slot:5 + slot].astype(jnp.int32)
    return lax.broadcasted_iota(jnp.int32, (route.shape[0], MOE_LC), 1) == pos


def _moe_residual(x_ref, ys_ref, r_ref, mod_ref):
    r = r_ref[...]
    sel = jnp.where(_slot_onehot(r, 0), r[:, 2:3], jnp.where(_slot_onehot(r, 1), r[:, 3:4], 0.0))
    sel = sel.astype(BF16)
    y = jnp.concatenate([jnp.dot(sel, half, preferred_element_type=F32)
                         for half in _unpack_bf16_pairs(ys_ref[...])], axis=-1)
    return x_ref[...] + mod_ref[5:6, :] * y


def _inproj_next_kernel(x_ref, ys_ref, r_ref, modp_ref, mod_ref, g_ref, w_ref, z_ref, xo_ref, wb_ref):
    x = _moe_residual(x_ref, ys_ref, r_ref, modp_ref)
    xo_ref[...] = x
    _inproj_body(x, mod_ref, g_ref, w_ref, wb_ref, z_ref)


def _tok_spec(cols):
    return pl.BlockSpec((TM, cols), lambda i: (i, 0))


def _mod_spec():
    return pl.BlockSpec((None, 6, D_MODEL), lambda i: (_cond_row(i), 0, 0))


def _full_spec(shape):
    return pl.BlockSpec(shape, lambda i: (0,) * len(shape))


def _ctx_lat_specs(cols):
    return [pl.BlockSpec((TM, cols), lambda i: (jnp.minimum(i, NB_CTX - 1), 0)),
            pl.BlockSpec((TM, cols), lambda i: (jnp.maximum(i - NB_CTX, 0), 0))]


def _inproj_first(x_ctx, x_lat, mod, g, w_in, layer):
    return pl.pallas_call(
        _inproj_first_kernel,
        grid=(NB_ALL,),
        in_specs=_ctx_lat_specs(D_MODEL) + [_mod_spec(), _full_spec((1, D_MODEL)),
                                            _layer_weight_spec(layer, D_MODEL, IN_COLS)],
        out_specs=[_tok_spec(IN_COLS), _tok_spec(D_MODEL)],
        out_shape=[jax.ShapeDtypeStruct((T_ALL, IN_COLS), BF16),
                   jax.ShapeDtypeStruct((T_ALL, D_MODEL), F32)],
        scratch_shapes=[pltpu.VMEM((D_MODEL, IN_COLS), BF16)],
        compiler_params=_cparams(("arbitrary",)),
        name="inproj_first",
    )(x_ctx, x_lat, mod, g, w_in)


def _inproj_next(x, ys, route, mod_prev, mod, g, w_in, layer):
    return pl.pallas_call(
        _inproj_next_kernel,
        grid=(NB_ALL,),
        in_specs=[_tok_spec(D_MODEL),
                  pl.BlockSpec((MOE_LC, D_HALF), lambda i: (i, 0)),
                  _tok_spec(ROUTE_COLS),
                  _mod_spec(), _mod_spec(), _full_spec((1, D_MODEL)),
                  _layer_weight_spec(layer, D_MODEL, IN_COLS)],
        out_specs=[_tok_spec(IN_COLS), _tok_spec(D_MODEL)],
        out_shape=[jax.ShapeDtypeStruct((T_ALL, IN_COLS), BF16),
                   jax.ShapeDtypeStruct((T_ALL, D_MODEL), F32)],
        scratch_shapes=[pltpu.VMEM((D_MODEL, IN_COLS), BF16)],
        compiler_params=_cparams(("arbitrary",)),
        name="inproj_next",
    )(x, ys, route, mod_prev, mod, g, w_in)


CONV_PAD = 16
CONV_CHUNK = 64


CONV_SPAN = CONV_CHUNK + 2 * CONV_PAD - SUBLANES


CONV_UNROLL = 4


def _conv_kernel(seq, z_ref, w_ref, b_ref, g_ref, be_ref, o_ref, upad_ref, shift_refs):
    zeros = jnp.zeros((CONV_PAD, CONV_CH), F32)
    upad_ref[0:CONV_PAD, :] = zeros
    upad_ref[seq + CONV_PAD:seq + 2 * CONV_PAD, :] = zeros

    def glu(ci, carry):
        base = pl.multiple_of(ci * 256, 256)
        zc = z_ref[pl.ds(base, 256), :].astype(F32)
        upad_ref[pl.ds(base + CONV_PAD, 256), :] = zc[:, :CONV_CH] * _sigmoid(zc[:, CONV_CH:])
        return carry

    lax.fori_loop(0, seq // 256, glu, 0)

    shift = CONV_PAD - CONV_K // 2

    def chunk(ci, shift_ref):
        base = pl.multiple_of(ci * CONV_CHUNK, CONV_CHUNK)
        win = upad_ref[pl.ds(base, CONV_CHUNK + 2 * CONV_PAD), :]
        acc = jnp.zeros((CONV_CHUNK, CONV_CH), F32)
        for sub in range(SUBLANES):
            shift_ref[sub] = win[sub:sub + CONV_SPAN, :]
            for k in range(CONV_K):
                if (k + shift) % SUBLANES == sub:
                    lo = k + shift - sub
                    acc = acc + w_ref[k:k + 1, :] * shift_ref[sub, lo:lo + CONV_CHUNK, :]
        acc = acc + b_ref[...]
        mu = jnp.mean(acc, axis=-1, keepdims=True)
        d = acc - mu
        var = jnp.mean(d * d, axis=-1, keepdims=True)
        n = d * lax.rsqrt(var + EPS) * g_ref[...] + be_ref[...]
        o_ref[pl.ds(base, CONV_CHUNK), :] = (n * _sigmoid(n)).astype(BF16)

    def chunks(cj, carry):
        for u in range(CONV_UNROLL):
            chunk(cj * CONV_UNROLL + u, shift_refs.at[u])
        return carry

    lax.fori_loop(0, seq // (CONV_CHUNK * CONV_UNROLL), chunks, 0)


def _conv(z, row_block0, nseq, seq, w, b, g, be):
    return pl.pallas_call(
        functools.partial(_conv_kernel, seq),
        grid=(nseq,),
        in_specs=[pl.BlockSpec((seq, 2 * CONV_CH), lambda s: (row_block0 + s, 0)),
                  _full_spec((CONV_K, CONV_CH)), _full_spec((1, CONV_CH)),
                  _full_spec((1, CONV_CH)), _full_spec((1, CONV_CH))],
        out_specs=pl.BlockSpec((seq, CONV_CH), lambda s: (s, 0)),
        out_shape=jax.ShapeDtypeStruct((nseq * seq, CONV_CH), BF16),
        scratch_shapes=[pltpu.VMEM((seq + 2 * CONV_PAD, CONV_CH), F32),
                        pltpu.VMEM((CONV_UNROLL, SUBLANES, CONV_SPAN, CONV_CH), F32)],
        compiler_params=_cparams(("arbitrary",)),
        name="conv_seq%d" % seq,
    )(z, w, b, g, be)


def _dot_nt(a, b):
    return lax.dot_general(a, b, (((1,), (1,)), ((), ())), preferred_element_type=F32)


NA_SCALE = NA_DIM ** -0.5
assert NA_SCALE == 2.0 ** round(np.log2(NA_SCALE)), "query pre-scaling assumes a power-of-two scale"


def _ctx_attn_kernel(layer, q_ref, k_ref, v_ref, *refs):
    if layer:
        _, _, o_ref, ko_ref, vo_ref = refs
    else:
        o_ref, ko_full, vo_full = refs
        ko_ref, vo_ref = ko_full.at[0], vo_full.at[0]
        for j in range(1, DEPTH):
            ko_full[j] = jnp.zeros(ko_full.shape[1:], F32)
            vo_full[j] = jnp.zeros(vo_full.shape[1:], F32)
    pair = 2 * NA_DIM
    left = lax.broadcasted_iota(jnp.int32, (SEQ, pair), 1) < NA_DIM
    outs = []
    for p in range(NA_HEADS // 2):
        lanes = slice(p * pair, (p + 1) * pair)
        qp, kp, vp = q_ref[:, lanes], k_ref[:, lanes], v_ref[:, lanes]
        kf, vf = kp.astype(F32), vp.astype(F32)
        o_h = []
        for hh in range(2):
            cols = slice(hh * NA_DIM, (hh + 1) * NA_DIM)
            ko_ref[2 * p + hh] = kf[:, cols]
            vo_ref[2 * p + hh] = vf[:, cols]
            qm = jnp.where(left == (hh == 0), qp, jnp.zeros_like(qp))
            s = _dot_nt(qm, kp) * NA_SCALE
            m = jnp.max(s, axis=-1, keepdims=True)
            e = jnp.exp(s - m)
            den = jnp.sum(e, axis=-1, keepdims=True)
            o_h.append(jnp.dot(e.astype(BF16), vp, preferred_element_type=F32) / den)
        outs.append(jnp.where(left, o_h[0], o_h[1]))
    o_ref[...] = jnp.concatenate(outs, axis=-1).astype(BF16)


def _ctx_attn(z, layer, k_prev=None, v_prev=None):
    qb, kb, vb = COL_NA_Q // NA_WIDTH, COL_NA_K // NA_WIDTH, COL_NA_V // NA_WIDTH
    head_shape = jax.ShapeDtypeStruct((BATCH, DEPTH, NA_HEADS, SEQ, NA_DIM), F32)
    head_spec = pl.BlockSpec((None, DEPTH, NA_HEADS, SEQ, NA_DIM), lambda b: (b, 0, 0, 0, 0))
    in_specs = [pl.BlockSpec((SEQ, NA_WIDTH), lambda b: (b, qb)),
                pl.BlockSpec((SEQ, NA_WIDTH), lambda b: (b, kb)),
                pl.BlockSpec((SEQ, NA_WIDTH), lambda b: (b, vb))]
    args = [z, z, z]
    aliases = {}
    if layer:
        any_spec = pl.BlockSpec(memory_space=pl.ANY)
        in_specs += [any_spec, any_spec]
        args += [k_prev, v_prev]
        aliases = {3: 1, 4: 2}
        head_spec = pl.BlockSpec((None, None, NA_HEADS, SEQ, NA_DIM), lambda b: (b, layer, 0, 0, 0))
    return pl.pallas_call(
        functools.partial(_ctx_attn_kernel, layer),
        grid=(BATCH,),
        in_specs=in_specs,
        out_specs=[pl.BlockSpec((SEQ, NA_WIDTH), lambda b: (b, 0)), head_spec, head_spec],
        out_shape=[jax.ShapeDtypeStruct((T_CTX, NA_WIDTH), BF16), head_shape, head_shape],
        input_output_aliases=aliases,
        compiler_params=_cparams(("arbitrary",)),
        name="ctx_attn",
    )(*args)


NA_KINDS = (0, NA_ROWS, GRID_H - NA_ROWS)
N_DR = 2 * NA_KH - 1
N_DC = 2 * NA_KW - 1


def _na_row_offset(r0, i, j):
    ks = min(max(r0 - NA_KH // 2, 0), GRID_H - NA_KROWS)
    r, kr = r0 + i, ks + j
    rs = min(max(r - NA_KH // 2, 0), GRID_H - NA_KH)
    return kr - r + NA_KH - 1 if rs <= kr < rs + NA_KH else None


def _na_bias_kernel(rpb_ref, o_ref):
    lh = pl.program_id(0)
    shape = (GRID_W, 2 * GRID_W)
    qc = lax.broadcasted_iota(jnp.int32, shape, 0)
    lane = lax.broadcasted_iota(jnp.int32, shape, 1)
    kc = lane % GRID_W
    dc = jnp.clip(kc - qc, -(NA_KW - 1), NA_KW - 1) + NA_KW - 1
    cs = jnp.clip(qc - NA_KW // 2, 0, GRID_W - NA_KW)
    col_ok = (kc >= cs) & (kc < cs + NA_KW)
    neg = jnp.full(shape, NEG_INF, F32)
    tiles = []
    for dr in range(N_DR):
        base = (lh * N_DR + dr) * N_DC
        val = jnp.zeros(shape, F32)
        for d in range(N_DC):
            val = jnp.where(dc == d, rpb_ref[base + d], val)
        tiles.append(jnp.where(col_ok, val, neg))
    left = lane < GRID_W
    for kind, r0 in enumerate(NA_KINDS):
        for i in range(NA_ROWS):
            for jp in range(NA_KROWS // 2):
                dl, dr_ = _na_row_offset(r0, i, 2 * jp), _na_row_offset(r0, i, 2 * jp + 1)
                tl = neg if dl is None else tiles[dl]
                tr = neg if dr_ is None else tiles[dr_]
                o_ref[kind, i * GRID_W:(i + 1) * GRID_W, jp * 2 * GRID_W:(jp + 1) * 2 * GRID_W] = (
                    jnp.where(left, tl, tr))


def _na_bias_tables(rpb):
    return pl.pallas_call(
        _na_bias_kernel,
        grid=(DEPTH * NA_HEADS,),
        in_specs=[pl.BlockSpec(memory_space=pltpu.SMEM)],
        out_specs=pl.BlockSpec((None, len(NA_KINDS), NA_Q, NA_KEYS), lambda i: (i, 0, 0, 0)),
        out_shape=jax.ShapeDtypeStruct((DEPTH * NA_HEADS, len(NA_KINDS), NA_Q, NA_KEYS), F32),
        compiler_params=_cparams(("arbitrary",)),
        name="nbr_bias",
    )(rpb.reshape(-1))


NA_G = 4


def _na_kernel(q_ref, k_ref, v_ref, kc_ref, vc_ref, bias_ref, o_ref):
    rb = pl.program_id(2)
    ks = jnp.clip(rb * NA_ROWS - NA_KH // 2, 0, GRID_H - NA_KROWS)
    start = pl.multiple_of(ks * GRID_W, GRID_W)
    q = q_ref[...] * NA_SCALE
    kl = k_ref[pl.ds(start, NA_KEYS), :]
    vl = v_ref[pl.ds(start, NA_KEYS), :]
    pair = 2 * NA_DIM
    left = lax.broadcasted_iota(jnp.int32, (NA_Q, pair), 1) < NA_DIM
    ones_loc = jnp.ones((NA_KEYS, pair), BF16)
    ones_ctx = jnp.ones((PAST_LEN, pair), BF16)
    kc, v_ext, vc_ext = [], [], []
    for p in range(NA_G // 2):
        lanes = slice(p * pair, (p + 1) * pair)
        kc.append(jnp.concatenate([kc_ref[2 * p].astype(BF16), kc_ref[2 * p + 1].astype(BF16)], axis=-1))
        vc = jnp.concatenate([vc_ref[2 * p].astype(BF16), vc_ref[2 * p + 1].astype(BF16)], axis=-1)
        v_ext.append(jnp.concatenate([vl[:, lanes], ones_loc], axis=-1))
        vc_ext.append(jnp.concatenate([vc, ones_ctx], axis=-1))

    def scores(hh):
        p = hh // 2
        lanes = slice(p * pair, (p + 1) * pair)
        qm = jnp.where(left == (hh % 2 == 0), q[:, lanes], jnp.zeros((NA_Q, pair), BF16))
        return _dot_nt(qm, kl[:, lanes]) + bias_ref[hh], _dot_nt(qm, kc[p])

    outs = []
    nxt = scores(0)
    for hh in range(NA_G):
        s_loc, s_ctx = nxt
        if hh + 1 < NA_G:
            nxt = scores(hh + 1)
        m = jnp.maximum(jnp.max(s_loc, axis=-1, keepdims=True), jnp.max(s_ctx, axis=-1, keepdims=True))
        p_loc = jnp.exp(s_loc - m).astype(BF16)
        p_ctx = jnp.exp(s_ctx - m).astype(BF16)
        o = (jnp.dot(p_loc, v_ext[hh // 2], preferred_element_type=F32)
             + jnp.dot(p_ctx, vc_ext[hh // 2], preferred_element_type=F32))
        outs.append(o[:, :pair] / o[:, pair:])
    o_ref[...] = jnp.concatenate([jnp.where(left, outs[2 * p], outs[2 * p + 1]) for p in range(NA_G // 2)],
                                 axis=-1).astype(BF16)


def _na_attn(z, cache_k, cache_v, bias, layer):
    lat_q0 = T_CTX // NA_Q
    lat_s0 = T_CTX // DEC_SEQ
    width = NA_G * NA_DIM
    qc, kc, vc = COL_NA_Q // width, COL_NA_K // width, COL_NA_V // width
    groups = NA_HEADS // NA_G

    def kind(rb):
        return jnp.where(rb == 0, 0, jnp.where(rb == NA_RB - 1, 2, 1))

    ctx_spec = pl.BlockSpec((None, None, NA_G, PAST_LEN, NA_DIM), lambda b, hg, rb: (b, layer, hg, 0, 0))
    return pl.pallas_call(
        _na_kernel,
        grid=(DEC_BATCH, groups, NA_RB),
        in_specs=[pl.BlockSpec((NA_Q, width), lambda b, hg, rb: (lat_q0 + b * NA_RB + rb, qc + hg)),
                  pl.BlockSpec((DEC_SEQ, width), lambda b, hg, rb: (lat_s0 + b, kc + hg)),
                  pl.BlockSpec((DEC_SEQ, width), lambda b, hg, rb: (lat_s0 + b, vc + hg)),
                  ctx_spec, ctx_spec,
                  pl.BlockSpec((NA_G, None, NA_Q, NA_KEYS),
                               lambda b, hg, rb: (layer * groups + hg, kind(rb), 0, 0))],
        out_specs=pl.BlockSpec((NA_Q, width), lambda b, hg, rb: (b * NA_RB + rb, hg)),
        out_shape=jax.ShapeDtypeStruct((T_LAT, NA_WIDTH), BF16),
        compiler_params=_cparams(("arbitrary", "arbitrary", "arbitrary")),
        name="nbr_attn",
    )(z, z, z, cache_k, cache_v, bias)


RET_PAIR = 2 * RET_DIM
RET_NPAIR = RET_HEADS // 2
assert RET_PAIR == LANES and RET_CHUNK == LANES
RET_UNROLL = 16
RET_CTX_GROUP = 8


def _rope_tables():
    n_freq = RET_DIM // 4
    t = np.arange(DEC_SEQ)
    inv = jnp.asarray(ROPE_BASE, F32) ** (-jnp.arange(n_freq, dtype=F32) / n_freq)
    ang_r = jnp.asarray(t // GRID_W, F32)[:, None] * inv[None, :]
    ang_c = jnp.asarray(t % GRID_W, F32)[:, None] * inv[None, :]
    cos = jnp.concatenate([jnp.cos(ang_r)] * 2 + [jnp.cos(ang_c)] * 2, axis=-1)
    sin = jnp.concatenate([-jnp.sin(ang_r), jnp.sin(ang_r), -jnp.sin(ang_c), jnp.sin(ang_c)], axis=-1)
    lane = np.arange(RET_WIDTH)
    src = np.where(lane % (2 * n_freq) < n_freq, lane + n_freq, lane - n_freq)
    swap = np.zeros((RET_WIDTH, RET_WIDTH), np.float32)
    swap[src, lane] = 1.0
    return jnp.tile(cos, (1, RET_HEADS)), jnp.tile(sin, (1, RET_HEADS)), jnp.asarray(swap, BF16)


def _ret_kernel(seq, group, latent, *refs):
    if latent:
        (lg_ref, z_ref, gn_ref, cos_ref, sin_ref, swap_ref, s0f_ref, s0b_ref, y_ref,
         q_s, k_s, kv_s, st_s) = refs
    else:
        lg_ref, z_ref, gn_ref, y_ref, sf_ref, sb_ref, q_s, k_s, kv_s, st_s = refs
    nc = seq // RET_CHUNK
    nct = group * nc
    ch, hd, pw = RET_CHUNK, RET_DIM, RET_PAIR

    row = lax.broadcasted_iota(jnp.int32, (ch, ch), 0).astype(F32)
    col = lax.broadcasted_iota(jnp.int32, (ch, ch), 1).astype(F32)
    pos = lax.broadcasted_iota(jnp.int32, (ch, pw), 0).astype(F32)
    left = lax.broadcasted_iota(jnp.int32, (ch, pw), 1) < hd
    top = lax.broadcasted_iota(jnp.int32, (pw, pw), 0) < hd
    same_head = top == (lax.broadcasted_iota(jnp.int32, (pw, pw), 1) < hd)
    same_head2 = jnp.concatenate([same_head, same_head], axis=0)

    def per_head(mask, fn, p):
        return jnp.where(mask, fn(2 * p), fn(2 * p + 1))

    decay = []
    for h in range(RET_HEADS):
        lf, lb = lg_ref[0, h], lg_ref[1, h]
        d_f = jnp.where(row >= col, jnp.exp(jnp.maximum(row - col, 0.0) * lf), 0.0)
        d_b = jnp.where(col >= row, jnp.exp(jnp.maximum(col - row, 0.0) * lb), 0.0)
        decay.append(d_f + d_b)
    q_dec, k_dec, c_dec_f, c_dec_b = [], [], [], []
    for p in range(RET_NPAIR):
        q_dec.append(jnp.concatenate(
            [per_head(left, lambda h: jnp.exp((pos + 1.0) * lg_ref[0, h]), p),
             per_head(left, lambda h: jnp.exp((ch - pos) * lg_ref[1, h]), p)], axis=-1))
        k_dec.append(jnp.concatenate(
            [per_head(left, lambda h: jnp.exp((ch - 1.0 - pos) * lg_ref[0, h]), p),
             per_head(left, lambda h: jnp.exp(pos * lg_ref[1, h]), p)], axis=-1))
        zero = jnp.zeros((pw, pw), F32)
        c_dec_f.append(per_head(top, lambda h: jnp.exp(zero + ch * lg_ref[0, h]), p))
        c_dec_b.append(per_head(top, lambda h: jnp.exp(zero + ch * lg_ref[1, h]), p))

    def rope(x, base):
        xf = x.astype(F32)
        if not latent:
            return xf
        swapped = jnp.dot(x, swap_ref[...], preferred_element_type=F32)
        return xf * cos_ref[pl.ds(base, ch), :] + swapped * sin_ref[pl.ds(base, ch), :]

    def pass1(n, carry):
        base = pl.multiple_of(n * ch, ch)
        zc = z_ref[pl.ds(base, ch), :]
        q = rope(zc[:, 0:RET_WIDTH], base)
        k = rope(zc[:, RET_WIDTH:2 * RET_WIDTH], base) * (RET_DIM ** -0.5)
        q_s[pl.ds(base, ch), :] = q.astype(BF16)
        k_s[pl.ds(base, ch), :] = k.astype(BF16)
        v = zc[:, 2 * RET_WIDTH:3 * RET_WIDTH]
        for p in range(RET_NPAIR):
            lanes = slice(p * pw, (p + 1) * pw)
            kp = k[:, lanes]
            k2 = (jnp.concatenate([kp, kp], axis=-1) * k_dec[p]).astype(BF16)
            kv = lax.dot_general(k2, v[:, lanes], (((0,), (0,)), ((), ())), preferred_element_type=F32)
            kv_s[n, p] = jnp.where(same_head2, kv, 0.0)
        return carry

    lax.fori_loop(0, nct, pass1, 0, unroll=min(RET_UNROLL, nct))

    def block_diag(a, b):
        z = jnp.zeros((hd, hd), F32)
        return jnp.concatenate([jnp.concatenate([a, z], axis=1), jnp.concatenate([z, b], axis=1)], axis=0)

    for sq in range(group):
        for p in range(RET_NPAIR):
            if latent:
                s_f = block_diag(s0f_ref[2 * p], s0f_ref[2 * p + 1])
                s_b = block_diag(s0b_ref[2 * p], s0b_ref[2 * p + 1])
            else:
                s_f = s_b = jnp.zeros((pw, pw), F32)

            def fwd(i, s, p=p, first=sq * nc):
                n = first + i
                st_s[n, p, 0:pw, :] = s.astype(BF16)
                return c_dec_f[p] * s + kv_s[n, p, 0:pw, :]

            def bwd(i, s, p=p, final=sq * nc + nc - 1):
                n = final - i
                st_s[n, p, pw:2 * pw, :] = s.astype(BF16)
                return c_dec_b[p] * s + kv_s[n, p, pw:2 * pw, :]

            s_f = lax.fori_loop(0, nc, fwd, s_f)
            s_b = lax.fori_loop(0, nc, bwd, s_b)
            if not latent:
                for hh in range(2):
                    blk = slice(hh * hd, (hh + 1) * hd)
                    sf_ref[sq, 2 * p + hh] = s_f[blk, blk]
                    sb_ref[sq, 2 * p + hh] = s_b[blk, blk]

    def pass3(n, carry):
        base = pl.multiple_of(n * ch, ch)
        zc = z_ref[pl.ds(base, ch), :]
        q = q_s[pl.ds(base, ch), :]
        k = k_s[pl.ds(base, ch), :]
        v = zc[:, 2 * RET_WIDTH:3 * RET_WIDTH]
        gate = zc[:, 3 * RET_WIDTH:4 * RET_WIDTH].astype(F32)
        outs = []
        for p in range(RET_NPAIR):
            lanes = slice(p * pw, (p + 1) * pw)
            qp, kp, vp = q[:, lanes], k[:, lanes], v[:, lanes]
            o_h = []
            for hh in range(2):
                qm = jnp.where(left == (hh == 0), qp, jnp.zeros_like(qp))
                s = _dot_nt(qm, kp) * decay[2 * p + hh]
                o_h.append(jnp.dot(s.astype(BF16), vp, preferred_element_type=F32))
            qf = qp.astype(F32)
            q2 = (jnp.concatenate([qf, qf], axis=-1) * q_dec[p]).astype(BF16)
            o = jnp.where(left, o_h[0], o_h[1]) + jnp.dot(q2, st_s[n, p], preferred_element_type=F32)

            def half_mean(t):
                s_l = jnp.sum(jnp.where(left, t, 0.0), axis=-1, keepdims=True)
                s_r = jnp.sum(jnp.where(left, 0.0, t), axis=-1, keepdims=True)
                return jnp.where(left, s_l, s_r) * (1.0 / hd)

            d = o - half_mean(o)
            outs.append(d * lax.rsqrt(half_mean(d * d) + EPS))
        nrm = jnp.concatenate(outs, axis=-1)
        y_ref[pl.ds(base, ch), :] = (nrm * gn_ref[...] * (gate * _sigmoid(gate))).astype(BF16)
        return carry

    lax.fori_loop(0, nct, pass3, 0, unroll=min(RET_UNROLL, nct))


def _retention(z, lg, gn_g, latent, layer=None, rope=None, s0_f=None, s0_b=None):
    seq = DEC_SEQ if latent else SEQ
    nseq = DEC_BATCH if latent else BATCH
    group = 1 if latent else RET_CTX_GROUP
    rows = group * seq
    nct = rows // RET_CHUNK
    row0 = (T_CTX // DEC_SEQ) if latent else 0
    cb = COL_RET // (4 * RET_WIDTH)
    in_specs = [pl.BlockSpec(memory_space=pltpu.SMEM),
                pl.BlockSpec((rows, 4 * RET_WIDTH), lambda s: (row0 + s, cb)),
                _full_spec((1, RET_WIDTH))]
    args = [lg, z, gn_g]
    state_shape = jax.ShapeDtypeStruct((nseq, RET_HEADS, RET_DIM, RET_DIM), F32)
    y_spec = pl.BlockSpec((rows, RET_WIDTH), lambda s: (s, 0))
    y_shape = jax.ShapeDtypeStruct((nseq * seq, RET_WIDTH), BF16)
    if latent:
        st_spec = pl.BlockSpec((None, None, RET_HEADS, RET_DIM, RET_DIM), lambda s: (s, layer, 0, 0, 0))

        def const_spec(shape):
            return pl.BlockSpec(shape, lambda s: (0,) * len(shape), pipeline_mode=pl.Buffered(1))

        in_specs += [const_spec((seq, RET_WIDTH)), const_spec((seq, RET_WIDTH)),
                     const_spec((RET_WIDTH, RET_WIDTH)), st_spec, st_spec]
        args += [rope[0], rope[1], rope[2], s0_f, s0_b]
        out_specs, out_shape = y_spec, y_shape
    else:
        so_spec = pl.BlockSpec((group, RET_HEADS, RET_DIM, RET_DIM), lambda s: (s, 0, 0, 0))
        out_specs, out_shape = [y_spec, so_spec, so_spec], [y_shape, state_shape, state_shape]
    return pl.pallas_call(
        functools.partial(_ret_kernel, seq, group, latent),
        grid=(nseq // group,),
        in_specs=in_specs,
        out_specs=out_specs,
        out_shape=out_shape,
        scratch_shapes=[pltpu.VMEM((rows, RET_WIDTH), BF16), pltpu.VMEM((rows, RET_WIDTH), BF16),
                        pltpu.VMEM((nct, RET_NPAIR, 2 * RET_PAIR, RET_PAIR), F32),
                        pltpu.VMEM((nct, RET_NPAIR, 2 * RET_PAIR, RET_PAIR), BF16)],
        compiler_params=_cparams(("arbitrary",)),
        name="retention_lat" if latent else "retention_ctx",
    )(*args)


def _route(logits):
    lane = lax.broadcasted_iota(jnp.int32, logits.shape, 1)
    lane_f = lane.astype(F32)
    big = float(ROUTE_COLS)
    neg = -jnp.inf
    is_grp = lane < N_GROUPS
    gl = jnp.where(is_grp, logits, neg)
    gmax = jnp.max(gl, axis=-1, keepdims=True)
    grp = jnp.min(jnp.where(gl == gmax, lane_f, big), axis=-1, keepdims=True)
    p_grp = 1.0 / jnp.sum(jnp.exp(gl - gmax), axis=-1, keepdims=True)
    e_f = lane_f - N_GROUPS
    lo = grp * EXPERTS_PER_GROUP
    in_grp = (e_f >= lo) & (e_f < lo + EXPERTS_PER_GROUP)
    el = jnp.where(in_grp, logits, neg)
    m1 = jnp.max(el, axis=-1, keepdims=True)
    i1 = jnp.min(jnp.where(el == m1, lane_f, big), axis=-1, keepdims=True)
    el2 = jnp.where(lane_f == i1, neg, el)
    m2 = jnp.max(el2, axis=-1, keepdims=True)
    i2 = jnp.min(jnp.where(el2 == m2, lane_f, big), axis=-1, keepdims=True)
    t = jnp.exp(m2 - m1)
    g1 = p_grp / (1.0 + t)
    g2 = p_grp * t / (1.0 + t)
    rows = logits.shape[0]
    oh1, oh2 = lane_f == i1, lane_f == i2
    oh = jnp.where(oh1 | oh2, 1.0, 0.0)
    tri = (lax.broadcasted_iota(jnp.int32, (rows, rows), 0)
           > lax.broadcasted_iota(jnp.int32, (rows, rows), 1))
    rank = jnp.dot(jnp.where(tri, 1.0, 0.0).astype(BF16), oh.astype(BF16), preferred_element_type=F32)
    tiles = jnp.floor((jnp.sum(oh, axis=0, keepdims=True) + (SUBLANES - 1)) * (1.0 / SUBLANES))
    upper = (lax.broadcasted_iota(jnp.int32, (ROUTE_COLS, ROUTE_COLS), 0)
             < lax.broadcasted_iota(jnp.int32, (ROUTE_COLS, ROUTE_COLS), 1))
    start = SUBLANES * jnp.dot(jnp.broadcast_to(tiles, (SUBLANES, ROUTE_COLS)).astype(BF16),
                               jnp.where(upper, 1.0, 0.0).astype(BF16),
                               preferred_element_type=F32)[0:1, :]
    pos = start + rank
    p1 = jnp.sum(jnp.where(oh1, pos, 0.0), axis=-1, keepdims=True)
    p2 = jnp.sum(jnp.where(oh2, pos, 0.0), axis=-1, keepdims=True)
    out = jnp.zeros(logits.shape, F32)
    for k, val in enumerate((i1 - N_GROUPS, i2 - N_GROUPS, g1, g2, p1, p2)):
        out = jnp.where(lane == k, val, out)
    return out, SUBLANES * tiles


def _outproj_kernel(ycc, ycl, ync, ynl, yrc, yrl, x_ref, mod_ref, g_ref, w_ref, wr_ref, br_ref,
                    xo_ref, xs_ref, r_ref, seg_ref, wb_ref):
    _cast_weights_once(w_ref, wb_ref)
    is_ctx = pl.program_id(0) < NB_CTX
    yc = jnp.where(is_ctx, ycc[...], ycl[...])
    yn = jnp.where(is_ctx, ync[...], ynl[...])
    yr = jnp.where(is_ctx, yrc[...], yrl[...])
    y = (jnp.dot(yc, wb_ref[0:CONV_CH, :], preferred_element_type=F32)
         + jnp.dot(yn, wb_ref[CONV_CH:CONV_CH + NA_WIDTH, :], preferred_element_type=F32)
         + jnp.dot(yr, wb_ref[CONV_CH + NA_WIDTH:, :], preferred_element_type=F32))
    x = x_ref[...] + mod_ref[2:3, :] * y
    xo_ref[...] = x
    h = _norm_mod(x, g_ref[...], mod_ref[3:4, :], mod_ref[4:5, :])
    h_hi = h.astype(BF16)
    h_lo = (h - h_hi.astype(F32)).astype(BF16)
    hw = jnp.dot(h_hi, wr_ref[...], preferred_element_type=F32)
    logits = (hw[:, :ROUTE_COLS] + hw[:, ROUTE_COLS:]
              + jnp.dot(h_lo, wr_ref[:, :ROUTE_COLS], preferred_element_type=F32) + br_ref[...])
    route, seg = _route(logits)
    r_ref[...] = route
    seg_ref[...] = jnp.broadcast_to(seg, seg_ref.shape)
    sel = _slot_onehot(route, 0) | _slot_onehot(route, 1)
    xs_ref[...] = _pack_bf16_pairs(lax.dot_general(jnp.where(sel, 1.0, 0.0).astype(BF16), h_hi,
                                                   (((0,), (0,)), ((), ())), preferred_element_type=F32))


def _outproj(y_conv, y_na, y_ret, x, mod, g, w_out, layer, w_route, b_route):
    return pl.pallas_call(
        _outproj_kernel,
        grid=(NB_ALL,),
        in_specs=(_ctx_lat_specs(CONV_CH) + _ctx_lat_specs(NA_WIDTH) + _ctx_lat_specs(RET_WIDTH)
                  + [_tok_spec(D_MODEL), _mod_spec(), _full_spec((1, D_MODEL)),
                     _layer_weight_spec(layer, D_MODEL, D_MODEL), _full_spec((D_MODEL, 2 * ROUTE_COLS)),
                     _full_spec((1, ROUTE_COLS))]),
        out_specs=[_tok_spec(D_MODEL), pl.BlockSpec((MOE_LC, D_HALF), lambda i: (i, 0)),
                   _tok_spec(ROUTE_COLS), pl.BlockSpec((None, SUBLANES, ROUTE_COLS), lambda i: (i, 0, 0))],
        out_shape=[jax.ShapeDtypeStruct((T_ALL, D_MODEL), F32),
                   jax.ShapeDtypeStruct((NB_ALL * MOE_LC, D_HALF), U32),
                   jax.ShapeDtypeStruct((T_ALL, ROUTE_COLS), F32),
                   jax.ShapeDtypeStruct((NB_ALL, SUBLANES, ROUTE_COLS), F32)],
        scratch_shapes=[pltpu.VMEM((D_MODEL, D_MODEL), BF16)],
        compiler_params=_cparams(("arbitrary",)),
        name="outproj_route",
    )(y_conv[0], y_conv[1], y_na[0], y_na[1], y_ret[0], y_ret[1], x, mod, g, w_out, w_route, b_route)


def _dispatch_tables(seg):
    seg_len = seg[:, 0, N_GROUPS:N_GROUPS + N_EXPERTS].astype(jnp.int32)
    experts = jnp.arange(N_EXPERTS, dtype=jnp.int32)
    in_chunk = jnp.cumsum(seg_len, axis=1) - seg_len
    seg_row = in_chunk + MOE_LC * jnp.arange(N_CHUNK, dtype=jnp.int32)[:, None]
    seg_off = jnp.cumsum(seg_len, axis=0) - seg_len
    rows_e = jnp.sum(seg_len, axis=0)
    chunk_rows = jnp.sum(seg_len, axis=1)
    nblk = (rows_e + MOE_BLK - 1) // MOE_BLK
    blk_end = jnp.cumsum(nblk)
    blk_start = blk_end - nblk
    blk = jnp.arange(MOE_NBLK, dtype=jnp.int32)
    n_active = blk_end[-1]
    blk_e = jnp.minimum(jnp.sum((blk_end[None, :] <= jnp.minimum(blk, n_active - 1)[:, None]).astype(jnp.int32),
                                axis=-1), N_EXPERTS - 1)
    mine = blk_e[:, None] == experts[None, :]
    blk_lo = (blk - jnp.sum(jnp.where(mine, blk_start[None, :], 0), axis=-1)) * MOE_BLK
    left = jnp.sum(jnp.where(mine, rows_e[None, :], 0), axis=-1) - blk_lo
    blk_nv = jnp.where(blk < n_active, jnp.clip(left, 0, MOE_BLK), 0).astype(jnp.int32)
    off_b = jnp.sum(jnp.where(mine[:, None, :], seg_off[None, :, :], 0), axis=-1)
    end_b = off_b + jnp.sum(jnp.where(mine[:, None, :], seg_len[None, :, :], 0), axis=-1)
    blk_c0 = jnp.sum((end_b <= blk_lo[:, None]).astype(jnp.int32), axis=-1)
    blk_c1 = jnp.sum((off_b < (blk_lo + blk_nv)[:, None]).astype(jnp.int32), axis=-1)
    after = jnp.sum(jnp.where(mine, blk_end[None, :], 0), axis=-1)
    blk_next_e = jnp.where(after < n_active, jnp.take(blk_e, jnp.minimum(after, MOE_NBLK - 1)), -1)
    row_b = jnp.sum(jnp.where(mine[:, None, :], seg_row[None, :, :], 0), axis=-1)
    first = jnp.maximum(off_b, blk_lo[:, None])
    piece_n = jnp.minimum(end_b, (blk_lo + blk_nv)[:, None]) - first
    piece_src = row_b + first - off_b
    piece_dst = first - blk_lo[:, None]
    piece = jnp.where(piece_n > 0,
                      (piece_src // SUBLANES) | ((piece_dst // SUBLANES) << PIECE_DST_SHIFT)
                      | ((piece_n // SUBLANES) << PIECE_N_SHIFT), 0).astype(jnp.int32)
    return (blk_e, blk_next_e.astype(jnp.int32), blk_nv, blk_c0, blk_c1, piece.reshape(-1), chunk_rows)


PIECE_DST_SHIFT = (N_CHUNK * MOE_LC // SUBLANES - 1).bit_length()
PIECE_N_SHIFT = PIECE_DST_SHIFT + (MOE_BLK // SUBLANES - 1).bit_length()
assert PIECE_N_SHIFT + (TM // SUBLANES).bit_length() <= 31


def _moe_kernel(layer, blk_e, blk_next_e, blk_nv, blk_c0, blk_c1, piece,
                chunk_rows, xs_hbm, w1_hbm, w3_hbm, w2_hbm, ys_hbm, xbuf, obuf, zeros,
                w1f, w3f, w2f, w1b, w3b, w2b, gsem, ssem, zsem, wsem):
    i = pl.program_id(0)
    last = pl.num_programs(0) - 1
    slot = i % 2

    def tiles(v):
        return pl.multiple_of(v, SUBLANES)

    def for_segments(blk, fn):
        def body(c, carry):
            word = piece[blk * N_CHUNK + c]
            src = (word & ((1 << PIECE_DST_SHIFT) - 1)) * SUBLANES
            dst = ((word >> PIECE_DST_SHIFT) & ((1 << (PIECE_N_SHIFT - PIECE_DST_SHIFT)) - 1)) * SUBLANES
            n = (word >> PIECE_N_SHIFT) * SUBLANES

            @pl.when(word != 0)
            def _():
                fn(tiles(src), tiles(dst), tiles(n))

            return carry

        lax.fori_loop(blk_c0[blk], blk_c1[blk], body, 0)

    def weight_copies(e):
        return [pltpu.make_async_copy(src.at[layer, e], dst, wsem)
                for src, dst in ((w1_hbm, w1f), (w3_hbm, w3f), (w2_hbm, w2f))]

    def start_gathers(blk, s):
        for_segments(blk, lambda src, dst, n: pltpu.make_async_copy(
            xs_hbm.at[pl.ds(src, n)], xbuf.at[s, pl.ds(dst, n)], gsem.at[s]).start())

    def start_scatters(blk, s):
        for_segments(blk, lambda dst, src, n: pltpu.make_async_copy(
            obuf.at[s, pl.ds(src, n)], ys_hbm.at[pl.ds(dst, n)], ssem.at[s]).start())

    def wait_rows(blk, s, sem):
        n = tiles(blk_nv[blk])

        @pl.when(n > 0)
        def _():
            pltpu.make_async_copy(xs_hbm.at[pl.ds(0, n)], xbuf.at[s, pl.ds(0, n)], sem.at[s]).wait()

    @pl.when(i == 0)
    def _():
        xbuf[...] = jnp.zeros_like(xbuf)
        zeros[...] = jnp.zeros_like(zeros)

        def tail(c):
            n = tiles(MOE_LC - chunk_rows[c])
            return n, pltpu.make_async_copy(zeros.at[pl.ds(0, n)],
                                            ys_hbm.at[pl.ds(tiles(c * MOE_LC + chunk_rows[c]), n)], zsem)

        def fill(c, carry):
            n, copy = tail(c)
            pl.when(n > 0)(copy.start)
            return carry

        def drain(c, carry):
            n, copy = tail(c)
            pl.when(n > 0)(copy.wait)
            return carry

        lax.fori_loop(0, N_CHUNK, fill, 0)
        lax.fori_loop(0, N_CHUNK, drain, 0)
        start_gathers(0, 0)
        for copy in weight_copies(blk_e[0]):
            copy.start()

    @pl.when(i < last)
    def _():
        start_gathers(i + 1, 1 - slot)

    @pl.when(i >= 2)
    def _():
        wait_rows(i - 2, slot, ssem)

    @pl.when(blk_nv[i] > 0)
    def _():
        @pl.when((i == 0) | (blk_e[i] != blk_e[jnp.maximum(i - 1, 0)]))
        def _():
            for copy in weight_copies(blk_e[i]):
                copy.wait()
            w1b[...] = w1f[...].astype(BF16)
            w3b[...] = w3f[...].astype(BF16)
            w2b[...] = w2f[...].astype(BF16)

            @pl.when(blk_next_e[i] >= 0)
            def _():
                for copy in weight_copies(blk_next_e[i]):
                    copy.start()

        wait_rows(i, slot, gsem)

        def expert_mlp(rows):
            x_lo, x_hi = _unpack_bf16_pairs(xbuf[slot, 0:rows, :])
            n_hid = D_EXPERT // MXU_TILE

            def in_dot(w, t):
                cols = slice(t * MXU_TILE, (t + 1) * MXU_TILE)
                return (jnp.dot(x_lo, w[:D_HALF, cols], preferred_element_type=F32)
                        + jnp.dot(x_hi, w[D_HALF:, cols], preferred_element_type=F32))

            ab = [(in_dot(w1b, t), in_dot(w3b, t)) for t in range(n_hid)]
            mid = [(a * _sigmoid(a) * b).astype(BF16) for a, b in ab]

            def out_dot(t):
                cols = slice(t * MXU_TILE, (t + 1) * MXU_TILE)
                return sum(jnp.dot(mid[j], w2b[j * MXU_TILE:(j + 1) * MXU_TILE, cols],
                                   preferred_element_type=F32) for j in range(n_hid))

            n_word = D_HALF // MXU_TILE
            for t in range(n_word):
                obuf[slot, 0:rows, t * MXU_TILE:(t + 1) * MXU_TILE] = _pack_words(
                    out_dot(t), out_dot(t + n_word))

        for rows in range(MOE_ROW_STEP, MOE_BLK + 1, MOE_ROW_STEP):
            @pl.when((blk_nv[i] > rows - MOE_ROW_STEP) & (blk_nv[i] <= rows))
            def _(rows=rows):
                expert_mlp(rows)

        start_scatters(i, slot)

    @pl.when(i == last)
    def _():
        wait_rows(i - 1, 1 - slot, ssem)
        wait_rows(i, slot, ssem)


def _moe(xs, w1, w3, w2, layer, blk_e, blk_next_e, blk_nv, blk_c0, blk_c1, piece, chunk_rows):
    any_spec = pl.BlockSpec(memory_space=pl.ANY)
    grid_spec = pltpu.PrefetchScalarGridSpec(
        num_scalar_prefetch=7,
        grid=(MOE_NBLK,),
        in_specs=[any_spec, any_spec, any_spec, any_spec],
        out_specs=any_spec,
        scratch_shapes=[pltpu.VMEM((2, MOE_BLK, D_HALF), U32), pltpu.VMEM((2, MOE_BLK, D_HALF), U32),
                        pltpu.VMEM((MOE_LC - 2 * TM, D_HALF), U32),
                        pltpu.VMEM((D_MODEL, D_EXPERT), F32), pltpu.VMEM((D_MODEL, D_EXPERT), F32),
                        pltpu.VMEM((D_EXPERT, D_MODEL), F32),
                        pltpu.VMEM((D_MODEL, D_EXPERT), BF16), pltpu.VMEM((D_MODEL, D_EXPERT), BF16),
                        pltpu.VMEM((D_EXPERT, D_MODEL), BF16),
                        pltpu.SemaphoreType.DMA((2,)), pltpu.SemaphoreType.DMA((2,)),
                        pltpu.SemaphoreType.DMA, pltpu.SemaphoreType.DMA])
    return pl.pallas_call(
        functools.partial(_moe_kernel, layer),
        grid_spec=grid_spec,
        out_shape=jax.ShapeDtypeStruct((NB_ALL * MOE_LC, D_HALF), U32),
        compiler_params=_cparams(("arbitrary",)),
        name="moe_experts",
    )(blk_e, blk_next_e, blk_nv, blk_c0, blk_c1, piece, chunk_rows, xs, w1, w3, w2)


def _final_kernel(x_ref, ys_ref, r_ref, mod_ref, g_ref, o_ref):
    x = _moe_residual(x_ref, ys_ref, r_ref, mod_ref)
    ms = jnp.mean(x * x, axis=-1, keepdims=True)
    o_ref[...] = x * lax.rsqrt(ms + EPS) * g_ref[...]


def _final(x, ys, route, mod, g, block0, nblocks):
    return pl.pallas_call(
        _final_kernel,
        grid=(nblocks,),
        in_specs=[pl.BlockSpec((TM, D_MODEL), lambda i: (block0 + i, 0)),
                  pl.BlockSpec((MOE_LC, D_HALF), lambda i: (block0 + i, 0)),
                  pl.BlockSpec((TM, ROUTE_COLS), lambda i: (block0 + i, 0)),
                  pl.BlockSpec((None, 6, D_MODEL), lambda i: (_cond_row(block0 + i), 0, 0)),
                  _full_spec((1, D_MODEL))],
        out_specs=_tok_spec(D_MODEL),
        out_shape=jax.ShapeDtypeStruct((nblocks * TM, D_MODEL), F32),
        compiler_params=_cparams(("arbitrary",)),
        name="final_norm",
    )(x, ys, route, mod, g)


def kernel(x_prompt, x_sample, c, cache_k, cache_v, state_ret_f, state_ret_b, c_ctx, w_ada, b_ada, norm1_g, norm2_g, w_in, w_out, conv_w, conv_b, conv_ln_g, conv_ln_b, na_rpb, ret_lg_f, ret_lg_b, ret_gn_g, w_route_g, b_route_g, w_route_e, b_route_e, w1, w3, w2, final_g):
    cv = jnp.zeros((COND_ROWS, D_MODEL), F32).at[0].set(c_ctx).at[1:N_COND].set(c)
    mods = _ada(cv, w_ada, b_ada).reshape(DEPTH, COND_ROWS, 6, D_MODEL)
    pad = ROUTE_COLS - N_GROUPS - N_EXPERTS
    w_route = jnp.pad(jnp.concatenate([w_route_g, w_route_e], axis=-1), ((0, 0), (0, 0), (0, pad)))
    b_route = jnp.pad(jnp.concatenate([b_route_g, b_route_e], axis=-1), ((0, 0), (0, pad)))
    w_route_hi = w_route.astype(BF16)
    w_route_lo = (w_route - w_route_hi.astype(F32)).astype(BF16)
    w_route = jnp.concatenate([w_route_hi, w_route_lo], axis=-1)
    na_bias = _na_bias_tables(na_rpb)
    rope = _rope_tables()
    lg = jnp.stack([ret_lg_f, ret_lg_b], axis=1)

    x_ctx = x_prompt.reshape(T_CTX, D_MODEL)
    x_lat = x_sample.reshape(T_LAT, D_MODEL)
    x = y = route = new_k = new_v = None
    sf_list, sb_list = [], []
    for l in range(DEPTH):
        g1 = norm1_g[l].reshape(1, D_MODEL)
        if l == 0:
            z, x = _inproj_first(x_ctx, x_lat, mods[l], g1, w_in, l)
        else:
            z, x = _inproj_next(x, y, route, mods[l - 1], mods[l], g1, w_in, l)
        conv_args = (conv_w[l], conv_b[l].reshape(1, -1), conv_ln_g[l].reshape(1, -1),
                     conv_ln_b[l].reshape(1, -1))
        yc_c = _conv(z, 0, BATCH, SEQ, *conv_args)
        yc_l = _conv(z, T_CTX // DEC_SEQ, DEC_BATCH, DEC_SEQ, *conv_args)
        yn_c, new_k, new_v = _ctx_attn(z, l, new_k, new_v)
        yn_l = _na_attn(z, cache_k, cache_v, na_bias, l)
        gn = ret_gn_g[l].reshape(1, RET_WIDTH)
        yr_c, sf_l, sb_l = _retention(z, lg[l], gn, latent=False)
        yr_l = _retention(z, lg[l], gn, latent=True, layer=l, rope=rope,
                          s0_f=state_ret_f, s0_b=state_ret_b)
        x, xs, route, seg = _outproj((yc_c, yc_l), (yn_c, yn_l), (yr_c, yr_l), x, mods[l],
                                     norm2_g[l].reshape(1, D_MODEL), w_out, l, w_route[l],
                                     b_route[l].reshape(1, ROUTE_COLS))
        y = _moe(xs, w1, w3, w2, l, *_dispatch_tables(seg))
        sf_list.append(sf_l)
        sb_list.append(sb_l)
    fg = final_g.reshape(1, D_MODEL)
    y_prompt = _final(x, y, route, mods[DEPTH - 1], fg, 0, NB_CTX).reshape(BATCH, SEQ, D_MODEL)
    y_sample = _final(x, y, route, mods[DEPTH - 1], fg, NB_CTX, NB_LAT).reshape(DEC_BATCH, DEC_SEQ, D_MODEL)
    return (y_prompt, y_sample, new_k, new_v, jnp.stack(sf_list, axis=1), jnp.stack(sb_list, axis=1))
```

```python
import functools

import numpy as np
import jax
import jax.numpy as jnp
from jax import lax
from jax.experimental import pallas as pl
from jax.experimental.pallas import tpu as pltpu

D_MODEL = 1024
BATCH = 32
SEQ = 256
DEPTH = 2
DEC_BATCH = 4
DEC_SEQ = 4096
PAST_LEN = 512
GRID_W = 64
GRID_H = DEC_SEQ // GRID_W
CONV_CH = 256
CONV_K = 31
NA_HEADS = 8
NA_DIM = 64
NA_WIDTH = NA_HEADS * NA_DIM
NA_KH = 8
NA_KW = 16
RET_HEADS = 4
RET_DIM = 64
RET_WIDTH = RET_HEADS * RET_DIM
RET_CHUNK = 128
ROPE_BASE = 10000.0
N_GROUPS = 4
EXPERTS_PER_GROUP = 8
N_EXPERTS = N_GROUPS * EXPERTS_PER_GROUP
D_EXPERT = 512
IN_COLS = 2 * CONV_CH + 3 * NA_WIDTH + 4 * RET_WIDTH
EPS = 1e-6
NEG_INF = -1e30

F32 = jnp.float32
BF16 = jnp.bfloat16
HIGHEST = lax.Precision.HIGHEST

T_CTX = BATCH * SEQ
T_LAT = DEC_BATCH * DEC_SEQ
T_ALL = T_CTX + T_LAT
N_COND = 1 + DEC_BATCH
COND_ROWS = 8

TM = 512
NB_CTX = T_CTX // TM
NB_LAT = T_LAT // TM
NB_ALL = NB_CTX + NB_LAT
LAT_BLOCKS_PER_REQ = DEC_SEQ // TM

LANES = 128
SUBLANES = 8
MXU_TILE = 256
ROUTE_COLS = LANES

COL_CONV = 0
COL_NA_Q = 2 * CONV_CH
COL_NA_K = COL_NA_Q + NA_WIDTH
COL_NA_V = COL_NA_K + NA_WIDTH
COL_RET = COL_NA_V + NA_WIDTH

NA_ROWS = 8
NA_Q = NA_ROWS * GRID_W
NA_KROWS = NA_ROWS + NA_KH
NA_KEYS = NA_KROWS * GRID_W
NA_RB = GRID_H // NA_ROWS

MOE_BLK = 1024
MOE_ROW_STEP = 128
MOE_LC = -(-(2 * TM + N_EXPERTS * (SUBLANES - 1)) // LANES) * LANES
N_CHUNK = NB_ALL
MOE_NBLK = -(-(N_CHUNK * MOE_LC) // MOE_BLK) + N_EXPERTS

VMEM_LIMIT = 56 * 1024 * 1024


def _cparams(sem):
    return pltpu.CompilerParams(dimension_semantics=sem, vmem_limit_bytes=VMEM_LIMIT)


def _sigmoid(x):
    return 1.0 / (1.0 + jnp.exp(-x))


def _cond_row(i):
    return jnp.where(i < NB_CTX, 0, 1 + (i - NB_CTX) // LAT_BLOCKS_PER_REQ)


ADA_TN = 1536


def _ada_kernel(cv_ref, w_ref, b_ref, o_ref):
    cv = cv_ref[...]
    s = cv * _sigmoid(cv)
    o_ref[...] = jnp.dot(s, w_ref[...], precision=HIGHEST, preferred_element_type=F32) + b_ref[...]


def _ada(cv, w_ada, b_ada):
    n = 6 * D_MODEL
    return pl.pallas_call(
        _ada_kernel,
        grid=(DEPTH, n // ADA_TN),
        in_specs=[
            pl.BlockSpec((COND_ROWS, D_MODEL), lambda l, j: (0, 0)),
            pl.BlockSpec((None, D_MODEL, ADA_TN), lambda l, j: (l, 0, j)),
            pl.BlockSpec((None, 1, ADA_TN), lambda l, j: (l, 0, j)),
        ],
        out_specs=pl.BlockSpec((None, COND_ROWS, ADA_TN), lambda l, j: (l, 0, j)),
        out_shape=jax.ShapeDtypeStruct((DEPTH, COND_ROWS, n), F32),
        compiler_params=_cparams(("arbitrary", "arbitrary")),
        name="ada_mod",
    )(cv, w_ada, b_ada.reshape(DEPTH, 1, n))


IN_TN = 768


def _norm_mod(x, g, shift, scale):
    ms = jnp.mean(x * x, axis=-1, keepdims=True)
    return (x * lax.rsqrt(ms + EPS) * g) * (1.0 + scale) + shift


def _cast_weights_once(w_ref, wb_ref):
    @pl.when(pl.program_id(0) == 0)
    def _():
        wb_ref[...] = w_ref[...].astype(BF16)


def _layer_weight_spec(layer, rows, cols):
    return pl.BlockSpec((None, rows, cols), lambda i: (layer, 0, 0), pipeline_mode=pl.Buffered(1))


def _inproj_body(x, mod_ref, g_ref, w_ref, wb_ref, z_ref):
    _cast_weights_once(w_ref, wb_ref)
    h = _norm_mod(x, g_ref[...], mod_ref[0:1, :], mod_ref[1:2, :]).astype(BF16)
    for c in range(IN_COLS // IN_TN):
        cols = slice(c * IN_TN, (c + 1) * IN_TN)
        z_ref[:, cols] = jnp.dot(h, wb_ref[:, cols], preferred_element_type=F32).astype(BF16)


def _inproj_first_kernel(xc_ref, xl_ref, mod_ref, g_ref, w_ref, z_ref, xo_ref, wb_ref):
    i = pl.program_id(0)
    x = jnp.where(i < NB_CTX, xc_ref[...], xl_ref[...])
    xo_ref[...] = x
    _inproj_body(x, mod_ref, g_ref, w_ref, wb_ref, z_ref)


U32 = jnp.uint32
D_HALF = D_MODEL // 2
_HI_MASK = np.uint32(0xFFFF0000)


def _pack_words(lo, hi):
    lo = lax.bitcast_convert_type(lo.astype(BF16).astype(F32), U32) >> 16
    hi = lax.bitcast_convert_type(hi.astype(BF16).astype(F32), U32) & _HI_MASK
    return lo | hi


def _pack_bf16_pairs(x):
    return _pack_words(x[:, :D_HALF], x[:, D_HALF:])


def _unpack_bf16_pairs(w):
    lo = lax.bitcast_convert_type(w << 16, F32).astype(BF16)
    hi = lax.bitcast_convert_type(w & _HI_MASK, F32).astype(BF16)
    return lo, hi


def _slot_onehot(route, slot):
    pos = route[:, 4 + slot:5 + slot].astype(jnp.int32)
    return lax.broadcasted_iota(jnp.int32, (route.shape[0], MOE_LC), 1) == pos


def _moe_residual(x_ref, ys_ref, r_ref, mod_ref):
    r = r_ref[...]
    sel = jnp.where(_slot_onehot(r, 0), r[:, 2:3], jnp.where(_slot_onehot(r, 1), r[:, 3:4], 0.0))
    sel = sel.astype(BF16)
    y = jnp.concatenate([jnp.dot(sel, half, preferred_element_type=F32)
                         for half in _unpack_bf16_pairs(ys_ref[...])], axis=-1)
    return x_ref[...] + mod_ref[5:6, :] * y


def _inproj_next_kernel(x_ref, ys_ref, r_ref, modp_ref, mod_ref, g_ref, w_ref, z_ref, xo_ref, wb_ref):
    x = _moe_residual(x_ref, ys_ref, r_ref, modp_ref)
    xo_ref[...] = x
    _inproj_body(x, mod_ref, g_ref, w_ref, wb_ref, z_ref)


def _tok_spec(cols):
    return pl.BlockSpec((TM, cols), lambda i: (i, 0))


def _mod_spec():
    return pl.BlockSpec((None, 6, D_MODEL), lambda i: (_cond_row(i), 0, 0))


def _full_spec(shape):
    return pl.BlockSpec(shape, lambda i: (0,) * len(shape))


def _ctx_lat_specs(cols):
    return [pl.BlockSpec((TM, cols), lambda i: (jnp.minimum(i, NB_CTX - 1), 0)),
            pl.BlockSpec((TM, cols), lambda i: (jnp.maximum(i - NB_CTX, 0), 0))]


def _inproj_first(x_ctx, x_lat, mod, g, w_in, layer):
    return pl.pallas_call(
        _inproj_first_kernel,
        grid=(NB_ALL,),
        in_specs=_ctx_lat_specs(D_MODEL) + [_mod_spec(), _full_spec((1, D_MODEL)),
                                            _layer_weight_spec(layer, D_MODEL, IN_COLS)],
        out_specs=[_tok_spec(IN_COLS), _tok_spec(D_MODEL)],
        out_shape=[jax.ShapeDtypeStruct((T_ALL, IN_COLS), BF16),
                   jax.ShapeDtypeStruct((T_ALL, D_MODEL), F32)],
        scratch_shapes=[pltpu.VMEM((D_MODEL, IN_COLS), BF16)],
        compiler_params=_cparams(("arbitrary",)),
        name="inproj_first",
    )(x_ctx, x_lat, mod, g, w_in)


def _inproj_next(x, ys, route, mod_prev, mod, g, w_in, layer):
    return pl.pallas_call(
        _inproj_next_kernel,
        grid=(NB_ALL,),
        in_specs=[_tok_spec(D_MODEL),
                  pl.BlockSpec((MOE_LC, D_HALF), lambda i: (i, 0)),
                  _tok_spec(ROUTE_COLS),
                  _mod_spec(), _mod_spec(), _full_spec((1, D_MODEL)),
                  _layer_weight_spec(layer, D_MODEL, IN_COLS)],
        out_specs=[_tok_spec(IN_COLS), _tok_spec(D_MODEL)],
        out_shape=[jax.ShapeDtypeStruct((T_ALL, IN_COLS), BF16),
                   jax.ShapeDtypeStruct((T_ALL, D_MODEL), F32)],
        scratch_shapes=[pltpu.VMEM((D_MODEL, IN_COLS), BF16)],
        compiler_params=_cparams(("arbitrary",)),
        name="inproj_next",
    )(x, ys, route, mod_prev, mod, g, w_in)


CONV_PAD = 16
CONV_CHUNK = 64


CONV_SPAN = CONV_CHUNK + 2 * CONV_PAD - SUBLANES


CONV_UNROLL = 4


def _conv_kernel(seq, z_ref, w_ref, b_ref, g_ref, be_ref, o_ref, upad_ref, shift_refs):
    zeros = jnp.zeros((CONV_PAD, CONV_CH), F32)
    upad_ref[0:CONV_PAD, :] = zeros
    upad_ref[seq + CONV_PAD:seq + 2 * CONV_PAD, :] = zeros

    def glu(ci, carry):
        base = pl.multiple_of(ci * 256, 256)
        zc = z_ref[pl.ds(base, 256), :].astype(F32)
        upad_ref[pl.ds(base + CONV_PAD, 256), :] = zc[:, :CONV_CH] * _sigmoid(zc[:, CONV_CH:])
        return carry

    lax.fori_loop(0, seq // 256, glu, 0)

    shift = CONV_PAD - CONV_K // 2

    def chunk(ci, shift_ref):
        base = pl.multiple_of(ci * CONV_CHUNK, CONV_CHUNK)
        win = upad_ref[pl.ds(base, CONV_CHUNK + 2 * CONV_PAD), :]
        acc = jnp.zeros((CONV_CHUNK, CONV_CH), F32)
        for sub in range(SUBLANES):
            shift_ref[sub] = win[sub:sub + CONV_SPAN, :]
            for k in range(CONV_K):
                if (k + shift) % SUBLANES == sub:
                    lo = k + shift - sub
                    acc = acc + w_ref[k:k + 1, :] * shift_ref[sub, lo:lo + CONV_CHUNK, :]
        acc = acc + b_ref[...]
        mu = jnp.mean(acc, axis=-1, keepdims=True)
        d = acc - mu
        var = jnp.mean(d * d, axis=-1, keepdims=True)
        n = d * lax.rsqrt(var + EPS) * g_ref[...] + be_ref[...]
        o_ref[pl.ds(base, CONV_CHUNK), :] = (n * _sigmoid(n)).astype(BF16)

    def chunks(cj, carry):
        for u in range(CONV_UNROLL):
            chunk(cj * CONV_UNROLL + u, shift_refs.at[u])
        return carry

    lax.fori_loop(0, seq // (CONV_CHUNK * CONV_UNROLL), chunks, 0)


def _conv(z, row_block0, nseq, seq, w, b, g, be):
    return pl.pallas_call(
        functools.partial(_conv_kernel, seq),
        grid=(nseq,),
        in_specs=[pl.BlockSpec((seq, 2 * CONV_CH), lambda s: (row_block0 + s, 0)),
                  _full_spec((CONV_K, CONV_CH)), _full_spec((1, CONV_CH)),
                  _full_spec((1, CONV_CH)), _full_spec((1, CONV_CH))],
        out_specs=pl.BlockSpec((seq, CONV_CH), lambda s: (s, 0)),
        out_shape=jax.ShapeDtypeStruct((nseq * seq, CONV_CH), BF16),
        scratch_shapes=[pltpu.VMEM((seq + 2 * CONV_PAD, CONV_CH), F32),
                        pltpu.VMEM((CONV_UNROLL, SUBLANES, CONV_SPAN, CONV_CH), F32)],
        compiler_params=_cparams(("arbitrary",)),
        name="conv_seq%d" % seq,
    )(z, w, b, g, be)


def _dot_nt(a, b):
    return lax.dot_general(a, b, (((1,), (1,)), ((), ())), preferred_element_type=F32)


NA_SCALE = NA_DIM ** -0.5
assert NA_SCALE == 2.0 ** round(np.log2(NA_SCALE)), "query pre-scaling assumes a power-of-two scale"


CTX_ATTN_GROUP = 4


def _ctx_attn_kernel(layer, q_ref, k_ref, v_ref, *refs):
    if layer:
        _, _, o_ref, ko_ref, vo_ref = refs
    else:
        o_ref, ko_full, vo_full = refs
    pair = 2 * NA_DIM
    left = lax.broadcasted_iota(jnp.int32, (SEQ, pair), 1) < NA_DIM
    for r in range(CTX_ATTN_GROUP):
        rows = slice(r * SEQ, (r + 1) * SEQ)
        if layer:
            ko, vo = ko_ref.at[r], vo_ref.at[r]
        else:
            ko, vo = ko_full.at[r, 0], vo_full.at[r, 0]
            for j in range(1, DEPTH):
                ko_full[r, j] = jnp.zeros(ko_full.shape[2:], F32)
                vo_full[r, j] = jnp.zeros(vo_full.shape[2:], F32)
        outs = []
        for p in range(NA_HEADS // 2):
            lanes = slice(p * pair, (p + 1) * pair)
            qp, kp, vp = q_ref[rows, lanes], k_ref[rows, lanes], v_ref[rows, lanes]
            kf, vf = kp.astype(F32), vp.astype(F32)
            o_h = []
            for hh in range(2):
                cols = slice(hh * NA_DIM, (hh + 1) * NA_DIM)
                ko[2 * p + hh] = kf[:, cols]
                vo[2 * p + hh] = vf[:, cols]
                qm = jnp.where(left == (hh == 0), qp, jnp.zeros_like(qp))
                s = _dot_nt(qm, kp) * NA_SCALE
                m = jnp.max(s, axis=-1, keepdims=True)
                e = jnp.exp(s - m)
                den = jnp.sum(e, axis=-1, keepdims=True)
                o_h.append(jnp.dot(e.astype(BF16), vp, preferred_element_type=F32) / den)
            outs.append(jnp.where(left, o_h[0], o_h[1]))
        o_ref[rows, :] = jnp.concatenate(outs, axis=-1).astype(BF16)


def _ctx_attn(z, layer, k_prev=None, v_prev=None):
    qb, kb, vb = COL_NA_Q // NA_WIDTH, COL_NA_K // NA_WIDTH, COL_NA_V // NA_WIDTH
    group, rows = CTX_ATTN_GROUP, CTX_ATTN_GROUP * SEQ
    head_shape = jax.ShapeDtypeStruct((BATCH, DEPTH, NA_HEADS, SEQ, NA_DIM), F32)
    head_spec = pl.BlockSpec((group, DEPTH, NA_HEADS, SEQ, NA_DIM), lambda b: (b, 0, 0, 0, 0))
    in_specs = [pl.BlockSpec((rows, NA_WIDTH), lambda b: (b, qb)),
                pl.BlockSpec((rows, NA_WIDTH), lambda b: (b, kb)),
                pl.BlockSpec((rows, NA_WIDTH), lambda b: (b, vb))]
    args = [z, z, z]
    aliases = {}
    if layer:
        any_spec = pl.BlockSpec(memory_space=pl.ANY)
        in_specs += [any_spec, any_spec]
        args += [k_prev, v_prev]
        aliases = {3: 1, 4: 2}
        head_spec = pl.BlockSpec((group, None, NA_HEADS, SEQ, NA_DIM), lambda b: (b, layer, 0, 0, 0))
    return pl.pallas_call(
        functools.partial(_ctx_attn_kernel, layer),
        grid=(BATCH // group,),
        in_specs=in_specs,
        out_specs=[pl.BlockSpec((rows, NA_WIDTH), lambda b: (b, 0)), head_spec, head_spec],
        out_shape=[jax.ShapeDtypeStruct((T_CTX, NA_WIDTH), BF16), head_shape, head_shape],
        input_output_aliases=aliases,
        compiler_params=_cparams(("arbitrary",)),
        name="ctx_attn",
    )(*args)


NA_KINDS = (0, NA_ROWS, GRID_H - NA_ROWS)
N_DR = 2 * NA_KH - 1
N_DC = 2 * NA_KW - 1


def _na_row_offset(r0, i, j):
    ks = min(max(r0 - NA_KH // 2, 0), GRID_H - NA_KROWS)
    r, kr = r0 + i, ks + j
    rs = min(max(r - NA_KH // 2, 0), GRID_H - NA_KH)
    return kr - r + NA_KH - 1 if rs <= kr < rs + NA_KH else None


def _na_bias_kernel(rpb_ref, o_ref):
    lh = pl.program_id(0)
    shape = (GRID_W, 2 * GRID_W)
    qc = lax.broadcasted_iota(jnp.int32, shape, 0)
    lane = lax.broadcasted_iota(jnp.int32, shape, 1)
    kc = lane % GRID_W
    dc = jnp.clip(kc - qc, -(NA_KW - 1), NA_KW - 1) + NA_KW - 1
    cs = jnp.clip(qc - NA_KW // 2, 0, GRID_W - NA_KW)
    col_ok = (kc >= cs) & (kc < cs + NA_KW)
    neg = jnp.full(shape, NEG_INF, F32)
    tiles = []
    for dr in range(N_DR):
        base = (lh * N_DR + dr) * N_DC
        val = jnp.zeros(shape, F32)
        for d in range(N_DC):
            val = jnp.where(dc == d, rpb_ref[base + d], val)
        tiles.append(jnp.where(col_ok, val, neg))
    left = lane < GRID_W
    for kind, r0 in enumerate(NA_KINDS):
        for i in range(NA_ROWS):
            for jp in range(NA_KROWS // 2):
                dl, dr_ = _na_row_offset(r0, i, 2 * jp), _na_row_offset(r0, i, 2 * jp + 1)
                tl = neg if dl is None else tiles[dl]
                tr = neg if dr_ is None else tiles[dr_]
                o_ref[kind, i * GRID_W:(i + 1) * GRID_W, jp * 2 * GRID_W:(jp + 1) * 2 * GRID_W] = (
                    jnp.where(left, tl, tr))


def _na_bias_tables(rpb):
    return pl.pallas_call(
        _na_bias_kernel,
        grid=(DEPTH * NA_HEADS,),
        in_specs=[pl.BlockSpec(memory_space=pltpu.SMEM)],
        out_specs=pl.BlockSpec((None, len(NA_KINDS), NA_Q, NA_KEYS), lambda i: (i, 0, 0, 0)),
        out_shape=jax.ShapeDtypeStruct((DEPTH * NA_HEADS, len(NA_KINDS), NA_Q, NA_KEYS), F32),
        compiler_params=_cparams(("arbitrary",)),
        name="nbr_bias",
    )(rpb.reshape(-1))


NA_G = 4


def _na_kernel(q_ref, k_ref, v_ref, kc_ref, vc_ref, bias_ref, o_ref):
    rb = pl.program_id(2)
    ks = jnp.clip(rb * NA_ROWS - NA_KH // 2, 0, GRID_H - NA_KROWS)
    start = pl.multiple_of(ks * GRID_W, GRID_W)
    q = q_ref[...] * NA_SCALE
    kl = k_ref[pl.ds(start, NA_KEYS), :]
    vl = v_ref[pl.ds(start, NA_KEYS), :]
    pair = 2 * NA_DIM
    left = lax.broadcasted_iota(jnp.int32, (NA_Q, pair), 1) < NA_DIM
    ones_loc = jnp.ones((NA_KEYS, pair), BF16)
    ones_ctx = jnp.ones((PAST_LEN, pair), BF16)
    kc, v_ext, vc_ext = [], [], []
    for p in range(NA_G // 2):
        lanes = slice(p * pair, (p + 1) * pair)
        kc.append(jnp.concatenate([kc_ref[2 * p].astype(BF16), kc_ref[2 * p + 1].astype(BF16)], axis=-1))
        vc = jnp.concatenate([vc_ref[2 * p].astype(BF16), vc_ref[2 * p + 1].astype(BF16)], axis=-1)
        v_ext.append(jnp.concatenate([vl[:, lanes], ones_loc], axis=-1))
        vc_ext.append(jnp.concatenate([vc, ones_ctx], axis=-1))

    def scores(hh):
        p = hh // 2
        lanes = slice(p * pair, (p + 1) * pair)
        qm = jnp.where(left == (hh % 2 == 0), q[:, lanes], jnp.zeros((NA_Q, pair), BF16))
        return _dot_nt(qm, kl[:, lanes]) + bias_ref[hh], _dot_nt(qm, kc[p])

    outs = []
    nxt = scores(0)
    for hh in range(NA_G):
        s_loc, s_ctx = nxt
        if hh + 1 < NA_G:
            nxt = scores(hh + 1)
        m = jnp.maximum(jnp.max(s_loc, axis=-1, keepdims=True), jnp.max(s_ctx, axis=-1, keepdims=True))
        p_loc = jnp.exp(s_loc - m).astype(BF16)
        p_ctx = jnp.exp(s_ctx - m).astype(BF16)
        o = (jnp.dot(p_loc, v_ext[hh // 2], preferred_element_type=F32)
             + jnp.dot(p_ctx, vc_ext[hh // 2], preferred_element_type=F32))
        outs.append(o[:, :pair] / o[:, pair:])
    o_ref[...] = jnp.concatenate([jnp.where(left, outs[2 * p], outs[2 * p + 1]) for p in range(NA_G // 2)],
                                 axis=-1).astype(BF16)


def _na_attn(z, cache_k, cache_v, bias, layer):
    lat_q0 = T_CTX // NA_Q
    lat_s0 = T_CTX // DEC_SEQ
    width = NA_G * NA_DIM
    qc, kc, vc = COL_NA_Q // width, COL_NA_K // width, COL_NA_V // width
    groups = NA_HEADS // NA_G

    def kind(rb):
        return jnp.where(rb == 0, 0, jnp.where(rb == NA_RB - 1, 2, 1))

    ctx_spec = pl.BlockSpec((None, None, NA_G, PAST_LEN, NA_DIM), lambda b, hg, rb: (b, layer, hg, 0, 0))
    return pl.pallas_call(
        _na_kernel,
        grid=(DEC_BATCH, groups, NA_RB),
        in_specs=[pl.BlockSpec((NA_Q, width), lambda b, hg, rb: (lat_q0 + b * NA_RB + rb, qc + hg)),
                  pl.BlockSpec((DEC_SEQ, width), lambda b, hg, rb: (lat_s0 + b, kc + hg)),
                  pl.BlockSpec((DEC_SEQ, width), lambda b, hg, rb: (lat_s0 + b, vc + hg)),
                  ctx_spec, ctx_spec,
                  pl.BlockSpec((NA_G, None, NA_Q, NA_KEYS),
                               lambda b, hg, rb: (layer * groups + hg, kind(rb), 0, 0))],
        out_specs=pl.BlockSpec((NA_Q, width), lambda b, hg, rb: (b * NA_RB + rb, hg)),
        out_shape=jax.ShapeDtypeStruct((T_LAT, NA_WIDTH), BF16),
        compiler_params=_cparams(("arbitrary", "arbitrary", "arbitrary")),
        name="nbr_attn",
    )(z, z, z, cache_k, cache_v, bias)


RET_PAIR = 2 * RET_DIM
RET_NPAIR = RET_HEADS // 2
assert RET_PAIR == LANES and RET_CHUNK == LANES
RET_UNROLL = 16
RET_CTX_GROUP = 8


def _rope_tables():
    n_freq = RET_DIM // 4
    t = np.arange(DEC_SEQ)
    inv = jnp.asarray(ROPE_BASE, F32) ** (-jnp.arange(n_freq, dtype=F32) / n_freq)
    ang_r = jnp.asarray(t // GRID_W, F32)[:, None] * inv[None, :]
    ang_c = jnp.asarray(t % GRID_W, F32)[:, None] * inv[None, :]
    cos = jnp.concatenate([jnp.cos(ang_r)] * 2 + [jnp.cos(ang_c)] * 2, axis=-1)
    sin = jnp.concatenate([-jnp.sin(ang_r), jnp.sin(ang_r), -jnp.sin(ang_c), jnp.sin(ang_c)], axis=-1)
    lane = np.arange(RET_WIDTH)
    src = np.where(lane % (2 * n_freq) < n_freq, lane + n_freq, lane - n_freq)
    swap = np.zeros((RET_WIDTH, RET_WIDTH), np.float32)
    swap[src, lane] = 1.0
    return jnp.tile(cos, (1, RET_HEADS)), jnp.tile(sin, (1, RET_HEADS)), jnp.asarray(swap, BF16)


def _ret_kernel(seq, group, latent, *refs):
    if latent:
        (lg_ref, z_ref, gn_ref, cos_ref, sin_ref, swap_ref, s0f_ref, s0b_ref, y_ref,
         q_s, k_s, kv_s, st_s) = refs
    else:
        lg_ref, z_ref, gn_ref, y_ref, sf_ref, sb_ref, q_s, k_s, kv_s, st_s = refs
    nc = seq // RET_CHUNK
    nct = group * nc
    ch, hd, pw = RET_CHUNK, RET_DIM, RET_PAIR

    row = lax.broadcasted_iota(jnp.int32, (ch, ch), 0).astype(F32)
    col = lax.broadcasted_iota(jnp.int32, (ch, ch), 1).astype(F32)
    pos = lax.broadcasted_iota(jnp.int32, (ch, pw), 0).astype(F32)
    left = lax.broadcasted_iota(jnp.int32, (ch, pw), 1) < hd
    top = lax.broadcasted_iota(jnp.int32, (pw, pw), 0) < hd
    same_head = top == (lax.broadcasted_iota(jnp.int32, (pw, pw), 1) < hd)
    same_head2 = jnp.concatenate([same_head, same_head], axis=0)

    def per_head(mask, fn, p):
        return jnp.where(mask, fn(2 * p), fn(2 * p + 1))

    decay = []
    for h in range(RET_HEADS):
        lf, lb = lg_ref[0, h], lg_ref[1, h]
        d_f = jnp.where(row >= col, jnp.exp(jnp.maximum(row - col, 0.0) * lf), 0.0)
        d_b = jnp.where(col >= row, jnp.exp(jnp.maximum(col - row, 0.0) * lb), 0.0)
        decay.append(d_f + d_b)
    q_dec, k_dec, c_dec_f, c_dec_b = [], [], [], []
    for p in range(RET_NPAIR):
        q_dec.append(jnp.concatenate(
            [per_head(left, lambda h: jnp.exp((pos + 1.0) * lg_ref[0, h]), p),
             per_head(left, lambda h: jnp.exp((ch - pos) * lg_ref[1, h]), p)], axis=-1))
        k_dec.append(jnp.concatenate(
            [per_head(left, lambda h: jnp.exp((ch - 1.0 - pos) * lg_ref[0, h]), p),
             per_head(left, lambda h: jnp.exp(pos * lg_ref[1, h]), p)], axis=-1))
        zero = jnp.zeros((pw, pw), F32)
        c_dec_f.append(per_head(top, lambda h: jnp.exp(zero + ch * lg_ref[0, h]), p))
        c_dec_b.append(per_head(top, lambda h: jnp.exp(zero + ch * lg_ref[1, h]), p))

    def rope(x, base):
        xf = x.astype(F32)
        if not latent:
            return xf
        swapped = jnp.dot(x, swap_ref[...], preferred_element_type=F32)
        return xf * cos_ref[pl.ds(base, ch), :] + swapped * sin_ref[pl.ds(base, ch), :]

    def pass1(n, carry):
        base = pl.multiple_of(n * ch, ch)
        zc = z_ref[pl.ds(base, ch), :]
        q = rope(zc[:, 0:RET_WIDTH], base)
        k = rope(zc[:, RET_WIDTH:2 * RET_WIDTH], base) * (RET_DIM ** -0.5)
        q_s[pl.ds(base, ch), :] = q.astype(BF16)
        k_s[pl.ds(base, ch), :] = k.astype(BF16)
        v = zc[:, 2 * RET_WIDTH:3 * RET_WIDTH]
        for p in range(RET_NPAIR):
            lanes = slice(p * pw, (p + 1) * pw)
            kp = k[:, lanes]
            k2 = (jnp.concatenate([kp, kp], axis=-1) * k_dec[p]).astype(BF16)
            kv = lax.dot_general(k2, v[:, lanes], (((0,), (0,)), ((), ())), preferred_element_type=F32)
            kv_s[n, p] = jnp.where(same_head2, kv, 0.0)
        return carry

    lax.fori_loop(0, nct, pass1, 0, unroll=min(RET_UNROLL, nct))

    def block_diag(a, b):
        z = jnp.zeros((hd, hd), F32)
        return jnp.concatenate([jnp.concatenate([a, z], axis=1), jnp.concatenate([z, b], axis=1)], axis=0)

    for sq in range(group):
        for p in range(RET_NPAIR):
            if latent:
                s_f = block_diag(s0f_ref[2 * p], s0f_ref[2 * p + 1])
                s_b = block_diag(s0b_ref[2 * p], s0b_ref[2 * p + 1])
            else:
                s_f = s_b = jnp.zeros((pw, pw), F32)

            def fwd(i, s, p=p, first=sq * nc):
                n = first + i
                st_s[n, p, 0:pw, :] = s.astype(BF16)
                return c_dec_f[p] * s + kv_s[n, p, 0:pw, :]

            def bwd(i, s, p=p, final=sq * nc + nc - 1):
                n = final - i
                st_s[n, p, pw:2 * pw, :] = s.astype(BF16)
                return c_dec_b[p] * s + kv_s[n, p, pw:2 * pw, :]

            s_f = lax.fori_loop(0, nc, fwd, s_f)
            s_b = lax.fori_loop(0, nc, bwd, s_b)
            if not latent:
                for hh in range(2):
                    blk = slice(hh * hd, (hh + 1) * hd)
                    sf_ref[sq, 2 * p + hh] = s_f[blk, blk]
                    sb_ref[sq, 2 * p + hh] = s_b[blk, blk]

    def pass3(n, carry):
        base = pl.multiple_of(n * ch, ch)
        zc = z_ref[pl.ds(base, ch), :]
        q = q_s[pl.ds(base, ch), :]
        k = k_s[pl.ds(base, ch), :]
        v = zc[:, 2 * RET_WIDTH:3 * RET_WIDTH]
        gate = zc[:, 3 * RET_WIDTH:4 * RET_WIDTH].astype(F32)
        outs = []
        for p in range(RET_NPAIR):
            lanes = slice(p * pw, (p + 1) * pw)
            qp, kp, vp = q[:, lanes], k[:, lanes], v[:, lanes]
            o_h = []
            for hh in range(2):
                qm = jnp.where(left == (hh == 0), qp, jnp.zeros_like(qp))
                s = _dot_nt(qm, kp) * decay[2 * p + hh]
                o_h.append(jnp.dot(s.astype(BF16), vp, preferred_element_type=F32))
            qf = qp.astype(F32)
            q2 = (jnp.concatenate([qf, qf], axis=-1) * q_dec[p]).astype(BF16)
            o = jnp.where(left, o_h[0], o_h[1]) + jnp.dot(q2, st_s[n, p], preferred_element_type=F32)

            def half_mean(t):
                s_l = jnp.sum(jnp.where(left, t, 0.0), axis=-1, keepdims=True)
                s_r = jnp.sum(jnp.where(left, 0.0, t), axis=-1, keepdims=True)
                return jnp.where(left, s_l, s_r) * (1.0 / hd)

            d = o - half_mean(o)
            outs.append(d * lax.rsqrt(half_mean(d * d) + EPS))
        nrm = jnp.concatenate(outs, axis=-1)
        y_ref[pl.ds(base, ch), :] = (nrm * gn_ref[...] * (gate * _sigmoid(gate))).astype(BF16)
        return carry

    lax.fori_loop(0, nct, pass3, 0, unroll=min(RET_UNROLL, nct))


def _retention(z, lg, gn_g, latent, layer=None, rope=None, s0_f=None, s0_b=None):
    seq = DEC_SEQ if latent else SEQ
    nseq = DEC_BATCH if latent else BATCH
    group = 1 if latent else RET_CTX_GROUP
    rows = group * seq
    nct = rows // RET_CHUNK
    row0 = (T_CTX // DEC_SEQ) if latent else 0
    cb = COL_RET // (4 * RET_WIDTH)
    in_specs = [pl.BlockSpec(memory_space=pltpu.SMEM),
                pl.BlockSpec((rows, 4 * RET_WIDTH), lambda s: (row0 + s, cb)),
                _full_spec((1, RET_WIDTH))]
    args = [lg, z, gn_g]
    state_shape = jax.ShapeDtypeStruct((nseq, RET_HEADS, RET_DIM, RET_DIM), F32)
    y_spec = pl.BlockSpec((rows, RET_WIDTH), lambda s: (s, 0))
    y_shape = jax.ShapeDtypeStruct((nseq * seq, RET_WIDTH), BF16)
    if latent:
        st_spec = pl.BlockSpec((None, None, RET_HEADS, RET_DIM, RET_DIM), lambda s: (s, layer, 0, 0, 0))

        def const_spec(shape):
            return pl.BlockSpec(shape, lambda s: (0,) * len(shape), pipeline_mode=pl.Buffered(1))

        in_specs += [const_spec((seq, RET_WIDTH)), const_spec((seq, RET_WIDTH)),
                     const_spec((RET_WIDTH, RET_WIDTH)), st_spec, st_spec]
        args += [rope[0], rope[1], rope[2], s0_f, s0_b]
        out_specs, out_shape = y_spec, y_shape
    else:
        so_spec = pl.BlockSpec((group, RET_HEADS, RET_DIM, RET_DIM), lambda s: (s, 0, 0, 0))
        out_specs, out_shape = [y_spec, so_spec, so_spec], [y_shape, state_shape, state_shape]
    return pl.pallas_call(
        functools.partial(_ret_kernel, seq, group, latent),
        grid=(nseq // group,),
        in_specs=in_specs,
        out_specs=out_specs,
        out_shape=out_shape,
        scratch_shapes=[pltpu.VMEM((rows, RET_WIDTH), BF16), pltpu.VMEM((rows, RET_WIDTH), BF16),
                        pltpu.VMEM((nct, RET_NPAIR, 2 * RET_PAIR, RET_PAIR), F32),
                        pltpu.VMEM((nct, RET_NPAIR, 2 * RET_PAIR, RET_PAIR), BF16)],
        compiler_params=_cparams(("arbitrary",)),
        name="retention_lat" if latent else "retention_ctx",
    )(*args)


def _route(logits):
    lane = lax.broadcasted_iota(jnp.int32, logits.shape, 1)
    lane_f = lane.astype(F32)
    big = float(ROUTE_COLS)
    neg = -jnp.inf
    is_grp = lane < N_GROUPS
    gl = jnp.where(is_grp, logits, neg)
    gmax = jnp.max(gl, axis=-1, keepdims=True)
    grp = jnp.min(jnp.where(gl == gmax, lane_f, big), axis=-1, keepdims=True)
    p_grp = 1.0 / jnp.sum(jnp.exp(gl - gmax), axis=-1, keepdims=True)
    e_f = lane_f - N_GROUPS
    lo = grp * EXPERTS_PER_GROUP
    in_grp = (e_f >= lo) & (e_f < lo + EXPERTS_PER_GROUP)
    el = jnp.where(in_grp, logits, neg)
    m1 = jnp.max(el, axis=-1, keepdims=True)
    i1 = jnp.min(jnp.where(el == m1, lane_f, big), axis=-1, keepdims=True)
    el2 = jnp.where(lane_f == i1, neg, el)
    m2 = jnp.max(el2, axis=-1, keepdims=True)
    i2 = jnp.min(jnp.where(el2 == m2, lane_f, big), axis=-1, keepdims=True)
    t = jnp.exp(m2 - m1)
    g1 = p_grp / (1.0 + t)
    g2 = p_grp * t / (1.0 + t)
    rows = logits.shape[0]
    oh1, oh2 = lane_f == i1, lane_f == i2
    oh = jnp.where(oh1 | oh2, 1.0, 0.0)
    tri = (lax.broadcasted_iota(jnp.int32, (rows, rows), 0)
           > lax.broadcasted_iota(jnp.int32, (rows, rows), 1))
    rank = jnp.dot(jnp.where(tri, 1.0, 0.0).astype(BF16), oh.astype(BF16), preferred_element_type=F32)
    tiles = jnp.floor((jnp.sum(oh, axis=0, keepdims=True) + (SUBLANES - 1)) * (1.0 / SUBLANES))
    upper = (lax.broadcasted_iota(jnp.int32, (ROUTE_COLS, ROUTE_COLS), 0)
             < lax.broadcasted_iota(jnp.int32, (ROUTE_COLS, ROUTE_COLS), 1))
    start = SUBLANES * jnp.dot(jnp.broadcast_to(tiles, (SUBLANES, ROUTE_COLS)).astype(BF16),
                               jnp.where(upper, 1.0, 0.0).astype(BF16),
                               preferred_element_type=F32)[0:1, :]
    pos = start + rank
    p1 = jnp.sum(jnp.where(oh1, pos, 0.0), axis=-1, keepdims=True)
    p2 = jnp.sum(jnp.where(oh2, pos, 0.0), axis=-1, keepdims=True)
    out = jnp.zeros(logits.shape, F32)
    for k, val in enumerate((i1 - N_GROUPS, i2 - N_GROUPS, g1, g2, p1, p2)):
        out = jnp.where(lane == k, val, out)
    return out, SUBLANES * tiles


def _outproj_kernel(ycc, ycl, ync, ynl, yrc, yrl, x_ref, mod_ref, g_ref, w_ref, wr_ref, br_ref,
                    xo_ref, xs_ref, r_ref, seg_ref, wb_ref):
    _cast_weights_once(w_ref, wb_ref)
    is_ctx = pl.program_id(0) < NB_CTX
    yc = jnp.where(is_ctx, ycc[...], ycl[...])
    yn = jnp.where(is_ctx, ync[...], ynl[...])
    yr = jnp.where(is_ctx, yrc[...], yrl[...])
    y = (jnp.dot(yc, wb_ref[0:CONV_CH, :], preferred_element_type=F32)
         + jnp.dot(yn, wb_ref[CONV_CH:CONV_CH + NA_WIDTH, :], preferred_element_type=F32)
         + jnp.dot(yr, wb_ref[CONV_CH + NA_WIDTH:, :], preferred_element_type=F32))
    x = x_ref[...] + mod_ref[2:3, :] * y
    xo_ref[...] = x
    h = _norm_mod(x, g_ref[...], mod_ref[3:4, :], mod_ref[4:5, :])
    h_hi = h.astype(BF16)
    h_lo = (h - h_hi.astype(F32)).astype(BF16)
    hw = jnp.dot(h_hi, wr_ref[...], preferred_element_type=F32)
    logits = (hw[:, :ROUTE_COLS] + hw[:, ROUTE_COLS:]
              + jnp.dot(h_lo, wr_ref[:, :ROUTE_COLS], preferred_element_type=F32) + br_ref[...])
    route, seg = _route(logits)
    r_ref[...] = route
    seg_ref[...] = jnp.broadcast_to(seg, seg_ref.shape)
    sel = _slot_onehot(route, 0) | _slot_onehot(route, 1)
    xs_ref[...] = _pack_bf16_pairs(lax.dot_general(jnp.where(sel, 1.0, 0.0).astype(BF16), h_hi,
                                                   (((0,), (0,)), ((), ())), preferred_element_type=F32))


def _outproj(y_conv, y_na, y_ret, x, mod, g, w_out, layer, w_route, b_route):
    return pl.pallas_call(
        _outproj_kernel,
        grid=(NB_ALL,),
        in_specs=(_ctx_lat_specs(CONV_CH) + _ctx_lat_specs(NA_WIDTH) + _ctx_lat_specs(RET_WIDTH)
                  + [_tok_spec(D_MODEL), _mod_spec(), _full_spec((1, D_MODEL)),
                     _layer_weight_spec(layer, D_MODEL, D_MODEL), _full_spec((D_MODEL, 2 * ROUTE_COLS)),
                     _full_spec((1, ROUTE_COLS))]),
        out_specs=[_tok_spec(D_MODEL), pl.BlockSpec((MOE_LC, D_HALF), lambda i: (i, 0)),
                   _tok_spec(ROUTE_COLS), pl.BlockSpec((None, SUBLANES, ROUTE_COLS), lambda i: (i, 0, 0))],
        out_shape=[jax.ShapeDtypeStruct((T_ALL, D_MODEL), F32),
                   jax.ShapeDtypeStruct((NB_ALL * MOE_LC, D_HALF), U32),
                   jax.ShapeDtypeStruct((T_ALL, ROUTE_COLS), F32),
                   jax.ShapeDtypeStruct((NB_ALL, SUBLANES, ROUTE_COLS), F32)],
        scratch_shapes=[pltpu.VMEM((D_MODEL, D_MODEL), BF16)],
        compiler_params=_cparams(("arbitrary",)),
        name="outproj_route",
    )(y_conv[0], y_conv[1], y_na[0], y_na[1], y_ret[0], y_ret[1], x, mod, g, w_out, w_route, b_route)


def _dispatch_tables(seg):
    seg_len = seg[:, 0, N_GROUPS:N_GROUPS + N_EXPERTS].astype(jnp.int32)
    experts = jnp.arange(N_EXPERTS, dtype=jnp.int32)
    in_chunk = jnp.cumsum(seg_len, axis=1) - seg_len
    seg_row = in_chunk + MOE_LC * jnp.arange(N_CHUNK, dtype=jnp.int32)[:, None]
    seg_off = jnp.cumsum(seg_len, axis=0) - seg_len
    rows_e = jnp.sum(seg_len, axis=0)
    chunk_rows = jnp.sum(seg_len, axis=1)
    nblk = (rows_e + MOE_BLK - 1) // MOE_BLK
    blk_end = jnp.cumsum(nblk)
    blk_start = blk_end - nblk
    blk = jnp.arange(MOE_NBLK, dtype=jnp.int32)
    n_active = blk_end[-1]
    blk_e = jnp.minimum(jnp.sum((blk_end[None, :] <= jnp.minimum(blk, n_active - 1)[:, None]).astype(jnp.int32),
                                axis=-1), N_EXPERTS - 1)
    mine = blk_e[:, None] == experts[None, :]
    blk_lo = (blk - jnp.sum(jnp.where(mine, blk_start[None, :], 0), axis=-1)) * MOE_BLK
    left = jnp.sum(jnp.where(mine, rows_e[None, :], 0), axis=-1) - blk_lo
    blk_nv = jnp.where(blk < n_active, jnp.clip(left, 0, MOE_BLK), 0).astype(jnp.int32)
    off_b = jnp.sum(jnp.where(mine[:, None, :], seg_off[None, :, :], 0), axis=-1)
    end_b = off_b + jnp.sum(jnp.where(mine[:, None, :], seg_len[None, :, :], 0), axis=-1)
    blk_c0 = jnp.sum((end_b <= blk_lo[:, None]).astype(jnp.int32), axis=-1)
    blk_c1 = jnp.sum((off_b < (blk_lo + blk_nv)[:, None]).astype(jnp.int32), axis=-1)
    after = jnp.sum(jnp.where(mine, blk_end[None, :], 0), axis=-1)
    blk_next_e = jnp.where(after < n_active, jnp.take(blk_e, jnp.minimum(after, MOE_NBLK - 1)), -1)
    row_b = jnp.sum(jnp.where(mine[:, None, :], seg_row[None, :, :], 0), axis=-1)
    first = jnp.maximum(off_b, blk_lo[:, None])
    piece_n = jnp.minimum(end_b, (blk_lo + blk_nv)[:, None]) - first
    piece_src = row_b + first - off_b
    piece_dst = first - blk_lo[:, None]
    piece = jnp.where(piece_n > 0,
                      (piece_src // SUBLANES) | ((piece_dst // SUBLANES) << PIECE_DST_SHIFT)
                      | ((piece_n // SUBLANES) << PIECE_N_SHIFT), 0).astype(jnp.int32)
    return (blk_e, blk_next_e.astype(jnp.int32), blk_nv, blk_c0, blk_c1, piece.reshape(-1), chunk_rows)


PIECE_DST_SHIFT = (N_CHUNK * MOE_LC // SUBLANES - 1).bit_length()
PIECE_N_SHIFT = PIECE_DST_SHIFT + (MOE_BLK // SUBLANES - 1).bit_length()
assert PIECE_N_SHIFT + (TM // SUBLANES).bit_length() <= 31


def _moe_kernel(layer, blk_e, blk_next_e, blk_nv, blk_c0, blk_c1, piece,
                chunk_rows, xs_hbm, w1_hbm, w3_hbm, w2_hbm, ys_hbm, xbuf, obuf, zeros,
                w1f, w3f, w2f, w1b, w3b, w2b, gsem, ssem, zsem, wsem):
    i = pl.program_id(0)
    last = pl.num_programs(0) - 1
    slot = i % 2

    def tiles(v):
        return pl.multiple_of(v, SUBLANES)

    def for_segments(blk, fn):
        def body(c, carry):
            word = piece[blk * N_CHUNK + c]
            src = (word & ((1 << PIECE_DST_SHIFT) - 1)) * SUBLANES
            dst = ((word >> PIECE_DST_SHIFT) & ((1 << (PIECE_N_SHIFT - PIECE_DST_SHIFT)) - 1)) * SUBLANES
            n = (word >> PIECE_N_SHIFT) * SUBLANES

            @pl.when(word != 0)
            def _():
                fn(tiles(src), tiles(dst), tiles(n))

            return carry

        lax.fori_loop(blk_c0[blk], blk_c1[blk], body, 0)

    def weight_copies(e):
        return [pltpu.make_async_copy(src.at[layer, e], dst, wsem)
                for src, dst in ((w1_hbm, w1f), (w3_hbm, w3f), (w2_hbm, w2f))]

    def start_gathers(blk, s):
        for_segments(blk, lambda src, dst, n: pltpu.make_async_copy(
            xs_hbm.at[pl.ds(src, n)], xbuf.at[s, pl.ds(dst, n)], gsem.at[s]).start())

    def start_scatters(blk, s):
        for_segments(blk, lambda dst, src, n: pltpu.make_async_copy(
            obuf.at[s, pl.ds(src, n)], ys_hbm.at[pl.ds(dst, n)], ssem.at[s]).start())

    def wait_rows(blk, s, sem):
        n = tiles(blk_nv[blk])

        @pl.when(n > 0)
        def _():
            pltpu.make_async_copy(xs_hbm.at[pl.ds(0, n)], xbuf.at[s, pl.ds(0, n)], sem.at[s]).wait()

    @pl.when(i == 0)
    def _():
        xbuf[...] = jnp.zeros_like(xbuf)
        zeros[...] = jnp.zeros_like(zeros)

        def tail(c):
            n = tiles(MOE_LC - chunk_rows[c])
            return n, pltpu.make_async_copy(zeros.at[pl.ds(0, n)],
                                            ys_hbm.at[pl.ds(tiles(c * MOE_LC + chunk_rows[c]), n)], zsem)

        def fill(c, carry):
            n, copy = tail(c)
            pl.when(n > 0)(copy.start)
            return carry

        def drain(c, carry):
            n, copy = tail(c)
            pl.when(n > 0)(copy.wait)
            return carry

        lax.fori_loop(0, N_CHUNK, fill, 0)
        lax.fori_loop(0, N_CHUNK, drain, 0)
        start_gathers(0, 0)
        for copy in weight_copies(blk_e[0]):
            copy.start()

    @pl.when(i < last)
    def _():
        start_gathers(i + 1, 1 - slot)

    @pl.when(i >= 2)
    def _():
        wait_rows(i - 2, slot, ssem)

    @pl.when(blk_nv[i] > 0)
    def _():
        @pl.when((i == 0) | (blk_e[i] != blk_e[jnp.maximum(i - 1, 0)]))
        def _():
            for copy in weight_copies(blk_e[i]):
                copy.wait()
            w1b[...] = w1f[...].astype(BF16)
            w3b[...] = w3f[...].astype(BF16)
            w2b[...] = w2f[...].astype(BF16)

            @pl.when(blk_next_e[i] >= 0)
            def _():
                for copy in weight_copies(blk_next_e[i]):
                    copy.start()

        wait_rows(i, slot, gsem)

        def expert_mlp(rows):
            x_lo, x_hi = _unpack_bf16_pairs(xbuf[slot, 0:rows, :])
            n_hid = D_EXPERT // MXU_TILE

            def in_dot(w, t):
                cols = slice(t * MXU_TILE, (t + 1) * MXU_TILE)
                return (jnp.dot(x_lo, w[:D_HALF, cols], preferred_element_type=F32)
                        + jnp.dot(x_hi, w[D_HALF:, cols], preferred_element_type=F32))

            ab = [(in_dot(w1b, t), in_dot(w3b, t)) for t in range(n_hid)]
            mid = [(a * _sigmoid(a) * b).astype(BF16) for a, b in ab]

            def out_dot(t):
                cols = slice(t * MXU_TILE, (t + 1) * MXU_TILE)
                return sum(jnp.dot(mid[j], w2b[j * MXU_TILE:(j + 1) * MXU_TILE, cols],
                                   preferred_element_type=F32) for j in range(n_hid))

            n_word = D_HALF // MXU_TILE
            for t in range(n_word):
                obuf[slot, 0:rows, t * MXU_TILE:(t + 1) * MXU_TILE] = _pack_words(
                    out_dot(t), out_dot(t + n_word))

        for rows in range(MOE_ROW_STEP, MOE_BLK + 1, MOE_ROW_STEP):
            @pl.when((blk_nv[i] > rows - MOE_ROW_STEP) & (blk_nv[i] <= rows))
            def _(rows=rows):
                expert_mlp(rows)

        start_scatters(i, slot)

    @pl.when(i == last)
    def _():
        wait_rows(i - 1, 1 - slot, ssem)
        wait_rows(i, slot, ssem)


def _moe(xs, w1, w3, w2, layer, blk_e, blk_next_e, blk_nv, blk_c0, blk_c1, piece, chunk_rows):
    any_spec = pl.BlockSpec(memory_space=pl.ANY)
    grid_spec = pltpu.PrefetchScalarGridSpec(
        num_scalar_prefetch=7,
        grid=(MOE_NBLK,),
        in_specs=[any_spec, any_spec, any_spec, any_spec],
        out_specs=any_spec,
        scratch_shapes=[pltpu.VMEM((2, MOE_BLK, D_HALF), U32), pltpu.VMEM((2, MOE_BLK, D_HALF), U32),
                        pltpu.VMEM((MOE_LC - 2 * TM, D_HALF), U32),
                        pltpu.VMEM((D_MODEL, D_EXPERT), F32), pltpu.VMEM((D_MODEL, D_EXPERT), F32),
                        pltpu.VMEM((D_EXPERT, D_MODEL), F32),
                        pltpu.VMEM((D_MODEL, D_EXPERT), BF16), pltpu.VMEM((D_MODEL, D_EXPERT), BF16),
                        pltpu.VMEM((D_EXPERT, D_MODEL), BF16),
                        pltpu.SemaphoreType.DMA((2,)), pltpu.SemaphoreType.DMA((2,)),
                        pltpu.SemaphoreType.DMA, pltpu.SemaphoreType.DMA])
    return pl.pallas_call(
        functools.partial(_moe_kernel, layer),
        grid_spec=grid_spec,
        out_shape=jax.ShapeDtypeStruct((NB_ALL * MOE_LC, D_HALF), U32),
        compiler_params=_cparams(("arbitrary",)),
        name="moe_experts",
    )(blk_e, blk_next_e, blk_nv, blk_c0, blk_c1, piece, chunk_rows, xs, w1, w3, w2)


def _final_kernel(x_ref, ys_ref, r_ref, mod_ref, g_ref, o_ref):
    x = _moe_residual(x_ref, ys_ref, r_ref, mod_ref)
    ms = jnp.mean(x * x, axis=-1, keepdims=True)
    o_ref[...] = x * lax.rsqrt(ms + EPS) * g_ref[...]


def _final(x, ys, route, mod, g, block0, nblocks):
    return pl.pallas_call(
        _final_kernel,
        grid=(nblocks,),
        in_specs=[pl.BlockSpec((TM, D_MODEL), lambda i: (block0 + i, 0)),
                  pl.BlockSpec((MOE_LC, D_HALF), lambda i: (block0 + i, 0)),
                  pl.BlockSpec((TM, ROUTE_COLS), lambda i: (block0 + i, 0)),
                  pl.BlockSpec((None, 6, D_MODEL), lambda i: (_cond_row(block0 + i), 0, 0)),
                  _full_spec((1, D_MODEL))],
        out_specs=_tok_spec(D_MODEL),
        out_shape=jax.ShapeDtypeStruct((nblocks * TM, D_MODEL), F32),
        compiler_params=_cparams(("arbitrary",)),
        name="final_norm",
    )(x, ys, route, mod, g)


def kernel(x_prompt, x_sample, c, cache_k, cache_v, state_ret_f, state_ret_b, c_ctx, w_ada, b_ada, norm1_g, norm2_g, w_in, w_out, conv_w, conv_b, conv_ln_g, conv_ln_b, na_rpb, ret_lg_f, ret_lg_b, ret_gn_g, w_route_g, b_route_g, w_route_e, b_route_e, w1, w3, w2, final_g):
    cv = jnp.zeros((COND_ROWS, D_MODEL), F32).at[0].set(c_ctx).at[1:N_COND].set(c)
    mods = _ada(cv, w_ada, b_ada).reshape(DEPTH, COND_ROWS, 6, D_MODEL)
    pad = ROUTE_COLS - N_GROUPS - N_EXPERTS
    w_route = jnp.pad(jnp.concatenate([w_route_g, w_route_e], axis=-1), ((0, 0), (0, 0), (0, pad)))
    b_route = jnp.pad(jnp.concatenate([b_route_g, b_route_e], axis=-1), ((0, 0), (0, pad)))
    w_route_hi = w_route.astype(BF16)
    w_route_lo = (w_route - w_route_hi.astype(F32)).astype(BF16)
    w_route = jnp.concatenate([w_route_hi, w_route_lo], axis=-1)
    na_bias = _na_bias_tables(na_rpb)
    rope = _rope_tables()
    lg = jnp.stack([ret_lg_f, ret_lg_b], axis=1)

    x_ctx = x_prompt.reshape(T_CTX, D_MODEL)
    x_lat = x_sample.reshape(T_LAT, D_MODEL)
    x = y = route = new_k = new_v = None
    sf_list, sb_list = [], []
    for l in range(DEPTH):
        g1 = norm1_g[l].reshape(1, D_MODEL)
        if l == 0:
            z, x = _inproj_first(x_ctx, x_lat, mods[l], g1, w_in, l)
        else:
            z, x = _inproj_next(x, y, route, mods[l - 1], mods[l], g1, w_in, l)
        conv_args = (conv_w[l], conv_b[l].reshape(1, -1), conv_ln_g[l].reshape(1, -1),
                     conv_ln_b[l].reshape(1, -1))
        yc_c = _conv(z, 0, BATCH, SEQ, *conv_args)
        yc_l = _conv(z, T_CTX // DEC_SEQ, DEC_BATCH, DEC_SEQ, *conv_args)
        yn_c, new_k, new_v = _ctx_attn(z, l, new_k, new_v)
        yn_l = _na_attn(z, cache_k, cache_v, na_bias, l)
        gn = ret_gn_g[l].reshape(1, RET_WIDTH)
        yr_c, sf_l, sb_l = _retention(z, lg[l], gn, latent=False)
        yr_l = _retention(z, lg[l], gn, latent=True, layer=l, rope=rope,
                          s0_f=state_ret_f, s0_b=state_ret_b)
        x, xs, route, seg = _outproj((yc_c, yc_l), (yn_c, yn_l), (yr_c, yr_l), x, mods[l],
                                     norm2_g[l].reshape(1, D_MODEL), w_out, l, w_route[l],
                                     b_route[l].reshape(1, ROUTE_COLS))
        y = _moe(xs, w1, w3, w2, l, *_dispatch_tables(seg))
        sf_list.append(sf_l)
        sb_list.append(sb_l)
    fg = final_g.reshape(1, D_MODEL)
    y_prompt = _final(x, y, route, mods[DEPTH - 1], fg, 0, NB_CTX).reshape(BATCH, SEQ, D_MODEL)
    y_sample = _final(x, y, route, mods[DEPTH - 1], fg, NB_CTX, NB_LAT).reshape(DEC_BATCH, DEC_SEQ, D_MODEL)
    return (y_prompt, y_sample, new_k, new_v, jnp.stack(sf_list, axis=1), jnp.stack(sb_list, axis=1))
```

```python
import functools

import numpy as np
import jax
import jax.numpy as jnp
from jax import lax
from jax.experimental import pallas as pl
from jax.experimental.pallas import tpu as pltpu

D_MODEL = 1024
BATCH = 32
SEQ = 256
DEPTH = 2
DEC_BATCH = 4
DEC_SEQ = 4096
PAST_LEN = 512
GRID_W = 64
GRID_H = DEC_SEQ // GRID_W
CONV_CH = 256
CONV_K = 31
NA_HEADS = 8
NA_DIM = 64
NA_WIDTH = NA_HEADS * NA_DIM
NA_KH = 8
NA_KW = 16
RET_HEADS = 4
RET_DIM = 64
RET_WIDTH = RET_HEADS * RET_DIM
RET_CHUNK = 128
ROPE_BASE = 10000.0
N_GROUPS = 4
EXPERTS_PER_GROUP = 8
N_EXPERTS = N_GROUPS * EXPERTS_PER_GROUP
D_EXPERT = 512
IN_COLS = 2 * CONV_CH + 3 * NA_WIDTH + 4 * RET_WIDTH
EPS = 1e-6
NEG_INF = -1e30

F32 = jnp.float32
BF16 = jnp.bfloat16
HIGHEST = lax.Precision.HIGHEST

T_CTX = BATCH * SEQ
T_LAT = DEC_BATCH * DEC_SEQ
T_ALL = T_CTX + T_LAT
N_COND = 1 + DEC_BATCH
COND_ROWS = 8

TM = 512
NB_CTX = T_CTX // TM
NB_LAT = T_LAT // TM
NB_ALL = NB_CTX + NB_LAT
LAT_BLOCKS_PER_REQ = DEC_SEQ // TM

LANES = 128
SUBLANES = 8
MXU_TILE = 256
ROUTE_COLS = LANES

COL_CONV = 0
COL_NA_Q = 2 * CONV_CH
COL_NA_K = COL_NA_Q + NA_WIDTH
COL_NA_V = COL_NA_K + NA_WIDTH
COL_RET = COL_NA_V + NA_WIDTH

NA_ROWS = 8
NA_Q = NA_ROWS * GRID_W
NA_KROWS = NA_ROWS + NA_KH
NA_KEYS = NA_KROWS * GRID_W
NA_RB = GRID_H // NA_ROWS

MOE_BLK = 1024
MOE_ROW_STEP = 128
MOE_LC = -(-(2 * TM + N_EXPERTS * (SUBLANES - 1)) // LANES) * LANES
N_CHUNK = NB_ALL
MOE_NBLK = -(-(N_CHUNK * MOE_LC) // MOE_BLK) + N_EXPERTS

VMEM_LIMIT = 56 * 1024 * 1024


def _cparams(sem):
    return pltpu.CompilerParams(dimension_semantics=sem, vmem_limit_bytes=VMEM_LIMIT)


def _sigmoid(x):
    return 1.0 / (1.0 + jnp.exp(-x))


def _cond_row(i):
    return jnp.where(i < NB_CTX, 0, 1 + (i - NB_CTX) // LAT_BLOCKS_PER_REQ)


ADA_TN = 1536


def _ada_kernel(cv_ref, w_ref, b_ref, o_ref):
    cv = cv_ref[...]
    s = cv * _sigmoid(cv)
    o_ref[...] = jnp.dot(s, w_ref[...], precision=HIGHEST, preferred_element_type=F32) + b_ref[...]


def _ada(cv, w_ada, b_ada):
    n = 6 * D_MODEL
    return pl.pallas_call(
        _ada_kernel,
        grid=(DEPTH, n // ADA_TN),
        in_specs=[
            pl.BlockSpec((COND_ROWS, D_MODEL), lambda l, j: (0, 0)),
            pl.BlockSpec((None, D_MODEL, ADA_TN), lambda l, j: (l, 0, j)),
            pl.BlockSpec((None, 1, ADA_TN), lambda l, j: (l, 0, j)),
        ],
        out_specs=pl.BlockSpec((None, COND_ROWS, ADA_TN), lambda l, j: (l, 0, j)),
        out_shape=jax.ShapeDtypeStruct((DEPTH, COND_ROWS, n), F32),
        compiler_params=_cparams(("arbitrary", "arbitrary")),
        name="ada_mod",
    )(cv, w_ada, b_ada.reshape(DEPTH, 1, n))


IN_TN = 768


def _norm_mod(x, g, shift, scale):
    ms = jnp.mean(x * x, axis=-1, keepdims=True)
    return (x * lax.rsqrt(ms + EPS) * g) * (1.0 + scale) + shift


def _cast_weights_once(w_ref, wb_ref):
    @pl.when(pl.program_id(0) == 0)
    def _():
        wb_ref[...] = w_ref[...].astype(BF16)


def _layer_weight_spec(layer, rows, cols):
    return pl.BlockSpec((None, rows, cols), lambda i: (layer, 0, 0), pipeline_mode=pl.Buffered(1))


def _inproj_body(x, mod_ref, g_ref, w_ref, wb_ref, z_ref):
    _cast_weights_once(w_ref, wb_ref)
    h = _norm_mod(x, g_ref[...], mod_ref[0:1, :], mod_ref[1:2, :]).astype(BF16)
    for c in range(IN_COLS // IN_TN):
        cols = slice(c * IN_TN, (c + 1) * IN_TN)
        z_ref[:, cols] = jnp.dot(h, wb_ref[:, cols], preferred_element_type=F32).astype(BF16)


def _inproj_first_kernel(xc_ref, xl_ref, mod_ref, g_ref, w_ref, z_ref, xo_ref, wb_ref):
    i = pl.program_id(0)
    x = jnp.where(i < NB_CTX, xc_ref[...], xl_ref[...])
    xo_ref[...] = x
    _inproj_body(x, mod_ref, g_ref, w_ref, wb_ref, z_ref)


U32 = jnp.uint32
D_HALF = D_MODEL // 2
_HI_MASK = np.uint32(0xFFFF0000)


def _pack_words(lo, hi):
    lo = lax.bitcast_convert_type(lo.astype(BF16).astype(F32), U32) >> 16
    hi = lax.bitcast_convert_type(hi.astype(BF16).astype(F32), U32) & _HI_MASK
    return lo | hi


def _pack_bf16_pairs(x):
    return _pack_words(x[:, :D_HALF], x[:, D_HALF:])


def _unpack_bf16_pairs(w):
    lo = lax.bitcast_convert_type(w << 16, F32).astype(BF16)
    hi = lax.bitcast_convert_type(w & _HI_MASK, F32).astype(BF16)
    return lo, hi


def _slot_onehot(route, slot):
    pos = route[:, 4 + slot:5 + slot].astype(jnp.int32)
    return lax.broadcasted_iota(jnp.int32, (route.shape[0], MOE_LC), 1) == pos


def _moe_residual(x_ref, ys_ref, r_ref, mod_ref):
    r = r_ref[...]
    sel = jnp.where(_slot_onehot(r, 0), r[:, 2:3], jnp.where(_slot_onehot(r, 1), r[:, 3:4], 0.0))
    sel = sel.astype(BF16)
    y = jnp.concatenate([jnp.dot(sel, half, preferred_element_type=F32)
                         for half in _unpack_bf16_pairs(ys_ref[...])], axis=-1)
    return x_ref[...] + mod_ref[5:6, :] * y


def _inproj_next_kernel(x_ref, ys_ref, r_ref, modp_ref, mod_ref, g_ref, w_ref, z_ref, xo_ref, wb_ref):
    x = _moe_residual(x_ref, ys_ref, r_ref, modp_ref)
    xo_ref[...] = x
    _inproj_body(x, mod_ref, g_ref, w_ref, wb_ref, z_ref)


def _tok_spec(cols):
    return pl.BlockSpec((TM, cols), lambda i: (i, 0))


def _mod_spec():
    return pl.BlockSpec((None, 6, D_MODEL), lambda i: (_cond_row(i), 0, 0))


def _full_spec(shape):
    return pl.BlockSpec(shape, lambda i: (0,) * len(shape))


def _ctx_lat_specs(cols):
    return [pl.BlockSpec((TM, cols), lambda i: (jnp.minimum(i, NB_CTX - 1), 0)),
            pl.BlockSpec((TM, cols), lambda i: (jnp.maximum(i - NB_CTX, 0), 0))]


def _inproj_first(x_ctx, x_lat, mod, g, w_in, layer):
    return pl.pallas_call(
        _inproj_first_kernel,
        grid=(NB_ALL,),
        in_specs=_ctx_lat_specs(D_MODEL) + [_mod_spec(), _full_spec((1, D_MODEL)),
                                            _layer_weight_spec(layer, D_MODEL, IN_COLS)],
        out_specs=[_tok_spec(IN_COLS), _tok_spec(D_MODEL)],
        out_shape=[jax.ShapeDtypeStruct((T_ALL, IN_COLS), BF16),
                   jax.ShapeDtypeStruct((T_ALL, D_MODEL), F32)],
        scratch_shapes=[pltpu.VMEM((D_MODEL, IN_COLS), BF16)],
        compiler_params=_cparams(("arbitrary",)),
        name="inproj_first",
    )(x_ctx, x_lat, mod, g, w_in)


def _inproj_next(x, ys, route, mod_prev, mod, g, w_in, layer):
    return pl.pallas_call(
        _inproj_next_kernel,
        grid=(NB_ALL,),
        in_specs=[_tok_spec(D_MODEL),
                  pl.BlockSpec((MOE_LC, D_HALF), lambda i: (i, 0)),
                  _tok_spec(ROUTE_COLS),
                  _mod_spec(), _mod_spec(), _full_spec((1, D_MODEL)),
                  _layer_weight_spec(layer, D_MODEL, IN_COLS)],
        out_specs=[_tok_spec(IN_COLS), _tok_spec(D_MODEL)],
        out_shape=[jax.ShapeDtypeStruct((T_ALL, IN_COLS), BF16),
                   jax.ShapeDtypeStruct((T_ALL, D_MODEL), F32)],
        scratch_shapes=[pltpu.VMEM((D_MODEL, IN_COLS), BF16)],
        compiler_params=_cparams(("arbitrary",)),
        name="inproj_next",
    )(x, ys, route, mod_prev, mod, g, w_in)


CONV_PAD = 16
CONV_CHUNK = 64


CONV_SPAN = CONV_CHUNK + 2 * CONV_PAD - SUBLANES


CONV_UNROLL = 4


def _conv_kernel(seq, z_ref, w_ref, b_ref, g_ref, be_ref, o_ref, upad_ref, shift_refs):
    zeros = jnp.zeros((CONV_PAD, CONV_CH), F32)
    upad_ref[0:CONV_PAD, :] = zeros
    upad_ref[seq + CONV_PAD:seq + 2 * CONV_PAD, :] = zeros

    def glu(ci, carry):
        base = pl.multiple_of(ci * 256, 256)
        zc = z_ref[pl.ds(base, 256), :].astype(F32)
        upad_ref[pl.ds(base + CONV_PAD, 256), :] = zc[:, :CONV_CH] * _sigmoid(zc[:, CONV_CH:])
        return carry

    lax.fori_loop(0, seq // 256, glu, 0)

    shift = CONV_PAD - CONV_K // 2

    def chunk(ci, shift_ref):
        base = pl.multiple_of(ci * CONV_CHUNK, CONV_CHUNK)
        win = upad_ref[pl.ds(base, CONV_CHUNK + 2 * CONV_PAD), :]
        acc = jnp.zeros((CONV_CHUNK, CONV_CH), F32)
        for sub in range(SUBLANES):
            shift_ref[sub] = win[sub:sub + CONV_SPAN, :]
            for k in range(CONV_K):
                if (k + shift) % SUBLANES == sub:
                    lo = k + shift - sub
                    acc = acc + w_ref[k:k + 1, :] * shift_ref[sub, lo:lo + CONV_CHUNK, :]
        acc = acc + b_ref[...]
        mu = jnp.mean(acc, axis=-1, keepdims=True)
        d = acc - mu
        var = jnp.mean(d * d, axis=-1, keepdims=True)
        n = d * lax.rsqrt(var + EPS) * g_ref[...] + be_ref[...]
        o_ref[pl.ds(base, CONV_CHUNK), :] = (n * _sigmoid(n)).astype(BF16)

    def chunks(cj, carry):
        for u in range(CONV_UNROLL):
            chunk(cj * CONV_UNROLL + u, shift_refs.at[u])
        return carry

    lax.fori_loop(0, seq // (CONV_CHUNK * CONV_UNROLL), chunks, 0)


def _conv(z, row_block0, nseq, seq, w, b, g, be):
    return pl.pallas_call(
        functools.partial(_conv_kernel, seq),
        grid=(nseq,),
        in_specs=[pl.BlockSpec((seq, 2 * CONV_CH), lambda s: (row_block0 + s, 0)),
                  _full_spec((CONV_K, CONV_CH)), _full_spec((1, CONV_CH)),
                  _full_spec((1, CONV_CH)), _full_spec((1, CONV_CH))],
        out_specs=pl.BlockSpec((seq, CONV_CH), lambda s: (s, 0)),
        out_shape=jax.ShapeDtypeStruct((nseq * seq, CONV_CH), BF16),
        scratch_shapes=[pltpu.VMEM((seq + 2 * CONV_PAD, CONV_CH), F32),
                        pltpu.VMEM((CONV_UNROLL, SUBLANES, CONV_SPAN, CONV_CH), F32)],
        compiler_params=_cparams(("arbitrary",)),
        name="conv_seq%d" % seq,
    )(z, w, b, g, be)


def _dot_nt(a, b):
    return lax.dot_general(a, b, (((1,), (1,)), ((), ())), preferred_element_type=F32)


NA_SCALE = NA_DIM ** -0.5
assert NA_SCALE == 2.0 ** round(np.log2(NA_SCALE)), "query pre-scaling assumes a power-of-two scale"


CTX_ATTN_GROUP = 4


def _ctx_attn_kernel(layer, q_ref, k_ref, v_ref, *refs):
    if layer:
        _, _, o_ref, ko_ref, vo_ref = refs
    else:
        o_ref, ko_full, vo_full = refs
    pair = 2 * NA_DIM
    left = lax.broadcasted_iota(jnp.int32, (SEQ, pair), 1) < NA_DIM
    for r in range(CTX_ATTN_GROUP):
        rows = slice(r * SEQ, (r + 1) * SEQ)
        if layer:
            ko, vo = ko_ref.at[r], vo_ref.at[r]
        else:
            ko, vo = ko_full.at[r, 0], vo_full.at[r, 0]
            for j in range(1, DEPTH):
                ko_full[r, j] = jnp.zeros(ko_full.shape[2:], F32)
                vo_full[r, j] = jnp.zeros(vo_full.shape[2:], F32)
        outs = []
        for p in range(NA_HEADS // 2):
            lanes = slice(p * pair, (p + 1) * pair)
            qp, kp, vp = q_ref[rows, lanes], k_ref[rows, lanes], v_ref[rows, lanes]
            kf, vf = kp.astype(F32), vp.astype(F32)
            o_h = []
            for hh in range(2):
                cols = slice(hh * NA_DIM, (hh + 1) * NA_DIM)
                ko[2 * p + hh] = kf[:, cols]
                vo[2 * p + hh] = vf[:, cols]
                qm = jnp.where(left == (hh == 0), qp, jnp.zeros_like(qp))
                s = _dot_nt(qm, kp) * NA_SCALE
                m = jnp.max(s, axis=-1, keepdims=True)
                e = jnp.exp(s - m)
                den = jnp.sum(e, axis=-1, keepdims=True)
                o_h.append(jnp.dot(e.astype(BF16), vp, preferred_element_type=F32) / den)
            outs.append(jnp.where(left, o_h[0], o_h[1]))
        o_ref[rows, :] = jnp.concatenate(outs, axis=-1).astype(BF16)


def _ctx_attn(z, layer, k_prev=None, v_prev=None):
    qb, kb, vb = COL_NA_Q // NA_WIDTH, COL_NA_K // NA_WIDTH, COL_NA_V // NA_WIDTH
    group, rows = CTX_ATTN_GROUP, CTX_ATTN_GROUP * SEQ
    head_shape = jax.ShapeDtypeStruct((BATCH, DEPTH, NA_HEADS, SEQ, NA_DIM), F32)
    head_spec = pl.BlockSpec((group, DEPTH, NA_HEADS, SEQ, NA_DIM), lambda b: (b, 0, 0, 0, 0))
    in_specs = [pl.BlockSpec((rows, NA_WIDTH), lambda b: (b, qb)),
                pl.BlockSpec((rows, NA_WIDTH), lambda b: (b, kb)),
                pl.BlockSpec((rows, NA_WIDTH), lambda b: (b, vb))]
    args = [z, z, z]
    aliases = {}
    if layer:
        any_spec = pl.BlockSpec(memory_space=pl.ANY)
        in_specs += [any_spec, any_spec]
        args += [k_prev, v_prev]
        aliases = {3: 1, 4: 2}
        head_spec = pl.BlockSpec((group, None, NA_HEADS, SEQ, NA_DIM), lambda b: (b, layer, 0, 0, 0))
    return pl.pallas_call(
        functools.partial(_ctx_attn_kernel, layer),
        grid=(BATCH // group,),
        in_specs=in_specs,
        out_specs=[pl.BlockSpec((rows, NA_WIDTH), lambda b: (b, 0)), head_spec, head_spec],
        out_shape=[jax.ShapeDtypeStruct((T_CTX, NA_WIDTH), BF16), head_shape, head_shape],
        input_output_aliases=aliases,
        compiler_params=_cparams(("arbitrary",)),
        name="ctx_attn",
    )(*args)


NA_KINDS = (0, NA_ROWS, GRID_H - NA_ROWS)
N_DR = 2 * NA_KH - 1
N_DC = 2 * NA_KW - 1


def _na_row_offset(r0, i, j):
    ks = min(max(r0 - NA_KH // 2, 0), GRID_H - NA_KROWS)
    r, kr = r0 + i, ks + j
    rs = min(max(r - NA_KH // 2, 0), GRID_H - NA_KH)
    return kr - r + NA_KH - 1 if rs <= kr < rs + NA_KH else None


def _na_bias_kernel(rpb_ref, o_ref):
    lh = pl.program_id(0)
    shape = (GRID_W, 2 * GRID_W)
    qc = lax.broadcasted_iota(jnp.int32, shape, 0)
    lane = lax.broadcasted_iota(jnp.int32, shape, 1)
    kc = lane % GRID_W
    dc = jnp.clip(kc - qc, -(NA_KW - 1), NA_KW - 1) + NA_KW - 1
    cs = jnp.clip(qc - NA_KW // 2, 0, GRID_W - NA_KW)
    col_ok = (kc >= cs) & (kc < cs + NA_KW)
    neg = jnp.full(shape, NEG_INF, F32)
    tiles = []
    for dr in range(N_DR):
        base = (lh * N_DR + dr) * N_DC
        val = jnp.zeros(shape, F32)
        for d in range(N_DC):
            val = jnp.where(dc == d, rpb_ref[base + d], val)
        tiles.append(jnp.where(col_ok, val, neg))
    left = lane < GRID_W
    for kind, r0 in enumerate(NA_KINDS):
        for i in range(NA_ROWS):
            for jp in range(NA_KROWS // 2):
                dl, dr_ = _na_row_offset(r0, i, 2 * jp), _na_row_offset(r0, i, 2 * jp + 1)
                tl = neg if dl is None else tiles[dl]
                tr = neg if dr_ is None else tiles[dr_]
                o_ref[kind, i * GRID_W:(i + 1) * GRID_W, jp * 2 * GRID_W:(jp + 1) * 2 * GRID_W] = (
                    jnp.where(left, tl, tr))


def _na_bias_tables(rpb):
    return pl.pallas_call(
        _na_bias_kernel,
        grid=(DEPTH * NA_HEADS,),
        in_specs=[pl.BlockSpec(memory_space=pltpu.SMEM)],
        out_specs=pl.BlockSpec((None, len(NA_KINDS), NA_Q, NA_KEYS), lambda i: (i, 0, 0, 0)),
        out_shape=jax.ShapeDtypeStruct((DEPTH * NA_HEADS, len(NA_KINDS), NA_Q, NA_KEYS), F32),
        compiler_params=_cparams(("arbitrary",)),
        name="nbr_bias",
    )(rpb.reshape(-1))


NA_G = 4


def _na_kernel(q_ref, k_ref, v_ref, kc_ref, vc_ref, bias_ref, o_ref):
    rb = pl.program_id(2)
    ks = jnp.clip(rb * NA_ROWS - NA_KH // 2, 0, GRID_H - NA_KROWS)
    start = pl.multiple_of(ks * GRID_W, GRID_W)
    pair = 2 * NA_DIM
    left = lax.broadcasted_iota(jnp.int32, (NA_Q, pair), 1) < NA_DIM
    ones_loc = jnp.ones((NA_KEYS, pair), BF16)
    ones_ctx = jnp.ones((PAST_LEN, pair), BF16)
    def pair_ctx(ref, p):
        return jnp.concatenate([ref[2 * p].astype(BF16), ref[2 * p + 1].astype(BF16)], axis=-1)

    def values(p):
        lanes = slice(p * pair, (p + 1) * pair)
        return (jnp.concatenate([v_ref[pl.ds(start, NA_KEYS), lanes], ones_loc], axis=-1),
                jnp.concatenate([pair_ctx(vc_ref, p), ones_ctx], axis=-1))

    def scores(hh):
        p = hh // 2
        lanes = slice(p * pair, (p + 1) * pair)
        qp = q_ref[:, lanes] * NA_SCALE
        qm = jnp.where(left == (hh % 2 == 0), qp, jnp.zeros((NA_Q, pair), BF16))
        return (_dot_nt(qm, k_ref[pl.ds(start, NA_KEYS), lanes]) + bias_ref[hh],
                _dot_nt(qm, pair_ctx(kc_ref, p)))

    outs = []
    nxt = scores(0)
    for hh in range(NA_G):
        s_loc, s_ctx = nxt
        if hh + 1 < NA_G:
            nxt = scores(hh + 1)
        m = jnp.maximum(jnp.max(s_loc, axis=-1, keepdims=True), jnp.max(s_ctx, axis=-1, keepdims=True))
        p_loc = jnp.exp(s_loc - m).astype(BF16)
        p_ctx = jnp.exp(s_ctx - m).astype(BF16)
        if hh % 2 == 0:
            v_ext, vc_ext = values(hh // 2)
        o = (jnp.dot(p_loc, v_ext, preferred_element_type=F32)
             + jnp.dot(p_ctx, vc_ext, preferred_element_type=F32))
        outs.append(o[:, :pair] / o[:, pair:])
    o_ref[...] = jnp.concatenate([jnp.where(left, outs[2 * p], outs[2 * p + 1]) for p in range(NA_G // 2)],
                                 axis=-1).astype(BF16)


def _na_attn(z, cache_k, cache_v, bias, layer):
    lat_q0 = T_CTX // NA_Q
    lat_s0 = T_CTX // DEC_SEQ
    width = NA_G * NA_DIM
    qc, kc, vc = COL_NA_Q // width, COL_NA_K // width, COL_NA_V // width
    groups = NA_HEADS // NA_G

    def kind(rb):
        return jnp.where(rb == 0, 0, jnp.where(rb == NA_RB - 1, 2, 1))

    ctx_spec = pl.BlockSpec((None, None, NA_G, PAST_LEN, NA_DIM), lambda b, hg, rb: (b, layer, hg, 0, 0))
    return pl.pallas_call(
        _na_kernel,
        grid=(DEC_BATCH, groups, NA_RB),
        in_specs=[pl.BlockSpec((NA_Q, width), lambda b, hg, rb: (lat_q0 + b * NA_RB + rb, qc + hg)),
                  pl.BlockSpec((DEC_SEQ, width), lambda b, hg, rb: (lat_s0 + b, kc + hg)),
                  pl.BlockSpec((DEC_SEQ, width), lambda b, hg, rb: (lat_s0 + b, vc + hg)),
                  ctx_spec, ctx_spec,
                  pl.BlockSpec((NA_G, None, NA_Q, NA_KEYS),
                               lambda b, hg, rb: (layer * groups + hg, kind(rb), 0, 0))],
        out_specs=pl.BlockSpec((NA_Q, width), lambda b, hg, rb: (b * NA_RB + rb, hg)),
        out_shape=jax.ShapeDtypeStruct((T_LAT, NA_WIDTH), BF16),
        compiler_params=_cparams(("arbitrary", "arbitrary", "arbitrary")),
        name="nbr_attn",
    )(z, z, z, cache_k, cache_v, bias)


RET_PAIR = 2 * RET_DIM
RET_NPAIR = RET_HEADS // 2
assert RET_PAIR == LANES and RET_CHUNK == LANES
RET_UNROLL = 16
RET_CTX_GROUP = 8


def _rope_tables():
    n_freq = RET_DIM // 4
    t = np.arange(DEC_SEQ)
    inv = jnp.asarray(ROPE_BASE, F32) ** (-jnp.arange(n_freq, dtype=F32) / n_freq)
    ang_r = jnp.asarray(t // GRID_W, F32)[:, None] * inv[None, :]
    ang_c = jnp.asarray(t % GRID_W, F32)[:, None] * inv[None, :]
    cos = jnp.concatenate([jnp.cos(ang_r)] * 2 + [jnp.cos(ang_c)] * 2, axis=-1)
    sin = jnp.concatenate([-jnp.sin(ang_r), jnp.sin(ang_r), -jnp.sin(ang_c), jnp.sin(ang_c)], axis=-1)
    lane = np.arange(RET_WIDTH)
    src = np.where(lane % (2 * n_freq) < n_freq, lane + n_freq, lane - n_freq)
    swap = np.zeros((RET_WIDTH, RET_WIDTH), np.float32)
    swap[src, lane] = 1.0
    return jnp.tile(cos, (1, RET_HEADS)), jnp.tile(sin, (1, RET_HEADS)), jnp.asarray(swap, BF16)


def _ret_kernel(seq, group, latent, *refs):
    if latent:
        (lg_ref, z_ref, gn_ref, cos_ref, sin_ref, swap_ref, s0f_ref, s0b_ref, y_ref,
         q_s, k_s, kv_s, st_s) = refs
    else:
        lg_ref, z_ref, gn_ref, y_ref, sf_ref, sb_ref, q_s, k_s, kv_s, st_s = refs
    nc = seq // RET_CHUNK
    nct = group * nc
    ch, hd, pw = RET_CHUNK, RET_DIM, RET_PAIR

    row = lax.broadcasted_iota(jnp.int32, (ch, ch), 0).astype(F32)
    col = lax.broadcasted_iota(jnp.int32, (ch, ch), 1).astype(F32)
    pos = lax.broadcasted_iota(jnp.int32, (ch, pw), 0).astype(F32)
    left = lax.broadcasted_iota(jnp.int32, (ch, pw), 1) < hd
    top = lax.broadcasted_iota(jnp.int32, (pw, pw), 0) < hd
    same_head = top == (lax.broadcasted_iota(jnp.int32, (pw, pw), 1) < hd)
    same_head2 = jnp.concatenate([same_head, same_head], axis=0)

    def per_head(mask, fn, p):
        return jnp.where(mask, fn(2 * p), fn(2 * p + 1))

    decay = []
    for h in range(RET_HEADS):
        lf, lb = lg_ref[0, h], lg_ref[1, h]
        d_f = jnp.where(row >= col, jnp.exp(jnp.maximum(row - col, 0.0) * lf), 0.0)
        d_b = jnp.where(col >= row, jnp.exp(jnp.maximum(col - row, 0.0) * lb), 0.0)
        decay.append(d_f + d_b)
    q_dec, k_dec, c_dec_f, c_dec_b = [], [], [], []
    for p in range(RET_NPAIR):
        q_dec.append(jnp.concatenate(
            [per_head(left, lambda h: jnp.exp((pos + 1.0) * lg_ref[0, h]), p),
             per_head(left, lambda h: jnp.exp((ch - pos) * lg_ref[1, h]), p)], axis=-1))
        k_dec.append(jnp.concatenate(
            [per_head(left, lambda h: jnp.exp((ch - 1.0 - pos) * lg_ref[0, h]), p),
             per_head(left, lambda h: jnp.exp(pos * lg_ref[1, h]), p)], axis=-1))
        zero = jnp.zeros((pw, pw), F32)
        c_dec_f.append(per_head(top, lambda h: jnp.exp(zero + ch * lg_ref[0, h]), p))
        c_dec_b.append(per_head(top, lambda h: jnp.exp(zero + ch * lg_ref[1, h]), p))

    def rope(x, base):
        xf = x.astype(F32)
        if not latent:
            return xf
        swapped = jnp.dot(x, swap_ref[...], preferred_element_type=F32)
        return xf * cos_ref[pl.ds(base, ch), :] + swapped * sin_ref[pl.ds(base, ch), :]

    def pass1(n, carry):
        base = pl.multiple_of(n * ch, ch)
        zc = z_ref[pl.ds(base, ch), :]
        q = rope(zc[:, 0:RET_WIDTH], base)
        k = rope(zc[:, RET_WIDTH:2 * RET_WIDTH], base) * (RET_DIM ** -0.5)
        q_s[pl.ds(base, ch), :] = q.astype(BF16)
        k_s[pl.ds(base, ch), :] = k.astype(BF16)
        v = zc[:, 2 * RET_WIDTH:3 * RET_WIDTH]
        for p in range(RET_NPAIR):
            lanes = slice(p * pw, (p + 1) * pw)
            kp = k[:, lanes]
            k2 = (jnp.concatenate([kp, kp], axis=-1) * k_dec[p]).astype(BF16)
            kv = lax.dot_general(k2, v[:, lanes], (((0,), (0,)), ((), ())), preferred_element_type=F32)
            kv_s[n, p] = jnp.where(same_head2, kv, 0.0)
        return carry

    lax.fori_loop(0, nct, pass1, 0, unroll=min(RET_UNROLL, nct))

    def block_diag(a, b):
        z = jnp.zeros((hd, hd), F32)
        return jnp.concatenate([jnp.concatenate([a, z], axis=1), jnp.concatenate([z, b], axis=1)], axis=0)

    for sq in range(group):
        for p in range(RET_NPAIR):
            if latent:
                s_f = block_diag(s0f_ref[2 * p], s0f_ref[2 * p + 1])
                s_b = block_diag(s0b_ref[2 * p], s0b_ref[2 * p + 1])
            else:
                s_f = s_b = jnp.zeros((pw, pw), F32)

            def fwd(i, s, p=p, first=sq * nc):
                n = first + i
                st_s[n, p, 0:pw, :] = s.astype(BF16)
                return c_dec_f[p] * s + kv_s[n, p, 0:pw, :]

            def bwd(i, s, p=p, final=sq * nc + nc - 1):
                n = final - i
                st_s[n, p, pw:2 * pw, :] = s.astype(BF16)
                return c_dec_b[p] * s + kv_s[n, p, pw:2 * pw, :]

            s_f = lax.fori_loop(0, nc, fwd, s_f)
            s_b = lax.fori_loop(0, nc, bwd, s_b)
            if not latent:
                for hh in range(2):
                    blk = slice(hh * hd, (hh + 1) * hd)
                    sf_ref[sq, 2 * p + hh] = s_f[blk, blk]
                    sb_ref[sq, 2 * p + hh] = s_b[blk, blk]

    def pass3(n, carry):
        base = pl.multiple_of(n * ch, ch)
        zc = z_ref[pl.ds(base, ch), :]
        q = q_s[pl.ds(base, ch), :]
        k = k_s[pl.ds(base, ch), :]
        v = zc[:, 2 * RET_WIDTH:3 * RET_WIDTH]
        gate = zc[:, 3 * RET_WIDTH:4 * RET_WIDTH].astype(F32)
        outs = []
        for p in range(RET_NPAIR):
            lanes = slice(p * pw, (p + 1) * pw)
            qp, kp, vp = q[:, lanes], k[:, lanes], v[:, lanes]
            o_h = []
            for hh in range(2):
                qm = jnp.where(left == (hh == 0), qp, jnp.zeros_like(qp))
                s = _dot_nt(qm, kp) * decay[2 * p + hh]
                o_h.append(jnp.dot(s.astype(BF16), vp, preferred_element_type=F32))
            qf = qp.astype(F32)
            q2 = (jnp.concatenate([qf, qf], axis=-1) * q_dec[p]).astype(BF16)
            o = jnp.where(left, o_h[0], o_h[1]) + jnp.dot(q2, st_s[n, p], preferred_element_type=F32)

            def half_mean(t):
                s_l = jnp.sum(jnp.where(left, t, 0.0), axis=-1, keepdims=True)
                s_r = jnp.sum(jnp.where(left, 0.0, t), axis=-1, keepdims=True)
                return jnp.where(left, s_l, s_r) * (1.0 / hd)

            d = o - half_mean(o)
            outs.append(d * lax.rsqrt(half_mean(d * d) + EPS))
        nrm = jnp.concatenate(outs, axis=-1)
        y_ref[pl.ds(base, ch), :] = (nrm * gn_ref[...] * (gate * _sigmoid(gate))).astype(BF16)
        return carry

    lax.fori_loop(0, nct, pass3, 0, unroll=min(RET_UNROLL, nct))


def _retention(z, lg, gn_g, latent, layer=None, rope=None, s0_f=None, s0_b=None):
    seq = DEC_SEQ if latent else SEQ
    nseq = DEC_BATCH if latent else BATCH
    group = 1 if latent else RET_CTX_GROUP
    rows = group * seq
    nct = rows // RET_CHUNK
    row0 = (T_CTX // DEC_SEQ) if latent else 0
    cb = COL_RET // (4 * RET_WIDTH)
    in_specs = [pl.BlockSpec(memory_space=pltpu.SMEM),
                pl.BlockSpec((rows, 4 * RET_WIDTH), lambda s: (row0 + s, cb)),
                _full_spec((1, RET_WIDTH))]
    args = [lg, z, gn_g]
    state_shape = jax.ShapeDtypeStruct((nseq, RET_HEADS, RET_DIM, RET_DIM), F32)
    y_spec = pl.BlockSpec((rows, RET_WIDTH), lambda s: (s, 0))
    y_shape = jax.ShapeDtypeStruct((nseq * seq, RET_WIDTH), BF16)
    if latent:
        st_spec = pl.BlockSpec((None, None, RET_HEADS, RET_DIM, RET_DIM), lambda s: (s, layer, 0, 0, 0))

        def const_spec(shape):
            return pl.BlockSpec(shape, lambda s: (0,) * len(shape), pipeline_mode=pl.Buffered(1))

        in_specs += [const_spec((seq, RET_WIDTH)), const_spec((seq, RET_WIDTH)),
                     const_spec((RET_WIDTH, RET_WIDTH)), st_spec, st_spec]
        args += [rope[0], rope[1], rope[2], s0_f, s0_b]
        out_specs, out_shape = y_spec, y_shape
    else:
        so_spec = pl.BlockSpec((group, RET_HEADS, RET_DIM, RET_DIM), lambda s: (s, 0, 0, 0))
        out_specs, out_shape = [y_spec, so_spec, so_spec], [y_shape, state_shape, state_shape]
    return pl.pallas_call(
        functools.partial(_ret_kernel, seq, group, latent),
        grid=(nseq // group,),
        in_specs=in_specs,
        out_specs=out_specs,
        out_shape=out_shape,
        scratch_shapes=[pltpu.VMEM((rows, RET_WIDTH), BF16), pltpu.VMEM((rows, RET_WIDTH), BF16),
                        pltpu.VMEM((nct, RET_NPAIR, 2 * RET_PAIR, RET_PAIR), F32),
                        pltpu.VMEM((nct, RET_NPAIR, 2 * RET_PAIR, RET_PAIR), BF16)],
        compiler_params=_cparams(("arbitrary",)),
        name="retention_lat" if latent else "retention_ctx",
    )(*args)


def _route(logits):
    lane = lax.broadcasted_iota(jnp.int32, logits.shape, 1)
    lane_f = lane.astype(F32)
    big = float(ROUTE_COLS)
    neg = -jnp.inf
    is_grp = lane < N_GROUPS
    gl = jnp.where(is_grp, logits, neg)
    gmax = jnp.max(gl, axis=-1, keepdims=True)
    grp = jnp.min(jnp.where(gl == gmax, lane_f, big), axis=-1, keepdims=True)
    p_grp = 1.0 / jnp.sum(jnp.exp(gl - gmax), axis=-1, keepdims=True)
    e_f = lane_f - N_GROUPS
    lo = grp * EXPERTS_PER_GROUP
    in_grp = (e_f >= lo) & (e_f < lo + EXPERTS_PER_GROUP)
    el = jnp.where(in_grp, logits, neg)
    m1 = jnp.max(el, axis=-1, keepdims=True)
    i1 = jnp.min(jnp.where(el == m1, lane_f, big), axis=-1, keepdims=True)
    el2 = jnp.where(lane_f == i1, neg, el)
    m2 = jnp.max(el2, axis=-1, keepdims=True)
    i2 = jnp.min(jnp.where(el2 == m2, lane_f, big), axis=-1, keepdims=True)
    t = jnp.exp(m2 - m1)
    g1 = p_grp / (1.0 + t)
    g2 = p_grp * t / (1.0 + t)
    rows = logits.shape[0]
    oh1, oh2 = lane_f == i1, lane_f == i2
    oh = jnp.where(oh1 | oh2, 1.0, 0.0)
    tri = (lax.broadcasted_iota(jnp.int32, (rows, rows), 0)
           > lax.broadcasted_iota(jnp.int32, (rows, rows), 1))
    rank = jnp.dot(jnp.where(tri, 1.0, 0.0).astype(BF16), oh.astype(BF16), preferred_element_type=F32)
    tiles = jnp.floor((jnp.sum(oh, axis=0, keepdims=True) + (SUBLANES - 1)) * (1.0 / SUBLANES))
    upper = (lax.broadcasted_iota(jnp.int32, (ROUTE_COLS, ROUTE_COLS), 0)
             < lax.broadcasted_iota(jnp.int32, (ROUTE_COLS, ROUTE_COLS), 1))
    start = SUBLANES * jnp.dot(jnp.broadcast_to(tiles, (SUBLANES, ROUTE_COLS)).astype(BF16),
                               jnp.where(upper, 1.0, 0.0).astype(BF16),
                               preferred_element_type=F32)[0:1, :]
    pos = start + rank
    p1 = jnp.sum(jnp.where(oh1, pos, 0.0), axis=-1, keepdims=True)
    p2 = jnp.sum(jnp.where(oh2, pos, 0.0), axis=-1, keepdims=True)
    out = jnp.zeros(logits.shape, F32)
    for k, val in enumerate((i1 - N_GROUPS, i2 - N_GROUPS, g1, g2, p1, p2)):
        out = jnp.where(lane == k, val, out)
    return out, SUBLANES * tiles


def _outproj_kernel(ycc, ycl, ync, ynl, yrc, yrl, x_ref, mod_ref, g_ref, w_ref, wr_ref, br_ref,
                    xo_ref, xs_ref, r_ref, seg_ref, wb_ref):
    _cast_weights_once(w_ref, wb_ref)
    is_ctx = pl.program_id(0) < NB_CTX
    yc = jnp.where(is_ctx, ycc[...], ycl[...])
    yn = jnp.where(is_ctx, ync[...], ynl[...])
    yr = jnp.where(is_ctx, yrc[...], yrl[...])
    y = (jnp.dot(yc, wb_ref[0:CONV_CH, :], preferred_element_type=F32)
         + jnp.dot(yn, wb_ref[CONV_CH:CONV_CH + NA_WIDTH, :], preferred_element_type=F32)
         + jnp.dot(yr, wb_ref[CONV_CH + NA_WIDTH:, :], preferred_element_type=F32))
    x = x_ref[...] + mod_ref[2:3, :] * y
    xo_ref[...] = x
    h = _norm_mod(x, g_ref[...], mod_ref[3:4, :], mod_ref[4:5, :])
    h_hi = h.astype(BF16)
    h_lo = (h - h_hi.astype(F32)).astype(BF16)
    hw = jnp.dot(h_hi, wr_ref[...], preferred_element_type=F32)
    logits = (hw[:, :ROUTE_COLS] + hw[:, ROUTE_COLS:]
              + jnp.dot(h_lo, wr_ref[:, :ROUTE_COLS], preferred_element_type=F32) + br_ref[...])
    route, seg = _route(logits)
    r_ref[...] = route
    seg_ref[...] = jnp.broadcast_to(seg, seg_ref.shape)
    sel = _slot_onehot(route, 0) | _slot_onehot(route, 1)
    xs_ref[...] = _pack_bf16_pairs(lax.dot_general(jnp.where(sel, 1.0, 0.0).astype(BF16), h_hi,
                                                   (((0,), (0,)), ((), ())), preferred_element_type=F32))


def _outproj(y_conv, y_na, y_ret, x, mod, g, w_out, layer, w_route, b_route):
    return pl.pallas_call(
        _outproj_kernel,
        grid=(NB_ALL,),
        in_specs=(_ctx_lat_specs(CONV_CH) + _ctx_lat_specs(NA_WIDTH) + _ctx_lat_specs(RET_WIDTH)
                  + [_tok_spec(D_MODEL), _mod_spec(), _full_spec((1, D_MODEL)),
                     _layer_weight_spec(layer, D_MODEL, D_MODEL), _full_spec((D_MODEL, 2 * ROUTE_COLS)),
                     _full_spec((1, ROUTE_COLS))]),
        out_specs=[_tok_spec(D_MODEL), pl.BlockSpec((MOE_LC, D_HALF), lambda i: (i, 0)),
                   _tok_spec(ROUTE_COLS), pl.BlockSpec((None, SUBLANES, ROUTE_COLS), lambda i: (i, 0, 0))],
        out_shape=[jax.ShapeDtypeStruct((T_ALL, D_MODEL), F32),
                   jax.ShapeDtypeStruct((NB_ALL * MOE_LC, D_HALF), U32),
                   jax.ShapeDtypeStruct((T_ALL, ROUTE_COLS), F32),
                   jax.ShapeDtypeStruct((NB_ALL, SUBLANES, ROUTE_COLS), F32)],
        scratch_shapes=[pltpu.VMEM((D_MODEL, D_MODEL), BF16)],
        compiler_params=_cparams(("arbitrary",)),
        name="outproj_route",
    )(y_conv[0], y_conv[1], y_na[0], y_na[1], y_ret[0], y_ret[1], x, mod, g, w_out, w_route, b_route)


def _dispatch_tables(seg):
    seg_len = seg[:, 0, N_GROUPS:N_GROUPS + N_EXPERTS].astype(jnp.int32)
    experts = jnp.arange(N_EXPERTS, dtype=jnp.int32)
    in_chunk = jnp.cumsum(seg_len, axis=1) - seg_len
    seg_row = in_chunk + MOE_LC * jnp.arange(N_CHUNK, dtype=jnp.int32)[:, None]
    seg_off = jnp.cumsum(seg_len, axis=0) - seg_len
    rows_e = jnp.sum(seg_len, axis=0)
    chunk_rows = jnp.sum(seg_len, axis=1)
    nblk = (rows_e + MOE_BLK - 1) // MOE_BLK
    blk_end = jnp.cumsum(nblk)
    blk_start = blk_end - nblk
    blk = jnp.arange(MOE_NBLK, dtype=jnp.int32)
    n_active = blk_end[-1]
    blk_e = jnp.minimum(jnp.sum((blk_end[None, :] <= jnp.minimum(blk, n_active - 1)[:, None]).astype(jnp.int32),
                                axis=-1), N_EXPERTS - 1)
    mine = blk_e[:, None] == experts[None, :]
    blk_lo = (blk - jnp.sum(jnp.where(mine, blk_start[None, :], 0), axis=-1)) * MOE_BLK
    left = jnp.sum(jnp.where(mine, rows_e[None, :], 0), axis=-1) - blk_lo
    blk_nv = jnp.where(blk < n_active, jnp.clip(left, 0, MOE_BLK), 0).astype(jnp.int32)
    off_b = jnp.sum(jnp.where(mine[:, None, :], seg_off[None, :, :], 0), axis=-1)
    end_b = off_b + jnp.sum(jnp.where(mine[:, None, :], seg_len[None, :, :], 0), axis=-1)
    blk_c0 = jnp.sum((end_b <= blk_lo[:, None]).astype(jnp.int32), axis=-1)
    blk_c1 = jnp.sum((off_b < (blk_lo + blk_nv)[:, None]).astype(jnp.int32), axis=-1)
    after = jnp.sum(jnp.where(mine, blk_end[None, :], 0), axis=-1)
    blk_next_e = jnp.where(after < n_active, jnp.take(blk_e, jnp.minimum(after, MOE_NBLK - 1)), -1)
    row_b = jnp.sum(jnp.where(mine[:, None, :], seg_row[None, :, :], 0), axis=-1)
    first = jnp.maximum(off_b, blk_lo[:, None])
    piece_n = jnp.minimum(end_b, (blk_lo + blk_nv)[:, None]) - first
    piece_src = row_b + first - off_b
    piece_dst = first - blk_lo[:, None]
    piece = jnp.where(piece_n > 0,
                      (piece_src // SUBLANES) | ((piece_dst // SUBLANES) << PIECE_DST_SHIFT)
                      | ((piece_n // SUBLANES) << PIECE_N_SHIFT), 0).astype(jnp.int32)
    return (blk_e, blk_next_e.astype(jnp.int32), blk_nv, blk_c0, blk_c1, piece.reshape(-1), chunk_rows)


PIECE_DST_SHIFT = (N_CHUNK * MOE_LC // SUBLANES - 1).bit_length()
PIECE_N_SHIFT = PIECE_DST_SHIFT + (MOE_BLK // SUBLANES - 1).bit_length()
assert PIECE_N_SHIFT + (TM // SUBLANES).bit_length() <= 31


def _moe_kernel(layer, blk_e, blk_next_e, blk_nv, blk_c0, blk_c1, piece,
                chunk_rows, xs_hbm, w1_hbm, w3_hbm, w2_hbm, ys_hbm, xbuf, obuf, zeros,
                w1f, w3f, w2f, w1b, w3b, w2b, gsem, ssem, zsem, wsem):
    i = pl.program_id(0)
    last = pl.num_programs(0) - 1
    slot = i % 2

    def tiles(v):
        return pl.multiple_of(v, SUBLANES)

    def for_segments(blk, fn):
        def body(c, carry):
            word = piece[blk * N_CHUNK + c]
            src = (word & ((1 << PIECE_DST_SHIFT) - 1)) * SUBLANES
            dst = ((word >> PIECE_DST_SHIFT) & ((1 << (PIECE_N_SHIFT - PIECE_DST_SHIFT)) - 1)) * SUBLANES
            n = (word >> PIECE_N_SHIFT) * SUBLANES

            @pl.when(word != 0)
            def _():
                fn(tiles(src), tiles(dst), tiles(n))

            return carry

        lax.fori_loop(blk_c0[blk], blk_c1[blk], body, 0)

    def weight_copies(e):
        return [pltpu.make_async_copy(src.at[layer, e], dst, wsem)
                for src, dst in ((w1_hbm, w1f), (w3_hbm, w3f), (w2_hbm, w2f))]

    def start_gathers(blk, s):
        for_segments(blk, lambda src, dst, n: pltpu.make_async_copy(
            xs_hbm.at[pl.ds(src, n)], xbuf.at[s, pl.ds(dst, n)], gsem.at[s]).start())

    def start_scatters(blk, s):
        for_segments(blk, lambda dst, src, n: pltpu.make_async_copy(
            obuf.at[s, pl.ds(src, n)], ys_hbm.at[pl.ds(dst, n)], ssem.at[s]).start())

    def wait_rows(blk, s, sem):
        n = tiles(blk_nv[blk])

        @pl.when(n > 0)
        def _():
            pltpu.make_async_copy(xs_hbm.at[pl.ds(0, n)], xbuf.at[s, pl.ds(0, n)], sem.at[s]).wait()

    @pl.when(i == 0)
    def _():
        xbuf[...] = jnp.zeros_like(xbuf)
        zeros[...] = jnp.zeros_like(zeros)

        def tail(c):
            n = tiles(MOE_LC - chunk_rows[c])
            return n, pltpu.make_async_copy(zeros.at[pl.ds(0, n)],
                                            ys_hbm.at[pl.ds(tiles(c * MOE_LC + chunk_rows[c]), n)], zsem)

        def fill(c, carry):
            n, copy = tail(c)
            pl.when(n > 0)(copy.start)
            return carry

        def drain(c, carry):
            n, copy = tail(c)
            pl.when(n > 0)(copy.wait)
            return carry

        lax.fori_loop(0, N_CHUNK, fill, 0)
        lax.fori_loop(0, N_CHUNK, drain, 0)
        start_gathers(0, 0)
        for copy in weight_copies(blk_e[0]):
            copy.start()

    @pl.when(i < last)
    def _():
        start_gathers(i + 1, 1 - slot)

    @pl.when(i >= 2)
    def _():
        wait_rows(i - 2, slot, ssem)

    @pl.when(blk_nv[i] > 0)
    def _():
        @pl.when((i == 0) | (blk_e[i] != blk_e[jnp.maximum(i - 1, 0)]))
        def _():
            for copy in weight_copies(blk_e[i]):
                copy.wait()
            w1b[...] = w1f[...].astype(BF16)
            w3b[...] = w3f[...].astype(BF16)
            w2b[...] = w2f[...].astype(BF16)

            @pl.when(blk_next_e[i] >= 0)
            def _():
                for copy in weight_copies(blk_next_e[i]):
                    copy.start()

        wait_rows(i, slot, gsem)

        def expert_mlp(rows):
            x_lo, x_hi = _unpack_bf16_pairs(xbuf[slot, 0:rows, :])
            n_hid = D_EXPERT // MXU_TILE

            def in_dot(w, t):
                cols = slice(t * MXU_TILE, (t + 1) * MXU_TILE)
                return (jnp.dot(x_lo, w[:D_HALF, cols], preferred_element_type=F32)
                        + jnp.dot(x_hi, w[D_HALF:, cols], preferred_element_type=F32))

            ab = [(in_dot(w1b, t), in_dot(w3b, t)) for t in range(n_hid)]
            mid = [(a * _sigmoid(a) * b).astype(BF16) for a, b in ab]

            def out_dot(t):
                cols = slice(t * MXU_TILE, (t + 1) * MXU_TILE)
                return sum(jnp.dot(mid[j], w2b[j * MXU_TILE:(j + 1) * MXU_TILE, cols],
                                   preferred_element_type=F32) for j in range(n_hid))

            n_word = D_HALF // MXU_TILE
            for t in range(n_word):
                obuf[slot, 0:rows, t * MXU_TILE:(t + 1) * MXU_TILE] = _pack_words(
                    out_dot(t), out_dot(t + n_word))

        for rows in range(MOE_ROW_STEP, MOE_BLK + 1, MOE_ROW_STEP):
            @pl.when((blk_nv[i] > rows - MOE_ROW_STEP) & (blk_nv[i] <= rows))
            def _(rows=rows):
                expert_mlp(rows)

        start_scatters(i, slot)

    @pl.when(i == last)
    def _():
        wait_rows(i - 1, 1 - slot, ssem)
        wait_rows(i, slot, ssem)


def _moe(xs, w1, w3, w2, layer, blk_e, blk_next_e, blk_nv, blk_c0, blk_c1, piece, chunk_rows):
    any_spec = pl.BlockSpec(memory_space=pl.ANY)
    grid_spec = pltpu.PrefetchScalarGridSpec(
        num_scalar_prefetch=7,
        grid=(MOE_NBLK,),
        in_specs=[any_spec, any_spec, any_spec, any_spec],
        out_specs=any_spec,
        scratch_shapes=[pltpu.VMEM((2, MOE_BLK, D_HALF), U32), pltpu.VMEM((2, MOE_BLK, D_HALF), U32),
                        pltpu.VMEM((MOE_LC - 2 * TM, D_HALF), U32),
                        pltpu.VMEM((D_MODEL, D_EXPERT), F32), pltpu.VMEM((D_MODEL, D_EXPERT), F32),
                        pltpu.VMEM((D_EXPERT, D_MODEL), F32),
                        pltpu.VMEM((D_MODEL, D_EXPERT), BF16), pltpu.VMEM((D_MODEL, D_EXPERT), BF16),
                        pltpu.VMEM((D_EXPERT, D_MODEL), BF16),
                        pltpu.SemaphoreType.DMA((2,)), pltpu.SemaphoreType.DMA((2,)),
                        pltpu.SemaphoreType.DMA, pltpu.SemaphoreType.DMA])
    return pl.pallas_call(
        functools.partial(_moe_kernel, layer),
        grid_spec=grid_spec,
        out_shape=jax.ShapeDtypeStruct((NB_ALL * MOE_LC, D_HALF), U32),
        compiler_params=_cparams(("arbitrary",)),
        name="moe_experts",
    )(blk_e, blk_next_e, blk_nv, blk_c0, blk_c1, piece, chunk_rows, xs, w1, w3, w2)


def _final_kernel(x_ref, ys_ref, r_ref, mod_ref, g_ref, o_ref):
    x = _moe_residual(x_ref, ys_ref, r_ref, mod_ref)
    ms = jnp.mean(x * x, axis=-1, keepdims=True)
    o_ref[...] = x * lax.rsqrt(ms + EPS) * g_ref[...]


def _final(x, ys, route, mod, g, block0, nblocks):
    return pl.pallas_call(
        _final_kernel,
        grid=(nblocks,),
        in_specs=[pl.BlockSpec((TM, D_MODEL), lambda i: (block0 + i, 0)),
                  pl.BlockSpec((MOE_LC, D_HALF), lambda i: (block0 + i, 0)),
                  pl.BlockSpec((TM, ROUTE_COLS), lambda i: (block0 + i, 0)),
                  pl.BlockSpec((None, 6, D_MODEL), lambda i: (_cond_row(block0 + i), 0, 0)),
                  _full_spec((1, D_MODEL))],
        out_specs=_tok_spec(D_MODEL),
        out_shape=jax.ShapeDtypeStruct((nblocks * TM, D_MODEL), F32),
        compiler_params=_cparams(("arbitrary",)),
        name="final_norm",
    )(x, ys, route, mod, g)


def kernel(x_prompt, x_sample, c, cache_k, cache_v, state_ret_f, state_ret_b, c_ctx, w_ada, b_ada, norm1_g, norm2_g, w_in, w_out, conv_w, conv_b, conv_ln_g, conv_ln_b, na_rpb, ret_lg_f, ret_lg_b, ret_gn_g, w_route_g, b_route_g, w_route_e, b_route_e, w1, w3, w2, final_g):
    cv = jnp.zeros((COND_ROWS, D_MODEL), F32).at[0].set(c_ctx).at[1:N_COND].set(c)
    mods = _ada(cv, w_ada, b_ada).reshape(DEPTH, COND_ROWS, 6, D_MODEL)
    pad = ROUTE_COLS - N_GROUPS - N_EXPERTS
    w_route = jnp.pad(jnp.concatenate([w_route_g, w_route_e], axis=-1), ((0, 0), (0, 0), (0, pad)))
    b_route = jnp.pad(jnp.concatenate([b_route_g, b_route_e], axis=-1), ((0, 0), (0, pad)))
    w_route_hi = w_route.astype(BF16)
    w_route_lo = (w_route - w_route_hi.astype(F32)).astype(BF16)
    w_route = jnp.concatenate([w_route_hi, w_route_lo], axis=-1)
    na_bias = _na_bias_tables(na_rpb)
    rope = _rope_tables()
    lg = jnp.stack([ret_lg_f, ret_lg_b], axis=1)

    x_ctx = x_prompt.reshape(T_CTX, D_MODEL)
    x_lat = x_sample.reshape(T_LAT, D_MODEL)
    x = y = route = new_k = new_v = None
    sf_list, sb_list = [], []
    for l in range(DEPTH):
        g1 = norm1_g[l].reshape(1, D_MODEL)
        if l == 0:
            z, x = _inproj_first(x_ctx, x_lat, mods[l], g1, w_in, l)
        else:
            z, x = _inproj_next(x, y, route, mods[l - 1], mods[l], g1, w_in, l)
        conv_args = (conv_w[l], conv_b[l].reshape(1, -1), conv_ln_g[l].reshape(1, -1),
                     conv_ln_b[l].reshape(1, -1))
        yc_c = _conv(z, 0, BATCH, SEQ, *conv_args)
        yc_l = _conv(z, T_CTX // DEC_SEQ, DEC_BATCH, DEC_SEQ, *conv_args)
        yn_c, new_k, new_v = _ctx_attn(z, l, new_k, new_v)
        yn_l = _na_attn(z, cache_k, cache_v, na_bias, l)
        gn = ret_gn_g[l].reshape(1, RET_WIDTH)
        yr_c, sf_l, sb_l = _retention(z, lg[l], gn, latent=False)
        yr_l = _retention(z, lg[l], gn, latent=True, layer=l, rope=rope,
                          s0_f=state_ret_f, s0_b=state_ret_b)
        x, xs, route, seg = _outproj((yc_c, yc_l), (yn_c, yn_l), (yr_c, yr_l), x, mods[l],
                                     norm2_g[l].reshape(1, D_MODEL), w_out, l, w_route[l],
                                     b_route[l].reshape(1, ROUTE_COLS))
        y = _moe(xs, w1, w3, w2, l, *_dispatch_tables(seg))
        sf_list.append(sf_l)
        sb_list.append(sb_l)
    fg = final_g.reshape(1, D_MODEL)
    y_prompt = _final(x, y, route, mods[DEPTH - 1], fg, 0, NB_CTX).reshape(BATCH, SEQ, D_MODEL)
    y_sample = _final(x, y, route, mods[DEPTH - 1], fg, NB_CTX, NB_LAT).reshape(DEC_BATCH, DEC_SEQ, D_MODEL)
    return (y_prompt, y_sample, new_k, new_v, jnp.stack(sf_list, axis=1), jnp.stack(sb_list, axis=1))
```
